```python
import jax, jax.numpy as jnp
from jax import lax
import numpy as np

D_MODEL = 1024
BATCH = 8
SEQ = 16384
DEPTH = 2

HEAD_DIM = 64
A_CH = D_MODEL // 2
A_CONV = 31
N_Q_HEADS = (D_MODEL // 2) // HEAD_DIM
N_KV_HEADS = 2
GROUP = N_Q_HEADS // N_KV_HEADS
WINDOW = 128
BLOCK = 128
ROPE_THETA = 500000.0
ROPE_DIM = HEAD_DIM // 4
Q_DIM = N_Q_HEADS * HEAD_DIM
KV_DIM = N_KV_HEADS * HEAD_DIM
EVEN_IN = 2 * A_CH + Q_DIM + 2 * KV_DIM
MIX_DIM = A_CH + Q_DIM
SC_DIM = D_MODEL
SC_CONV = 3
D_FF = 2816
FFN_CONV = 3
N_EVEN = (DEPTH + 1) // 2
N_ODD = DEPTH // 2
RMS_EPS = 1e-6
LN_EPS = 1e-5

kernel_name = "hybrid_conformer_swa_shortconv_trunk"


def rms_norm(x, g):
    xf = x.astype(jnp.float32)
    y = xf * lax.rsqrt(jnp.mean(xf * xf, axis=-1, keepdims=True) + RMS_EPS)
    return (y * g.astype(jnp.float32)).astype(x.dtype)


def layer_norm(x, g, b):
    xf = x.astype(jnp.float32)
    mu = jnp.mean(xf, axis=-1, keepdims=True)
    xc = xf - mu
    y = xc * lax.rsqrt(jnp.mean(xc * xc, axis=-1, keepdims=True) + LN_EPS)
    return (y * g.astype(jnp.float32) + b.astype(jnp.float32)).astype(x.dtype)


def causal_dwconv(x, w):
    k, c = w.shape
    return lax.conv_general_dilated(
        x, w[:, None, :].astype(x.dtype), window_strides=(1,), padding=[(k - 1, 0)],
        dimension_numbers=('NWC', 'WIO', 'NWC'), feature_group_count=c)


def partial_rope(x, positions):
    half = ROPE_DIM // 2
    inv_freq = ROPE_THETA ** (-(jnp.arange(half, dtype=jnp.float32) * 2.0 / ROPE_DIM))
    ang = positions.astype(jnp.float32)[..., None] * inv_freq
    cos = jnp.cos(ang)[:, :, None, :]
    sin = jnp.sin(ang)[:, :, None, :]
    xf = x.astype(jnp.float32)
    x1, x2, rest = xf[..., :half], xf[..., half:ROPE_DIM], xf[..., ROPE_DIM:]
    out = jnp.concatenate([x1 * cos - x2 * sin, x2 * cos + x1 * sin, rest], axis=-1)
    return out.astype(x.dtype)


def sliding_window_attention(q, k, v, sinks):
    bsz, s_len = q.shape[0], q.shape[1]
    nb = s_len // BLOCK
    qb = q.reshape(bsz, nb, BLOCK, N_KV_HEADS, GROUP, HEAD_DIM)
    pad = ((0, 0), (BLOCK, 0), (0, 0), (0, 0))
    kb = jnp.pad(k, pad).reshape(bsz, nb + 1, BLOCK, N_KV_HEADS, HEAD_DIM)
    vb = jnp.pad(v, pad).reshape(bsz, nb + 1, BLOCK, N_KV_HEADS, HEAD_DIM)
    kw = jnp.concatenate([kb[:, :-1], kb[:, 1:]], axis=2)
    vw = jnp.concatenate([vb[:, :-1], vb[:, 1:]], axis=2)
    s = jnp.einsum('bnqhgd,bnkhd->bnhgqk', qb, kw,
                   preferred_element_type=jnp.float32) * (HEAD_DIM ** -0.5)
    q_idx = jnp.arange(BLOCK)[:, None]
    k_idx = jnp.arange(2 * BLOCK)[None, :]
    diff = q_idx + BLOCK - k_idx
    band = (diff >= 0) & (diff < WINDOW)
    key_valid = (jnp.arange(nb)[:, None] * BLOCK - BLOCK + k_idx) >= 0
    mask = band[None, :, :] & key_valid[:, None, :]
    s = jnp.where(mask[None, :, None, None], s, -jnp.inf)
    sink = sinks.astype(jnp.float32).reshape(N_KV_HEADS, GROUP)[None, None, :, :, None, None]
    m = jnp.maximum(jnp.max(s, axis=-1, keepdims=True), sink)
    p = jnp.exp(s - m)
    denom = jnp.sum(p, axis=-1, keepdims=True) + jnp.exp(sink - m)
    o = jnp.einsum('bnhgqk,bnkhd->bnqhgd', (p / denom).astype(v.dtype), vw)
    return o.reshape(bsz, s_len, N_Q_HEADS * HEAD_DIM)


def conv_attn_mixer(h, positions, w_in, a_conv_w, a_conv_b, a_ln_g, a_ln_b, sinks, w_out):
    bsz, s_len = h.shape[0], h.shape[1]
    z = h @ w_in
    a_lin, a_gate, q, k, v = jnp.split(
        z, [A_CH, 2 * A_CH, 2 * A_CH + Q_DIM, 2 * A_CH + Q_DIM + KV_DIM], axis=-1)
    a = a_lin * jax.nn.sigmoid(a_gate)
    a = causal_dwconv(a, a_conv_w) + a_conv_b
    a = jax.nn.silu(layer_norm(a, a_ln_g, a_ln_b))
    q = partial_rope(q.reshape(bsz, s_len, N_Q_HEADS, HEAD_DIM), positions)
    k = partial_rope(k.reshape(bsz, s_len, N_KV_HEADS, HEAD_DIM), positions)
    v = v.reshape(bsz, s_len, N_KV_HEADS, HEAD_DIM)
    o = sliding_window_attention(q, k, v, sinks)
    return jnp.concatenate([a, o], axis=-1) @ w_out


def short_conv_mixer(h, w_in, conv_w, w_out):
    z = h @ w_in
    b_gate, c_gate, u = jnp.split(z, 3, axis=-1)
    y = b_gate * causal_dwconv(c_gate * u, conv_w)
    return y @ w_out


def conv_glu_ffn(h, w_up, conv_w, w_down):
    u = causal_dwconv(h @ w_up, conv_w)
    g, val = jnp.split(u, 2, axis=-1)
    return (jax.nn.silu(g) * val) @ w_down


def _fwd_setup_inputs(seed: int = 0) -> dict:
    key = jax.random.key(seed)
    ks = jax.random.split(key, 20)

    def nrm(k, shape, scale):
        return jax.random.normal(k, shape, jnp.float32) * scale

    def gain(k, shape):
        return 1.0 + 0.05 * jax.random.normal(k, shape, jnp.float32)

    x = nrm(ks[0], (BATCH, SEQ, D_MODEL), 1.0)
    offsets = jax.random.randint(ks[1], (BATCH, 1), 0, 4096, dtype=jnp.int32)
    positions = offsets + jnp.arange(SEQ, dtype=jnp.int32)[None, :]
    return {
        'x': x,
        'positions': positions,
        'mix_norm_pre': gain(ks[2], (DEPTH, D_MODEL)),
        'mix_norm_post': gain(ks[3], (DEPTH, D_MODEL)),
        'ffn_norm_pre': gain(ks[4], (DEPTH, D_MODEL)),
        'ffn_norm_post': gain(ks[5], (DEPTH, D_MODEL)),
        'ev_w_in': nrm(ks[6], (N_EVEN, D_MODEL, EVEN_IN), D_MODEL ** -0.5),
        'ev_a_conv_w': nrm(ks[7], (N_EVEN, A_CONV, A_CH), A_CONV ** -0.5),
        'ev_a_conv_b': nrm(ks[8], (N_EVEN, A_CH), 0.02),
        'ev_a_ln_g': gain(ks[9], (N_EVEN, A_CH)),
        'ev_a_ln_b': nrm(ks[10], (N_EVEN, A_CH), 0.02),
        'ev_sinks': nrm(ks[11], (N_EVEN, N_Q_HEADS), 1.0),
        'ev_w_out': nrm(ks[12], (N_EVEN, MIX_DIM, D_MODEL), MIX_DIM ** -0.5),
        'od_w_in': nrm(ks[13], (N_ODD, D_MODEL, 3 * SC_DIM), D_MODEL ** -0.5),
        'od_conv_w': nrm(ks[14], (N_ODD, SC_CONV, SC_DIM), SC_CONV ** -0.5),
        'od_w_out': nrm(ks[15], (N_ODD, SC_DIM, D_MODEL), SC_DIM ** -0.5),
        'ffn_w_up': nrm(ks[16], (DEPTH, D_MODEL, 2 * D_FF), D_MODEL ** -0.5),
        'ffn_conv_w': nrm(ks[17], (DEPTH, FFN_CONV, 2 * D_FF), FFN_CONV ** -0.5),
        'ffn_w_down': nrm(ks[18], (DEPTH, D_FF, D_MODEL), D_FF ** -0.5),
    }


def _fwd_reference(x, positions, mix_norm_pre, mix_norm_post, ffn_norm_pre, ffn_norm_post,
              ev_w_in, ev_a_conv_w, ev_a_conv_b, ev_a_ln_g, ev_a_ln_b, ev_sinks, ev_w_out,
              od_w_in, od_conv_w, od_w_out, ffn_w_up, ffn_conv_w, ffn_w_down):
    for i in range(DEPTH):
        j = i // 2
        h = rms_norm(x, mix_norm_pre[i])
        if i % 2 == 0:
            m = conv_attn_mixer(h, positions, ev_w_in[j], ev_a_conv_w[j], ev_a_conv_b[j],
                                ev_a_ln_g[j], ev_a_ln_b[j], ev_sinks[j], ev_w_out[j])
        else:
            m = short_conv_mixer(h, od_w_in[j], od_conv_w[j], od_w_out[j])
        x = x + rms_norm(m, mix_norm_post[i])
        h = rms_norm(x, ffn_norm_pre[i])
        f = conv_glu_ffn(h, ffn_w_up[i], ffn_conv_w[i], ffn_w_down[i])
        x = x + rms_norm(f, ffn_norm_post[i])
    return x


import jax as _jax
import jax.numpy as _jnp

TWIN_FORMAT = 'train_step'
FWD_PARAMS = ['x', 'positions', 'mix_norm_pre', 'mix_norm_post', 'ffn_norm_pre', 'ffn_norm_post', 'ev_w_in', 'ev_a_conv_w', 'ev_a_conv_b', 'ev_a_ln_g', 'ev_a_ln_b', 'ev_sinks', 'ev_w_out', 'od_w_in', 'od_conv_w', 'od_w_out', 'ffn_w_up', 'ffn_conv_w', 'ffn_w_down']
TWIN_WEIGHTS = ['mix_norm_pre', 'mix_norm_post', 'ffn_norm_pre', 'ffn_norm_post', 'ev_w_in', 'ev_a_conv_w', 'ev_a_conv_b', 'ev_a_ln_g', 'ev_a_ln_b', 'ev_sinks', 'ev_w_out', 'od_w_in', 'od_conv_w', 'od_w_out', 'ffn_w_up', 'ffn_conv_w', 'ffn_w_down']
TWIN_DIFF_INPUT = 'x'
TWIN_INPUTS = ['x', 'positions', 'mix_norm_pre', 'mix_norm_post', 'ffn_norm_pre', 'ffn_norm_post', 'ev_w_in', 'ev_a_conv_w', 'ev_a_conv_b', 'ev_a_ln_g', 'ev_a_ln_b', 'ev_sinks', 'ev_w_out', 'od_w_in', 'od_conv_w', 'od_w_out', 'ffn_w_up', 'ffn_conv_w', 'ffn_w_down', 'loss_target', 'm_mix_norm_pre', 'm_mix_norm_post', 'm_ffn_norm_pre', 'm_ffn_norm_post', 'm_ev_w_in', 'm_ev_a_conv_w', 'm_ev_a_conv_b', 'm_ev_a_ln_g', 'm_ev_a_ln_b', 'm_ev_sinks', 'm_ev_w_out', 'm_od_w_in', 'm_od_conv_w', 'm_od_w_out', 'm_ffn_w_up', 'm_ffn_conv_w', 'm_ffn_w_down', 'v_mix_norm_pre', 'v_mix_norm_post', 'v_ffn_norm_pre', 'v_ffn_norm_post', 'v_ev_w_in', 'v_ev_a_conv_w', 'v_ev_a_conv_b', 'v_ev_a_ln_g', 'v_ev_a_ln_b', 'v_ev_sinks', 'v_ev_w_out', 'v_od_w_in', 'v_od_conv_w', 'v_od_w_out', 'v_ffn_w_up', 'v_ffn_conv_w', 'v_ffn_w_down']
TWIN_OUTPUTS = ['loss', 'grad_x', 'grad_mix_norm_pre', 'grad_mix_norm_post', 'grad_ffn_norm_pre', 'grad_ffn_norm_post', 'grad_ev_w_in', 'grad_ev_a_conv_w', 'grad_ev_a_conv_b', 'grad_ev_a_ln_g', 'grad_ev_a_ln_b', 'grad_ev_sinks', 'grad_ev_w_out', 'grad_od_w_in', 'grad_od_conv_w', 'grad_od_w_out', 'grad_ffn_w_up', 'grad_ffn_conv_w', 'grad_ffn_w_down', 'delta_mix_norm_pre', 'delta_mix_norm_post', 'delta_ffn_norm_pre', 'delta_ffn_norm_post', 'delta_ev_w_in', 'delta_ev_a_conv_w', 'delta_ev_a_conv_b', 'delta_ev_a_ln_g', 'delta_ev_a_ln_b', 'delta_ev_sinks', 'delta_ev_w_out', 'delta_od_w_in', 'delta_od_conv_w', 'delta_od_w_out', 'delta_ffn_w_up', 'delta_ffn_conv_w', 'delta_ffn_w_down', 'new_m_mix_norm_pre', 'new_m_mix_norm_post', 'new_m_ffn_norm_pre', 'new_m_ffn_norm_post', 'new_m_ev_w_in', 'new_m_ev_a_conv_w', 'new_m_ev_a_conv_b', 'new_m_ev_a_ln_g', 'new_m_ev_a_ln_b', 'new_m_ev_sinks', 'new_m_ev_w_out', 'new_m_od_w_in', 'new_m_od_conv_w', 'new_m_od_w_out', 'new_m_ffn_w_up', 'new_m_ffn_conv_w', 'new_m_ffn_w_down', 'new_v_mix_norm_pre', 'new_v_mix_norm_post', 'new_v_ffn_norm_pre', 'new_v_ffn_norm_post', 'new_v_ev_w_in', 'new_v_ev_a_conv_w', 'new_v_ev_a_conv_b', 'new_v_ev_a_ln_g', 'new_v_ev_a_ln_b', 'new_v_ev_sinks', 'new_v_ev_w_out', 'new_v_od_w_in', 'new_v_od_conv_w', 'new_v_od_w_out', 'new_v_ffn_w_up', 'new_v_ffn_conv_w', 'new_v_ffn_w_down']
TWIN_LEAF_KINDS = {'loss': 'loss', 'grad_x': 'grad_x', 'grad_mix_norm_pre': 'grad_w', 'grad_mix_norm_post': 'grad_w', 'grad_ffn_norm_pre': 'grad_w', 'grad_ffn_norm_post': 'grad_w', 'grad_ev_w_in': 'grad_w', 'grad_ev_a_conv_w': 'grad_w', 'grad_ev_a_conv_b': 'grad_w', 'grad_ev_a_ln_g': 'grad_w', 'grad_ev_a_ln_b': 'grad_w', 'grad_ev_sinks': 'grad_w', 'grad_ev_w_out': 'grad_w', 'grad_od_w_in': 'grad_w', 'grad_od_conv_w': 'grad_w', 'grad_od_w_out': 'grad_w', 'grad_ffn_w_up': 'grad_w', 'grad_ffn_conv_w': 'grad_w', 'grad_ffn_w_down': 'grad_w', 'delta_mix_norm_pre': 'delta_w', 'delta_mix_norm_post': 'delta_w', 'delta_ffn_norm_pre': 'delta_w', 'delta_ffn_norm_post': 'delta_w', 'delta_ev_w_in': 'delta_w', 'delta_ev_a_conv_w': 'delta_w', 'delta_ev_a_conv_b': 'delta_w', 'delta_ev_a_ln_g': 'delta_w', 'delta_ev_a_ln_b': 'delta_w', 'delta_ev_sinks': 'delta_w', 'delta_ev_w_out': 'delta_w', 'delta_od_w_in': 'delta_w', 'delta_od_conv_w': 'delta_w', 'delta_od_w_out': 'delta_w', 'delta_ffn_w_up': 'delta_w', 'delta_ffn_conv_w': 'delta_w', 'delta_ffn_w_down': 'delta_w', 'new_m_mix_norm_pre': 'new_m', 'new_m_mix_norm_post': 'new_m', 'new_m_ffn_norm_pre': 'new_m', 'new_m_ffn_norm_post': 'new_m', 'new_m_ev_w_in': 'new_m', 'new_m_ev_a_conv_w': 'new_m', 'new_m_ev_a_conv_b': 'new_m', 'new_m_ev_a_ln_g': 'new_m', 'new_m_ev_a_ln_b': 'new_m', 'new_m_ev_sinks': 'new_m', 'new_m_ev_w_out': 'new_m', 'new_m_od_w_in': 'new_m', 'new_m_od_conv_w': 'new_m', 'new_m_od_w_out': 'new_m', 'new_m_ffn_w_up': 'new_m', 'new_m_ffn_conv_w': 'new_m', 'new_m_ffn_w_down': 'new_m', 'new_v_mix_norm_pre': 'new_v', 'new_v_mix_norm_post': 'new_v', 'new_v_ffn_norm_pre': 'new_v', 'new_v_ffn_norm_post': 'new_v', 'new_v_ev_w_in': 'new_v', 'new_v_ev_a_conv_w': 'new_v', 'new_v_ev_a_conv_b': 'new_v', 'new_v_ev_a_ln_g': 'new_v', 'new_v_ev_a_ln_b': 'new_v', 'new_v_ev_sinks': 'new_v', 'new_v_ev_w_out': 'new_v', 'new_v_od_w_in': 'new_v', 'new_v_od_conv_w': 'new_v', 'new_v_od_w_out': 'new_v', 'new_v_ffn_w_up': 'new_v', 'new_v_ffn_conv_w': 'new_v', 'new_v_ffn_w_down': 'new_v'}


def _forward(args):
    return _fwd_reference(*[args[k] for k in FWD_PARAMS])


def _output_shape():
    def fwd():
        inp = _fwd_setup_inputs(0)
        return _fwd_reference(*[inp[k] for k in FWD_PARAMS])
    out = _jax.eval_shape(fwd)
    return out.shape, out.dtype

N_MICROBATCH = 1
ADAM_LR = 0.001
ADAM_B1 = 0.9
ADAM_B2 = 0.999
ADAM_EPS = 1e-08
ADAM_WD = 0.01
ADAM_STEP = 10
PER_EXAMPLE_BATCH_AXIS = {'x': 0, 'positions': 0, 'loss_target': 0}
SHARED_INPUTS = []
_WEIGHT_DTYPES = {'mix_norm_pre': _jnp.float32, 'mix_norm_post': _jnp.float32, 'ffn_norm_pre': _jnp.float32, 'ffn_norm_post': _jnp.float32, 'ev_w_in': _jnp.float32, 'ev_a_conv_w': _jnp.float32, 'ev_a_conv_b': _jnp.float32, 'ev_a_ln_g': _jnp.float32, 'ev_a_ln_b': _jnp.float32, 'ev_sinks': _jnp.float32, 'ev_w_out': _jnp.float32, 'od_w_in': _jnp.float32, 'od_conv_w': _jnp.float32, 'od_w_out': _jnp.float32, 'ffn_w_up': _jnp.float32, 'ffn_conv_w': _jnp.float32, 'ffn_w_down': _jnp.float32}
MOMENT_SCALE = {'mix_norm_pre': 2.244705e+00, 'mix_norm_post': 1.287547e+02, 'ffn_norm_pre': 3.012128e+00, 'ffn_norm_post': 1.282609e+02, 'ev_w_in': 1.754897e+00, 'ev_a_conv_w': 3.419773e+00, 'ev_a_conv_b': 4.468706e+01, 'ev_a_ln_g': 1.532855e+01, 'ev_a_ln_b': 2.618497e+01, 'ev_sinks': 8.116382e-01, 'ev_w_out': 5.295254e+00, 'od_w_in': 1.051150e+00, 'od_conv_w': 1.067308e+00, 'od_w_out': 1.149985e+00, 'ffn_w_up': 1.309906e+00, 'ffn_conv_w': 1.709490e+00, 'ffn_w_down': 2.891627e+00}


def _to_microbatches(a, axis):
    t = _jnp.moveaxis(a, axis, 0)
    t = t.reshape((N_MICROBATCH, t.shape[0] // N_MICROBATCH) + t.shape[1:])
    return _jnp.moveaxis(t, 1, axis + 1)


def setup_inputs(seed: int = 0) -> dict:
    inp = _fwd_setup_inputs(seed)
    key = _jax.random.fold_in(_jax.random.key(seed), 7919)
    shape, _ = _output_shape()
    out = dict(inp)
    out["loss_target"] = _jax.random.normal(_jax.random.fold_in(key, 0), shape, _jnp.float32)
    for i, name in enumerate(TWIN_WEIGHTS):
        w = inp[name].astype(_jnp.float32)
        if MOMENT_SCALE is None:
            s = _jnp.sqrt(_jnp.mean(_jnp.square(w)) + 1e-30)
        else:
            s = MOMENT_SCALE[name]
        km, kv = _jax.random.split(_jax.random.fold_in(key, i + 1))
        out[name] = w
        out["m_" + name] = s * _jax.random.normal(km, w.shape, _jnp.float32)
        out["v_" + name] = (s * s) * _jax.random.uniform(kv, w.shape, _jnp.float32, 0.5, 1.5)
    if N_MICROBATCH > 1:
        for name, axis in PER_EXAMPLE_BATCH_AXIS.items():
            out[name] = _to_microbatches(out[name], axis)
    return {'x': out['x'], 'positions': out['positions'], 'mix_norm_pre': out['mix_norm_pre'], 'mix_norm_post': out['mix_norm_post'], 'ffn_norm_pre': out['ffn_norm_pre'], 'ffn_norm_post': out['ffn_norm_post'], 'ev_w_in': out['ev_w_in'], 'ev_a_conv_w': out['ev_a_conv_w'], 'ev_a_conv_b': out['ev_a_conv_b'], 'ev_a_ln_g': out['ev_a_ln_g'], 'ev_a_ln_b': out['ev_a_ln_b'], 'ev_sinks': out['ev_sinks'], 'ev_w_out': out['ev_w_out'], 'od_w_in': out['od_w_in'], 'od_conv_w': out['od_conv_w'], 'od_w_out': out['od_w_out'], 'ffn_w_up': out['ffn_w_up'], 'ffn_conv_w': out['ffn_conv_w'], 'ffn_w_down': out['ffn_w_down'], 'loss_target': out['loss_target'], 'm_mix_norm_pre': out['m_mix_norm_pre'], 'm_mix_norm_post': out['m_mix_norm_post'], 'm_ffn_norm_pre': out['m_ffn_norm_pre'], 'm_ffn_norm_post': out['m_ffn_norm_post'], 'm_ev_w_in': out['m_ev_w_in'], 'm_ev_a_conv_w': out['m_ev_a_conv_w'], 'm_ev_a_conv_b': out['m_ev_a_conv_b'], 'm_ev_a_ln_g': out['m_ev_a_ln_g'], 'm_ev_a_ln_b': out['m_ev_a_ln_b'], 'm_ev_sinks': out['m_ev_sinks'], 'm_ev_w_out': out['m_ev_w_out'], 'm_od_w_in': out['m_od_w_in'], 'm_od_conv_w': out['m_od_conv_w'], 'm_od_w_out': out['m_od_w_out'], 'm_ffn_w_up': out['m_ffn_w_up'], 'm_ffn_conv_w': out['m_ffn_conv_w'], 'm_ffn_w_down': out['m_ffn_w_down'], 'v_mix_norm_pre': out['v_mix_norm_pre'], 'v_mix_norm_post': out['v_mix_norm_post'], 'v_ffn_norm_pre': out['v_ffn_norm_pre'], 'v_ffn_norm_post': out['v_ffn_norm_post'], 'v_ev_w_in': out['v_ev_w_in'], 'v_ev_a_conv_w': out['v_ev_a_conv_w'], 'v_ev_a_conv_b': out['v_ev_a_conv_b'], 'v_ev_a_ln_g': out['v_ev_a_ln_g'], 'v_ev_a_ln_b': out['v_ev_a_ln_b'], 'v_ev_sinks': out['v_ev_sinks'], 'v_ev_w_out': out['v_ev_w_out'], 'v_od_w_in': out['v_od_w_in'], 'v_od_conv_w': out['v_od_conv_w'], 'v_od_w_out': out['v_od_w_out'], 'v_ffn_w_up': out['v_ffn_w_up'], 'v_ffn_conv_w': out['v_ffn_conv_w'], 'v_ffn_w_down': out['v_ffn_w_down']}


def _loss(weights, diff, rest, loss_target):
    with _jax.named_scope("forward"):
        args = {**rest, TWIN_DIFF_INPUT: diff, **{k: w.astype(_WEIGHT_DTYPES[k]) for k, w in weights.items()}}
        y = _forward(args)
    with _jax.named_scope("loss_head"):
        err = _jnp.square(y.astype(_jnp.float32) - loss_target)
        return 0.5 * _jnp.sum(_jnp.mean(err, axis=-1)) if err.ndim else 0.5 * err


def _adamw(w, g, m, v):
    m = ADAM_B1 * m + (1.0 - ADAM_B1) * g
    v = ADAM_B2 * v + (1.0 - ADAM_B2) * _jnp.square(g)
    m_hat = m / (1.0 - ADAM_B1 ** ADAM_STEP)
    v_hat = v / (1.0 - ADAM_B2 ** ADAM_STEP)
    delta = -ADAM_LR * (m_hat / (_jnp.sqrt(v_hat) + ADAM_EPS) + ADAM_WD * w)
    return delta, m, v


def reference(x, positions, mix_norm_pre, mix_norm_post, ffn_norm_pre, ffn_norm_post, ev_w_in, ev_a_conv_w, ev_a_conv_b, ev_a_ln_g, ev_a_ln_b, ev_sinks, ev_w_out, od_w_in, od_conv_w, od_w_out, ffn_w_up, ffn_conv_w, ffn_w_down, loss_target, m_mix_norm_pre, m_mix_norm_post, m_ffn_norm_pre, m_ffn_norm_post, m_ev_w_in, m_ev_a_conv_w, m_ev_a_conv_b, m_ev_a_ln_g, m_ev_a_ln_b, m_ev_sinks, m_ev_w_out, m_od_w_in, m_od_conv_w, m_od_w_out, m_ffn_w_up, m_ffn_conv_w, m_ffn_w_down, v_mix_norm_pre, v_mix_norm_post, v_ffn_norm_pre, v_ffn_norm_post, v_ev_w_in, v_ev_a_conv_w, v_ev_a_conv_b, v_ev_a_ln_g, v_ev_a_ln_b, v_ev_sinks, v_ev_w_out, v_od_w_in, v_od_conv_w, v_od_w_out, v_ffn_w_up, v_ffn_conv_w, v_ffn_w_down):
    given = dict(x=x, positions=positions, mix_norm_pre=mix_norm_pre, mix_norm_post=mix_norm_post, ffn_norm_pre=ffn_norm_pre, ffn_norm_post=ffn_norm_post, ev_w_in=ev_w_in, ev_a_conv_w=ev_a_conv_w, ev_a_conv_b=ev_a_conv_b, ev_a_ln_g=ev_a_ln_g, ev_a_ln_b=ev_a_ln_b, ev_sinks=ev_sinks, ev_w_out=ev_w_out, od_w_in=od_w_in, od_conv_w=od_conv_w, od_w_out=od_w_out, ffn_w_up=ffn_w_up, ffn_conv_w=ffn_conv_w, ffn_w_down=ffn_w_down, loss_target=loss_target, m_mix_norm_pre=m_mix_norm_pre, m_mix_norm_post=m_mix_norm_post, m_ffn_norm_pre=m_ffn_norm_pre, m_ffn_norm_post=m_ffn_norm_post, m_ev_w_in=m_ev_w_in, m_ev_a_conv_w=m_ev_a_conv_w, m_ev_a_conv_b=m_ev_a_conv_b, m_ev_a_ln_g=m_ev_a_ln_g, m_ev_a_ln_b=m_ev_a_ln_b, m_ev_sinks=m_ev_sinks, m_ev_w_out=m_ev_w_out, m_od_w_in=m_od_w_in, m_od_conv_w=m_od_conv_w, m_od_w_out=m_od_w_out, m_ffn_w_up=m_ffn_w_up, m_ffn_conv_w=m_ffn_conv_w, m_ffn_w_down=m_ffn_w_down, v_mix_norm_pre=v_mix_norm_pre, v_mix_norm_post=v_mix_norm_post, v_ffn_norm_pre=v_ffn_norm_pre, v_ffn_norm_post=v_ffn_norm_post, v_ev_w_in=v_ev_w_in, v_ev_a_conv_w=v_ev_a_conv_w, v_ev_a_conv_b=v_ev_a_conv_b, v_ev_a_ln_g=v_ev_a_ln_g, v_ev_a_ln_b=v_ev_a_ln_b, v_ev_sinks=v_ev_sinks, v_ev_w_out=v_ev_w_out, v_od_w_in=v_od_w_in, v_od_conv_w=v_od_conv_w, v_od_w_out=v_od_w_out, v_ffn_w_up=v_ffn_w_up, v_ffn_conv_w=v_ffn_conv_w, v_ffn_w_down=v_ffn_w_down)
    weights = {n: given[n] for n in TWIN_WEIGHTS}
    shared = {n: given[n] for n in SHARED_INPUTS}
    per_example = {n: given[n] for n in ['x', 'positions']}
    grad_fn = _jax.value_and_grad(_loss, argnums=(0, 1))

    def one_microbatch(ex, loss_target):
        ex = dict(ex)
        diff = ex.pop(TWIN_DIFF_INPUT)
        return grad_fn(weights, diff, {**shared, **ex}, loss_target)

    if N_MICROBATCH == 1:
        loss, (grad_w, grad_x) = one_microbatch(per_example, given["loss_target"])
    else:
        def body(carry, xs):
            loss_sum, grad_sum = carry
            l_k, (gw_k, gx_k) = one_microbatch(xs[0], xs[1])
            with _jax.named_scope("update"):
                return (loss_sum + l_k, _jax.tree.map(_jnp.add, grad_sum, gw_k)), gx_k

        init = (_jnp.zeros((), _jnp.float32), _jax.tree.map(_jnp.zeros_like, weights))
        (loss, grad_w), grad_x = _jax.lax.scan(body, init, (per_example, given["loss_target"]))
    with _jax.named_scope("update"):
        delta_w, new_m, new_v = {}, {}, {}
        for n in TWIN_WEIGHTS:
            delta_w[n], new_m[n], new_v[n] = _adamw(weights[n], grad_w[n], given["m_" + n], given["v_" + n])
    return (loss, grad_x, *[grad_w[n] for n in TWIN_WEIGHTS], *[delta_w[n] for n in TWIN_WEIGHTS],
            *[new_m[n] for n in TWIN_WEIGHTS], *[new_v[n] for n in TWIN_WEIGHTS])
```

```python
import functools

import jax
import jax.numpy as jnp
from jax import lax
from jax.experimental import pallas as pl
from jax.experimental.pallas import tpu as pltpu

F32, BF16 = jnp.float32, jnp.bfloat16

D_MODEL = 1024
A_CH = 512
A_CONV = 31
Q_DIM = 512
KV_DIM = 128
HEAD_DIM = 64
N_Q_HEADS = 8
N_KV_HEADS = 2
GROUP = 4
BLOCK = 128
EVEN_IN = 1792
SC_DIM = 1024
D_FF = 2816
FF_CH = D_FF // 2
ROPE_THETA = 500000.0
ROPE_DIM = 16
RMS_EPS = 1e-6
LN_EPS = 1e-5
SCALE = HEAD_DIM ** -0.5
NEG = -1e30

ADAM_LR, ADAM_B1, ADAM_B2, ADAM_EPS, ADAM_WD, ADAM_STEP = 0.001, 0.9, 0.999, 1e-08, 0.01, 10

N_DEV = 8
LANES = 1024
HALO3 = 8
HALO31 = 32
VMEM_LIMIT = 56 * 1024 * 1024

TM = 512
TM_BWD = 256
TK_DW = 512


def _cp(sem, vmem=VMEM_LIMIT):
    return pltpu.CompilerParams(dimension_semantics=sem, vmem_limit_bytes=vmem)


def _full(shape):
    n = len(shape)
    return pl.BlockSpec(shape, lambda *_: (0,) * n)


def _rows(tm, n):
    return pl.BlockSpec((tm, n), lambda i, *_: (i, 0))


def _rsqrt_mean(x):
    return lax.rsqrt(jnp.mean(x * x, axis=-1, keepdims=True) + RMS_EPS)


def _rms_bwd(x, g, dy):
    r = _rsqrt_mean(x)
    xh = x * r
    dxh = dy * g
    dx = r * (dxh - xh * jnp.mean(dxh * xh, axis=-1, keepdims=True))
    return dx, jnp.sum(dy * xh, axis=0, keepdims=True)


def _acc_out(ref, first, val):
    @pl.when(first)
    def _():
        ref[...] = val

    @pl.when(jnp.logical_not(first))
    def _():
        ref[...] += val


def _rope_tables(positions):
    half = ROPE_DIM // 2
    inv_freq = ROPE_THETA ** (-(jnp.arange(half, dtype=F32) * 2.0 / ROPE_DIM))
    ang = positions.astype(F32)[:, None] * inv_freq
    cos, sin = jnp.cos(ang), jnp.sin(ang)
    t = positions.shape[0]
    one, zero = jnp.ones((t, HEAD_DIM - ROPE_DIM), F32), jnp.zeros((t, HEAD_DIM - ROPE_DIM), F32)
    z8 = jnp.zeros((t, half), F32)
    c = jnp.concatenate([cos, cos, one], axis=1)
    sa = jnp.concatenate([z8, sin, zero], axis=1)
    sb = jnp.concatenate([-sin, z8, zero], axis=1)
    return tuple(jnp.tile(a, (1, 2)) for a in (c, sa, sb))


def _rope(t, c, sa, sb):
    return t * c + pltpu.roll(t, 8, 1) * sa + pltpu.roll(t, 120, 1) * sb


def _rope_bwd(d, c, sa, sb):
    return d * c + pltpu.roll(d * sa, 120, 1) + pltpu.roll(d * sb, 8, 1)


def _ev_in(x, gpre, w_in, rc, rsa, rsb):
    t = x.shape[0]
    tm = min(TM, t)

    def body(x_ref, g_ref, w_ref, c_ref, sa_ref, sb_ref, h_ref, zag_ref, q_ref, k_ref, v_ref):
        xv = x_ref[...]
        h = (xv * _rsqrt_mean(xv) * g_ref[...]).astype(BF16)
        h_ref[...] = h
        z = jnp.dot(h, w_ref[...], preferred_element_type=F32)
        zag_ref[...] = z[:, :2 * A_CH].astype(BF16)
        c, sa, sb = c_ref[...], sa_ref[...], sb_ref[...]
        q0 = 2 * A_CH
        for j in range(Q_DIM // 128):
            q_ref[:, 128 * j:128 * (j + 1)] = _rope(z[:, q0 + 128 * j:q0 + 128 * (j + 1)], c, sa, sb).astype(BF16)
        k0 = q0 + Q_DIM
        k_ref[...] = _rope(z[:, k0:k0 + KV_DIM], c, sa, sb).astype(BF16)
        v_ref[...] = z[:, k0 + KV_DIM:k0 + 2 * KV_DIM].astype(BF16)

    return pl.pallas_call(
        body, name="ev_in", grid=(t // tm,),
        in_specs=[_rows(tm, D_MODEL), _full((1, D_MODEL)), _full((D_MODEL, EVEN_IN)),
                  _rows(tm, 128), _rows(tm, 128), _rows(tm, 128)],
        out_specs=[_rows(tm, D_MODEL), _rows(tm, 2 * A_CH), _rows(tm, Q_DIM), _rows(tm, KV_DIM), _rows(tm, KV_DIM)],
        out_shape=[jax.ShapeDtypeStruct((t, D_MODEL), BF16), jax.ShapeDtypeStruct((t, 2 * A_CH), BF16),
                   jax.ShapeDtypeStruct((t, Q_DIM), BF16), jax.ShapeDtypeStruct((t, KV_DIM), BF16),
                   jax.ShapeDtypeStruct((t, KV_DIM), BF16)],
        compiler_params=_cp(("arbitrary",)),
    )(x, gpre, w_in, rc, rsa, rsb)


def _glu(zag):
    z = zag.astype(F32)
    return z[:, :A_CH] * jax.nn.sigmoid(z[:, A_CH:])


def _conf_fwd(zag, conv_w, conv_b, ln_g, ln_b):
    t = zag.shape[0]
    tm = min(TM_BWD, t)

    def body(z_ref, w_ref, b_ref, g_ref, lb_ref, c_ref, a_ref, ext):
        i = pl.program_id(0)

        @pl.when(i == 0)
        def _():
            ext[0:HALO31, :] = jnp.zeros((HALO31, A_CH), F32)

        ext[HALO31:HALO31 + tm, :] = _glu(z_ref[...])
        acc = jnp.zeros((tm, A_CH), F32)
        for j in range(A_CONV):
            s = HALO31 - (A_CONV - 1) + j
            acc = acc + w_ref[j:j + 1, :] * ext[s:s + tm, :]
        ext[0:HALO31, :] = ext[tm:tm + HALO31, :]
        cv = acc + b_ref[...]
        c_ref[...] = cv
        mu = jnp.mean(cv, axis=-1, keepdims=True)
        xc = cv - mu
        ln = xc * lax.rsqrt(jnp.mean(xc * xc, axis=-1, keepdims=True) + LN_EPS) * g_ref[...] + lb_ref[...]
        a_ref[...] = (ln * jax.nn.sigmoid(ln)).astype(BF16)

    return pl.pallas_call(
        body, name="conf_fwd", grid=(t // tm,),
        in_specs=[_rows(tm, 2 * A_CH), _full((32, A_CH)), _full((1, A_CH)), _full((1, A_CH)), _full((1, A_CH))],
        out_specs=[_rows(tm, A_CH), _rows(tm, A_CH)],
        out_shape=[jax.ShapeDtypeStruct((t, A_CH), F32), jax.ShapeDtypeStruct((t, A_CH), BF16)],
        scratch_shapes=[pltpu.VMEM((HALO31 + tm, A_CH), F32)],
        compiler_params=_cp(("arbitrary",)),
    )(zag, conv_w, conv_b, ln_g, ln_b)


def _attn_mask(first_block):
    row = lax.broadcasted_iota(jnp.int32, (GROUP * BLOCK, 2 * BLOCK), 0) & (BLOCK - 1)
    col = lax.broadcasted_iota(jnp.int32, (GROUP * BLOCK, 2 * BLOCK), 1)
    diff = row + BLOCK - col
    return (diff >= 0) & (diff < BLOCK) & ((col >= BLOCK) | jnp.logical_not(first_block))


def _sink_rows(s_ref, h):
    grp = lax.broadcasted_iota(jnp.int32, (GROUP * BLOCK, 1), 0) >> 7
    out = jnp.full((GROUP * BLOCK, 1), s_ref[GROUP * h], F32)
    for g in range(1, GROUP):
        out = jnp.where(grp == g, s_ref[GROUP * h + g], out)
    return out


_CONTRACT_LAST = (((1,), (1,)), ((), ()))
_CONTRACT_FIRST = (((0,), (0,)), ((), ()))


def _attn_probs(q4, k2, mask, sink):
    s = lax.dot_general(q4, k2, _CONTRACT_LAST, preferred_element_type=F32) * SCALE
    s = jnp.where(mask, s, NEG)
    m = jnp.maximum(jnp.max(s, axis=-1, keepdims=True), sink)
    p = jnp.exp(s - m)
    es = jnp.exp(sink - m)
    den = jnp.sum(p, axis=-1, keepdims=True) + es
    return p / den, es / den


def _attn_fwd(qh, kh, vh, sinks):
    t = qh.shape[1]
    nb = t // BLOCK

    def body(s_ref, q_ref, kc_ref, kp_ref, vc_ref, vp_ref, o_ref):
        mask = _attn_mask(pl.program_id(0) == 0)
        for h in range(N_KV_HEADS):
            q4 = q_ref[GROUP * h:GROUP * (h + 1)].reshape(GROUP * BLOCK, HEAD_DIM)
            k2 = jnp.concatenate([kp_ref[h], kc_ref[h]], axis=0)
            v2 = jnp.concatenate([vp_ref[h], vc_ref[h]], axis=0)
            pn, _ = _attn_probs(q4, k2, mask, _sink_rows(s_ref, h))
            o4 = jnp.dot(pn.astype(BF16), v2, preferred_element_type=F32)
            o_ref[GROUP * h:GROUP * (h + 1)] = o4.reshape(GROUP, BLOCK, HEAD_DIM).astype(BF16)

    cur = lambda n: pl.BlockSpec((n, BLOCK, HEAD_DIM), lambda i: (0, i, 0))
    prev = lambda n: pl.BlockSpec((n, BLOCK, HEAD_DIM), lambda i: (0, jnp.maximum(i - 1, 0), 0))
    return pl.pallas_call(
        body, name="attn_fwd", grid=(nb,),
        in_specs=[pl.BlockSpec(memory_space=pltpu.SMEM), cur(N_Q_HEADS), cur(N_KV_HEADS), prev(N_KV_HEADS),
                  cur(N_KV_HEADS), prev(N_KV_HEADS)],
        out_specs=cur(N_Q_HEADS),
        out_shape=jax.ShapeDtypeStruct((N_Q_HEADS, t, HEAD_DIM), BF16),
        compiler_params=_cp(("arbitrary",)),
    )(sinks, qh, kh, kh, vh, vh)


def _out_post(lhs, ws, x_in, gpost):
    t = x_in.shape[0]
    tm = min(TM, t)
    n = len(lhs)

    def body(*refs):
        x_ref, g_ref, m_ref, xo_ref = refs[2 * n:]
        m = jnp.dot(refs[0][...], refs[n][...], preferred_element_type=F32)
        for j in range(1, n):
            m = m + jnp.dot(refs[j][...], refs[n + j][...], preferred_element_type=F32)
        m_ref[...] = m
        xo_ref[...] = x_ref[...] + m * _rsqrt_mean(m) * g_ref[...]

    return pl.pallas_call(
        body, name="out_post", grid=(t // tm,),
        in_specs=[_rows(tm, a.shape[1]) for a in lhs] + [_full(w.shape) for w in ws]
                 + [_rows(tm, D_MODEL), _full((1, D_MODEL))],
        out_specs=[_rows(tm, D_MODEL), _rows(tm, D_MODEL)],
        out_shape=[jax.ShapeDtypeStruct((t, D_MODEL), F32)] * 2,
        compiler_params=_cp(("arbitrary",)),
    )(*lhs, *ws, x_in, gpost)


def _conv3(w_ref, ext, tm):
    s = HALO3 - 2
    return (w_ref[0:1, :] * ext[s:s + tm, :] + w_ref[1:2, :] * ext[s + 1:s + 1 + tm, :]
            + w_ref[2:3, :] * ext[s + 2:s + 2 + tm, :])


def _ffn_fwd(x1, gpre, wup, cw, wd, gpost):
    t = x1.shape[0]
    tm = min(TM, t)
    nc = D_FF // FF_CH
    w2 = 2 * FF_CH

    def body(x_ref, gpre_ref, wup_ref, cw_ref, wd_ref, gpost_ref, h_ref, up_ref, f_ref, xo_ref, h_s, acc, ext, hal):
        i, c = pl.program_id(0), pl.program_id(1)

        @pl.when(c == 0)
        def _():
            xv = x_ref[...]
            h = (xv * _rsqrt_mean(xv) * gpre_ref[...]).astype(BF16)
            h_s[...] = h
            h_ref[...] = h

        up = jnp.dot(h_s[...], wup_ref[...], preferred_element_type=F32)
        up_ref[...] = up.astype(BF16)

        @pl.when(i == 0)
        def _():
            hal[c] = jnp.zeros((HALO3, w2), F32)

        ext[0:HALO3, :] = hal[c]
        ext[HALO3:HALO3 + tm, :] = up
        hal[c] = ext[tm:tm + HALO3, :]
        u = _conv3(cw_ref, ext, tm)
        g, v = u[:, :FF_CH], u[:, FF_CH:]
        act = (g * jax.nn.sigmoid(g) * v).astype(BF16)
        part = jnp.dot(act, wd_ref[...], preferred_element_type=F32)

        @pl.when(c == 0)
        def _():
            acc[...] = part

        @pl.when(jnp.logical_and(c > 0, c < nc - 1))
        def _():
            acc[...] += part

        @pl.when(c == nc - 1)
        def _():
            f = acc[...] + part
            f_ref[...] = f
            xo_ref[...] = x_ref[...] + f * _rsqrt_mean(f) * gpost_ref[...]

    row = lambda n: pl.BlockSpec((tm, n), lambda i, c: (i, 0))
    return pl.pallas_call(
        body, name="ffn_fwd", grid=(t // tm, nc),
        in_specs=[row(D_MODEL), _full((1, D_MODEL)), pl.BlockSpec((D_MODEL, w2), lambda i, c: (0, c)),
                  pl.BlockSpec((3, w2), lambda i, c: (0, c)), pl.BlockSpec((FF_CH, D_MODEL), lambda i, c: (c, 0)),
                  _full((1, D_MODEL))],
        out_specs=[row(D_MODEL), pl.BlockSpec((tm, w2), lambda i, c: (i, c)), row(D_MODEL), row(D_MODEL)],
        out_shape=[jax.ShapeDtypeStruct((t, D_MODEL), BF16), jax.ShapeDtypeStruct((t, 2 * D_FF), BF16),
                   jax.ShapeDtypeStruct((t, D_MODEL), F32), jax.ShapeDtypeStruct((t, D_MODEL), F32)],
        scratch_shapes=[pltpu.VMEM((tm, D_MODEL), BF16), pltpu.VMEM((tm, D_MODEL), F32),
                        pltpu.VMEM((HALO3 + tm, w2), F32), pltpu.VMEM((nc, HALO3, w2), F32)],
        compiler_params=_cp(("arbitrary", "arbitrary")),
    )(x1, gpre, wup, cw, wd, gpost)


def _od_fwd(x_in, gpre, w_in, cw, w_out, gpost):
    t = x_in.shape[0]
    tm = min(TM, t)

    def body(x_ref, gpre_ref, w_ref, cw_ref, wo_ref, gpost_ref, h_ref, z_ref, y_ref, m_ref, xo_ref, ext):
        i = pl.program_id(0)
        xv = x_ref[...]
        h = (xv * _rsqrt_mean(xv) * gpre_ref[...]).astype(BF16)
        h_ref[...] = h
        z = jnp.dot(h, w_ref[...], preferred_element_type=F32)
        z_ref[...] = z.astype(BF16)

        @pl.when(i == 0)
        def _():
            ext[0:HALO3, :] = jnp.zeros((HALO3, SC_DIM), F32)

        ext[HALO3:HALO3 + tm, :] = z[:, SC_DIM:2 * SC_DIM] * z[:, 2 * SC_DIM:]
        y = (z[:, :SC_DIM] * _conv3(cw_ref, ext, tm)).astype(BF16)
        ext[0:HALO3, :] = ext[tm:tm + HALO3, :]
        y_ref[...] = y
        m = jnp.dot(y, wo_ref[...], preferred_element_type=F32)
        m_ref[...] = m
        xo_ref[...] = xv + m * _rsqrt_mean(m) * gpost_ref[...]

    return pl.pallas_call(
        body, name="od_fwd", grid=(t // tm,),
        in_specs=[_rows(tm, D_MODEL), _full((1, D_MODEL)), _full((D_MODEL, 3 * SC_DIM)), _full((3, SC_DIM)),
                  _full((SC_DIM, D_MODEL)), _full((1, D_MODEL))],
        out_specs=[_rows(tm, D_MODEL), _rows(tm, 3 * SC_DIM), _rows(tm, SC_DIM), _rows(tm, D_MODEL),
                   _rows(tm, D_MODEL)],
        out_shape=[jax.ShapeDtypeStruct((t, D_MODEL), BF16), jax.ShapeDtypeStruct((t, 3 * SC_DIM), BF16),
                   jax.ShapeDtypeStruct((t, SC_DIM), BF16), jax.ShapeDtypeStruct((t, D_MODEL), F32),
                   jax.ShapeDtypeStruct((t, D_MODEL), F32)],
        scratch_shapes=[pltpu.VMEM((HALO3 + tm, SC_DIM), F32)],
        compiler_params=_cp(("arbitrary",)),
    )(x_in, gpre, w_in, cw, w_out, gpost)


def _loss_grad(y, target):
    t = y.shape[0]
    tm = min(TM, t)

    def body(y_ref, t_ref, dy_ref, l_ref):
        e = y_ref[...] - t_ref[...]
        dy_ref[...] = e * (1.0 / D_MODEL)
        part = jnp.zeros((1, 128), F32) + jnp.sum(e * e) * (0.5 / D_MODEL)
        _acc_out(l_ref, pl.program_id(0) == 0, part)

    return pl.pallas_call(
        body, name="loss_grad", grid=(t // tm,),
        in_specs=[_rows(tm, D_MODEL), _rows(tm, D_MODEL)],
        out_specs=[_rows(tm, D_MODEL), _full((1, 128))],
        out_shape=[jax.ShapeDtypeStruct((t, D_MODEL), F32), jax.ShapeDtypeStruct((1, 128), F32)],
        compiler_params=_cp(("arbitrary",)),
    )(y, target)


def _dw(a, b, bm, bn):
    t, m = a.shape
    n = b.shape[1]
    tk = min(TK_DW, t)

    def body(a_ref, b_ref, o_ref):
        part = lax.dot_general(a_ref[...], b_ref[...], _CONTRACT_FIRST, preferred_element_type=F32)
        _acc_out(o_ref, pl.program_id(2) == 0, part)

    return pl.pallas_call(
        body, name="dw", grid=(m // bm, n // bn, t // tk),
        in_specs=[pl.BlockSpec((tk, bm), lambda i, j, k: (k, i)), pl.BlockSpec((tk, bn), lambda i, j, k: (k, j))],
        out_specs=pl.BlockSpec((bm, bn), lambda i, j, k: (i, j)),
        out_shape=jax.ShapeDtypeStruct((m, n), F32),
        compiler_params=_cp(("arbitrary", "arbitrary", "arbitrary")),
    )(a, b)


def _dz_wt_rms_bwd(dz, wt, x_in, gpre, dres, kc):
    t, n = dz.shape
    tm = min(TM, t)
    nk = n // kc

    def body(dz_ref, wt_ref, x_ref, g_ref, dres_ref, dx_ref, dg_ref, acc):
        i, k = pl.program_id(0), pl.program_id(1)
        part = jnp.dot(dz_ref[...], wt_ref[...], preferred_element_type=F32)
        if nk > 1:
            _acc_out(acc, k == 0, part)

        @pl.when(k == nk - 1)
        def _():
            dh = acc[...] if nk > 1 else part
            dx, dg = _rms_bwd(x_ref[...], g_ref[...], dh)
            dx_ref[...] = dres_ref[...] + dx
            _acc_out(dg_ref, i == 0, dg)

    row = lambda w: pl.BlockSpec((tm, w), lambda i, k: (i, 0))
    return pl.pallas_call(
        body, name="dz_wt_rms_bwd", grid=(t // tm, nk),
        in_specs=[pl.BlockSpec((tm, kc), lambda i, k: (i, k)), pl.BlockSpec((kc, D_MODEL), lambda i, k: (k, 0)),
                  row(D_MODEL), _full((1, D_MODEL)), row(D_MODEL)],
        out_specs=[row(D_MODEL), _full((1, D_MODEL))],
        out_shape=[jax.ShapeDtypeStruct((t, D_MODEL), F32), jax.ShapeDtypeStruct((1, D_MODEL), F32)],
        scratch_shapes=[pltpu.VMEM((tm, D_MODEL), F32)],
        compiler_params=_cp(("arbitrary", "arbitrary")),
    )(dz, wt, x_in, gpre, dres)


def _ffn_bwd1(f, dxo, gpost, up, cw, wdt):
    t = f.shape[0]
    tm = min(TM_BWD, t)
    nt = t // tm
    nc = D_FF // FF_CH
    w2 = 2 * FF_CH
    hb = 16

    def body(f_ref, dxo_ref, g_ref, up_ref, uph_ref, cw_ref, wdt_ref,
             df_ref, act_ref, dup_ref, dg_ref, dcw_ref, df_s, ext_in, ext_out, hal):
        i, c = pl.program_id(0), pl.program_id(1)
        r = nt - 1 - i

        @pl.when(c == 0)
        def _():
            df, dg = _rms_bwd(f_ref[...], g_ref[...], dxo_ref[...])
            df_s[...] = df.astype(BF16)
            df_ref[...] = df.astype(BF16)
            _acc_out(dg_ref, i == 0, dg)

        @pl.when(i == 0)
        def _():
            hal[c] = jnp.zeros((HALO3, w2), F32)
            dcw_ref[c] = jnp.zeros((8, w2), F32)

        prev = uph_ref[...].astype(F32)[hb - HALO3:, :]
        ext_in[0:HALO3, :] = jnp.where(r > 0, prev, 0.0)
        ext_in[HALO3:HALO3 + tm, :] = up_ref[...].astype(F32)
        u = _conv3(cw_ref, ext_in, tm)
        g, v = u[:, :FF_CH], u[:, FF_CH:]
        sg = jax.nn.sigmoid(g)
        sil = g * sg
        act_ref[...] = (sil * v).astype(BF16)
        dact = jnp.dot(df_s[...], wdt_ref[...], preferred_element_type=F32)
        ext_out[0:tm, 0:FF_CH] = dact * v * (sg * (1.0 + g * (1.0 - sg)))
        ext_out[0:tm, FF_CH:w2] = dact * sil
        ext_out[tm:tm + HALO3, :] = hal[c]
        hal[c] = ext_out[0:HALO3, :]
        dup = (cw_ref[0:1, :] * ext_out[2:2 + tm, :] + cw_ref[1:2, :] * ext_out[1:1 + tm, :]
               + cw_ref[2:3, :] * ext_out[0:tm, :])
        dup_ref[...] = dup.astype(BF16)
        du = ext_out[0:tm, :]
        for j in range(3):
            s = HALO3 - 2 + j
            dcw_ref[c, j:j + 1, :] += jnp.sum(du * ext_in[s:s + tm, :], axis=0, keepdims=True)

    rrow = lambda w: pl.BlockSpec((tm, w), lambda i, c: (nt - 1 - i, 0))
    halo_idx = lambda i, c: (jnp.maximum((nt - 1 - i) * (tm // hb) - 1, 0), c)
    return pl.pallas_call(
        body, name="ffn_bwd1", grid=(nt, nc),
        in_specs=[rrow(D_MODEL), rrow(D_MODEL), _full((1, D_MODEL)),
                  pl.BlockSpec((tm, w2), lambda i, c: (nt - 1 - i, c)), pl.BlockSpec((hb, w2), halo_idx),
                  pl.BlockSpec((3, w2), lambda i, c: (0, c)), pl.BlockSpec((D_MODEL, FF_CH), lambda i, c: (0, c))],
        out_specs=[rrow(D_MODEL), pl.BlockSpec((tm, FF_CH), lambda i, c: (nt - 1 - i, c)),
                   pl.BlockSpec((tm, w2), lambda i, c: (nt - 1 - i, c)), _full((1, D_MODEL)), _full((nc, 8, w2))],
        out_shape=[jax.ShapeDtypeStruct((t, D_MODEL), BF16), jax.ShapeDtypeStruct((t, D_FF), BF16),
                   jax.ShapeDtypeStruct((t, 2 * D_FF), BF16), jax.ShapeDtypeStruct((1, D_MODEL), F32),
                   jax.ShapeDtypeStruct((nc, 8, w2), F32)],
        scratch_shapes=[pltpu.VMEM((tm, D_MODEL), BF16), pltpu.VMEM((HALO3 + tm, w2), F32),
                        pltpu.VMEM((tm + HALO3, w2), F32), pltpu.VMEM((nc, HALO3, w2), F32)],
        compiler_params=_cp(("arbitrary", "arbitrary")),
    )(f, dxo, gpost, up, up, cw, wdt)


def _od_bwd1(m, dxo, gpost, z, cw, wot):
    t = m.shape[0]
    tm = min(TM_BWD, t)
    nt = t // tm
    hb = 16

    def body(m_ref, dxo_ref, g_ref, z_ref, zh_ref, cw_ref, wot_ref, dm_ref, dz_ref, dg_ref, dcw_ref,
             ext_in, ext_out):
        i = pl.program_id(0)
        r = nt - 1 - i
        dm, dg = _rms_bwd(m_ref[...], g_ref[...], dxo_ref[...])
        dmb = dm.astype(BF16)
        dm_ref[...] = dmb
        _acc_out(dg_ref, i == 0, dg)

        @pl.when(i == 0)
        def _():
            ext_out[tm:tm + HALO3, :] = jnp.zeros((HALO3, SC_DIM), F32)
            dcw_ref[...] = jnp.zeros((8, SC_DIM), F32)

        dy = jnp.dot(dmb, wot_ref[...], preferred_element_type=F32)
        z = z_ref[...].astype(F32)
        b, cg, u = z[:, :SC_DIM], z[:, SC_DIM:2 * SC_DIM], z[:, 2 * SC_DIM:]
        zp = zh_ref[...].astype(F32)[hb - HALO3:, :]
        ext_in[0:HALO3, :] = jnp.where(r > 0, zp[:, SC_DIM:2 * SC_DIM] * zp[:, 2 * SC_DIM:], 0.0)
        ext_in[HALO3:HALO3 + tm, :] = cg * u
        cv = _conv3(cw_ref, ext_in, tm)
        dz_ref[:, 0:SC_DIM] = (dy * cv).astype(BF16)
        dcv = dy * b
        ext_out[0:tm, :] = dcv
        dcu = (cw_ref[0:1, :] * ext_out[2:2 + tm, :] + cw_ref[1:2, :] * ext_out[1:1 + tm, :]
               + cw_ref[2:3, :] * dcv)
        ext_out[tm:tm + HALO3, :] = ext_out[0:HALO3, :]
        dz_ref[:, SC_DIM:2 * SC_DIM] = (dcu * u).astype(BF16)
        dz_ref[:, 2 * SC_DIM:3 * SC_DIM] = (dcu * cg).astype(BF16)
        for j in range(3):
            s = HALO3 - 2 + j
            dcw_ref[j:j + 1, :] += jnp.sum(dcv * ext_in[s:s + tm, :], axis=0, keepdims=True)

    rrow = lambda w: pl.BlockSpec((tm, w), lambda i: (nt - 1 - i, 0))
    halo = pl.BlockSpec((hb, 3 * SC_DIM), lambda i: (jnp.maximum((nt - 1 - i) * (tm // hb) - 1, 0), 0))
    return pl.pallas_call(
        body, name="od_bwd1", grid=(nt,),
        in_specs=[rrow(D_MODEL), rrow(D_MODEL), _full((1, D_MODEL)), rrow(3 * SC_DIM), halo, _full((3, SC_DIM)),
                  _full((D_MODEL, SC_DIM))],
        out_specs=[rrow(D_MODEL), rrow(3 * SC_DIM), _full((1, D_MODEL)), _full((8, SC_DIM))],
        out_shape=[jax.ShapeDtypeStruct((t, D_MODEL), BF16), jax.ShapeDtypeStruct((t, 3 * SC_DIM), BF16),
                   jax.ShapeDtypeStruct((1, D_MODEL), F32), jax.ShapeDtypeStruct((8, SC_DIM), F32)],
        scratch_shapes=[pltpu.VMEM((HALO3 + tm, SC_DIM), F32), pltpu.VMEM((tm + HALO3, SC_DIM), F32)],
        compiler_params=_cp(("arbitrary",)),
    )(m, dxo, gpost, z, z, cw, wot)


def _ev_bwd1(m, dxo, gpost, wot):
    t = m.shape[0]
    tm = min(TM, t)

    def body(m_ref, dxo_ref, g_ref, wot_ref, dm_ref, da_ref, do_ref, dg_ref):
        dm, dg = _rms_bwd(m_ref[...], g_ref[...], dxo_ref[...])
        dmb = dm.astype(BF16)
        dm_ref[...] = dmb
        _acc_out(dg_ref, pl.program_id(0) == 0, dg)
        dao = jnp.dot(dmb, wot_ref[...], preferred_element_type=F32)
        da_ref[...] = dao[:, :A_CH]
        do_ref[...] = dao[:, A_CH:].astype(BF16)

    return pl.pallas_call(
        body, name="ev_bwd1", grid=(t // tm,),
        in_specs=[_rows(tm, D_MODEL), _rows(tm, D_MODEL), _full((1, D_MODEL)), _full((D_MODEL, A_CH + Q_DIM))],
        out_specs=[_rows(tm, D_MODEL), _rows(tm, A_CH), _rows(tm, Q_DIM), _full((1, D_MODEL))],
        out_shape=[jax.ShapeDtypeStruct((t, D_MODEL), BF16), jax.ShapeDtypeStruct((t, A_CH), F32),
                   jax.ShapeDtypeStruct((t, Q_DIM), BF16), jax.ShapeDtypeStruct((1, D_MODEL), F32)],
        compiler_params=_cp(("arbitrary",)),
    )(m, dxo, gpost, wot)


def _conf_bwd(da, cv, zag, conv_w, ln_g, ln_b):
    t = da.shape[0]
    tm = min(TM_BWD, t)
    nt = t // tm

    def body(da_ref, c_ref, z_ref, zh_ref, w_ref, g_ref, lb_ref, dz_ref, dw_ref, dv_ref, ext_in, ext_out):
        i = pl.program_id(0)
        r = nt - 1 - i

        @pl.when(i == 0)
        def _():
            ext_out[tm:tm + HALO31, :] = jnp.zeros((HALO31, A_CH), F32)
            dw_ref[...] = jnp.zeros((32, A_CH), F32)
            dv_ref[...] = jnp.zeros((8, A_CH), F32)

        x = c_ref[...]
        mu = jnp.mean(x, axis=-1, keepdims=True)
        xc = x - mu
        rstd = lax.rsqrt(jnp.mean(xc * xc, axis=-1, keepdims=True) + LN_EPS)
        xh = xc * rstd
        ln = xh * g_ref[...] + lb_ref[...]
        sl = jax.nn.sigmoid(ln)
        dln = da_ref[...] * (sl * (1.0 + ln * (1.0 - sl)))
        dxh = dln * g_ref[...]
        dc = rstd * (dxh - jnp.mean(dxh, axis=-1, keepdims=True) - xh * jnp.mean(dxh * xh, axis=-1, keepdims=True))
        dv_ref[0:1, :] += jnp.sum(dc, axis=0, keepdims=True)
        dv_ref[1:2, :] += jnp.sum(dln * xh, axis=0, keepdims=True)
        dv_ref[2:3, :] += jnp.sum(dln, axis=0, keepdims=True)

        ext_out[0:tm, :] = dc
        dglu = jnp.zeros((tm, A_CH), F32)
        for j in range(A_CONV):
            s = A_CONV - 1 - j
            dglu = dglu + w_ref[j:j + 1, :] * ext_out[s:s + tm, :]
        ext_out[tm:tm + HALO31, :] = ext_out[0:HALO31, :]

        ext_in[0:HALO31, :] = jnp.where(r > 0, _glu(zh_ref[...]), 0.0)
        z = z_ref[...].astype(F32)
        al, sg = z[:, :A_CH], jax.nn.sigmoid(z[:, A_CH:])
        ext_in[HALO31:HALO31 + tm, :] = al * sg
        for j in range(A_CONV):
            s = HALO31 - (A_CONV - 1) + j
            dw_ref[j:j + 1, :] += jnp.sum(dc * ext_in[s:s + tm, :], axis=0, keepdims=True)
        dz_ref[:, 0:A_CH] = (dglu * sg).astype(BF16)
        dz_ref[:, A_CH:2 * A_CH] = (dglu * al * sg * (1.0 - sg)).astype(BF16)

    rrow = lambda w: pl.BlockSpec((tm, w), lambda i: (nt - 1 - i, 0))
    halo = pl.BlockSpec((HALO31, 2 * A_CH), lambda i: (jnp.maximum((nt - 1 - i) * (tm // HALO31) - 1, 0), 0))
    return pl.pallas_call(
        body, name="conf_bwd", grid=(nt,),
        in_specs=[rrow(A_CH), rrow(A_CH), rrow(2 * A_CH), halo, _full((32, A_CH)), _full((1, A_CH)),
                  _full((1, A_CH))],
        out_specs=[rrow(2 * A_CH), _full((32, A_CH)), _full((8, A_CH))],
        out_shape=[jax.ShapeDtypeStruct((t, 2 * A_CH), BF16), jax.ShapeDtypeStruct((32, A_CH), F32),
                   jax.ShapeDtypeStruct((8, A_CH), F32)],
        scratch_shapes=[pltpu.VMEM((HALO31 + tm, A_CH), F32), pltpu.VMEM((tm + HALO31, A_CH), F32)],
        compiler_params=_cp(("arbitrary",)),
    )(da, cv, zag, zag, conv_w, ln_g, ln_b)


def _attn_bwd(qh, kh, vh, doh, sinks):
    t = qh.shape[1]
    nb = t // BLOCK

    def body(s_ref, q_ref, kc_ref, kp_ref, vc_ref, vp_ref, do_ref, dq_ref, dk_ref, dv_ref, ds_ref, dkc, dvc):
        i = pl.program_id(0)
        r = nb - 1 - i

        @pl.when(i == 0)
        def _():
            dkc[...] = jnp.zeros_like(dkc)
            dvc[...] = jnp.zeros_like(dvc)
            ds_ref[...] = jnp.zeros_like(ds_ref)

        mask = _attn_mask(r == 0)
        lane = lax.broadcasted_iota(jnp.int32, (1, N_Q_HEADS), 1)
        dsv = jnp.zeros((1, N_Q_HEADS), F32)
        for h in range(N_KV_HEADS):
            q4 = q_ref[GROUP * h:GROUP * (h + 1)].reshape(GROUP * BLOCK, HEAD_DIM)
            do4 = do_ref[GROUP * h:GROUP * (h + 1)].reshape(GROUP * BLOCK, HEAD_DIM)
            k2 = jnp.concatenate([kp_ref[h], kc_ref[h]], axis=0)
            v2 = jnp.concatenate([vp_ref[h], vc_ref[h]], axis=0)
            pn, ps = _attn_probs(q4, k2, mask, _sink_rows(s_ref, h))
            dp = lax.dot_general(do4, v2, _CONTRACT_LAST, preferred_element_type=F32)
            dl = jnp.sum(pn * dp, axis=-1, keepdims=True)
            dsb = (pn * (dp - dl)).astype(BF16)
            dq4 = jnp.dot(dsb, k2, preferred_element_type=F32) * SCALE
            dq_ref[GROUP * h:GROUP * (h + 1)] = dq4.reshape(GROUP, BLOCK, HEAD_DIM).astype(BF16)
            dk2 = lax.dot_general(dsb, q4, _CONTRACT_FIRST, preferred_element_type=F32) * SCALE
            dv2 = lax.dot_general(pn.astype(BF16), do4, _CONTRACT_FIRST, preferred_element_type=F32)
            dk_ref[h] = dk2[BLOCK:, :] + dkc[h]
            dv_ref[h] = dv2[BLOCK:, :] + dvc[h]
            dkc[h] = dk2[:BLOCK, :]
            dvc[h] = dv2[:BLOCK, :]
            srow = -ps * dl
            for g in range(GROUP):
                dsv = dsv + jnp.where(lane == GROUP * h + g, jnp.sum(srow[BLOCK * g:BLOCK * (g + 1), :]), 0.0)
        ds_ref[...] += dsv

    cur = lambda n: pl.BlockSpec((n, BLOCK, HEAD_DIM), lambda i: (0, nb - 1 - i, 0))
    prev = lambda n: pl.BlockSpec((n, BLOCK, HEAD_DIM), lambda i: (0, jnp.maximum(nb - 2 - i, 0), 0))
    return pl.pallas_call(
        body, name="attn_bwd", grid=(nb,),
        in_specs=[pl.BlockSpec(memory_space=pltpu.SMEM), cur(N_Q_HEADS), cur(N_KV_HEADS), prev(N_KV_HEADS),
                  cur(N_KV_HEADS), prev(N_KV_HEADS), cur(N_Q_HEADS)],
        out_specs=[cur(N_Q_HEADS), cur(N_KV_HEADS), cur(N_KV_HEADS), _full((1, N_Q_HEADS))],
        out_shape=[jax.ShapeDtypeStruct((N_Q_HEADS, t, HEAD_DIM), BF16),
                   jax.ShapeDtypeStruct((N_KV_HEADS, t, HEAD_DIM), F32),
                   jax.ShapeDtypeStruct((N_KV_HEADS, t, HEAD_DIM), F32), jax.ShapeDtypeStruct((1, N_Q_HEADS), F32)],
        scratch_shapes=[pltpu.VMEM((N_KV_HEADS, BLOCK, HEAD_DIM), F32), pltpu.VMEM((N_KV_HEADS, BLOCK, HEAD_DIM), F32)],
        compiler_params=_cp(("arbitrary",)),
    )(sinks, qh, kh, kh, vh, vh, doh)


def _ev_dz(dzag, dq, dk, dv, rc, rsa, rsb):
    t = dzag.shape[0]
    tm = min(TM, t)

    def body(dzag_ref, dq_ref, dk_ref, dv_ref, c_ref, sa_ref, sb_ref, dz_ref):
        c, sa, sb = c_ref[...], sa_ref[...], sb_ref[...]
        dz_ref[:, 0:2 * A_CH] = dzag_ref[...]
        q0 = 2 * A_CH
        for j in range(Q_DIM // 128):
            d = dq_ref[:, 128 * j:128 * (j + 1)].astype(F32)
            dz_ref[:, q0 + 128 * j:q0 + 128 * (j + 1)] = _rope_bwd(d, c, sa, sb).astype(BF16)
        k0 = q0 + Q_DIM
        dz_ref[:, k0:k0 + KV_DIM] = _rope_bwd(dk_ref[...], c, sa, sb).astype(BF16)
        dz_ref[:, k0 + KV_DIM:k0 + 2 * KV_DIM] = dv_ref[...].astype(BF16)

    return pl.pallas_call(
        body, name="ev_dz", grid=(t // tm,),
        in_specs=[_rows(tm, 2 * A_CH), _rows(tm, Q_DIM), _rows(tm, KV_DIM), _rows(tm, KV_DIM),
                  _rows(tm, 128), _rows(tm, 128), _rows(tm, 128)],
        out_specs=_rows(tm, EVEN_IN),
        out_shape=jax.ShapeDtypeStruct((t, EVEN_IN), BF16),
        compiler_params=_cp(("arbitrary",)),
    )(dzag, dq, dk, dv, rc, rsa, rsb)


def _to_heads(a, n):
    t = a.shape[0]
    return a.reshape(t, n, HEAD_DIM).transpose(1, 0, 2)


def _from_heads(a):
    n, t, _ = a.shape
    return a.transpose(1, 0, 2).reshape(t, n * HEAD_DIM)


def _ff_chunked(a):
    nc = D_FF // FF_CH
    parts = []
    for c in range(nc):
        parts += [a[..., c * FF_CH:(c + 1) * FF_CH], a[..., D_FF + c * FF_CH:D_FF + (c + 1) * FF_CH]]
    return jnp.concatenate(parts, axis=-1)


def _ff_unchunked(a):
    nc = D_FF // FF_CH
    gs = [a[..., 2 * c * FF_CH:(2 * c + 1) * FF_CH] for c in range(nc)]
    vs = [a[..., (2 * c + 1) * FF_CH:(2 * c + 2) * FF_CH] for c in range(nc)]
    return jnp.concatenate(gs + vs, axis=-1)


def _local_step(x, positions, target, p):
    row = lambda a: a.reshape(1, -1)
    rc, rsa, rsb = _rope_tables(positions)
    conv31 = jnp.pad(p["ev_a_conv_w"][0], ((0, 1), (0, 0)))
    cw_ffn = [_ff_chunked(p["ffn_conv_w"][i]) for i in range(2)]
    wup = [_ff_chunked(p["ffn_w_up"][i]) for i in range(2)]
    wd = [p["ffn_w_down"][i] for i in range(2)]
    sinks = p["ev_sinks"][0]
    g = {}

    h0, zag, q, k, v = _ev_in(x, row(p["mix_norm_pre"][0]), p["ev_w_in"][0], rc, rsa, rsb)
    cv, a = _conf_fwd(zag, conv31, p["ev_a_conv_b"], p["ev_a_ln_g"], p["ev_a_ln_b"])
    qh, kh, vh = _to_heads(q, N_Q_HEADS), _to_heads(k, N_KV_HEADS), _to_heads(v, N_KV_HEADS)
    o = _from_heads(_attn_fwd(qh, kh, vh, sinks))
    wo = p["ev_w_out"][0]
    m0, x1 = _out_post([a, o], [wo[:A_CH], wo[A_CH:]], x, row(p["mix_norm_post"][0]))
    h1, up0, f0, x2 = _ffn_fwd(x1, row(p["ffn_norm_pre"][0]), wup[0], cw_ffn[0], wd[0], row(p["ffn_norm_post"][0]))
    h2, z, y, m1, x3 = _od_fwd(x2, row(p["mix_norm_pre"][1]), p["od_w_in"][0], p["od_conv_w"][0], p["od_w_out"][0],
                               row(p["mix_norm_post"][1]))
    h3, up1, f1, x4 = _ffn_fwd(x3, row(p["ffn_norm_pre"][1]), wup[1], cw_ffn[1], wd[1], row(p["ffn_norm_post"][1]))
    dx, lpart = _loss_grad(x4, target)

    def ffn_back(i, f, dxo, up, h, x_in):
        df, act, dup, dgpost, dcw = _ffn_bwd1(f, dxo, row(p["ffn_norm_post"][i]), up, cw_ffn[i], wd[i].T)
        dwd = _dw(act, df, FF_CH, D_MODEL)
        dwup = _dw(h, dup, D_MODEL, FF_CH)
        dx_in, dgpre = _dz_wt_rms_bwd(dup, wup[i].T, x_in, row(p["ffn_norm_pre"][i]), dxo, FF_CH)
        dcw = jnp.concatenate([dcw[c, 0:3] for c in range(dcw.shape[0])], axis=-1)
        return dx_in, dgpost, dgpre, dwd, _ff_unchunked(dwup), _ff_unchunked(dcw)

    dx, dgfpost1, dgfpre1, dwd1, dwup1, dcw1 = ffn_back(1, f1, dx, up1, h3, x3)

    dm1, dz, dgpost1, dcw_od = _od_bwd1(m1, dx, row(p["mix_norm_post"][1]), z, p["od_conv_w"][0], p["od_w_out"][0].T)
    g["od_w_out"] = _dw(y, dm1, SC_DIM, D_MODEL)[None]
    g["od_w_in"] = _dw(h2, dz, D_MODEL, 3 * SC_DIM // 2)[None]
    g["od_conv_w"] = dcw_od[None, 0:3]
    dx, dgpre1 = _dz_wt_rms_bwd(dz, p["od_w_in"][0].T, x2, row(p["mix_norm_pre"][1]), dx, 3 * SC_DIM // 2)

    dx, dgfpost0, dgfpre0, dwd0, dwup0, dcw0 = ffn_back(0, f0, dx, up0, h1, x1)

    dm0, da, do, dgpost0 = _ev_bwd1(m0, dx, row(p["mix_norm_post"][0]), wo.T)
    g["ev_w_out"] = jnp.concatenate([_dw(a, dm0, A_CH, D_MODEL), _dw(o, dm0, Q_DIM, D_MODEL)], axis=0)[None]
    dzag, dcw31, dvec = _conf_bwd(da, cv, zag, conv31, p["ev_a_ln_g"], p["ev_a_ln_b"])
    dqh, dkh, dvh, dsinks = _attn_bwd(qh, kh, vh, _to_heads(do, N_Q_HEADS), sinks)
    dz0 = _ev_dz(dzag, _from_heads(dqh), _from_heads(dkh), _from_heads(dvh), rc, rsa, rsb)
    g["ev_w_in"] = _dw(h0, dz0, D_MODEL, EVEN_IN // 2)[None]
    dx, dgpre0 = _dz_wt_rms_bwd(dz0, p["ev_w_in"][0].T, x, row(p["mix_norm_pre"][0]), dx, EVEN_IN)

    g["mix_norm_pre"] = jnp.concatenate([dgpre0, dgpre1], axis=0)
    g["mix_norm_post"] = jnp.concatenate([dgpost0, dgpost1], axis=0)
    g["ffn_norm_pre"] = jnp.concatenate([dgfpre0, dgfpre1], axis=0)
    g["ffn_norm_post"] = jnp.concatenate([dgfpost0, dgfpost1], axis=0)
    g["ev_a_conv_w"] = dcw31[None, 0:A_CONV]
    g["ev_a_conv_b"], g["ev_a_ln_g"], g["ev_a_ln_b"] = dvec[0:1], dvec[1:2], dvec[2:3]
    g["ev_sinks"] = dsinks
    g["ffn_w_up"] = jnp.stack([dwup0, dwup1])
    g["ffn_conv_w"] = jnp.stack([dcw0, dcw1])
    g["ffn_w_down"] = jnp.stack([dwd0, dwd1])
    return lpart[0, 0], dx, g


MESH = pl.DeviceIdType.MESH
_ANY = pl.BlockSpec(memory_space=pl.ANY)


def _all_gather(shard, name):
    m_per, n = shard.shape

    def body(x_ref, out_ref, send_sems, recv_sems, local_sem):
        x, y, c = lax.axis_index("x"), lax.axis_index("y"), lax.axis_index("c")
        me, sibling = (x, y, c), (x, y, 1 - c)
        chips = [(1 - x, y), (x, 1 - y), (1 - x, 1 - y)]

        def rows(px, py, pc):
            return out_ref.at[pl.ds((4 * px + 2 * py + pc) * m_per, m_per), :]

        def copy(k, block, to, src=None):
            return pltpu.make_async_remote_copy(
                src_ref=rows(*block) if src is None else src, dst_ref=rows(*block),
                send_sem=send_sems.at[k], recv_sem=recv_sems.at[k], device_id=to, device_id_type=MESH)

        mine = pltpu.make_async_copy(x_ref, rows(*me), local_sem)
        mine.start()
        first = [copy(0, me, sibling, src=x_ref)]
        first += [copy(1 + j, me, (*chip, c), src=x_ref) for j, chip in enumerate(chips)]
        for cp in first:
            cp.start()
        passed = [copy(4 + j, (*chip, c), sibling) for j, chip in enumerate(chips)]
        for j, chip in enumerate(chips):
            copy(1 + j, (*chip, c), me).wait_recv()
            passed[j].start()
        copy(0, sibling, me).wait_recv()
        for j, chip in enumerate(chips):
            copy(4 + j, (*chip, 1 - c), me).wait_recv()
        for cp in first + passed:
            cp.wait_send()
        mine.wait()

    return pl.pallas_call(
        body, name=name, out_shape=jax.ShapeDtypeStruct((N_DEV * m_per, n), shard.dtype),
        in_specs=[_ANY], out_specs=_ANY,
        scratch_shapes=[pltpu.SemaphoreType.DMA((7,)), pltpu.SemaphoreType.DMA((7,)), pltpu.SemaphoreType.DMA],
    )(shard)


def _rs_d2d(gp):
    _, r, l = gp.shape

    def body(g_ref, recv_ref, send_sems, recv_sems):
        x, y, c = lax.axis_index("x"), lax.axis_index("y"), lax.axis_index("c")
        copies = [pltpu.make_async_remote_copy(
            src_ref=g_ref.at[2 * q + (1 - c)], dst_ref=recv_ref.at[q], send_sem=send_sems.at[q],
            recv_sem=recv_sems.at[q], device_id=(x, y, 1 - c), device_id_type=MESH) for q in range(4)]
        for cp in copies:
            cp.start()
        for cp in copies:
            cp.wait()

    return pl.pallas_call(
        body, name="rs_d2d", out_shape=jax.ShapeDtypeStruct((4, r, l), gp.dtype), in_specs=[_ANY], out_specs=_ANY,
        scratch_shapes=[pltpu.SemaphoreType.DMA((4,)), pltpu.SemaphoreType.DMA((4,))],
    )(gp)


def _rs_ici(pp):
    _, r, l = pp.shape

    def body(p_ref, recv_ref, send_sems, recv_sems):
        x, y, c = lax.axis_index("x"), lax.axis_index("y"), lax.axis_index("c")
        chips = [(1 - x, y), (x, 1 - y), (1 - x, 1 - y)]
        copies = [pltpu.make_async_remote_copy(
            src_ref=p_ref.at[2 * px + py], dst_ref=recv_ref.at[j], send_sem=send_sems.at[j],
            recv_sem=recv_sems.at[j], device_id=(px, py, c), device_id_type=MESH) for j, (px, py) in enumerate(chips)]
        for cp in copies:
            cp.start()
        for cp in copies:
            cp.wait()

    return pl.pallas_call(
        body, name="rs_ici", out_shape=jax.ShapeDtypeStruct((3, r, l), pp.dtype), in_specs=[_ANY], out_specs=_ANY,
        scratch_shapes=[pltpu.SemaphoreType.DMA((3,)), pltpu.SemaphoreType.DMA((3,))],
    )(pp)


def _row_tile(rows, cap):
    best = rows
    for d in range(8, min(rows, cap) + 1, 8):
        if rows % d == 0:
            best = d
    return best if best <= cap else rows


def _rs_add(gp, recv, core):
    _, r, l = gp.shape
    tr = _row_tile(r, 768)

    def body(c_ref, g_ref, r_ref, o_ref):
        o_ref[...] = g_ref[...] + r_ref[...]

    return pl.pallas_call(
        body, name="rs_add", out_shape=jax.ShapeDtypeStruct((4, r, l), gp.dtype),
        grid_spec=pltpu.PrefetchScalarGridSpec(
            num_scalar_prefetch=1, grid=(4, r // tr),
            in_specs=[pl.BlockSpec((1, tr, l), lambda q, j, c: (2 * q + c[0], j, 0)),
                      pl.BlockSpec((1, tr, l), lambda q, j, c: (q, j, 0))],
            out_specs=pl.BlockSpec((1, tr, l), lambda q, j, c: (q, j, 0))),
        compiler_params=_cp(("arbitrary", "arbitrary")),
    )(core, gp, recv)


def _rs_final(pp, recv, chip):
    _, r, l = pp.shape
    tr = _row_tile(r, 768)

    def body(q_ref, p_ref, r_ref, o_ref):
        o_ref[...] = ((p_ref[0] + r_ref[0]) + r_ref[1]) + r_ref[2]

    return pl.pallas_call(
        body, name="rs_final", out_shape=jax.ShapeDtypeStruct((r, l), pp.dtype),
        grid_spec=pltpu.PrefetchScalarGridSpec(
            num_scalar_prefetch=1, grid=(r // tr,),
            in_specs=[pl.BlockSpec((1, tr, l), lambda j, q: (q[0], j, 0)),
                      pl.BlockSpec((3, tr, l), lambda j, q: (0, j, 0))],
            out_specs=pl.BlockSpec((tr, l), lambda j, q: (j, 0))),
        compiler_params=_cp(("arbitrary",)),
    )(chip, pp, recv)


def _sum_blocks(a, nblk):
    m = a.shape[0] // nblk
    n = a.shape[1]

    def body(a_ref, o_ref):
        acc = a_ref[0]
        for j in range(1, nblk):
            acc = acc + a_ref[j]
        o_ref[...] = acc

    return pl.pallas_call(
        body, name="sum_blocks", out_shape=jax.ShapeDtypeStruct((m, n), a.dtype),
        in_specs=[_full((nblk, m, n))], out_specs=_full((m, n)),
    )(a.reshape(nblk, m, n))


def _adamw(w, g, m, v):
    rows, c = w.shape
    tr = _row_tile(rows, 512)
    bc1 = 1.0 - ADAM_B1 ** ADAM_STEP
    bc2 = 1.0 - ADAM_B2 ** ADAM_STEP

    def body(w_ref, g_ref, m_ref, v_ref, d_ref, mo_ref, vo_ref):
        gv = g_ref[...]
        mn = ADAM_B1 * m_ref[...] + (1.0 - ADAM_B1) * gv
        vn = ADAM_B2 * v_ref[...] + (1.0 - ADAM_B2) * (gv * gv)
        mo_ref[...] = mn
        vo_ref[...] = vn
        d_ref[...] = -ADAM_LR * ((mn / bc1) / (jnp.sqrt(vn / bc2) + ADAM_EPS) + ADAM_WD * w_ref[...])

    spec = pl.BlockSpec((tr, c), lambda i: (i, 0))
    return pl.pallas_call(
        body, name="adamw", grid=(rows // tr,), in_specs=[spec] * 4, out_specs=[spec] * 3,
        out_shape=[jax.ShapeDtypeStruct((rows, c), F32)] * 3, compiler_params=_cp(("arbitrary",)),
    )(w, g, m, v)


WEIGHTS = ["mix_norm_pre", "mix_norm_post", "ffn_norm_pre", "ffn_norm_post", "ev_w_in", "ev_a_conv_w", "ev_a_conv_b",
           "ev_a_ln_g", "ev_a_ln_b", "ev_sinks", "ev_w_out", "od_w_in", "od_conv_w", "od_w_out", "ffn_w_up",
           "ffn_conv_w", "ffn_w_down"]
BIG = [("ev_w_in", "col"), ("ev_w_out", "row"), ("od_w_in", "col"), ("od_w_out", "row"), ("ffn_w_up", "col"),
       ("ffn_w_down", "row")]
SMALL_REPL = ["mix_norm_pre", "mix_norm_post", "ffn_norm_pre", "ffn_norm_post", "ev_a_conv_b", "ev_a_ln_g",
              "ev_a_ln_b", "ev_sinks"]
SMALL_SHARDED = ["ev_a_conv_w", "od_conv_w", "ffn_conv_w"]


def _pad_rows(flat, rows):
    return jnp.pad(flat, (0, rows * LANES - flat.shape[0])).reshape(rows, LANES)


def _pack(arrs, rows):
    return _pad_rows(jnp.concatenate([a.reshape(-1) for a in arrs]), rows)


def _unpack(packed, shapes):
    flat, out, off = packed.reshape(-1), [], 0
    for s in shapes:
        n = 1
        for d in s:
            n *= d
        out.append(flat[off:off + n].reshape(s))
        off += n
    return out


def _big_rows(local):
    return [local[n].size // LANES for n, _ in BIG]


def _unpack_big_gathered(gat, local):
    out, off = {}, 0
    for (n, kind), rows in zip(BIG, _big_rows(local)):
        l, a, b = local[n].shape
        blk = gat[:, off:off + rows].reshape(N_DEV, l, a, b)
        if kind == "col":
            out[n] = blk.transpose(1, 2, 0, 3).reshape(l, a, N_DEV * b)
        else:
            out[n] = blk.transpose(1, 0, 2, 3).reshape(l, N_DEV * a, b)
        off += rows
    return out


def _pack_big_grads(g, local):
    parts = []
    for n, kind in BIG:
        l, a, b = local[n].shape
        if kind == "col":
            blk = g[n].reshape(l, a, N_DEV, b).transpose(2, 0, 1, 3)
        else:
            blk = g[n].reshape(l, N_DEV, a, b).transpose(1, 0, 2, 3)
        parts.append(blk.reshape(N_DEV, -1, LANES))
    return jnp.concatenate(parts, axis=1)


def kernel(x, positions, mix_norm_pre, mix_norm_post, ffn_norm_pre, ffn_norm_post, ev_w_in, ev_a_conv_w, ev_a_conv_b, ev_a_ln_g, ev_a_ln_b, ev_sinks, ev_w_out, od_w_in, od_conv_w, od_w_out, ffn_w_up, ffn_conv_w, ffn_w_down, loss_target, m_mix_norm_pre, m_mix_norm_post, m_ffn_norm_pre, m_ffn_norm_post, m_ev_w_in, m_ev_a_conv_w, m_ev_a_conv_b, m_ev_a_ln_g, m_ev_a_ln_b, m_ev_sinks, m_ev_w_out, m_od_w_in, m_od_conv_w, m_od_w_out, m_ffn_w_up, m_ffn_conv_w, m_ffn_w_down, v_mix_norm_pre, v_mix_norm_post, v_ffn_norm_pre, v_ffn_norm_post, v_ev_w_in, v_ev_a_conv_w, v_ev_a_conv_b, v_ev_a_ln_g, v_ev_a_ln_b, v_ev_sinks, v_ev_w_out, v_od_w_in, v_od_conv_w, v_od_w_out, v_ffn_w_up, v_ffn_conv_w, v_ffn_w_down):
    w = dict(zip(WEIGHTS, (mix_norm_pre, mix_norm_post, ffn_norm_pre, ffn_norm_post, ev_w_in, ev_a_conv_w, ev_a_conv_b,
                           ev_a_ln_g, ev_a_ln_b, ev_sinks, ev_w_out, od_w_in, od_conv_w, od_w_out, ffn_w_up, ffn_conv_w,
                           ffn_w_down)))
    mom = dict(zip(WEIGHTS, (m_mix_norm_pre, m_mix_norm_post, m_ffn_norm_pre, m_ffn_norm_post, m_ev_w_in, m_ev_a_conv_w,
                             m_ev_a_conv_b, m_ev_a_ln_g, m_ev_a_ln_b, m_ev_sinks, m_ev_w_out, m_od_w_in, m_od_conv_w,
                             m_od_w_out, m_ffn_w_up, m_ffn_conv_w, m_ffn_w_down)))
    var = dict(zip(WEIGHTS, (v_mix_norm_pre, v_mix_norm_post, v_ffn_norm_pre, v_ffn_norm_post, v_ev_w_in, v_ev_a_conv_w,
                             v_ev_a_conv_b, v_ev_a_ln_g, v_ev_a_ln_b, v_ev_sinks, v_ev_w_out, v_od_w_in, v_od_conv_w,
                             v_od_w_out, v_ffn_w_up, v_ffn_conv_w, v_ffn_w_down)))
    ix, iy, ic = lax.axis_index("x"), lax.axis_index("y"), lax.axis_index("c")
    dev = 4 * ix + 2 * iy + ic

    big_rows = sum(_big_rows(w))
    big_shard = jnp.concatenate([w[n].astype(BF16).reshape(-1, LANES) for n, _ in BIG], axis=0)
    full = _unpack_big_gathered(_all_gather(big_shard, "gather_big").reshape(N_DEV, big_rows, LANES), w)
    small_shapes = [w[n].shape for n in SMALL_SHARDED]
    small_gat = _all_gather(_pack([w[n] for n in SMALL_SHARDED], 8), "gather_conv").reshape(N_DEV, 8, LANES)
    per_dev = [_unpack(small_gat[d], small_shapes) for d in range(N_DEV)]
    for k, n in enumerate(SMALL_SHARDED):
        full[n] = jnp.concatenate([per_dev[d][k] for d in range(N_DEV)], axis=-1)
    for n in SMALL_REPL:
        full[n] = w[n]

    lpart, grad_x, g = _local_step(x[0], positions[0], loss_target[0], full)
    loss = lax.psum(lpart, ("x", "y", "c"))

    gp = _pack_big_grads(g, w)
    core = jnp.reshape(ic, (1,)).astype(jnp.int32)
    chip = jnp.reshape(2 * ix + iy, (1,)).astype(jnp.int32)
    pp = _rs_add(gp, _rs_d2d(gp), core)
    g_big = _rs_final(pp, _rs_ici(pp), chip)
    grads, off = {}, 0
    for (n, _), rows in zip(BIG, _big_rows(w)):
        grads[n] = g_big[off:off + rows].reshape(w[n].shape)
        off += rows

    small_names = SMALL_REPL + SMALL_SHARDED
    small_full_shapes = [full[n].shape for n in small_names]
    s_rows = 64
    s_all = _sum_blocks(_all_gather(_pack([g[n] for n in small_names], s_rows), "gather_small_grads"), N_DEV)
    for n, a in zip(small_names, _unpack(s_all, small_full_shapes)):
        if n in SMALL_SHARDED:
            width = w[n].shape[-1]
            a = lax.dynamic_slice_in_dim(a, dev * width, width, axis=a.ndim - 1)
        grads[n] = a

    delta, new_m, new_v = {}, {}, {}
    for n, _ in BIG:
        shp = w[n].shape
        two = lambda a: a.reshape(-1, shp[-1])
        d, mn, vn = _adamw(two(w[n]), two(grads[n]), two(mom[n]), two(var[n]))
        delta[n], new_m[n], new_v[n] = d.reshape(shp), mn.reshape(shp), vn.reshape(shp)
    loc_shapes = [w[n].shape for n in small_names]
    pk = lambda dct: _pack([dct[n] for n in small_names], 16)
    d, mn, vn = _adamw(pk(w), pk(grads), pk(mom), pk(var))
    for dst, packed in ((delta, d), (new_m, mn), (new_v, vn)):
        for n, a in zip(small_names, _unpack(packed, loc_shapes)):
            dst[n] = a

    return (loss, grad_x[None], *[grads[n] for n in WEIGHTS], *[delta[n] for n in WEIGHTS],
            *[new_m[n] for n in WEIGHTS], *[new_v[n] for n in WEIGHTS])
```

```python
import jax
import jax.numpy as jnp
from jax import lax
from jax.experimental import pallas as pl
from jax.experimental.pallas import tpu as pltpu

F32, BF16 = jnp.float32, jnp.bfloat16

D_MODEL = 1024
A_CH = 512
A_CONV = 31
Q_DIM = 512
KV_DIM = 128
HEAD_DIM = 64
N_Q_HEADS = 8
N_KV_HEADS = 2
GROUP = 4
BLOCK = 128
EVEN_IN = 1792
SC_DIM = 1024
D_FF = 2816
ROPE_THETA = 500000.0
ROPE_DIM = 16
RMS_EPS = 1e-6
LN_EPS = 1e-5
SCALE = HEAD_DIM ** -0.5
NEG = -1e30

ADAM_LR, ADAM_B1, ADAM_B2, ADAM_EPS, ADAM_WD, ADAM_STEP = 0.001, 0.9, 0.999, 1e-08, 0.01, 10

N_DEV = 8
FF_N = 2 * D_FF // N_DEV
LANES = 1024
HALO3 = 8
HALO31 = 32
VMEM_LIMIT = 56 * 1024 * 1024

TM = 512
TM_BWD = 256
TK_DW = 2048

_ANY = pl.BlockSpec(memory_space=pl.ANY)
_CONTRACT_LAST = (((1,), (1,)), ((), ()))
_CONTRACT_FIRST = (((0,), (0,)), ((), ()))


def _cp(sem, vmem=VMEM_LIMIT):
    return pltpu.CompilerParams(dimension_semantics=sem, vmem_limit_bytes=vmem)


def _full(shape):
    n = len(shape)
    return pl.BlockSpec(shape, lambda *_: (0,) * n)


def _rows(tm, n):
    return pl.BlockSpec((tm, n), lambda i, *_: (i, 0))


def _rsqrt_mean(x):
    return lax.rsqrt(jnp.mean(x * x, axis=-1, keepdims=True) + RMS_EPS)


def _rms_bwd(x, g, dy):
    r = _rsqrt_mean(x)
    xh = x * r
    dxh = dy * g
    dx = r * (dxh - xh * jnp.mean(dxh * xh, axis=-1, keepdims=True))
    return dx, jnp.sum(dy * xh, axis=0, keepdims=True)


def _acc_out(ref, first, val):
    @pl.when(first)
    def _():
        ref[...] = val

    @pl.when(jnp.logical_not(first))
    def _():
        ref[...] += val


def _rope_tables(positions):
    half = ROPE_DIM // 2
    inv_freq = ROPE_THETA ** (-(jnp.arange(half, dtype=F32) * 2.0 / ROPE_DIM))
    ang = positions.astype(F32)[:, None] * inv_freq
    cos, sin = jnp.cos(ang), jnp.sin(ang)
    t = positions.shape[0]
    one, zero = jnp.ones((t, HEAD_DIM - ROPE_DIM), F32), jnp.zeros((t, HEAD_DIM - ROPE_DIM), F32)
    z8 = jnp.zeros((t, half), F32)
    c = jnp.concatenate([cos, cos, one], axis=1)
    sa = jnp.concatenate([z8, sin, zero], axis=1)
    sb = jnp.concatenate([-sin, z8, zero], axis=1)
    return tuple(jnp.tile(a, (1, 2)) for a in (c, sa, sb))


def _rope(t, c, sa, sb):
    return t * c + pltpu.roll(t, 8, 1) * sa + pltpu.roll(t, 120, 1) * sb


def _rope_bwd(d, c, sa, sb):
    return d * c + pltpu.roll(d * sa, 120, 1) + pltpu.roll(d * sb, 8, 1)


def _ev_in(x, gpre, w_in, rc, rsa, rsb):
    t = x.shape[0]
    tm = min(TM, t)

    def body(x_ref, g_ref, w_ref, c_ref, sa_ref, sb_ref, h_ref, zag_ref, q_ref, k_ref, v_ref):
        xv = x_ref[...]
        h = (xv * _rsqrt_mean(xv) * g_ref[...]).astype(BF16)
        h_ref[...] = h
        z = jnp.dot(h, w_ref[...], preferred_element_type=F32)
        zag_ref[...] = z[:, :2 * A_CH].astype(BF16)
        c, sa, sb = c_ref[...], sa_ref[...], sb_ref[...]
        q0 = 2 * A_CH
        for j in range(Q_DIM // 128):
            q_ref[:, 128 * j:128 * (j + 1)] = _rope(z[:, q0 + 128 * j:q0 + 128 * (j + 1)], c, sa, sb).astype(BF16)
        k0 = q0 + Q_DIM
        k_ref[...] = _rope(z[:, k0:k0 + KV_DIM], c, sa, sb).astype(BF16)
        v_ref[...] = z[:, k0 + KV_DIM:k0 + 2 * KV_DIM].astype(BF16)

    return pl.pallas_call(
        body, name="ev_in", grid=(t // tm,),
        in_specs=[_rows(tm, D_MODEL), _full((1, D_MODEL)), _full((D_MODEL, EVEN_IN)),
                  _rows(tm, 128), _rows(tm, 128), _rows(tm, 128)],
        out_specs=[_rows(tm, D_MODEL), _rows(tm, 2 * A_CH), _rows(tm, Q_DIM), _rows(tm, KV_DIM), _rows(tm, KV_DIM)],
        out_shape=[jax.ShapeDtypeStruct((t, D_MODEL), BF16), jax.ShapeDtypeStruct((t, 2 * A_CH), BF16),
                   jax.ShapeDtypeStruct((t, Q_DIM), BF16), jax.ShapeDtypeStruct((t, KV_DIM), BF16),
                   jax.ShapeDtypeStruct((t, KV_DIM), BF16)],
        compiler_params=_cp(("arbitrary",)),
    )(x, gpre, w_in, rc, rsa, rsb)


def _glu(zag):
    z = zag.astype(F32)
    return z[:, :A_CH] * jax.nn.sigmoid(z[:, A_CH:])


def _conf_fwd(zag, conv_w, conv_b, ln_g, ln_b):
    t = zag.shape[0]
    tm = min(TM_BWD, t)

    def body(z_ref, w_ref, b_ref, g_ref, lb_ref, c_ref, a_ref, ext):
        i = pl.program_id(0)

        @pl.when(i == 0)
        def _():
            ext[0:HALO31, :] = jnp.zeros((HALO31, A_CH), F32)

        ext[HALO31:HALO31 + tm, :] = _glu(z_ref[...])
        acc = jnp.zeros((tm, A_CH), F32)
        for j in range(A_CONV):
            s = HALO31 - (A_CONV - 1) + j
            acc = acc + w_ref[j:j + 1, :] * ext[s:s + tm, :]
        ext[0:HALO31, :] = ext[tm:tm + HALO31, :]
        cv = acc + b_ref[...]
        c_ref[...] = cv
        mu = jnp.mean(cv, axis=-1, keepdims=True)
        xc = cv - mu
        ln = xc * lax.rsqrt(jnp.mean(xc * xc, axis=-1, keepdims=True) + LN_EPS) * g_ref[...] + lb_ref[...]
        a_ref[...] = (ln * jax.nn.sigmoid(ln)).astype(BF16)

    return pl.pallas_call(
        body, name="conf_fwd", grid=(t // tm,),
        in_specs=[_rows(tm, 2 * A_CH), _full((32, A_CH)), _full((1, A_CH)), _full((1, A_CH)), _full((1, A_CH))],
        out_specs=[_rows(tm, A_CH), _rows(tm, A_CH)],
        out_shape=[jax.ShapeDtypeStruct((t, A_CH), F32), jax.ShapeDtypeStruct((t, A_CH), BF16)],
        scratch_shapes=[pltpu.VMEM((HALO31 + tm, A_CH), F32)],
        compiler_params=_cp(("arbitrary",)),
    )(zag, conv_w, conv_b, ln_g, ln_b)


def _attn_mask(first_block):
    row = lax.broadcasted_iota(jnp.int32, (GROUP * BLOCK, 2 * BLOCK), 0) & (BLOCK - 1)
    col = lax.broadcasted_iota(jnp.int32, (GROUP * BLOCK, 2 * BLOCK), 1)
    diff = row + BLOCK - col
    return (diff >= 0) & (diff < BLOCK) & ((col >= BLOCK) | jnp.logical_not(first_block))


def _sink_rows(s_ref, h):
    grp = lax.broadcasted_iota(jnp.int32, (GROUP * BLOCK, 1), 0) >> 7
    out = jnp.full((GROUP * BLOCK, 1), s_ref[GROUP * h], F32)
    for g in range(1, GROUP):
        out = jnp.where(grp == g, s_ref[GROUP * h + g], out)
    return out


def _attn_probs(q4, k2, mask, sink):
    s = lax.dot_general(q4, k2, _CONTRACT_LAST, preferred_element_type=F32) * SCALE
    s = jnp.where(mask, s, NEG)
    m = jnp.maximum(jnp.max(s, axis=-1, keepdims=True), sink)
    p = jnp.exp(s - m)
    es = jnp.exp(sink - m)
    den = jnp.sum(p, axis=-1, keepdims=True) + es
    return p / den, es / den


def _q_heads(q, h):
    return jnp.concatenate([q[:, HEAD_DIM * (GROUP * h + g):HEAD_DIM * (GROUP * h + g + 1)] for g in range(GROUP)],
                           axis=0)


def _kv_head(prev, cur, h):
    return jnp.concatenate([prev[:, HEAD_DIM * h:HEAD_DIM * (h + 1)], cur[:, HEAD_DIM * h:HEAD_DIM * (h + 1)]], axis=0)


def _attn_fwd(q, k, v, sinks):
    t = q.shape[0]
    nb = t // BLOCK

    def body(s_ref, q_ref, kc_ref, kp_ref, vc_ref, vp_ref, o_ref):
        mask = _attn_mask(pl.program_id(0) == 0)
        qv, kc, kp, vc, vp = q_ref[...], kc_ref[...], kp_ref[...], vc_ref[...], vp_ref[...]
        for h in range(N_KV_HEADS):
            v2 = _kv_head(vp, vc, h)
            pn, _ = _attn_probs(_q_heads(qv, h), _kv_head(kp, kc, h), mask, _sink_rows(s_ref, h))
            o4 = jnp.dot(pn.astype(BF16), v2, preferred_element_type=F32).astype(BF16)
            for g in range(GROUP):
                c0 = HEAD_DIM * (GROUP * h + g)
                o_ref[:, c0:c0 + HEAD_DIM] = o4[BLOCK * g:BLOCK * (g + 1), :]

    cur = lambda n: pl.BlockSpec((BLOCK, n), lambda i: (i, 0))
    prev = lambda n: pl.BlockSpec((BLOCK, n), lambda i: (jnp.maximum(i - 1, 0), 0))
    return pl.pallas_call(
        body, name="attn_fwd", grid=(nb,),
        in_specs=[pl.BlockSpec(memory_space=pltpu.SMEM), cur(Q_DIM), cur(KV_DIM), prev(KV_DIM), cur(KV_DIM),
                  prev(KV_DIM)],
        out_specs=cur(Q_DIM),
        out_shape=jax.ShapeDtypeStruct((t, Q_DIM), BF16),
        compiler_params=_cp(("arbitrary",)),
    )(sinks, q, k, k, v, v)


def _out_post(lhs, ws, x_in, gpost):
    t = x_in.shape[0]
    tm = min(TM, t)
    n = len(lhs)

    def body(*refs):
        x_ref, g_ref, m_ref, xo_ref = refs[2 * n:]
        m = jnp.dot(refs[0][...], refs[n][...], preferred_element_type=F32)
        for j in range(1, n):
            m = m + jnp.dot(refs[j][...], refs[n + j][...], preferred_element_type=F32)
        m_ref[...] = m
        xo_ref[...] = x_ref[...] + m * _rsqrt_mean(m) * g_ref[...]

    return pl.pallas_call(
        body, name="out_post", grid=(t // tm,),
        in_specs=[_rows(tm, a.shape[1]) for a in lhs] + [_full(w.shape) for w in ws]
                 + [_rows(tm, D_MODEL), _full((1, D_MODEL))],
        out_specs=[_rows(tm, D_MODEL), _rows(tm, D_MODEL)],
        out_shape=[jax.ShapeDtypeStruct((t, D_MODEL), F32)] * 2,
        compiler_params=_cp(("arbitrary",)),
    )(*lhs, *ws, x_in, gpost)


def _conv3(w_ref, ext, tm):
    s = HALO3 - 2
    return (w_ref[0:1, :] * ext[s:s + tm, :] + w_ref[1:2, :] * ext[s + 1:s + 1 + tm, :]
            + w_ref[2:3, :] * ext[s + 2:s + 2 + tm, :])


def _ffn_fwd(x1, gpre, wup, layer, cw, wd, gpost):
    t = x1.shape[0]
    tm = min(TM, t)
    nc, n = wup.shape[1], wup.shape[4]

    def body(x_ref, gpre_ref, wup_ref, cw_ref, wd_ref, gpost_ref, h_ref, up_ref, f_ref, xo_ref, h_s, acc, ext, hal):
        i, c = pl.program_id(0), pl.program_id(1)

        @pl.when(c == 0)
        def _():
            xv = x_ref[...]
            h = (xv * _rsqrt_mean(xv) * gpre_ref[...]).astype(BF16)
            h_s[...] = h
            h_ref[...] = h

        @pl.when(i == 0)
        def _():
            hal[c] = jnp.zeros((2, HALO3, n), F32)

        u = []
        for gv in range(2):
            up = jnp.dot(h_s[...], wup_ref[gv, 0, 0], preferred_element_type=F32)
            up_ref[gv, 0] = up.astype(BF16)
            ext[gv, 0:HALO3, :] = hal[c, gv]
            ext[gv, HALO3:HALO3 + tm, :] = up
            hal[c, gv] = ext[gv, tm:tm + HALO3, :]
            s = HALO3 - 2
            u.append(cw_ref[gv, 0, 0:1, :] * ext[gv, s:s + tm, :] + cw_ref[gv, 0, 1:2, :] * ext[gv, s + 1:s + 1 + tm, :]
                     + cw_ref[gv, 0, 2:3, :] * up)
        act = (u[0] * jax.nn.sigmoid(u[0]) * u[1]).astype(BF16)
        part = jnp.dot(act, wd_ref[...], preferred_element_type=F32)

        @pl.when(c == 0)
        def _():
            acc[...] = part

        @pl.when(jnp.logical_and(c > 0, c < nc - 1))
        def _():
            acc[...] += part

        @pl.when(c == nc - 1)
        def _():
            f = acc[...] + part
            f_ref[...] = f
            xo_ref[...] = x_ref[...] + f * _rsqrt_mean(f) * gpost_ref[...]

    row = lambda w: pl.BlockSpec((tm, w), lambda i, c: (i, 0))
    one = _full((1, D_MODEL))
    return pl.pallas_call(
        body, name="ffn_fwd", grid=(t // tm, nc),
        in_specs=[row(D_MODEL), one, pl.BlockSpec((2, 1, 1, D_MODEL, n), lambda i, c: (0, c, layer, 0, 0)),
                  pl.BlockSpec((2, 1, 3, n), lambda i, c: (0, c, 0, 0)), pl.BlockSpec((n, D_MODEL), lambda i, c: (c, 0)),
                  one],
        out_specs=[row(D_MODEL), pl.BlockSpec((2, 1, tm, n), lambda i, c: (0, c, i, 0)), row(D_MODEL), row(D_MODEL)],
        out_shape=[jax.ShapeDtypeStruct((t, D_MODEL), BF16), jax.ShapeDtypeStruct((2, nc, t, n), BF16),
                   jax.ShapeDtypeStruct((t, D_MODEL), F32), jax.ShapeDtypeStruct((t, D_MODEL), F32)],
        scratch_shapes=[pltpu.VMEM((tm, D_MODEL), BF16), pltpu.VMEM((tm, D_MODEL), F32),
                        pltpu.VMEM((2, HALO3 + tm, n), F32), pltpu.VMEM((nc, 2, HALO3, n), F32)],
        compiler_params=_cp(("arbitrary", "arbitrary")),
    )(x1, gpre, wup, cw, wd, gpost)


def _od_fwd(x_in, gpre, w_in, cw, w_out, gpost):
    t = x_in.shape[0]
    tm = min(TM, t)
    ns, _, n = w_in.shape

    def body(x_ref, gpre_ref, w_ref, cw_ref, wo_ref, gpost_ref, h_ref, z_ref, y_ref, m_ref, xo_ref, z_s, ext):
        i = pl.program_id(0)
        xv = x_ref[...]
        h = (xv * _rsqrt_mean(xv) * gpre_ref[...]).astype(BF16)
        h_ref[...] = h
        for j in range(ns):
            z_s[:, n * j:n * (j + 1)] = jnp.dot(h, w_ref[j], preferred_element_type=F32)
        z_ref[...] = z_s[...].astype(BF16)

        @pl.when(i == 0)
        def _():
            ext[0:HALO3, :] = jnp.zeros((HALO3, SC_DIM), F32)

        ext[HALO3:HALO3 + tm, :] = z_s[:, SC_DIM:2 * SC_DIM] * z_s[:, 2 * SC_DIM:]
        y = (z_s[:, :SC_DIM] * _conv3(cw_ref, ext, tm)).astype(BF16)
        ext[0:HALO3, :] = ext[tm:tm + HALO3, :]
        y_ref[...] = y
        m = jnp.dot(y, wo_ref[...], preferred_element_type=F32)
        m_ref[...] = m
        xo_ref[...] = xv + m * _rsqrt_mean(m) * gpost_ref[...]

    return pl.pallas_call(
        body, name="od_fwd", grid=(t // tm,),
        in_specs=[_rows(tm, D_MODEL), _full((1, D_MODEL)), _full((ns, D_MODEL, n)), _full((3, SC_DIM)),
                  _full((SC_DIM, D_MODEL)), _full((1, D_MODEL))],
        out_specs=[_rows(tm, D_MODEL), _rows(tm, 3 * SC_DIM), _rows(tm, SC_DIM), _rows(tm, D_MODEL),
                   _rows(tm, D_MODEL)],
        out_shape=[jax.ShapeDtypeStruct((t, D_MODEL), BF16), jax.ShapeDtypeStruct((t, 3 * SC_DIM), BF16),
                   jax.ShapeDtypeStruct((t, SC_DIM), BF16), jax.ShapeDtypeStruct((t, D_MODEL), F32),
                   jax.ShapeDtypeStruct((t, D_MODEL), F32)],
        scratch_shapes=[pltpu.VMEM((tm, 3 * SC_DIM), F32), pltpu.VMEM((HALO3 + tm, SC_DIM), F32)],
        compiler_params=_cp(("arbitrary",)),
    )(x_in, gpre, w_in, cw, w_out, gpost)


def _loss_grad(y, target):
    t = y.shape[0]
    tm = min(TM, t)

    def body(y_ref, t_ref, dy_ref, l_ref):
        e = y_ref[...] - t_ref[...]
        dy_ref[...] = e * (1.0 / D_MODEL)
        part = jnp.zeros((1, 128), F32) + jnp.sum(e * e) * (0.5 / D_MODEL)
        _acc_out(l_ref, pl.program_id(0) == 0, part)

    return pl.pallas_call(
        body, name="loss_grad", grid=(t // tm,),
        in_specs=[_rows(tm, D_MODEL), _rows(tm, D_MODEL)],
        out_specs=[_rows(tm, D_MODEL), _full((1, 128))],
        out_shape=[jax.ShapeDtypeStruct((t, D_MODEL), F32), jax.ShapeDtypeStruct((1, 128), F32)],
        compiler_params=_cp(("arbitrary",)),
    )(y, target)


def _dw2d(a, b, bm, bn):
    t, m = a.shape
    n = b.shape[1]
    tk = min(TK_DW, t)

    def body(a_ref, b_ref, o_ref):
        part = lax.dot_general(a_ref[...], b_ref[...], _CONTRACT_FIRST, preferred_element_type=F32)
        _acc_out(o_ref, pl.program_id(2) == 0, part)

    return pl.pallas_call(
        body, name="dw2d", grid=(m // bm, n // bn, t // tk),
        in_specs=[pl.BlockSpec((tk, bm), lambda i, j, k: (k, i)), pl.BlockSpec((tk, bn), lambda i, j, k: (k, j))],
        out_specs=pl.BlockSpec((bm, bn), lambda i, j, k: (i, j)),
        out_shape=jax.ShapeDtypeStruct((m, n), F32),
        compiler_params=_cp(("arbitrary", "arbitrary", "arbitrary")),
    )(a, b)


def _dw_cols(a, b, n_blk):
    t, m = a.shape
    s = b.shape[1] // n_blk
    tk = min(TK_DW, t)

    def body(a_ref, b_ref, o_ref):
        part = lax.dot_general(a_ref[...], b_ref[...], _CONTRACT_FIRST, preferred_element_type=F32)
        _acc_out(o_ref.at[0], pl.program_id(1) == 0, part)

    return pl.pallas_call(
        body, name="dw_cols", grid=(s, t // tk),
        in_specs=[pl.BlockSpec((tk, m), lambda j, k: (k, 0)), pl.BlockSpec((tk, n_blk), lambda j, k: (k, j))],
        out_specs=pl.BlockSpec((1, m, n_blk), lambda j, k: (j, 0, 0)),
        out_shape=jax.ShapeDtypeStruct((s, m, n_blk), F32),
        compiler_params=_cp(("arbitrary", "arbitrary")),
    )(a, b)


def _dw_up(h, dup, layer, buf):
    t, m = h.shape
    s, _, n = dup.shape
    tk = min(TK_DW, t)

    def body(*refs):
        a_ref, b_ref, o_ref = refs[0], refs[1], refs[-1]
        part = lax.dot_general(a_ref[...], b_ref[0], _CONTRACT_FIRST, preferred_element_type=F32)
        _acc_out(o_ref.at[0, 0], pl.program_id(1) == 0, part)

    ins = [h, dup] + ([] if buf is None else [buf])
    return pl.pallas_call(
        body, name="dw_up", grid=(s, t // tk),
        in_specs=[pl.BlockSpec((tk, m), lambda j, k: (k, 0)), pl.BlockSpec((1, tk, n), lambda j, k: (j, k, 0))]
                 + ([] if buf is None else [_ANY]),
        out_specs=pl.BlockSpec((1, 1, m, n), lambda j, k: (j, layer, 0, 0)),
        out_shape=jax.ShapeDtypeStruct((s, 2, m, n), F32),
        input_output_aliases={} if buf is None else {2: 0},
        compiler_params=_cp(("arbitrary", "arbitrary")),
    )(*ins)


def _dw_down(act, df, layer, buf):
    nc, t, n = act.shape
    d = df.shape[1]
    tk = min(TK_DW, t)

    def body(*refs):
        a_ref, b_ref, o_ref = refs[0], refs[1], refs[-1]
        part = lax.dot_general(a_ref[0], b_ref[...], _CONTRACT_FIRST, preferred_element_type=F32)
        part = part.reshape(2, n // 2, d)
        first = pl.program_id(1) == 0

        @pl.when(first)
        def _():
            o_ref[:, 0] = part

        @pl.when(jnp.logical_not(first))
        def _():
            o_ref[:, 0] += part

    ins = [act, df] + ([] if buf is None else [buf])
    return pl.pallas_call(
        body, name="dw_down", grid=(nc, t // tk),
        in_specs=[pl.BlockSpec((1, tk, n), lambda c, k: (c, k, 0)), pl.BlockSpec((tk, d), lambda c, k: (k, 0))]
                 + ([] if buf is None else [_ANY]),
        out_specs=pl.BlockSpec((2, 1, n // 2, d), lambda c, k: (c, layer, 0, 0)),
        out_shape=jax.ShapeDtypeStruct((2 * nc, 2, n // 2, d), F32),
        input_output_aliases={} if buf is None else {2: 0},
        compiler_params=_cp(("arbitrary", "arbitrary")),
    )(*ins)


def _dz_wt_rms_bwd(dz, wt, x_in, gpre, dres):
    t, n = dz.shape
    tm = min(TM, t)

    def body(dz_ref, wt_ref, x_ref, g_ref, dres_ref, dx_ref, dg_ref):
        dh = jnp.dot(dz_ref[...], wt_ref[...], preferred_element_type=F32)
        dx, dg = _rms_bwd(x_ref[...], g_ref[...], dh)
        dx_ref[...] = dres_ref[...] + dx
        _acc_out(dg_ref, pl.program_id(0) == 0, dg)

    return pl.pallas_call(
        body, name="dz_wt_rms_bwd", grid=(t // tm,),
        in_specs=[_rows(tm, n), _full((n, D_MODEL)), _rows(tm, D_MODEL), _full((1, D_MODEL)), _rows(tm, D_MODEL)],
        out_specs=[_rows(tm, D_MODEL), _full((1, D_MODEL))],
        out_shape=[jax.ShapeDtypeStruct((t, D_MODEL), F32), jax.ShapeDtypeStruct((1, D_MODEL), F32)],
        compiler_params=_cp(("arbitrary",)),
    )(dz, wt, x_in, gpre, dres)


def _ffn_bwd(f, dxo, gpost, x_in, gpre, up, cw, wdt, wupt):
    t = f.shape[0]
    tm = min(TM_BWD, t)
    nt = t // tm
    nc, n = up.shape[1], up.shape[3]
    hb = 16

    def body(f_ref, dxo_ref, gpost_ref, x_ref, gpre_ref, up_ref, uph_ref, cw_ref, wdt_ref, wupt_ref,
             df_ref, act_ref, dup_ref, dx_ref, dgpost_ref, dgpre_ref, dcw_ref, df_s, acc, ext_in, ext_out, hal):
        i, c = pl.program_id(0), pl.program_id(1)
        r = nt - 1 - i

        @pl.when(c == 0)
        def _():
            df, dg = _rms_bwd(f_ref[...], gpost_ref[...], dxo_ref[...])
            df_s[...] = df.astype(BF16)
            df_ref[...] = df.astype(BF16)
            _acc_out(dgpost_ref, i == 0, dg)

        @pl.when(i == 0)
        def _():
            hal[c] = jnp.zeros((2, HALO3, n), F32)
            dcw_ref[0, c] = jnp.zeros((8, n), F32)
            dcw_ref[1, c] = jnp.zeros((8, n), F32)

        dact = jnp.dot(df_s[...], wdt_ref[0], preferred_element_type=F32)
        u = []
        for gv in range(2):
            prev = uph_ref[gv, 0].astype(F32)[hb - HALO3:, :]
            ext_in[gv, 0:HALO3, :] = jnp.where(r > 0, prev, 0.0)
            ext_in[gv, HALO3:HALO3 + tm, :] = up_ref[gv, 0].astype(F32)
            s = HALO3 - 2
            u.append(cw_ref[gv, 0, 0:1, :] * ext_in[gv, s:s + tm, :] + cw_ref[gv, 0, 1:2, :] * ext_in[gv, s + 1:s + 1 + tm, :]
                     + cw_ref[gv, 0, 2:3, :] * ext_in[gv, s + 2:s + 2 + tm, :])
        g, v = u
        sg = jax.nn.sigmoid(g)
        sil = g * sg
        act_ref[0] = (sil * v).astype(BF16)
        ext_out[0, 0:tm, :] = dact * v * (sg * (1.0 + g * (1.0 - sg)))
        ext_out[1, 0:tm, :] = dact * sil
        dh = None
        for gv in range(2):
            ext_out[gv, tm:tm + HALO3, :] = hal[c, gv]
            hal[c, gv] = ext_out[gv, 0:HALO3, :]
            du, d1, d2 = ext_out[gv, 0:tm, :], ext_out[gv, 1:1 + tm, :], ext_out[gv, 2:2 + tm, :]
            dup = (cw_ref[gv, 0, 2:3, :] * du + cw_ref[gv, 0, 1:2, :] * d1 + cw_ref[gv, 0, 0:1, :] * d2).astype(BF16)
            dup_ref[gv, 0] = dup
            upc = ext_in[gv, HALO3:HALO3 + tm, :]
            dcw_ref[gv, c, 2:3, :] += jnp.sum(upc * du, axis=0, keepdims=True)
            dcw_ref[gv, c, 1:2, :] += jnp.sum(upc * d1, axis=0, keepdims=True)
            dcw_ref[gv, c, 0:1, :] += jnp.sum(upc * d2, axis=0, keepdims=True)
            part = jnp.dot(dup, wupt_ref[gv, 0], preferred_element_type=F32)
            dh = part if dh is None else dh + part
        _acc_out(acc, c == 0, dh)

        @pl.when(c == nc - 1)
        def _():
            dx, dg = _rms_bwd(x_ref[...], gpre_ref[...], acc[...])
            dx_ref[...] = dxo_ref[...] + dx
            _acc_out(dgpre_ref, i == 0, dg)

    rrow = lambda w: pl.BlockSpec((tm, w), lambda i, c: (nt - 1 - i, 0))
    blk = lambda rows: pl.BlockSpec((2, 1, rows, n), lambda i, c: (0, c, nt - 1 - i, 0))
    halo = pl.BlockSpec((2, 1, hb, n), lambda i, c: (0, c, jnp.maximum((nt - 1 - i) * (tm // hb) - 1, 0), 0))
    one = _full((1, D_MODEL))
    return pl.pallas_call(
        body, name="ffn_bwd", grid=(nt, nc),
        in_specs=[rrow(D_MODEL), rrow(D_MODEL), one, rrow(D_MODEL), one, blk(tm), halo,
                  pl.BlockSpec((2, 1, 3, n), lambda i, c: (0, c, 0, 0)),
                  pl.BlockSpec((1, D_MODEL, n), lambda i, c: (c, 0, 0)),
                  pl.BlockSpec((2, 1, n, D_MODEL), lambda i, c: (0, c, 0, 0))],
        out_specs=[rrow(D_MODEL), pl.BlockSpec((1, tm, n), lambda i, c: (c, nt - 1 - i, 0)), blk(tm), rrow(D_MODEL),
                   one, one, _full((2, nc, 8, n))],
        out_shape=[jax.ShapeDtypeStruct((t, D_MODEL), BF16), jax.ShapeDtypeStruct((nc, t, n), BF16),
                   jax.ShapeDtypeStruct((2, nc, t, n), BF16), jax.ShapeDtypeStruct((t, D_MODEL), F32),
                   jax.ShapeDtypeStruct((1, D_MODEL), F32), jax.ShapeDtypeStruct((1, D_MODEL), F32),
                   jax.ShapeDtypeStruct((2, nc, 8, n), F32)],
        scratch_shapes=[pltpu.VMEM((tm, D_MODEL), BF16), pltpu.VMEM((tm, D_MODEL), F32),
                        pltpu.VMEM((2, HALO3 + tm, n), F32), pltpu.VMEM((2, tm + HALO3, n), F32),
                        pltpu.VMEM((nc, 2, HALO3, n), F32)],
        compiler_params=_cp(("arbitrary", "arbitrary")),
    )(f, dxo, gpost, x_in, gpre, up, up, cw, wdt, wupt)


def _od_bwd(m, dxo, gpost, x_in, gpre, z, cw, wot, wint):
    t = m.shape[0]
    tm = min(TM_BWD, t)
    nt = t // tm
    hb = 16

    def body(m_ref, dxo_ref, gpost_ref, x_ref, gpre_ref, z_ref, zh_ref, cw_ref, wot_ref, wint_ref,
             dm_ref, dz_ref, dx_ref, dgpost_ref, dgpre_ref, dcw_ref, ext_in, ext_out, dz_s):
        i = pl.program_id(0)
        r = nt - 1 - i
        dxo = dxo_ref[...]
        dm, dg = _rms_bwd(m_ref[...], gpost_ref[...], dxo)
        dmb = dm.astype(BF16)
        dm_ref[...] = dmb
        _acc_out(dgpost_ref, i == 0, dg)

        @pl.when(i == 0)
        def _():
            ext_out[tm:tm + HALO3, :] = jnp.zeros((HALO3, SC_DIM), F32)
            dcw_ref[...] = jnp.zeros((8, SC_DIM), F32)

        dy = jnp.dot(dmb, wot_ref[...], preferred_element_type=F32)
        z = z_ref[...].astype(F32)
        b, cg, u = z[:, :SC_DIM], z[:, SC_DIM:2 * SC_DIM], z[:, 2 * SC_DIM:]
        zp = zh_ref[...].astype(F32)[hb - HALO3:, :]
        ext_in[0:HALO3, :] = jnp.where(r > 0, zp[:, SC_DIM:2 * SC_DIM] * zp[:, 2 * SC_DIM:], 0.0)
        ext_in[HALO3:HALO3 + tm, :] = cg * u
        dz_s[:, 0:SC_DIM] = (dy * _conv3(cw_ref, ext_in, tm)).astype(BF16)
        dcv = dy * b
        ext_out[0:tm, :] = dcv
        d1, d2 = ext_out[1:1 + tm, :], ext_out[2:2 + tm, :]
        dcu = cw_ref[2:3, :] * dcv + cw_ref[1:2, :] * d1 + cw_ref[0:1, :] * d2
        cu = ext_in[HALO3:HALO3 + tm, :]
        dcw_ref[2:3, :] += jnp.sum(cu * dcv, axis=0, keepdims=True)
        dcw_ref[1:2, :] += jnp.sum(cu * d1, axis=0, keepdims=True)
        dcw_ref[0:1, :] += jnp.sum(cu * d2, axis=0, keepdims=True)
        ext_out[tm:tm + HALO3, :] = ext_out[0:HALO3, :]
        dz_s[:, SC_DIM:2 * SC_DIM] = (dcu * u).astype(BF16)
        dz_s[:, 2 * SC_DIM:3 * SC_DIM] = (dcu * cg).astype(BF16)
        dz_ref[...] = dz_s[...]
        dh = jnp.dot(dz_s[...], wint_ref[...], preferred_element_type=F32)
        dx, dg2 = _rms_bwd(x_ref[...], gpre_ref[...], dh)
        dx_ref[...] = dxo + dx
        _acc_out(dgpre_ref, i == 0, dg2)

    rrow = lambda w: pl.BlockSpec((tm, w), lambda i: (nt - 1 - i, 0))
    halo = pl.BlockSpec((hb, 3 * SC_DIM), lambda i: (jnp.maximum((nt - 1 - i) * (tm // hb) - 1, 0), 0))
    one = _full((1, D_MODEL))
    return pl.pallas_call(
        body, name="od_bwd", grid=(nt,),
        in_specs=[rrow(D_MODEL), rrow(D_MODEL), one, rrow(D_MODEL), one, rrow(3 * SC_DIM), halo, _full((3, SC_DIM)),
                  _full((D_MODEL, SC_DIM)), _full((3 * SC_DIM, D_MODEL))],
        out_specs=[rrow(D_MODEL), rrow(3 * SC_DIM), rrow(D_MODEL), one, one, _full((8, SC_DIM))],
        out_shape=[jax.ShapeDtypeStruct((t, D_MODEL), BF16), jax.ShapeDtypeStruct((t, 3 * SC_DIM), BF16),
                   jax.ShapeDtypeStruct((t, D_MODEL), F32), jax.ShapeDtypeStruct((1, D_MODEL), F32),
                   jax.ShapeDtypeStruct((1, D_MODEL), F32), jax.ShapeDtypeStruct((8, SC_DIM), F32)],
        scratch_shapes=[pltpu.VMEM((HALO3 + tm, SC_DIM), F32), pltpu.VMEM((tm + HALO3, SC_DIM), F32),
                        pltpu.VMEM((tm, 3 * SC_DIM), BF16)],
        compiler_params=_cp(("arbitrary",)),
    )(m, dxo, gpost, x_in, gpre, z, z, cw, wot, wint)


def _ev_bwd1(m, dxo, gpost, wot):
    t = m.shape[0]
    tm = min(TM, t)

    def body(m_ref, dxo_ref, g_ref, wot_ref, dm_ref, da_ref, do_ref, dg_ref):
        dm, dg = _rms_bwd(m_ref[...], g_ref[...], dxo_ref[...])
        dmb = dm.astype(BF16)
        dm_ref[...] = dmb
        _acc_out(dg_ref, pl.program_id(0) == 0, dg)
        dao = jnp.dot(dmb, wot_ref[...], preferred_element_type=F32)
        da_ref[...] = dao[:, :A_CH]
        do_ref[...] = dao[:, A_CH:].astype(BF16)

    return pl.pallas_call(
        body, name="ev_bwd1", grid=(t // tm,),
        in_specs=[_rows(tm, D_MODEL), _rows(tm, D_MODEL), _full((1, D_MODEL)), _full((D_MODEL, A_CH + Q_DIM))],
        out_specs=[_rows(tm, D_MODEL), _rows(tm, A_CH), _rows(tm, Q_DIM), _full((1, D_MODEL))],
        out_shape=[jax.ShapeDtypeStruct((t, D_MODEL), BF16), jax.ShapeDtypeStruct((t, A_CH), F32),
                   jax.ShapeDtypeStruct((t, Q_DIM), BF16), jax.ShapeDtypeStruct((1, D_MODEL), F32)],
        compiler_params=_cp(("arbitrary",)),
    )(m, dxo, gpost, wot)


def _conf_bwd(da, cv, zag, conv_w, ln_g, ln_b):
    t = da.shape[0]
    tm = min(TM_BWD, t)
    nt = t // tm

    def body(da_ref, c_ref, z_ref, zh_ref, w_ref, g_ref, lb_ref, dz_ref, dw_ref, dv_ref, ext_in, ext_out):
        i = pl.program_id(0)
        r = nt - 1 - i

        @pl.when(i == 0)
        def _():
            ext_out[tm:tm + HALO31, :] = jnp.zeros((HALO31, A_CH), F32)
            dw_ref[...] = jnp.zeros((32, A_CH), F32)
            dv_ref[...] = jnp.zeros((8, A_CH), F32)

        x = c_ref[...]
        mu = jnp.mean(x, axis=-1, keepdims=True)
        xc = x - mu
        rstd = lax.rsqrt(jnp.mean(xc * xc, axis=-1, keepdims=True) + LN_EPS)
        xh = xc * rstd
        ln = xh * g_ref[...] + lb_ref[...]
        sl = jax.nn.sigmoid(ln)
        dln = da_ref[...] * (sl * (1.0 + ln * (1.0 - sl)))
        dxh = dln * g_ref[...]
        dc = rstd * (dxh - jnp.mean(dxh, axis=-1, keepdims=True) - xh * jnp.mean(dxh * xh, axis=-1, keepdims=True))
        dv_ref[0:1, :] += jnp.sum(dc, axis=0, keepdims=True)
        dv_ref[1:2, :] += jnp.sum(dln * xh, axis=0, keepdims=True)
        dv_ref[2:3, :] += jnp.sum(dln, axis=0, keepdims=True)

        ext_out[0:tm, :] = dc
        dglu = jnp.zeros((tm, A_CH), F32)
        for j in range(A_CONV):
            s = A_CONV - 1 - j
            dglu = dglu + w_ref[j:j + 1, :] * ext_out[s:s + tm, :]
        ext_out[tm:tm + HALO31, :] = ext_out[0:HALO31, :]

        ext_in[0:HALO31, :] = jnp.where(r > 0, _glu(zh_ref[...]), 0.0)
        z = z_ref[...].astype(F32)
        al, sg = z[:, :A_CH], jax.nn.sigmoid(z[:, A_CH:])
        ext_in[HALO31:HALO31 + tm, :] = al * sg
        for j in range(A_CONV):
            s = HALO31 - (A_CONV - 1) + j
            dw_ref[j:j + 1, :] += jnp.sum(dc * ext_in[s:s + tm, :], axis=0, keepdims=True)
        dz_ref[:, 0:A_CH] = (dglu * sg).astype(BF16)
        dz_ref[:, A_CH:2 * A_CH] = (dglu * al * sg * (1.0 - sg)).astype(BF16)

    rrow = lambda w: pl.BlockSpec((tm, w), lambda i: (nt - 1 - i, 0))
    halo = pl.BlockSpec((HALO31, 2 * A_CH), lambda i: (jnp.maximum((nt - 1 - i) * (tm // HALO31) - 1, 0), 0))
    return pl.pallas_call(
        body, name="conf_bwd", grid=(nt,),
        in_specs=[rrow(A_CH), rrow(A_CH), rrow(2 * A_CH), halo, _full((32, A_CH)), _full((1, A_CH)),
                  _full((1, A_CH))],
        out_specs=[rrow(2 * A_CH), _full((32, A_CH)), _full((8, A_CH))],
        out_shape=[jax.ShapeDtypeStruct((t, 2 * A_CH), BF16), jax.ShapeDtypeStruct((32, A_CH), F32),
                   jax.ShapeDtypeStruct((8, A_CH), F32)],
        scratch_shapes=[pltpu.VMEM((HALO31 + tm, A_CH), F32), pltpu.VMEM((tm + HALO31, A_CH), F32)],
        compiler_params=_cp(("arbitrary",)),
    )(da, cv, zag, zag, conv_w, ln_g, ln_b)


def _attn_bwd(q, k, v, do, sinks):
    t = q.shape[0]
    nb = t // BLOCK

    def body(s_ref, q_ref, kc_ref, kp_ref, vc_ref, vp_ref, do_ref, dq_ref, dk_ref, dv_ref, ds_ref, dkc, dvc):
        i = pl.program_id(0)
        r = nb - 1 - i

        @pl.when(i == 0)
        def _():
            dkc[...] = jnp.zeros_like(dkc)
            dvc[...] = jnp.zeros_like(dvc)
            ds_ref[...] = jnp.zeros_like(ds_ref)

        mask = _attn_mask(r == 0)
        lane = lax.broadcasted_iota(jnp.int32, (1, N_Q_HEADS), 1)
        dsv = jnp.zeros((1, N_Q_HEADS), F32)
        qv, kc, kp, vc, vp, dov = q_ref[...], kc_ref[...], kp_ref[...], vc_ref[...], vp_ref[...], do_ref[...]
        for h in range(N_KV_HEADS):
            q4, do4 = _q_heads(qv, h), _q_heads(dov, h)
            k2, v2 = _kv_head(kp, kc, h), _kv_head(vp, vc, h)
            pn, ps = _attn_probs(q4, k2, mask, _sink_rows(s_ref, h))
            dp = lax.dot_general(do4, v2, _CONTRACT_LAST, preferred_element_type=F32)
            dl = jnp.sum(pn * dp, axis=-1, keepdims=True)
            dsb = (pn * (dp - dl)).astype(BF16)
            dq4 = (jnp.dot(dsb, k2, preferred_element_type=F32) * SCALE).astype(BF16)
            for g in range(GROUP):
                c0 = HEAD_DIM * (GROUP * h + g)
                dq_ref[:, c0:c0 + HEAD_DIM] = dq4[BLOCK * g:BLOCK * (g + 1), :]
            dk2 = lax.dot_general(dsb, q4, _CONTRACT_FIRST, preferred_element_type=F32) * SCALE
            dv2 = lax.dot_general(pn.astype(BF16), do4, _CONTRACT_FIRST, preferred_element_type=F32)
            dk_ref[:, HEAD_DIM * h:HEAD_DIM * (h + 1)] = dk2[BLOCK:, :] + dkc[h]
            dv_ref[:, HEAD_DIM * h:HEAD_DIM * (h + 1)] = dv2[BLOCK:, :] + dvc[h]
            dkc[h] = dk2[:BLOCK, :]
            dvc[h] = dv2[:BLOCK, :]
            srow = -ps * dl
            for g in range(GROUP):
                dsv = dsv + jnp.where(lane == GROUP * h + g, jnp.sum(srow[BLOCK * g:BLOCK * (g + 1), :]), 0.0)
        ds_ref[...] += dsv

    cur = lambda n: pl.BlockSpec((BLOCK, n), lambda i: (nb - 1 - i, 0))
    prev = lambda n: pl.BlockSpec((BLOCK, n), lambda i: (jnp.maximum(nb - 2 - i, 0), 0))
    return pl.pallas_call(
        body, name="attn_bwd", grid=(nb,),
        in_specs=[pl.BlockSpec(memory_space=pltpu.SMEM), cur(Q_DIM), cur(KV_DIM), prev(KV_DIM), cur(KV_DIM),
                  prev(KV_DIM), cur(Q_DIM)],
        out_specs=[cur(Q_DIM), cur(KV_DIM), cur(KV_DIM), _full((1, N_Q_HEADS))],
        out_shape=[jax.ShapeDtypeStruct((t, Q_DIM), BF16), jax.ShapeDtypeStruct((t, KV_DIM), F32),
                   jax.ShapeDtypeStruct((t, KV_DIM), F32), jax.ShapeDtypeStruct((1, N_Q_HEADS), F32)],
        scratch_shapes=[pltpu.VMEM((N_KV_HEADS, BLOCK, HEAD_DIM), F32), pltpu.VMEM((N_KV_HEADS, BLOCK, HEAD_DIM), F32)],
        compiler_params=_cp(("arbitrary",)),
    )(sinks, q, k, k, v, v, do)


def _ev_dz(dzag, dq, dk, dv, rc, rsa, rsb):
    t = dzag.shape[0]
    tm = min(TM, t)

    def body(dzag_ref, dq_ref, dk_ref, dv_ref, c_ref, sa_ref, sb_ref, dz_ref):
        c, sa, sb = c_ref[...], sa_ref[...], sb_ref[...]
        dz_ref[:, 0:2 * A_CH] = dzag_ref[...]
        q0 = 2 * A_CH
        for j in range(Q_DIM // 128):
            d = dq_ref[:, 128 * j:128 * (j + 1)].astype(F32)
            dz_ref[:, q0 + 128 * j:q0 + 128 * (j + 1)] = _rope_bwd(d, c, sa, sb).astype(BF16)
        k0 = q0 + Q_DIM
        dz_ref[:, k0:k0 + KV_DIM] = _rope_bwd(dk_ref[...], c, sa, sb).astype(BF16)
        dz_ref[:, k0 + KV_DIM:k0 + 2 * KV_DIM] = dv_ref[...].astype(BF16)

    return pl.pallas_call(
        body, name="ev_dz", grid=(t // tm,),
        in_specs=[_rows(tm, 2 * A_CH), _rows(tm, Q_DIM), _rows(tm, KV_DIM), _rows(tm, KV_DIM),
                  _rows(tm, 128), _rows(tm, 128), _rows(tm, 128)],
        out_specs=_rows(tm, EVEN_IN),
        out_shape=jax.ShapeDtypeStruct((t, EVEN_IN), BF16),
        compiler_params=_cp(("arbitrary",)),
    )(dzag, dq, dk, dv, rc, rsa, rsb)


def _prep_weights(gat):
    p = {}
    w = gat["ev_w_in"][:, 0].transpose(1, 0, 2).reshape(D_MODEL, EVEN_IN)
    p["ev_w_in"], p["ev_w_in_t"] = w, w.T
    w = gat["ev_w_out"].reshape(A_CH + Q_DIM, D_MODEL)
    p["ev_w_out"], p["ev_w_out_t"] = w, w.T
    g = gat["od_w_in"][:, 0]
    p["od_w_in"], p["od_w_in_t"] = g, g.transpose(0, 2, 1).reshape(3 * SC_DIM, D_MODEL)
    w = gat["od_w_out"].reshape(SC_DIM, D_MODEL)
    p["od_w_out"], p["od_w_out_t"] = w, w.T
    g = gat["ffn_w_up"]
    p["ffn_w_up"] = g.reshape(2, N_DEV // 2, 2, D_MODEL, FF_N)
    p["ffn_w_up_t"] = [g[:, i].transpose(0, 2, 1).reshape(2, N_DEV // 2, FF_N, D_MODEL) for i in range(2)]
    g = gat["ffn_w_down"]
    p["ffn_w_down"] = [g[:, i].reshape(D_FF, D_MODEL) for i in range(2)]
    p["ffn_w_down_t"] = [w.reshape(N_DEV // 2, FF_N, D_MODEL).transpose(0, 2, 1) for w in p["ffn_w_down"]]
    return p


def _local_step(x, positions, target, p, s):
    row = lambda a: a.reshape(1, -1)
    nc = N_DEV // 2
    rc, rsa, rsb = _rope_tables(positions)
    conv31 = jnp.pad(s["ev_a_conv_w"][0], ((0, 1), (0, 0)))
    cw_ffn = [s["ffn_conv_w"][i].reshape(3, 2, nc, FF_N).transpose(1, 2, 0, 3) for i in range(2)]
    sinks = s["ev_sinks"][0]
    big, g = {}, {}

    h0, zag, q, k, v = _ev_in(x, row(s["mix_norm_pre"][0]), p["ev_w_in"], rc, rsa, rsb)
    cv, a = _conf_fwd(zag, conv31, s["ev_a_conv_b"], s["ev_a_ln_g"], s["ev_a_ln_b"])
    o = _attn_fwd(q, k, v, sinks)
    wo = p["ev_w_out"]
    m0, x1 = _out_post([a, o], [wo[:A_CH], wo[A_CH:]], x, row(s["mix_norm_post"][0]))
    h1, up0, f0, x2 = _ffn_fwd(x1, row(s["ffn_norm_pre"][0]), p["ffn_w_up"], 0, cw_ffn[0], p["ffn_w_down"][0],
                               row(s["ffn_norm_post"][0]))
    h2, z, y, m1, x3 = _od_fwd(x2, row(s["mix_norm_pre"][1]), p["od_w_in"], s["od_conv_w"][0], p["od_w_out"],
                               row(s["mix_norm_post"][1]))
    h3, up1, f1, x4 = _ffn_fwd(x3, row(s["ffn_norm_pre"][1]), p["ffn_w_up"], 1, cw_ffn[1], p["ffn_w_down"][1],
                               row(s["ffn_norm_post"][1]))
    dx, lpart = _loss_grad(x4, target)

    def ffn_back(i, f, dxo, up, h, x_in, bufs):
        df, act, dup, dx_in, dgpost, dgpre, dcw = _ffn_bwd(
            f, dxo, row(s["ffn_norm_post"][i]), x_in, row(s["ffn_norm_pre"][i]), up, cw_ffn[i], p["ffn_w_down_t"][i],
            p["ffn_w_up_t"][i])
        bufs = (_dw_up(h, dup.reshape(N_DEV, -1, FF_N), i, bufs[0]), _dw_down(act, df, i, bufs[1]))
        return dx_in, dgpost, dgpre, dcw[:, :, 0:3].transpose(2, 0, 1, 3).reshape(3, 2 * D_FF), bufs

    dx, dgfpost1, dgfpre1, dcw1, bufs = ffn_back(1, f1, dx, up1, h3, x3, (None, None))

    dm1, dz, dx, dgpost1, dgpre1, dcw_od = _od_bwd(m1, dx, row(s["mix_norm_post"][1]), x2, row(s["mix_norm_pre"][1]), z,
                                                   s["od_conv_w"][0], p["od_w_out_t"], p["od_w_in_t"])
    big["od_w_out"] = _dw2d(y, dm1, SC_DIM, D_MODEL).reshape(N_DEV, -1, D_MODEL)
    big["od_w_in"] = _dw_cols(h2, dz, 3 * SC_DIM // N_DEV)
    g["od_conv_w"] = dcw_od[None, 0:3]

    dx, dgfpost0, dgfpre0, dcw0, bufs = ffn_back(0, f0, dx, up0, h1, x1, bufs)
    big["ffn_w_up"], big["ffn_w_down"] = bufs

    dm0, da, do, dgpost0 = _ev_bwd1(m0, dx, row(s["mix_norm_post"][0]), p["ev_w_out_t"])
    big["ev_w_out"] = jnp.concatenate([_dw2d(a, dm0, A_CH, D_MODEL), _dw2d(o, dm0, Q_DIM, D_MODEL)],
                                      axis=0).reshape(N_DEV, -1, D_MODEL)
    dzag, dcw31, dvec = _conf_bwd(da, cv, zag, conv31, s["ev_a_ln_g"], s["ev_a_ln_b"])
    dq, dk, dv, dsinks = _attn_bwd(q, k, v, do, sinks)
    dz0 = _ev_dz(dzag, dq, dk, dv, rc, rsa, rsb)
    dw_in = _dw2d(h0, dz0, D_MODEL, EVEN_IN // 2)
    big["ev_w_in"] = dw_in.reshape(D_MODEL, N_DEV, EVEN_IN // N_DEV).transpose(1, 0, 2)
    dx, dgpre0 = _dz_wt_rms_bwd(dz0, p["ev_w_in_t"], x, row(s["mix_norm_pre"][0]), dx)

    g["mix_norm_pre"] = jnp.concatenate([dgpre0, dgpre1], axis=0)
    g["mix_norm_post"] = jnp.concatenate([dgpost0, dgpost1], axis=0)
    g["ffn_norm_pre"] = jnp.concatenate([dgfpre0, dgfpre1], axis=0)
    g["ffn_norm_post"] = jnp.concatenate([dgfpost0, dgfpost1], axis=0)
    g["ev_a_conv_w"] = dcw31[None, 0:A_CONV]
    g["ev_a_conv_b"], g["ev_a_ln_g"], g["ev_a_ln_b"] = dvec[0:1], dvec[1:2], dvec[2:3]
    g["ev_sinks"] = dsinks
    g["ffn_conv_w"] = jnp.stack([dcw0, dcw1])
    return lpart[0, 0], dx, big, g


MESH = pl.DeviceIdType.MESH


def _all_gather(shards, name):
    nw = len(shards)

    def body(*refs):
        x_refs, out_refs = refs[:nw], refs[nw:2 * nw]
        send_sems, recv_sems, local_sems = refs[2 * nw:]
        x, y, c = lax.axis_index("x"), lax.axis_index("y"), lax.axis_index("c")
        me, sibling = (x, y, c), (x, y, 1 - c)
        chips = [(1 - x, y), (x, 1 - y), (1 - x, 1 - y)]

        def rows(w, px, py, pc):
            m_per = shards[w].shape[0]
            return out_refs[w].at[pl.ds((4 * px + 2 * py + pc) * m_per, m_per), :]

        def copy(w, k, block, to, src=None):
            return pltpu.make_async_remote_copy(
                src_ref=rows(w, *block) if src is None else src, dst_ref=rows(w, *block),
                send_sem=send_sems.at[w, k], recv_sem=recv_sems.at[w, k], device_id=to, device_id_type=MESH)

        mine, first, passed = [], [], []
        for w in range(nw):
            cp = pltpu.make_async_copy(x_refs[w], rows(w, *me), local_sems.at[w])
            cp.start()
            mine.append(cp)
            first.append([copy(w, 0, me, sibling, src=x_refs[w])]
                         + [copy(w, 1 + j, me, (*chip, c), src=x_refs[w]) for j, chip in enumerate(chips)])
            for cp in first[w]:
                cp.start()
        for w in range(nw):
            passed.append([copy(w, 4 + j, (*chip, c), sibling) for j, chip in enumerate(chips)])
            for j, chip in enumerate(chips):
                copy(w, 1 + j, (*chip, c), me).wait_recv()
                passed[w][j].start()
        for w in range(nw):
            copy(w, 0, sibling, me).wait_recv()
            for j, chip in enumerate(chips):
                copy(w, 4 + j, (*chip, 1 - c), me).wait_recv()
            for cp in first[w] + passed[w]:
                cp.wait_send()
            mine[w].wait()

    return pl.pallas_call(
        body, name=name,
        out_shape=[jax.ShapeDtypeStruct((N_DEV * a.shape[0], a.shape[1]), a.dtype) for a in shards],
        in_specs=[_ANY] * nw, out_specs=[_ANY] * nw,
        scratch_shapes=[pltpu.SemaphoreType.DMA((nw, 7)), pltpu.SemaphoreType.DMA((nw, 7)),
                        pltpu.SemaphoreType.DMA((nw,))],
    )(*shards)


def _rs_d2d(bufs):
    nw = len(bufs)

    def body(*refs):
        g_refs, recv_refs = refs[:nw], refs[nw:2 * nw]
        send_sems, recv_sems = refs[2 * nw:]
        x, y, c = lax.axis_index("x"), lax.axis_index("y"), lax.axis_index("c")
        copies = [pltpu.make_async_remote_copy(
            src_ref=g_refs[w].at[2 * q + (1 - c)], dst_ref=recv_refs[w].at[q], send_sem=send_sems.at[w, q],
            recv_sem=recv_sems.at[w, q], device_id=(x, y, 1 - c), device_id_type=MESH)
            for w in range(nw) for q in range(4)]
        for cp in copies:
            cp.start()
        for cp in copies:
            cp.wait()

    return pl.pallas_call(
        body, name="rs_d2d", out_shape=[jax.ShapeDtypeStruct((4,) + b.shape[1:], b.dtype) for b in bufs],
        in_specs=[_ANY] * nw, out_specs=[_ANY] * nw,
        scratch_shapes=[pltpu.SemaphoreType.DMA((nw, 4)), pltpu.SemaphoreType.DMA((nw, 4))],
    )(*bufs)


def _rs_ici(parts):
    nw = len(parts)

    def body(*refs):
        p_refs, recv_refs = refs[:nw], refs[nw:2 * nw]
        send_sems, recv_sems = refs[2 * nw:]
        x, y, c = lax.axis_index("x"), lax.axis_index("y"), lax.axis_index("c")
        chips = [(1 - x, y), (x, 1 - y), (1 - x, 1 - y)]
        copies = [pltpu.make_async_remote_copy(
            src_ref=p_refs[w].at[2 * px + py], dst_ref=recv_refs[w].at[j], send_sem=send_sems.at[w, j],
            recv_sem=recv_sems.at[w, j], device_id=(px, py, c), device_id_type=MESH)
            for w in range(nw) for j, (px, py) in enumerate(chips)]
        for cp in copies:
            cp.start()
        for cp in copies:
            cp.wait()

    return pl.pallas_call(
        body, name="rs_ici", out_shape=[jax.ShapeDtypeStruct((3,) + b.shape[1:], b.dtype) for b in parts],
        in_specs=[_ANY] * nw, out_specs=[_ANY] * nw,
        scratch_shapes=[pltpu.SemaphoreType.DMA((nw, 3)), pltpu.SemaphoreType.DMA((nw, 3))],
    )(*parts)


def _row_tile(rows, cap):
    best = None
    for d in range(16, min(rows, cap) + 1, 16):
        if rows % d == 0:
            best = d
    return rows if best is None else best


def _rs_add(gp, recv, core):
    _, r, l = gp.shape
    tr = _row_tile(r, 512)

    def body(c_ref, g_ref, r_ref, o_ref):
        o_ref[...] = (g_ref[...] + r_ref[...]).astype(BF16)

    return pl.pallas_call(
        body, name="rs_add", out_shape=jax.ShapeDtypeStruct((4, r, l), BF16),
        grid_spec=pltpu.PrefetchScalarGridSpec(
            num_scalar_prefetch=1, grid=(4, r // tr),
            in_specs=[pl.BlockSpec((1, tr, l), lambda q, j, c: (2 * q + c[0], j, 0)),
                      pl.BlockSpec((1, tr, l), lambda q, j, c: (q, j, 0))],
            out_specs=pl.BlockSpec((1, tr, l), lambda q, j, c: (q, j, 0))),
        compiler_params=_cp(("arbitrary", "arbitrary")),
    )(core, gp, recv)


def _adam_math(w, g, m, v):
    bc1 = 1.0 - ADAM_B1 ** ADAM_STEP
    bc2 = 1.0 - ADAM_B2 ** ADAM_STEP
    mn = ADAM_B1 * m + (1.0 - ADAM_B1) * g
    vn = ADAM_B2 * v + (1.0 - ADAM_B2) * (g * g)
    return -ADAM_LR * ((mn / bc1) / (jnp.sqrt(vn / bc2) + ADAM_EPS) + ADAM_WD * w), mn, vn


def _adamw_rs(gp, recv_a, recv_b, w, m, v, where):
    _, r, l = gp.shape
    tr = _row_tile(r, 256)

    def body(i_ref, g_ref, a_ref, b_ref, w_ref, m_ref, v_ref, go_ref, d_ref, mo_ref, vo_ref):
        g = g_ref[0] + a_ref[0]
        for j in range(3):
            g = g + b_ref[j].astype(F32)
        go_ref[...] = g
        d_ref[...], mo_ref[...], vo_ref[...] = _adam_math(w_ref[...], g, m_ref[...], v_ref[...])

    spec = pl.BlockSpec((tr, l), lambda i, s: (i, 0))
    return pl.pallas_call(
        body, name="adamw_rs", out_shape=[jax.ShapeDtypeStruct((r, l), F32)] * 4,
        grid_spec=pltpu.PrefetchScalarGridSpec(
            num_scalar_prefetch=1, grid=(r // tr,),
            in_specs=[pl.BlockSpec((1, tr, l), lambda i, s: (s[0], i, 0)),
                      pl.BlockSpec((1, tr, l), lambda i, s: (s[1], i, 0)),
                      pl.BlockSpec((3, tr, l), lambda i, s: (0, i, 0)), spec, spec, spec],
            out_specs=[spec] * 4),
        compiler_params=_cp(("arbitrary",)),
    )(where, gp, recv_a, recv_b, w, m, v)


def _sum_blocks(a, nblk):
    m = a.shape[0] // nblk
    n = a.shape[1]

    def body(a_ref, o_ref):
        acc = a_ref[0]
        for j in range(1, nblk):
            acc = acc + a_ref[j]
        o_ref[...] = acc

    return pl.pallas_call(
        body, name="sum_blocks", out_shape=jax.ShapeDtypeStruct((m, n), a.dtype),
        in_specs=[_full((nblk, m, n))], out_specs=_full((m, n)),
    )(a.reshape(nblk, m, n))


def _adamw(w, g, m, v):
    rows, c = w.shape

    def body(w_ref, g_ref, m_ref, v_ref, d_ref, mo_ref, vo_ref):
        d_ref[...], mo_ref[...], vo_ref[...] = _adam_math(w_ref[...], g_ref[...], m_ref[...], v_ref[...])

    return pl.pallas_call(
        body, name="adamw", in_specs=[_full((rows, c))] * 4, out_specs=[_full((rows, c))] * 3,
        out_shape=[jax.ShapeDtypeStruct((rows, c), F32)] * 3,
    )(w, g, m, v)


WEIGHTS = ["mix_norm_pre", "mix_norm_post", "ffn_norm_pre", "ffn_norm_post", "ev_w_in", "ev_a_conv_w", "ev_a_conv_b",
           "ev_a_ln_g", "ev_a_ln_b", "ev_sinks", "ev_w_out", "od_w_in", "od_conv_w", "od_w_out", "ffn_w_up",
           "ffn_conv_w", "ffn_w_down"]
BIG = ["ev_w_in", "ev_w_out", "od_w_in", "od_w_out", "ffn_w_up", "ffn_w_down"]
SMALL_REPL = ["mix_norm_pre", "mix_norm_post", "ffn_norm_pre", "ffn_norm_post", "ev_a_conv_b", "ev_a_ln_g",
              "ev_a_ln_b", "ev_sinks"]
SMALL_SHARDED = ["ev_a_conv_w", "od_conv_w", "ffn_conv_w"]


def _pack(arrs, rows):
    flat = jnp.concatenate([a.reshape(-1) for a in arrs])
    return jnp.pad(flat, (0, rows * LANES - flat.shape[0])).reshape(rows, LANES)


def _unpack(packed, shapes):
    flat, out, off = packed.reshape(-1), [], 0
    for s in shapes:
        n = 1
        for d in s:
            n *= d
        out.append(flat[off:off + n].reshape(s))
        off += n
    return out


def kernel(x, positions, mix_norm_pre, mix_norm_post, ffn_norm_pre, ffn_norm_post, ev_w_in, ev_a_conv_w, ev_a_conv_b, ev_a_ln_g, ev_a_ln_b, ev_sinks, ev_w_out, od_w_in, od_conv_w, od_w_out, ffn_w_up, ffn_conv_w, ffn_w_down, loss_target, m_mix_norm_pre, m_mix_norm_post, m_ffn_norm_pre, m_ffn_norm_post, m_ev_w_in, m_ev_a_conv_w, m_ev_a_conv_b, m_ev_a_ln_g, m_ev_a_ln_b, m_ev_sinks, m_ev_w_out, m_od_w_in, m_od_conv_w, m_od_w_out, m_ffn_w_up, m_ffn_conv_w, m_ffn_w_down, v_mix_norm_pre, v_mix_norm_post, v_ffn_norm_pre, v_ffn_norm_post, v_ev_w_in, v_ev_a_conv_w, v_ev_a_conv_b, v_ev_a_ln_g, v_ev_a_ln_b, v_ev_sinks, v_ev_w_out, v_od_w_in, v_od_conv_w, v_od_w_out, v_ffn_w_up, v_ffn_conv_w, v_ffn_w_down):
    w = dict(zip(WEIGHTS, (mix_norm_pre, mix_norm_post, ffn_norm_pre, ffn_norm_post, ev_w_in, ev_a_conv_w, ev_a_conv_b,
                           ev_a_ln_g, ev_a_ln_b, ev_sinks, ev_w_out, od_w_in, od_conv_w, od_w_out, ffn_w_up, ffn_conv_w,
                           ffn_w_down)))
    mom = dict(zip(WEIGHTS, (m_mix_norm_pre, m_mix_norm_post, m_ffn_norm_pre, m_ffn_norm_post, m_ev_w_in, m_ev_a_conv_w,
                             m_ev_a_conv_b, m_ev_a_ln_g, m_ev_a_ln_b, m_ev_sinks, m_ev_w_out, m_od_w_in, m_od_conv_w,
                             m_od_w_out, m_ffn_w_up, m_ffn_conv_w, m_ffn_w_down)))
    var = dict(zip(WEIGHTS, (v_mix_norm_pre, v_mix_norm_post, v_ffn_norm_pre, v_ffn_norm_post, v_ev_w_in, v_ev_a_conv_w,
                             v_ev_a_conv_b, v_ev_a_ln_g, v_ev_a_ln_b, v_ev_sinks, v_ev_w_out, v_od_w_in, v_od_conv_w,
                             v_od_w_out, v_ffn_w_up, v_ffn_conv_w, v_ffn_w_down)))
    ix, iy, ic = lax.axis_index("x"), lax.axis_index("y"), lax.axis_index("c")
    dev = 4 * ix + 2 * iy + ic
    two = lambda a: a.reshape(-1, a.shape[-1])

    gat = _all_gather([two(w[n].astype(BF16)) for n in BIG], "gather_big")
    p = _prep_weights({n: a.reshape((N_DEV,) + w[n].shape) for n, a in zip(BIG, gat)})
    small = {n: w[n] for n in SMALL_REPL}
    small_shapes = [w[n].shape for n in SMALL_SHARDED]
    conv_gat = _all_gather([_pack([w[n] for n in SMALL_SHARDED], 8)], "gather_conv")[0].reshape(N_DEV, 8, LANES)
    per_dev = [_unpack(conv_gat[d], small_shapes) for d in range(N_DEV)]
    for k, n in enumerate(SMALL_SHARDED):
        small[n] = jnp.concatenate([per_dev[d][k] for d in range(N_DEV)], axis=-1)

    lpart, grad_x, big, g = _local_step(x[0], positions[0], loss_target[0], p, small)
    loss = lax.psum(lpart, ("x", "y", "c"))

    bufs = [big[n].reshape(N_DEV, -1, w[n].shape[-1]) for n in BIG]
    core = jnp.reshape(ic, (1,)).astype(jnp.int32)
    where = jnp.stack([dev, 2 * ix + iy]).astype(jnp.int32)
    recv_a = _rs_d2d(bufs)
    recv_b = _rs_ici([_rs_add(b, ra, core) for b, ra in zip(bufs, recv_a)])
    grads, delta, new_m, new_v = {}, {}, {}, {}
    for n, b, ra, rb in zip(BIG, bufs, recv_a, recv_b):
        outs = _adamw_rs(b, ra, rb, two(w[n]), two(mom[n]), two(var[n]), where)
        grads[n], delta[n], new_m[n], new_v[n] = (a.reshape(w[n].shape) for a in outs)

    small_names = SMALL_REPL + SMALL_SHARDED
    s_all = _sum_blocks(_all_gather([_pack([g[n] for n in small_names], 64)], "gather_small_grads")[0], N_DEV)
    for n, a in zip(small_names, _unpack(s_all, [small[n].shape for n in small_names])):
        if n in SMALL_SHARDED:
            width = w[n].shape[-1]
            a = lax.dynamic_slice_in_dim(a, dev * width, width, axis=a.ndim - 1)
        grads[n] = a
    pk = lambda dct: _pack([dct[n] for n in small_names], 16)
    outs = _adamw(pk(w), pk(grads), pk(mom), pk(var))
    for dst, packed in zip((delta, new_m, new_v), outs):
        for n, a in zip(small_names, _unpack(packed, [w[n].shape for n in small_names])):
            dst[n] = a

    return (loss, grad_x[None], *[grads[n] for n in WEIGHTS], *[delta[n] for n in WEIGHTS],
            *[new_m[n] for n in WEIGHTS], *[new_v[n] for n in WEIGHTS])
```

```python
import jax
import jax.numpy as jnp
from jax import lax
from jax.experimental import pallas as pl
from jax.experimental.pallas import tpu as pltpu

F32, BF16 = jnp.float32, jnp.bfloat16

D_MODEL = 1024
A_CH = 512
A_CONV = 31
Q_DIM = 512
KV_DIM = 128
HEAD_DIM = 64
N_Q_HEADS = 8
N_KV_HEADS = 2
GROUP = 4
BLOCK = 128
EVEN_IN = 1792
SC_DIM = 1024
D_FF = 2816
ROPE_THETA = 500000.0
ROPE_DIM = 16
RMS_EPS = 1e-6
LN_EPS = 1e-5
SCALE = HEAD_DIM ** -0.5
NEG = -1e30

ADAM_LR, ADAM_B1, ADAM_B2, ADAM_EPS, ADAM_WD, ADAM_STEP = 0.001, 0.9, 0.999, 1e-08, 0.01, 10

N_DEV = 8
FF_N = 2 * D_FF // N_DEV
LANES = 1024
HALO3 = 8
HALO31 = 32
VMEM_LIMIT = 56 * 1024 * 1024

TM = 512
TM_BWD = 256
TK_DW = 2048
ATT_NB = 4

_ANY = pl.BlockSpec(memory_space=pl.ANY)
_CONTRACT_LAST = (((1,), (1,)), ((), ()))
_CONTRACT_FIRST = (((0,), (0,)), ((), ()))


def _cp(sem, vmem=VMEM_LIMIT):
    return pltpu.CompilerParams(dimension_semantics=sem, vmem_limit_bytes=vmem)


def _full(shape):
    n = len(shape)
    return pl.BlockSpec(shape, lambda *_: (0,) * n)


def _rows(tm, n):
    return pl.BlockSpec((tm, n), lambda i, *_: (i, 0))


def _sigmoid(x):
    return 0.5 * jnp.tanh(0.5 * x) + 0.5


def _rsqrt_mean(x):
    return lax.rsqrt(jnp.mean(x * x, axis=-1, keepdims=True) + RMS_EPS)


def _rms_bwd(x, g, dy):
    r = _rsqrt_mean(x)
    xh = x * r
    dxh = dy * g
    dx = r * (dxh - xh * jnp.mean(dxh * xh, axis=-1, keepdims=True))
    return dx, jnp.sum(dy * xh, axis=0, keepdims=True)


def _acc_out(ref, first, val):
    @pl.when(first)
    def _():
        ref[...] = val

    @pl.when(jnp.logical_not(first))
    def _():
        ref[...] += val


def _rope_tables(positions):
    half = ROPE_DIM // 2
    inv_freq = ROPE_THETA ** (-(jnp.arange(half, dtype=F32) * 2.0 / ROPE_DIM))
    ang = positions.astype(F32)[:, None] * inv_freq
    cos, sin = jnp.cos(ang), jnp.sin(ang)
    t = positions.shape[0]
    one, zero = jnp.ones((t, HEAD_DIM - ROPE_DIM), F32), jnp.zeros((t, HEAD_DIM - ROPE_DIM), F32)
    z8 = jnp.zeros((t, half), F32)
    c = jnp.concatenate([cos, cos, one], axis=1)
    sa = jnp.concatenate([z8, sin, zero], axis=1)
    sb = jnp.concatenate([-sin, z8, zero], axis=1)
    return tuple(jnp.tile(a, (1, 2)) for a in (c, sa, sb))


def _rope(t, c, sa, sb):
    return t * c + pltpu.roll(t, 8, 1) * sa + pltpu.roll(t, 120, 1) * sb


def _rope_bwd(d, c, sa, sb):
    return d * c + pltpu.roll(d * sa, 120, 1) + pltpu.roll(d * sb, 8, 1)


def _ev_in(x, gpre, w_in, rc, rsa, rsb):
    t = x.shape[0]
    tm = min(TM, t)

    def body(x_ref, g_ref, w_ref, c_ref, sa_ref, sb_ref, h_ref, zag_ref, q_ref, k_ref, v_ref):
        xv = x_ref[...]
        h = (xv * _rsqrt_mean(xv) * g_ref[...]).astype(BF16)
        h_ref[...] = h
        z = jnp.dot(h, w_ref[...], preferred_element_type=F32)
        zag_ref[...] = z[:, :2 * A_CH].astype(BF16)
        c, sa, sb = c_ref[...], sa_ref[...], sb_ref[...]
        q0 = 2 * A_CH
        for j in range(Q_DIM // 128):
            q_ref[:, 128 * j:128 * (j + 1)] = _rope(z[:, q0 + 128 * j:q0 + 128 * (j + 1)], c, sa, sb).astype(BF16)
        k0 = q0 + Q_DIM
        k_ref[...] = _rope(z[:, k0:k0 + KV_DIM], c, sa, sb).astype(BF16)
        v_ref[...] = z[:, k0 + KV_DIM:k0 + 2 * KV_DIM].astype(BF16)

    return pl.pallas_call(
        body, name="ev_in", grid=(t // tm,),
        in_specs=[_rows(tm, D_MODEL), _full((1, D_MODEL)), _full((D_MODEL, EVEN_IN)),
                  _rows(tm, 128), _rows(tm, 128), _rows(tm, 128)],
        out_specs=[_rows(tm, D_MODEL), _rows(tm, 2 * A_CH), _rows(tm, Q_DIM), _rows(tm, KV_DIM), _rows(tm, KV_DIM)],
        out_shape=[jax.ShapeDtypeStruct((t, D_MODEL), BF16), jax.ShapeDtypeStruct((t, 2 * A_CH), BF16),
                   jax.ShapeDtypeStruct((t, Q_DIM), BF16), jax.ShapeDtypeStruct((t, KV_DIM), BF16),
                   jax.ShapeDtypeStruct((t, KV_DIM), BF16)],
        compiler_params=_cp(("arbitrary",)),
    )(x, gpre, w_in, rc, rsa, rsb)


def _glu(zag):
    z = zag.astype(F32)
    return z[:, :A_CH] * jax.nn.sigmoid(z[:, A_CH:])


def _conf_fwd(zag, conv_w, conv_b, ln_g, ln_b):
    t = zag.shape[0]
    tm = min(TM_BWD, t)

    def body(z_ref, w_ref, b_ref, g_ref, lb_ref, c_ref, a_ref, ext):
        i = pl.program_id(0)

        @pl.when(i == 0)
        def _():
            ext[0:HALO31, :] = jnp.zeros((HALO31, A_CH), F32)

        ext[HALO31:HALO31 + tm, :] = _glu(z_ref[...])
        acc = jnp.zeros((tm, A_CH), F32)
        for j in range(A_CONV):
            s = HALO31 - (A_CONV - 1) + j
            acc = acc + w_ref[j:j + 1, :] * ext[s:s + tm, :]
        ext[0:HALO31, :] = ext[tm:tm + HALO31, :]
        cv = acc + b_ref[...]
        c_ref[...] = cv
        mu = jnp.mean(cv, axis=-1, keepdims=True)
        xc = cv - mu
        ln = xc * lax.rsqrt(jnp.mean(xc * xc, axis=-1, keepdims=True) + LN_EPS) * g_ref[...] + lb_ref[...]
        a_ref[...] = (ln * jax.nn.sigmoid(ln)).astype(BF16)

    return pl.pallas_call(
        body, name="conf_fwd", grid=(t // tm,),
        in_specs=[_rows(tm, 2 * A_CH), _full((32, A_CH)), _full((1, A_CH)), _full((1, A_CH)), _full((1, A_CH))],
        out_specs=[_rows(tm, A_CH), _rows(tm, A_CH)],
        out_shape=[jax.ShapeDtypeStruct((t, A_CH), F32), jax.ShapeDtypeStruct((t, A_CH), BF16)],
        scratch_shapes=[pltpu.VMEM((HALO31 + tm, A_CH), F32)],
        compiler_params=_cp(("arbitrary",)),
    )(zag, conv_w, conv_b, ln_g, ln_b)


def _attn_mask(first_block):
    row = lax.broadcasted_iota(jnp.int32, (GROUP * BLOCK, 2 * BLOCK), 0) & (BLOCK - 1)
    col = lax.broadcasted_iota(jnp.int32, (GROUP * BLOCK, 2 * BLOCK), 1)
    diff = row + BLOCK - col
    return (diff >= 0) & (diff < BLOCK) & ((col >= BLOCK) | jnp.logical_not(first_block))


def _sink_rows(s_ref, h):
    grp = lax.broadcasted_iota(jnp.int32, (GROUP * BLOCK, 1), 0) >> 7
    out = jnp.full((GROUP * BLOCK, 1), s_ref[GROUP * h], F32)
    for g in range(1, GROUP):
        out = jnp.where(grp == g, s_ref[GROUP * h + g], out)
    return out


def _attn_probs(q4, k2, mask, sink):
    s = lax.dot_general(q4, k2, _CONTRACT_LAST, preferred_element_type=F32) * SCALE
    s = jnp.where(mask, s, NEG)
    m = jnp.maximum(jnp.max(s, axis=-1, keepdims=True), sink)
    p = jnp.exp(s - m)
    es = jnp.exp(sink - m)
    inv = 1.0 / (jnp.sum(p, axis=-1, keepdims=True) + es)
    return p * inv, es * inv


def _q_heads(q, h):
    return jnp.concatenate([q[:, HEAD_DIM * (GROUP * h + g):HEAD_DIM * (GROUP * h + g + 1)] for g in range(GROUP)],
                           axis=0)


def _kv_head(prev, cur, h):
    return jnp.concatenate([prev[:, HEAD_DIM * h:HEAD_DIM * (h + 1)], cur[:, HEAD_DIM * h:HEAD_DIM * (h + 1)]], axis=0)


def _attn_fwd(q, k, v, sinks):
    t = q.shape[0]
    nb = min(ATT_NB, t // BLOCK)
    rows = nb * BLOCK

    def body(s_ref, q_ref, kc_ref, kp_ref, vc_ref, vp_ref, o_ref):
        first = pl.program_id(0) == 0
        for b in range(nb):
            lo = BLOCK * b
            mask = _attn_mask(first) if b == 0 else _attn_mask(False)
            qv, kc, vc = q_ref[lo:lo + BLOCK, :], kc_ref[lo:lo + BLOCK, :], vc_ref[lo:lo + BLOCK, :]
            kp = kp_ref[...] if b == 0 else kc_ref[lo - BLOCK:lo, :]
            vp = vp_ref[...] if b == 0 else vc_ref[lo - BLOCK:lo, :]
            for h in range(N_KV_HEADS):
                pn, _ = _attn_probs(_q_heads(qv, h), _kv_head(kp, kc, h), mask, _sink_rows(s_ref, h))
                o4 = jnp.dot(pn.astype(BF16), _kv_head(vp, vc, h), preferred_element_type=F32).astype(BF16)
                for g in range(GROUP):
                    c0 = HEAD_DIM * (GROUP * h + g)
                    o_ref[lo:lo + BLOCK, c0:c0 + HEAD_DIM] = o4[BLOCK * g:BLOCK * (g + 1), :]

    cur = lambda n: pl.BlockSpec((rows, n), lambda i: (i, 0))
    prev = lambda n: pl.BlockSpec((BLOCK, n), lambda i: (jnp.maximum(i * nb - 1, 0), 0))
    return pl.pallas_call(
        body, name="attn_fwd", grid=(t // rows,),
        in_specs=[pl.BlockSpec(memory_space=pltpu.SMEM), cur(Q_DIM), cur(KV_DIM), prev(KV_DIM), cur(KV_DIM),
                  prev(KV_DIM)],
        out_specs=cur(Q_DIM),
        out_shape=jax.ShapeDtypeStruct((t, Q_DIM), BF16),
        compiler_params=_cp(("arbitrary",)),
    )(sinks, q, k, k, v, v)


def _out_post(lhs, ws, x_in, gpost):
    t = x_in.shape[0]
    tm = min(TM, t)
    n = len(lhs)

    def body(*refs):
        x_ref, g_ref, m_ref, xo_ref = refs[2 * n:]
        m = jnp.dot(refs[0][...], refs[n][...], preferred_element_type=F32)
        for j in range(1, n):
            m = m + jnp.dot(refs[j][...], refs[n + j][...], preferred_element_type=F32)
        m_ref[...] = m
        xo_ref[...] = x_ref[...] + m * _rsqrt_mean(m) * g_ref[...]

    return pl.pallas_call(
        body, name="out_post", grid=(t // tm,),
        in_specs=[_rows(tm, a.shape[1]) for a in lhs] + [_full(w.shape) for w in ws]
                 + [_rows(tm, D_MODEL), _full((1, D_MODEL))],
        out_specs=[_rows(tm, D_MODEL), _rows(tm, D_MODEL)],
        out_shape=[jax.ShapeDtypeStruct((t, D_MODEL), F32)] * 2,
        compiler_params=_cp(("arbitrary",)),
    )(*lhs, *ws, x_in, gpost)


def _conv3(w_ref, ext, tm):
    s = HALO3 - 2
    return (w_ref[0:1, :] * ext[s:s + tm, :] + w_ref[1:2, :] * ext[s + 1:s + 1 + tm, :]
            + w_ref[2:3, :] * ext[s + 2:s + 2 + tm, :])


def _ffn_fwd(x1, gpre, wup, layer, cw, wd, gpost):
    t = x1.shape[0]
    tm = min(TM, t)
    nc, n = wup.shape[1], wup.shape[4]

    def body(x_ref, gpre_ref, wup_ref, cw_ref, wd_ref, gpost_ref, h_ref, up_ref, u_ref, f_ref, xo_ref, h_s, acc, ext, hal):
        i, c = pl.program_id(0), pl.program_id(1)

        @pl.when(c == 0)
        def _():
            xv = x_ref[...]
            h = (xv * _rsqrt_mean(xv) * gpre_ref[...]).astype(BF16)
            h_s[...] = h
            h_ref[...] = h

        @pl.when(i == 0)
        def _():
            hal[c] = jnp.zeros((2, HALO3, n), F32)

        u = []
        for gv in range(2):
            up = jnp.dot(h_s[...], wup_ref[gv, 0, 0], preferred_element_type=F32)
            up_ref[gv, 0] = up.astype(BF16)
            ext[gv, 0:HALO3, :] = hal[c, gv]
            ext[gv, HALO3:HALO3 + tm, :] = up
            hal[c, gv] = ext[gv, tm:tm + HALO3, :]
            s = HALO3 - 2
            u.append(cw_ref[gv, 0, 0:1, :] * ext[gv, s:s + tm, :] + cw_ref[gv, 0, 1:2, :] * ext[gv, s + 1:s + 1 + tm, :]
                     + cw_ref[gv, 0, 2:3, :] * up)
            u_ref[gv, 0] = u[gv].astype(BF16)
        act = (u[0] * _sigmoid(u[0]) * u[1]).astype(BF16)
        part = jnp.dot(act, wd_ref[...], preferred_element_type=F32)

        @pl.when(c == 0)
        def _():
            acc[...] = part

        @pl.when(jnp.logical_and(c > 0, c < nc - 1))
        def _():
            acc[...] += part

        @pl.when(c == nc - 1)
        def _():
            f = acc[...] + part
            f_ref[...] = f
            xo_ref[...] = x_ref[...] + f * _rsqrt_mean(f) * gpost_ref[...]

    row = lambda w: pl.BlockSpec((tm, w), lambda i, c: (i, 0))
    one = _full((1, D_MODEL))
    return pl.pallas_call(
        body, name="ffn_fwd", grid=(t // tm, nc),
        in_specs=[row(D_MODEL), one, pl.BlockSpec((2, 1, 1, D_MODEL, n), lambda i, c: (0, c, layer, 0, 0)),
                  pl.BlockSpec((2, 1, 3, n), lambda i, c: (0, c, 0, 0)), pl.BlockSpec((n, D_MODEL), lambda i, c: (c, 0)),
                  one],
        out_specs=[row(D_MODEL), pl.BlockSpec((2, 1, tm, n), lambda i, c: (0, c, i, 0)),
                   pl.BlockSpec((2, 1, tm, n), lambda i, c: (0, c, i, 0)), row(D_MODEL), row(D_MODEL)],
        out_shape=[jax.ShapeDtypeStruct((t, D_MODEL), BF16), jax.ShapeDtypeStruct((2, nc, t, n), BF16),
                   jax.ShapeDtypeStruct((2, nc, t, n), BF16), jax.ShapeDtypeStruct((t, D_MODEL), F32),
                   jax.ShapeDtypeStruct((t, D_MODEL), F32)],
        scratch_shapes=[pltpu.VMEM((tm, D_MODEL), BF16), pltpu.VMEM((tm, D_MODEL), F32),
                        pltpu.VMEM((2, HALO3 + tm, n), F32), pltpu.VMEM((nc, 2, HALO3, n), F32)],
        compiler_params=_cp(("arbitrary", "arbitrary")),
    )(x1, gpre, wup, cw, wd, gpost)


def _od_fwd(x_in, gpre, w_in, cw, w_out, gpost):
    t = x_in.shape[0]
    tm = min(TM, t)
    ns, _, n = w_in.shape

    def body(x_ref, gpre_ref, w_ref, cw_ref, wo_ref, gpost_ref, h_ref, z_ref, cv_ref, y_ref, m_ref, xo_ref, z_s, ext):
        i = pl.program_id(0)
        xv = x_ref[...]
        h = (xv * _rsqrt_mean(xv) * gpre_ref[...]).astype(BF16)
        h_ref[...] = h
        for j in range(ns):
            z_s[:, n * j:n * (j + 1)] = jnp.dot(h, w_ref[j], preferred_element_type=F32)
        z_ref[...] = z_s[...].astype(BF16)

        @pl.when(i == 0)
        def _():
            ext[0:HALO3, :] = jnp.zeros((HALO3, SC_DIM), F32)

        ext[HALO3:HALO3 + tm, :] = z_s[:, SC_DIM:2 * SC_DIM] * z_s[:, 2 * SC_DIM:]
        cv = _conv3(cw_ref, ext, tm)
        cv_ref[...] = cv.astype(BF16)
        y = (z_s[:, :SC_DIM] * cv).astype(BF16)
        ext[0:HALO3, :] = ext[tm:tm + HALO3, :]
        y_ref[...] = y
        m = jnp.dot(y, wo_ref[...], preferred_element_type=F32)
        m_ref[...] = m
        xo_ref[...] = xv + m * _rsqrt_mean(m) * gpost_ref[...]

    return pl.pallas_call(
        body, name="od_fwd", grid=(t // tm,),
        in_specs=[_rows(tm, D_MODEL), _full((1, D_MODEL)), _full((ns, D_MODEL, n)), _full((3, SC_DIM)),
                  _full((SC_DIM, D_MODEL)), _full((1, D_MODEL))],
        out_specs=[_rows(tm, D_MODEL), _rows(tm, 3 * SC_DIM), _rows(tm, SC_DIM), _rows(tm, SC_DIM), _rows(tm, D_MODEL),
                   _rows(tm, D_MODEL)],
        out_shape=[jax.ShapeDtypeStruct((t, D_MODEL), BF16), jax.ShapeDtypeStruct((t, 3 * SC_DIM), BF16),
                   jax.ShapeDtypeStruct((t, SC_DIM), BF16), jax.ShapeDtypeStruct((t, SC_DIM), BF16),
                   jax.ShapeDtypeStruct((t, D_MODEL), F32), jax.ShapeDtypeStruct((t, D_MODEL), F32)],
        scratch_shapes=[pltpu.VMEM((tm, 3 * SC_DIM), F32), pltpu.VMEM((HALO3 + tm, SC_DIM), F32)],
        compiler_params=_cp(("arbitrary",)),
    )(x_in, gpre, w_in, cw, w_out, gpost)


def _loss_grad(y, target):
    t = y.shape[0]
    tm = min(TM, t)

    def body(y_ref, t_ref, dy_ref, l_ref):
        e = y_ref[...] - t_ref[...]
        dy_ref[...] = e * (1.0 / D_MODEL)
        part = jnp.zeros((1, 128), F32) + jnp.sum(e * e) * (0.5 / D_MODEL)
        _acc_out(l_ref, pl.program_id(0) == 0, part)

    return pl.pallas_call(
        body, name="loss_grad", grid=(t // tm,),
        in_specs=[_rows(tm, D_MODEL), _rows(tm, D_MODEL)],
        out_specs=[_rows(tm, D_MODEL), _full((1, 128))],
        out_shape=[jax.ShapeDtypeStruct((t, D_MODEL), F32), jax.ShapeDtypeStruct((1, 128), F32)],
        compiler_params=_cp(("arbitrary",)),
    )(y, target)


def _dw2d(a, b, bm, bn):
    t, m = a.shape
    n = b.shape[1]
    tk = min(TK_DW, t)

    def body(a_ref, b_ref, o_ref):
        part = lax.dot_general(a_ref[...], b_ref[...], _CONTRACT_FIRST, preferred_element_type=F32)
        _acc_out(o_ref, pl.program_id(2) == 0, part)

    return pl.pallas_call(
        body, name="dw2d", grid=(m // bm, n // bn, t // tk),
        in_specs=[pl.BlockSpec((tk, bm), lambda i, j, k: (k, i)), pl.BlockSpec((tk, bn), lambda i, j, k: (k, j))],
        out_specs=pl.BlockSpec((bm, bn), lambda i, j, k: (i, j)),
        out_shape=jax.ShapeDtypeStruct((m, n), F32),
        compiler_params=_cp(("arbitrary", "arbitrary", "arbitrary")),
    )(a, b)


def _dw_cols(a, b, n_blk):
    t, m = a.shape
    s = b.shape[1] // n_blk
    tk = min(TK_DW, t)

    def body(a_ref, b_ref, o_ref):
        part = lax.dot_general(a_ref[...], b_ref[...], _CONTRACT_FIRST, preferred_element_type=F32)
        _acc_out(o_ref.at[0], pl.program_id(1) == 0, part)

    return pl.pallas_call(
        body, name="dw_cols", grid=(s, t // tk),
        in_specs=[pl.BlockSpec((tk, m), lambda j, k: (k, 0)), pl.BlockSpec((tk, n_blk), lambda j, k: (k, j))],
        out_specs=pl.BlockSpec((1, m, n_blk), lambda j, k: (j, 0, 0)),
        out_shape=jax.ShapeDtypeStruct((s, m, n_blk), F32),
        compiler_params=_cp(("arbitrary", "arbitrary")),
    )(a, b)


def _dw_up(h, dup, layer, buf):
    t, m = h.shape
    s, _, n = dup.shape
    tk = min(TK_DW, t)

    def body(*refs):
        a_ref, b_ref, o_ref = refs[0], refs[1], refs[-1]
        part = lax.dot_general(a_ref[...], b_ref[0], _CONTRACT_FIRST, preferred_element_type=F32)
        _acc_out(o_ref.at[0, 0], pl.program_id(1) == 0, part)

    ins = [h, dup] + ([] if buf is None else [buf])
    return pl.pallas_call(
        body, name="dw_up", grid=(s, t // tk),
        in_specs=[pl.BlockSpec((tk, m), lambda j, k: (k, 0)), pl.BlockSpec((1, tk, n), lambda j, k: (j, k, 0))]
                 + ([] if buf is None else [_ANY]),
        out_specs=pl.BlockSpec((1, 1, m, n), lambda j, k: (j, layer, 0, 0)),
        out_shape=jax.ShapeDtypeStruct((s, 2, m, n), F32),
        input_output_aliases={} if buf is None else {2: 0},
        compiler_params=_cp(("arbitrary", "arbitrary")),
    )(*ins)


def _dw_down(act, df, layer, buf):
    nc, t, n = act.shape
    d = df.shape[1]
    tk = min(TK_DW, t)

    def body(*refs):
        a_ref, b_ref, o_ref = refs[0], refs[1], refs[-1]
        part = lax.dot_general(a_ref[0], b_ref[...], _CONTRACT_FIRST, preferred_element_type=F32)
        part = part.reshape(2, n // 2, d)
        first = pl.program_id(1) == 0

        @pl.when(first)
        def _():
            o_ref[:, 0] = part

        @pl.when(jnp.logical_not(first))
        def _():
            o_ref[:, 0] += part

    ins = [act, df] + ([] if buf is None else [buf])
    return pl.pallas_call(
        body, name="dw_down", grid=(nc, t // tk),
        in_specs=[pl.BlockSpec((1, tk, n), lambda c, k: (c, k, 0)), pl.BlockSpec((tk, d), lambda c, k: (k, 0))]
                 + ([] if buf is None else [_ANY]),
        out_specs=pl.BlockSpec((2, 1, n // 2, d), lambda c, k: (c, layer, 0, 0)),
        out_shape=jax.ShapeDtypeStruct((2 * nc, 2, n // 2, d), F32),
        input_output_aliases={} if buf is None else {2: 0},
        compiler_params=_cp(("arbitrary", "arbitrary")),
    )(*ins)


def _dz_wt_rms_bwd(dz, wt, x_in, gpre, dres):
    t, n = dz.shape
    tm = min(TM, t)

    def body(dz_ref, wt_ref, x_ref, g_ref, dres_ref, dx_ref, dg_ref):
        dh = jnp.dot(dz_ref[...], wt_ref[...], preferred_element_type=F32)
        dx, dg = _rms_bwd(x_ref[...], g_ref[...], dh)
        dx_ref[...] = dres_ref[...] + dx
        _acc_out(dg_ref, pl.program_id(0) == 0, dg)

    return pl.pallas_call(
        body, name="dz_wt_rms_bwd", grid=(t // tm,),
        in_specs=[_rows(tm, n), _full((n, D_MODEL)), _rows(tm, D_MODEL), _full((1, D_MODEL)), _rows(tm, D_MODEL)],
        out_specs=[_rows(tm, D_MODEL), _full((1, D_MODEL))],
        out_shape=[jax.ShapeDtypeStruct((t, D_MODEL), F32), jax.ShapeDtypeStruct((1, D_MODEL), F32)],
        compiler_params=_cp(("arbitrary",)),
    )(dz, wt, x_in, gpre, dres)


def _shift_matrices(shift, shift_h, tm, hb):
    row = lax.broadcasted_iota(jnp.int32, (2 * tm, tm), 0)
    col = lax.broadcasted_iota(jnp.int32, (2 * tm, tm), 1)
    hit = ((row < tm) & (col == row + 1)) | ((row >= tm) & (col == row - tm + 2))
    shift[...] = jnp.where(hit, 1.0, 0.0).astype(BF16)
    row = lax.broadcasted_iota(jnp.int32, (hb, hb), 0)
    col = lax.broadcasted_iota(jnp.int32, (hb, hb), 1)
    hit = ((row < HALO3) & (col == row - (HALO3 - 1))) | ((row >= HALO3) & (col == row - (2 * HALO3 - 2)))
    shift_h[...] = jnp.where(hit, 1.0, 0.0).astype(BF16)


def _next_rows(shift, shift_h, xb, nxt, d12_s, tm):
    d12_s[...] = jnp.dot(shift[...], xb, preferred_element_type=F32)
    edge = jnp.dot(shift_h[...], nxt, preferred_element_type=F32)
    d12_s[tm - HALO3:tm, :] += edge[0:HALO3, :]
    d12_s[2 * tm - HALO3:2 * tm, :] += edge[HALO3:2 * HALO3, :]


def _ffn_bwd(f, dxo, gpost, x_in, gpre, up, u, cw, wdt, wupt):
    t = f.shape[0]
    tm = min(TM_BWD, t)
    nt = t // tm
    nc, n = up.shape[1], up.shape[3]
    hb = 2 * HALO3

    def body(f_ref, dxo_ref, gpost_ref, x_ref, gpre_ref, up_ref, u_ref, cw_ref, wdt_ref, wupt_ref,
             df_ref, act_ref, dup_ref, dx_ref, dgpost_ref, dgpre_ref, dcw_ref, df_s, acc, du_s, dub_s, d12_s, hal, shift, shift_h):
        i, c = pl.program_id(0), pl.program_id(1)

        @pl.when(c == 0)
        def _():
            df, dg = _rms_bwd(f_ref[...], gpost_ref[...], dxo_ref[...])
            df_s[...] = df.astype(BF16)
            df_ref[...] = df.astype(BF16)
            _acc_out(dgpost_ref, i == 0, dg)

        @pl.when(i == 0)
        def _():
            hal[c] = jnp.zeros((2, hb, n), BF16)
            dcw_ref[0, c] = jnp.zeros((8, n), F32)
            dcw_ref[1, c] = jnp.zeros((8, n), F32)

        @pl.when(jnp.logical_and(i == 0, c == 0))
        def _():
            _shift_matrices(shift, shift_h, tm, hb)

        dact = jnp.dot(df_s[...], wdt_ref[0], preferred_element_type=F32)
        g, v = u_ref[0, 0].astype(F32), u_ref[1, 0].astype(F32)
        sg = _sigmoid(g)
        sil = g * sg
        act_ref[0] = (sil * v).astype(BF16)
        dug = dact * v * (sg + sil * (1.0 - sg))
        duv = dact * sil
        du_s[0], du_s[1] = dug, duv
        dub_s[0], dub_s[1] = dug.astype(BF16), duv.astype(BF16)
        dh = None
        for gv in range(2):
            _next_rows(shift, shift_h, dub_s[gv], hal[c, gv], d12_s, tm)
            hal[c, gv] = dub_s[gv, 0:hb, :]
            du, d1, d2 = du_s[gv], d12_s[0:tm, :], d12_s[tm:2 * tm, :]
            dup = (cw_ref[gv, 0, 2:3, :] * du + cw_ref[gv, 0, 1:2, :] * d1 + cw_ref[gv, 0, 0:1, :] * d2).astype(BF16)
            dup_ref[gv, 0] = dup
            upc = up_ref[gv, 0].astype(F32)
            dcw_ref[gv, c, 2:3, :] += jnp.sum(upc * du, axis=0, keepdims=True)
            dcw_ref[gv, c, 1:2, :] += jnp.sum(upc * d1, axis=0, keepdims=True)
            dcw_ref[gv, c, 0:1, :] += jnp.sum(upc * d2, axis=0, keepdims=True)
            part = jnp.dot(dup, wupt_ref[gv, 0], preferred_element_type=F32)
            dh = part if dh is None else dh + part
        _acc_out(acc, c == 0, dh)

        @pl.when(c == nc - 1)
        def _():
            dx, dg = _rms_bwd(x_ref[...], gpre_ref[...], acc[...])
            dx_ref[...] = dxo_ref[...] + dx
            _acc_out(dgpre_ref, i == 0, dg)

    rrow = lambda w: pl.BlockSpec((tm, w), lambda i, c: (nt - 1 - i, 0))
    blk = pl.BlockSpec((2, 1, tm, n), lambda i, c: (0, c, nt - 1 - i, 0))
    one = _full((1, D_MODEL))
    return pl.pallas_call(
        body, name="ffn_bwd", grid=(nt, nc),
        in_specs=[rrow(D_MODEL), rrow(D_MODEL), one, rrow(D_MODEL), one, blk, blk,
                  pl.BlockSpec((2, 1, 3, n), lambda i, c: (0, c, 0, 0)),
                  pl.BlockSpec((1, D_MODEL, n), lambda i, c: (c, 0, 0)),
                  pl.BlockSpec((2, 1, n, D_MODEL), lambda i, c: (0, c, 0, 0))],
        out_specs=[rrow(D_MODEL), pl.BlockSpec((1, tm, n), lambda i, c: (c, nt - 1 - i, 0)), blk, rrow(D_MODEL),
                   one, one, _full((2, nc, 8, n))],
        out_shape=[jax.ShapeDtypeStruct((t, D_MODEL), BF16), jax.ShapeDtypeStruct((nc, t, n), BF16),
                   jax.ShapeDtypeStruct((2, nc, t, n), BF16), jax.ShapeDtypeStruct((t, D_MODEL), F32),
                   jax.ShapeDtypeStruct((1, D_MODEL), F32), jax.ShapeDtypeStruct((1, D_MODEL), F32),
                   jax.ShapeDtypeStruct((2, nc, 8, n), F32)],
        scratch_shapes=[pltpu.VMEM((tm, D_MODEL), BF16), pltpu.VMEM((tm, D_MODEL), F32),
                        pltpu.VMEM((2, tm, n), F32), pltpu.VMEM((2, tm, n), BF16), pltpu.VMEM((2 * tm, n), F32),
                        pltpu.VMEM((nc, 2, hb, n), BF16), pltpu.VMEM((2 * tm, tm), BF16), pltpu.VMEM((hb, hb), BF16)],
        compiler_params=_cp(("arbitrary", "arbitrary")),
    )(f, dxo, gpost, x_in, gpre, up, u, cw, wdt, wupt)


def _od_bwd(m, dxo, gpost, x_in, gpre, z, cv, cw, wot, wint):
    t = m.shape[0]
    tm = min(TM_BWD, t)
    nt = t // tm
    hb = 2 * HALO3

    def body(m_ref, dxo_ref, gpost_ref, x_ref, gpre_ref, z_ref, cv_ref, cw_ref, wot_ref, wint_ref,
             dm_ref, dz_ref, dx_ref, dgpost_ref, dgpre_ref, dcw_ref, dcvb_s, d12_s, dz_s, hal, shift, shift_h):
        i = pl.program_id(0)
        dxo = dxo_ref[...]
        dm, dg = _rms_bwd(m_ref[...], gpost_ref[...], dxo)
        dmb = dm.astype(BF16)
        dm_ref[...] = dmb
        _acc_out(dgpost_ref, i == 0, dg)

        @pl.when(i == 0)
        def _():
            hal[...] = jnp.zeros((hb, SC_DIM), BF16)
            dcw_ref[...] = jnp.zeros((8, SC_DIM), F32)
            _shift_matrices(shift, shift_h, tm, hb)

        dy = jnp.dot(dmb, wot_ref[...], preferred_element_type=F32)
        z = z_ref[...].astype(F32)
        b, cg, u = z[:, :SC_DIM], z[:, SC_DIM:2 * SC_DIM], z[:, 2 * SC_DIM:]
        dz_s[:, 0:SC_DIM] = (dy * cv_ref[...].astype(F32)).astype(BF16)
        dcv = dy * b
        dcvb_s[...] = dcv.astype(BF16)
        _next_rows(shift, shift_h, dcvb_s[...], hal[...], d12_s, tm)
        hal[...] = dcvb_s[0:hb, :]
        d1, d2 = d12_s[0:tm, :], d12_s[tm:2 * tm, :]
        dcu = cw_ref[2:3, :] * dcv + cw_ref[1:2, :] * d1 + cw_ref[0:1, :] * d2
        cu = cg * u
        dcw_ref[2:3, :] += jnp.sum(cu * dcv, axis=0, keepdims=True)
        dcw_ref[1:2, :] += jnp.sum(cu * d1, axis=0, keepdims=True)
        dcw_ref[0:1, :] += jnp.sum(cu * d2, axis=0, keepdims=True)
        dz_s[:, SC_DIM:2 * SC_DIM] = (dcu * u).astype(BF16)
        dz_s[:, 2 * SC_DIM:3 * SC_DIM] = (dcu * cg).astype(BF16)
        dz_ref[...] = dz_s[...]
        dh = jnp.dot(dz_s[...], wint_ref[...], preferred_element_type=F32)
        dx, dg2 = _rms_bwd(x_ref[...], gpre_ref[...], dh)
        dx_ref[...] = dxo + dx
        _acc_out(dgpre_ref, i == 0, dg2)

    rrow = lambda w: pl.BlockSpec((tm, w), lambda i: (nt - 1 - i, 0))
    one = _full((1, D_MODEL))
    return pl.pallas_call(
        body, name="od_bwd", grid=(nt,),
        in_specs=[rrow(D_MODEL), rrow(D_MODEL), one, rrow(D_MODEL), one, rrow(3 * SC_DIM), rrow(SC_DIM),
                  _full((3, SC_DIM)), _full((D_MODEL, SC_DIM)), _full((3 * SC_DIM, D_MODEL))],
        out_specs=[rrow(D_MODEL), rrow(3 * SC_DIM), rrow(D_MODEL), one, one, _full((8, SC_DIM))],
        out_shape=[jax.ShapeDtypeStruct((t, D_MODEL), BF16), jax.ShapeDtypeStruct((t, 3 * SC_DIM), BF16),
                   jax.ShapeDtypeStruct((t, D_MODEL), F32), jax.ShapeDtypeStruct((1, D_MODEL), F32),
                   jax.ShapeDtypeStruct((1, D_MODEL), F32), jax.ShapeDtypeStruct((8, SC_DIM), F32)],
        scratch_shapes=[pltpu.VMEM((tm, SC_DIM), BF16), pltpu.VMEM((2 * tm, SC_DIM), F32),
                        pltpu.VMEM((tm, 3 * SC_DIM), BF16), pltpu.VMEM((hb, SC_DIM), BF16),
                        pltpu.VMEM((2 * tm, tm), BF16), pltpu.VMEM((hb, hb), BF16)],
        compiler_params=_cp(("arbitrary",)),
    )(m, dxo, gpost, x_in, gpre, z, cv, cw, wot, wint)


def _ev_bwd1(m, dxo, gpost, wot):
    t = m.shape[0]
    tm = min(TM, t)

    def body(m_ref, dxo_ref, g_ref, wot_ref, dm_ref, da_ref, do_ref, dg_ref):
        dm, dg = _rms_bwd(m_ref[...], g_ref[...], dxo_ref[...])
        dmb = dm.astype(BF16)
        dm_ref[...] = dmb
        _acc_out(dg_ref, pl.program_id(0) == 0, dg)
        dao = jnp.dot(dmb, wot_ref[...], preferred_element_type=F32)
        da_ref[...] = dao[:, :A_CH]
        do_ref[...] = dao[:, A_CH:].astype(BF16)

    return pl.pallas_call(
        body, name="ev_bwd1", grid=(t // tm,),
        in_specs=[_rows(tm, D_MODEL), _rows(tm, D_MODEL), _full((1, D_MODEL)), _full((D_MODEL, A_CH + Q_DIM))],
        out_specs=[_rows(tm, D_MODEL), _rows(tm, A_CH), _rows(tm, Q_DIM), _full((1, D_MODEL))],
        out_shape=[jax.ShapeDtypeStruct((t, D_MODEL), BF16), jax.ShapeDtypeStruct((t, A_CH), F32),
                   jax.ShapeDtypeStruct((t, Q_DIM), BF16), jax.ShapeDtypeStruct((1, D_MODEL), F32)],
        compiler_params=_cp(("arbitrary",)),
    )(m, dxo, gpost, wot)


def _conf_bwd(da, cv, zag, conv_w, ln_g, ln_b):
    t = da.shape[0]
    tm = min(TM_BWD, t)
    nt = t // tm

    def body(da_ref, c_ref, z_ref, zh_ref, w_ref, g_ref, lb_ref, dz_ref, dw_ref, dv_ref, ext_in, ext_out):
        i = pl.program_id(0)
        r = nt - 1 - i

        @pl.when(i == 0)
        def _():
            ext_out[tm:tm + HALO31, :] = jnp.zeros((HALO31, A_CH), F32)
            dw_ref[...] = jnp.zeros((32, A_CH), F32)
            dv_ref[...] = jnp.zeros((8, A_CH), F32)

        x = c_ref[...]
        mu = jnp.mean(x, axis=-1, keepdims=True)
        xc = x - mu
        rstd = lax.rsqrt(jnp.mean(xc * xc, axis=-1, keepdims=True) + LN_EPS)
        xh = xc * rstd
        ln = xh * g_ref[...] + lb_ref[...]
        sl = jax.nn.sigmoid(ln)
        dln = da_ref[...] * (sl * (1.0 + ln * (1.0 - sl)))
        dxh = dln * g_ref[...]
        dc = rstd * (dxh - jnp.mean(dxh, axis=-1, keepdims=True) - xh * jnp.mean(dxh * xh, axis=-1, keepdims=True))
        dv_ref[0:1, :] += jnp.sum(dc, axis=0, keepdims=True)
        dv_ref[1:2, :] += jnp.sum(dln * xh, axis=0, keepdims=True)
        dv_ref[2:3, :] += jnp.sum(dln, axis=0, keepdims=True)

        ext_out[0:tm, :] = dc
        dglu = jnp.zeros((tm, A_CH), F32)
        for j in range(A_CONV):
            s = A_CONV - 1 - j
            dglu = dglu + w_ref[j:j + 1, :] * ext_out[s:s + tm, :]
        ext_out[tm:tm + HALO31, :] = ext_out[0:HALO31, :]

        ext_in[0:HALO31, :] = jnp.where(r > 0, _glu(zh_ref[...]), 0.0)
        z = z_ref[...].astype(F32)
        al, sg = z[:, :A_CH], jax.nn.sigmoid(z[:, A_CH:])
        ext_in[HALO31:HALO31 + tm, :] = al * sg
        for j in range(A_CONV):
            s = HALO31 - (A_CONV - 1) + j
            dw_ref[j:j + 1, :] += jnp.sum(dc * ext_in[s:s + tm, :], axis=0, keepdims=True)
        dz_ref[:, 0:A_CH] = (dglu * sg).astype(BF16)
        dz_ref[:, A_CH:2 * A_CH] = (dglu * al * sg * (1.0 - sg)).astype(BF16)

    rrow = lambda w: pl.BlockSpec((tm, w), lambda i: (nt - 1 - i, 0))
    halo = pl.BlockSpec((HALO31, 2 * A_CH), lambda i: (jnp.maximum((nt - 1 - i) * (tm // HALO31) - 1, 0), 0))
    return pl.pallas_call(
        body, name="conf_bwd", grid=(nt,),
        in_specs=[rrow(A_CH), rrow(A_CH), rrow(2 * A_CH), halo, _full((32, A_CH)), _full((1, A_CH)),
                  _full((1, A_CH))],
        out_specs=[rrow(2 * A_CH), _full((32, A_CH)), _full((8, A_CH))],
        out_shape=[jax.ShapeDtypeStruct((t, 2 * A_CH), BF16), jax.ShapeDtypeStruct((32, A_CH), F32),
                   jax.ShapeDtypeStruct((8, A_CH), F32)],
        scratch_shapes=[pltpu.VMEM((HALO31 + tm, A_CH), F32), pltpu.VMEM((tm + HALO31, A_CH), F32)],
        compiler_params=_cp(("arbitrary",)),
    )(da, cv, zag, zag, conv_w, ln_g, ln_b)


def _attn_bwd(q, k, v, do, sinks):
    t = q.shape[0]
    nb = min(ATT_NB, t // BLOCK)
    rows = nb * BLOCK
    ns = t // rows

    def body(s_ref, q_ref, kc_ref, kp_ref, vc_ref, vp_ref, do_ref, dq_ref, dk_ref, dv_ref, ds_ref, dkc, dvc):
        i = pl.program_id(0)
        r = ns - 1 - i

        @pl.when(i == 0)
        def _():
            dkc[...] = jnp.zeros_like(dkc)
            dvc[...] = jnp.zeros_like(dvc)
            ds_ref[...] = jnp.zeros_like(ds_ref)

        lane = lax.broadcasted_iota(jnp.int32, (1, N_Q_HEADS), 1)
        dsv = jnp.zeros((1, N_Q_HEADS), F32)
        for b in range(nb - 1, -1, -1):
            lo = BLOCK * b
            mask = _attn_mask(r == 0) if b == 0 else _attn_mask(False)
            qv, dov = q_ref[lo:lo + BLOCK, :], do_ref[lo:lo + BLOCK, :]
            kc, vc = kc_ref[lo:lo + BLOCK, :], vc_ref[lo:lo + BLOCK, :]
            kp = kp_ref[...] if b == 0 else kc_ref[lo - BLOCK:lo, :]
            vp = vp_ref[...] if b == 0 else vc_ref[lo - BLOCK:lo, :]
            for h in range(N_KV_HEADS):
                q4, do4 = _q_heads(qv, h), _q_heads(dov, h)
                k2, v2 = _kv_head(kp, kc, h), _kv_head(vp, vc, h)
                pn, ps = _attn_probs(q4, k2, mask, _sink_rows(s_ref, h))
                dp = lax.dot_general(do4, v2, _CONTRACT_LAST, preferred_element_type=F32)
                dl = jnp.sum(pn * dp, axis=-1, keepdims=True)
                dsb = (pn * (dp - dl)).astype(BF16)
                dq4 = (jnp.dot(dsb, k2, preferred_element_type=F32) * SCALE).astype(BF16)
                for g in range(GROUP):
                    c0 = HEAD_DIM * (GROUP * h + g)
                    dq_ref[lo:lo + BLOCK, c0:c0 + HEAD_DIM] = dq4[BLOCK * g:BLOCK * (g + 1), :]
                dk2 = lax.dot_general(dsb, q4, _CONTRACT_FIRST, preferred_element_type=F32) * SCALE
                dv2 = lax.dot_general(pn.astype(BF16), do4, _CONTRACT_FIRST, preferred_element_type=F32)
                dk_ref[lo:lo + BLOCK, HEAD_DIM * h:HEAD_DIM * (h + 1)] = dk2[BLOCK:, :] + dkc[h]
                dv_ref[lo:lo + BLOCK, HEAD_DIM * h:HEAD_DIM * (h + 1)] = dv2[BLOCK:, :] + dvc[h]
                dkc[h] = dk2[:BLOCK, :]
                dvc[h] = dv2[:BLOCK, :]
                srow = -ps * dl
                for g in range(GROUP):
                    dsv = dsv + jnp.where(lane == GROUP * h + g, jnp.sum(srow[BLOCK * g:BLOCK * (g + 1), :]), 0.0)
        ds_ref[...] += dsv

    cur = lambda n: pl.BlockSpec((rows, n), lambda i: (ns - 1 - i, 0))
    prev = lambda n: pl.BlockSpec((BLOCK, n), lambda i: (jnp.maximum((ns - 1 - i) * nb - 1, 0), 0))
    return pl.pallas_call(
        body, name="attn_bwd", grid=(ns,),
        in_specs=[pl.BlockSpec(memory_space=pltpu.SMEM), cur(Q_DIM), cur(KV_DIM), prev(KV_DIM), cur(KV_DIM),
                  prev(KV_DIM), cur(Q_DIM)],
        out_specs=[cur(Q_DIM), cur(KV_DIM), cur(KV_DIM), _full((1, N_Q_HEADS))],
        out_shape=[jax.ShapeDtypeStruct((t, Q_DIM), BF16), jax.ShapeDtypeStruct((t, KV_DIM), F32),
                   jax.ShapeDtypeStruct((t, KV_DIM), F32), jax.ShapeDtypeStruct((1, N_Q_HEADS), F32)],
        scratch_shapes=[pltpu.VMEM((N_KV_HEADS, BLOCK, HEAD_DIM), F32), pltpu.VMEM((N_KV_HEADS, BLOCK, HEAD_DIM), F32)],
        compiler_params=_cp(("arbitrary",)),
    )(sinks, q, k, k, v, v, do)


def _ev_dz(dzag, dq, dk, dv, rc, rsa, rsb):
    t = dzag.shape[0]
    tm = min(TM, t)

    def body(dzag_ref, dq_ref, dk_ref, dv_ref, c_ref, sa_ref, sb_ref, dz_ref):
        c, sa, sb = c_ref[...], sa_ref[...], sb_ref[...]
        dz_ref[:, 0:2 * A_CH] = dzag_ref[...]
        q0 = 2 * A_CH
        for j in range(Q_DIM // 128):
            d = dq_ref[:, 128 * j:128 * (j + 1)].astype(F32)
            dz_ref[:, q0 + 128 * j:q0 + 128 * (j + 1)] = _rope_bwd(d, c, sa, sb).astype(BF16)
        k0 = q0 + Q_DIM
        dz_ref[:, k0:k0 + KV_DIM] = _rope_bwd(dk_ref[...], c, sa, sb).astype(BF16)
        dz_ref[:, k0 + KV_DIM:k0 + 2 * KV_DIM] = dv_ref[...].astype(BF16)

    return pl.pallas_call(
        body, name="ev_dz", grid=(t // tm,),
        in_specs=[_rows(tm, 2 * A_CH), _rows(tm, Q_DIM), _rows(tm, KV_DIM), _rows(tm, KV_DIM),
                  _rows(tm, 128), _rows(tm, 128), _rows(tm, 128)],
        out_specs=_rows(tm, EVEN_IN),
        out_shape=jax.ShapeDtypeStruct((t, EVEN_IN), BF16),
        compiler_params=_cp(("arbitrary",)),
    )(dzag, dq, dk, dv, rc, rsa, rsb)


def _prep_weights(gat):
    p = {}
    w = gat["ev_w_in"][:, 0].transpose(1, 0, 2).reshape(D_MODEL, EVEN_IN)
    p["ev_w_in"], p["ev_w_in_t"] = w, w.T
    w = gat["ev_w_out"].reshape(A_CH + Q_DIM, D_MODEL)
    p["ev_w_out"], p["ev_w_out_t"] = w, w.T
    g = gat["od_w_in"][:, 0]
    p["od_w_in"], p["od_w_in_t"] = g, g.transpose(0, 2, 1).reshape(3 * SC_DIM, D_MODEL)
    w = gat["od_w_out"].reshape(SC_DIM, D_MODEL)
    p["od_w_out"], p["od_w_out_t"] = w, w.T
    g = gat["ffn_w_up"]
    p["ffn_w_up"] = g.reshape(2, N_DEV // 2, 2, D_MODEL, FF_N)
    p["ffn_w_up_t"] = [g[:, i].transpose(0, 2, 1).reshape(2, N_DEV // 2, FF_N, D_MODEL) for i in range(2)]
    g = gat["ffn_w_down"]
    p["ffn_w_down"] = [g[:, i].reshape(D_FF, D_MODEL) for i in range(2)]
    p["ffn_w_down_t"] = [w.reshape(N_DEV // 2, FF_N, D_MODEL).transpose(0, 2, 1) for w in p["ffn_w_down"]]
    return p


def _local_step(x, positions, target, p, s):
    row = lambda a: a.reshape(1, -1)
    nc = N_DEV // 2
    rc, rsa, rsb = _rope_tables(positions)
    conv31 = jnp.pad(s["ev_a_conv_w"][0], ((0, 1), (0, 0)))
    cw_ffn = [s["ffn_conv_w"][i].reshape(3, 2, nc, FF_N).transpose(1, 2, 0, 3) for i in range(2)]
    sinks = s["ev_sinks"][0]
    big, g = {}, {}

    h0, zag, q, k, v = _ev_in(x, row(s["mix_norm_pre"][0]), p["ev_w_in"], rc, rsa, rsb)
    cv, a = _conf_fwd(zag, conv31, s["ev_a_conv_b"], s["ev_a_ln_g"], s["ev_a_ln_b"])
    o = _attn_fwd(q, k, v, sinks)
    wo = p["ev_w_out"]
    m0, x1 = _out_post([a, o], [wo[:A_CH], wo[A_CH:]], x, row(s["mix_norm_post"][0]))
    h1, up0, u0, f0, x2 = _ffn_fwd(x1, row(s["ffn_norm_pre"][0]), p["ffn_w_up"], 0, cw_ffn[0], p["ffn_w_down"][0],
                                   row(s["ffn_norm_post"][0]))
    h2, z, cv1, y, m1, x3 = _od_fwd(x2, row(s["mix_norm_pre"][1]), p["od_w_in"], s["od_conv_w"][0], p["od_w_out"],
                                    row(s["mix_norm_post"][1]))
    h3, up1, u1, f1, x4 = _ffn_fwd(x3, row(s["ffn_norm_pre"][1]), p["ffn_w_up"], 1, cw_ffn[1], p["ffn_w_down"][1],
                                   row(s["ffn_norm_post"][1]))
    dx, lpart = _loss_grad(x4, target)

    def ffn_back(i, f, dxo, up, u, h, x_in, bufs):
        df, act, dup, dx_in, dgpost, dgpre, dcw = _ffn_bwd(
            f, dxo, row(s["ffn_norm_post"][i]), x_in, row(s["ffn_norm_pre"][i]), up, u, cw_ffn[i],
            p["ffn_w_down_t"][i], p["ffn_w_up_t"][i])
        bufs = (_dw_up(h, dup.reshape(N_DEV, -1, FF_N), i, bufs[0]), _dw_down(act, df, i, bufs[1]))
        return dx_in, dgpost, dgpre, dcw[:, :, 0:3].transpose(2, 0, 1, 3).reshape(3, 2 * D_FF), bufs

    dx, dgfpost1, dgfpre1, dcw1, bufs = ffn_back(1, f1, dx, up1, u1, h3, x3, (None, None))

    dm1, dz, dx, dgpost1, dgpre1, dcw_od = _od_bwd(m1, dx, row(s["mix_norm_post"][1]), x2, row(s["mix_norm_pre"][1]), z,
                                                   cv1, s["od_conv_w"][0], p["od_w_out_t"], p["od_w_in_t"])
    big["od_w_out"] = _dw2d(y, dm1, SC_DIM, D_MODEL).reshape(N_DEV, -1, D_MODEL)
    big["od_w_in"] = _dw_cols(h2, dz, 3 * SC_DIM // N_DEV)
    g["od_conv_w"] = dcw_od[None, 0:3]

    dx, dgfpost0, dgfpre0, dcw0, bufs = ffn_back(0, f0, dx, up0, u0, h1, x1, bufs)
    big["ffn_w_up"], big["ffn_w_down"] = bufs

    dm0, da, do, dgpost0 = _ev_bwd1(m0, dx, row(s["mix_norm_post"][0]), p["ev_w_out_t"])
    big["ev_w_out"] = jnp.concatenate([_dw2d(a, dm0, A_CH, D_MODEL), _dw2d(o, dm0, Q_DIM, D_MODEL)],
                                      axis=0).reshape(N_DEV, -1, D_MODEL)
    dzag, dcw31, dvec = _conf_bwd(da, cv, zag, conv31, s["ev_a_ln_g"], s["ev_a_ln_b"])
    dq, dk, dv, dsinks = _attn_bwd(q, k, v, do, sinks)
    dz0 = _ev_dz(dzag, dq, dk, dv, rc, rsa, rsb)
    dw_in = _dw2d(h0, dz0, D_MODEL, EVEN_IN // 2)
    big["ev_w_in"] = dw_in.reshape(D_MODEL, N_DEV, EVEN_IN // N_DEV).transpose(1, 0, 2)
    dx, dgpre0 = _dz_wt_rms_bwd(dz0, p["ev_w_in_t"], x, row(s["mix_norm_pre"][0]), dx)

    g["mix_norm_pre"] = jnp.concatenate([dgpre0, dgpre1], axis=0)
    g["mix_norm_post"] = jnp.concatenate([dgpost0, dgpost1], axis=0)
    g["ffn_norm_pre"] = jnp.concatenate([dgfpre0, dgfpre1], axis=0)
    g["ffn_norm_post"] = jnp.concatenate([dgfpost0, dgfpost1], axis=0)
    g["ev_a_conv_w"] = dcw31[None, 0:A_CONV]
    g["ev_a_conv_b"], g["ev_a_ln_g"], g["ev_a_ln_b"] = dvec[0:1], dvec[1:2], dvec[2:3]
    g["ev_sinks"] = dsinks
    g["ffn_conv_w"] = jnp.stack([dcw0, dcw1])
    return lpart[0, 0], dx, big, g


MESH = pl.DeviceIdType.MESH


def _all_gather(shards, name):
    nw = len(shards)

    def body(*refs):
        x_refs, out_refs = refs[:nw], refs[nw:2 * nw]
        send_sems, recv_sems, local_sems = refs[2 * nw:]
        x, y, c = lax.axis_index("x"), lax.axis_index("y"), lax.axis_index("c")
        me, sibling = (x, y, c), (x, y, 1 - c)
        chips = [(1 - x, y), (x, 1 - y), (1 - x, 1 - y)]

        def rows(w, px, py, pc):
            m_per = shards[w].shape[0]
            return out_refs[w].at[pl.ds((4 * px + 2 * py + pc) * m_per, m_per), :]

        def copy(w, k, block, to, src=None):
            return pltpu.make_async_remote_copy(
                src_ref=rows(w, *block) if src is None else src, dst_ref=rows(w, *block),
                send_sem=send_sems.at[w, k], recv_sem=recv_sems.at[w, k], device_id=to, device_id_type=MESH)

        mine, first, passed = [], [], []
        for w in range(nw):
            cp = pltpu.make_async_copy(x_refs[w], rows(w, *me), local_sems.at[w])
            cp.start()
            mine.append(cp)
            first.append([copy(w, 0, me, sibling, src=x_refs[w])]
                         + [copy(w, 1 + j, me, (*chip, c), src=x_refs[w]) for j, chip in enumerate(chips)])
            for cp in first[w]:
                cp.start()
        for w in range(nw):
            passed.append([copy(w, 4 + j, (*chip, c), sibling) for j, chip in enumerate(chips)])
            for j, chip in enumerate(chips):
                copy(w, 1 + j, (*chip, c), me).wait_recv()
                passed[w][j].start()
        for w in range(nw):
            copy(w, 0, sibling, me).wait_recv()
            for j, chip in enumerate(chips):
                copy(w, 4 + j, (*chip, 1 - c), me).wait_recv()
            for cp in first[w] + passed[w]:
                cp.wait_send()
            mine[w].wait()

    return pl.pallas_call(
        body, name=name,
        out_shape=[jax.ShapeDtypeStruct((N_DEV * a.shape[0], a.shape[1]), a.dtype) for a in shards],
        in_specs=[_ANY] * nw, out_specs=[_ANY] * nw,
        scratch_shapes=[pltpu.SemaphoreType.DMA((nw, 7)), pltpu.SemaphoreType.DMA((nw, 7)),
                        pltpu.SemaphoreType.DMA((nw,))],
    )(*shards)


def _rs_d2d(bufs):
    nw = len(bufs)

    def body(*refs):
        g_refs, recv_refs = refs[:nw], refs[nw:2 * nw]
        send_sems, recv_sems = refs[2 * nw:]
        x, y, c = lax.axis_index("x"), lax.axis_index("y"), lax.axis_index("c")
        copies = [pltpu.make_async_remote_copy(
            src_ref=g_refs[w].at[2 * q + (1 - c)], dst_ref=recv_refs[w].at[q], send_sem=send_sems.at[w, q],
            recv_sem=recv_sems.at[w, q], device_id=(x, y, 1 - c), device_id_type=MESH)
            for w in range(nw) for q in range(4)]
        for cp in copies:
            cp.start()
        for cp in copies:
            cp.wait()

    return pl.pallas_call(
        body, name="rs_d2d", out_shape=[jax.ShapeDtypeStruct((4,) + b.shape[1:], b.dtype) for b in bufs],
        in_specs=[_ANY] * nw, out_specs=[_ANY] * nw,
        scratch_shapes=[pltpu.SemaphoreType.DMA((nw, 4)), pltpu.SemaphoreType.DMA((nw, 4))],
    )(*bufs)


def _rs_ici(parts):
    nw = len(parts)

    def body(*refs):
        p_refs, recv_refs = refs[:nw], refs[nw:2 * nw]
        send_sems, recv_sems = refs[2 * nw:]
        x, y, c = lax.axis_index("x"), lax.axis_index("y"), lax.axis_index("c")
        chips = [(1 - x, y), (x, 1 - y), (1 - x, 1 - y)]
        copies = [pltpu.make_async_remote_copy(
            src_ref=p_refs[w].at[2 * px + py], dst_ref=recv_refs[w].at[j], send_sem=send_sems.at[w, j],
            recv_sem=recv_sems.at[w, j], device_id=(px, py, c), device_id_type=MESH)
            for w in range(nw) for j, (px, py) in enumerate(chips)]
        for cp in copies:
            cp.start()
        for cp in copies:
            cp.wait()

    return pl.pallas_call(
        body, name="rs_ici", out_shape=[jax.ShapeDtypeStruct((3,) + b.shape[1:], b.dtype) for b in parts],
        in_specs=[_ANY] * nw, out_specs=[_ANY] * nw,
        scratch_shapes=[pltpu.SemaphoreType.DMA((nw, 3)), pltpu.SemaphoreType.DMA((nw, 3))],
    )(*parts)


def _row_tile(rows, cap):
    best = None
    for d in range(16, min(rows, cap) + 1, 16):
        if rows % d == 0:
            best = d
    return rows if best is None else best


def _rs_add(gp, recv, core):
    _, r, l = gp.shape
    tr = _row_tile(r, 512)

    def body(c_ref, g_ref, r_ref, o_ref):
        o_ref[...] = (g_ref[...] + r_ref[...]).astype(BF16)

    return pl.pallas_call(
        body, name="rs_add", out_shape=jax.ShapeDtypeStruct((4, r, l), BF16),
        grid_spec=pltpu.PrefetchScalarGridSpec(
            num_scalar_prefetch=1, grid=(4, r // tr),
            in_specs=[pl.BlockSpec((1, tr, l), lambda q, j, c: (2 * q + c[0], j, 0)),
                      pl.BlockSpec((1, tr, l), lambda q, j, c: (q, j, 0))],
            out_specs=pl.BlockSpec((1, tr, l), lambda q, j, c: (q, j, 0))),
        compiler_params=_cp(("arbitrary", "arbitrary")),
    )(core, gp, recv)


def _adam_math(w, g, m, v):
    bc1 = 1.0 - ADAM_B1 ** ADAM_STEP
    bc2 = 1.0 - ADAM_B2 ** ADAM_STEP
    mn = ADAM_B1 * m + (1.0 - ADAM_B1) * g
    vn = ADAM_B2 * v + (1.0 - ADAM_B2) * (g * g)
    return -ADAM_LR * ((mn / bc1) / (jnp.sqrt(vn / bc2) + ADAM_EPS) + ADAM_WD * w), mn, vn


def _adamw_rs(gp, recv_a, recv_b, w, m, v, where):
    _, r, l = gp.shape
    tr = _row_tile(r, 256)

    def body(i_ref, g_ref, a_ref, b_ref, w_ref, m_ref, v_ref, go_ref, d_ref, mo_ref, vo_ref):
        g = g_ref[0] + a_ref[0]
        for j in range(3):
            g = g + b_ref[j].astype(F32)
        go_ref[...] = g
        d_ref[...], mo_ref[...], vo_ref[...] = _adam_math(w_ref[...], g, m_ref[...], v_ref[...])

    spec = pl.BlockSpec((tr, l), lambda i, s: (i, 0))
    return pl.pallas_call(
        body, name="adamw_rs", out_shape=[jax.ShapeDtypeStruct((r, l), F32)] * 4,
        grid_spec=pltpu.PrefetchScalarGridSpec(
            num_scalar_prefetch=1, grid=(r // tr,),
            in_specs=[pl.BlockSpec((1, tr, l), lambda i, s: (s[0], i, 0)),
                      pl.BlockSpec((1, tr, l), lambda i, s: (s[1], i, 0)),
                      pl.BlockSpec((3, tr, l), lambda i, s: (0, i, 0)), spec, spec, spec],
            out_specs=[spec] * 4),
        compiler_params=_cp(("arbitrary",)),
    )(where, gp, recv_a, recv_b, w, m, v)


def _sum_blocks(a, nblk):
    m = a.shape[0] // nblk
    n = a.shape[1]

    def body(a_ref, o_ref):
        acc = a_ref[0]
        for j in range(1, nblk):
            acc = acc + a_ref[j]
        o_ref[...] = acc

    return pl.pallas_call(
        body, name="sum_blocks", out_shape=jax.ShapeDtypeStruct((m, n), a.dtype),
        in_specs=[_full((nblk, m, n))], out_specs=_full((m, n)),
    )(a.reshape(nblk, m, n))


def _adamw(w, g, m, v):
    rows, c = w.shape

    def body(w_ref, g_ref, m_ref, v_ref, d_ref, mo_ref, vo_ref):
        d_ref[...], mo_ref[...], vo_ref[...] = _adam_math(w_ref[...], g_ref[...], m_ref[...], v_ref[...])

    return pl.pallas_call(
        body, name="adamw", in_specs=[_full((rows, c))] * 4, out_specs=[_full((rows, c))] * 3,
        out_shape=[jax.ShapeDtypeStruct((rows, c), F32)] * 3,
    )(w, g, m, v)


WEIGHTS = ["mix_norm_pre", "mix_norm_post", "ffn_norm_pre", "ffn_norm_post", "ev_w_in", "ev_a_conv_w", "ev_a_conv_b",
           "ev_a_ln_g", "ev_a_ln_b", "ev_sinks", "ev_w_out", "od_w_in", "od_conv_w", "od_w_out", "ffn_w_up",
           "ffn_conv_w", "ffn_w_down"]
BIG = ["ev_w_in", "ev_w_out", "od_w_in", "od_w_out", "ffn_w_up", "ffn_w_down"]
SMALL_REPL = ["mix_norm_pre", "mix_norm_post", "ffn_norm_pre", "ffn_norm_post", "ev_a_conv_b", "ev_a_ln_g",
              "ev_a_ln_b", "ev_sinks"]
SMALL_SHARDED = ["ev_a_conv_w", "od_conv_w", "ffn_conv_w"]


def _pack(arrs, rows):
    flat = jnp.concatenate([a.reshape(-1) for a in arrs])
    return jnp.pad(flat, (0, rows * LANES - flat.shape[0])).reshape(rows, LANES)


def _unpack(packed, shapes):
    flat, out, off = packed.reshape(-1), [], 0
    for s in shapes:
        n = 1
        for d in s:
            n *= d
        out.append(flat[off:off + n].reshape(s))
        off += n
    return out


def kernel(x, positions, mix_norm_pre, mix_norm_post, ffn_norm_pre, ffn_norm_post, ev_w_in, ev_a_conv_w, ev_a_conv_b, ev_a_ln_g, ev_a_ln_b, ev_sinks, ev_w_out, od_w_in, od_conv_w, od_w_out, ffn_w_up, ffn_conv_w, ffn_w_down, loss_target, m_mix_norm_pre, m_mix_norm_post, m_ffn_norm_pre, m_ffn_norm_post, m_ev_w_in, m_ev_a_conv_w, m_ev_a_conv_b, m_ev_a_ln_g, m_ev_a_ln_b, m_ev_sinks, m_ev_w_out, m_od_w_in, m_od_conv_w, m_od_w_out, m_ffn_w_up, m_ffn_conv_w, m_ffn_w_down, v_mix_norm_pre, v_mix_norm_post, v_ffn_norm_pre, v_ffn_norm_post, v_ev_w_in, v_ev_a_conv_w, v_ev_a_conv_b, v_ev_a_ln_g, v_ev_a_ln_b, v_ev_sinks, v_ev_w_out, v_od_w_in, v_od_conv_w, v_od_w_out, v_ffn_w_up, v_ffn_conv_w, v_ffn_w_down):
    w = dict(zip(WEIGHTS, (mix_norm_pre, mix_norm_post, ffn_norm_pre, ffn_norm_post, ev_w_in, ev_a_conv_w, ev_a_conv_b,
                           ev_a_ln_g, ev_a_ln_b, ev_sinks, ev_w_out, od_w_in, od_conv_w, od_w_out, ffn_w_up, ffn_conv_w,
                           ffn_w_down)))
    mom = dict(zip(WEIGHTS, (m_mix_norm_pre, m_mix_norm_post, m_ffn_norm_pre, m_ffn_norm_post, m_ev_w_in, m_ev_a_conv_w,
                             m_ev_a_conv_b, m_ev_a_ln_g, m_ev_a_ln_b, m_ev_sinks, m_ev_w_out, m_od_w_in, m_od_conv_w,
                             m_od_w_out, m_ffn_w_up, m_ffn_conv_w, m_ffn_w_down)))
    var = dict(zip(WEIGHTS, (v_mix_norm_pre, v_mix_norm_post, v_ffn_norm_pre, v_ffn_norm_post, v_ev_w_in, v_ev_a_conv_w,
                             v_ev_a_conv_b, v_ev_a_ln_g, v_ev_a_ln_b, v_ev_sinks, v_ev_w_out, v_od_w_in, v_od_conv_w,
                             v_od_w_out, v_ffn_w_up, v_ffn_conv_w, v_ffn_w_down)))
    ix, iy, ic = lax.axis_index("x"), lax.axis_index("y"), lax.axis_index("c")
    dev = 4 * ix + 2 * iy + ic
    two = lambda a: a.reshape(-1, a.shape[-1])

    gat = _all_gather([two(w[n].astype(BF16)) for n in BIG], "gather_big")
    p = _prep_weights({n: a.reshape((N_DEV,) + w[n].shape) for n, a in zip(BIG, gat)})
    small = {n: w[n] for n in SMALL_REPL}
    small_shapes = [w[n].shape for n in SMALL_SHARDED]
    conv_gat = _all_gather([_pack([w[n] for n in SMALL_SHARDED], 8)], "gather_conv")[0].reshape(N_DEV, 8, LANES)
    per_dev = [_unpack(conv_gat[d], small_shapes) for d in range(N_DEV)]
    for k, n in enumerate(SMALL_SHARDED):
        small[n] = jnp.concatenate([per_dev[d][k] for d in range(N_DEV)], axis=-1)

    lpart, grad_x, big, g = _local_step(x[0], positions[0], loss_target[0], p, small)
    loss = lax.psum(lpart, ("x", "y", "c"))

    bufs = [big[n].reshape(N_DEV, -1, w[n].shape[-1]) for n in BIG]
    core = jnp.reshape(ic, (1,)).astype(jnp.int32)
    where = jnp.stack([dev, 2 * ix + iy]).astype(jnp.int32)
    recv_a = _rs_d2d(bufs)
    recv_b = _rs_ici([_rs_add(b, ra, core) for b, ra in zip(bufs, recv_a)])
    grads, delta, new_m, new_v = {}, {}, {}, {}
    for n, b, ra, rb in zip(BIG, bufs, recv_a, recv_b):
        outs = _adamw_rs(b, ra, rb, two(w[n]), two(mom[n]), two(var[n]), where)
        grads[n], delta[n], new_m[n], new_v[n] = (a.reshape(w[n].shape) for a in outs)

    small_names = SMALL_REPL + SMALL_SHARDED
    s_all = _sum_blocks(_all_gather([_pack([g[n] for n in small_names], 64)], "gather_small_grads")[0], N_DEV)
    for n, a in zip(small_names, _unpack(s_all, [small[n].shape for n in small_names])):
        if n in SMALL_SHARDED:
            width = w[n].shape[-1]
            a = lax.dynamic_slice_in_dim(a, dev * width, width, axis=a.ndim - 1)
        grads[n] = a
    pk = lambda dct: _pack([dct[n] for n in small_names], 16)
    outs = _adamw(pk(w), pk(grads), pk(mom), pk(var))
    for dst, packed in zip((delta, new_m, new_v), outs):
        for n, a in zip(small_names, _unpack(packed, [w[n].shape for n in small_names])):
            dst[n] = a

    return (loss, grad_x[None], *[grads[n] for n in WEIGHTS], *[delta[n] for n in WEIGHTS],
            *[new_m[n] for n in WEIGHTS], *[new_v[n] for n in WEIGHTS])
```

```python
import jax
import jax.numpy as jnp
from jax import lax
from jax.experimental import pallas as pl
from jax.experimental.pallas import tpu as pltpu

F32, BF16 = jnp.float32, jnp.bfloat16

D_MODEL = 1024
A_CH = 512
A_CONV = 31
Q_DIM = 512
KV_DIM = 128
HEAD_DIM = 64
N_Q_HEADS = 8
N_KV_HEADS = 2
GROUP = 4
BLOCK = 128
EVEN_IN = 1792
SC_DIM = 1024
D_FF = 2816
ROPE_THETA = 500000.0
ROPE_DIM = 16
RMS_EPS = 1e-6
LN_EPS = 1e-5
SCALE = HEAD_DIM ** -0.5
NEG = -1e30

ADAM_LR, ADAM_B1, ADAM_B2, ADAM_EPS, ADAM_WD, ADAM_STEP = 0.001, 0.9, 0.999, 1e-08, 0.01, 10

N_DEV = 8
FF_N = 2 * D_FF // N_DEV
LANES = 1024
HALO3 = 8
HALO31 = 32
VMEM_LIMIT = 56 * 1024 * 1024

TM = 512
TM_BWD = 256
TK_DW = 2048
ATT_NB = 4

_ANY = pl.BlockSpec(memory_space=pl.ANY)
_CONTRACT_LAST = (((1,), (1,)), ((), ()))
_CONTRACT_FIRST = (((0,), (0,)), ((), ()))


def _cp(sem, vmem=VMEM_LIMIT):
    return pltpu.CompilerParams(dimension_semantics=sem, vmem_limit_bytes=vmem)


def _full(shape):
    n = len(shape)
    return pl.BlockSpec(shape, lambda *_: (0,) * n)


def _rows(tm, n):
    return pl.BlockSpec((tm, n), lambda i, *_: (i, 0))


def _sigmoid(x):
    return 0.5 * jnp.tanh(0.5 * x) + 0.5


def _rsqrt_mean(x):
    return lax.rsqrt(jnp.mean(x * x, axis=-1, keepdims=True) + RMS_EPS)


def _rms_bwd(x, g, dy):
    r = _rsqrt_mean(x)
    xh = x * r
    dxh = dy * g
    dx = r * (dxh - xh * jnp.mean(dxh * xh, axis=-1, keepdims=True))
    return dx, jnp.sum(dy * xh, axis=0, keepdims=True)


def _acc_out(ref, first, val):
    @pl.when(first)
    def _():
        ref[...] = val

    @pl.when(jnp.logical_not(first))
    def _():
        ref[...] += val


def _rope_tables(positions):
    half = ROPE_DIM // 2
    inv_freq = ROPE_THETA ** (-(jnp.arange(half, dtype=F32) * 2.0 / ROPE_DIM))
    ang = positions.astype(F32)[:, None] * inv_freq
    cos, sin = jnp.cos(ang), jnp.sin(ang)
    t = positions.shape[0]
    one, zero = jnp.ones((t, HEAD_DIM - ROPE_DIM), F32), jnp.zeros((t, HEAD_DIM - ROPE_DIM), F32)
    z8 = jnp.zeros((t, half), F32)
    c = jnp.concatenate([cos, cos, one], axis=1)
    sa = jnp.concatenate([z8, sin, zero], axis=1)
    sb = jnp.concatenate([-sin, z8, zero], axis=1)
    return tuple(jnp.tile(a, (1, 2)) for a in (c, sa, sb))


def _rope(t, c, sa, sb):
    return t * c + pltpu.roll(t, 8, 1) * sa + pltpu.roll(t, 120, 1) * sb


def _rope_bwd(d, c, sa, sb):
    return d * c + pltpu.roll(d * sa, 120, 1) + pltpu.roll(d * sb, 8, 1)


def _ev_in(x, gpre, w_in, rc, rsa, rsb):
    t = x.shape[0]
    tm = min(TM, t)

    def body(x_ref, g_ref, w_ref, c_ref, sa_ref, sb_ref, h_ref, zag_ref, q_ref, k_ref, v_ref):
        xv = x_ref[...]
        h = (xv * _rsqrt_mean(xv) * g_ref[...]).astype(BF16)
        h_ref[...] = h
        z = jnp.dot(h, w_ref[...], preferred_element_type=F32)
        zag_ref[...] = z[:, :2 * A_CH].astype(BF16)
        c, sa, sb = c_ref[...], sa_ref[...], sb_ref[...]
        q0 = 2 * A_CH
        for j in range(Q_DIM // 128):
            q_ref[:, 128 * j:128 * (j + 1)] = _rope(z[:, q0 + 128 * j:q0 + 128 * (j + 1)], c, sa, sb).astype(BF16)
        k0 = q0 + Q_DIM
        k_ref[...] = _rope(z[:, k0:k0 + KV_DIM], c, sa, sb).astype(BF16)
        v_ref[...] = z[:, k0 + KV_DIM:k0 + 2 * KV_DIM].astype(BF16)

    return pl.pallas_call(
        body, name="ev_in", grid=(t // tm,),
        in_specs=[_rows(tm, D_MODEL), _full((1, D_MODEL)), _full((D_MODEL, EVEN_IN)),
                  _rows(tm, 128), _rows(tm, 128), _rows(tm, 128)],
        out_specs=[_rows(tm, D_MODEL), _rows(tm, 2 * A_CH), _rows(tm, Q_DIM), _rows(tm, KV_DIM), _rows(tm, KV_DIM)],
        out_shape=[jax.ShapeDtypeStruct((t, D_MODEL), BF16), jax.ShapeDtypeStruct((t, 2 * A_CH), BF16),
                   jax.ShapeDtypeStruct((t, Q_DIM), BF16), jax.ShapeDtypeStruct((t, KV_DIM), BF16),
                   jax.ShapeDtypeStruct((t, KV_DIM), BF16)],
        compiler_params=_cp(("arbitrary",)),
    )(x, gpre, w_in, rc, rsa, rsb)


def _glu(zag):
    z = zag.astype(F32)
    return z[:, :A_CH] * jax.nn.sigmoid(z[:, A_CH:])


def _conf_fwd(zag, conv_w, conv_b, ln_g, ln_b):
    t = zag.shape[0]
    tm = min(TM_BWD, t)

    def body(z_ref, w_ref, b_ref, g_ref, lb_ref, c_ref, a_ref, ext):
        i = pl.program_id(0)

        @pl.when(i == 0)
        def _():
            ext[0:HALO31, :] = jnp.zeros((HALO31, A_CH), F32)

        ext[HALO31:HALO31 + tm, :] = _glu(z_ref[...])
        acc = jnp.zeros((tm, A_CH), F32)
        for j in range(A_CONV):
            s = HALO31 - (A_CONV - 1) + j
            acc = acc + w_ref[j:j + 1, :] * ext[s:s + tm, :]
        ext[0:HALO31, :] = ext[tm:tm + HALO31, :]
        cv = acc + b_ref[...]
        c_ref[...] = cv
        mu = jnp.mean(cv, axis=-1, keepdims=True)
        xc = cv - mu
        ln = xc * lax.rsqrt(jnp.mean(xc * xc, axis=-1, keepdims=True) + LN_EPS) * g_ref[...] + lb_ref[...]
        a_ref[...] = (ln * jax.nn.sigmoid(ln)).astype(BF16)

    return pl.pallas_call(
        body, name="conf_fwd", grid=(t // tm,),
        in_specs=[_rows(tm, 2 * A_CH), _full((32, A_CH)), _full((1, A_CH)), _full((1, A_CH)), _full((1, A_CH))],
        out_specs=[_rows(tm, A_CH), _rows(tm, A_CH)],
        out_shape=[jax.ShapeDtypeStruct((t, A_CH), F32), jax.ShapeDtypeStruct((t, A_CH), BF16)],
        scratch_shapes=[pltpu.VMEM((HALO31 + tm, A_CH), F32)],
        compiler_params=_cp(("arbitrary",)),
    )(zag, conv_w, conv_b, ln_g, ln_b)


def _attn_mask(first_block):
    row = lax.broadcasted_iota(jnp.int32, (GROUP * BLOCK, 2 * BLOCK), 0) & (BLOCK - 1)
    col = lax.broadcasted_iota(jnp.int32, (GROUP * BLOCK, 2 * BLOCK), 1)
    diff = row + BLOCK - col
    return (diff >= 0) & (diff < BLOCK) & ((col >= BLOCK) | jnp.logical_not(first_block))


def _sink_rows(s_ref, h):
    grp = lax.broadcasted_iota(jnp.int32, (GROUP * BLOCK, 1), 0) >> 7
    out = jnp.full((GROUP * BLOCK, 1), s_ref[GROUP * h], F32)
    for g in range(1, GROUP):
        out = jnp.where(grp == g, s_ref[GROUP * h + g], out)
    return out


def _attn_probs(q4, k2, mask, sink):
    s = lax.dot_general(q4, k2, _CONTRACT_LAST, preferred_element_type=F32) * SCALE
    s = jnp.where(mask, s, NEG)
    m = jnp.maximum(jnp.max(s, axis=-1, keepdims=True), sink)
    p = jnp.exp(s - m)
    es = jnp.exp(sink - m)
    inv = 1.0 / (jnp.sum(p, axis=-1, keepdims=True) + es)
    return p * inv, es * inv


def _q_heads(q, h):
    return jnp.concatenate([q[:, HEAD_DIM * (GROUP * h + g):HEAD_DIM * (GROUP * h + g + 1)] for g in range(GROUP)],
                           axis=0)


def _kv_head(prev, cur, h):
    return jnp.concatenate([prev[:, HEAD_DIM * h:HEAD_DIM * (h + 1)], cur[:, HEAD_DIM * h:HEAD_DIM * (h + 1)]], axis=0)


def _attn_fwd(q, k, v, sinks):
    t = q.shape[0]
    nb = min(ATT_NB, t // BLOCK)
    rows = nb * BLOCK

    def body(s_ref, q_ref, kc_ref, kp_ref, vc_ref, vp_ref, o_ref):
        first = pl.program_id(0) == 0
        for b in range(nb):
            lo = BLOCK * b
            mask = _attn_mask(first) if b == 0 else _attn_mask(False)
            qv, kc, vc = q_ref[lo:lo + BLOCK, :], kc_ref[lo:lo + BLOCK, :], vc_ref[lo:lo + BLOCK, :]
            kp = kp_ref[...] if b == 0 else kc_ref[lo - BLOCK:lo, :]
            vp = vp_ref[...] if b == 0 else vc_ref[lo - BLOCK:lo, :]
            for h in range(N_KV_HEADS):
                pn, _ = _attn_probs(_q_heads(qv, h), _kv_head(kp, kc, h), mask, _sink_rows(s_ref, h))
                o4 = jnp.dot(pn.astype(BF16), _kv_head(vp, vc, h), preferred_element_type=F32).astype(BF16)
                for g in range(GROUP):
                    c0 = HEAD_DIM * (GROUP * h + g)
                    o_ref[lo:lo + BLOCK, c0:c0 + HEAD_DIM] = o4[BLOCK * g:BLOCK * (g + 1), :]

    cur = lambda n: pl.BlockSpec((rows, n), lambda i: (i, 0))
    prev = lambda n: pl.BlockSpec((BLOCK, n), lambda i: (jnp.maximum(i * nb - 1, 0), 0))
    return pl.pallas_call(
        body, name="attn_fwd", grid=(t // rows,),
        in_specs=[pl.BlockSpec(memory_space=pltpu.SMEM), cur(Q_DIM), cur(KV_DIM), prev(KV_DIM), cur(KV_DIM),
                  prev(KV_DIM)],
        out_specs=cur(Q_DIM),
        out_shape=jax.ShapeDtypeStruct((t, Q_DIM), BF16),
        compiler_params=_cp(("arbitrary",)),
    )(sinks, q, k, k, v, v)


def _out_post(lhs, ws, x_in, gpost):
    t = x_in.shape[0]
    tm = min(TM, t)
    n = len(lhs)

    def body(*refs):
        x_ref, g_ref, m_ref, xo_ref = refs[2 * n:]
        m = jnp.dot(refs[0][...], refs[n][...], preferred_element_type=F32)
        for j in range(1, n):
            m = m + jnp.dot(refs[j][...], refs[n + j][...], preferred_element_type=F32)
        m_ref[...] = m
        xo_ref[...] = x_ref[...] + m * _rsqrt_mean(m) * g_ref[...]

    return pl.pallas_call(
        body, name="out_post", grid=(t // tm,),
        in_specs=[_rows(tm, a.shape[1]) for a in lhs] + [_full(w.shape) for w in ws]
                 + [_rows(tm, D_MODEL), _full((1, D_MODEL))],
        out_specs=[_rows(tm, D_MODEL), _rows(tm, D_MODEL)],
        out_shape=[jax.ShapeDtypeStruct((t, D_MODEL), F32)] * 2,
        compiler_params=_cp(("arbitrary",)),
    )(*lhs, *ws, x_in, gpost)


def _conv3(w_ref, ext, tm):
    s = HALO3 - 2
    return (w_ref[0:1, :] * ext[s:s + tm, :] + w_ref[1:2, :] * ext[s + 1:s + 1 + tm, :]
            + w_ref[2:3, :] * ext[s + 2:s + 2 + tm, :])


def _ffn_fwd(x1, gpre, wup, layer, cw, wd, gpost):
    t = x1.shape[0]
    tm = min(TM, t)
    nc, n = wup.shape[1], wup.shape[4]

    def body(x_ref, gpre_ref, wup_ref, cw_ref, wd_ref, gpost_ref, h_ref, up_ref, u_ref, f_ref, xo_ref, h_s, acc, ext, hal):
        i, c = pl.program_id(0), pl.program_id(1)

        @pl.when(c == 0)
        def _():
            xv = x_ref[...]
            h = (xv * _rsqrt_mean(xv) * gpre_ref[...]).astype(BF16)
            h_s[...] = h
            h_ref[...] = h

        @pl.when(i == 0)
        def _():
            hal[c] = jnp.zeros((2, HALO3, n), F32)

        u = []
        for gv in range(2):
            up = jnp.dot(h_s[...], wup_ref[gv, 0, 0], preferred_element_type=F32)
            up_ref[gv, 0] = up.astype(BF16)
            ext[gv, 0:HALO3, :] = hal[c, gv]
            ext[gv, HALO3:HALO3 + tm, :] = up
            hal[c, gv] = ext[gv, tm:tm + HALO3, :]
            s = HALO3 - 2
            u.append(cw_ref[gv, 0, 0:1, :] * ext[gv, s:s + tm, :] + cw_ref[gv, 0, 1:2, :] * ext[gv, s + 1:s + 1 + tm, :]
                     + cw_ref[gv, 0, 2:3, :] * up)
            u_ref[gv, 0] = u[gv].astype(BF16)
        act = (u[0] * _sigmoid(u[0]) * u[1]).astype(BF16)
        part = jnp.dot(act, wd_ref[...], preferred_element_type=F32)

        @pl.when(c == 0)
        def _():
            acc[...] = part

        @pl.when(jnp.logical_and(c > 0, c < nc - 1))
        def _():
            acc[...] += part

        @pl.when(c == nc - 1)
        def _():
            f = acc[...] + part
            f_ref[...] = f
            xo_ref[...] = x_ref[...] + f * _rsqrt_mean(f) * gpost_ref[...]

    row = lambda w: pl.BlockSpec((tm, w), lambda i, c: (i, 0))
    one = _full((1, D_MODEL))
    return pl.pallas_call(
        body, name="ffn_fwd", grid=(t // tm, nc),
        in_specs=[row(D_MODEL), one, pl.BlockSpec((2, 1, 1, D_MODEL, n), lambda i, c: (0, c, layer, 0, 0)),
                  pl.BlockSpec((2, 1, 3, n), lambda i, c: (0, c, 0, 0)), pl.BlockSpec((n, D_MODEL), lambda i, c: (c, 0)),
                  one],
        out_specs=[row(D_MODEL), pl.BlockSpec((2, 1, tm, n), lambda i, c: (0, c, i, 0)),
                   pl.BlockSpec((2, 1, tm, n), lambda i, c: (0, c, i, 0)), row(D_MODEL), row(D_MODEL)],
        out_shape=[jax.ShapeDtypeStruct((t, D_MODEL), BF16), jax.ShapeDtypeStruct((2, nc, t, n), BF16),
                   jax.ShapeDtypeStruct((2, nc, t, n), BF16), jax.ShapeDtypeStruct((t, D_MODEL), F32),
                   jax.ShapeDtypeStruct((t, D_MODEL), F32)],
        scratch_shapes=[pltpu.VMEM((tm, D_MODEL), BF16), pltpu.VMEM((tm, D_MODEL), F32),
                        pltpu.VMEM((2, HALO3 + tm, n), F32), pltpu.VMEM((nc, 2, HALO3, n), F32)],
        compiler_params=_cp(("arbitrary", "arbitrary")),
    )(x1, gpre, wup, cw, wd, gpost)


def _od_fwd(x_in, gpre, w_in, cw, w_out, gpost):
    t = x_in.shape[0]
    tm = min(TM, t)
    ns, _, n = w_in.shape

    def body(x_ref, gpre_ref, w_ref, cw_ref, wo_ref, gpost_ref, h_ref, z_ref, cv_ref, y_ref, m_ref, xo_ref, z_s, ext):
        i = pl.program_id(0)
        xv = x_ref[...]
        h = (xv * _rsqrt_mean(xv) * gpre_ref[...]).astype(BF16)
        h_ref[...] = h
        for j in range(ns):
            z_s[:, n * j:n * (j + 1)] = jnp.dot(h, w_ref[j], preferred_element_type=F32)
        z_ref[...] = z_s[...].astype(BF16)

        @pl.when(i == 0)
        def _():
            ext[0:HALO3, :] = jnp.zeros((HALO3, SC_DIM), F32)

        ext[HALO3:HALO3 + tm, :] = z_s[:, SC_DIM:2 * SC_DIM] * z_s[:, 2 * SC_DIM:]
        cv = _conv3(cw_ref, ext, tm)
        cv_ref[...] = cv.astype(BF16)
        y = (z_s[:, :SC_DIM] * cv).astype(BF16)
        ext[0:HALO3, :] = ext[tm:tm + HALO3, :]
        y_ref[...] = y
        m = jnp.dot(y, wo_ref[...], preferred_element_type=F32)
        m_ref[...] = m
        xo_ref[...] = xv + m * _rsqrt_mean(m) * gpost_ref[...]

    return pl.pallas_call(
        body, name="od_fwd", grid=(t // tm,),
        in_specs=[_rows(tm, D_MODEL), _full((1, D_MODEL)), _full((ns, D_MODEL, n)), _full((3, SC_DIM)),
                  _full((SC_DIM, D_MODEL)), _full((1, D_MODEL))],
        out_specs=[_rows(tm, D_MODEL), _rows(tm, 3 * SC_DIM), _rows(tm, SC_DIM), _rows(tm, SC_DIM), _rows(tm, D_MODEL),
                   _rows(tm, D_MODEL)],
        out_shape=[jax.ShapeDtypeStruct((t, D_MODEL), BF16), jax.ShapeDtypeStruct((t, 3 * SC_DIM), BF16),
                   jax.ShapeDtypeStruct((t, SC_DIM), BF16), jax.ShapeDtypeStruct((t, SC_DIM), BF16),
                   jax.ShapeDtypeStruct((t, D_MODEL), F32), jax.ShapeDtypeStruct((t, D_MODEL), F32)],
        scratch_shapes=[pltpu.VMEM((tm, 3 * SC_DIM), F32), pltpu.VMEM((HALO3 + tm, SC_DIM), F32)],
        compiler_params=_cp(("arbitrary",)),
    )(x_in, gpre, w_in, cw, w_out, gpost)


def _loss_grad(y, target):
    t = y.shape[0]
    tm = min(TM, t)

    def body(y_ref, t_ref, dy_ref, l_ref):
        e = y_ref[...] - t_ref[...]
        dy_ref[...] = e * (1.0 / D_MODEL)
        part = jnp.zeros((1, 128), F32) + jnp.sum(e * e) * (0.5 / D_MODEL)
        _acc_out(l_ref, pl.program_id(0) == 0, part)

    return pl.pallas_call(
        body, name="loss_grad", grid=(t // tm,),
        in_specs=[_rows(tm, D_MODEL), _rows(tm, D_MODEL)],
        out_specs=[_rows(tm, D_MODEL), _full((1, 128))],
        out_shape=[jax.ShapeDtypeStruct((t, D_MODEL), F32), jax.ShapeDtypeStruct((1, 128), F32)],
        compiler_params=_cp(("arbitrary",)),
    )(y, target)


def _dw2d(a, b, bm, bn):
    t, m = a.shape
    n = b.shape[1]
    tk = min(TK_DW, t)

    def body(a_ref, b_ref, o_ref):
        part = lax.dot_general(a_ref[...], b_ref[...], _CONTRACT_FIRST, preferred_element_type=F32)
        _acc_out(o_ref, pl.program_id(2) == 0, part)

    return pl.pallas_call(
        body, name="dw2d", grid=(m // bm, n // bn, t // tk),
        in_specs=[pl.BlockSpec((tk, bm), lambda i, j, k: (k, i)), pl.BlockSpec((tk, bn), lambda i, j, k: (k, j))],
        out_specs=pl.BlockSpec((bm, bn), lambda i, j, k: (i, j)),
        out_shape=jax.ShapeDtypeStruct((m, n), F32),
        compiler_params=_cp(("arbitrary", "arbitrary", "arbitrary")),
    )(a, b)


def _dw_cols(a, b, n_blk):
    t, m = a.shape
    s = b.shape[1] // n_blk
    tk = min(TK_DW, t)

    def body(a_ref, b_ref, o_ref):
        part = lax.dot_general(a_ref[...], b_ref[...], _CONTRACT_FIRST, preferred_element_type=F32)
        _acc_out(o_ref.at[0], pl.program_id(1) == 0, part)

    return pl.pallas_call(
        body, name="dw_cols", grid=(s, t // tk),
        in_specs=[pl.BlockSpec((tk, m), lambda j, k: (k, 0)), pl.BlockSpec((tk, n_blk), lambda j, k: (k, j))],
        out_specs=pl.BlockSpec((1, m, n_blk), lambda j, k: (j, 0, 0)),
        out_shape=jax.ShapeDtypeStruct((s, m, n_blk), F32),
        compiler_params=_cp(("arbitrary", "arbitrary")),
    )(a, b)


def _dw_up(h, dup, layer, buf):
    t, m = h.shape
    s, _, n = dup.shape
    tk = min(TK_DW, t)

    def body(*refs):
        a_ref, b_ref, o_ref = refs[0], refs[1], refs[-1]
        part = lax.dot_general(a_ref[...], b_ref[0], _CONTRACT_FIRST, preferred_element_type=F32)
        _acc_out(o_ref.at[0, 0], pl.program_id(1) == 0, part)

    ins = [h, dup] + ([] if buf is None else [buf])
    return pl.pallas_call(
        body, name="dw_up", grid=(s, t // tk),
        in_specs=[pl.BlockSpec((tk, m), lambda j, k: (k, 0)), pl.BlockSpec((1, tk, n), lambda j, k: (j, k, 0))]
                 + ([] if buf is None else [_ANY]),
        out_specs=pl.BlockSpec((1, 1, m, n), lambda j, k: (j, layer, 0, 0)),
        out_shape=jax.ShapeDtypeStruct((s, 2, m, n), F32),
        input_output_aliases={} if buf is None else {2: 0},
        compiler_params=_cp(("arbitrary", "arbitrary")),
    )(*ins)


def _dw_down(act, df, layer, buf):
    nc, t, n = act.shape
    d = df.shape[1]
    tk = min(TK_DW, t)

    def body(*refs):
        a_ref, b_ref, o_ref = refs[0], refs[1], refs[-1]
        part = lax.dot_general(a_ref[0], b_ref[...], _CONTRACT_FIRST, preferred_element_type=F32)
        part = part.reshape(2, n // 2, d)
        first = pl.program_id(1) == 0

        @pl.when(first)
        def _():
            o_ref[:, 0] = part

        @pl.when(jnp.logical_not(first))
        def _():
            o_ref[:, 0] += part

    ins = [act, df] + ([] if buf is None else [buf])
    return pl.pallas_call(
        body, name="dw_down", grid=(nc, t // tk),
        in_specs=[pl.BlockSpec((1, tk, n), lambda c, k: (c, k, 0)), pl.BlockSpec((tk, d), lambda c, k: (k, 0))]
                 + ([] if buf is None else [_ANY]),
        out_specs=pl.BlockSpec((2, 1, n // 2, d), lambda c, k: (c, layer, 0, 0)),
        out_shape=jax.ShapeDtypeStruct((2 * nc, 2, n // 2, d), F32),
        input_output_aliases={} if buf is None else {2: 0},
        compiler_params=_cp(("arbitrary", "arbitrary")),
    )(*ins)


def _dz_wt_rms_bwd(dz, wt, x_in, gpre, dres):
    t, n = dz.shape
    tm = min(TM, t)

    def body(dz_ref, wt_ref, x_ref, g_ref, dres_ref, dx_ref, dg_ref):
        dh = jnp.dot(dz_ref[...], wt_ref[...], preferred_element_type=F32)
        dx, dg = _rms_bwd(x_ref[...], g_ref[...], dh)
        dx_ref[...] = dres_ref[...] + dx
        _acc_out(dg_ref, pl.program_id(0) == 0, dg)

    return pl.pallas_call(
        body, name="dz_wt_rms_bwd", grid=(t // tm,),
        in_specs=[_rows(tm, n), _full((n, D_MODEL)), _rows(tm, D_MODEL), _full((1, D_MODEL)), _rows(tm, D_MODEL)],
        out_specs=[_rows(tm, D_MODEL), _full((1, D_MODEL))],
        out_shape=[jax.ShapeDtypeStruct((t, D_MODEL), F32), jax.ShapeDtypeStruct((1, D_MODEL), F32)],
        compiler_params=_cp(("arbitrary",)),
    )(dz, wt, x_in, gpre, dres)


def _shift_matrices(shift, shift_h, tm, hb):
    row = lax.broadcasted_iota(jnp.int32, (2 * tm, tm), 0)
    col = lax.broadcasted_iota(jnp.int32, (2 * tm, tm), 1)
    hit = ((row < tm) & (col == row + 1)) | ((row >= tm) & (col == row - tm + 2))
    shift[...] = jnp.where(hit, 1.0, 0.0).astype(BF16)
    row = lax.broadcasted_iota(jnp.int32, (hb, hb), 0)
    col = lax.broadcasted_iota(jnp.int32, (hb, hb), 1)
    hit = ((row < HALO3) & (col == row - (HALO3 - 1))) | ((row >= HALO3) & (col == row - (2 * HALO3 - 2)))
    shift_h[...] = jnp.where(hit, 1.0, 0.0).astype(BF16)


def _next_rows(shift, shift_h, xb, nxt, d12_s, tm):
    d12_s[...] = jnp.dot(shift[...], xb, preferred_element_type=F32)
    edge = jnp.dot(shift_h[...], nxt, preferred_element_type=F32)
    d12_s[tm - HALO3:tm, :] += edge[0:HALO3, :]
    d12_s[2 * tm - HALO3:2 * tm, :] += edge[HALO3:2 * HALO3, :]


def _ffn_bwd(f, dxo, gpost, x_in, gpre, up, u, cw, wdt, wupt):
    t = f.shape[0]
    tm = min(TM_BWD, t)
    nt = t // tm
    nc, n = up.shape[1], up.shape[3]
    hb = 2 * HALO3

    def body(f_ref, dxo_ref, gpost_ref, x_ref, gpre_ref, up_ref, u_ref, cw_ref, wdt_ref, wupt_ref,
             df_ref, act_ref, dup_ref, dx_ref, dgpost_ref, dgpre_ref, dcw_ref, df_s, acc, du_s, dub_s, d12_s, hal, shift, shift_h):
        i, c = pl.program_id(0), pl.program_id(1)

        @pl.when(c == 0)
        def _():
            df, dg = _rms_bwd(f_ref[...], gpost_ref[...], dxo_ref[...])
            df_s[...] = df.astype(BF16)
            df_ref[...] = df.astype(BF16)
            _acc_out(dgpost_ref, i == 0, dg)

        @pl.when(i == 0)
        def _():
            hal[c] = jnp.zeros((2, hb, n), BF16)
            dcw_ref[0, c] = jnp.zeros((8, n), F32)
            dcw_ref[1, c] = jnp.zeros((8, n), F32)

        @pl.when(jnp.logical_and(i == 0, c == 0))
        def _():
            _shift_matrices(shift, shift_h, tm, hb)

        dact = jnp.dot(df_s[...], wdt_ref[0], preferred_element_type=F32)
        g, v = u_ref[0, 0].astype(F32), u_ref[1, 0].astype(F32)
        sg = _sigmoid(g)
        sil = g * sg
        act_ref[0] = (sil * v).astype(BF16)
        dug = dact * v * (sg + sil * (1.0 - sg))
        duv = dact * sil
        du_s[0], du_s[1] = dug, duv
        dub_s[0], dub_s[1] = dug.astype(BF16), duv.astype(BF16)
        dh = None
        for gv in range(2):
            _next_rows(shift, shift_h, dub_s[gv], hal[c, gv], d12_s, tm)
            hal[c, gv] = dub_s[gv, 0:hb, :]
            du, d1, d2 = du_s[gv], d12_s[0:tm, :], d12_s[tm:2 * tm, :]
            dup = (cw_ref[gv, 0, 2:3, :] * du + cw_ref[gv, 0, 1:2, :] * d1 + cw_ref[gv, 0, 0:1, :] * d2).astype(BF16)
            dup_ref[gv, 0] = dup
            upc = up_ref[gv, 0].astype(F32)
            dcw_ref[gv, c, 2:3, :] += jnp.sum(upc * du, axis=0, keepdims=True)
            dcw_ref[gv, c, 1:2, :] += jnp.sum(upc * d1, axis=0, keepdims=True)
            dcw_ref[gv, c, 0:1, :] += jnp.sum(upc * d2, axis=0, keepdims=True)
            part = jnp.dot(dup, wupt_ref[gv, 0], preferred_element_type=F32)
            dh = part if dh is None else dh + part
        _acc_out(acc, c == 0, dh)

        @pl.when(c == nc - 1)
        def _():
            dx, dg = _rms_bwd(x_ref[...], gpre_ref[...], acc[...])
            dx_ref[...] = dxo_ref[...] + dx
            _acc_out(dgpre_ref, i == 0, dg)

    rrow = lambda w: pl.BlockSpec((tm, w), lambda i, c: (nt - 1 - i, 0))
    blk = pl.BlockSpec((2, 1, tm, n), lambda i, c: (0, c, nt - 1 - i, 0))
    one = _full((1, D_MODEL))
    return pl.pallas_call(
        body, name="ffn_bwd", grid=(nt, nc),
        in_specs=[rrow(D_MODEL), rrow(D_MODEL), one, rrow(D_MODEL), one, blk, blk,
                  pl.BlockSpec((2, 1, 3, n), lambda i, c: (0, c, 0, 0)),
                  pl.BlockSpec((1, D_MODEL, n), lambda i, c: (c, 0, 0)),
                  pl.BlockSpec((2, 1, n, D_MODEL), lambda i, c: (0, c, 0, 0))],
        out_specs=[rrow(D_MODEL), pl.BlockSpec((1, tm, n), lambda i, c: (c, nt - 1 - i, 0)), blk, rrow(D_MODEL),
                   one, one, _full((2, nc, 8, n))],
        out_shape=[jax.ShapeDtypeStruct((t, D_MODEL), BF16), jax.ShapeDtypeStruct((nc, t, n), BF16),
                   jax.ShapeDtypeStruct((2, nc, t, n), BF16), jax.ShapeDtypeStruct((t, D_MODEL), F32),
                   jax.ShapeDtypeStruct((1, D_MODEL), F32), jax.ShapeDtypeStruct((1, D_MODEL), F32),
                   jax.ShapeDtypeStruct((2, nc, 8, n), F32)],
        scratch_shapes=[pltpu.VMEM((tm, D_MODEL), BF16), pltpu.VMEM((tm, D_MODEL), F32),
                        pltpu.VMEM((2, tm, n), F32), pltpu.VMEM((2, tm, n), BF16), pltpu.VMEM((2 * tm, n), F32),
                        pltpu.VMEM((nc, 2, hb, n), BF16), pltpu.VMEM((2 * tm, tm), BF16), pltpu.VMEM((hb, hb), BF16)],
        compiler_params=_cp(("arbitrary", "arbitrary")),
    )(f, dxo, gpost, x_in, gpre, up, u, cw, wdt, wupt)


def _od_bwd(m, dxo, gpost, x_in, gpre, z, cv, cw, wot, wint):
    t = m.shape[0]
    tm = min(TM_BWD, t)
    nt = t // tm
    hb = 2 * HALO3

    def body(m_ref, dxo_ref, gpost_ref, x_ref, gpre_ref, z_ref, cv_ref, cw_ref, wot_ref, wint_ref,
             dm_ref, dz_ref, dx_ref, dgpost_ref, dgpre_ref, dcw_ref, dcvb_s, d12_s, dz_s, hal, shift, shift_h):
        i = pl.program_id(0)
        dxo = dxo_ref[...]
        dm, dg = _rms_bwd(m_ref[...], gpost_ref[...], dxo)
        dmb = dm.astype(BF16)
        dm_ref[...] = dmb
        _acc_out(dgpost_ref, i == 0, dg)

        @pl.when(i == 0)
        def _():
            hal[...] = jnp.zeros((hb, SC_DIM), BF16)
            dcw_ref[...] = jnp.zeros((8, SC_DIM), F32)
            _shift_matrices(shift, shift_h, tm, hb)

        dy = jnp.dot(dmb, wot_ref[...], preferred_element_type=F32)
        z = z_ref[...].astype(F32)
        b, cg, u = z[:, :SC_DIM], z[:, SC_DIM:2 * SC_DIM], z[:, 2 * SC_DIM:]
        dz_s[:, 0:SC_DIM] = (dy * cv_ref[...].astype(F32)).astype(BF16)
        dcv = dy * b
        dcvb_s[...] = dcv.astype(BF16)
        _next_rows(shift, shift_h, dcvb_s[...], hal[...], d12_s, tm)
        hal[...] = dcvb_s[0:hb, :]
        d1, d2 = d12_s[0:tm, :], d12_s[tm:2 * tm, :]
        dcu = cw_ref[2:3, :] * dcv + cw_ref[1:2, :] * d1 + cw_ref[0:1, :] * d2
        cu = cg * u
        dcw_ref[2:3, :] += jnp.sum(cu * dcv, axis=0, keepdims=True)
        dcw_ref[1:2, :] += jnp.sum(cu * d1, axis=0, keepdims=True)
        dcw_ref[0:1, :] += jnp.sum(cu * d2, axis=0, keepdims=True)
        dz_s[:, SC_DIM:2 * SC_DIM] = (dcu * u).astype(BF16)
        dz_s[:, 2 * SC_DIM:3 * SC_DIM] = (dcu * cg).astype(BF16)
        dz_ref[...] = dz_s[...]
        dh = jnp.dot(dz_s[...], wint_ref[...], preferred_element_type=F32)
        dx, dg2 = _rms_bwd(x_ref[...], gpre_ref[...], dh)
        dx_ref[...] = dxo + dx
        _acc_out(dgpre_ref, i == 0, dg2)

    rrow = lambda w: pl.BlockSpec((tm, w), lambda i: (nt - 1 - i, 0))
    one = _full((1, D_MODEL))
    return pl.pallas_call(
        body, name="od_bwd", grid=(nt,),
        in_specs=[rrow(D_MODEL), rrow(D_MODEL), one, rrow(D_MODEL), one, rrow(3 * SC_DIM), rrow(SC_DIM),
                  _full((3, SC_DIM)), _full((D_MODEL, SC_DIM)), _full((3 * SC_DIM, D_MODEL))],
        out_specs=[rrow(D_MODEL), rrow(3 * SC_DIM), rrow(D_MODEL), one, one, _full((8, SC_DIM))],
        out_shape=[jax.ShapeDtypeStruct((t, D_MODEL), BF16), jax.ShapeDtypeStruct((t, 3 * SC_DIM), BF16),
                   jax.ShapeDtypeStruct((t, D_MODEL), F32), jax.ShapeDtypeStruct((1, D_MODEL), F32),
                   jax.ShapeDtypeStruct((1, D_MODEL), F32), jax.ShapeDtypeStruct((8, SC_DIM), F32)],
        scratch_shapes=[pltpu.VMEM((tm, SC_DIM), BF16), pltpu.VMEM((2 * tm, SC_DIM), F32),
                        pltpu.VMEM((tm, 3 * SC_DIM), BF16), pltpu.VMEM((hb, SC_DIM), BF16),
                        pltpu.VMEM((2 * tm, tm), BF16), pltpu.VMEM((hb, hb), BF16)],
        compiler_params=_cp(("arbitrary",)),
    )(m, dxo, gpost, x_in, gpre, z, cv, cw, wot, wint)


def _ev_bwd1(m, dxo, gpost, wot):
    t = m.shape[0]
    tm = min(TM, t)

    def body(m_ref, dxo_ref, g_ref, wot_ref, dm_ref, da_ref, do_ref, dg_ref):
        dm, dg = _rms_bwd(m_ref[...], g_ref[...], dxo_ref[...])
        dmb = dm.astype(BF16)
        dm_ref[...] = dmb
        _acc_out(dg_ref, pl.program_id(0) == 0, dg)
        dao = jnp.dot(dmb, wot_ref[...], preferred_element_type=F32)
        da_ref[...] = dao[:, :A_CH]
        do_ref[...] = dao[:, A_CH:].astype(BF16)

    return pl.pallas_call(
        body, name="ev_bwd1", grid=(t // tm,),
        in_specs=[_rows(tm, D_MODEL), _rows(tm, D_MODEL), _full((1, D_MODEL)), _full((D_MODEL, A_CH + Q_DIM))],
        out_specs=[_rows(tm, D_MODEL), _rows(tm, A_CH), _rows(tm, Q_DIM), _full((1, D_MODEL))],
        out_shape=[jax.ShapeDtypeStruct((t, D_MODEL), BF16), jax.ShapeDtypeStruct((t, A_CH), F32),
                   jax.ShapeDtypeStruct((t, Q_DIM), BF16), jax.ShapeDtypeStruct((1, D_MODEL), F32)],
        compiler_params=_cp(("arbitrary",)),
    )(m, dxo, gpost, wot)


def _conf_bwd(da, cv, zag, conv_w, ln_g, ln_b):
    t = da.shape[0]
    tm = min(TM_BWD, t)
    nt = t // tm

    def body(da_ref, c_ref, z_ref, zh_ref, w_ref, g_ref, lb_ref, dz_ref, dw_ref, dv_ref, ext_in, ext_out):
        i = pl.program_id(0)
        r = nt - 1 - i

        @pl.when(i == 0)
        def _():
            ext_out[tm:tm + HALO31, :] = jnp.zeros((HALO31, A_CH), F32)
            dw_ref[...] = jnp.zeros((32, A_CH), F32)
            dv_ref[...] = jnp.zeros((8, A_CH), F32)

        x = c_ref[...]
        mu = jnp.mean(x, axis=-1, keepdims=True)
        xc = x - mu
        rstd = lax.rsqrt(jnp.mean(xc * xc, axis=-1, keepdims=True) + LN_EPS)
        xh = xc * rstd
        ln = xh * g_ref[...] + lb_ref[...]
        sl = jax.nn.sigmoid(ln)
        dln = da_ref[...] * (sl * (1.0 + ln * (1.0 - sl)))
        dxh = dln * g_ref[...]
        dc = rstd * (dxh - jnp.mean(dxh, axis=-1, keepdims=True) - xh * jnp.mean(dxh * xh, axis=-1, keepdims=True))
        dv_ref[0:1, :] += jnp.sum(dc, axis=0, keepdims=True)
        dv_ref[1:2, :] += jnp.sum(dln * xh, axis=0, keepdims=True)
        dv_ref[2:3, :] += jnp.sum(dln, axis=0, keepdims=True)

        ext_out[0:tm, :] = dc
        dglu = jnp.zeros((tm, A_CH), F32)
        for j in range(A_CONV):
            s = A_CONV - 1 - j
            dglu = dglu + w_ref[j:j + 1, :] * ext_out[s:s + tm, :]
        ext_out[tm:tm + HALO31, :] = ext_out[0:HALO31, :]

        ext_in[0:HALO31, :] = jnp.where(r > 0, _glu(zh_ref[...]), 0.0)
        z = z_ref[...].astype(F32)
        al, sg = z[:, :A_CH], jax.nn.sigmoid(z[:, A_CH:])
        ext_in[HALO31:HALO31 + tm, :] = al * sg
        for j in range(A_CONV):
            s = HALO31 - (A_CONV - 1) + j
            dw_ref[j:j + 1, :] += jnp.sum(dc * ext_in[s:s + tm, :], axis=0, keepdims=True)
        dz_ref[:, 0:A_CH] = (dglu * sg).astype(BF16)
        dz_ref[:, A_CH:2 * A_CH] = (dglu * al * sg * (1.0 - sg)).astype(BF16)

    rrow = lambda w: pl.BlockSpec((tm, w), lambda i: (nt - 1 - i, 0))
    halo = pl.BlockSpec((HALO31, 2 * A_CH), lambda i: (jnp.maximum((nt - 1 - i) * (tm // HALO31) - 1, 0), 0))
    return pl.pallas_call(
        body, name="conf_bwd", grid=(nt,),
        in_specs=[rrow(A_CH), rrow(A_CH), rrow(2 * A_CH), halo, _full((32, A_CH)), _full((1, A_CH)),
                  _full((1, A_CH))],
        out_specs=[rrow(2 * A_CH), _full((32, A_CH)), _full((8, A_CH))],
        out_shape=[jax.ShapeDtypeStruct((t, 2 * A_CH), BF16), jax.ShapeDtypeStruct((32, A_CH), F32),
                   jax.ShapeDtypeStruct((8, A_CH), F32)],
        scratch_shapes=[pltpu.VMEM((HALO31 + tm, A_CH), F32), pltpu.VMEM((tm + HALO31, A_CH), F32)],
        compiler_params=_cp(("arbitrary",)),
    )(da, cv, zag, zag, conv_w, ln_g, ln_b)


def _attn_bwd(q, k, v, do, sinks):
    t = q.shape[0]
    nb = min(ATT_NB, t // BLOCK)
    rows = nb * BLOCK
    ns = t // rows

    def body(s_ref, q_ref, kc_ref, kp_ref, vc_ref, vp_ref, do_ref, dq_ref, dk_ref, dv_ref, ds_ref, dkc, dvc):
        i = pl.program_id(0)
        r = ns - 1 - i

        @pl.when(i == 0)
        def _():
            dkc[...] = jnp.zeros_like(dkc)
            dvc[...] = jnp.zeros_like(dvc)
            ds_ref[...] = jnp.zeros_like(ds_ref)

        lane = lax.broadcasted_iota(jnp.int32, (1, N_Q_HEADS), 1)
        dsv = jnp.zeros((1, N_Q_HEADS), F32)
        for b in range(nb - 1, -1, -1):
            lo = BLOCK * b
            mask = _attn_mask(r == 0) if b == 0 else _attn_mask(False)
            qv, dov = q_ref[lo:lo + BLOCK, :], do_ref[lo:lo + BLOCK, :]
            kc, vc = kc_ref[lo:lo + BLOCK, :], vc_ref[lo:lo + BLOCK, :]
            kp = kp_ref[...] if b == 0 else kc_ref[lo - BLOCK:lo, :]
            vp = vp_ref[...] if b == 0 else vc_ref[lo - BLOCK:lo, :]
            for h in range(N_KV_HEADS):
                q4, do4 = _q_heads(qv, h), _q_heads(dov, h)
                k2, v2 = _kv_head(kp, kc, h), _kv_head(vp, vc, h)
                pn, ps = _attn_probs(q4, k2, mask, _sink_rows(s_ref, h))
                dp = lax.dot_general(do4, v2, _CONTRACT_LAST, preferred_element_type=F32)
                dl = jnp.sum(pn * dp, axis=-1, keepdims=True)
                dsb = (pn * (dp - dl)).astype(BF16)
                dq4 = (jnp.dot(dsb, k2, preferred_element_type=F32) * SCALE).astype(BF16)
                for g in range(GROUP):
                    c0 = HEAD_DIM * (GROUP * h + g)
                    dq_ref[lo:lo + BLOCK, c0:c0 + HEAD_DIM] = dq4[BLOCK * g:BLOCK * (g + 1), :]
                dk2 = lax.dot_general(dsb, q4, _CONTRACT_FIRST, preferred_element_type=F32) * SCALE
                dv2 = lax.dot_general(pn.astype(BF16), do4, _CONTRACT_FIRST, preferred_element_type=F32)
                dk_ref[lo:lo + BLOCK, HEAD_DIM * h:HEAD_DIM * (h + 1)] = dk2[BLOCK:, :] + dkc[h]
                dv_ref[lo:lo + BLOCK, HEAD_DIM * h:HEAD_DIM * (h + 1)] = dv2[BLOCK:, :] + dvc[h]
                dkc[h] = dk2[:BLOCK, :]
                dvc[h] = dv2[:BLOCK, :]
                srow = -ps * dl
                for g in range(GROUP):
                    dsv = dsv + jnp.where(lane == GROUP * h + g, jnp.sum(srow[BLOCK * g:BLOCK * (g + 1), :]), 0.0)
        ds_ref[...] += dsv

    cur = lambda n: pl.BlockSpec((rows, n), lambda i: (ns - 1 - i, 0))
    prev = lambda n: pl.BlockSpec((BLOCK, n), lambda i: (jnp.maximum((ns - 1 - i) * nb - 1, 0), 0))
    return pl.pallas_call(
        body, name="attn_bwd", grid=(ns,),
        in_specs=[pl.BlockSpec(memory_space=pltpu.SMEM), cur(Q_DIM), cur(KV_DIM), prev(KV_DIM), cur(KV_DIM),
                  prev(KV_DIM), cur(Q_DIM)],
        out_specs=[cur(Q_DIM), cur(KV_DIM), cur(KV_DIM), _full((1, N_Q_HEADS))],
        out_shape=[jax.ShapeDtypeStruct((t, Q_DIM), BF16), jax.ShapeDtypeStruct((t, KV_DIM), F32),
                   jax.ShapeDtypeStruct((t, KV_DIM), F32), jax.ShapeDtypeStruct((1, N_Q_HEADS), F32)],
        scratch_shapes=[pltpu.VMEM((N_KV_HEADS, BLOCK, HEAD_DIM), F32), pltpu.VMEM((N_KV_HEADS, BLOCK, HEAD_DIM), F32)],
        compiler_params=_cp(("arbitrary",)),
    )(sinks, q, k, k, v, v, do)


def _ev_dz(dzag, dq, dk, dv, rc, rsa, rsb):
    t = dzag.shape[0]
    tm = min(TM, t)

    def body(dzag_ref, dq_ref, dk_ref, dv_ref, c_ref, sa_ref, sb_ref, dz_ref):
        c, sa, sb = c_ref[...], sa_ref[...], sb_ref[...]
        dz_ref[:, 0:2 * A_CH] = dzag_ref[...]
        q0 = 2 * A_CH
        for j in range(Q_DIM // 128):
            d = dq_ref[:, 128 * j:128 * (j + 1)].astype(F32)
            dz_ref[:, q0 + 128 * j:q0 + 128 * (j + 1)] = _rope_bwd(d, c, sa, sb).astype(BF16)
        k0 = q0 + Q_DIM
        dz_ref[:, k0:k0 + KV_DIM] = _rope_bwd(dk_ref[...], c, sa, sb).astype(BF16)
        dz_ref[:, k0 + KV_DIM:k0 + 2 * KV_DIM] = dv_ref[...].astype(BF16)

    return pl.pallas_call(
        body, name="ev_dz", grid=(t // tm,),
        in_specs=[_rows(tm, 2 * A_CH), _rows(tm, Q_DIM), _rows(tm, KV_DIM), _rows(tm, KV_DIM),
                  _rows(tm, 128), _rows(tm, 128), _rows(tm, 128)],
        out_specs=_rows(tm, EVEN_IN),
        out_shape=jax.ShapeDtypeStruct((t, EVEN_IN), BF16),
        compiler_params=_cp(("arbitrary",)),
    )(dzag, dq, dk, dv, rc, rsa, rsb)


def _prep_ev(gat):
    p = {}
    w = gat["ev_w_in"][:, 0].transpose(1, 0, 2).reshape(D_MODEL, EVEN_IN)
    p["ev_w_in"], p["ev_w_in_t"] = w, w.T
    w = gat["ev_w_out"].reshape(A_CH + Q_DIM, D_MODEL)
    p["ev_w_out"], p["ev_w_out_t"] = w, w.T
    return p


def _prep_rest(gat):
    p = {}
    g = gat["od_w_in"][:, 0]
    p["od_w_in"], p["od_w_in_t"] = g, g.transpose(0, 2, 1).reshape(3 * SC_DIM, D_MODEL)
    w = gat["od_w_out"].reshape(SC_DIM, D_MODEL)
    p["od_w_out"], p["od_w_out_t"] = w, w.T
    g = gat["ffn_w_up"]
    p["ffn_w_up"] = g.reshape(2, N_DEV // 2, 2, D_MODEL, FF_N)
    p["ffn_w_up_t"] = [g[:, i].transpose(0, 2, 1).reshape(2, N_DEV // 2, FF_N, D_MODEL) for i in range(2)]
    g = gat["ffn_w_down"]
    p["ffn_w_down"] = [g[:, i].reshape(D_FF, D_MODEL) for i in range(2)]
    p["ffn_w_down_t"] = [w.reshape(N_DEV // 2, FF_N, D_MODEL).transpose(0, 2, 1) for w in p["ffn_w_down"]]
    return p


def _local_step(x, positions, target, p, rest_weights, s, token, grads_ready):
    row = lambda a, tok=None: a.reshape(1, -1) if tok is None else a.reshape(1, -1) + tok
    nc = N_DEV // 2
    rc, rsa, rsb = _rope_tables(positions)
    conv31 = jnp.pad(s["ev_a_conv_w"][0], ((0, 1), (0, 0)))
    cw_ffn = [s["ffn_conv_w"][i].reshape(3, 2, nc, FF_N).transpose(1, 2, 0, 3) for i in range(2)]
    sinks = s["ev_sinks"][0]
    big, g = {}, {}

    h0, zag, q, k, v = _ev_in(x, row(s["mix_norm_pre"][0], token), p["ev_w_in"], rc, rsa, rsb)
    cv, a = _conf_fwd(zag, conv31, s["ev_a_conv_b"], s["ev_a_ln_g"], s["ev_a_ln_b"])
    o = _attn_fwd(q, k, v, sinks)
    wo = p["ev_w_out"]
    m0, x1 = _out_post([a, o], [wo[:A_CH], wo[A_CH:]], x, row(s["mix_norm_post"][0]))
    p = {**p, **rest_weights(m0)}
    h1, up0, u0, f0, x2 = _ffn_fwd(x1, row(s["ffn_norm_pre"][0]), p["ffn_w_up"], 0, cw_ffn[0], p["ffn_w_down"][0],
                                   row(s["ffn_norm_post"][0]))
    h2, z, cv1, y, m1, x3 = _od_fwd(x2, row(s["mix_norm_pre"][1]), p["od_w_in"], s["od_conv_w"][0], p["od_w_out"],
                                    row(s["mix_norm_post"][1]))
    h3, up1, u1, f1, x4 = _ffn_fwd(x3, row(s["ffn_norm_pre"][1]), p["ffn_w_up"], 1, cw_ffn[1], p["ffn_w_down"][1],
                                   row(s["ffn_norm_post"][1]))
    dx, lpart = _loss_grad(x4, target)

    def ffn_back(i, f, dxo, up, u, h, x_in, bufs, tok=None):
        df, act, dup, dx_in, dgpost, dgpre, dcw = _ffn_bwd(
            f, dxo, row(s["ffn_norm_post"][i], tok), x_in, row(s["ffn_norm_pre"][i]), up, u, cw_ffn[i],
            p["ffn_w_down_t"][i], p["ffn_w_up_t"][i])
        bufs = (_dw_up(h, dup.reshape(N_DEV, -1, FF_N), i, bufs[0]), _dw_down(act, df, i, bufs[1]))
        return dx_in, dgpost, dgpre, dcw[:, :, 0:3].transpose(2, 0, 1, 3).reshape(3, 2 * D_FF), bufs

    dx, dgfpost1, dgfpre1, dcw1, bufs = ffn_back(1, f1, dx, up1, u1, h3, x3, (None, None))

    dm1, dz, dx, dgpost1, dgpre1, dcw_od = _od_bwd(m1, dx, row(s["mix_norm_post"][1]), x2, row(s["mix_norm_pre"][1]), z,
                                                   cv1, s["od_conv_w"][0], p["od_w_out_t"], p["od_w_in_t"])
    big["od_w_out"] = _dw2d(y, dm1, SC_DIM, D_MODEL).reshape(N_DEV, -1, D_MODEL)
    big["od_w_in"] = _dw_cols(h2, dz, 3 * SC_DIM // N_DEV)
    g["od_conv_w"] = dcw_od[None, 0:3]
    tok = grads_ready(["od_w_in", "od_w_out"], big)

    dx, dgfpost0, dgfpre0, dcw0, bufs = ffn_back(0, f0, dx, up0, u0, h1, x1, bufs, tok)
    big["ffn_w_up"], big["ffn_w_down"] = bufs
    tok = grads_ready(["ffn_w_up", "ffn_w_down"], big)

    dm0, da, do, dgpost0 = _ev_bwd1(m0, dx, row(s["mix_norm_post"][0], tok), p["ev_w_out_t"])
    big["ev_w_out"] = jnp.concatenate([_dw2d(a, dm0, A_CH, D_MODEL), _dw2d(o, dm0, Q_DIM, D_MODEL)],
                                      axis=0).reshape(N_DEV, -1, D_MODEL)
    dzag, dcw31, dvec = _conf_bwd(da, cv, zag, conv31, s["ev_a_ln_g"], s["ev_a_ln_b"])
    dq, dk, dv, dsinks = _attn_bwd(q, k, v, do, sinks)
    dz0 = _ev_dz(dzag, dq, dk, dv, rc, rsa, rsb)
    dw_in = _dw2d(h0, dz0, D_MODEL, EVEN_IN // 2)
    big["ev_w_in"] = dw_in.reshape(D_MODEL, N_DEV, EVEN_IN // N_DEV).transpose(1, 0, 2)
    dx, dgpre0 = _dz_wt_rms_bwd(dz0, p["ev_w_in_t"], x, row(s["mix_norm_pre"][0]), dx)
    grads_ready(["ev_w_in", "ev_w_out"], big)

    g["mix_norm_pre"] = jnp.concatenate([dgpre0, dgpre1], axis=0)
    g["mix_norm_post"] = jnp.concatenate([dgpost0, dgpost1], axis=0)
    g["ffn_norm_pre"] = jnp.concatenate([dgfpre0, dgfpre1], axis=0)
    g["ffn_norm_post"] = jnp.concatenate([dgfpost0, dgfpost1], axis=0)
    g["ev_a_conv_w"] = dcw31[None, 0:A_CONV]
    g["ev_a_conv_b"], g["ev_a_ln_g"], g["ev_a_ln_b"] = dvec[0:1], dvec[1:2], dvec[2:3]
    g["ev_sinks"] = dsinks
    g["ffn_conv_w"] = jnp.stack([dcw0, dcw1])
    return lpart[0, 0], dx, big, g


MESH = pl.DeviceIdType.MESH


def _all_gather(shards, name):
    nw = len(shards)

    def body(*refs):
        x_refs, out_refs = refs[:nw], refs[nw:2 * nw]
        send_sems, recv_sems, local_sems = refs[2 * nw:]
        x, y, c = lax.axis_index("x"), lax.axis_index("y"), lax.axis_index("c")
        me, sibling = (x, y, c), (x, y, 1 - c)
        chips = [(1 - x, y), (x, 1 - y), (1 - x, 1 - y)]

        def rows(w, px, py, pc):
            m_per = shards[w].shape[0]
            return out_refs[w].at[pl.ds((4 * px + 2 * py + pc) * m_per, m_per), :]

        def copy(w, k, block, to, src=None):
            return pltpu.make_async_remote_copy(
                src_ref=rows(w, *block) if src is None else src, dst_ref=rows(w, *block),
                send_sem=send_sems.at[w, k], recv_sem=recv_sems.at[w, k], device_id=to, device_id_type=MESH)

        mine, first, passed = [], [], []
        for w in range(nw):
            cp = pltpu.make_async_copy(x_refs[w], rows(w, *me), local_sems.at[w])
            cp.start()
            mine.append(cp)
            first.append([copy(w, 0, me, sibling, src=x_refs[w])]
                         + [copy(w, 1 + j, me, (*chip, c), src=x_refs[w]) for j, chip in enumerate(chips)])
            for cp in first[w]:
                cp.start()
        for w in range(nw):
            passed.append([copy(w, 4 + j, (*chip, c), sibling) for j, chip in enumerate(chips)])
            for j, chip in enumerate(chips):
                copy(w, 1 + j, (*chip, c), me).wait_recv()
                passed[w][j].start()
        for w in range(nw):
            copy(w, 0, sibling, me).wait_recv()
            for j, chip in enumerate(chips):
                copy(w, 4 + j, (*chip, 1 - c), me).wait_recv()
            for cp in first[w] + passed[w]:
                cp.wait_send()
            mine[w].wait()

    return pl.pallas_call(
        body, name=name,
        out_shape=[jax.ShapeDtypeStruct((N_DEV * a.shape[0], a.shape[1]), a.dtype) for a in shards],
        in_specs=[_ANY] * nw, out_specs=[_ANY] * nw,
        scratch_shapes=[pltpu.SemaphoreType.DMA((nw, 7)), pltpu.SemaphoreType.DMA((nw, 7)),
                        pltpu.SemaphoreType.DMA((nw,))],
    )(*shards)


_HBM = pl.BlockSpec(memory_space=pltpu.HBM)
_SEM = pl.BlockSpec(memory_space=pltpu.SEMAPHORE)
_EFFECT = pltpu.SideEffectType.DATAFLOW_SIDE_EFFECTING
_RELATIONS = [(dx, dy, dc) for dx in (0, 1) for dy in (0, 1) for dc in (0, 1)][1:]


def _peer(rel):
    x, y, c = lax.axis_index("x"), lax.axis_index("y"), lax.axis_index("c")
    px, py, pc = x ^ rel[0], y ^ rel[1], c ^ rel[2]
    return (px, py, pc), 4 * px + 2 * py + pc, 4 * x + 2 * y + c


def _exchange_copy(k, rel, src_ref, land_ref, send_sems, recv_sems, w, scatter):
    peer, peer_idx, my_idx = _peer(rel)
    src = src_ref.at[peer_idx] if scatter else src_ref
    return pltpu.make_async_remote_copy(src_ref=src, dst_ref=land_ref.at[my_idx], send_sem=send_sems.at[7 * w + k],
                                        recv_sem=recv_sems.at[7 * w + k], device_id=peer, device_id_type=MESH)


def _exchange_start(srcs, scatter, name):
    nw = len(srcs)
    lands = [lax.empty((N_DEV,) + (a.shape[1:] if scatter else a.shape), a.dtype) for a in srcs]

    def body(*refs):
        src_refs, land_refs = refs[:nw], refs[nw:2 * nw]
        send_sems, recv_sems = refs[2 * nw], refs[2 * nw + 1]
        token = refs[-1]
        for w in range(nw):
            for k, rel in enumerate(_RELATIONS):
                _exchange_copy(k, rel, src_refs[w], land_refs[w], send_sems, recv_sems, w, scatter).start()
        token[...] = jnp.zeros_like(token)

    hbm = lambda a: pltpu.HBM(a.shape, a.dtype)
    outs = pl.pallas_call(
        body, name=name,
        out_shape=(pltpu.SemaphoreType.DMA((7 * nw,)), pltpu.SemaphoreType.DMA((7 * nw,)), *[hbm(a) for a in srcs],
                   *[hbm(a) for a in lands], jax.ShapeDtypeStruct((8, 128), F32)),
        in_specs=[_HBM] * (2 * nw),
        out_specs=(_SEM, _SEM, *[_HBM] * (2 * nw), pl.BlockSpec(memory_space=pltpu.VMEM)),
        input_output_aliases={i: 2 + i for i in range(2 * nw)},
        compiler_params=pltpu.CompilerParams(has_side_effects=_EFFECT),
    )(*[pltpu.with_memory_space_constraint(a, pltpu.HBM) for a in srcs],
      *[pltpu.with_memory_space_constraint(a, pltpu.HBM) for a in lands])
    return outs[0], outs[1], list(outs[2:2 + nw]), list(outs[2 + nw:2 + 2 * nw]), outs[-1]


def _exchange_wait(started, scatter, after, name):
    send_sems, recv_sems, srcs, lands, _ = started
    nw = len(srcs)

    def body(*refs):
        src_refs, land_refs = refs[:nw], refs[nw:2 * nw]
        send_s, recv_s = refs[2 * nw], refs[2 * nw + 1]
        for w in range(nw):
            for k, rel in enumerate(_RELATIONS):
                cp = _exchange_copy(k, rel, src_refs[w], land_refs[w], send_s, recv_s, w, scatter)
                cp.wait_send()
                _, peer_idx, _ = _peer(rel)
                pltpu.make_async_remote_copy(
                    src_ref=src_refs[w].at[peer_idx] if scatter else src_refs[w], dst_ref=land_refs[w].at[peer_idx],
                    send_sem=send_s.at[7 * w + k], recv_sem=recv_s.at[7 * w + k], device_id=_peer(rel)[0],
                    device_id_type=MESH).wait_recv()

    hbm = lambda a: pltpu.HBM(a.shape, a.dtype)
    outs = pl.pallas_call(
        body, name=name, out_shape=tuple(hbm(a) for a in srcs + lands),
        in_specs=[_HBM] * (2 * nw) + [_SEM, _SEM, _ANY], out_specs=tuple([_HBM] * (2 * nw)),
        input_output_aliases={i: i for i in range(2 * nw)},
        compiler_params=pltpu.CompilerParams(has_side_effects=_EFFECT),
    )(*srcs, *lands, send_sems, recv_sems, after)
    return list(outs[nw:])


def _place_own(land, shard, dev):
    def body(d_ref, land_ref, x_ref, o_ref, sem):
        cp = pltpu.make_async_copy(x_ref, o_ref.at[d_ref[0]], sem)
        cp.start()
        cp.wait()

    return pl.pallas_call(
        body, name="place_own", out_shape=jax.ShapeDtypeStruct(land.shape, land.dtype),
        grid_spec=pltpu.PrefetchScalarGridSpec(num_scalar_prefetch=1, grid=(1,), in_specs=[_ANY, _ANY], out_specs=_ANY,
                                               scratch_shapes=[pltpu.SemaphoreType.DMA]),
        input_output_aliases={1: 0},
    )(dev, land, shard)


def _to_bf16(a):
    _, r, l = a.shape
    tr = _row_tile(r, 512)

    def body(a_ref, o_ref):
        o_ref[...] = a_ref[...].astype(BF16)

    spec = pl.BlockSpec((1, tr, l), lambda j, i: (j, i, 0))
    return pl.pallas_call(
        body, name="to_bf16", grid=(N_DEV, r // tr), in_specs=[spec], out_specs=spec,
        out_shape=jax.ShapeDtypeStruct(a.shape, BF16), compiler_params=_cp(("arbitrary", "arbitrary")),
    )(a)


def _row_tile(rows, cap):
    best = None
    for d in range(16, min(rows, cap) + 1, 16):
        if rows % d == 0:
            best = d
    return rows if best is None else best


def _adam_math(w, g, m, v):
    bc1 = 1.0 - ADAM_B1 ** ADAM_STEP
    bc2 = 1.0 - ADAM_B2 ** ADAM_STEP
    mn = ADAM_B1 * m + (1.0 - ADAM_B1) * g
    vn = ADAM_B2 * v + (1.0 - ADAM_B2) * (g * g)
    return -ADAM_LR * ((mn / bc1) / (jnp.sqrt(vn / bc2) + ADAM_EPS) + ADAM_WD * w), mn, vn


def _adamw_rs(gp, land, w, m, v, dev):
    _, r, l = gp.shape
    tr = _row_tile(r, 256)

    def body(i_ref, g_ref, b_ref, w_ref, m_ref, v_ref, go_ref, d_ref, mo_ref, vo_ref):
        g = g_ref[0]
        for j in range(N_DEV):
            g = g + jnp.where(i_ref[0] == j, 0.0, b_ref[j].astype(F32))
        go_ref[...] = g
        d_ref[...], mo_ref[...], vo_ref[...] = _adam_math(w_ref[...], g, m_ref[...], v_ref[...])

    spec = pl.BlockSpec((tr, l), lambda i, s: (i, 0))
    return pl.pallas_call(
        body, name="adamw_rs", out_shape=[jax.ShapeDtypeStruct((r, l), F32)] * 4,
        grid_spec=pltpu.PrefetchScalarGridSpec(
            num_scalar_prefetch=1, grid=(r // tr,),
            in_specs=[pl.BlockSpec((1, tr, l), lambda i, s: (s[0], i, 0)),
                      pl.BlockSpec((N_DEV, tr, l), lambda i, s: (0, i, 0)), spec, spec, spec],
            out_specs=[spec] * 4),
        compiler_params=_cp(("arbitrary",)),
    )(dev, gp, land, w, m, v)


def _sum_blocks(a, nblk):
    m = a.shape[0] // nblk
    n = a.shape[1]

    def body(a_ref, o_ref):
        acc = a_ref[0]
        for j in range(1, nblk):
            acc = acc + a_ref[j]
        o_ref[...] = acc

    return pl.pallas_call(
        body, name="sum_blocks", out_shape=jax.ShapeDtypeStruct((m, n), a.dtype),
        in_specs=[_full((nblk, m, n))], out_specs=_full((m, n)),
    )(a.reshape(nblk, m, n))


def _adamw(w, g, m, v):
    rows, c = w.shape

    def body(w_ref, g_ref, m_ref, v_ref, d_ref, mo_ref, vo_ref):
        d_ref[...], mo_ref[...], vo_ref[...] = _adam_math(w_ref[...], g_ref[...], m_ref[...], v_ref[...])

    return pl.pallas_call(
        body, name="adamw", in_specs=[_full((rows, c))] * 4, out_specs=[_full((rows, c))] * 3,
        out_shape=[jax.ShapeDtypeStruct((rows, c), F32)] * 3,
    )(w, g, m, v)


WEIGHTS = ["mix_norm_pre", "mix_norm_post", "ffn_norm_pre", "ffn_norm_post", "ev_w_in", "ev_a_conv_w", "ev_a_conv_b",
           "ev_a_ln_g", "ev_a_ln_b", "ev_sinks", "ev_w_out", "od_w_in", "od_conv_w", "od_w_out", "ffn_w_up",
           "ffn_conv_w", "ffn_w_down"]
BIG = ["ev_w_in", "ev_w_out", "od_w_in", "od_w_out", "ffn_w_up", "ffn_w_down"]
SMALL_REPL = ["mix_norm_pre", "mix_norm_post", "ffn_norm_pre", "ffn_norm_post", "ev_a_conv_b", "ev_a_ln_g",
              "ev_a_ln_b", "ev_sinks"]
SMALL_SHARDED = ["ev_a_conv_w", "od_conv_w", "ffn_conv_w"]


def _pack(arrs, rows):
    flat = jnp.concatenate([a.reshape(-1) for a in arrs])
    return jnp.pad(flat, (0, rows * LANES - flat.shape[0])).reshape(rows, LANES)


def _unpack(packed, shapes):
    flat, out, off = packed.reshape(-1), [], 0
    for s in shapes:
        n = 1
        for d in s:
            n *= d
        out.append(flat[off:off + n].reshape(s))
        off += n
    return out


def kernel(x, positions, mix_norm_pre, mix_norm_post, ffn_norm_pre, ffn_norm_post, ev_w_in, ev_a_conv_w, ev_a_conv_b, ev_a_ln_g, ev_a_ln_b, ev_sinks, ev_w_out, od_w_in, od_conv_w, od_w_out, ffn_w_up, ffn_conv_w, ffn_w_down, loss_target, m_mix_norm_pre, m_mix_norm_post, m_ffn_norm_pre, m_ffn_norm_post, m_ev_w_in, m_ev_a_conv_w, m_ev_a_conv_b, m_ev_a_ln_g, m_ev_a_ln_b, m_ev_sinks, m_ev_w_out, m_od_w_in, m_od_conv_w, m_od_w_out, m_ffn_w_up, m_ffn_conv_w, m_ffn_w_down, v_mix_norm_pre, v_mix_norm_post, v_ffn_norm_pre, v_ffn_norm_post, v_ev_w_in, v_ev_a_conv_w, v_ev_a_conv_b, v_ev_a_ln_g, v_ev_a_ln_b, v_ev_sinks, v_ev_w_out, v_od_w_in, v_od_conv_w, v_od_w_out, v_ffn_w_up, v_ffn_conv_w, v_ffn_w_down):
    w = dict(zip(WEIGHTS, (mix_norm_pre, mix_norm_post, ffn_norm_pre, ffn_norm_post, ev_w_in, ev_a_conv_w, ev_a_conv_b,
                           ev_a_ln_g, ev_a_ln_b, ev_sinks, ev_w_out, od_w_in, od_conv_w, od_w_out, ffn_w_up, ffn_conv_w,
                           ffn_w_down)))
    mom = dict(zip(WEIGHTS, (m_mix_norm_pre, m_mix_norm_post, m_ffn_norm_pre, m_ffn_norm_post, m_ev_w_in, m_ev_a_conv_w,
                             m_ev_a_conv_b, m_ev_a_ln_g, m_ev_a_ln_b, m_ev_sinks, m_ev_w_out, m_od_w_in, m_od_conv_w,
                             m_od_w_out, m_ffn_w_up, m_ffn_conv_w, m_ffn_w_down)))
    var = dict(zip(WEIGHTS, (v_mix_norm_pre, v_mix_norm_post, v_ffn_norm_pre, v_ffn_norm_post, v_ev_w_in, v_ev_a_conv_w,
                             v_ev_a_conv_b, v_ev_a_ln_g, v_ev_a_ln_b, v_ev_sinks, v_ev_w_out, v_od_w_in, v_od_conv_w,
                             v_od_w_out, v_ffn_w_up, v_ffn_conv_w, v_ffn_w_down)))
    ix, iy, ic = lax.axis_index("x"), lax.axis_index("y"), lax.axis_index("c")
    dev = 4 * ix + 2 * iy + ic
    two = lambda a: a.reshape(-1, a.shape[-1])

    dev1 = jnp.reshape(dev, (1,)).astype(jnp.int32)
    shard = {n: two(w[n].astype(BF16)) for n in BIG}
    gathered = lambda n, a: a.reshape((N_DEV,) + w[n].shape)
    rest_names = [n for n in BIG if not n.startswith("ev_")]
    started = _exchange_start([shard[n] for n in rest_names], False, "gather_start")
    ev_names = [n for n in BIG if n.startswith("ev_")]
    p = _prep_ev({n: gathered(n, a) for n, a in zip(ev_names, _all_gather([shard[n] for n in ev_names], "gather_ev"))})

    def rest_weights(after):
        lands = _exchange_wait(started, False, after, "gather_wait")
        return _prep_rest({n: gathered(n, _place_own(a, shard[n], dev1)) for n, a in zip(rest_names, lands)})

    small = {n: w[n] for n in SMALL_REPL}
    small_shapes = [w[n].shape for n in SMALL_SHARDED]
    conv_gat = _all_gather([_pack([w[n] for n in SMALL_SHARDED], 8)], "gather_conv")[0].reshape(N_DEV, 8, LANES)
    per_dev = [_unpack(conv_gat[d], small_shapes) for d in range(N_DEV)]
    for k, n in enumerate(SMALL_SHARDED):
        small[n] = jnp.concatenate([per_dev[d][k] for d in range(N_DEV)], axis=-1)

    exchanges = []

    def grads_ready(names, big):
        bufs = [big[n].reshape(N_DEV, -1, w[n].shape[-1]) for n in names]
        st = _exchange_start([_to_bf16(b) for b in bufs], True, "grads_start_" + names[0])
        exchanges.append((names, bufs, st))
        return st[-1][0, 0]

    lpart, grad_x, big, g = _local_step(x[0], positions[0], loss_target[0], p, rest_weights, small, started[-1][0, 0],
                                        grads_ready)
    loss = lax.psum(lpart, ("x", "y", "c"))

    grads, delta, new_m, new_v = {}, {}, {}, {}
    for names, bufs, st in exchanges:
        lands = _exchange_wait(st, True, grad_x, "grads_wait_" + names[0])
        for n, b, land in zip(names, bufs, lands):
            outs = _adamw_rs(b, land, two(w[n]), two(mom[n]), two(var[n]), dev1)
            grads[n], delta[n], new_m[n], new_v[n] = (a.reshape(w[n].shape) for a in outs)

    small_names = SMALL_REPL + SMALL_SHARDED
    s_all = _sum_blocks(_all_gather([_pack([g[n] for n in small_names], 64)], "gather_small_grads")[0], N_DEV)
    for n, a in zip(small_names, _unpack(s_all, [small[n].shape for n in small_names])):
        if n in SMALL_SHARDED:
            width = w[n].shape[-1]
            a = lax.dynamic_slice_in_dim(a, dev * width, width, axis=a.ndim - 1)
        grads[n] = a
    pk = lambda dct: _pack([dct[n] for n in small_names], 16)
    outs = _adamw(pk(w), pk(grads), pk(mom), pk(var))
    for dst, packed in zip((delta, new_m, new_v), outs):
        for n, a in zip(small_names, _unpack(packed, [w[n].shape for n in small_names])):
            dst[n] = a

    return (loss, grad_x[None], *[grads[n] for n in WEIGHTS], *[delta[n] for n in WEIGHTS],
            *[new_m[n] for n in WEIGHTS], *[new_v[n] for n in WEIGHTS])
```

```python
import jax
import jax.numpy as jnp
from jax import lax
from jax.experimental import pallas as pl
from jax.experimental.pallas import tpu as pltpu

F32, BF16 = jnp.float32, jnp.bfloat16

D_MODEL = 1024
A_CH = 512
A_CONV = 31
Q_DIM = 512
KV_DIM = 128
HEAD_DIM = 64
N_Q_HEADS = 8
N_KV_HEADS = 2
GROUP = 4
BLOCK = 128
EVEN_IN = 1792
SC_DIM = 1024
D_FF = 2816
ROPE_THETA = 500000.0
ROPE_DIM = 16
RMS_EPS = 1e-6
LN_EPS = 1e-5
SCALE = HEAD_DIM ** -0.5
NEG = -1e30

ADAM_LR, ADAM_B1, ADAM_B2, ADAM_EPS, ADAM_WD, ADAM_STEP = 0.001, 0.9, 0.999, 1e-08, 0.01, 10

N_DEV = 8
FF_N = 2 * D_FF // N_DEV
LANES = 1024
HALO3 = 8
HALO31 = 32
VMEM_LIMIT = 56 * 1024 * 1024

TM = 512
TM_BWD = 256
TK_DW = 2048
ATT_NB = 4

_ANY = pl.BlockSpec(memory_space=pl.ANY)
_CONTRACT_LAST = (((1,), (1,)), ((), ()))
_CONTRACT_FIRST = (((0,), (0,)), ((), ()))


def _cp(sem, vmem=VMEM_LIMIT):
    return pltpu.CompilerParams(dimension_semantics=sem, vmem_limit_bytes=vmem)


def _full(shape):
    n = len(shape)
    return pl.BlockSpec(shape, lambda *_: (0,) * n)


def _rows(tm, n):
    return pl.BlockSpec((tm, n), lambda i, *_: (i, 0))


def _sigmoid(x):
    return 0.5 * jnp.tanh(0.5 * x) + 0.5


def _rsqrt_mean(x):
    return lax.rsqrt(jnp.mean(x * x, axis=-1, keepdims=True) + RMS_EPS)


def _rms_bwd(x, g, dy):
    r = _rsqrt_mean(x)
    xh = x * r
    dxh = dy * g
    dx = r * (dxh - xh * jnp.mean(dxh * xh, axis=-1, keepdims=True))
    return dx, jnp.sum(dy * xh, axis=0, keepdims=True)


def _acc_out(ref, first, val):
    @pl.when(first)
    def _():
        ref[...] = val

    @pl.when(jnp.logical_not(first))
    def _():
        ref[...] += val


def _rope_tables(positions):
    half = ROPE_DIM // 2
    inv_freq = ROPE_THETA ** (-(jnp.arange(half, dtype=F32) * 2.0 / ROPE_DIM))
    ang = positions.astype(F32)[:, None] * inv_freq
    cos, sin = jnp.cos(ang), jnp.sin(ang)
    t = positions.shape[0]
    one, zero = jnp.ones((t, HEAD_DIM - ROPE_DIM), F32), jnp.zeros((t, HEAD_DIM - ROPE_DIM), F32)
    z8 = jnp.zeros((t, half), F32)
    c = jnp.concatenate([cos, cos, one], axis=1)
    sa = jnp.concatenate([z8, sin, zero], axis=1)
    sb = jnp.concatenate([-sin, z8, zero], axis=1)
    return tuple(jnp.tile(a, (1, 2)) for a in (c, sa, sb))


def _rope(t, c, sa, sb):
    return t * c + pltpu.roll(t, 8, 1) * sa + pltpu.roll(t, 120, 1) * sb


def _rope_bwd(d, c, sa, sb):
    return d * c + pltpu.roll(d * sa, 120, 1) + pltpu.roll(d * sb, 8, 1)


def _ev_in(x, gpre, w_in, rc, rsa, rsb):
    t = x.shape[0]
    tm = min(TM, t)

    def body(x_ref, g_ref, w_ref, c_ref, sa_ref, sb_ref, h_ref, zag_ref, q_ref, k_ref, v_ref):
        xv = x_ref[...]
        h = (xv * _rsqrt_mean(xv) * g_ref[...]).astype(BF16)
        h_ref[...] = h
        z = jnp.dot(h, w_ref[...], preferred_element_type=F32)
        zag_ref[...] = z[:, :2 * A_CH].astype(BF16)
        c, sa, sb = c_ref[...], sa_ref[...], sb_ref[...]
        q0 = 2 * A_CH
        for j in range(Q_DIM // 128):
            q_ref[:, 128 * j:128 * (j + 1)] = _rope(z[:, q0 + 128 * j:q0 + 128 * (j + 1)], c, sa, sb).astype(BF16)
        k0 = q0 + Q_DIM
        k_ref[...] = _rope(z[:, k0:k0 + KV_DIM], c, sa, sb).astype(BF16)
        v_ref[...] = z[:, k0 + KV_DIM:k0 + 2 * KV_DIM].astype(BF16)

    return pl.pallas_call(
        body, name="ev_in", grid=(t // tm,),
        in_specs=[_rows(tm, D_MODEL), _full((1, D_MODEL)), _full((D_MODEL, EVEN_IN)),
                  _rows(tm, 128), _rows(tm, 128), _rows(tm, 128)],
        out_specs=[_rows(tm, D_MODEL), _rows(tm, 2 * A_CH), _rows(tm, Q_DIM), _rows(tm, KV_DIM), _rows(tm, KV_DIM)],
        out_shape=[jax.ShapeDtypeStruct((t, D_MODEL), BF16), jax.ShapeDtypeStruct((t, 2 * A_CH), BF16),
                   jax.ShapeDtypeStruct((t, Q_DIM), BF16), jax.ShapeDtypeStruct((t, KV_DIM), BF16),
                   jax.ShapeDtypeStruct((t, KV_DIM), BF16)],
        compiler_params=_cp(("arbitrary",)),
    )(x, gpre, w_in, rc, rsa, rsb)


def _glu(zag):
    z = zag.astype(F32)
    return z[:, :A_CH] * jax.nn.sigmoid(z[:, A_CH:])


def _conf_fwd(zag, conv_w, conv_b, ln_g, ln_b):
    t = zag.shape[0]
    tm = min(TM_BWD, t)

    def body(z_ref, w_ref, b_ref, g_ref, lb_ref, c_ref, a_ref, ext):
        i = pl.program_id(0)

        @pl.when(i == 0)
        def _():
            ext[0:HALO31, :] = jnp.zeros((HALO31, A_CH), F32)

        ext[HALO31:HALO31 + tm, :] = _glu(z_ref[...])
        acc = jnp.zeros((tm, A_CH), F32)
        for j in range(A_CONV):
            s = HALO31 - (A_CONV - 1) + j
            acc = acc + w_ref[j:j + 1, :] * ext[s:s + tm, :]
        ext[0:HALO31, :] = ext[tm:tm + HALO31, :]
        cv = acc + b_ref[...]
        c_ref[...] = cv
        mu = jnp.mean(cv, axis=-1, keepdims=True)
        xc = cv - mu
        ln = xc * lax.rsqrt(jnp.mean(xc * xc, axis=-1, keepdims=True) + LN_EPS) * g_ref[...] + lb_ref[...]
        a_ref[...] = (ln * jax.nn.sigmoid(ln)).astype(BF16)

    return pl.pallas_call(
        body, name="conf_fwd", grid=(t // tm,),
        in_specs=[_rows(tm, 2 * A_CH), _full((32, A_CH)), _full((1, A_CH)), _full((1, A_CH)), _full((1, A_CH))],
        out_specs=[_rows(tm, A_CH), _rows(tm, A_CH)],
        out_shape=[jax.ShapeDtypeStruct((t, A_CH), F32), jax.ShapeDtypeStruct((t, A_CH), BF16)],
        scratch_shapes=[pltpu.VMEM((HALO31 + tm, A_CH), F32)],
        compiler_params=_cp(("arbitrary",)),
    )(zag, conv_w, conv_b, ln_g, ln_b)


def _attn_mask(first_block):
    row = lax.broadcasted_iota(jnp.int32, (GROUP * BLOCK, 2 * BLOCK), 0) & (BLOCK - 1)
    col = lax.broadcasted_iota(jnp.int32, (GROUP * BLOCK, 2 * BLOCK), 1)
    diff = row + BLOCK - col
    return (diff >= 0) & (diff < BLOCK) & ((col >= BLOCK) | jnp.logical_not(first_block))


def _sink_rows(s_ref, h):
    grp = lax.broadcasted_iota(jnp.int32, (GROUP * BLOCK, 1), 0) >> 7
    out = jnp.full((GROUP * BLOCK, 1), s_ref[GROUP * h], F32)
    for g in range(1, GROUP):
        out = jnp.where(grp == g, s_ref[GROUP * h + g], out)
    return out


def _attn_probs(q4, k2, mask, sink):
    s = lax.dot_general(q4, k2, _CONTRACT_LAST, preferred_element_type=F32) * SCALE
    s = jnp.where(mask, s, NEG)
    m = jnp.maximum(jnp.max(s, axis=-1, keepdims=True), sink)
    p = jnp.exp(s - m)
    es = jnp.exp(sink - m)
    inv = 1.0 / (jnp.sum(p, axis=-1, keepdims=True) + es)
    return p * inv, es * inv


def _q_heads(q, h):
    return jnp.concatenate([q[:, HEAD_DIM * (GROUP * h + g):HEAD_DIM * (GROUP * h + g + 1)] for g in range(GROUP)],
                           axis=0)


def _kv_head(prev, cur, h):
    return jnp.concatenate([prev[:, HEAD_DIM * h:HEAD_DIM * (h + 1)], cur[:, HEAD_DIM * h:HEAD_DIM * (h + 1)]], axis=0)


def _attn_fwd(q, k, v, sinks):
    t = q.shape[0]
    nb = min(ATT_NB, t // BLOCK)
    rows = nb * BLOCK

    def body(s_ref, q_ref, kc_ref, kp_ref, vc_ref, vp_ref, o_ref):
        first = pl.program_id(0) == 0
        for b in range(nb):
            lo = BLOCK * b
            mask = _attn_mask(first) if b == 0 else _attn_mask(False)
            qv, kc, vc = q_ref[lo:lo + BLOCK, :], kc_ref[lo:lo + BLOCK, :], vc_ref[lo:lo + BLOCK, :]
            kp = kp_ref[...] if b == 0 else kc_ref[lo - BLOCK:lo, :]
            vp = vp_ref[...] if b == 0 else vc_ref[lo - BLOCK:lo, :]
            for h in range(N_KV_HEADS):
                pn, _ = _attn_probs(_q_heads(qv, h), _kv_head(kp, kc, h), mask, _sink_rows(s_ref, h))
                o4 = jnp.dot(pn.astype(BF16), _kv_head(vp, vc, h), preferred_element_type=F32).astype(BF16)
                for g in range(GROUP):
                    c0 = HEAD_DIM * (GROUP * h + g)
                    o_ref[lo:lo + BLOCK, c0:c0 + HEAD_DIM] = o4[BLOCK * g:BLOCK * (g + 1), :]

    cur = lambda n: pl.BlockSpec((rows, n), lambda i: (i, 0))
    prev = lambda n: pl.BlockSpec((BLOCK, n), lambda i: (jnp.maximum(i * nb - 1, 0), 0))
    return pl.pallas_call(
        body, name="attn_fwd", grid=(t // rows,),
        in_specs=[pl.BlockSpec(memory_space=pltpu.SMEM), cur(Q_DIM), cur(KV_DIM), prev(KV_DIM), cur(KV_DIM),
                  prev(KV_DIM)],
        out_specs=cur(Q_DIM),
        out_shape=jax.ShapeDtypeStruct((t, Q_DIM), BF16),
        compiler_params=_cp(("arbitrary",)),
    )(sinks, q, k, k, v, v)


def _out_post(lhs, ws, x_in, gpost):
    t = x_in.shape[0]
    tm = min(TM, t)
    n = len(lhs)

    def body(*refs):
        x_ref, g_ref, m_ref, xo_ref = refs[2 * n:]
        m = jnp.dot(refs[0][...], refs[n][...], preferred_element_type=F32)
        for j in range(1, n):
            m = m + jnp.dot(refs[j][...], refs[n + j][...], preferred_element_type=F32)
        m_ref[...] = m
        xo_ref[...] = x_ref[...] + m * _rsqrt_mean(m) * g_ref[...]

    return pl.pallas_call(
        body, name="out_post", grid=(t // tm,),
        in_specs=[_rows(tm, a.shape[1]) for a in lhs] + [_full(w.shape) for w in ws]
                 + [_rows(tm, D_MODEL), _full((1, D_MODEL))],
        out_specs=[_rows(tm, D_MODEL), _rows(tm, D_MODEL)],
        out_shape=[jax.ShapeDtypeStruct((t, D_MODEL), F32)] * 2,
        compiler_params=_cp(("arbitrary",)),
    )(*lhs, *ws, x_in, gpost)


def _conv3(w_ref, ext, tm):
    s = HALO3 - 2
    return (w_ref[0:1, :] * ext[s:s + tm, :] + w_ref[1:2, :] * ext[s + 1:s + 1 + tm, :]
            + w_ref[2:3, :] * ext[s + 2:s + 2 + tm, :])


def _ffn_fwd(x1, gpre, wup, layer, cw, wd, gpost):
    t = x1.shape[0]
    tm = min(TM, t)
    nc, n = wup.shape[1], wup.shape[4]

    def body(x_ref, gpre_ref, wup_ref, cw_ref, wd_ref, gpost_ref, h_ref, up_ref, u_ref, f_ref, xo_ref, h_s, acc, ext, hal):
        i, c = pl.program_id(0), pl.program_id(1)

        @pl.when(c == 0)
        def _():
            xv = x_ref[...]
            h = (xv * _rsqrt_mean(xv) * gpre_ref[...]).astype(BF16)
            h_s[...] = h
            h_ref[...] = h

        @pl.when(i == 0)
        def _():
            hal[c] = jnp.zeros((2, HALO3, n), F32)

        u = []
        for gv in range(2):
            up = jnp.dot(h_s[...], wup_ref[gv, 0, 0], preferred_element_type=F32)
            up_ref[gv, 0] = up.astype(BF16)
            ext[gv, 0:HALO3, :] = hal[c, gv]
            ext[gv, HALO3:HALO3 + tm, :] = up
            hal[c, gv] = ext[gv, tm:tm + HALO3, :]
            s = HALO3 - 2
            u.append(cw_ref[gv, 0, 0:1, :] * ext[gv, s:s + tm, :] + cw_ref[gv, 0, 1:2, :] * ext[gv, s + 1:s + 1 + tm, :]
                     + cw_ref[gv, 0, 2:3, :] * up)
            u_ref[gv, 0] = u[gv].astype(BF16)
        act = (u[0] * _sigmoid(u[0]) * u[1]).astype(BF16)
        part = jnp.dot(act, wd_ref[...], preferred_element_type=F32)

        @pl.when(c == 0)
        def _():
            acc[...] = part

        @pl.when(jnp.logical_and(c > 0, c < nc - 1))
        def _():
            acc[...] += part

        @pl.when(c == nc - 1)
        def _():
            f = acc[...] + part
            f_ref[...] = f
            xo_ref[...] = x_ref[...] + f * _rsqrt_mean(f) * gpost_ref[...]

    row = lambda w: pl.BlockSpec((tm, w), lambda i, c: (i, 0))
    one = _full((1, D_MODEL))
    return pl.pallas_call(
        body, name="ffn_fwd", grid=(t // tm, nc),
        in_specs=[row(D_MODEL), one, pl.BlockSpec((2, 1, 1, D_MODEL, n), lambda i, c: (0, c, layer, 0, 0)),
                  pl.BlockSpec((2, 1, 3, n), lambda i, c: (0, c, 0, 0)), pl.BlockSpec((n, D_MODEL), lambda i, c: (c, 0)),
                  one],
        out_specs=[row(D_MODEL), pl.BlockSpec((2, 1, tm, n), lambda i, c: (0, c, i, 0)),
                   pl.BlockSpec((2, 1, tm, n), lambda i, c: (0, c, i, 0)), row(D_MODEL), row(D_MODEL)],
        out_shape=[jax.ShapeDtypeStruct((t, D_MODEL), BF16), jax.ShapeDtypeStruct((2, nc, t, n), BF16),
                   jax.ShapeDtypeStruct((2, nc, t, n), BF16), jax.ShapeDtypeStruct((t, D_MODEL), F32),
                   jax.ShapeDtypeStruct((t, D_MODEL), F32)],
        scratch_shapes=[pltpu.VMEM((tm, D_MODEL), BF16), pltpu.VMEM((tm, D_MODEL), F32),
                        pltpu.VMEM((2, HALO3 + tm, n), F32), pltpu.VMEM((nc, 2, HALO3, n), F32)],
        compiler_params=_cp(("arbitrary", "arbitrary")),
    )(x1, gpre, wup, cw, wd, gpost)


def _od_fwd(x_in, gpre, w_in, cw, w_out, gpost):
    t = x_in.shape[0]
    tm = min(TM, t)
    ns, _, n = w_in.shape

    def body(x_ref, gpre_ref, w_ref, cw_ref, wo_ref, gpost_ref, h_ref, z_ref, cv_ref, y_ref, m_ref, xo_ref, z_s, ext):
        i = pl.program_id(0)
        xv = x_ref[...]
        h = (xv * _rsqrt_mean(xv) * gpre_ref[...]).astype(BF16)
        h_ref[...] = h
        for j in range(ns):
            z_s[:, n * j:n * (j + 1)] = jnp.dot(h, w_ref[j], preferred_element_type=F32)
        z_ref[...] = z_s[...].astype(BF16)

        @pl.when(i == 0)
        def _():
            ext[0:HALO3, :] = jnp.zeros((HALO3, SC_DIM), F32)

        ext[HALO3:HALO3 + tm, :] = z_s[:, SC_DIM:2 * SC_DIM] * z_s[:, 2 * SC_DIM:]
        cv = _conv3(cw_ref, ext, tm)
        cv_ref[...] = cv.astype(BF16)
        y = (z_s[:, :SC_DIM] * cv).astype(BF16)
        ext[0:HALO3, :] = ext[tm:tm + HALO3, :]
        y_ref[...] = y
        m = jnp.dot(y, wo_ref[...], preferred_element_type=F32)
        m_ref[...] = m
        xo_ref[...] = xv + m * _rsqrt_mean(m) * gpost_ref[...]

    return pl.pallas_call(
        body, name="od_fwd", grid=(t // tm,),
        in_specs=[_rows(tm, D_MODEL), _full((1, D_MODEL)), _full((ns, D_MODEL, n)), _full((3, SC_DIM)),
                  _full((SC_DIM, D_MODEL)), _full((1, D_MODEL))],
        out_specs=[_rows(tm, D_MODEL), _rows(tm, 3 * SC_DIM), _rows(tm, SC_DIM), _rows(tm, SC_DIM), _rows(tm, D_MODEL),
                   _rows(tm, D_MODEL)],
        out_shape=[jax.ShapeDtypeStruct((t, D_MODEL), BF16), jax.ShapeDtypeStruct((t, 3 * SC_DIM), BF16),
                   jax.ShapeDtypeStruct((t, SC_DIM), BF16), jax.ShapeDtypeStruct((t, SC_DIM), BF16),
                   jax.ShapeDtypeStruct((t, D_MODEL), F32), jax.ShapeDtypeStruct((t, D_MODEL), F32)],
        scratch_shapes=[pltpu.VMEM((tm, 3 * SC_DIM), F32), pltpu.VMEM((HALO3 + tm, SC_DIM), F32)],
        compiler_params=_cp(("arbitrary",)),
    )(x_in, gpre, w_in, cw, w_out, gpost)


def _loss_grad(y, target):
    t = y.shape[0]
    tm = min(TM, t)

    def body(y_ref, t_ref, dy_ref, l_ref):
        e = y_ref[...] - t_ref[...]
        dy_ref[...] = e * (1.0 / D_MODEL)
        part = jnp.zeros((1, 128), F32) + jnp.sum(e * e) * (0.5 / D_MODEL)
        _acc_out(l_ref, pl.program_id(0) == 0, part)

    return pl.pallas_call(
        body, name="loss_grad", grid=(t // tm,),
        in_specs=[_rows(tm, D_MODEL), _rows(tm, D_MODEL)],
        out_specs=[_rows(tm, D_MODEL), _full((1, 128))],
        out_shape=[jax.ShapeDtypeStruct((t, D_MODEL), F32), jax.ShapeDtypeStruct((1, 128), F32)],
        compiler_params=_cp(("arbitrary",)),
    )(y, target)


def _dw2d(a, b, bm, bn):
    t, m = a.shape
    n = b.shape[1]
    tk = min(TK_DW, t)

    def body(a_ref, b_ref, o_ref):
        part = lax.dot_general(a_ref[...], b_ref[...], _CONTRACT_FIRST, preferred_element_type=F32)
        _acc_out(o_ref, pl.program_id(2) == 0, part)

    return pl.pallas_call(
        body, name="dw2d", grid=(m // bm, n // bn, t // tk),
        in_specs=[pl.BlockSpec((tk, bm), lambda i, j, k: (k, i)), pl.BlockSpec((tk, bn), lambda i, j, k: (k, j))],
        out_specs=pl.BlockSpec((bm, bn), lambda i, j, k: (i, j)),
        out_shape=jax.ShapeDtypeStruct((m, n), F32),
        compiler_params=_cp(("arbitrary", "arbitrary", "arbitrary")),
    )(a, b)


def _dw_cols(a, b, n_blk):
    t, m = a.shape
    s = b.shape[1] // n_blk
    tk = min(TK_DW, t)

    def body(a_ref, b_ref, o_ref):
        part = lax.dot_general(a_ref[...], b_ref[...], _CONTRACT_FIRST, preferred_element_type=F32)
        _acc_out(o_ref.at[0], pl.program_id(1) == 0, part)

    return pl.pallas_call(
        body, name="dw_cols", grid=(s, t // tk),
        in_specs=[pl.BlockSpec((tk, m), lambda j, k: (k, 0)), pl.BlockSpec((tk, n_blk), lambda j, k: (k, j))],
        out_specs=pl.BlockSpec((1, m, n_blk), lambda j, k: (j, 0, 0)),
        out_shape=jax.ShapeDtypeStruct((s, m, n_blk), F32),
        compiler_params=_cp(("arbitrary", "arbitrary")),
    )(a, b)


def _dw_up(h, dup, layer, buf):
    t, m = h.shape
    s, _, n = dup.shape
    tk = min(TK_DW, t)

    def body(*refs):
        a_ref, b_ref, o_ref = refs[0], refs[1], refs[-1]
        part = lax.dot_general(a_ref[...], b_ref[0], _CONTRACT_FIRST, preferred_element_type=F32)
        _acc_out(o_ref.at[0, 0], pl.program_id(1) == 0, part)

    ins = [h, dup] + ([] if buf is None else [buf])
    return pl.pallas_call(
        body, name="dw_up", grid=(s, t // tk),
        in_specs=[pl.BlockSpec((tk, m), lambda j, k: (k, 0)), pl.BlockSpec((1, tk, n), lambda j, k: (j, k, 0))]
                 + ([] if buf is None else [_ANY]),
        out_specs=pl.BlockSpec((1, 1, m, n), lambda j, k: (j, layer, 0, 0)),
        out_shape=jax.ShapeDtypeStruct((s, 2, m, n), F32),
        input_output_aliases={} if buf is None else {2: 0},
        compiler_params=_cp(("arbitrary", "arbitrary")),
    )(*ins)


def _dw_down(act, df, layer, buf):
    nc, t, n = act.shape
    d = df.shape[1]
    tk = min(TK_DW, t)

    def body(*refs):
        a_ref, b_ref, o_ref = refs[0], refs[1], refs[-1]
        part = lax.dot_general(a_ref[0], b_ref[...], _CONTRACT_FIRST, preferred_element_type=F32)
        part = part.reshape(2, n // 2, d)
        first = pl.program_id(1) == 0

        @pl.when(first)
        def _():
            o_ref[:, 0] = part

        @pl.when(jnp.logical_not(first))
        def _():
            o_ref[:, 0] += part

    ins = [act, df] + ([] if buf is None else [buf])
    return pl.pallas_call(
        body, name="dw_down", grid=(nc, t // tk),
        in_specs=[pl.BlockSpec((1, tk, n), lambda c, k: (c, k, 0)), pl.BlockSpec((tk, d), lambda c, k: (k, 0))]
                 + ([] if buf is None else [_ANY]),
        out_specs=pl.BlockSpec((2, 1, n // 2, d), lambda c, k: (c, layer, 0, 0)),
        out_shape=jax.ShapeDtypeStruct((2 * nc, 2, n // 2, d), F32),
        input_output_aliases={} if buf is None else {2: 0},
        compiler_params=_cp(("arbitrary", "arbitrary")),
    )(*ins)


def _dz_wt_rms_bwd(dz, wt, x_in, gpre, dres):
    t, n = dz.shape
    tm = min(TM, t)

    def body(dz_ref, wt_ref, x_ref, g_ref, dres_ref, dx_ref, dg_ref):
        dh = jnp.dot(dz_ref[...], wt_ref[...], preferred_element_type=F32)
        dx, dg = _rms_bwd(x_ref[...], g_ref[...], dh)
        dx_ref[...] = dres_ref[...] + dx
        _acc_out(dg_ref, pl.program_id(0) == 0, dg)

    return pl.pallas_call(
        body, name="dz_wt_rms_bwd", grid=(t // tm,),
        in_specs=[_rows(tm, n), _full((n, D_MODEL)), _rows(tm, D_MODEL), _full((1, D_MODEL)), _rows(tm, D_MODEL)],
        out_specs=[_rows(tm, D_MODEL), _full((1, D_MODEL))],
        out_shape=[jax.ShapeDtypeStruct((t, D_MODEL), F32), jax.ShapeDtypeStruct((1, D_MODEL), F32)],
        compiler_params=_cp(("arbitrary",)),
    )(dz, wt, x_in, gpre, dres)


def _shift_matrices(shift, shift_h, tm, hb):
    row = lax.broadcasted_iota(jnp.int32, (2 * tm, tm), 0)
    col = lax.broadcasted_iota(jnp.int32, (2 * tm, tm), 1)
    hit = ((row < tm) & (col == row + 1)) | ((row >= tm) & (col == row - tm + 2))
    shift[...] = jnp.where(hit, 1.0, 0.0).astype(BF16)
    row = lax.broadcasted_iota(jnp.int32, (hb, hb), 0)
    col = lax.broadcasted_iota(jnp.int32, (hb, hb), 1)
    hit = ((row < HALO3) & (col == row - (HALO3 - 1))) | ((row >= HALO3) & (col == row - (2 * HALO3 - 2)))
    shift_h[...] = jnp.where(hit, 1.0, 0.0).astype(BF16)


def _next_rows(shift, shift_h, xb, nxt, d12_s, tm):
    d12_s[...] = jnp.dot(shift[...], xb, preferred_element_type=F32)
    edge = jnp.dot(shift_h[...], nxt, preferred_element_type=F32)
    d12_s[tm - HALO3:tm, :] += edge[0:HALO3, :]
    d12_s[2 * tm - HALO3:2 * tm, :] += edge[HALO3:2 * HALO3, :]


def _ffn_bwd(f, dxo, gpost, x_in, gpre, up, u, cw, wdt, wupt):
    t = f.shape[0]
    tm = min(TM_BWD, t)
    nt = t // tm
    nc, n = up.shape[1], up.shape[3]
    hb = 2 * HALO3

    def body(f_ref, dxo_ref, gpost_ref, x_ref, gpre_ref, up_ref, u_ref, cw_ref, wdt_ref, wupt_ref,
             df_ref, act_ref, dup_ref, dx_ref, dgpost_ref, dgpre_ref, dcw_ref, df_s, acc, du_s, dub_s, d12_s, hal, shift, shift_h):
        i, c = pl.program_id(0), pl.program_id(1)

        @pl.when(c == 0)
        def _():
            df, dg = _rms_bwd(f_ref[...], gpost_ref[...], dxo_ref[...])
            df_s[...] = df.astype(BF16)
            df_ref[...] = df.astype(BF16)
            _acc_out(dgpost_ref, i == 0, dg)

        @pl.when(i == 0)
        def _():
            hal[c] = jnp.zeros((2, hb, n), BF16)
            dcw_ref[0, c] = jnp.zeros((8, n), F32)
            dcw_ref[1, c] = jnp.zeros((8, n), F32)

        @pl.when(jnp.logical_and(i == 0, c == 0))
        def _():
            _shift_matrices(shift, shift_h, tm, hb)

        dact = jnp.dot(df_s[...], wdt_ref[0], preferred_element_type=F32)
        g, v = u_ref[0, 0].astype(F32), u_ref[1, 0].astype(F32)
        sg = _sigmoid(g)
        sil = g * sg
        act_ref[0] = (sil * v).astype(BF16)
        dug = dact * v * (sg + sil * (1.0 - sg))
        duv = dact * sil
        du_s[0], du_s[1] = dug, duv
        dub_s[0], dub_s[1] = dug.astype(BF16), duv.astype(BF16)
        dh = None
        for gv in range(2):
            _next_rows(shift, shift_h, dub_s[gv], hal[c, gv], d12_s, tm)
            hal[c, gv] = dub_s[gv, 0:hb, :]
            du, d1, d2 = du_s[gv], d12_s[0:tm, :], d12_s[tm:2 * tm, :]
            dup = (cw_ref[gv, 0, 2:3, :] * du + cw_ref[gv, 0, 1:2, :] * d1 + cw_ref[gv, 0, 0:1, :] * d2).astype(BF16)
            dup_ref[gv, 0] = dup
            upc = up_ref[gv, 0].astype(F32)
            dcw_ref[gv, c, 2:3, :] += jnp.sum(upc * du, axis=0, keepdims=True)
            dcw_ref[gv, c, 1:2, :] += jnp.sum(upc * d1, axis=0, keepdims=True)
            dcw_ref[gv, c, 0:1, :] += jnp.sum(upc * d2, axis=0, keepdims=True)
            part = jnp.dot(dup, wupt_ref[gv, 0], preferred_element_type=F32)
            dh = part if dh is None else dh + part
        _acc_out(acc, c == 0, dh)

        @pl.when(c == nc - 1)
        def _():
            dx, dg = _rms_bwd(x_ref[...], gpre_ref[...], acc[...])
            dx_ref[...] = dxo_ref[...] + dx
            _acc_out(dgpre_ref, i == 0, dg)

    rrow = lambda w: pl.BlockSpec((tm, w), lambda i, c: (nt - 1 - i, 0))
    blk = pl.BlockSpec((2, 1, tm, n), lambda i, c: (0, c, nt - 1 - i, 0))
    one = _full((1, D_MODEL))
    return pl.pallas_call(
        body, name="ffn_bwd", grid=(nt, nc),
        in_specs=[rrow(D_MODEL), rrow(D_MODEL), one, rrow(D_MODEL), one, blk, blk,
                  pl.BlockSpec((2, 1, 3, n), lambda i, c: (0, c, 0, 0)),
                  pl.BlockSpec((1, D_MODEL, n), lambda i, c: (c, 0, 0)),
                  pl.BlockSpec((2, 1, n, D_MODEL), lambda i, c: (0, c, 0, 0))],
        out_specs=[rrow(D_MODEL), pl.BlockSpec((1, tm, n), lambda i, c: (c, nt - 1 - i, 0)), blk, rrow(D_MODEL),
                   one, one, _full((2, nc, 8, n))],
        out_shape=[jax.ShapeDtypeStruct((t, D_MODEL), BF16), jax.ShapeDtypeStruct((nc, t, n), BF16),
                   jax.ShapeDtypeStruct((2, nc, t, n), BF16), jax.ShapeDtypeStruct((t, D_MODEL), F32),
                   jax.ShapeDtypeStruct((1, D_MODEL), F32), jax.ShapeDtypeStruct((1, D_MODEL), F32),
                   jax.ShapeDtypeStruct((2, nc, 8, n), F32)],
        scratch_shapes=[pltpu.VMEM((tm, D_MODEL), BF16), pltpu.VMEM((tm, D_MODEL), F32),
                        pltpu.VMEM((2, tm, n), F32), pltpu.VMEM((2, tm, n), BF16), pltpu.VMEM((2 * tm, n), F32),
                        pltpu.VMEM((nc, 2, hb, n), BF16), pltpu.VMEM((2 * tm, tm), BF16), pltpu.VMEM((hb, hb), BF16)],
        compiler_params=_cp(("arbitrary", "arbitrary")),
    )(f, dxo, gpost, x_in, gpre, up, u, cw, wdt, wupt)


def _od_bwd(m, dxo, gpost, x_in, gpre, z, cv, cw, wot, wint):
    t = m.shape[0]
    tm = min(TM_BWD, t)
    nt = t // tm
    hb = 2 * HALO3

    def body(m_ref, dxo_ref, gpost_ref, x_ref, gpre_ref, z_ref, cv_ref, cw_ref, wot_ref, wint_ref,
             dm_ref, dz_ref, dx_ref, dgpost_ref, dgpre_ref, dcw_ref, dcvb_s, d12_s, dz_s, hal, shift, shift_h):
        i = pl.program_id(0)
        dxo = dxo_ref[...]
        dm, dg = _rms_bwd(m_ref[...], gpost_ref[...], dxo)
        dmb = dm.astype(BF16)
        dm_ref[...] = dmb
        _acc_out(dgpost_ref, i == 0, dg)

        @pl.when(i == 0)
        def _():
            hal[...] = jnp.zeros((hb, SC_DIM), BF16)
            dcw_ref[...] = jnp.zeros((8, SC_DIM), F32)
            _shift_matrices(shift, shift_h, tm, hb)

        dy = jnp.dot(dmb, wot_ref[...], preferred_element_type=F32)
        z = z_ref[...].astype(F32)
        b, cg, u = z[:, :SC_DIM], z[:, SC_DIM:2 * SC_DIM], z[:, 2 * SC_DIM:]
        dz_s[:, 0:SC_DIM] = (dy * cv_ref[...].astype(F32)).astype(BF16)
        dcv = dy * b
        dcvb_s[...] = dcv.astype(BF16)
        _next_rows(shift, shift_h, dcvb_s[...], hal[...], d12_s, tm)
        hal[...] = dcvb_s[0:hb, :]
        d1, d2 = d12_s[0:tm, :], d12_s[tm:2 * tm, :]
        dcu = cw_ref[2:3, :] * dcv + cw_ref[1:2, :] * d1 + cw_ref[0:1, :] * d2
        cu = cg * u
        dcw_ref[2:3, :] += jnp.sum(cu * dcv, axis=0, keepdims=True)
        dcw_ref[1:2, :] += jnp.sum(cu * d1, axis=0, keepdims=True)
        dcw_ref[0:1, :] += jnp.sum(cu * d2, axis=0, keepdims=True)
        dz_s[:, SC_DIM:2 * SC_DIM] = (dcu * u).astype(BF16)
        dz_s[:, 2 * SC_DIM:3 * SC_DIM] = (dcu * cg).astype(BF16)
        dz_ref[...] = dz_s[...]
        dh = jnp.dot(dz_s[...], wint_ref[...], preferred_element_type=F32)
        dx, dg2 = _rms_bwd(x_ref[...], gpre_ref[...], dh)
        dx_ref[...] = dxo + dx
        _acc_out(dgpre_ref, i == 0, dg2)

    rrow = lambda w: pl.BlockSpec((tm, w), lambda i: (nt - 1 - i, 0))
    one = _full((1, D_MODEL))
    return pl.pallas_call(
        body, name="od_bwd", grid=(nt,),
        in_specs=[rrow(D_MODEL), rrow(D_MODEL), one, rrow(D_MODEL), one, rrow(3 * SC_DIM), rrow(SC_DIM),
                  _full((3, SC_DIM)), _full((D_MODEL, SC_DIM)), _full((3 * SC_DIM, D_MODEL))],
        out_specs=[rrow(D_MODEL), rrow(3 * SC_DIM), rrow(D_MODEL), one, one, _full((8, SC_DIM))],
        out_shape=[jax.ShapeDtypeStruct((t, D_MODEL), BF16), jax.ShapeDtypeStruct((t, 3 * SC_DIM), BF16),
                   jax.ShapeDtypeStruct((t, D_MODEL), F32), jax.ShapeDtypeStruct((1, D_MODEL), F32),
                   jax.ShapeDtypeStruct((1, D_MODEL), F32), jax.ShapeDtypeStruct((8, SC_DIM), F32)],
        scratch_shapes=[pltpu.VMEM((tm, SC_DIM), BF16), pltpu.VMEM((2 * tm, SC_DIM), F32),
                        pltpu.VMEM((tm, 3 * SC_DIM), BF16), pltpu.VMEM((hb, SC_DIM), BF16),
                        pltpu.VMEM((2 * tm, tm), BF16), pltpu.VMEM((hb, hb), BF16)],
        compiler_params=_cp(("arbitrary",)),
    )(m, dxo, gpost, x_in, gpre, z, cv, cw, wot, wint)


def _ev_bwd1(m, dxo, gpost, wot):
    t = m.shape[0]
    tm = min(TM, t)

    def body(m_ref, dxo_ref, g_ref, wot_ref, dm_ref, da_ref, do_ref, dg_ref):
        dm, dg = _rms_bwd(m_ref[...], g_ref[...], dxo_ref[...])
        dmb = dm.astype(BF16)
        dm_ref[...] = dmb
        _acc_out(dg_ref, pl.program_id(0) == 0, dg)
        dao = jnp.dot(dmb, wot_ref[...], preferred_element_type=F32)
        da_ref[...] = dao[:, :A_CH]
        do_ref[...] = dao[:, A_CH:].astype(BF16)

    return pl.pallas_call(
        body, name="ev_bwd1", grid=(t // tm,),
        in_specs=[_rows(tm, D_MODEL), _rows(tm, D_MODEL), _full((1, D_MODEL)), _full((D_MODEL, A_CH + Q_DIM))],
        out_specs=[_rows(tm, D_MODEL), _rows(tm, A_CH), _rows(tm, Q_DIM), _full((1, D_MODEL))],
        out_shape=[jax.ShapeDtypeStruct((t, D_MODEL), BF16), jax.ShapeDtypeStruct((t, A_CH), F32),
                   jax.ShapeDtypeStruct((t, Q_DIM), BF16), jax.ShapeDtypeStruct((1, D_MODEL), F32)],
        compiler_params=_cp(("arbitrary",)),
    )(m, dxo, gpost, wot)


def _conf_bwd(da, cv, zag, conv_w, ln_g, ln_b):
    t = da.shape[0]
    tm = min(TM_BWD, t)
    nt = t // tm

    def body(da_ref, c_ref, z_ref, zh_ref, w_ref, g_ref, lb_ref, dz_ref, dw_ref, dv_ref, ext_in, ext_out):
        i = pl.program_id(0)
        r = nt - 1 - i

        @pl.when(i == 0)
        def _():
            ext_out[tm:tm + HALO31, :] = jnp.zeros((HALO31, A_CH), F32)
            dw_ref[...] = jnp.zeros((32, A_CH), F32)
            dv_ref[...] = jnp.zeros((8, A_CH), F32)

        x = c_ref[...]
        mu = jnp.mean(x, axis=-1, keepdims=True)
        xc = x - mu
        rstd = lax.rsqrt(jnp.mean(xc * xc, axis=-1, keepdims=True) + LN_EPS)
        xh = xc * rstd
        ln = xh * g_ref[...] + lb_ref[...]
        sl = jax.nn.sigmoid(ln)
        dln = da_ref[...] * (sl * (1.0 + ln * (1.0 - sl)))
        dxh = dln * g_ref[...]
        dc = rstd * (dxh - jnp.mean(dxh, axis=-1, keepdims=True) - xh * jnp.mean(dxh * xh, axis=-1, keepdims=True))
        dv_ref[0:1, :] += jnp.sum(dc, axis=0, keepdims=True)
        dv_ref[1:2, :] += jnp.sum(dln * xh, axis=0, keepdims=True)
        dv_ref[2:3, :] += jnp.sum(dln, axis=0, keepdims=True)

        ext_out[0:tm, :] = dc
        dglu = jnp.zeros((tm, A_CH), F32)
        for j in range(A_CONV):
            s = A_CONV - 1 - j
            dglu = dglu + w_ref[j:j + 1, :] * ext_out[s:s + tm, :]
        ext_out[tm:tm + HALO31, :] = ext_out[0:HALO31, :]

        ext_in[0:HALO31, :] = jnp.where(r > 0, _glu(zh_ref[...]), 0.0)
        z = z_ref[...].astype(F32)
        al, sg = z[:, :A_CH], jax.nn.sigmoid(z[:, A_CH:])
        ext_in[HALO31:HALO31 + tm, :] = al * sg
        for j in range(A_CONV):
            s = HALO31 - (A_CONV - 1) + j
            dw_ref[j:j + 1, :] += jnp.sum(dc * ext_in[s:s + tm, :], axis=0, keepdims=True)
        dz_ref[:, 0:A_CH] = (dglu * sg).astype(BF16)
        dz_ref[:, A_CH:2 * A_CH] = (dglu * al * sg * (1.0 - sg)).astype(BF16)

    rrow = lambda w: pl.BlockSpec((tm, w), lambda i: (nt - 1 - i, 0))
    halo = pl.BlockSpec((HALO31, 2 * A_CH), lambda i: (jnp.maximum((nt - 1 - i) * (tm // HALO31) - 1, 0), 0))
    return pl.pallas_call(
        body, name="conf_bwd", grid=(nt,),
        in_specs=[rrow(A_CH), rrow(A_CH), rrow(2 * A_CH), halo, _full((32, A_CH)), _full((1, A_CH)),
                  _full((1, A_CH))],
        out_specs=[rrow(2 * A_CH), _full((32, A_CH)), _full((8, A_CH))],
        out_shape=[jax.ShapeDtypeStruct((t, 2 * A_CH), BF16), jax.ShapeDtypeStruct((32, A_CH), F32),
                   jax.ShapeDtypeStruct((8, A_CH), F32)],
        scratch_shapes=[pltpu.VMEM((HALO31 + tm, A_CH), F32), pltpu.VMEM((tm + HALO31, A_CH), F32)],
        compiler_params=_cp(("arbitrary",)),
    )(da, cv, zag, zag, conv_w, ln_g, ln_b)


def _attn_bwd(q, k, v, do, sinks):
    t = q.shape[0]
    nb = min(ATT_NB, t // BLOCK)
    rows = nb * BLOCK
    ns = t // rows

    def body(s_ref, q_ref, kc_ref, kp_ref, vc_ref, vp_ref, do_ref, dq_ref, dk_ref, dv_ref, ds_ref, dkc, dvc):
        i = pl.program_id(0)
        r = ns - 1 - i

        @pl.when(i == 0)
        def _():
            dkc[...] = jnp.zeros_like(dkc)
            dvc[...] = jnp.zeros_like(dvc)
            ds_ref[...] = jnp.zeros_like(ds_ref)

        lane = lax.broadcasted_iota(jnp.int32, (1, N_Q_HEADS), 1)
        dsv = jnp.zeros((1, N_Q_HEADS), F32)
        for b in range(nb - 1, -1, -1):
            lo = BLOCK * b
            mask = _attn_mask(r == 0) if b == 0 else _attn_mask(False)
            qv, dov = q_ref[lo:lo + BLOCK, :], do_ref[lo:lo + BLOCK, :]
            kc, vc = kc_ref[lo:lo + BLOCK, :], vc_ref[lo:lo + BLOCK, :]
            kp = kp_ref[...] if b == 0 else kc_ref[lo - BLOCK:lo, :]
            vp = vp_ref[...] if b == 0 else vc_ref[lo - BLOCK:lo, :]
            for h in range(N_KV_HEADS):
                q4, do4 = _q_heads(qv, h), _q_heads(dov, h)
                k2, v2 = _kv_head(kp, kc, h), _kv_head(vp, vc, h)
                pn, ps = _attn_probs(q4, k2, mask, _sink_rows(s_ref, h))
                dp = lax.dot_general(do4, v2, _CONTRACT_LAST, preferred_element_type=F32)
                dl = jnp.sum(pn * dp, axis=-1, keepdims=True)
                dsb = (pn * (dp - dl)).astype(BF16)
                dq4 = (jnp.dot(dsb, k2, preferred_element_type=F32) * SCALE).astype(BF16)
                for g in range(GROUP):
                    c0 = HEAD_DIM * (GROUP * h + g)
                    dq_ref[lo:lo + BLOCK, c0:c0 + HEAD_DIM] = dq4[BLOCK * g:BLOCK * (g + 1), :]
                dk2 = lax.dot_general(dsb, q4, _CONTRACT_FIRST, preferred_element_type=F32) * SCALE
                dv2 = lax.dot_general(pn.astype(BF16), do4, _CONTRACT_FIRST, preferred_element_type=F32)
                dk_ref[lo:lo + BLOCK, HEAD_DIM * h:HEAD_DIM * (h + 1)] = dk2[BLOCK:, :] + dkc[h]
                dv_ref[lo:lo + BLOCK, HEAD_DIM * h:HEAD_DIM * (h + 1)] = dv2[BLOCK:, :] + dvc[h]
                dkc[h] = dk2[:BLOCK, :]
                dvc[h] = dv2[:BLOCK, :]
                srow = -ps * dl
                for g in range(GROUP):
                    dsv = dsv + jnp.where(lane == GROUP * h + g, jnp.sum(srow[BLOCK * g:BLOCK * (g + 1), :]), 0.0)
        ds_ref[...] += dsv

    cur = lambda n: pl.BlockSpec((rows, n), lambda i: (ns - 1 - i, 0))
    prev = lambda n: pl.BlockSpec((BLOCK, n), lambda i: (jnp.maximum((ns - 1 - i) * nb - 1, 0), 0))
    return pl.pallas_call(
        body, name="attn_bwd", grid=(ns,),
        in_specs=[pl.BlockSpec(memory_space=pltpu.SMEM), cur(Q_DIM), cur(KV_DIM), prev(KV_DIM), cur(KV_DIM),
                  prev(KV_DIM), cur(Q_DIM)],
        out_specs=[cur(Q_DIM), cur(KV_DIM), cur(KV_DIM), _full((1, N_Q_HEADS))],
        out_shape=[jax.ShapeDtypeStruct((t, Q_DIM), BF16), jax.ShapeDtypeStruct((t, KV_DIM), F32),
                   jax.ShapeDtypeStruct((t, KV_DIM), F32), jax.ShapeDtypeStruct((1, N_Q_HEADS), F32)],
        scratch_shapes=[pltpu.VMEM((N_KV_HEADS, BLOCK, HEAD_DIM), F32), pltpu.VMEM((N_KV_HEADS, BLOCK, HEAD_DIM), F32)],
        compiler_params=_cp(("arbitrary",)),
    )(sinks, q, k, k, v, v, do)


def _ev_dz(dzag, dq, dk, dv, rc, rsa, rsb):
    t = dzag.shape[0]
    tm = min(TM, t)

    def body(dzag_ref, dq_ref, dk_ref, dv_ref, c_ref, sa_ref, sb_ref, dz_ref):
        c, sa, sb = c_ref[...], sa_ref[...], sb_ref[...]
        dz_ref[:, 0:2 * A_CH] = dzag_ref[...]
        q0 = 2 * A_CH
        for j in range(Q_DIM // 128):
            d = dq_ref[:, 128 * j:128 * (j + 1)].astype(F32)
            dz_ref[:, q0 + 128 * j:q0 + 128 * (j + 1)] = _rope_bwd(d, c, sa, sb).astype(BF16)
        k0 = q0 + Q_DIM
        dz_ref[:, k0:k0 + KV_DIM] = _rope_bwd(dk_ref[...], c, sa, sb).astype(BF16)
        dz_ref[:, k0 + KV_DIM:k0 + 2 * KV_DIM] = dv_ref[...].astype(BF16)

    return pl.pallas_call(
        body, name="ev_dz", grid=(t // tm,),
        in_specs=[_rows(tm, 2 * A_CH), _rows(tm, Q_DIM), _rows(tm, KV_DIM), _rows(tm, KV_DIM),
                  _rows(tm, 128), _rows(tm, 128), _rows(tm, 128)],
        out_specs=_rows(tm, EVEN_IN),
        out_shape=jax.ShapeDtypeStruct((t, EVEN_IN), BF16),
        compiler_params=_cp(("arbitrary",)),
    )(dzag, dq, dk, dv, rc, rsa, rsb)


def _prep_ev(gat):
    p = {}
    w = gat["ev_w_in"][:, 0].transpose(1, 0, 2).reshape(D_MODEL, EVEN_IN)
    p["ev_w_in"], p["ev_w_in_t"] = w, w.T
    w = gat["ev_w_out"].reshape(A_CH + Q_DIM, D_MODEL)
    p["ev_w_out"], p["ev_w_out_t"] = w, w.T
    return p


def _prep_rest(gat):
    p = {}
    g = gat["od_w_in"][:, 0]
    p["od_w_in"], p["od_w_in_t"] = g, g.transpose(0, 2, 1).reshape(3 * SC_DIM, D_MODEL)
    w = gat["od_w_out"].reshape(SC_DIM, D_MODEL)
    p["od_w_out"], p["od_w_out_t"] = w, w.T
    g = gat["ffn_w_up"]
    p["ffn_w_up"] = g.reshape(2, N_DEV // 2, 2, D_MODEL, FF_N)
    p["ffn_w_up_t"] = [g[:, i].transpose(0, 2, 1).reshape(2, N_DEV // 2, FF_N, D_MODEL) for i in range(2)]
    g = gat["ffn_w_down"]
    p["ffn_w_down"] = [g[:, i].reshape(D_FF, D_MODEL) for i in range(2)]
    p["ffn_w_down_t"] = [w.reshape(N_DEV // 2, FF_N, D_MODEL).transpose(0, 2, 1) for w in p["ffn_w_down"]]
    return p


def _local_step(x, positions, target, p, rest_weights, s, token, grads_ready):
    row = lambda a, tok=None: a.reshape(1, -1) if tok is None else a.reshape(1, -1) + tok
    nc = N_DEV // 2
    rc, rsa, rsb = _rope_tables(positions)
    conv31 = jnp.pad(s["ev_a_conv_w"][0], ((0, 1), (0, 0)))
    cw_ffn = [s["ffn_conv_w"][i].reshape(3, 2, nc, FF_N).transpose(1, 2, 0, 3) for i in range(2)]
    sinks = s["ev_sinks"][0]
    big, g = {}, {}

    h0, zag, q, k, v = _ev_in(x, row(s["mix_norm_pre"][0], token), p["ev_w_in"], rc, rsa, rsb)
    cv, a = _conf_fwd(zag, conv31, s["ev_a_conv_b"], s["ev_a_ln_g"], s["ev_a_ln_b"])
    o = _attn_fwd(q, k, v, sinks)
    wo = p["ev_w_out"]
    m0, x1 = _out_post([a, o], [wo[:A_CH], wo[A_CH:]], x, row(s["mix_norm_post"][0]))
    p = {**p, **rest_weights(m0)}
    h1, up0, u0, f0, x2 = _ffn_fwd(x1, row(s["ffn_norm_pre"][0]), p["ffn_w_up"], 0, cw_ffn[0], p["ffn_w_down"][0],
                                   row(s["ffn_norm_post"][0]))
    h2, z, cv1, y, m1, x3 = _od_fwd(x2, row(s["mix_norm_pre"][1]), p["od_w_in"], s["od_conv_w"][0], p["od_w_out"],
                                    row(s["mix_norm_post"][1]))
    h3, up1, u1, f1, x4 = _ffn_fwd(x3, row(s["ffn_norm_pre"][1]), p["ffn_w_up"], 1, cw_ffn[1], p["ffn_w_down"][1],
                                   row(s["ffn_norm_post"][1]))
    dx, lpart = _loss_grad(x4, target)

    def ffn_back(i, f, dxo, up, u, h, x_in, bufs, tok=None):
        df, act, dup, dx_in, dgpost, dgpre, dcw = _ffn_bwd(
            f, dxo, row(s["ffn_norm_post"][i], tok), x_in, row(s["ffn_norm_pre"][i]), up, u, cw_ffn[i],
            p["ffn_w_down_t"][i], p["ffn_w_up_t"][i])
        bufs = (_dw_up(h, dup.reshape(N_DEV, -1, FF_N), i, bufs[0]), _dw_down(act, df, i, bufs[1]))
        return dx_in, dgpost, dgpre, dcw[:, :, 0:3].transpose(2, 0, 1, 3).reshape(3, 2 * D_FF), bufs

    dx, dgfpost1, dgfpre1, dcw1, bufs = ffn_back(1, f1, dx, up1, u1, h3, x3, (None, None))

    dm1, dz, dx, dgpost1, dgpre1, dcw_od = _od_bwd(m1, dx, row(s["mix_norm_post"][1]), x2, row(s["mix_norm_pre"][1]), z,
                                                   cv1, s["od_conv_w"][0], p["od_w_out_t"], p["od_w_in_t"])
    big["od_w_out"] = _dw2d(y, dm1, SC_DIM, D_MODEL).reshape(N_DEV, -1, D_MODEL)
    big["od_w_in"] = _dw_cols(h2, dz, 3 * SC_DIM // N_DEV)
    g["od_conv_w"] = dcw_od[None, 0:3]
    tok = grads_ready(["od_w_in", "od_w_out"], big)

    dx, dgfpost0, dgfpre0, dcw0, bufs = ffn_back(0, f0, dx, up0, u0, h1, x1, bufs, tok)
    big["ffn_w_up"], big["ffn_w_down"] = bufs
    tok = grads_ready(["ffn_w_up", "ffn_w_down"], big)

    dm0, da, do, dgpost0 = _ev_bwd1(m0, dx, row(s["mix_norm_post"][0], tok), p["ev_w_out_t"])
    big["ev_w_out"] = jnp.concatenate([_dw2d(a, dm0, A_CH, D_MODEL), _dw2d(o, dm0, Q_DIM, D_MODEL)],
                                      axis=0).reshape(N_DEV, -1, D_MODEL)
    dzag, dcw31, dvec = _conf_bwd(da, cv, zag, conv31, s["ev_a_ln_g"], s["ev_a_ln_b"])
    dq, dk, dv, dsinks = _attn_bwd(q, k, v, do, sinks)
    dz0 = _ev_dz(dzag, dq, dk, dv, rc, rsa, rsb)
    dw_in = _dw2d(h0, dz0, D_MODEL, EVEN_IN // 2)
    big["ev_w_in"] = dw_in.reshape(D_MODEL, N_DEV, EVEN_IN // N_DEV).transpose(1, 0, 2)
    dx, dgpre0 = _dz_wt_rms_bwd(dz0, p["ev_w_in_t"], x, row(s["mix_norm_pre"][0]), dx)
    grads_ready(["ev_w_in", "ev_w_out"], big)

    g["mix_norm_pre"] = jnp.concatenate([dgpre0, dgpre1], axis=0)
    g["mix_norm_post"] = jnp.concatenate([dgpost0, dgpost1], axis=0)
    g["ffn_norm_pre"] = jnp.concatenate([dgfpre0, dgfpre1], axis=0)
    g["ffn_norm_post"] = jnp.concatenate([dgfpost0, dgfpost1], axis=0)
    g["ev_a_conv_w"] = dcw31[None, 0:A_CONV]
    g["ev_a_conv_b"], g["ev_a_ln_g"], g["ev_a_ln_b"] = dvec[0:1], dvec[1:2], dvec[2:3]
    g["ev_sinks"] = dsinks
    g["ffn_conv_w"] = jnp.stack([dcw0, dcw1])
    return lpart[0, 0], dx, big, g


MESH = pl.DeviceIdType.MESH


def _all_gather(shards, name):
    nw = len(shards)

    def body(*refs):
        x_refs, out_refs = refs[:nw], refs[nw:2 * nw]
        send_sems, recv_sems, local_sems = refs[2 * nw:]
        x, y, c = lax.axis_index("x"), lax.axis_index("y"), lax.axis_index("c")
        me, sibling = (x, y, c), (x, y, 1 - c)
        chips = [(1 - x, y), (x, 1 - y), (1 - x, 1 - y)]

        def rows(w, px, py, pc):
            m_per = shards[w].shape[0]
            return out_refs[w].at[pl.ds((4 * px + 2 * py + pc) * m_per, m_per), :]

        def copy(w, k, block, to, src=None):
            return pltpu.make_async_remote_copy(
                src_ref=rows(w, *block) if src is None else src, dst_ref=rows(w, *block),
                send_sem=send_sems.at[w, k], recv_sem=recv_sems.at[w, k], device_id=to, device_id_type=MESH)

        mine, first, passed = [], [], []
        for w in range(nw):
            cp = pltpu.make_async_copy(x_refs[w], rows(w, *me), local_sems.at[w])
            cp.start()
            mine.append(cp)
            first.append([copy(w, 0, me, sibling, src=x_refs[w])]
                         + [copy(w, 1 + j, me, (*chip, c), src=x_refs[w]) for j, chip in enumerate(chips)])
            for cp in first[w]:
                cp.start()
        for w in range(nw):
            passed.append([copy(w, 4 + j, (*chip, c), sibling) for j, chip in enumerate(chips)])
            for j, chip in enumerate(chips):
                copy(w, 1 + j, (*chip, c), me).wait_recv()
                passed[w][j].start()
        for w in range(nw):
            copy(w, 0, sibling, me).wait_recv()
            for j, chip in enumerate(chips):
                copy(w, 4 + j, (*chip, 1 - c), me).wait_recv()
            for cp in first[w] + passed[w]:
                cp.wait_send()
            mine[w].wait()

    return pl.pallas_call(
        body, name=name,
        out_shape=[jax.ShapeDtypeStruct((N_DEV * a.shape[0], a.shape[1]), a.dtype) for a in shards],
        in_specs=[_ANY] * nw, out_specs=[_ANY] * nw,
        scratch_shapes=[pltpu.SemaphoreType.DMA((nw, 7)), pltpu.SemaphoreType.DMA((nw, 7)),
                        pltpu.SemaphoreType.DMA((nw,))],
    )(*shards)


_HBM = pl.BlockSpec(memory_space=pltpu.HBM)
_SEM = pl.BlockSpec(memory_space=pltpu.SEMAPHORE)
_EFFECT = pltpu.SideEffectType.DATAFLOW_SIDE_EFFECTING
_RELATIONS = [(dx, dy, dc) for dx in (0, 1) for dy in (0, 1) for dc in (0, 1)][1:]


def _peer(rel):
    x, y, c = lax.axis_index("x"), lax.axis_index("y"), lax.axis_index("c")
    px, py, pc = x ^ rel[0], y ^ rel[1], c ^ rel[2]
    return (px, py, pc), 4 * px + 2 * py + pc, 4 * x + 2 * y + c


def _exchange_copy(k, rel, src_ref, land_ref, send_sems, recv_sems, w, scatter):
    peer, peer_idx, my_idx = _peer(rel)
    src = src_ref.at[peer_idx] if scatter else src_ref
    return pltpu.make_async_remote_copy(
        src_ref=src, dst_ref=land_ref.at[my_idx], send_sem=send_sems.at[_sends(scatter) * w + k],
        recv_sem=recv_sems.at[7 * w + k], device_id=peer, device_id_type=MESH)


def _sends(scatter):
    return 7 if scatter else 8


def _own_copy(src_ref, land_ref, send_sems, w):
    my_idx = _peer(_RELATIONS[0])[2]
    return pltpu.make_async_copy(src_ref, land_ref.at[my_idx], send_sems.at[8 * w + 7])


def _exchange_start(srcs, scatter, name):
    nw = len(srcs)
    lands = [lax.empty((N_DEV,) + (a.shape[1:] if scatter else a.shape), a.dtype) for a in srcs]

    def body(*refs):
        src_refs, land_refs = refs[:nw], refs[nw:2 * nw]
        send_sems, recv_sems = refs[2 * nw], refs[2 * nw + 1]
        token = refs[-1]
        for w in range(nw):
            for k, rel in enumerate(_RELATIONS):
                _exchange_copy(k, rel, src_refs[w], land_refs[w], send_sems, recv_sems, w, scatter).start()
            if not scatter:
                _own_copy(src_refs[w], land_refs[w], send_sems, w).start()
        token[...] = jnp.zeros_like(token)

    hbm = lambda a: pltpu.HBM(a.shape, a.dtype)
    outs = pl.pallas_call(
        body, name=name,
        out_shape=(pltpu.SemaphoreType.DMA((_sends(scatter) * nw,)), pltpu.SemaphoreType.DMA((7 * nw,)),
                   *[hbm(a) for a in srcs],
                   *[hbm(a) for a in lands], jax.ShapeDtypeStruct((8, 128), F32)),
        in_specs=[_HBM] * (2 * nw),
        out_specs=(_SEM, _SEM, *[_HBM] * (2 * nw), pl.BlockSpec(memory_space=pltpu.VMEM)),
        input_output_aliases={i: 2 + i for i in range(2 * nw)},
        compiler_params=pltpu.CompilerParams(has_side_effects=_EFFECT),
    )(*[pltpu.with_memory_space_constraint(a, pltpu.HBM) for a in srcs],
      *[pltpu.with_memory_space_constraint(a, pltpu.HBM) for a in lands])
    return outs[0], outs[1], list(outs[2:2 + nw]), list(outs[2 + nw:2 + 2 * nw]), outs[-1]


def _exchange_wait(started, scatter, after, name):
    send_sems, recv_sems, srcs, lands, _ = started
    nw = len(srcs)

    def body(*refs):
        src_refs, land_refs = refs[:nw], refs[nw:2 * nw]
        send_s, recv_s = refs[2 * nw], refs[2 * nw + 1]
        for w in range(nw):
            for k, rel in enumerate(_RELATIONS):
                cp = _exchange_copy(k, rel, src_refs[w], land_refs[w], send_s, recv_s, w, scatter)
                cp.wait_send()
                _, peer_idx, _ = _peer(rel)
                pltpu.make_async_remote_copy(
                    src_ref=src_refs[w].at[peer_idx] if scatter else src_refs[w], dst_ref=land_refs[w].at[peer_idx],
                    send_sem=send_s.at[_sends(scatter) * w + k], recv_sem=recv_s.at[7 * w + k],
                    device_id=_peer(rel)[0], device_id_type=MESH).wait_recv()
            if not scatter:
                _own_copy(src_refs[w], land_refs[w], send_s, w).wait()

    hbm = lambda a: pltpu.HBM(a.shape, a.dtype)
    outs = pl.pallas_call(
        body, name=name, out_shape=tuple(hbm(a) for a in srcs + lands),
        in_specs=[_HBM] * (2 * nw) + [_SEM, _SEM, _ANY], out_specs=tuple([_HBM] * (2 * nw)),
        input_output_aliases={i: i for i in range(2 * nw)},
        compiler_params=pltpu.CompilerParams(has_side_effects=_EFFECT),
    )(*srcs, *lands, send_sems, recv_sems, after)
    return list(outs[nw:])


def _to_bf16(a):
    _, r, l = a.shape
    tr = _row_tile(r, 512)

    def body(a_ref, o_ref):
        o_ref[...] = a_ref[...].astype(BF16)

    spec = pl.BlockSpec((1, tr, l), lambda j, i: (j, i, 0))
    return pl.pallas_call(
        body, name="to_bf16", grid=(N_DEV, r // tr), in_specs=[spec], out_specs=spec,
        out_shape=jax.ShapeDtypeStruct(a.shape, BF16), compiler_params=_cp(("arbitrary", "arbitrary")),
    )(a)


def _row_tile(rows, cap):
    best = None
    for d in range(16, min(rows, cap) + 1, 16):
        if rows % d == 0:
            best = d
    return rows if best is None else best


def _adam_math(w, g, m, v):
    bc1 = 1.0 - ADAM_B1 ** ADAM_STEP
    bc2 = 1.0 - ADAM_B2 ** ADAM_STEP
    mn = ADAM_B1 * m + (1.0 - ADAM_B1) * g
    vn = ADAM_B2 * v + (1.0 - ADAM_B2) * (g * g)
    return -ADAM_LR * ((mn / bc1) / (jnp.sqrt(vn / bc2) + ADAM_EPS) + ADAM_WD * w), mn, vn


def _adamw_rs(gp, land, w, m, v, dev):
    _, r, l = gp.shape
    tr = _row_tile(r, 256)

    def body(i_ref, g_ref, b_ref, w_ref, m_ref, v_ref, go_ref, d_ref, mo_ref, vo_ref):
        g = g_ref[0]
        for j in range(N_DEV):
            g = g + jnp.where(i_ref[0] == j, 0.0, b_ref[j].astype(F32))
        go_ref[...] = g
        d_ref[...], mo_ref[...], vo_ref[...] = _adam_math(w_ref[...], g, m_ref[...], v_ref[...])

    spec = pl.BlockSpec((tr, l), lambda i, s: (i, 0))
    return pl.pallas_call(
        body, name="adamw_rs", out_shape=[jax.ShapeDtypeStruct((r, l), F32)] * 4,
        grid_spec=pltpu.PrefetchScalarGridSpec(
            num_scalar_prefetch=1, grid=(r // tr,),
            in_specs=[pl.BlockSpec((1, tr, l), lambda i, s: (s[0], i, 0)),
                      pl.BlockSpec((N_DEV, tr, l), lambda i, s: (0, i, 0)), spec, spec, spec],
            out_specs=[spec] * 4),
        compiler_params=_cp(("arbitrary",)),
    )(dev, gp, land, w, m, v)


def _sum_blocks(a, nblk):
    m = a.shape[0] // nblk
    n = a.shape[1]

    def body(a_ref, o_ref):
        acc = a_ref[0]
        for j in range(1, nblk):
            acc = acc + a_ref[j]
        o_ref[...] = acc

    return pl.pallas_call(
        body, name="sum_blocks", out_shape=jax.ShapeDtypeStruct((m, n), a.dtype),
        in_specs=[_full((nblk, m, n))], out_specs=_full((m, n)),
    )(a.reshape(nblk, m, n))


def _adamw(w, g, m, v):
    rows, c = w.shape

    def body(w_ref, g_ref, m_ref, v_ref, d_ref, mo_ref, vo_ref):
        d_ref[...], mo_ref[...], vo_ref[...] = _adam_math(w_ref[...], g_ref[...], m_ref[...], v_ref[...])

    return pl.pallas_call(
        body, name="adamw", in_specs=[_full((rows, c))] * 4, out_specs=[_full((rows, c))] * 3,
        out_shape=[jax.ShapeDtypeStruct((rows, c), F32)] * 3,
    )(w, g, m, v)


WEIGHTS = ["mix_norm_pre", "mix_norm_post", "ffn_norm_pre", "ffn_norm_post", "ev_w_in", "ev_a_conv_w", "ev_a_conv_b",
           "ev_a_ln_g", "ev_a_ln_b", "ev_sinks", "ev_w_out", "od_w_in", "od_conv_w", "od_w_out", "ffn_w_up",
           "ffn_conv_w", "ffn_w_down"]
BIG = ["ev_w_in", "ev_w_out", "od_w_in", "od_w_out", "ffn_w_up", "ffn_w_down"]
SMALL_REPL = ["mix_norm_pre", "mix_norm_post", "ffn_norm_pre", "ffn_norm_post", "ev_a_conv_b", "ev_a_ln_g",
              "ev_a_ln_b", "ev_sinks"]
SMALL_SHARDED = ["ev_a_conv_w", "od_conv_w", "ffn_conv_w"]


def _pack(arrs, rows):
    flat = jnp.concatenate([a.reshape(-1) for a in arrs])
    return jnp.pad(flat, (0, rows * LANES - flat.shape[0])).reshape(rows, LANES)


def _unpack(packed, shapes):
    flat, out, off = packed.reshape(-1), [], 0
    for s in shapes:
        n = 1
        for d in s:
            n *= d
        out.append(flat[off:off + n].reshape(s))
        off += n
    return out


def kernel(x, positions, mix_norm_pre, mix_norm_post, ffn_norm_pre, ffn_norm_post, ev_w_in, ev_a_conv_w, ev_a_conv_b, ev_a_ln_g, ev_a_ln_b, ev_sinks, ev_w_out, od_w_in, od_conv_w, od_w_out, ffn_w_up, ffn_conv_w, ffn_w_down, loss_target, m_mix_norm_pre, m_mix_norm_post, m_ffn_norm_pre, m_ffn_norm_post, m_ev_w_in, m_ev_a_conv_w, m_ev_a_conv_b, m_ev_a_ln_g, m_ev_a_ln_b, m_ev_sinks, m_ev_w_out, m_od_w_in, m_od_conv_w, m_od_w_out, m_ffn_w_up, m_ffn_conv_w, m_ffn_w_down, v_mix_norm_pre, v_mix_norm_post, v_ffn_norm_pre, v_ffn_norm_post, v_ev_w_in, v_ev_a_conv_w, v_ev_a_conv_b, v_ev_a_ln_g, v_ev_a_ln_b, v_ev_sinks, v_ev_w_out, v_od_w_in, v_od_conv_w, v_od_w_out, v_ffn_w_up, v_ffn_conv_w, v_ffn_w_down):
    w = dict(zip(WEIGHTS, (mix_norm_pre, mix_norm_post, ffn_norm_pre, ffn_norm_post, ev_w_in, ev_a_conv_w, ev_a_conv_b,
                           ev_a_ln_g, ev_a_ln_b, ev_sinks, ev_w_out, od_w_in, od_conv_w, od_w_out, ffn_w_up, ffn_conv_w,
                           ffn_w_down)))
    mom = dict(zip(WEIGHTS, (m_mix_norm_pre, m_mix_norm_post, m_ffn_norm_pre, m_ffn_norm_post, m_ev_w_in, m_ev_a_conv_w,
                             m_ev_a_conv_b, m_ev_a_ln_g, m_ev_a_ln_b, m_ev_sinks, m_ev_w_out, m_od_w_in, m_od_conv_w,
                             m_od_w_out, m_ffn_w_up, m_ffn_conv_w, m_ffn_w_down)))
    var = dict(zip(WEIGHTS, (v_mix_norm_pre, v_mix_norm_post, v_ffn_norm_pre, v_ffn_norm_post, v_ev_w_in, v_ev_a_conv_w,
                             v_ev_a_conv_b, v_ev_a_ln_g, v_ev_a_ln_b, v_ev_sinks, v_ev_w_out, v_od_w_in, v_od_conv_w,
                             v_od_w_out, v_ffn_w_up, v_ffn_conv_w, v_ffn_w_down)))
    ix, iy, ic = lax.axis_index("x"), lax.axis_index("y"), lax.axis_index("c")
    dev = 4 * ix + 2 * iy + ic
    two = lambda a: a.reshape(-1, a.shape[-1])

    dev1 = jnp.reshape(dev, (1,)).astype(jnp.int32)
    shard = {n: two(w[n].astype(BF16)) for n in BIG}
    gathered = lambda n, a: a.reshape((N_DEV,) + w[n].shape)
    ev_names = [n for n in BIG if n.startswith("ev_")]
    ev_gat = _all_gather([shard[n] for n in ev_names], "gather_ev")
    p = _prep_ev({n: gathered(n, a) for n, a in zip(ev_names, ev_gat)})
    rest_names = [n for n in BIG if not n.startswith("ev_")]
    first = shard[rest_names[0]] + (ev_gat[0][0:1, 0:1] * 0).astype(BF16)
    started = _exchange_start([first] + [shard[n] for n in rest_names[1:]], False, "gather_start")

    def rest_weights(after):
        lands = _exchange_wait(started, False, after, "gather_wait")
        return _prep_rest({n: gathered(n, a) for n, a in zip(rest_names, lands)})

    small = {n: w[n] for n in SMALL_REPL}
    small_shapes = [w[n].shape for n in SMALL_SHARDED]
    conv_gat = _all_gather([_pack([w[n] for n in SMALL_SHARDED], 8)], "gather_conv")[0].reshape(N_DEV, 8, LANES)
    per_dev = [_unpack(conv_gat[d], small_shapes) for d in range(N_DEV)]
    for k, n in enumerate(SMALL_SHARDED):
        small[n] = jnp.concatenate([per_dev[d][k] for d in range(N_DEV)], axis=-1)

    exchanges = []

    def grads_ready(names, big):
        bufs = [big[n].reshape(N_DEV, -1, w[n].shape[-1]) for n in names]
        st = _exchange_start([_to_bf16(b) for b in bufs], True, "grads_start_" + names[0])
        exchanges.append((names, bufs, st))
        return st[-1][0, 0]

    lpart, grad_x, big, g = _local_step(x[0], positions[0], loss_target[0], p, rest_weights, small, started[-1][0, 0],
                                        grads_ready)
    loss = lax.psum(lpart, ("x", "y", "c"))

    grads, delta, new_m, new_v = {}, {}, {}, {}
    for names, bufs, st in exchanges:
        lands = _exchange_wait(st, True, grad_x, "grads_wait_" + names[0])
        for n, b, land in zip(names, bufs, lands):
            outs = _adamw_rs(b, land, two(w[n]), two(mom[n]), two(var[n]), dev1)
            grads[n], delta[n], new_m[n], new_v[n] = (a.reshape(w[n].shape) for a in outs)

    small_names = SMALL_REPL + SMALL_SHARDED
    s_all = _sum_blocks(_all_gather([_pack([g[n] for n in small_names], 64)], "gather_small_grads")[0], N_DEV)
    for n, a in zip(small_names, _unpack(s_all, [small[n].shape for n in small_names])):
        if n in SMALL_SHARDED:
            width = w[n].shape[-1]
            a = lax.dynamic_slice_in_dim(a, dev * width, width, axis=a.ndim - 1)
        grads[n] = a
    pk = lambda dct: _pack([dct[n] for n in small_names], 16)
    outs = _adamw(pk(w), pk(grads), pk(mom), pk(var))
    for dst, packed in zip((delta, new_m, new_v), outs):
        for n, a in zip(small_names, _unpack(packed, [w[n].shape for n in small_names])):
            dst[n] = a

    return (loss, grad_x[None], *[grads[n] for n in WEIGHTS], *[delta[n] for n in WEIGHTS],
            *[new_m[n] for n in WEIGHTS], *[new_v[n] for n in WEIGHTS])
```

```python
import jax
import jax.numpy as jnp
from jax import lax
from jax.experimental import pallas as pl
from jax.experimental.pallas import tpu as pltpu

F32, BF16 = jnp.float32, jnp.bfloat16

D_MODEL = 1024
A_CH = 512
A_CONV = 31
Q_DIM = 512
KV_DIM = 128
HEAD_DIM = 64
N_Q_HEADS = 8
N_KV_HEADS = 2
GROUP = 4
BLOCK = 128
EVEN_IN = 1792
SC_DIM = 1024
D_FF = 2816
ROPE_THETA = 500000.0
ROPE_DIM = 16
RMS_EPS = 1e-6
LN_EPS = 1e-5
SCALE = HEAD_DIM ** -0.5
NEG = -1e30

ADAM_LR, ADAM_B1, ADAM_B2, ADAM_EPS, ADAM_WD, ADAM_STEP = 0.001, 0.9, 0.999, 1e-08, 0.01, 10

N_DEV = 8
FF_N = 2 * D_FF // N_DEV
LANES = 1024
HALO3 = 8
HALO31 = 32
VMEM_LIMIT = 56 * 1024 * 1024

TM = 512
TM_BWD = 256
TK_DW = 2048
ATT_NB = 4

_ANY = pl.BlockSpec(memory_space=pl.ANY)
_CONTRACT_LAST = (((1,), (1,)), ((), ()))
_CONTRACT_FIRST = (((0,), (0,)), ((), ()))


def _cp(sem, vmem=VMEM_LIMIT):
    return pltpu.CompilerParams(dimension_semantics=sem, vmem_limit_bytes=vmem)


def _full(shape):
    n = len(shape)
    return pl.BlockSpec(shape, lambda *_: (0,) * n)


def _rows(tm, n):
    return pl.BlockSpec((tm, n), lambda i, *_: (i, 0))


def _sigmoid(x):
    return 0.5 * jnp.tanh(0.5 * x) + 0.5


def _rsqrt_mean(x):
    return lax.rsqrt(jnp.mean(x * x, axis=-1, keepdims=True) + RMS_EPS)


def _rms_bwd(x, g, dy):
    r = _rsqrt_mean(x)
    xh = x * r
    dxh = dy * g
    dx = r * (dxh - xh * jnp.mean(dxh * xh, axis=-1, keepdims=True))
    return dx, jnp.sum(dy * xh, axis=0, keepdims=True)


def _acc_out(ref, first, val):
    @pl.when(first)
    def _():
        ref[...] = val

    @pl.when(jnp.logical_not(first))
    def _():
        ref[...] += val


def _rope_tables(positions):
    half = ROPE_DIM // 2
    inv_freq = ROPE_THETA ** (-(jnp.arange(half, dtype=F32) * 2.0 / ROPE_DIM))
    ang = positions.astype(F32)[:, None] * inv_freq
    cos, sin = jnp.cos(ang), jnp.sin(ang)
    t = positions.shape[0]
    one, zero = jnp.ones((t, HEAD_DIM - ROPE_DIM), F32), jnp.zeros((t, HEAD_DIM - ROPE_DIM), F32)
    z8 = jnp.zeros((t, half), F32)
    c = jnp.concatenate([cos, cos, one], axis=1)
    sa = jnp.concatenate([z8, sin, zero], axis=1)
    sb = jnp.concatenate([-sin, z8, zero], axis=1)
    return tuple(jnp.tile(a, (1, 2)) for a in (c, sa, sb))


def _rope(t, c, sa, sb):
    return t * c + pltpu.roll(t, 8, 1) * sa + pltpu.roll(t, 120, 1) * sb


def _rope_bwd(d, c, sa, sb):
    return d * c + pltpu.roll(d * sa, 120, 1) + pltpu.roll(d * sb, 8, 1)


def _ev_in(x, gpre, w_in, rc, rsa, rsb):
    t = x.shape[0]
    tm = min(TM, t)

    def body(x_ref, g_ref, w_ref, c_ref, sa_ref, sb_ref, h_ref, zag_ref, q_ref, k_ref, v_ref):
        xv = x_ref[...]
        h = (xv * _rsqrt_mean(xv) * g_ref[...]).astype(BF16)
        h_ref[...] = h
        z = jnp.dot(h, w_ref[...], preferred_element_type=F32)
        zag_ref[...] = z[:, :2 * A_CH].astype(BF16)
        c, sa, sb = c_ref[...], sa_ref[...], sb_ref[...]
        q0 = 2 * A_CH
        for j in range(Q_DIM // 128):
            q_ref[:, 128 * j:128 * (j + 1)] = _rope(z[:, q0 + 128 * j:q0 + 128 * (j + 1)], c, sa, sb).astype(BF16)
        k0 = q0 + Q_DIM
        k_ref[...] = _rope(z[:, k0:k0 + KV_DIM], c, sa, sb).astype(BF16)
        v_ref[...] = z[:, k0 + KV_DIM:k0 + 2 * KV_DIM].astype(BF16)

    return pl.pallas_call(
        body, name="ev_in", grid=(t // tm,),
        in_specs=[_rows(tm, D_MODEL), _full((1, D_MODEL)), _full((D_MODEL, EVEN_IN)),
                  _rows(tm, 128), _rows(tm, 128), _rows(tm, 128)],
        out_specs=[_rows(tm, D_MODEL), _rows(tm, 2 * A_CH), _rows(tm, Q_DIM), _rows(tm, KV_DIM), _rows(tm, KV_DIM)],
        out_shape=[jax.ShapeDtypeStruct((t, D_MODEL), BF16), jax.ShapeDtypeStruct((t, 2 * A_CH), BF16),
                   jax.ShapeDtypeStruct((t, Q_DIM), BF16), jax.ShapeDtypeStruct((t, KV_DIM), BF16),
                   jax.ShapeDtypeStruct((t, KV_DIM), BF16)],
        compiler_params=_cp(("arbitrary",)),
    )(x, gpre, w_in, rc, rsa, rsb)


def _glu(zag):
    z = zag.astype(F32)
    return z[:, :A_CH] * jax.nn.sigmoid(z[:, A_CH:])


def _conf_fwd(zag, conv_w, conv_b, ln_g, ln_b):
    t = zag.shape[0]
    tm = min(TM_BWD, t)

    def body(z_ref, w_ref, b_ref, g_ref, lb_ref, c_ref, a_ref, ext):
        i = pl.program_id(0)

        @pl.when(i == 0)
        def _():
            ext[0:HALO31, :] = jnp.zeros((HALO31, A_CH), F32)

        ext[HALO31:HALO31 + tm, :] = _glu(z_ref[...])
        acc = jnp.zeros((tm, A_CH), F32)
        for j in range(A_CONV):
            s = HALO31 - (A_CONV - 1) + j
            acc = acc + w_ref[j:j + 1, :] * ext[s:s + tm, :]
        ext[0:HALO31, :] = ext[tm:tm + HALO31, :]
        cv = acc + b_ref[...]
        c_ref[...] = cv
        mu = jnp.mean(cv, axis=-1, keepdims=True)
        xc = cv - mu
        ln = xc * lax.rsqrt(jnp.mean(xc * xc, axis=-1, keepdims=True) + LN_EPS) * g_ref[...] + lb_ref[...]
        a_ref[...] = (ln * jax.nn.sigmoid(ln)).astype(BF16)

    return pl.pallas_call(
        body, name="conf_fwd", grid=(t // tm,),
        in_specs=[_rows(tm, 2 * A_CH), _full((32, A_CH)), _full((1, A_CH)), _full((1, A_CH)), _full((1, A_CH))],
        out_specs=[_rows(tm, A_CH), _rows(tm, A_CH)],
        out_shape=[jax.ShapeDtypeStruct((t, A_CH), F32), jax.ShapeDtypeStruct((t, A_CH), BF16)],
        scratch_shapes=[pltpu.VMEM((HALO31 + tm, A_CH), F32)],
        compiler_params=_cp(("arbitrary",)),
    )(zag, conv_w, conv_b, ln_g, ln_b)


def _attn_mask(first_block):
    row = lax.broadcasted_iota(jnp.int32, (GROUP * BLOCK, 2 * BLOCK), 0) & (BLOCK - 1)
    col = lax.broadcasted_iota(jnp.int32, (GROUP * BLOCK, 2 * BLOCK), 1)
    diff = row + BLOCK - col
    return (diff >= 0) & (diff < BLOCK) & ((col >= BLOCK) | jnp.logical_not(first_block))


def _sink_rows(s_ref, h):
    grp = lax.broadcasted_iota(jnp.int32, (GROUP * BLOCK, 1), 0) >> 7
    out = jnp.full((GROUP * BLOCK, 1), s_ref[GROUP * h], F32)
    for g in range(1, GROUP):
        out = jnp.where(grp == g, s_ref[GROUP * h + g], out)
    return out


def _attn_probs(q4, k2, mask, sink):
    s = lax.dot_general(q4, k2, _CONTRACT_LAST, preferred_element_type=F32) * SCALE
    s = jnp.where(mask, s, NEG)
    m = jnp.maximum(jnp.max(s, axis=-1, keepdims=True), sink)
    p = jnp.exp(s - m)
    es = jnp.exp(sink - m)
    inv = 1.0 / (jnp.sum(p, axis=-1, keepdims=True) + es)
    return p * inv, es * inv


def _q_heads(q, h):
    return jnp.concatenate([q[:, HEAD_DIM * (GROUP * h + g):HEAD_DIM * (GROUP * h + g + 1)] for g in range(GROUP)],
                           axis=0)


def _kv_head(prev, cur, h):
    return jnp.concatenate([prev[:, HEAD_DIM * h:HEAD_DIM * (h + 1)], cur[:, HEAD_DIM * h:HEAD_DIM * (h + 1)]], axis=0)


def _attn_fwd(q, k, v, sinks):
    t = q.shape[0]
    nb = min(ATT_NB, t // BLOCK)
    rows = nb * BLOCK

    def body(s_ref, q_ref, kc_ref, kp_ref, vc_ref, vp_ref, o_ref):
        first = pl.program_id(0) == 0
        for b in range(nb):
            lo = BLOCK * b
            mask = _attn_mask(first) if b == 0 else _attn_mask(False)
            qv, kc, vc = q_ref[lo:lo + BLOCK, :], kc_ref[lo:lo + BLOCK, :], vc_ref[lo:lo + BLOCK, :]
            kp = kp_ref[...] if b == 0 else kc_ref[lo - BLOCK:lo, :]
            vp = vp_ref[...] if b == 0 else vc_ref[lo - BLOCK:lo, :]
            for h in range(N_KV_HEADS):
                pn, _ = _attn_probs(_q_heads(qv, h), _kv_head(kp, kc, h), mask, _sink_rows(s_ref, h))
                o4 = jnp.dot(pn.astype(BF16), _kv_head(vp, vc, h), preferred_element_type=F32).astype(BF16)
                for g in range(GROUP):
                    c0 = HEAD_DIM * (GROUP * h + g)
                    o_ref[lo:lo + BLOCK, c0:c0 + HEAD_DIM] = o4[BLOCK * g:BLOCK * (g + 1), :]

    cur = lambda n: pl.BlockSpec((rows, n), lambda i: (i, 0))
    prev = lambda n: pl.BlockSpec((BLOCK, n), lambda i: (jnp.maximum(i * nb - 1, 0), 0))
    return pl.pallas_call(
        body, name="attn_fwd", grid=(t // rows,),
        in_specs=[pl.BlockSpec(memory_space=pltpu.SMEM), cur(Q_DIM), cur(KV_DIM), prev(KV_DIM), cur(KV_DIM),
                  prev(KV_DIM)],
        out_specs=cur(Q_DIM),
        out_shape=jax.ShapeDtypeStruct((t, Q_DIM), BF16),
        compiler_params=_cp(("arbitrary",)),
    )(sinks, q, k, k, v, v)


def _out_post(lhs, ws, x_in, gpost):
    t = x_in.shape[0]
    tm = min(TM, t)
    n = len(lhs)

    def body(*refs):
        x_ref, g_ref, m_ref, xo_ref = refs[2 * n:]
        m = jnp.dot(refs[0][...], refs[n][...], preferred_element_type=F32)
        for j in range(1, n):
            m = m + jnp.dot(refs[j][...], refs[n + j][...], preferred_element_type=F32)
        m_ref[...] = m
        xo_ref[...] = x_ref[...] + m * _rsqrt_mean(m) * g_ref[...]

    return pl.pallas_call(
        body, name="out_post", grid=(t // tm,),
        in_specs=[_rows(tm, a.shape[1]) for a in lhs] + [_full(w.shape) for w in ws]
                 + [_rows(tm, D_MODEL), _full((1, D_MODEL))],
        out_specs=[_rows(tm, D_MODEL), _rows(tm, D_MODEL)],
        out_shape=[jax.ShapeDtypeStruct((t, D_MODEL), F32)] * 2,
        compiler_params=_cp(("arbitrary",)),
    )(*lhs, *ws, x_in, gpost)


def _conv3(w_ref, ext, tm):
    s = HALO3 - 2
    return (w_ref[0:1, :] * ext[s:s + tm, :] + w_ref[1:2, :] * ext[s + 1:s + 1 + tm, :]
            + w_ref[2:3, :] * ext[s + 2:s + 2 + tm, :])


def _ffn_fwd(x1, gpre, wup, layer, cw, wd, gpost):
    t = x1.shape[0]
    tm = min(TM, t)
    nc, n = wup.shape[1], wup.shape[4]

    def body(x_ref, gpre_ref, wup_ref, cw_ref, wd_ref, gpost_ref, h_ref, up_ref, u_ref, f_ref, xo_ref, h_s, acc, ext, hal):
        i, c = pl.program_id(0), pl.program_id(1)

        @pl.when(c == 0)
        def _():
            xv = x_ref[...]
            h = (xv * _rsqrt_mean(xv) * gpre_ref[...]).astype(BF16)
            h_s[...] = h
            h_ref[...] = h

        @pl.when(i == 0)
        def _():
            hal[c] = jnp.zeros((2, HALO3, n), F32)

        u = []
        for gv in range(2):
            up = jnp.dot(h_s[...], wup_ref[gv, 0, 0], preferred_element_type=F32)
            up_ref[gv, 0] = up.astype(BF16)
            ext[gv, 0:HALO3, :] = hal[c, gv]
            ext[gv, HALO3:HALO3 + tm, :] = up
            hal[c, gv] = ext[gv, tm:tm + HALO3, :]
            s = HALO3 - 2
            u.append(cw_ref[gv, 0, 0:1, :] * ext[gv, s:s + tm, :] + cw_ref[gv, 0, 1:2, :] * ext[gv, s + 1:s + 1 + tm, :]
                     + cw_ref[gv, 0, 2:3, :] * up)
            u_ref[gv, 0] = u[gv].astype(BF16)
        act = (u[0] * _sigmoid(u[0]) * u[1]).astype(BF16)
        part = jnp.dot(act, wd_ref[...], preferred_element_type=F32)

        @pl.when(c == 0)
        def _():
            acc[...] = part

        @pl.when(jnp.logical_and(c > 0, c < nc - 1))
        def _():
            acc[...] += part

        @pl.when(c == nc - 1)
        def _():
            f = acc[...] + part
            f_ref[...] = f
            xo_ref[...] = x_ref[...] + f * _rsqrt_mean(f) * gpost_ref[...]

    row = lambda w: pl.BlockSpec((tm, w), lambda i, c: (i, 0))
    one = _full((1, D_MODEL))
    return pl.pallas_call(
        body, name="ffn_fwd", grid=(t // tm, nc),
        in_specs=[row(D_MODEL), one, pl.BlockSpec((2, 1, 1, D_MODEL, n), lambda i, c: (0, c, layer, 0, 0)),
                  pl.BlockSpec((2, 1, 3, n), lambda i, c: (0, c, 0, 0)), pl.BlockSpec((n, D_MODEL), lambda i, c: (c, 0)),
                  one],
        out_specs=[row(D_MODEL), pl.BlockSpec((2, 1, tm, n), lambda i, c: (0, c, i, 0)),
                   pl.BlockSpec((2, 1, tm, n), lambda i, c: (0, c, i, 0)), row(D_MODEL), row(D_MODEL)],
        out_shape=[jax.ShapeDtypeStruct((t, D_MODEL), BF16), jax.ShapeDtypeStruct((2, nc, t, n), BF16),
                   jax.ShapeDtypeStruct((2, nc, t, n), BF16), jax.ShapeDtypeStruct((t, D_MODEL), F32),
                   jax.ShapeDtypeStruct((t, D_MODEL), F32)],
        scratch_shapes=[pltpu.VMEM((tm, D_MODEL), BF16), pltpu.VMEM((tm, D_MODEL), F32),
                        pltpu.VMEM((2, HALO3 + tm, n), F32), pltpu.VMEM((nc, 2, HALO3, n), F32)],
        compiler_params=_cp(("arbitrary", "arbitrary")),
    )(x1, gpre, wup, cw, wd, gpost)


def _od_fwd(x_in, gpre, w_in, cw, w_out, gpost):
    t = x_in.shape[0]
    tm = min(TM, t)
    ns, _, n = w_in.shape

    def body(x_ref, gpre_ref, w_ref, cw_ref, wo_ref, gpost_ref, h_ref, z_ref, cv_ref, y_ref, m_ref, xo_ref, z_s, ext):
        i = pl.program_id(0)
        xv = x_ref[...]
        h = (xv * _rsqrt_mean(xv) * gpre_ref[...]).astype(BF16)
        h_ref[...] = h
        for j in range(ns):
            z_s[:, n * j:n * (j + 1)] = jnp.dot(h, w_ref[j], preferred_element_type=F32)
        z_ref[...] = z_s[...].astype(BF16)

        @pl.when(i == 0)
        def _():
            ext[0:HALO3, :] = jnp.zeros((HALO3, SC_DIM), F32)

        ext[HALO3:HALO3 + tm, :] = z_s[:, SC_DIM:2 * SC_DIM] * z_s[:, 2 * SC_DIM:]
        cv = _conv3(cw_ref, ext, tm)
        cv_ref[...] = cv.astype(BF16)
        y = (z_s[:, :SC_DIM] * cv).astype(BF16)
        ext[0:HALO3, :] = ext[tm:tm + HALO3, :]
        y_ref[...] = y
        m = jnp.dot(y, wo_ref[...], preferred_element_type=F32)
        m_ref[...] = m
        xo_ref[...] = xv + m * _rsqrt_mean(m) * gpost_ref[...]

    return pl.pallas_call(
        body, name="od_fwd", grid=(t // tm,),
        in_specs=[_rows(tm, D_MODEL), _full((1, D_MODEL)), _full((ns, D_MODEL, n)), _full((3, SC_DIM)),
                  _full((SC_DIM, D_MODEL)), _full((1, D_MODEL))],
        out_specs=[_rows(tm, D_MODEL), _rows(tm, 3 * SC_DIM), _rows(tm, SC_DIM), _rows(tm, SC_DIM), _rows(tm, D_MODEL),
                   _rows(tm, D_MODEL)],
        out_shape=[jax.ShapeDtypeStruct((t, D_MODEL), BF16), jax.ShapeDtypeStruct((t, 3 * SC_DIM), BF16),
                   jax.ShapeDtypeStruct((t, SC_DIM), BF16), jax.ShapeDtypeStruct((t, SC_DIM), BF16),
                   jax.ShapeDtypeStruct((t, D_MODEL), F32), jax.ShapeDtypeStruct((t, D_MODEL), F32)],
        scratch_shapes=[pltpu.VMEM((tm, 3 * SC_DIM), F32), pltpu.VMEM((HALO3 + tm, SC_DIM), F32)],
        compiler_params=_cp(("arbitrary",)),
    )(x_in, gpre, w_in, cw, w_out, gpost)


def _loss_grad(y, target):
    t = y.shape[0]
    tm = min(TM, t)

    def body(y_ref, t_ref, dy_ref, l_ref):
        e = y_ref[...] - t_ref[...]
        dy_ref[...] = e * (1.0 / D_MODEL)
        part = jnp.zeros((1, 128), F32) + jnp.sum(e * e) * (0.5 / D_MODEL)
        _acc_out(l_ref, pl.program_id(0) == 0, part)

    return pl.pallas_call(
        body, name="loss_grad", grid=(t // tm,),
        in_specs=[_rows(tm, D_MODEL), _rows(tm, D_MODEL)],
        out_specs=[_rows(tm, D_MODEL), _full((1, 128))],
        out_shape=[jax.ShapeDtypeStruct((t, D_MODEL), F32), jax.ShapeDtypeStruct((1, 128), F32)],
        compiler_params=_cp(("arbitrary",)),
    )(y, target)


def _dw2d(a, b, bm, bn):
    t, m = a.shape
    n = b.shape[1]
    tk = min(TK_DW, t)

    def body(a_ref, b_ref, o_ref):
        part = lax.dot_general(a_ref[...], b_ref[...], _CONTRACT_FIRST, preferred_element_type=F32)
        _acc_out(o_ref, pl.program_id(2) == 0, part)

    return pl.pallas_call(
        body, name="dw2d", grid=(m // bm, n // bn, t // tk),
        in_specs=[pl.BlockSpec((tk, bm), lambda i, j, k: (k, i)), pl.BlockSpec((tk, bn), lambda i, j, k: (k, j))],
        out_specs=pl.BlockSpec((bm, bn), lambda i, j, k: (i, j)),
        out_shape=jax.ShapeDtypeStruct((m, n), F32),
        compiler_params=_cp(("arbitrary", "arbitrary", "arbitrary")),
    )(a, b)


def _dw_cols(a, b, n_blk):
    t, m = a.shape
    s = b.shape[1] // n_blk
    tk = min(TK_DW, t)
    nk = t // tk

    def body(a_ref, b_ref, o_ref, ob_ref):
        part = lax.dot_general(a_ref[...], b_ref[...], _CONTRACT_FIRST, preferred_element_type=F32)
        _acc_out(o_ref.at[0], pl.program_id(1) == 0, part)

        @pl.when(pl.program_id(1) == nk - 1)
        def _():
            ob_ref[...] = o_ref[...].astype(BF16)

    spec = pl.BlockSpec((1, m, n_blk), lambda j, k: (j, 0, 0))
    return pl.pallas_call(
        body, name="dw_cols", grid=(s, nk),
        in_specs=[pl.BlockSpec((tk, m), lambda j, k: (k, 0)), pl.BlockSpec((tk, n_blk), lambda j, k: (k, j))],
        out_specs=[spec, spec],
        out_shape=[jax.ShapeDtypeStruct((s, m, n_blk), F32), jax.ShapeDtypeStruct((s, m, n_blk), BF16)],
        compiler_params=_cp(("arbitrary", "arbitrary")),
    )(a, b)


def _dw_up(h, dup, layer, buf):
    t, m = h.shape
    s, _, n = dup.shape
    tk = min(TK_DW, t)
    nk = t // tk

    def body(*refs):
        a_ref, b_ref, o_ref, ob_ref = refs[0], refs[1], refs[-2], refs[-1]
        part = lax.dot_general(a_ref[...], b_ref[0], _CONTRACT_FIRST, preferred_element_type=F32)
        _acc_out(o_ref.at[0, 0], pl.program_id(1) == 0, part)

        @pl.when(pl.program_id(1) == nk - 1)
        def _():
            ob_ref[...] = o_ref[...].astype(BF16)

    spec = pl.BlockSpec((1, 1, m, n), lambda j, k: (j, layer, 0, 0))
    return pl.pallas_call(
        body, name="dw_up", grid=(s, nk),
        in_specs=[pl.BlockSpec((tk, m), lambda j, k: (k, 0)), pl.BlockSpec((1, tk, n), lambda j, k: (j, k, 0))]
                 + ([] if buf is None else [_ANY, _ANY]),
        out_specs=[spec, spec],
        out_shape=[jax.ShapeDtypeStruct((s, 2, m, n), F32), jax.ShapeDtypeStruct((s, 2, m, n), BF16)],
        input_output_aliases={} if buf is None else {2: 0, 3: 1},
        compiler_params=_cp(("arbitrary", "arbitrary")),
    )(h, dup, *([] if buf is None else buf))


def _dw_down(act, df, layer, buf):
    nc, t, n = act.shape
    d = df.shape[1]
    tk = min(TK_DW, t)
    nk = t // tk

    def body(*refs):
        a_ref, b_ref, o_ref, ob_ref = refs[0], refs[1], refs[-2], refs[-1]
        part = lax.dot_general(a_ref[0], b_ref[...], _CONTRACT_FIRST, preferred_element_type=F32)
        part = part.reshape(2, n // 2, d)
        first = pl.program_id(1) == 0

        @pl.when(first)
        def _():
            o_ref[:, 0] = part

        @pl.when(jnp.logical_not(first))
        def _():
            o_ref[:, 0] += part

        @pl.when(pl.program_id(1) == nk - 1)
        def _():
            ob_ref[...] = o_ref[...].astype(BF16)

    spec = pl.BlockSpec((2, 1, n // 2, d), lambda c, k: (c, layer, 0, 0))
    return pl.pallas_call(
        body, name="dw_down", grid=(nc, nk),
        in_specs=[pl.BlockSpec((1, tk, n), lambda c, k: (c, k, 0)), pl.BlockSpec((tk, d), lambda c, k: (k, 0))]
                 + ([] if buf is None else [_ANY, _ANY]),
        out_specs=[spec, spec],
        out_shape=[jax.ShapeDtypeStruct((2 * nc, 2, n // 2, d), F32), jax.ShapeDtypeStruct((2 * nc, 2, n // 2, d), BF16)],
        input_output_aliases={} if buf is None else {2: 0, 3: 1},
        compiler_params=_cp(("arbitrary", "arbitrary")),
    )(act, df, *([] if buf is None else buf))


def _dz_wt_rms_bwd(dz, wt, x_in, gpre, dres):
    t, n = dz.shape
    tm = min(TM, t)

    def body(dz_ref, wt_ref, x_ref, g_ref, dres_ref, dx_ref, dg_ref):
        dh = jnp.dot(dz_ref[...], wt_ref[...], preferred_element_type=F32)
        dx, dg = _rms_bwd(x_ref[...], g_ref[...], dh)
        dx_ref[...] = dres_ref[...] + dx
        _acc_out(dg_ref, pl.program_id(0) == 0, dg)

    return pl.pallas_call(
        body, name="dz_wt_rms_bwd", grid=(t // tm,),
        in_specs=[_rows(tm, n), _full((n, D_MODEL)), _rows(tm, D_MODEL), _full((1, D_MODEL)), _rows(tm, D_MODEL)],
        out_specs=[_rows(tm, D_MODEL), _full((1, D_MODEL))],
        out_shape=[jax.ShapeDtypeStruct((t, D_MODEL), F32), jax.ShapeDtypeStruct((1, D_MODEL), F32)],
        compiler_params=_cp(("arbitrary",)),
    )(dz, wt, x_in, gpre, dres)


def _shift_matrices(shift, shift_h, tm, hb):
    row = lax.broadcasted_iota(jnp.int32, (2 * tm, tm), 0)
    col = lax.broadcasted_iota(jnp.int32, (2 * tm, tm), 1)
    hit = ((row < tm) & (col == row + 1)) | ((row >= tm) & (col == row - tm + 2))
    shift[...] = jnp.where(hit, 1.0, 0.0).astype(BF16)
    row = lax.broadcasted_iota(jnp.int32, (hb, hb), 0)
    col = lax.broadcasted_iota(jnp.int32, (hb, hb), 1)
    hit = ((row < HALO3) & (col == row - (HALO3 - 1))) | ((row >= HALO3) & (col == row - (2 * HALO3 - 2)))
    shift_h[...] = jnp.where(hit, 1.0, 0.0).astype(BF16)


def _next_rows(shift, shift_h, xb, nxt, d12_s, tm):
    d12_s[...] = jnp.dot(shift[...], xb, preferred_element_type=F32)
    edge = jnp.dot(shift_h[...], nxt, preferred_element_type=F32)
    d12_s[tm - HALO3:tm, :] += edge[0:HALO3, :]
    d12_s[2 * tm - HALO3:2 * tm, :] += edge[HALO3:2 * HALO3, :]


def _ffn_bwd(f, dxo, gpost, x_in, gpre, up, u, cw, wdt, wupt):
    t = f.shape[0]
    tm = min(TM_BWD, t)
    nt = t // tm
    nc, n = up.shape[1], up.shape[3]
    hb = 2 * HALO3

    def body(f_ref, dxo_ref, gpost_ref, x_ref, gpre_ref, up_ref, u_ref, cw_ref, wdt_ref, wupt_ref,
             df_ref, act_ref, dup_ref, dx_ref, dgpost_ref, dgpre_ref, dcw_ref, df_s, acc, du_s, dub_s, d12_s, hal, shift, shift_h):
        i, c = pl.program_id(0), pl.program_id(1)

        @pl.when(c == 0)
        def _():
            df, dg = _rms_bwd(f_ref[...], gpost_ref[...], dxo_ref[...])
            df_s[...] = df.astype(BF16)
            df_ref[...] = df.astype(BF16)
            _acc_out(dgpost_ref, i == 0, dg)

        @pl.when(i == 0)
        def _():
            hal[c] = jnp.zeros((2, hb, n), BF16)
            dcw_ref[0, c] = jnp.zeros((8, n), F32)
            dcw_ref[1, c] = jnp.zeros((8, n), F32)

        @pl.when(jnp.logical_and(i == 0, c == 0))
        def _():
            _shift_matrices(shift, shift_h, tm, hb)

        dact = jnp.dot(df_s[...], wdt_ref[0], preferred_element_type=F32)
        g, v = u_ref[0, 0].astype(F32), u_ref[1, 0].astype(F32)
        sg = _sigmoid(g)
        sil = g * sg
        act_ref[0] = (sil * v).astype(BF16)
        dug = dact * v * (sg + sil * (1.0 - sg))
        duv = dact * sil
        du_s[0], du_s[1] = dug, duv
        dub_s[0], dub_s[1] = dug.astype(BF16), duv.astype(BF16)
        dh = None
        for gv in range(2):
            _next_rows(shift, shift_h, dub_s[gv], hal[c, gv], d12_s, tm)
            hal[c, gv] = dub_s[gv, 0:hb, :]
            du, d1, d2 = du_s[gv], d12_s[0:tm, :], d12_s[tm:2 * tm, :]
            dup = (cw_ref[gv, 0, 2:3, :] * du + cw_ref[gv, 0, 1:2, :] * d1 + cw_ref[gv, 0, 0:1, :] * d2).astype(BF16)
            dup_ref[gv, 0] = dup
            upc = up_ref[gv, 0].astype(F32)
            dcw_ref[gv, c, 2:3, :] += jnp.sum(upc * du, axis=0, keepdims=True)
            dcw_ref[gv, c, 1:2, :] += jnp.sum(upc * d1, axis=0, keepdims=True)
            dcw_ref[gv, c, 0:1, :] += jnp.sum(upc * d2, axis=0, keepdims=True)
            part = jnp.dot(dup, wupt_ref[gv, 0], preferred_element_type=F32)
            dh = part if dh is None else dh + part
        _acc_out(acc, c == 0, dh)

        @pl.when(c == nc - 1)
        def _():
            dx, dg = _rms_bwd(x_ref[...], gpre_ref[...], acc[...])
            dx_ref[...] = dxo_ref[...] + dx
            _acc_out(dgpre_ref, i == 0, dg)

    rrow = lambda w: pl.BlockSpec((tm, w), lambda i, c: (nt - 1 - i, 0))
    blk = pl.BlockSpec((2, 1, tm, n), lambda i, c: (0, c, nt - 1 - i, 0))
    one = _full((1, D_MODEL))
    return pl.pallas_call(
        body, name="ffn_bwd", grid=(nt, nc),
        in_specs=[rrow(D_MODEL), rrow(D_MODEL), one, rrow(D_MODEL), one, blk, blk,
                  pl.BlockSpec((2, 1, 3, n), lambda i, c: (0, c, 0, 0)),
                  pl.BlockSpec((1, D_MODEL, n), lambda i, c: (c, 0, 0)),
                  pl.BlockSpec((2, 1, n, D_MODEL), lambda i, c: (0, c, 0, 0))],
        out_specs=[rrow(D_MODEL), pl.BlockSpec((1, tm, n), lambda i, c: (c, nt - 1 - i, 0)), blk, rrow(D_MODEL),
                   one, one, _full((2, nc, 8, n))],
        out_shape=[jax.ShapeDtypeStruct((t, D_MODEL), BF16), jax.ShapeDtypeStruct((nc, t, n), BF16),
                   jax.ShapeDtypeStruct((2, nc, t, n), BF16), jax.ShapeDtypeStruct((t, D_MODEL), F32),
                   jax.ShapeDtypeStruct((1, D_MODEL), F32), jax.ShapeDtypeStruct((1, D_MODEL), F32),
                   jax.ShapeDtypeStruct((2, nc, 8, n), F32)],
        scratch_shapes=[pltpu.VMEM((tm, D_MODEL), BF16), pltpu.VMEM((tm, D_MODEL), F32),
                        pltpu.VMEM((2, tm, n), F32), pltpu.VMEM((2, tm, n), BF16), pltpu.VMEM((2 * tm, n), F32),
                        pltpu.VMEM((nc, 2, hb, n), BF16), pltpu.VMEM((2 * tm, tm), BF16), pltpu.VMEM((hb, hb), BF16)],
        compiler_params=_cp(("arbitrary", "arbitrary")),
    )(f, dxo, gpost, x_in, gpre, up, u, cw, wdt, wupt)


def _od_bwd(m, dxo, gpost, x_in, gpre, z, cv, cw, wot, wint):
    t = m.shape[0]
    tm = min(TM_BWD, t)
    nt = t // tm
    hb = 2 * HALO3

    def body(m_ref, dxo_ref, gpost_ref, x_ref, gpre_ref, z_ref, cv_ref, cw_ref, wot_ref, wint_ref,
             dm_ref, dz_ref, dx_ref, dgpost_ref, dgpre_ref, dcw_ref, dcvb_s, d12_s, dz_s, hal, shift, shift_h):
        i = pl.program_id(0)
        dxo = dxo_ref[...]
        dm, dg = _rms_bwd(m_ref[...], gpost_ref[...], dxo)
        dmb = dm.astype(BF16)
        dm_ref[...] = dmb
        _acc_out(dgpost_ref, i == 0, dg)

        @pl.when(i == 0)
        def _():
            hal[...] = jnp.zeros((hb, SC_DIM), BF16)
            dcw_ref[...] = jnp.zeros((8, SC_DIM), F32)
            _shift_matrices(shift, shift_h, tm, hb)

        dy = jnp.dot(dmb, wot_ref[...], preferred_element_type=F32)
        z = z_ref[...].astype(F32)
        b, cg, u = z[:, :SC_DIM], z[:, SC_DIM:2 * SC_DIM], z[:, 2 * SC_DIM:]
        dz_s[:, 0:SC_DIM] = (dy * cv_ref[...].astype(F32)).astype(BF16)
        dcv = dy * b
        dcvb_s[...] = dcv.astype(BF16)
        _next_rows(shift, shift_h, dcvb_s[...], hal[...], d12_s, tm)
        hal[...] = dcvb_s[0:hb, :]
        d1, d2 = d12_s[0:tm, :], d12_s[tm:2 * tm, :]
        dcu = cw_ref[2:3, :] * dcv + cw_ref[1:2, :] * d1 + cw_ref[0:1, :] * d2
        cu = cg * u
        dcw_ref[2:3, :] += jnp.sum(cu * dcv, axis=0, keepdims=True)
        dcw_ref[1:2, :] += jnp.sum(cu * d1, axis=0, keepdims=True)
        dcw_ref[0:1, :] += jnp.sum(cu * d2, axis=0, keepdims=True)
        dz_s[:, SC_DIM:2 * SC_DIM] = (dcu * u).astype(BF16)
        dz_s[:, 2 * SC_DIM:3 * SC_DIM] = (dcu * cg).astype(BF16)
        dz_ref[...] = dz_s[...]
        dh = jnp.dot(dz_s[...], wint_ref[...], preferred_element_type=F32)
        dx, dg2 = _rms_bwd(x_ref[...], gpre_ref[...], dh)
        dx_ref[...] = dxo + dx
        _acc_out(dgpre_ref, i == 0, dg2)

    rrow = lambda w: pl.BlockSpec((tm, w), lambda i: (nt - 1 - i, 0))
    one = _full((1, D_MODEL))
    return pl.pallas_call(
        body, name="od_bwd", grid=(nt,),
        in_specs=[rrow(D_MODEL), rrow(D_MODEL), one, rrow(D_MODEL), one, rrow(3 * SC_DIM), rrow(SC_DIM),
                  _full((3, SC_DIM)), _full((D_MODEL, SC_DIM)), _full((3 * SC_DIM, D_MODEL))],
        out_specs=[rrow(D_MODEL), rrow(3 * SC_DIM), rrow(D_MODEL), one, one, _full((8, SC_DIM))],
        out_shape=[jax.ShapeDtypeStruct((t, D_MODEL), BF16), jax.ShapeDtypeStruct((t, 3 * SC_DIM), BF16),
                   jax.ShapeDtypeStruct((t, D_MODEL), F32), jax.ShapeDtypeStruct((1, D_MODEL), F32),
                   jax.ShapeDtypeStruct((1, D_MODEL), F32), jax.ShapeDtypeStruct((8, SC_DIM), F32)],
        scratch_shapes=[pltpu.VMEM((tm, SC_DIM), BF16), pltpu.VMEM((2 * tm, SC_DIM), F32),
                        pltpu.VMEM((tm, 3 * SC_DIM), BF16), pltpu.VMEM((hb, SC_DIM), BF16),
                        pltpu.VMEM((2 * tm, tm), BF16), pltpu.VMEM((hb, hb), BF16)],
        compiler_params=_cp(("arbitrary",)),
    )(m, dxo, gpost, x_in, gpre, z, cv, cw, wot, wint)


def _ev_bwd1(m, dxo, gpost, wot):
    t = m.shape[0]
    tm = min(TM, t)

    def body(m_ref, dxo_ref, g_ref, wot_ref, dm_ref, da_ref, do_ref, dg_ref):
        dm, dg = _rms_bwd(m_ref[...], g_ref[...], dxo_ref[...])
        dmb = dm.astype(BF16)
        dm_ref[...] = dmb
        _acc_out(dg_ref, pl.program_id(0) == 0, dg)
        dao = jnp.dot(dmb, wot_ref[...], preferred_element_type=F32)
        da_ref[...] = dao[:, :A_CH]
        do_ref[...] = dao[:, A_CH:].astype(BF16)

    return pl.pallas_call(
        body, name="ev_bwd1", grid=(t // tm,),
        in_specs=[_rows(tm, D_MODEL), _rows(tm, D_MODEL), _full((1, D_MODEL)), _full((D_MODEL, A_CH + Q_DIM))],
        out_specs=[_rows(tm, D_MODEL), _rows(tm, A_CH), _rows(tm, Q_DIM), _full((1, D_MODEL))],
        out_shape=[jax.ShapeDtypeStruct((t, D_MODEL), BF16), jax.ShapeDtypeStruct((t, A_CH), F32),
                   jax.ShapeDtypeStruct((t, Q_DIM), BF16), jax.ShapeDtypeStruct((1, D_MODEL), F32)],
        compiler_params=_cp(("arbitrary",)),
    )(m, dxo, gpost, wot)


def _conf_bwd(da, cv, zag, conv_w, ln_g, ln_b):
    t = da.shape[0]
    tm = min(TM_BWD, t)
    nt = t // tm

    def body(da_ref, c_ref, z_ref, zh_ref, w_ref, g_ref, lb_ref, dz_ref, dw_ref, dv_ref, ext_in, ext_out):
        i = pl.program_id(0)
        r = nt - 1 - i

        @pl.when(i == 0)
        def _():
            ext_out[tm:tm + HALO31, :] = jnp.zeros((HALO31, A_CH), F32)
            dw_ref[...] = jnp.zeros((32, A_CH), F32)
            dv_ref[...] = jnp.zeros((8, A_CH), F32)

        x = c_ref[...]
        mu = jnp.mean(x, axis=-1, keepdims=True)
        xc = x - mu
        rstd = lax.rsqrt(jnp.mean(xc * xc, axis=-1, keepdims=True) + LN_EPS)
        xh = xc * rstd
        ln = xh * g_ref[...] + lb_ref[...]
        sl = jax.nn.sigmoid(ln)
        dln = da_ref[...] * (sl * (1.0 + ln * (1.0 - sl)))
        dxh = dln * g_ref[...]
        dc = rstd * (dxh - jnp.mean(dxh, axis=-1, keepdims=True) - xh * jnp.mean(dxh * xh, axis=-1, keepdims=True))
        dv_ref[0:1, :] += jnp.sum(dc, axis=0, keepdims=True)
        dv_ref[1:2, :] += jnp.sum(dln * xh, axis=0, keepdims=True)
        dv_ref[2:3, :] += jnp.sum(dln, axis=0, keepdims=True)

        ext_out[0:tm, :] = dc
        dglu = jnp.zeros((tm, A_CH), F32)
        for j in range(A_CONV):
            s = A_CONV - 1 - j
            dglu = dglu + w_ref[j:j + 1, :] * ext_out[s:s + tm, :]
        ext_out[tm:tm + HALO31, :] = ext_out[0:HALO31, :]

        ext_in[0:HALO31, :] = jnp.where(r > 0, _glu(zh_ref[...]), 0.0)
        z = z_ref[...].astype(F32)
        al, sg = z[:, :A_CH], jax.nn.sigmoid(z[:, A_CH:])
        ext_in[HALO31:HALO31 + tm, :] = al * sg
        for j in range(A_CONV):
            s = HALO31 - (A_CONV - 1) + j
            dw_ref[j:j + 1, :] += jnp.sum(dc * ext_in[s:s + tm, :], axis=0, keepdims=True)
        dz_ref[:, 0:A_CH] = (dglu * sg).astype(BF16)
        dz_ref[:, A_CH:2 * A_CH] = (dglu * al * sg * (1.0 - sg)).astype(BF16)

    rrow = lambda w: pl.BlockSpec((tm, w), lambda i: (nt - 1 - i, 0))
    halo = pl.BlockSpec((HALO31, 2 * A_CH), lambda i: (jnp.maximum((nt - 1 - i) * (tm // HALO31) - 1, 0), 0))
    return pl.pallas_call(
        body, name="conf_bwd", grid=(nt,),
        in_specs=[rrow(A_CH), rrow(A_CH), rrow(2 * A_CH), halo, _full((32, A_CH)), _full((1, A_CH)),
                  _full((1, A_CH))],
        out_specs=[rrow(2 * A_CH), _full((32, A_CH)), _full((8, A_CH))],
        out_shape=[jax.ShapeDtypeStruct((t, 2 * A_CH), BF16), jax.ShapeDtypeStruct((32, A_CH), F32),
                   jax.ShapeDtypeStruct((8, A_CH), F32)],
        scratch_shapes=[pltpu.VMEM((HALO31 + tm, A_CH), F32), pltpu.VMEM((tm + HALO31, A_CH), F32)],
        compiler_params=_cp(("arbitrary",)),
    )(da, cv, zag, zag, conv_w, ln_g, ln_b)


def _attn_bwd(q, k, v, do, sinks):
    t = q.shape[0]
    nb = min(ATT_NB, t // BLOCK)
    rows = nb * BLOCK
    ns = t // rows

    def body(s_ref, q_ref, kc_ref, kp_ref, vc_ref, vp_ref, do_ref, dq_ref, dk_ref, dv_ref, ds_ref, dkc, dvc):
        i = pl.program_id(0)
        r = ns - 1 - i

        @pl.when(i == 0)
        def _():
            dkc[...] = jnp.zeros_like(dkc)
            dvc[...] = jnp.zeros_like(dvc)
            ds_ref[...] = jnp.zeros_like(ds_ref)

        lane = lax.broadcasted_iota(jnp.int32, (1, N_Q_HEADS), 1)
        dsv = jnp.zeros((1, N_Q_HEADS), F32)
        for b in range(nb - 1, -1, -1):
            lo = BLOCK * b
            mask = _attn_mask(r == 0) if b == 0 else _attn_mask(False)
            qv, dov = q_ref[lo:lo + BLOCK, :], do_ref[lo:lo + BLOCK, :]
            kc, vc = kc_ref[lo:lo + BLOCK, :], vc_ref[lo:lo + BLOCK, :]
            kp = kp_ref[...] if b == 0 else kc_ref[lo - BLOCK:lo, :]
            vp = vp_ref[...] if b == 0 else vc_ref[lo - BLOCK:lo, :]
            for h in range(N_KV_HEADS):
                q4, do4 = _q_heads(qv, h), _q_heads(dov, h)
                k2, v2 = _kv_head(kp, kc, h), _kv_head(vp, vc, h)
                pn, ps = _attn_probs(q4, k2, mask, _sink_rows(s_ref, h))
                dp = lax.dot_general(do4, v2, _CONTRACT_LAST, preferred_element_type=F32)
                dl = jnp.sum(pn * dp, axis=-1, keepdims=True)
                dsb = (pn * (dp - dl)).astype(BF16)
                dq4 = (jnp.dot(dsb, k2, preferred_element_type=F32) * SCALE).astype(BF16)
                for g in range(GROUP):
                    c0 = HEAD_DIM * (GROUP * h + g)
                    dq_ref[lo:lo + BLOCK, c0:c0 + HEAD_DIM] = dq4[BLOCK * g:BLOCK * (g + 1), :]
                dk2 = lax.dot_general(dsb, q4, _CONTRACT_FIRST, preferred_element_type=F32) * SCALE
                dv2 = lax.dot_general(pn.astype(BF16), do4, _CONTRACT_FIRST, preferred_element_type=F32)
                dk_ref[lo:lo + BLOCK, HEAD_DIM * h:HEAD_DIM * (h + 1)] = dk2[BLOCK:, :] + dkc[h]
                dv_ref[lo:lo + BLOCK, HEAD_DIM * h:HEAD_DIM * (h + 1)] = dv2[BLOCK:, :] + dvc[h]
                dkc[h] = dk2[:BLOCK, :]
                dvc[h] = dv2[:BLOCK, :]
                srow = -ps * dl
                for g in range(GROUP):
                    dsv = dsv + jnp.where(lane == GROUP * h + g, jnp.sum(srow[BLOCK * g:BLOCK * (g + 1), :]), 0.0)
        ds_ref[...] += dsv

    cur = lambda n: pl.BlockSpec((rows, n), lambda i: (ns - 1 - i, 0))
    prev = lambda n: pl.BlockSpec((BLOCK, n), lambda i: (jnp.maximum((ns - 1 - i) * nb - 1, 0), 0))
    return pl.pallas_call(
        body, name="attn_bwd", grid=(ns,),
        in_specs=[pl.BlockSpec(memory_space=pltpu.SMEM), cur(Q_DIM), cur(KV_DIM), prev(KV_DIM), cur(KV_DIM),
                  prev(KV_DIM), cur(Q_DIM)],
        out_specs=[cur(Q_DIM), cur(KV_DIM), cur(KV_DIM), _full((1, N_Q_HEADS))],
        out_shape=[jax.ShapeDtypeStruct((t, Q_DIM), BF16), jax.ShapeDtypeStruct((t, KV_DIM), F32),
                   jax.ShapeDtypeStruct((t, KV_DIM), F32), jax.ShapeDtypeStruct((1, N_Q_HEADS), F32)],
        scratch_shapes=[pltpu.VMEM((N_KV_HEADS, BLOCK, HEAD_DIM), F32), pltpu.VMEM((N_KV_HEADS, BLOCK, HEAD_DIM), F32)],
        compiler_params=_cp(("arbitrary",)),
    )(sinks, q, k, k, v, v, do)


def _ev_dz(dzag, dq, dk, dv, rc, rsa, rsb):
    t = dzag.shape[0]
    tm = min(TM, t)

    def body(dzag_ref, dq_ref, dk_ref, dv_ref, c_ref, sa_ref, sb_ref, dz_ref):
        c, sa, sb = c_ref[...], sa_ref[...], sb_ref[...]
        dz_ref[:, 0:2 * A_CH] = dzag_ref[...]
        q0 = 2 * A_CH
        for j in range(Q_DIM // 128):
            d = dq_ref[:, 128 * j:128 * (j + 1)].astype(F32)
            dz_ref[:, q0 + 128 * j:q0 + 128 * (j + 1)] = _rope_bwd(d, c, sa, sb).astype(BF16)
        k0 = q0 + Q_DIM
        dz_ref[:, k0:k0 + KV_DIM] = _rope_bwd(dk_ref[...], c, sa, sb).astype(BF16)
        dz_ref[:, k0 + KV_DIM:k0 + 2 * KV_DIM] = dv_ref[...].astype(BF16)

    return pl.pallas_call(
        body, name="ev_dz", grid=(t // tm,),
        in_specs=[_rows(tm, 2 * A_CH), _rows(tm, Q_DIM), _rows(tm, KV_DIM), _rows(tm, KV_DIM),
                  _rows(tm, 128), _rows(tm, 128), _rows(tm, 128)],
        out_specs=_rows(tm, EVEN_IN),
        out_shape=jax.ShapeDtypeStruct((t, EVEN_IN), BF16),
        compiler_params=_cp(("arbitrary",)),
    )(dzag, dq, dk, dv, rc, rsa, rsb)


def _prep_ev(gat):
    p = {}
    w = gat["ev_w_in"][:, 0].transpose(1, 0, 2).reshape(D_MODEL, EVEN_IN)
    p["ev_w_in"], p["ev_w_in_t"] = w, w.T
    w = gat["ev_w_out"].reshape(A_CH + Q_DIM, D_MODEL)
    p["ev_w_out"], p["ev_w_out_t"] = w, w.T
    return p


def _prep_rest(gat):
    p = {}
    g = gat["od_w_in"][:, 0]
    p["od_w_in"], p["od_w_in_t"] = g, g.transpose(0, 2, 1).reshape(3 * SC_DIM, D_MODEL)
    w = gat["od_w_out"].reshape(SC_DIM, D_MODEL)
    p["od_w_out"], p["od_w_out_t"] = w, w.T
    g = gat["ffn_w_up"]
    p["ffn_w_up"] = g.reshape(2, N_DEV // 2, 2, D_MODEL, FF_N)
    p["ffn_w_up_t"] = [g[:, i].transpose(0, 2, 1).reshape(2, N_DEV // 2, FF_N, D_MODEL) for i in range(2)]
    g = gat["ffn_w_down"]
    p["ffn_w_down"] = [g[:, i].reshape(D_FF, D_MODEL) for i in range(2)]
    p["ffn_w_down_t"] = [w.reshape(N_DEV // 2, FF_N, D_MODEL).transpose(0, 2, 1) for w in p["ffn_w_down"]]
    return p


def _local_step(x, positions, target, p, rest_weights, s, token, grads_ready):
    row = lambda a, tok=None: a.reshape(1, -1) if tok is None else a.reshape(1, -1) + tok
    nc = N_DEV // 2
    rc, rsa, rsb = _rope_tables(positions)
    conv31 = jnp.pad(s["ev_a_conv_w"][0], ((0, 1), (0, 0)))
    cw_ffn = [s["ffn_conv_w"][i].reshape(3, 2, nc, FF_N).transpose(1, 2, 0, 3) for i in range(2)]
    sinks = s["ev_sinks"][0]
    big, g = {}, {}

    h0, zag, q, k, v = _ev_in(x, row(s["mix_norm_pre"][0], token), p["ev_w_in"], rc, rsa, rsb)
    cv, a = _conf_fwd(zag, conv31, s["ev_a_conv_b"], s["ev_a_ln_g"], s["ev_a_ln_b"])
    o = _attn_fwd(q, k, v, sinks)
    wo = p["ev_w_out"]
    m0, x1 = _out_post([a, o], [wo[:A_CH], wo[A_CH:]], x, row(s["mix_norm_post"][0]))
    p = {**p, **rest_weights(m0)}
    h1, up0, u0, f0, x2 = _ffn_fwd(x1, row(s["ffn_norm_pre"][0]), p["ffn_w_up"], 0, cw_ffn[0], p["ffn_w_down"][0],
                                   row(s["ffn_norm_post"][0]))
    h2, z, cv1, y, m1, x3 = _od_fwd(x2, row(s["mix_norm_pre"][1]), p["od_w_in"], s["od_conv_w"][0], p["od_w_out"],
                                    row(s["mix_norm_post"][1]))
    h3, up1, u1, f1, x4 = _ffn_fwd(x3, row(s["ffn_norm_pre"][1]), p["ffn_w_up"], 1, cw_ffn[1], p["ffn_w_down"][1],
                                   row(s["ffn_norm_post"][1]))
    dx, lpart = _loss_grad(x4, target)

    def ffn_back(i, f, dxo, up, u, h, x_in, bufs, tok=None):
        df, act, dup, dx_in, dgpost, dgpre, dcw = _ffn_bwd(
            f, dxo, row(s["ffn_norm_post"][i], tok), x_in, row(s["ffn_norm_pre"][i]), up, u, cw_ffn[i],
            p["ffn_w_down_t"][i], p["ffn_w_up_t"][i])
        bufs = (_dw_up(h, dup.reshape(N_DEV, -1, FF_N), i, bufs[0]), _dw_down(act, df, i, bufs[1]))
        return dx_in, dgpost, dgpre, dcw[:, :, 0:3].transpose(2, 0, 1, 3).reshape(3, 2 * D_FF), bufs

    dx, dgfpost1, dgfpre1, dcw1, bufs = ffn_back(1, f1, dx, up1, u1, h3, x3, (None, None))

    dm1, dz, dx, dgpost1, dgpre1, dcw_od = _od_bwd(m1, dx, row(s["mix_norm_post"][1]), x2, row(s["mix_norm_pre"][1]), z,
                                                   cv1, s["od_conv_w"][0], p["od_w_out_t"], p["od_w_in_t"])
    big["od_w_out"] = _dw2d(y, dm1, SC_DIM, D_MODEL).reshape(N_DEV, -1, D_MODEL)
    big["od_w_in"], big["od_w_in:bf16"] = _dw_cols(h2, dz, 3 * SC_DIM // N_DEV)
    g["od_conv_w"] = dcw_od[None, 0:3]
    tok = grads_ready(["od_w_in", "od_w_out"], big)

    dx, dgfpost0, dgfpre0, dcw0, bufs = ffn_back(0, f0, dx, up0, u0, h1, x1, bufs, tok)
    (big["ffn_w_up"], big["ffn_w_up:bf16"]), (big["ffn_w_down"], big["ffn_w_down:bf16"]) = bufs
    tok = grads_ready(["ffn_w_up", "ffn_w_down"], big)

    dm0, da, do, dgpost0 = _ev_bwd1(m0, dx, row(s["mix_norm_post"][0], tok), p["ev_w_out_t"])
    big["ev_w_out"] = jnp.concatenate([_dw2d(a, dm0, A_CH, D_MODEL), _dw2d(o, dm0, Q_DIM, D_MODEL)],
                                      axis=0).reshape(N_DEV, -1, D_MODEL)
    dzag, dcw31, dvec = _conf_bwd(da, cv, zag, conv31, s["ev_a_ln_g"], s["ev_a_ln_b"])
    dq, dk, dv, dsinks = _attn_bwd(q, k, v, do, sinks)
    dz0 = _ev_dz(dzag, dq, dk, dv, rc, rsa, rsb)
    dw_in = _dw2d(h0, dz0, D_MODEL, EVEN_IN // 2)
    big["ev_w_in"] = dw_in.reshape(D_MODEL, N_DEV, EVEN_IN // N_DEV).transpose(1, 0, 2)
    dx, dgpre0 = _dz_wt_rms_bwd(dz0, p["ev_w_in_t"], x, row(s["mix_norm_pre"][0]), dx)
    grads_ready(["ev_w_in", "ev_w_out"], big)

    g["mix_norm_pre"] = jnp.concatenate([dgpre0, dgpre1], axis=0)
    g["mix_norm_post"] = jnp.concatenate([dgpost0, dgpost1], axis=0)
    g["ffn_norm_pre"] = jnp.concatenate([dgfpre0, dgfpre1], axis=0)
    g["ffn_norm_post"] = jnp.concatenate([dgfpost0, dgfpost1], axis=0)
    g["ev_a_conv_w"] = dcw31[None, 0:A_CONV]
    g["ev_a_conv_b"], g["ev_a_ln_g"], g["ev_a_ln_b"] = dvec[0:1], dvec[1:2], dvec[2:3]
    g["ev_sinks"] = dsinks
    g["ffn_conv_w"] = jnp.stack([dcw0, dcw1])
    return lpart[0, 0], dx, big, g


MESH = pl.DeviceIdType.MESH


def _all_gather(shards, name):
    nw = len(shards)

    def body(*refs):
        x_refs, out_refs = refs[:nw], refs[nw:2 * nw]
        send_sems, recv_sems, local_sems = refs[2 * nw:]
        x, y, c = lax.axis_index("x"), lax.axis_index("y"), lax.axis_index("c")
        me, sibling = (x, y, c), (x, y, 1 - c)
        chips = [(1 - x, y), (x, 1 - y), (1 - x, 1 - y)]

        def rows(w, px, py, pc):
            m_per = shards[w].shape[0]
            return out_refs[w].at[pl.ds((4 * px + 2 * py + pc) * m_per, m_per), :]

        def copy(w, k, block, to, src=None):
            return pltpu.make_async_remote_copy(
                src_ref=rows(w, *block) if src is None else src, dst_ref=rows(w, *block),
                send_sem=send_sems.at[w, k], recv_sem=recv_sems.at[w, k], device_id=to, device_id_type=MESH)

        mine, first, passed = [], [], []
        for w in range(nw):
            cp = pltpu.make_async_copy(x_refs[w], rows(w, *me), local_sems.at[w])
            cp.start()
            mine.append(cp)
            first.append([copy(w, 0, me, sibling, src=x_refs[w])]
                         + [copy(w, 1 + j, me, (*chip, c), src=x_refs[w]) for j, chip in enumerate(chips)])
            for cp in first[w]:
                cp.start()
        for w in range(nw):
            passed.append([copy(w, 4 + j, (*chip, c), sibling) for j, chip in enumerate(chips)])
            for j, chip in enumerate(chips):
                copy(w, 1 + j, (*chip, c), me).wait_recv()
                passed[w][j].start()
        for w in range(nw):
            copy(w, 0, sibling, me).wait_recv()
            for j, chip in enumerate(chips):
                copy(w, 4 + j, (*chip, 1 - c), me).wait_recv()
            for cp in first[w] + passed[w]:
                cp.wait_send()
            mine[w].wait()

    return pl.pallas_call(
        body, name=name,
        out_shape=[jax.ShapeDtypeStruct((N_DEV * a.shape[0], a.shape[1]), a.dtype) for a in shards],
        in_specs=[_ANY] * nw, out_specs=[_ANY] * nw,
        scratch_shapes=[pltpu.SemaphoreType.DMA((nw, 7)), pltpu.SemaphoreType.DMA((nw, 7)),
                        pltpu.SemaphoreType.DMA((nw,))],
    )(*shards)


_HBM = pl.BlockSpec(memory_space=pltpu.HBM)
_SEM = pl.BlockSpec(memory_space=pltpu.SEMAPHORE)
_EFFECT = pltpu.SideEffectType.DATAFLOW_SIDE_EFFECTING
_RELATIONS = [(dx, dy, dc) for dx in (0, 1) for dy in (0, 1) for dc in (0, 1)][1:]


def _peer(rel):
    x, y, c = lax.axis_index("x"), lax.axis_index("y"), lax.axis_index("c")
    px, py, pc = x ^ rel[0], y ^ rel[1], c ^ rel[2]
    return (px, py, pc), 4 * px + 2 * py + pc, 4 * x + 2 * y + c


def _exchange_copy(k, rel, src_ref, land_ref, send_sems, recv_sems, w, scatter):
    peer, peer_idx, my_idx = _peer(rel)
    src = src_ref.at[peer_idx] if scatter else src_ref
    return pltpu.make_async_remote_copy(
        src_ref=src, dst_ref=land_ref.at[my_idx], send_sem=send_sems.at[_sends(scatter) * w + k],
        recv_sem=recv_sems.at[7 * w + k], device_id=peer, device_id_type=MESH)


def _sends(scatter):
    return 7 if scatter else 8


def _own_copy(src_ref, land_ref, send_sems, w):
    my_idx = _peer(_RELATIONS[0])[2]
    return pltpu.make_async_copy(src_ref, land_ref.at[my_idx], send_sems.at[8 * w + 7])


def _exchange_start(srcs, scatter, name):
    nw = len(srcs)
    lands = [lax.empty((N_DEV,) + (a.shape[1:] if scatter else a.shape), a.dtype) for a in srcs]

    def body(*refs):
        src_refs, land_refs = refs[:nw], refs[nw:2 * nw]
        send_sems, recv_sems = refs[2 * nw], refs[2 * nw + 1]
        token = refs[-1]
        for w in range(nw):
            for k, rel in enumerate(_RELATIONS):
                _exchange_copy(k, rel, src_refs[w], land_refs[w], send_sems, recv_sems, w, scatter).start()
            if not scatter:
                _own_copy(src_refs[w], land_refs[w], send_sems, w).start()
        token[...] = jnp.zeros_like(token)

    hbm = lambda a: pltpu.HBM(a.shape, a.dtype)
    outs = pl.pallas_call(
        body, name=name,
        out_shape=(pltpu.SemaphoreType.DMA((_sends(scatter) * nw,)), pltpu.SemaphoreType.DMA((7 * nw,)),
                   *[hbm(a) for a in srcs],
                   *[hbm(a) for a in lands], jax.ShapeDtypeStruct((8, 128), F32)),
        in_specs=[_HBM] * (2 * nw),
        out_specs=(_SEM, _SEM, *[_HBM] * (2 * nw), pl.BlockSpec(memory_space=pltpu.VMEM)),
        input_output_aliases={i: 2 + i for i in range(2 * nw)},
        compiler_params=pltpu.CompilerParams(has_side_effects=_EFFECT),
    )(*[pltpu.with_memory_space_constraint(a, pltpu.HBM) for a in srcs],
      *[pltpu.with_memory_space_constraint(a, pltpu.HBM) for a in lands])
    return outs[0], outs[1], list(outs[2:2 + nw]), list(outs[2 + nw:2 + 2 * nw]), outs[-1]


def _exchange_wait(started, scatter, after, name):
    send_sems, recv_sems, srcs, lands, _ = started
    nw = len(srcs)

    def body(*refs):
        src_refs, land_refs = refs[:nw], refs[nw:2 * nw]
        send_s, recv_s = refs[2 * nw], refs[2 * nw + 1]
        for w in range(nw):
            for k, rel in enumerate(_RELATIONS):
                cp = _exchange_copy(k, rel, src_refs[w], land_refs[w], send_s, recv_s, w, scatter)
                cp.wait_send()
                _, peer_idx, _ = _peer(rel)
                pltpu.make_async_remote_copy(
                    src_ref=src_refs[w].at[peer_idx] if scatter else src_refs[w], dst_ref=land_refs[w].at[peer_idx],
                    send_sem=send_s.at[_sends(scatter) * w + k], recv_sem=recv_s.at[7 * w + k],
                    device_id=_peer(rel)[0], device_id_type=MESH).wait_recv()
            if not scatter:
                _own_copy(src_refs[w], land_refs[w], send_s, w).wait()

    hbm = lambda a: pltpu.HBM(a.shape, a.dtype)
    outs = pl.pallas_call(
        body, name=name, out_shape=tuple(hbm(a) for a in srcs + lands),
        in_specs=[_HBM] * (2 * nw) + [_SEM, _SEM, _ANY], out_specs=tuple([_HBM] * (2 * nw)),
        input_output_aliases={i: i for i in range(2 * nw)},
        compiler_params=pltpu.CompilerParams(has_side_effects=_EFFECT),
    )(*srcs, *lands, send_sems, recv_sems, after)
    return list(outs[nw:])


def _to_bf16(a):
    _, r, l = a.shape
    tr = _row_tile(r, 512)

    def body(a_ref, o_ref):
        o_ref[...] = a_ref[...].astype(BF16)

    spec = pl.BlockSpec((1, tr, l), lambda j, i: (j, i, 0))
    return pl.pallas_call(
        body, name="to_bf16", grid=(N_DEV, r // tr), in_specs=[spec], out_specs=spec,
        out_shape=jax.ShapeDtypeStruct(a.shape, BF16), compiler_params=_cp(("arbitrary", "arbitrary")),
    )(a)


def _row_tile(rows, cap):
    best = None
    for d in range(16, min(rows, cap) + 1, 16):
        if rows % d == 0:
            best = d
    return rows if best is None else best


def _adam_math(w, g, m, v):
    bc1 = 1.0 - ADAM_B1 ** ADAM_STEP
    bc2 = 1.0 - ADAM_B2 ** ADAM_STEP
    mn = ADAM_B1 * m + (1.0 - ADAM_B1) * g
    vn = ADAM_B2 * v + (1.0 - ADAM_B2) * (g * g)
    return -ADAM_LR * ((mn / bc1) / (jnp.sqrt(vn / bc2) + ADAM_EPS) + ADAM_WD * w), mn, vn


def _adamw_rs(gp, land, w, m, v, dev):
    _, r, l = gp.shape
    tr = _row_tile(r, 256)

    def body(i_ref, g_ref, b_ref, w_ref, m_ref, v_ref, go_ref, d_ref, mo_ref, vo_ref):
        g = g_ref[0]
        for j in range(N_DEV):
            g = g + jnp.where(i_ref[0] == j, 0.0, b_ref[j].astype(F32))
        go_ref[...] = g
        d_ref[...], mo_ref[...], vo_ref[...] = _adam_math(w_ref[...], g, m_ref[...], v_ref[...])

    spec = pl.BlockSpec((tr, l), lambda i, s: (i, 0))
    return pl.pallas_call(
        body, name="adamw_rs", out_shape=[jax.ShapeDtypeStruct((r, l), F32)] * 4,
        grid_spec=pltpu.PrefetchScalarGridSpec(
            num_scalar_prefetch=1, grid=(r // tr,),
            in_specs=[pl.BlockSpec((1, tr, l), lambda i, s: (s[0], i, 0)),
                      pl.BlockSpec((N_DEV, tr, l), lambda i, s: (0, i, 0)), spec, spec, spec],
            out_specs=[spec] * 4),
        compiler_params=_cp(("arbitrary",)),
    )(dev, gp, land, w, m, v)


def _sum_blocks(a, nblk):
    m = a.shape[0] // nblk
    n = a.shape[1]

    def body(a_ref, o_ref):
        acc = a_ref[0]
        for j in range(1, nblk):
            acc = acc + a_ref[j]
        o_ref[...] = acc

    return pl.pallas_call(
        body, name="sum_blocks", out_shape=jax.ShapeDtypeStruct((m, n), a.dtype),
        in_specs=[_full((nblk, m, n))], out_specs=_full((m, n)),
    )(a.reshape(nblk, m, n))


def _adamw(w, g, m, v):
    rows, c = w.shape

    def body(w_ref, g_ref, m_ref, v_ref, d_ref, mo_ref, vo_ref):
        d_ref[...], mo_ref[...], vo_ref[...] = _adam_math(w_ref[...], g_ref[...], m_ref[...], v_ref[...])

    return pl.pallas_call(
        body, name="adamw", in_specs=[_full((rows, c))] * 4, out_specs=[_full((rows, c))] * 3,
        out_shape=[jax.ShapeDtypeStruct((rows, c), F32)] * 3,
    )(w, g, m, v)


WEIGHTS = ["mix_norm_pre", "mix_norm_post", "ffn_norm_pre", "ffn_norm_post", "ev_w_in", "ev_a_conv_w", "ev_a_conv_b",
           "ev_a_ln_g", "ev_a_ln_b", "ev_sinks", "ev_w_out", "od_w_in", "od_conv_w", "od_w_out", "ffn_w_up",
           "ffn_conv_w", "ffn_w_down"]
BIG = ["ev_w_in", "ev_w_out", "od_w_in", "od_w_out", "ffn_w_up", "ffn_w_down"]
SMALL_REPL = ["mix_norm_pre", "mix_norm_post", "ffn_norm_pre", "ffn_norm_post", "ev_a_conv_b", "ev_a_ln_g",
              "ev_a_ln_b", "ev_sinks"]
SMALL_SHARDED = ["ev_a_conv_w", "od_conv_w", "ffn_conv_w"]


def _pack(arrs, rows):
    flat = jnp.concatenate([a.reshape(-1) for a in arrs])
    return jnp.pad(flat, (0, rows * LANES - flat.shape[0])).reshape(rows, LANES)


def _unpack(packed, shapes):
    flat, out, off = packed.reshape(-1), [], 0
    for s in shapes:
        n = 1
        for d in s:
            n *= d
        out.append(flat[off:off + n].reshape(s))
        off += n
    return out


def kernel(x, positions, mix_norm_pre, mix_norm_post, ffn_norm_pre, ffn_norm_post, ev_w_in, ev_a_conv_w, ev_a_conv_b, ev_a_ln_g, ev_a_ln_b, ev_sinks, ev_w_out, od_w_in, od_conv_w, od_w_out, ffn_w_up, ffn_conv_w, ffn_w_down, loss_target, m_mix_norm_pre, m_mix_norm_post, m_ffn_norm_pre, m_ffn_norm_post, m_ev_w_in, m_ev_a_conv_w, m_ev_a_conv_b, m_ev_a_ln_g, m_ev_a_ln_b, m_ev_sinks, m_ev_w_out, m_od_w_in, m_od_conv_w, m_od_w_out, m_ffn_w_up, m_ffn_conv_w, m_ffn_w_down, v_mix_norm_pre, v_mix_norm_post, v_ffn_norm_pre, v_ffn_norm_post, v_ev_w_in, v_ev_a_conv_w, v_ev_a_conv_b, v_ev_a_ln_g, v_ev_a_ln_b, v_ev_sinks, v_ev_w_out, v_od_w_in, v_od_conv_w, v_od_w_out, v_ffn_w_up, v_ffn_conv_w, v_ffn_w_down):
    w = dict(zip(WEIGHTS, (mix_norm_pre, mix_norm_post, ffn_norm_pre, ffn_norm_post, ev_w_in, ev_a_conv_w, ev_a_conv_b,
                           ev_a_ln_g, ev_a_ln_b, ev_sinks, ev_w_out, od_w_in, od_conv_w, od_w_out, ffn_w_up, ffn_conv_w,
                           ffn_w_down)))
    mom = dict(zip(WEIGHTS, (m_mix_norm_pre, m_mix_norm_post, m_ffn_norm_pre, m_ffn_norm_post, m_ev_w_in, m_ev_a_conv_w,
                             m_ev_a_conv_b, m_ev_a_ln_g, m_ev_a_ln_b, m_ev_sinks, m_ev_w_out, m_od_w_in, m_od_conv_w,
                             m_od_w_out, m_ffn_w_up, m_ffn_conv_w, m_ffn_w_down)))
    var = dict(zip(WEIGHTS, (v_mix_norm_pre, v_mix_norm_post, v_ffn_norm_pre, v_ffn_norm_post, v_ev_w_in, v_ev_a_conv_w,
                             v_ev_a_conv_b, v_ev_a_ln_g, v_ev_a_ln_b, v_ev_sinks, v_ev_w_out, v_od_w_in, v_od_conv_w,
                             v_od_w_out, v_ffn_w_up, v_ffn_conv_w, v_ffn_w_down)))
    ix, iy, ic = lax.axis_index("x"), lax.axis_index("y"), lax.axis_index("c")
    dev = 4 * ix + 2 * iy + ic
    two = lambda a: a.reshape(-1, a.shape[-1])

    dev1 = jnp.reshape(dev, (1,)).astype(jnp.int32)
    shard = {n: two(w[n].astype(BF16)) for n in BIG}
    gathered = lambda n, a: a.reshape((N_DEV,) + w[n].shape)
    ev_names = [n for n in BIG if n.startswith("ev_")]
    ev_gat = _all_gather([shard[n] for n in ev_names] + [_pack([w[n] for n in SMALL_SHARDED], 8)], "gather_ev")
    p = _prep_ev({n: gathered(n, a) for n, a in zip(ev_names, ev_gat)})
    rest_names = [n for n in BIG if not n.startswith("ev_")]
    first = shard[rest_names[0]] + (ev_gat[0][0:1, 0:1] * 0).astype(BF16)
    started = _exchange_start([first] + [shard[n] for n in rest_names[1:]], False, "gather_start")

    def rest_weights(after):
        lands = _exchange_wait(started, False, after, "gather_wait")
        return _prep_rest({n: gathered(n, a) for n, a in zip(rest_names, lands)})

    small = {n: w[n] for n in SMALL_REPL}
    small_shapes = [w[n].shape for n in SMALL_SHARDED]
    conv_gat = ev_gat[len(ev_names)].reshape(N_DEV, 8, LANES)
    per_dev = [_unpack(conv_gat[d], small_shapes) for d in range(N_DEV)]
    for k, n in enumerate(SMALL_SHARDED):
        small[n] = jnp.concatenate([per_dev[d][k] for d in range(N_DEV)], axis=-1)

    exchanges = []

    def grads_ready(names, big):
        blocks = lambda a, n: a.reshape(N_DEV, -1, w[n].shape[-1])
        bufs = [blocks(big[n], n) for n in names]
        payload = [blocks(big[n + ":bf16"], n) if n + ":bf16" in big else _to_bf16(b) for n, b in zip(names, bufs)]
        st = _exchange_start(payload, True, "grads_start_" + names[0])
        exchanges.append((names, bufs, st))
        return st[-1][0, 0]

    lpart, grad_x, big, g = _local_step(x[0], positions[0], loss_target[0], p, rest_weights, small, started[-1][0, 0],
                                        grads_ready)
    loss = lax.psum(lpart, ("x", "y", "c"))

    grads, delta, new_m, new_v = {}, {}, {}, {}
    for names, bufs, st in exchanges:
        lands = _exchange_wait(st, True, grad_x, "grads_wait_" + names[0])
        for n, b, land in zip(names, bufs, lands):
            outs = _adamw_rs(b, land, two(w[n]), two(mom[n]), two(var[n]), dev1)
            grads[n], delta[n], new_m[n], new_v[n] = (a.reshape(w[n].shape) for a in outs)

    small_names = SMALL_REPL + SMALL_SHARDED
    s_all = _sum_blocks(_all_gather([_pack([g[n] for n in small_names], 64)], "gather_small_grads")[0], N_DEV)
    for n, a in zip(small_names, _unpack(s_all, [small[n].shape for n in small_names])):
        if n in SMALL_SHARDED:
            width = w[n].shape[-1]
            a = lax.dynamic_slice_in_dim(a, dev * width, width, axis=a.ndim - 1)
        grads[n] = a
    pk = lambda dct: _pack([dct[n] for n in small_names], 16)
    outs = _adamw(pk(w), pk(grads), pk(mom), pk(var))
    for dst, packed in zip((delta, new_m, new_v), outs):
        for n, a in zip(small_names, _unpack(packed, [w[n].shape for n in small_names])):
            dst[n] = a

    return (loss, grad_x[None], *[grads[n] for n in WEIGHTS], *[delta[n] for n in WEIGHTS],
            *[new_m[n] for n in WEIGHTS], *[new_v[n] for n in WEIGHTS])
```

```python
import jax
import jax.numpy as jnp
from jax import lax
from jax.experimental import pallas as pl
from jax.experimental.pallas import tpu as pltpu

F32, BF16 = jnp.float32, jnp.bfloat16

D_MODEL = 1024
A_CH = 512
A_CONV = 31
Q_DIM = 512
KV_DIM = 128
HEAD_DIM = 64
N_Q_HEADS = 8
N_KV_HEADS = 2
GROUP = 4
BLOCK = 128
EVEN_IN = 1792
SC_DIM = 1024
D_FF = 2816
ROPE_THETA = 500000.0
ROPE_DIM = 16
RMS_EPS = 1e-6
LN_EPS = 1e-5
SCALE = HEAD_DIM ** -0.5
NEG = -1e30

ADAM_LR, ADAM_B1, ADAM_B2, ADAM_EPS, ADAM_WD, ADAM_STEP = 0.001, 0.9, 0.999, 1e-08, 0.01, 10

N_DEV = 8
FF_N = 2 * D_FF // N_DEV
LANES = 1024
HALO3 = 8
HALO31 = 32
VMEM_LIMIT = 56 * 1024 * 1024

TM = 512
TM_BWD = 256
TK_DW = 2048
ATT_NB = 4
SUB = 128

_ANY = pl.BlockSpec(memory_space=pl.ANY)
_CONTRACT_LAST = (((1,), (1,)), ((), ()))
_CONTRACT_FIRST = (((0,), (0,)), ((), ()))


def _cp(sem, vmem=VMEM_LIMIT):
    return pltpu.CompilerParams(dimension_semantics=sem, vmem_limit_bytes=vmem)


def _full(shape):
    n = len(shape)
    return pl.BlockSpec(shape, lambda *_: (0,) * n)


def _rows(tm, n):
    return pl.BlockSpec((tm, n), lambda i, *_: (i, 0))


def _sigmoid(x):
    return 0.5 * jnp.tanh(0.5 * x) + 0.5


def _rsqrt_mean(x):
    return lax.rsqrt(jnp.mean(x * x, axis=-1, keepdims=True) + RMS_EPS)


def _rms_bwd(x, g, dy):
    r = _rsqrt_mean(x)
    xh = x * r
    dxh = dy * g
    dx = r * (dxh - xh * jnp.mean(dxh * xh, axis=-1, keepdims=True))
    return dx, jnp.sum(dy * xh, axis=0, keepdims=True)


def _acc_out(ref, first, val):
    @pl.when(first)
    def _():
        ref[...] = val

    @pl.when(jnp.logical_not(first))
    def _():
        ref[...] += val


def _rope_tables(positions):
    half = ROPE_DIM // 2
    inv_freq = ROPE_THETA ** (-(jnp.arange(half, dtype=F32) * 2.0 / ROPE_DIM))
    ang = positions.astype(F32)[:, None] * inv_freq
    cos, sin = jnp.cos(ang), jnp.sin(ang)
    t = positions.shape[0]
    one, zero = jnp.ones((t, HEAD_DIM - ROPE_DIM), F32), jnp.zeros((t, HEAD_DIM - ROPE_DIM), F32)
    z8 = jnp.zeros((t, half), F32)
    c = jnp.concatenate([cos, cos, one], axis=1)
    sa = jnp.concatenate([z8, sin, zero], axis=1)
    sb = jnp.concatenate([-sin, z8, zero], axis=1)
    return tuple(jnp.tile(a, (1, 2)) for a in (c, sa, sb))


def _rope(t, c, sa, sb):
    return t * c + pltpu.roll(t, 8, 1) * sa + pltpu.roll(t, 120, 1) * sb


def _rope_bwd(d, c, sa, sb):
    return d * c + pltpu.roll(d * sa, 120, 1) + pltpu.roll(d * sb, 8, 1)


def _ev_in(x, gpre, w_in, rc, rsa, rsb):
    t = x.shape[0]
    tm = min(TM, t)

    def body(x_ref, g_ref, w_ref, c_ref, sa_ref, sb_ref, h_ref, zag_ref, q_ref, k_ref, v_ref):
        xv = x_ref[...]
        h = (xv * _rsqrt_mean(xv) * g_ref[...]).astype(BF16)
        h_ref[...] = h
        z = jnp.dot(h, w_ref[...], preferred_element_type=F32)
        zag_ref[...] = z[:, :2 * A_CH].astype(BF16)
        c, sa, sb = c_ref[...], sa_ref[...], sb_ref[...]
        q0 = 2 * A_CH
        for j in range(Q_DIM // 128):
            q_ref[:, 128 * j:128 * (j + 1)] = _rope(z[:, q0 + 128 * j:q0 + 128 * (j + 1)], c, sa, sb).astype(BF16)
        k0 = q0 + Q_DIM
        k_ref[...] = _rope(z[:, k0:k0 + KV_DIM], c, sa, sb).astype(BF16)
        v_ref[...] = z[:, k0 + KV_DIM:k0 + 2 * KV_DIM].astype(BF16)

    return pl.pallas_call(
        body, name="ev_in", grid=(t // tm,),
        in_specs=[_rows(tm, D_MODEL), _full((1, D_MODEL)), _full((D_MODEL, EVEN_IN)),
                  _rows(tm, 128), _rows(tm, 128), _rows(tm, 128)],
        out_specs=[_rows(tm, D_MODEL), _rows(tm, 2 * A_CH), _rows(tm, Q_DIM), _rows(tm, KV_DIM), _rows(tm, KV_DIM)],
        out_shape=[jax.ShapeDtypeStruct((t, D_MODEL), BF16), jax.ShapeDtypeStruct((t, 2 * A_CH), BF16),
                   jax.ShapeDtypeStruct((t, Q_DIM), BF16), jax.ShapeDtypeStruct((t, KV_DIM), BF16),
                   jax.ShapeDtypeStruct((t, KV_DIM), BF16)],
        compiler_params=_cp(("arbitrary",)),
    )(x, gpre, w_in, rc, rsa, rsb)


def _glu(zag):
    z = zag.astype(F32)
    return z[:, :A_CH] * _sigmoid(z[:, A_CH:])


def _tap_copies(ext, cbuf, first_row, rows):
    for b in range(1, 8):
        s = first_row(b)
        cbuf[b - 1] = ext[s:s + rows, :]


def _conf_fwd(zag, conv_w, conv_b, ln_g, ln_b):
    t = zag.shape[0]
    tm = min(TM_BWD, t)
    rows = tm + HALO31 - 8

    def body(z_ref, w_ref, b_ref, g_ref, lb_ref, c_ref, a_ref, ext, cbuf):
        i = pl.program_id(0)

        @pl.when(i == 0)
        def _():
            ext[0:HALO31, :] = jnp.zeros((HALO31, A_CH), F32)

        ext[HALO31:HALO31 + tm, :] = _glu(z_ref[...])
        _tap_copies(ext, cbuf, lambda b: 8 - b, rows)
        for rs in range(0, tm, SUB):
            for cs in range(0, A_CH, 128):
                acc = jnp.zeros((SUB, 128), F32)
                for k in range(A_CONV):
                    lag_a, lag_b = divmod(k, 8)
                    r0 = HALO31 - 8 - 8 * lag_a + rs
                    src = (ext[r0 + 8:r0 + 8 + SUB, cs:cs + 128] if lag_b == 0
                           else cbuf[lag_b - 1, r0:r0 + SUB, cs:cs + 128])
                    acc = acc + w_ref[A_CONV - 1 - k:A_CONV - k, cs:cs + 128] * src
                c_ref[rs:rs + SUB, cs:cs + 128] = acc
        ext[0:HALO31, :] = ext[tm:tm + HALO31, :]
        cv = c_ref[...] + b_ref[...]
        c_ref[...] = cv
        mu = jnp.mean(cv, axis=-1, keepdims=True)
        xc = cv - mu
        ln = xc * lax.rsqrt(jnp.mean(xc * xc, axis=-1, keepdims=True) + LN_EPS) * g_ref[...] + lb_ref[...]
        a_ref[...] = (ln * _sigmoid(ln)).astype(BF16)

    return pl.pallas_call(
        body, name="conf_fwd", grid=(t // tm,),
        in_specs=[_rows(tm, 2 * A_CH), _full((32, A_CH)), _full((1, A_CH)), _full((1, A_CH)), _full((1, A_CH))],
        out_specs=[_rows(tm, A_CH), _rows(tm, A_CH)],
        out_shape=[jax.ShapeDtypeStruct((t, A_CH), F32), jax.ShapeDtypeStruct((t, A_CH), BF16)],
        scratch_shapes=[pltpu.VMEM((HALO31 + tm, A_CH), F32), pltpu.VMEM((7, rows, A_CH), F32)],
        compiler_params=_cp(("arbitrary",)),
    )(zag, conv_w, conv_b, ln_g, ln_b)


def _attn_mask(first_block):
    row = lax.broadcasted_iota(jnp.int32, (GROUP * BLOCK, 2 * BLOCK), 0) & (BLOCK - 1)
    col = lax.broadcasted_iota(jnp.int32, (GROUP * BLOCK, 2 * BLOCK), 1)
    diff = row + BLOCK - col
    return (diff >= 0) & (diff < BLOCK) & ((col >= BLOCK) | jnp.logical_not(first_block))


def _sink_rows(s_ref, h):
    grp = lax.broadcasted_iota(jnp.int32, (GROUP * BLOCK, 1), 0) >> 7
    out = jnp.full((GROUP * BLOCK, 1), s_ref[GROUP * h], F32)
    for g in range(1, GROUP):
        out = jnp.where(grp == g, s_ref[GROUP * h + g], out)
    return out


def _attn_probs(q4, k2, mask, sink):
    s = lax.dot_general(q4, k2, _CONTRACT_LAST, preferred_element_type=F32) * SCALE
    s = jnp.where(mask, s, NEG)
    m = jnp.maximum(jnp.max(s, axis=-1, keepdims=True), sink)
    p = jnp.exp(s - m)
    es = jnp.exp(sink - m)
    inv = 1.0 / (jnp.sum(p, axis=-1, keepdims=True) + es)
    return p * inv, es * inv


def _q_heads(q, h):
    return jnp.concatenate([q[:, HEAD_DIM * (GROUP * h + g):HEAD_DIM * (GROUP * h + g + 1)] for g in range(GROUP)],
                           axis=0)


def _kv_head(prev, cur, h):
    return jnp.concatenate([prev[:, HEAD_DIM * h:HEAD_DIM * (h + 1)], cur[:, HEAD_DIM * h:HEAD_DIM * (h + 1)]], axis=0)


def _attn_fwd(q, k, v, sinks):
    t = q.shape[0]
    nb = min(ATT_NB, t // BLOCK)
    rows = nb * BLOCK

    def body(s_ref, q_ref, kc_ref, kp_ref, vc_ref, vp_ref, o_ref):
        first = pl.program_id(0) == 0
        for b in range(nb):
            lo = BLOCK * b
            mask = _attn_mask(first) if b == 0 else _attn_mask(False)
            qv, kc, vc = q_ref[lo:lo + BLOCK, :], kc_ref[lo:lo + BLOCK, :], vc_ref[lo:lo + BLOCK, :]
            kp = kp_ref[...] if b == 0 else kc_ref[lo - BLOCK:lo, :]
            vp = vp_ref[...] if b == 0 else vc_ref[lo - BLOCK:lo, :]
            for h in range(N_KV_HEADS):
                pn, _ = _attn_probs(_q_heads(qv, h), _kv_head(kp, kc, h), mask, _sink_rows(s_ref, h))
                o4 = jnp.dot(pn.astype(BF16), _kv_head(vp, vc, h), preferred_element_type=F32).astype(BF16)
                for g in range(GROUP):
                    c0 = HEAD_DIM * (GROUP * h + g)
                    o_ref[lo:lo + BLOCK, c0:c0 + HEAD_DIM] = o4[BLOCK * g:BLOCK * (g + 1), :]

    cur = lambda n: pl.BlockSpec((rows, n), lambda i: (i, 0))
    prev = lambda n: pl.BlockSpec((BLOCK, n), lambda i: (jnp.maximum(i * nb - 1, 0), 0))
    return pl.pallas_call(
        body, name="attn_fwd", grid=(t // rows,),
        in_specs=[pl.BlockSpec(memory_space=pltpu.SMEM), cur(Q_DIM), cur(KV_DIM), prev(KV_DIM), cur(KV_DIM),
                  prev(KV_DIM)],
        out_specs=cur(Q_DIM),
        out_shape=jax.ShapeDtypeStruct((t, Q_DIM), BF16),
        compiler_params=_cp(("arbitrary",)),
    )(sinks, q, k, k, v, v)


def _out_post(lhs, ws, x_in, gpost):
    t = x_in.shape[0]
    tm = min(TM, t)
    n = len(lhs)

    def body(*refs):
        x_ref, g_ref, m_ref, xo_ref = refs[2 * n:]
        m = jnp.dot(refs[0][...], refs[n][...], preferred_element_type=F32)
        for j in range(1, n):
            m = m + jnp.dot(refs[j][...], refs[n + j][...], preferred_element_type=F32)
        m_ref[...] = m
        xo_ref[...] = x_ref[...] + m * _rsqrt_mean(m) * g_ref[...]

    return pl.pallas_call(
        body, name="out_post", grid=(t // tm,),
        in_specs=[_rows(tm, a.shape[1]) for a in lhs] + [_full(w.shape) for w in ws]
                 + [_rows(tm, D_MODEL), _full((1, D_MODEL))],
        out_specs=[_rows(tm, D_MODEL), _rows(tm, D_MODEL)],
        out_shape=[jax.ShapeDtypeStruct((t, D_MODEL), F32)] * 2,
        compiler_params=_cp(("arbitrary",)),
    )(*lhs, *ws, x_in, gpost)


def _conv3(w_ref, ext, tm):
    s = HALO3 - 2
    return (w_ref[0:1, :] * ext[s:s + tm, :] + w_ref[1:2, :] * ext[s + 1:s + 1 + tm, :]
            + w_ref[2:3, :] * ext[s + 2:s + 2 + tm, :])


def _ffn_fwd(x1, gpre, wup, layer, cw, wd, gpost):
    t = x1.shape[0]
    tm = min(TM, t)
    nc, n = wup.shape[1], wup.shape[4]

    def body(x_ref, gpre_ref, wup_ref, cw_ref, wd_ref, gpost_ref, h_ref, up_ref, u_ref, f_ref, xo_ref, h_s, acc, ext, hal):
        i, c = pl.program_id(0), pl.program_id(1)

        @pl.when(c == 0)
        def _():
            xv = x_ref[...]
            h = (xv * _rsqrt_mean(xv) * gpre_ref[...]).astype(BF16)
            h_s[...] = h
            h_ref[...] = h

        @pl.when(i == 0)
        def _():
            hal[c] = jnp.zeros((2, HALO3, n), F32)

        u = []
        for gv in range(2):
            up = jnp.dot(h_s[...], wup_ref[gv, 0, 0], preferred_element_type=F32)
            up_ref[gv, 0] = up.astype(BF16)
            ext[gv, 0:HALO3, :] = hal[c, gv]
            ext[gv, HALO3:HALO3 + tm, :] = up
            hal[c, gv] = ext[gv, tm:tm + HALO3, :]
            s = HALO3 - 2
            u.append(cw_ref[gv, 0, 0:1, :] * ext[gv, s:s + tm, :] + cw_ref[gv, 0, 1:2, :] * ext[gv, s + 1:s + 1 + tm, :]
                     + cw_ref[gv, 0, 2:3, :] * up)
            u_ref[gv, 0] = u[gv].astype(BF16)
        act = (u[0] * _sigmoid(u[0]) * u[1]).astype(BF16)
        part = jnp.dot(act, wd_ref[...], preferred_element_type=F32)

        @pl.when(c == 0)
        def _():
            acc[...] = part

        @pl.when(jnp.logical_and(c > 0, c < nc - 1))
        def _():
            acc[...] += part

        @pl.when(c == nc - 1)
        def _():
            f = acc[...] + part
            f_ref[...] = f
            xo_ref[...] = x_ref[...] + f * _rsqrt_mean(f) * gpost_ref[...]

    row = lambda w: pl.BlockSpec((tm, w), lambda i, c: (i, 0))
    one = _full((1, D_MODEL))
    return pl.pallas_call(
        body, name="ffn_fwd", grid=(t // tm, nc),
        in_specs=[row(D_MODEL), one, pl.BlockSpec((2, 1, 1, D_MODEL, n), lambda i, c: (0, c, layer, 0, 0)),
                  pl.BlockSpec((2, 1, 3, n), lambda i, c: (0, c, 0, 0)), pl.BlockSpec((n, D_MODEL), lambda i, c: (c, 0)),
                  one],
        out_specs=[row(D_MODEL), pl.BlockSpec((2, 1, tm, n), lambda i, c: (0, c, i, 0)),
                   pl.BlockSpec((2, 1, tm, n), lambda i, c: (0, c, i, 0)), row(D_MODEL), row(D_MODEL)],
        out_shape=[jax.ShapeDtypeStruct((t, D_MODEL), BF16), jax.ShapeDtypeStruct((2, nc, t, n), BF16),
                   jax.ShapeDtypeStruct((2, nc, t, n), BF16), jax.ShapeDtypeStruct((t, D_MODEL), F32),
                   jax.ShapeDtypeStruct((t, D_MODEL), F32)],
        scratch_shapes=[pltpu.VMEM((tm, D_MODEL), BF16), pltpu.VMEM((tm, D_MODEL), F32),
                        pltpu.VMEM((2, HALO3 + tm, n), F32), pltpu.VMEM((nc, 2, HALO3, n), F32)],
        compiler_params=_cp(("arbitrary", "arbitrary")),
    )(x1, gpre, wup, cw, wd, gpost)


def _od_fwd(x_in, gpre, w_in, cw, w_out, gpost):
    t = x_in.shape[0]
    tm = min(TM, t)
    ns, _, n = w_in.shape

    def body(x_ref, gpre_ref, w_ref, cw_ref, wo_ref, gpost_ref, h_ref, z_ref, cv_ref, y_ref, m_ref, xo_ref, z_s, ext):
        i = pl.program_id(0)
        xv = x_ref[...]
        h = (xv * _rsqrt_mean(xv) * gpre_ref[...]).astype(BF16)
        h_ref[...] = h
        for j in range(ns):
            z_s[:, n * j:n * (j + 1)] = jnp.dot(h, w_ref[j], preferred_element_type=F32)
        z_ref[...] = z_s[...].astype(BF16)

        @pl.when(i == 0)
        def _():
            ext[0:HALO3, :] = jnp.zeros((HALO3, SC_DIM), F32)

        ext[HALO3:HALO3 + tm, :] = z_s[:, SC_DIM:2 * SC_DIM] * z_s[:, 2 * SC_DIM:]
        cv = _conv3(cw_ref, ext, tm)
        cv_ref[...] = cv.astype(BF16)
        y = (z_s[:, :SC_DIM] * cv).astype(BF16)
        ext[0:HALO3, :] = ext[tm:tm + HALO3, :]
        y_ref[...] = y
        m = jnp.dot(y, wo_ref[...], preferred_element_type=F32)
        m_ref[...] = m
        xo_ref[...] = xv + m * _rsqrt_mean(m) * gpost_ref[...]

    return pl.pallas_call(
        body, name="od_fwd", grid=(t // tm,),
        in_specs=[_rows(tm, D_MODEL), _full((1, D_MODEL)), _full((ns, D_MODEL, n)), _full((3, SC_DIM)),
                  _full((SC_DIM, D_MODEL)), _full((1, D_MODEL))],
        out_specs=[_rows(tm, D_MODEL), _rows(tm, 3 * SC_DIM), _rows(tm, SC_DIM), _rows(tm, SC_DIM), _rows(tm, D_MODEL),
                   _rows(tm, D_MODEL)],
        out_shape=[jax.ShapeDtypeStruct((t, D_MODEL), BF16), jax.ShapeDtypeStruct((t, 3 * SC_DIM), BF16),
                   jax.ShapeDtypeStruct((t, SC_DIM), BF16), jax.ShapeDtypeStruct((t, SC_DIM), BF16),
                   jax.ShapeDtypeStruct((t, D_MODEL), F32), jax.ShapeDtypeStruct((t, D_MODEL), F32)],
        scratch_shapes=[pltpu.VMEM((tm, 3 * SC_DIM), F32), pltpu.VMEM((HALO3 + tm, SC_DIM), F32)],
        compiler_params=_cp(("arbitrary",)),
    )(x_in, gpre, w_in, cw, w_out, gpost)


def _loss_grad(y, target):
    t = y.shape[0]
    tm = min(TM, t)

    def body(y_ref, t_ref, dy_ref, l_ref):
        e = y_ref[...] - t_ref[...]
        dy_ref[...] = e * (1.0 / D_MODEL)
        part = jnp.zeros((1, 128), F32) + jnp.sum(e * e) * (0.5 / D_MODEL)
        _acc_out(l_ref, pl.program_id(0) == 0, part)

    return pl.pallas_call(
        body, name="loss_grad", grid=(t // tm,),
        in_specs=[_rows(tm, D_MODEL), _rows(tm, D_MODEL)],
        out_specs=[_rows(tm, D_MODEL), _full((1, 128))],
        out_shape=[jax.ShapeDtypeStruct((t, D_MODEL), F32), jax.ShapeDtypeStruct((1, 128), F32)],
        compiler_params=_cp(("arbitrary",)),
    )(y, target)


def _dw2d(a, b, bm, bn):
    t, m = a.shape
    n = b.shape[1]
    tk = min(TK_DW, t)

    def body(a_ref, b_ref, o_ref):
        part = lax.dot_general(a_ref[...], b_ref[...], _CONTRACT_FIRST, preferred_element_type=F32)
        _acc_out(o_ref, pl.program_id(2) == 0, part)

    return pl.pallas_call(
        body, name="dw2d", grid=(m // bm, n // bn, t // tk),
        in_specs=[pl.BlockSpec((tk, bm), lambda i, j, k: (k, i)), pl.BlockSpec((tk, bn), lambda i, j, k: (k, j))],
        out_specs=pl.BlockSpec((bm, bn), lambda i, j, k: (i, j)),
        out_shape=jax.ShapeDtypeStruct((m, n), F32),
        compiler_params=_cp(("arbitrary", "arbitrary", "arbitrary")),
    )(a, b)


def _dw_cols(a, b, n_blk):
    t, m = a.shape
    s = b.shape[1] // n_blk
    tk = min(TK_DW, t)
    nk = t // tk

    def body(a_ref, b_ref, o_ref, ob_ref):
        part = lax.dot_general(a_ref[...], b_ref[...], _CONTRACT_FIRST, preferred_element_type=F32)
        _acc_out(o_ref.at[0], pl.program_id(1) == 0, part)

        @pl.when(pl.program_id(1) == nk - 1)
        def _():
            ob_ref[...] = o_ref[...].astype(BF16)

    spec = pl.BlockSpec((1, m, n_blk), lambda j, k: (j, 0, 0))
    return pl.pallas_call(
        body, name="dw_cols", grid=(s, nk),
        in_specs=[pl.BlockSpec((tk, m), lambda j, k: (k, 0)), pl.BlockSpec((tk, n_blk), lambda j, k: (k, j))],
        out_specs=[spec, spec],
        out_shape=[jax.ShapeDtypeStruct((s, m, n_blk), F32), jax.ShapeDtypeStruct((s, m, n_blk), BF16)],
        compiler_params=_cp(("arbitrary", "arbitrary")),
    )(a, b)


def _dw_up(h, dup, layer, buf):
    t, m = h.shape
    s, _, n = dup.shape
    tk = min(TK_DW, t)
    nk = t // tk

    def body(*refs):
        a_ref, b_ref, o_ref, ob_ref = refs[0], refs[1], refs[-2], refs[-1]
        part = lax.dot_general(a_ref[...], b_ref[0], _CONTRACT_FIRST, preferred_element_type=F32)
        _acc_out(o_ref.at[0, 0], pl.program_id(1) == 0, part)

        @pl.when(pl.program_id(1) == nk - 1)
        def _():
            ob_ref[...] = o_ref[...].astype(BF16)

    spec = pl.BlockSpec((1, 1, m, n), lambda j, k: (j, layer, 0, 0))
    return pl.pallas_call(
        body, name="dw_up", grid=(s, nk),
        in_specs=[pl.BlockSpec((tk, m), lambda j, k: (k, 0)), pl.BlockSpec((1, tk, n), lambda j, k: (j, k, 0))]
                 + ([] if buf is None else [_ANY, _ANY]),
        out_specs=[spec, spec],
        out_shape=[jax.ShapeDtypeStruct((s, 2, m, n), F32), jax.ShapeDtypeStruct((s, 2, m, n), BF16)],
        input_output_aliases={} if buf is None else {2: 0, 3: 1},
        compiler_params=_cp(("arbitrary", "arbitrary")),
    )(h, dup, *([] if buf is None else buf))


def _dw_down(act, df, layer, buf):
    nc, t, n = act.shape
    d = df.shape[1]
    tk = min(TK_DW, t)
    nk = t // tk

    def body(*refs):
        a_ref, b_ref, o_ref, ob_ref = refs[0], refs[1], refs[-2], refs[-1]
        part = lax.dot_general(a_ref[0], b_ref[...], _CONTRACT_FIRST, preferred_element_type=F32)
        part = part.reshape(2, n // 2, d)
        first = pl.program_id(1) == 0

        @pl.when(first)
        def _():
            o_ref[:, 0] = part

        @pl.when(jnp.logical_not(first))
        def _():
            o_ref[:, 0] += part

        @pl.when(pl.program_id(1) == nk - 1)
        def _():
            ob_ref[...] = o_ref[...].astype(BF16)

    spec = pl.BlockSpec((2, 1, n // 2, d), lambda c, k: (c, layer, 0, 0))
    return pl.pallas_call(
        body, name="dw_down", grid=(nc, nk),
        in_specs=[pl.BlockSpec((1, tk, n), lambda c, k: (c, k, 0)), pl.BlockSpec((tk, d), lambda c, k: (k, 0))]
                 + ([] if buf is None else [_ANY, _ANY]),
        out_specs=[spec, spec],
        out_shape=[jax.ShapeDtypeStruct((2 * nc, 2, n // 2, d), F32), jax.ShapeDtypeStruct((2 * nc, 2, n // 2, d), BF16)],
        input_output_aliases={} if buf is None else {2: 0, 3: 1},
        compiler_params=_cp(("arbitrary", "arbitrary")),
    )(act, df, *([] if buf is None else buf))


def _dz_wt_rms_bwd(dz, wt, x_in, gpre, dres):
    t, n = dz.shape
    tm = min(TM, t)

    def body(dz_ref, wt_ref, x_ref, g_ref, dres_ref, dx_ref, dg_ref):
        dh = jnp.dot(dz_ref[...], wt_ref[...], preferred_element_type=F32)
        dx, dg = _rms_bwd(x_ref[...], g_ref[...], dh)
        dx_ref[...] = dres_ref[...] + dx
        _acc_out(dg_ref, pl.program_id(0) == 0, dg)

    return pl.pallas_call(
        body, name="dz_wt_rms_bwd", grid=(t // tm,),
        in_specs=[_rows(tm, n), _full((n, D_MODEL)), _rows(tm, D_MODEL), _full((1, D_MODEL)), _rows(tm, D_MODEL)],
        out_specs=[_rows(tm, D_MODEL), _full((1, D_MODEL))],
        out_shape=[jax.ShapeDtypeStruct((t, D_MODEL), F32), jax.ShapeDtypeStruct((1, D_MODEL), F32)],
        compiler_params=_cp(("arbitrary",)),
    )(dz, wt, x_in, gpre, dres)


def _shift_matrices(shift, shift_h, tm, hb):
    row = lax.broadcasted_iota(jnp.int32, (2 * tm, tm), 0)
    col = lax.broadcasted_iota(jnp.int32, (2 * tm, tm), 1)
    hit = ((row < tm) & (col == row + 1)) | ((row >= tm) & (col == row - tm + 2))
    shift[...] = jnp.where(hit, 1.0, 0.0).astype(BF16)
    row = lax.broadcasted_iota(jnp.int32, (hb, hb), 0)
    col = lax.broadcasted_iota(jnp.int32, (hb, hb), 1)
    hit = ((row < HALO3) & (col == row - (HALO3 - 1))) | ((row >= HALO3) & (col == row - (2 * HALO3 - 2)))
    shift_h[...] = jnp.where(hit, 1.0, 0.0).astype(BF16)


def _next_rows(shift, shift_h, xb, nxt, d12_s, tm):
    d12_s[...] = jnp.dot(shift[...], xb, preferred_element_type=F32)
    edge = jnp.dot(shift_h[...], nxt, preferred_element_type=F32)
    d12_s[tm - HALO3:tm, :] += edge[0:HALO3, :]
    d12_s[2 * tm - HALO3:2 * tm, :] += edge[HALO3:2 * HALO3, :]


def _ffn_bwd(f, dxo, gpost, x_in, gpre, up, u, cw, wdt, wupt):
    t = f.shape[0]
    tm = min(TM_BWD, t)
    nt = t // tm
    nc, n = up.shape[1], up.shape[3]
    hb = 2 * HALO3

    def body(f_ref, dxo_ref, gpost_ref, x_ref, gpre_ref, up_ref, u_ref, cw_ref, wdt_ref, wupt_ref,
             df_ref, act_ref, dup_ref, dx_ref, dgpost_ref, dgpre_ref, dcw_ref, df_s, acc, du_s, dub_s, d12_s, hal, shift, shift_h):
        i, c = pl.program_id(0), pl.program_id(1)

        @pl.when(c == 0)
        def _():
            df, dg = _rms_bwd(f_ref[...], gpost_ref[...], dxo_ref[...])
            df_s[...] = df.astype(BF16)
            df_ref[...] = df.astype(BF16)
            _acc_out(dgpost_ref, i == 0, dg)

        @pl.when(i == 0)
        def _():
            hal[c] = jnp.zeros((2, hb, n), BF16)
            dcw_ref[0, c] = jnp.zeros((8, n), F32)
            dcw_ref[1, c] = jnp.zeros((8, n), F32)

        @pl.when(jnp.logical_and(i == 0, c == 0))
        def _():
            _shift_matrices(shift, shift_h, tm, hb)

        dact = jnp.dot(df_s[...], wdt_ref[0], preferred_element_type=F32)
        g, v = u_ref[0, 0].astype(F32), u_ref[1, 0].astype(F32)
        sg = _sigmoid(g)
        sil = g * sg
        act_ref[0] = (sil * v).astype(BF16)
        dug = dact * v * (sg + sil * (1.0 - sg))
        duv = dact * sil
        du_s[0], du_s[1] = dug, duv
        dub_s[0], dub_s[1] = dug.astype(BF16), duv.astype(BF16)
        dh = None
        for gv in range(2):
            _next_rows(shift, shift_h, dub_s[gv], hal[c, gv], d12_s, tm)
            hal[c, gv] = dub_s[gv, 0:hb, :]
            du, d1, d2 = du_s[gv], d12_s[0:tm, :], d12_s[tm:2 * tm, :]
            dup = (cw_ref[gv, 0, 2:3, :] * du + cw_ref[gv, 0, 1:2, :] * d1 + cw_ref[gv, 0, 0:1, :] * d2).astype(BF16)
            dup_ref[gv, 0] = dup
            upc = up_ref[gv, 0].astype(F32)
            dcw_ref[gv, c, 2:3, :] += jnp.sum(upc * du, axis=0, keepdims=True)
            dcw_ref[gv, c, 1:2, :] += jnp.sum(upc * d1, axis=0, keepdims=True)
            dcw_ref[gv, c, 0:1, :] += jnp.sum(upc * d2, axis=0, keepdims=True)
            part = jnp.dot(dup, wupt_ref[gv, 0], preferred_element_type=F32)
            dh = part if dh is None else dh + part
        _acc_out(acc, c == 0, dh)

        @pl.when(c == nc - 1)
        def _():
            dx, dg = _rms_bwd(x_ref[...], gpre_ref[...], acc[...])
            dx_ref[...] = dxo_ref[...] + dx
            _acc_out(dgpre_ref, i == 0, dg)

    rrow = lambda w: pl.BlockSpec((tm, w), lambda i, c: (nt - 1 - i, 0))
    blk = pl.BlockSpec((2, 1, tm, n), lambda i, c: (0, c, nt - 1 - i, 0))
    one = _full((1, D_MODEL))
    return pl.pallas_call(
        body, name="ffn_bwd", grid=(nt, nc),
        in_specs=[rrow(D_MODEL), rrow(D_MODEL), one, rrow(D_MODEL), one, blk, blk,
                  pl.BlockSpec((2, 1, 3, n), lambda i, c: (0, c, 0, 0)),
                  pl.BlockSpec((1, D_MODEL, n), lambda i, c: (c, 0, 0)),
                  pl.BlockSpec((2, 1, n, D_MODEL), lambda i, c: (0, c, 0, 0))],
        out_specs=[rrow(D_MODEL), pl.BlockSpec((1, tm, n), lambda i, c: (c, nt - 1 - i, 0)), blk, rrow(D_MODEL),
                   one, one, _full((2, nc, 8, n))],
        out_shape=[jax.ShapeDtypeStruct((t, D_MODEL), BF16), jax.ShapeDtypeStruct((nc, t, n), BF16),
                   jax.ShapeDtypeStruct((2, nc, t, n), BF16), jax.ShapeDtypeStruct((t, D_MODEL), F32),
                   jax.ShapeDtypeStruct((1, D_MODEL), F32), jax.ShapeDtypeStruct((1, D_MODEL), F32),
                   jax.ShapeDtypeStruct((2, nc, 8, n), F32)],
        scratch_shapes=[pltpu.VMEM((tm, D_MODEL), BF16), pltpu.VMEM((tm, D_MODEL), F32),
                        pltpu.VMEM((2, tm, n), F32), pltpu.VMEM((2, tm, n), BF16), pltpu.VMEM((2 * tm, n), F32),
                        pltpu.VMEM((nc, 2, hb, n), BF16), pltpu.VMEM((2 * tm, tm), BF16), pltpu.VMEM((hb, hb), BF16)],
        compiler_params=_cp(("arbitrary", "arbitrary")),
    )(f, dxo, gpost, x_in, gpre, up, u, cw, wdt, wupt)


def _od_bwd(m, dxo, gpost, x_in, gpre, z, cv, cw, wot, wint):
    t = m.shape[0]
    tm = min(TM_BWD, t)
    nt = t // tm
    hb = 2 * HALO3

    def body(m_ref, dxo_ref, gpost_ref, x_ref, gpre_ref, z_ref, cv_ref, cw_ref, wot_ref, wint_ref,
             dm_ref, dz_ref, dx_ref, dgpost_ref, dgpre_ref, dcw_ref, dcvb_s, d12_s, dz_s, hal, shift, shift_h):
        i = pl.program_id(0)
        dxo = dxo_ref[...]
        dm, dg = _rms_bwd(m_ref[...], gpost_ref[...], dxo)
        dmb = dm.astype(BF16)
        dm_ref[...] = dmb
        _acc_out(dgpost_ref, i == 0, dg)

        @pl.when(i == 0)
        def _():
            hal[...] = jnp.zeros((hb, SC_DIM), BF16)
            dcw_ref[...] = jnp.zeros((8, SC_DIM), F32)
            _shift_matrices(shift, shift_h, tm, hb)

        dy = jnp.dot(dmb, wot_ref[...], preferred_element_type=F32)
        z = z_ref[...].astype(F32)
        b, cg, u = z[:, :SC_DIM], z[:, SC_DIM:2 * SC_DIM], z[:, 2 * SC_DIM:]
        dz_s[:, 0:SC_DIM] = (dy * cv_ref[...].astype(F32)).astype(BF16)
        dcv = dy * b
        dcvb_s[...] = dcv.astype(BF16)
        _next_rows(shift, shift_h, dcvb_s[...], hal[...], d12_s, tm)
        hal[...] = dcvb_s[0:hb, :]
        d1, d2 = d12_s[0:tm, :], d12_s[tm:2 * tm, :]
        dcu = cw_ref[2:3, :] * dcv + cw_ref[1:2, :] * d1 + cw_ref[0:1, :] * d2
        cu = cg * u
        dcw_ref[2:3, :] += jnp.sum(cu * dcv, axis=0, keepdims=True)
        dcw_ref[1:2, :] += jnp.sum(cu * d1, axis=0, keepdims=True)
        dcw_ref[0:1, :] += jnp.sum(cu * d2, axis=0, keepdims=True)
        dz_s[:, SC_DIM:2 * SC_DIM] = (dcu * u).astype(BF16)
        dz_s[:, 2 * SC_DIM:3 * SC_DIM] = (dcu * cg).astype(BF16)
        dz_ref[...] = dz_s[...]
        dh = jnp.dot(dz_s[...], wint_ref[...], preferred_element_type=F32)
        dx, dg2 = _rms_bwd(x_ref[...], gpre_ref[...], dh)
        dx_ref[...] = dxo + dx
        _acc_out(dgpre_ref, i == 0, dg2)

    rrow = lambda w: pl.BlockSpec((tm, w), lambda i: (nt - 1 - i, 0))
    one = _full((1, D_MODEL))
    return pl.pallas_call(
        body, name="od_bwd", grid=(nt,),
        in_specs=[rrow(D_MODEL), rrow(D_MODEL), one, rrow(D_MODEL), one, rrow(3 * SC_DIM), rrow(SC_DIM),
                  _full((3, SC_DIM)), _full((D_MODEL, SC_DIM)), _full((3 * SC_DIM, D_MODEL))],
        out_specs=[rrow(D_MODEL), rrow(3 * SC_DIM), rrow(D_MODEL), one, one, _full((8, SC_DIM))],
        out_shape=[jax.ShapeDtypeStruct((t, D_MODEL), BF16), jax.ShapeDtypeStruct((t, 3 * SC_DIM), BF16),
                   jax.ShapeDtypeStruct((t, D_MODEL), F32), jax.ShapeDtypeStruct((1, D_MODEL), F32),
                   jax.ShapeDtypeStruct((1, D_MODEL), F32), jax.ShapeDtypeStruct((8, SC_DIM), F32)],
        scratch_shapes=[pltpu.VMEM((tm, SC_DIM), BF16), pltpu.VMEM((2 * tm, SC_DIM), F32),
                        pltpu.VMEM((tm, 3 * SC_DIM), BF16), pltpu.VMEM((hb, SC_DIM), BF16),
                        pltpu.VMEM((2 * tm, tm), BF16), pltpu.VMEM((hb, hb), BF16)],
        compiler_params=_cp(("arbitrary",)),
    )(m, dxo, gpost, x_in, gpre, z, cv, cw, wot, wint)


def _ev_bwd1(m, dxo, gpost, wot):
    t = m.shape[0]
    tm = min(TM, t)

    def body(m_ref, dxo_ref, g_ref, wot_ref, dm_ref, da_ref, do_ref, dg_ref):
        dm, dg = _rms_bwd(m_ref[...], g_ref[...], dxo_ref[...])
        dmb = dm.astype(BF16)
        dm_ref[...] = dmb
        _acc_out(dg_ref, pl.program_id(0) == 0, dg)
        dao = jnp.dot(dmb, wot_ref[...], preferred_element_type=F32)
        da_ref[...] = dao[:, :A_CH]
        do_ref[...] = dao[:, A_CH:].astype(BF16)

    return pl.pallas_call(
        body, name="ev_bwd1", grid=(t // tm,),
        in_specs=[_rows(tm, D_MODEL), _rows(tm, D_MODEL), _full((1, D_MODEL)), _full((D_MODEL, A_CH + Q_DIM))],
        out_specs=[_rows(tm, D_MODEL), _rows(tm, A_CH), _rows(tm, Q_DIM), _full((1, D_MODEL))],
        out_shape=[jax.ShapeDtypeStruct((t, D_MODEL), BF16), jax.ShapeDtypeStruct((t, A_CH), F32),
                   jax.ShapeDtypeStruct((t, Q_DIM), BF16), jax.ShapeDtypeStruct((1, D_MODEL), F32)],
        compiler_params=_cp(("arbitrary",)),
    )(m, dxo, gpost, wot)


def _conf_bwd(da, cv, zag, conv_w, ln_g, ln_b):
    t = da.shape[0]
    tm = min(TM_BWD, t)
    nt = t // tm
    rows = tm + HALO31 - 8

    def body(da_ref, c_ref, z_ref, w_ref, g_ref, lb_ref, dz_ref, dw_ref, dv_ref, ext_out, cbuf, glu_s, dglu_s):
        i = pl.program_id(0)

        @pl.when(i == 0)
        def _():
            ext_out[tm:tm + HALO31, :] = jnp.zeros((HALO31, A_CH), F32)
            dw_ref[...] = jnp.zeros((32, A_CH), F32)
            dv_ref[...] = jnp.zeros((8, A_CH), F32)

        x = c_ref[...]
        mu = jnp.mean(x, axis=-1, keepdims=True)
        xc = x - mu
        rstd = lax.rsqrt(jnp.mean(xc * xc, axis=-1, keepdims=True) + LN_EPS)
        xh = xc * rstd
        ln = xh * g_ref[...] + lb_ref[...]
        sl = _sigmoid(ln)
        dln = da_ref[...] * (sl * (1.0 + ln * (1.0 - sl)))
        dxh = dln * g_ref[...]
        dc = rstd * (dxh - jnp.mean(dxh, axis=-1, keepdims=True) - xh * jnp.mean(dxh * xh, axis=-1, keepdims=True))
        dv_ref[0:1, :] += jnp.sum(dc, axis=0, keepdims=True)
        dv_ref[1:2, :] += jnp.sum(dln * xh, axis=0, keepdims=True)
        dv_ref[2:3, :] += jnp.sum(dln, axis=0, keepdims=True)

        ext_out[0:tm, :] = dc
        _tap_copies(ext_out, cbuf, lambda b: b, rows)
        z = z_ref[...].astype(F32)
        al, sg = z[:, :A_CH], _sigmoid(z[:, A_CH:])
        glu_s[...] = al * sg
        for rs in range(0, tm, SUB):
            for cs in range(0, A_CH, 128):
                glu = glu_s[rs:rs + SUB, cs:cs + 128]
                acc = jnp.zeros((SUB, 128), F32)
                for k in range(A_CONV):
                    lag_a, lag_b = divmod(k, 8)
                    r0 = 8 * lag_a + rs
                    d = (ext_out[r0:r0 + SUB, cs:cs + 128] if lag_b == 0
                         else cbuf[lag_b - 1, r0:r0 + SUB, cs:cs + 128])
                    j = A_CONV - 1 - k
                    acc = acc + w_ref[j:j + 1, cs:cs + 128] * d
                    dw_ref[j:j + 1, cs:cs + 128] += jnp.sum(glu * d, axis=0, keepdims=True)
                dglu_s[rs:rs + SUB, cs:cs + 128] = acc
        dglu = dglu_s[...]
        ext_out[tm:tm + HALO31, :] = ext_out[0:HALO31, :]
        dz_ref[:, 0:A_CH] = (dglu * sg).astype(BF16)
        dz_ref[:, A_CH:2 * A_CH] = (dglu * al * sg * (1.0 - sg)).astype(BF16)

    rrow = lambda w: pl.BlockSpec((tm, w), lambda i: (nt - 1 - i, 0))
    return pl.pallas_call(
        body, name="conf_bwd", grid=(nt,),
        in_specs=[rrow(A_CH), rrow(A_CH), rrow(2 * A_CH), _full((32, A_CH)), _full((1, A_CH)), _full((1, A_CH))],
        out_specs=[rrow(2 * A_CH), _full((32, A_CH)), _full((8, A_CH))],
        out_shape=[jax.ShapeDtypeStruct((t, 2 * A_CH), BF16), jax.ShapeDtypeStruct((32, A_CH), F32),
                   jax.ShapeDtypeStruct((8, A_CH), F32)],
        scratch_shapes=[pltpu.VMEM((tm + HALO31, A_CH), F32), pltpu.VMEM((7, rows, A_CH), F32),
                        pltpu.VMEM((tm, A_CH), F32), pltpu.VMEM((tm, A_CH), F32)],
        compiler_params=_cp(("arbitrary",)),
    )(da, cv, zag, conv_w, ln_g, ln_b)


def _attn_bwd(q, k, v, do, sinks):
    t = q.shape[0]
    nb = min(ATT_NB, t // BLOCK)
    rows = nb * BLOCK
    ns = t // rows

    def body(s_ref, q_ref, kc_ref, kp_ref, vc_ref, vp_ref, do_ref, dq_ref, dk_ref, dv_ref, ds_ref, dkc, dvc):
        i = pl.program_id(0)
        r = ns - 1 - i

        @pl.when(i == 0)
        def _():
            dkc[...] = jnp.zeros_like(dkc)
            dvc[...] = jnp.zeros_like(dvc)
            ds_ref[...] = jnp.zeros_like(ds_ref)

        lane = lax.broadcasted_iota(jnp.int32, (1, N_Q_HEADS), 1)
        dsv = jnp.zeros((1, N_Q_HEADS), F32)
        for b in range(nb - 1, -1, -1):
            lo = BLOCK * b
            mask = _attn_mask(r == 0) if b == 0 else _attn_mask(False)
            qv, dov = q_ref[lo:lo + BLOCK, :], do_ref[lo:lo + BLOCK, :]
            kc, vc = kc_ref[lo:lo + BLOCK, :], vc_ref[lo:lo + BLOCK, :]
            kp = kp_ref[...] if b == 0 else kc_ref[lo - BLOCK:lo, :]
            vp = vp_ref[...] if b == 0 else vc_ref[lo - BLOCK:lo, :]
            for h in range(N_KV_HEADS):
                q4, do4 = _q_heads(qv, h), _q_heads(dov, h)
                k2, v2 = _kv_head(kp, kc, h), _kv_head(vp, vc, h)
                pn, ps = _attn_probs(q4, k2, mask, _sink_rows(s_ref, h))
                dp = lax.dot_general(do4, v2, _CONTRACT_LAST, preferred_element_type=F32)
                dl = jnp.sum(pn * dp, axis=-1, keepdims=True)
                dsb = (pn * (dp - dl)).astype(BF16)
                dq4 = (jnp.dot(dsb, k2, preferred_element_type=F32) * SCALE).astype(BF16)
                for g in range(GROUP):
                    c0 = HEAD_DIM * (GROUP * h + g)
                    dq_ref[lo:lo + BLOCK, c0:c0 + HEAD_DIM] = dq4[BLOCK * g:BLOCK * (g + 1), :]
                dk2 = lax.dot_general(dsb, q4, _CONTRACT_FIRST, preferred_element_type=F32) * SCALE
                dv2 = lax.dot_general(pn.astype(BF16), do4, _CONTRACT_FIRST, preferred_element_type=F32)
                dk_ref[lo:lo + BLOCK, HEAD_DIM * h:HEAD_DIM * (h + 1)] = dk2[BLOCK:, :] + dkc[h]
                dv_ref[lo:lo + BLOCK, HEAD_DIM * h:HEAD_DIM * (h + 1)] = dv2[BLOCK:, :] + dvc[h]
                dkc[h] = dk2[:BLOCK, :]
                dvc[h] = dv2[:BLOCK, :]
                srow = -ps * dl
                for g in range(GROUP):
                    dsv = dsv + jnp.where(lane == GROUP * h + g, jnp.sum(srow[BLOCK * g:BLOCK * (g + 1), :]), 0.0)
        ds_ref[...] += dsv

    cur = lambda n: pl.BlockSpec((rows, n), lambda i: (ns - 1 - i, 0))
    prev = lambda n: pl.BlockSpec((BLOCK, n), lambda i: (jnp.maximum((ns - 1 - i) * nb - 1, 0), 0))
    return pl.pallas_call(
        body, name="attn_bwd", grid=(ns,),
        in_specs=[pl.BlockSpec(memory_space=pltpu.SMEM), cur(Q_DIM), cur(KV_DIM), prev(KV_DIM), cur(KV_DIM),
                  prev(KV_DIM), cur(Q_DIM)],
        out_specs=[cur(Q_DIM), cur(KV_DIM), cur(KV_DIM), _full((1, N_Q_HEADS))],
        out_shape=[jax.ShapeDtypeStruct((t, Q_DIM), BF16), jax.ShapeDtypeStruct((t, KV_DIM), F32),
                   jax.ShapeDtypeStruct((t, KV_DIM), F32), jax.ShapeDtypeStruct((1, N_Q_HEADS), F32)],
        scratch_shapes=[pltpu.VMEM((N_KV_HEADS, BLOCK, HEAD_DIM), F32), pltpu.VMEM((N_KV_HEADS, BLOCK, HEAD_DIM), F32)],
        compiler_params=_cp(("arbitrary",)),
    )(sinks, q, k, k, v, v, do)


def _ev_dz(dzag, dq, dk, dv, rc, rsa, rsb):
    t = dzag.shape[0]
    tm = min(TM, t)

    def body(dzag_ref, dq_ref, dk_ref, dv_ref, c_ref, sa_ref, sb_ref, dz_ref):
        c, sa, sb = c_ref[...], sa_ref[...], sb_ref[...]
        dz_ref[:, 0:2 * A_CH] = dzag_ref[...]
        q0 = 2 * A_CH
        for j in range(Q_DIM // 128):
            d = dq_ref[:, 128 * j:128 * (j + 1)].astype(F32)
            dz_ref[:, q0 + 128 * j:q0 + 128 * (j + 1)] = _rope_bwd(d, c, sa, sb).astype(BF16)
        k0 = q0 + Q_DIM
        dz_ref[:, k0:k0 + KV_DIM] = _rope_bwd(dk_ref[...], c, sa, sb).astype(BF16)
        dz_ref[:, k0 + KV_DIM:k0 + 2 * KV_DIM] = dv_ref[...].astype(BF16)

    return pl.pallas_call(
        body, name="ev_dz", grid=(t // tm,),
        in_specs=[_rows(tm, 2 * A_CH), _rows(tm, Q_DIM), _rows(tm, KV_DIM), _rows(tm, KV_DIM),
                  _rows(tm, 128), _rows(tm, 128), _rows(tm, 128)],
        out_specs=_rows(tm, EVEN_IN),
        out_shape=jax.ShapeDtypeStruct((t, EVEN_IN), BF16),
        compiler_params=_cp(("arbitrary",)),
    )(dzag, dq, dk, dv, rc, rsa, rsb)


def _prep_ev(gat):
    p = {}
    w = gat["ev_w_in"][:, 0].transpose(1, 0, 2).reshape(D_MODEL, EVEN_IN)
    p["ev_w_in"], p["ev_w_in_t"] = w, w.T
    w = gat["ev_w_out"].reshape(A_CH + Q_DIM, D_MODEL)
    p["ev_w_out"], p["ev_w_out_t"] = w, w.T
    return p


def _prep_rest(gat):
    p = {}
    g = gat["od_w_in"][:, 0]
    p["od_w_in"], p["od_w_in_t"] = g, g.transpose(0, 2, 1).reshape(3 * SC_DIM, D_MODEL)
    w = gat["od_w_out"].reshape(SC_DIM, D_MODEL)
    p["od_w_out"], p["od_w_out_t"] = w, w.T
    g = gat["ffn_w_up"]
    p["ffn_w_up"] = g.reshape(2, N_DEV // 2, 2, D_MODEL, FF_N)
    p["ffn_w_up_t"] = [g[:, i].transpose(0, 2, 1).reshape(2, N_DEV // 2, FF_N, D_MODEL) for i in range(2)]
    g = gat["ffn_w_down"]
    p["ffn_w_down"] = [g[:, i].reshape(D_FF, D_MODEL) for i in range(2)]
    p["ffn_w_down_t"] = [w.reshape(N_DEV // 2, FF_N, D_MODEL).transpose(0, 2, 1) for w in p["ffn_w_down"]]
    return p


def _local_step(x, positions, target, p, rest_weights, s, token, grads_ready):
    row = lambda a, tok=None: a.reshape(1, -1) if tok is None else a.reshape(1, -1) + tok
    nc = N_DEV // 2
    rc, rsa, rsb = _rope_tables(positions)
    conv31 = jnp.pad(s["ev_a_conv_w"][0], ((0, 1), (0, 0)))
    cw_ffn = [s["ffn_conv_w"][i].reshape(3, 2, nc, FF_N).transpose(1, 2, 0, 3) for i in range(2)]
    sinks = s["ev_sinks"][0]
    big, g = {}, {}

    h0, zag, q, k, v = _ev_in(x, row(s["mix_norm_pre"][0], token), p["ev_w_in"], rc, rsa, rsb)
    cv, a = _conf_fwd(zag, conv31, s["ev_a_conv_b"], s["ev_a_ln_g"], s["ev_a_ln_b"])
    o = _attn_fwd(q, k, v, sinks)
    wo = p["ev_w_out"]
    m0, x1 = _out_post([a, o], [wo[:A_CH], wo[A_CH:]], x, row(s["mix_norm_post"][0]))
    p = {**p, **rest_weights(m0)}
    h1, up0, u0, f0, x2 = _ffn_fwd(x1, row(s["ffn_norm_pre"][0]), p["ffn_w_up"], 0, cw_ffn[0], p["ffn_w_down"][0],
                                   row(s["ffn_norm_post"][0]))
    h2, z, cv1, y, m1, x3 = _od_fwd(x2, row(s["mix_norm_pre"][1]), p["od_w_in"], s["od_conv_w"][0], p["od_w_out"],
                                    row(s["mix_norm_post"][1]))
    h3, up1, u1, f1, x4 = _ffn_fwd(x3, row(s["ffn_norm_pre"][1]), p["ffn_w_up"], 1, cw_ffn[1], p["ffn_w_down"][1],
                                   row(s["ffn_norm_post"][1]))
    dx, lpart = _loss_grad(x4, target)

    def ffn_back(i, f, dxo, up, u, h, x_in, bufs, tok=None):
        df, act, dup, dx_in, dgpost, dgpre, dcw = _ffn_bwd(
            f, dxo, row(s["ffn_norm_post"][i], tok), x_in, row(s["ffn_norm_pre"][i]), up, u, cw_ffn[i],
            p["ffn_w_down_t"][i], p["ffn_w_up_t"][i])
        bufs = (_dw_up(h, dup.reshape(N_DEV, -1, FF_N), i, bufs[0]), _dw_down(act, df, i, bufs[1]))
        return dx_in, dgpost, dgpre, dcw[:, :, 0:3].transpose(2, 0, 1, 3).reshape(3, 2 * D_FF), bufs

    dx, dgfpost1, dgfpre1, dcw1, bufs = ffn_back(1, f1, dx, up1, u1, h3, x3, (None, None))

    dm1, dz, dx, dgpost1, dgpre1, dcw_od = _od_bwd(m1, dx, row(s["mix_norm_post"][1]), x2, row(s["mix_norm_pre"][1]), z,
                                                   cv1, s["od_conv_w"][0], p["od_w_out_t"], p["od_w_in_t"])
    big["od_w_out"] = _dw2d(y, dm1, SC_DIM, D_MODEL).reshape(N_DEV, -1, D_MODEL)
    big["od_w_in"], big["od_w_in:bf16"] = _dw_cols(h2, dz, 3 * SC_DIM // N_DEV)
    g["od_conv_w"] = dcw_od[None, 0:3]
    tok = grads_ready(["od_w_in", "od_w_out"], big)

    dx, dgfpost0, dgfpre0, dcw0, bufs = ffn_back(0, f0, dx, up0, u0, h1, x1, bufs, tok)
    (big["ffn_w_up"], big["ffn_w_up:bf16"]), (big["ffn_w_down"], big["ffn_w_down:bf16"]) = bufs
    tok = grads_ready(["ffn_w_up", "ffn_w_down"], big)

    dm0, da, do, dgpost0 = _ev_bwd1(m0, dx, row(s["mix_norm_post"][0], tok), p["ev_w_out_t"])
    big["ev_w_out"] = jnp.concatenate([_dw2d(a, dm0, A_CH, D_MODEL), _dw2d(o, dm0, Q_DIM, D_MODEL)],
                                      axis=0).reshape(N_DEV, -1, D_MODEL)
    dzag, dcw31, dvec = _conf_bwd(da, cv, zag, conv31, s["ev_a_ln_g"], s["ev_a_ln_b"])
    dq, dk, dv, dsinks = _attn_bwd(q, k, v, do, sinks)
    dz0 = _ev_dz(dzag, dq, dk, dv, rc, rsa, rsb)
    dw_in = _dw2d(h0, dz0, D_MODEL, EVEN_IN // 2)
    big["ev_w_in"] = dw_in.reshape(D_MODEL, N_DEV, EVEN_IN // N_DEV).transpose(1, 0, 2)
    dx, dgpre0 = _dz_wt_rms_bwd(dz0, p["ev_w_in_t"], x, row(s["mix_norm_pre"][0]), dx)
    grads_ready(["ev_w_in", "ev_w_out"], big)

    g["mix_norm_pre"] = jnp.concatenate([dgpre0, dgpre1], axis=0)
    g["mix_norm_post"] = jnp.concatenate([dgpost0, dgpost1], axis=0)
    g["ffn_norm_pre"] = jnp.concatenate([dgfpre0, dgfpre1], axis=0)
    g["ffn_norm_post"] = jnp.concatenate([dgfpost0, dgfpost1], axis=0)
    g["ev_a_conv_w"] = dcw31[None, 0:A_CONV]
    g["ev_a_conv_b"], g["ev_a_ln_g"], g["ev_a_ln_b"] = dvec[0:1], dvec[1:2], dvec[2:3]
    g["ev_sinks"] = dsinks
    g["ffn_conv_w"] = jnp.stack([dcw0, dcw1])
    return lpart[0, 0], dx, big, g


MESH = pl.DeviceIdType.MESH


def _all_gather(shards, name):
    nw = len(shards)

    def body(*refs):
        x_refs, out_refs = refs[:nw], refs[nw:2 * nw]
        send_sems, recv_sems, local_sems = refs[2 * nw:]
        x, y, c = lax.axis_index("x"), lax.axis_index("y"), lax.axis_index("c")
        me, sibling = (x, y, c), (x, y, 1 - c)
        chips = [(1 - x, y), (x, 1 - y), (1 - x, 1 - y)]

        def rows(w, px, py, pc):
            m_per = shards[w].shape[0]
            return out_refs[w].at[pl.ds((4 * px + 2 * py + pc) * m_per, m_per), :]

        def copy(w, k, block, to, src=None):
            return pltpu.make_async_remote_copy(
                src_ref=rows(w, *block) if src is None else src, dst_ref=rows(w, *block),
                send_sem=send_sems.at[w, k], recv_sem=recv_sems.at[w, k], device_id=to, device_id_type=MESH)

        mine, first, passed = [], [], []
        for w in range(nw):
            cp = pltpu.make_async_copy(x_refs[w], rows(w, *me), local_sems.at[w])
            cp.start()
            mine.append(cp)
            first.append([copy(w, 0, me, sibling, src=x_refs[w])]
                         + [copy(w, 1 + j, me, (*chip, c), src=x_refs[w]) for j, chip in enumerate(chips)])
            for cp in first[w]:
                cp.start()
        for w in range(nw):
            passed.append([copy(w, 4 + j, (*chip, c), sibling) for j, chip in enumerate(chips)])
            for j, chip in enumerate(chips):
                copy(w, 1 + j, (*chip, c), me).wait_recv()
                passed[w][j].start()
        for w in range(nw):
            copy(w, 0, sibling, me).wait_recv()
            for j, chip in enumerate(chips):
                copy(w, 4 + j, (*chip, 1 - c), me).wait_recv()
            for cp in first[w] + passed[w]:
                cp.wait_send()
            mine[w].wait()

    return pl.pallas_call(
        body, name=name,
        out_shape=[jax.ShapeDtypeStruct((N_DEV * a.shape[0], a.shape[1]), a.dtype) for a in shards],
        in_specs=[_ANY] * nw, out_specs=[_ANY] * nw,
        scratch_shapes=[pltpu.SemaphoreType.DMA((nw, 7)), pltpu.SemaphoreType.DMA((nw, 7)),
                        pltpu.SemaphoreType.DMA((nw,))],
    )(*shards)


_HBM = pl.BlockSpec(memory_space=pltpu.HBM)
_SEM = pl.BlockSpec(memory_space=pltpu.SEMAPHORE)
_EFFECT = pltpu.SideEffectType.DATAFLOW_SIDE_EFFECTING
_RELATIONS = [(dx, dy, dc) for dx in (0, 1) for dy in (0, 1) for dc in (0, 1)][1:]


def _peer(rel):
    x, y, c = lax.axis_index("x"), lax.axis_index("y"), lax.axis_index("c")
    px, py, pc = x ^ rel[0], y ^ rel[1], c ^ rel[2]
    return (px, py, pc), 4 * px + 2 * py + pc, 4 * x + 2 * y + c


def _exchange_copy(k, rel, src_ref, land_ref, send_sems, recv_sems, w, scatter):
    peer, peer_idx, my_idx = _peer(rel)
    src = src_ref.at[peer_idx] if scatter else src_ref
    return pltpu.make_async_remote_copy(
        src_ref=src, dst_ref=land_ref.at[my_idx], send_sem=send_sems.at[_sends(scatter) * w + k],
        recv_sem=recv_sems.at[7 * w + k], device_id=peer, device_id_type=MESH)


def _sends(scatter):
    return 7 if scatter else 8


def _own_copy(src_ref, land_ref, send_sems, w):
    my_idx = _peer(_RELATIONS[0])[2]
    return pltpu.make_async_copy(src_ref, land_ref.at[my_idx], send_sems.at[8 * w + 7])


def _exchange_start(srcs, scatter, name):
    nw = len(srcs)
    lands = [lax.empty((N_DEV,) + (a.shape[1:] if scatter else a.shape), a.dtype) for a in srcs]

    def body(*refs):
        src_refs, land_refs = refs[:nw], refs[nw:2 * nw]
        send_sems, recv_sems = refs[2 * nw], refs[2 * nw + 1]
        token = refs[-1]
        for w in range(nw):
            for k, rel in enumerate(_RELATIONS):
                _exchange_copy(k, rel, src_refs[w], land_refs[w], send_sems, recv_sems, w, scatter).start()
            if not scatter:
                _own_copy(src_refs[w], land_refs[w], send_sems, w).start()
        token[...] = jnp.zeros_like(token)

    hbm = lambda a: pltpu.HBM(a.shape, a.dtype)
    outs = pl.pallas_call(
        body, name=name,
        out_shape=(pltpu.SemaphoreType.DMA((_sends(scatter) * nw,)), pltpu.SemaphoreType.DMA((7 * nw,)),
                   *[hbm(a) for a in srcs],
                   *[hbm(a) for a in lands], jax.ShapeDtypeStruct((8, 128), F32)),
        in_specs=[_HBM] * (2 * nw),
        out_specs=(_SEM, _SEM, *[_HBM] * (2 * nw), pl.BlockSpec(memory_space=pltpu.VMEM)),
        input_output_aliases={i: 2 + i for i in range(2 * nw)},
        compiler_params=pltpu.CompilerParams(has_side_effects=_EFFECT),
    )(*[pltpu.with_memory_space_constraint(a, pltpu.HBM) for a in srcs],
      *[pltpu.with_memory_space_constraint(a, pltpu.HBM) for a in lands])
    return outs[0], outs[1], list(outs[2:2 + nw]), list(outs[2 + nw:2 + 2 * nw]), outs[-1]


def _exchange_wait(started, scatter, after, name):
    send_sems, recv_sems, srcs, lands, _ = started
    nw = len(srcs)

    def body(*refs):
        src_refs, land_refs = refs[:nw], refs[nw:2 * nw]
        send_s, recv_s = refs[2 * nw], refs[2 * nw + 1]
        for w in range(nw):
            for k, rel in enumerate(_RELATIONS):
                cp = _exchange_copy(k, rel, src_refs[w], land_refs[w], send_s, recv_s, w, scatter)
                cp.wait_send()
                _, peer_idx, _ = _peer(rel)
                pltpu.make_async_remote_copy(
                    src_ref=src_refs[w].at[peer_idx] if scatter else src_refs[w], dst_ref=land_refs[w].at[peer_idx],
                    send_sem=send_s.at[_sends(scatter) * w + k], recv_sem=recv_s.at[7 * w + k],
                    device_id=_peer(rel)[0], device_id_type=MESH).wait_recv()
            if not scatter:
                _own_copy(src_refs[w], land_refs[w], send_s, w).wait()

    hbm = lambda a: pltpu.HBM(a.shape, a.dtype)
    outs = pl.pallas_call(
        body, name=name, out_shape=tuple(hbm(a) for a in srcs + lands),
        in_specs=[_HBM] * (2 * nw) + [_SEM, _SEM, _ANY], out_specs=tuple([_HBM] * (2 * nw)),
        input_output_aliases={i: i for i in range(2 * nw)},
        compiler_params=pltpu.CompilerParams(has_side_effects=_EFFECT),
    )(*srcs, *lands, send_sems, recv_sems, after)
    return list(outs[nw:])


def _to_bf16(a):
    _, r, l = a.shape
    tr = _row_tile(r, 512)

    def body(a_ref, o_ref):
        o_ref[...] = a_ref[...].astype(BF16)

    spec = pl.BlockSpec((1, tr, l), lambda j, i: (j, i, 0))
    return pl.pallas_call(
        body, name="to_bf16", grid=(N_DEV, r // tr), in_specs=[spec], out_specs=spec,
        out_shape=jax.ShapeDtypeStruct(a.shape, BF16), compiler_params=_cp(("arbitrary", "arbitrary")),
    )(a)


def _row_tile(rows, cap):
    best = None
    for d in range(16, min(rows, cap) + 1, 16):
        if rows % d == 0:
            best = d
    return rows if best is None else best


def _adam_math(w, g, m, v):
    bc1 = 1.0 - ADAM_B1 ** ADAM_STEP
    bc2 = 1.0 - ADAM_B2 ** ADAM_STEP
    mn = ADAM_B1 * m + (1.0 - ADAM_B1) * g
    vn = ADAM_B2 * v + (1.0 - ADAM_B2) * (g * g)
    return -ADAM_LR * ((mn / bc1) / (jnp.sqrt(vn / bc2) + ADAM_EPS) + ADAM_WD * w), mn, vn


def _adamw_rs(gp, land, w, m, v, dev):
    _, r, l = gp.shape
    tr = _row_tile(r, 256)

    def body(i_ref, g_ref, b_ref, w_ref, m_ref, v_ref, go_ref, d_ref, mo_ref, vo_ref):
        g = g_ref[0]
        for j in range(N_DEV):
            g = g + jnp.where(i_ref[0] == j, 0.0, b_ref[j].astype(F32))
        go_ref[...] = g
        d_ref[...], mo_ref[...], vo_ref[...] = _adam_math(w_ref[...], g, m_ref[...], v_ref[...])

    spec = pl.BlockSpec((tr, l), lambda i, s: (i, 0))
    return pl.pallas_call(
        body, name="adamw_rs", out_shape=[jax.ShapeDtypeStruct((r, l), F32)] * 4,
        grid_spec=pltpu.PrefetchScalarGridSpec(
            num_scalar_prefetch=1, grid=(r // tr,),
            in_specs=[pl.BlockSpec((1, tr, l), lambda i, s: (s[0], i, 0)),
                      pl.BlockSpec((N_DEV, tr, l), lambda i, s: (0, i, 0)), spec, spec, spec],
            out_specs=[spec] * 4),
        compiler_params=_cp(("arbitrary",)),
    )(dev, gp, land, w, m, v)


def _sum_blocks(a, nblk):
    m = a.shape[0] // nblk
    n = a.shape[1]

    def body(a_ref, o_ref):
        acc = a_ref[0]
        for j in range(1, nblk):
            acc = acc + a_ref[j]
        o_ref[...] = acc

    return pl.pallas_call(
        body, name="sum_blocks", out_shape=jax.ShapeDtypeStruct((m, n), a.dtype),
        in_specs=[_full((nblk, m, n))], out_specs=_full((m, n)),
    )(a.reshape(nblk, m, n))


def _adamw(w, g, m, v):
    rows, c = w.shape

    def body(w_ref, g_ref, m_ref, v_ref, d_ref, mo_ref, vo_ref):
        d_ref[...], mo_ref[...], vo_ref[...] = _adam_math(w_ref[...], g_ref[...], m_ref[...], v_ref[...])

    return pl.pallas_call(
        body, name="adamw", in_specs=[_full((rows, c))] * 4, out_specs=[_full((rows, c))] * 3,
        out_shape=[jax.ShapeDtypeStruct((rows, c), F32)] * 3,
    )(w, g, m, v)


WEIGHTS = ["mix_norm_pre", "mix_norm_post", "ffn_norm_pre", "ffn_norm_post", "ev_w_in", "ev_a_conv_w", "ev_a_conv_b",
           "ev_a_ln_g", "ev_a_ln_b", "ev_sinks", "ev_w_out", "od_w_in", "od_conv_w", "od_w_out", "ffn_w_up",
           "ffn_conv_w", "ffn_w_down"]
BIG = ["ev_w_in", "ev_w_out", "od_w_in", "od_w_out", "ffn_w_up", "ffn_w_down"]
SMALL_REPL = ["mix_norm_pre", "mix_norm_post", "ffn_norm_pre", "ffn_norm_post", "ev_a_conv_b", "ev_a_ln_g",
              "ev_a_ln_b", "ev_sinks"]
SMALL_SHARDED = ["ev_a_conv_w", "od_conv_w", "ffn_conv_w"]


def _pack(arrs, rows):
    flat = jnp.concatenate([a.reshape(-1) for a in arrs])
    return jnp.pad(flat, (0, rows * LANES - flat.shape[0])).reshape(rows, LANES)


def _unpack(packed, shapes):
    flat, out, off = packed.reshape(-1), [], 0
    for s in shapes:
        n = 1
        for d in s:
            n *= d
        out.append(flat[off:off + n].reshape(s))
        off += n
    return out


def kernel(x, positions, mix_norm_pre, mix_norm_post, ffn_norm_pre, ffn_norm_post, ev_w_in, ev_a_conv_w, ev_a_conv_b, ev_a_ln_g, ev_a_ln_b, ev_sinks, ev_w_out, od_w_in, od_conv_w, od_w_out, ffn_w_up, ffn_conv_w, ffn_w_down, loss_target, m_mix_norm_pre, m_mix_norm_post, m_ffn_norm_pre, m_ffn_norm_post, m_ev_w_in, m_ev_a_conv_w, m_ev_a_conv_b, m_ev_a_ln_g, m_ev_a_ln_b, m_ev_sinks, m_ev_w_out, m_od_w_in, m_od_conv_w, m_od_w_out, m_ffn_w_up, m_ffn_conv_w, m_ffn_w_down, v_mix_norm_pre, v_mix_norm_post, v_ffn_norm_pre, v_ffn_norm_post, v_ev_w_in, v_ev_a_conv_w, v_ev_a_conv_b, v_ev_a_ln_g, v_ev_a_ln_b, v_ev_sinks, v_ev_w_out, v_od_w_in, v_od_conv_w, v_od_w_out, v_ffn_w_up, v_ffn_conv_w, v_ffn_w_down):
    w = dict(zip(WEIGHTS, (mix_norm_pre, mix_norm_post, ffn_norm_pre, ffn_norm_post, ev_w_in, ev_a_conv_w, ev_a_conv_b,
                           ev_a_ln_g, ev_a_ln_b, ev_sinks, ev_w_out, od_w_in, od_conv_w, od_w_out, ffn_w_up, ffn_conv_w,
                           ffn_w_down)))
    mom = dict(zip(WEIGHTS, (m_mix_norm_pre, m_mix_norm_post, m_ffn_norm_pre, m_ffn_norm_post, m_ev_w_in, m_ev_a_conv_w,
                             m_ev_a_conv_b, m_ev_a_ln_g, m_ev_a_ln_b, m_ev_sinks, m_ev_w_out, m_od_w_in, m_od_conv_w,
                             m_od_w_out, m_ffn_w_up, m_ffn_conv_w, m_ffn_w_down)))
    var = dict(zip(WEIGHTS, (v_mix_norm_pre, v_mix_norm_post, v_ffn_norm_pre, v_ffn_norm_post, v_ev_w_in, v_ev_a_conv_w,
                             v_ev_a_conv_b, v_ev_a_ln_g, v_ev_a_ln_b, v_ev_sinks, v_ev_w_out, v_od_w_in, v_od_conv_w,
                             v_od_w_out, v_ffn_w_up, v_ffn_conv_w, v_ffn_w_down)))
    ix, iy, ic = lax.axis_index("x"), lax.axis_index("y"), lax.axis_index("c")
    dev = 4 * ix + 2 * iy + ic
    two = lambda a: a.reshape(-1, a.shape[-1])

    dev1 = jnp.reshape(dev, (1,)).astype(jnp.int32)
    shard = {n: two(w[n].astype(BF16)) for n in BIG}
    gathered = lambda n, a: a.reshape((N_DEV,) + w[n].shape)
    ev_names = [n for n in BIG if n.startswith("ev_")]
    ev_gat = _all_gather([shard[n] for n in ev_names] + [_pack([w[n] for n in SMALL_SHARDED], 8)], "gather_ev")
    p = _prep_ev({n: gathered(n, a) for n, a in zip(ev_names, ev_gat)})
    rest_names = [n for n in BIG if not n.startswith("ev_")]
    first = shard[rest_names[0]] + (ev_gat[0][0:1, 0:1] * 0).astype(BF16)
    started = _exchange_start([first] + [shard[n] for n in rest_names[1:]], False, "gather_start")

    def rest_weights(after):
        lands = _exchange_wait(started, False, after, "gather_wait")
        return _prep_rest({n: gathered(n, a) for n, a in zip(rest_names, lands)})

    small = {n: w[n] for n in SMALL_REPL}
    small_shapes = [w[n].shape for n in SMALL_SHARDED]
    conv_gat = ev_gat[len(ev_names)].reshape(N_DEV, 8, LANES)
    per_dev = [_unpack(conv_gat[d], small_shapes) for d in range(N_DEV)]
    for k, n in enumerate(SMALL_SHARDED):
        small[n] = jnp.concatenate([per_dev[d][k] for d in range(N_DEV)], axis=-1)

    exchanges = []

    def grads_ready(names, big):
        blocks = lambda a, n: a.reshape(N_DEV, -1, w[n].shape[-1])
        bufs = [blocks(big[n], n) for n in names]
        payload = [blocks(big[n + ":bf16"], n) if n + ":bf16" in big else _to_bf16(b) for n, b in zip(names, bufs)]
        st = _exchange_start(payload, True, "grads_start_" + names[0])
        exchanges.append((names, bufs, st))
        return st[-1][0, 0]

    lpart, grad_x, big, g = _local_step(x[0], positions[0], loss_target[0], p, rest_weights, small, started[-1][0, 0],
                                        grads_ready)
    loss = lax.psum(lpart, ("x", "y", "c"))

    grads, delta, new_m, new_v = {}, {}, {}, {}
    for names, bufs, st in exchanges:
        lands = _exchange_wait(st, True, grad_x, "grads_wait_" + names[0])
        for n, b, land in zip(names, bufs, lands):
            outs = _adamw_rs(b, land, two(w[n]), two(mom[n]), two(var[n]), dev1)
            grads[n], delta[n], new_m[n], new_v[n] = (a.reshape(w[n].shape) for a in outs)

    small_names = SMALL_REPL + SMALL_SHARDED
    s_all = _sum_blocks(_all_gather([_pack([g[n] for n in small_names], 64)], "gather_small_grads")[0], N_DEV)
    for n, a in zip(small_names, _unpack(s_all, [small[n].shape for n in small_names])):
        if n in SMALL_SHARDED:
            width = w[n].shape[-1]
            a = lax.dynamic_slice_in_dim(a, dev * width, width, axis=a.ndim - 1)
        grads[n] = a
    pk = lambda dct: _pack([dct[n] for n in small_names], 16)
    outs = _adamw(pk(w), pk(grads), pk(mom), pk(var))
    for dst, packed in zip((delta, new_m, new_v), outs):
        for n, a in zip(small_names, _unpack(packed, [w[n].shape for n in small_names])):
            dst[n] = a

    return (loss, grad_x[None], *[grads[n] for n in WEIGHTS], *[delta[n] for n in WEIGHTS],
            *[new_m[n] for n in WEIGHTS], *[new_v[n] for n in WEIGHTS])
```

```python
import jax
import jax.numpy as jnp
from jax import lax
from jax.experimental import pallas as pl
from jax.experimental.pallas import tpu as pltpu

F32, BF16 = jnp.float32, jnp.bfloat16

D_MODEL = 1024
A_CH = 512
A_CONV = 31
Q_DIM = 512
KV_DIM = 128
HEAD_DIM = 64
N_Q_HEADS = 8
N_KV_HEADS = 2
GROUP = 4
BLOCK = 128
EVEN_IN = 1792
SC_DIM = 1024
D_FF = 2816
ROPE_THETA = 500000.0
ROPE_DIM = 16
RMS_EPS = 1e-6
LN_EPS = 1e-5
SCALE = HEAD_DIM ** -0.5
NEG = -1e30

ADAM_LR, ADAM_B1, ADAM_B2, ADAM_EPS, ADAM_WD, ADAM_STEP = 0.001, 0.9, 0.999, 1e-08, 0.01, 10

N_DEV = 8
FF_N = 2 * D_FF // N_DEV
LANES = 1024
HALO3 = 8
HALO31 = 32
VMEM_LIMIT = 56 * 1024 * 1024

TM = 512
TM_BWD = 256
TK_DW = 2048
ATT_NB = 4
SUB = 128

_ANY = pl.BlockSpec(memory_space=pl.ANY)
_CONTRACT_LAST = (((1,), (1,)), ((), ()))
_CONTRACT_FIRST = (((0,), (0,)), ((), ()))


def _cp(sem, vmem=VMEM_LIMIT):
    return pltpu.CompilerParams(dimension_semantics=sem, vmem_limit_bytes=vmem)


def _full(shape):
    n = len(shape)
    return pl.BlockSpec(shape, lambda *_: (0,) * n)


def _rows(tm, n):
    return pl.BlockSpec((tm, n), lambda i, *_: (i, 0))


def _sigmoid(x):
    return 0.5 * jnp.tanh(0.5 * x) + 0.5


def _rsqrt_mean(x):
    return lax.rsqrt(jnp.mean(x * x, axis=-1, keepdims=True) + RMS_EPS)


def _rms_bwd(x, g, dy):
    r = _rsqrt_mean(x)
    xh = x * r
    dxh = dy * g
    dx = r * (dxh - xh * jnp.mean(dxh * xh, axis=-1, keepdims=True))
    return dx, jnp.sum(dy * xh, axis=0, keepdims=True)


def _acc_out(ref, first, val):
    @pl.when(first)
    def _():
        ref[...] = val

    @pl.when(jnp.logical_not(first))
    def _():
        ref[...] += val


def _rope_tables(positions):
    half = ROPE_DIM // 2
    inv_freq = ROPE_THETA ** (-(jnp.arange(half, dtype=F32) * 2.0 / ROPE_DIM))
    ang = positions.astype(F32)[:, None] * inv_freq
    cos, sin = jnp.cos(ang), jnp.sin(ang)
    t = positions.shape[0]
    one, zero = jnp.ones((t, HEAD_DIM - ROPE_DIM), F32), jnp.zeros((t, HEAD_DIM - ROPE_DIM), F32)
    z8 = jnp.zeros((t, half), F32)
    c = jnp.concatenate([cos, cos, one], axis=1)
    sa = jnp.concatenate([z8, sin, zero], axis=1)
    sb = jnp.concatenate([-sin, z8, zero], axis=1)
    return tuple(jnp.tile(a, (1, 2)) for a in (c, sa, sb))


def _rope(t, c, sa, sb):
    return t * c + pltpu.roll(t, 8, 1) * sa + pltpu.roll(t, 120, 1) * sb


def _rope_bwd(d, c, sa, sb):
    return d * c + pltpu.roll(d * sa, 120, 1) + pltpu.roll(d * sb, 8, 1)


def _ev_in(x, gpre, w_in, rc, rsa, rsb):
    t = x.shape[0]
    tm = min(TM, t)

    def body(x_ref, g_ref, w_ref, c_ref, sa_ref, sb_ref, h_ref, zag_ref, q_ref, k_ref, v_ref):
        xv = x_ref[...]
        h = (xv * _rsqrt_mean(xv) * g_ref[...]).astype(BF16)
        h_ref[...] = h
        z = jnp.dot(h, w_ref[...], preferred_element_type=F32)
        zag_ref[...] = z[:, :2 * A_CH].astype(BF16)
        c, sa, sb = c_ref[...], sa_ref[...], sb_ref[...]
        q0 = 2 * A_CH
        for j in range(Q_DIM // 128):
            q_ref[:, 128 * j:128 * (j + 1)] = _rope(z[:, q0 + 128 * j:q0 + 128 * (j + 1)], c, sa, sb).astype(BF16)
        k0 = q0 + Q_DIM
        k_ref[...] = _rope(z[:, k0:k0 + KV_DIM], c, sa, sb).astype(BF16)
        v_ref[...] = z[:, k0 + KV_DIM:k0 + 2 * KV_DIM].astype(BF16)

    return pl.pallas_call(
        body, name="ev_in", grid=(t // tm,),
        in_specs=[_rows(tm, D_MODEL), _full((1, D_MODEL)), _full((D_MODEL, EVEN_IN)),
                  _rows(tm, 128), _rows(tm, 128), _rows(tm, 128)],
        out_specs=[_rows(tm, D_MODEL), _rows(tm, 2 * A_CH), _rows(tm, Q_DIM), _rows(tm, KV_DIM), _rows(tm, KV_DIM)],
        out_shape=[jax.ShapeDtypeStruct((t, D_MODEL), BF16), jax.ShapeDtypeStruct((t, 2 * A_CH), BF16),
                   jax.ShapeDtypeStruct((t, Q_DIM), BF16), jax.ShapeDtypeStruct((t, KV_DIM), BF16),
                   jax.ShapeDtypeStruct((t, KV_DIM), BF16)],
        compiler_params=_cp(("arbitrary",)),
    )(x, gpre, w_in, rc, rsa, rsb)


def _glu(zag):
    z = zag.astype(F32)
    return z[:, :A_CH] * _sigmoid(z[:, A_CH:])


def _tap_copies(ext, cbuf, first_row, rows):
    for b in range(1, 8):
        s = first_row(b)
        cbuf[b - 1] = ext[s:s + rows, :]


def _conf_fwd(zag, conv_w, conv_b, ln_g, ln_b):
    t = zag.shape[0]
    tm = min(TM_BWD, t)
    rows = tm + HALO31 - 8

    def body(z_ref, w_ref, b_ref, g_ref, lb_ref, c_ref, a_ref, ext, cbuf):
        i = pl.program_id(0)

        @pl.when(i == 0)
        def _():
            ext[0:HALO31, :] = jnp.zeros((HALO31, A_CH), F32)

        ext[HALO31:HALO31 + tm, :] = _glu(z_ref[...])
        _tap_copies(ext, cbuf, lambda b: 8 - b, rows)
        for rs in range(0, tm, SUB):
            for cs in range(0, A_CH, 128):
                acc = jnp.zeros((SUB, 128), F32)
                for k in range(A_CONV):
                    lag_a, lag_b = divmod(k, 8)
                    r0 = HALO31 - 8 - 8 * lag_a + rs
                    src = (ext[r0 + 8:r0 + 8 + SUB, cs:cs + 128] if lag_b == 0
                           else cbuf[lag_b - 1, r0:r0 + SUB, cs:cs + 128])
                    acc = acc + w_ref[A_CONV - 1 - k:A_CONV - k, cs:cs + 128] * src
                c_ref[rs:rs + SUB, cs:cs + 128] = acc
        ext[0:HALO31, :] = ext[tm:tm + HALO31, :]
        cv = c_ref[...] + b_ref[...]
        c_ref[...] = cv
        mu = jnp.mean(cv, axis=-1, keepdims=True)
        xc = cv - mu
        ln = xc * lax.rsqrt(jnp.mean(xc * xc, axis=-1, keepdims=True) + LN_EPS) * g_ref[...] + lb_ref[...]
        a_ref[...] = (ln * _sigmoid(ln)).astype(BF16)

    return pl.pallas_call(
        body, name="conf_fwd", grid=(t // tm,),
        in_specs=[_rows(tm, 2 * A_CH), _full((32, A_CH)), _full((1, A_CH)), _full((1, A_CH)), _full((1, A_CH))],
        out_specs=[_rows(tm, A_CH), _rows(tm, A_CH)],
        out_shape=[jax.ShapeDtypeStruct((t, A_CH), F32), jax.ShapeDtypeStruct((t, A_CH), BF16)],
        scratch_shapes=[pltpu.VMEM((HALO31 + tm, A_CH), F32), pltpu.VMEM((7, rows, A_CH), F32)],
        compiler_params=_cp(("arbitrary",)),
    )(zag, conv_w, conv_b, ln_g, ln_b)


def _attn_mask(first_block):
    row = lax.broadcasted_iota(jnp.int32, (GROUP * BLOCK, 2 * BLOCK), 0) & (BLOCK - 1)
    col = lax.broadcasted_iota(jnp.int32, (GROUP * BLOCK, 2 * BLOCK), 1)
    diff = row + BLOCK - col
    return (diff >= 0) & (diff < BLOCK) & ((col >= BLOCK) | jnp.logical_not(first_block))


def _sink_rows(s_ref, h):
    grp = lax.broadcasted_iota(jnp.int32, (GROUP * BLOCK, 1), 0) >> 7
    out = jnp.full((GROUP * BLOCK, 1), s_ref[GROUP * h], F32)
    for g in range(1, GROUP):
        out = jnp.where(grp == g, s_ref[GROUP * h + g], out)
    return out


def _attn_probs(q4, k2, mask, sink):
    s = lax.dot_general(q4, k2, _CONTRACT_LAST, preferred_element_type=F32) * SCALE
    s = jnp.where(mask, s, NEG)
    m = jnp.maximum(jnp.max(s, axis=-1, keepdims=True), sink)
    p = jnp.exp(s - m)
    es = jnp.exp(sink - m)
    inv = 1.0 / (jnp.sum(p, axis=-1, keepdims=True) + es)
    return p * inv, es * inv


def _q_heads(q, h):
    return jnp.concatenate([q[:, HEAD_DIM * (GROUP * h + g):HEAD_DIM * (GROUP * h + g + 1)] for g in range(GROUP)],
                           axis=0)


def _kv_head(prev, cur, h):
    return jnp.concatenate([prev[:, HEAD_DIM * h:HEAD_DIM * (h + 1)], cur[:, HEAD_DIM * h:HEAD_DIM * (h + 1)]], axis=0)


def _attn_fwd(q, k, v, sinks):
    t = q.shape[0]
    nb = min(ATT_NB, t // BLOCK)
    rows = nb * BLOCK

    def body(s_ref, q_ref, kc_ref, kp_ref, vc_ref, vp_ref, o_ref):
        first = pl.program_id(0) == 0
        for b in range(nb):
            lo = BLOCK * b
            mask = _attn_mask(first) if b == 0 else _attn_mask(False)
            qv, kc, vc = q_ref[lo:lo + BLOCK, :], kc_ref[lo:lo + BLOCK, :], vc_ref[lo:lo + BLOCK, :]
            kp = kp_ref[...] if b == 0 else kc_ref[lo - BLOCK:lo, :]
            vp = vp_ref[...] if b == 0 else vc_ref[lo - BLOCK:lo, :]
            for h in range(N_KV_HEADS):
                pn, _ = _attn_probs(_q_heads(qv, h), _kv_head(kp, kc, h), mask, _sink_rows(s_ref, h))
                o4 = jnp.dot(pn.astype(BF16), _kv_head(vp, vc, h), preferred_element_type=F32).astype(BF16)
                for g in range(GROUP):
                    c0 = HEAD_DIM * (GROUP * h + g)
                    o_ref[lo:lo + BLOCK, c0:c0 + HEAD_DIM] = o4[BLOCK * g:BLOCK * (g + 1), :]

    cur = lambda n: pl.BlockSpec((rows, n), lambda i: (i, 0))
    prev = lambda n: pl.BlockSpec((BLOCK, n), lambda i: (jnp.maximum(i * nb - 1, 0), 0))
    return pl.pallas_call(
        body, name="attn_fwd", grid=(t // rows,),
        in_specs=[pl.BlockSpec(memory_space=pltpu.SMEM), cur(Q_DIM), cur(KV_DIM), prev(KV_DIM), cur(KV_DIM),
                  prev(KV_DIM)],
        out_specs=cur(Q_DIM),
        out_shape=jax.ShapeDtypeStruct((t, Q_DIM), BF16),
        compiler_params=_cp(("arbitrary",)),
    )(sinks, q, k, k, v, v)


def _out_post(lhs, ws, x_in, gpost):
    t = x_in.shape[0]
    tm = min(TM, t)
    n = len(lhs)

    def body(*refs):
        x_ref, g_ref, m_ref, xo_ref = refs[2 * n:]
        m = jnp.dot(refs[0][...], refs[n][...], preferred_element_type=F32)
        for j in range(1, n):
            m = m + jnp.dot(refs[j][...], refs[n + j][...], preferred_element_type=F32)
        m_ref[...] = m
        xo_ref[...] = x_ref[...] + m * _rsqrt_mean(m) * g_ref[...]

    return pl.pallas_call(
        body, name="out_post", grid=(t // tm,),
        in_specs=[_rows(tm, a.shape[1]) for a in lhs] + [_full(w.shape) for w in ws]
                 + [_rows(tm, D_MODEL), _full((1, D_MODEL))],
        out_specs=[_rows(tm, D_MODEL), _rows(tm, D_MODEL)],
        out_shape=[jax.ShapeDtypeStruct((t, D_MODEL), F32)] * 2,
        compiler_params=_cp(("arbitrary",)),
    )(*lhs, *ws, x_in, gpost)


def _conv3(w_ref, ext, tm):
    s = HALO3 - 2
    return (w_ref[0:1, :] * ext[s:s + tm, :] + w_ref[1:2, :] * ext[s + 1:s + 1 + tm, :]
            + w_ref[2:3, :] * ext[s + 2:s + 2 + tm, :])


def _ffn_fwd(x1, gpre, wup, layer, cw, wd, gpost):
    t = x1.shape[0]
    tm = min(TM, t)
    nc, n = wup.shape[1], wup.shape[4]

    def body(x_ref, gpre_ref, wup_ref, cw_ref, wd_ref, gpost_ref, h_ref, up_ref, u_ref, f_ref, xo_ref, h_s, acc, ext, hal):
        i, c = pl.program_id(0), pl.program_id(1)

        @pl.when(c == 0)
        def _():
            xv = x_ref[...]
            h = (xv * _rsqrt_mean(xv) * gpre_ref[...]).astype(BF16)
            h_s[...] = h
            h_ref[...] = h

        @pl.when(i == 0)
        def _():
            hal[c] = jnp.zeros((2, HALO3, n), F32)

        u = []
        for gv in range(2):
            up = jnp.dot(h_s[...], wup_ref[gv, 0, 0], preferred_element_type=F32)
            up_ref[gv, 0] = up.astype(BF16)
            ext[gv, 0:HALO3, :] = hal[c, gv]
            ext[gv, HALO3:HALO3 + tm, :] = up
            hal[c, gv] = ext[gv, tm:tm + HALO3, :]
            s = HALO3 - 2
            u.append(cw_ref[gv, 0, 0:1, :] * ext[gv, s:s + tm, :] + cw_ref[gv, 0, 1:2, :] * ext[gv, s + 1:s + 1 + tm, :]
                     + cw_ref[gv, 0, 2:3, :] * up)
            u_ref[gv, 0] = u[gv].astype(BF16)
        act = (u[0] * _sigmoid(u[0]) * u[1]).astype(BF16)
        part = jnp.dot(act, wd_ref[...], preferred_element_type=F32)

        @pl.when(c == 0)
        def _():
            acc[...] = part

        @pl.when(jnp.logical_and(c > 0, c < nc - 1))
        def _():
            acc[...] += part

        @pl.when(c == nc - 1)
        def _():
            f = acc[...] + part
            f_ref[...] = f
            xo_ref[...] = x_ref[...] + f * _rsqrt_mean(f) * gpost_ref[...]

    row = lambda w: pl.BlockSpec((tm, w), lambda i, c: (i, 0))
    one = _full((1, D_MODEL))
    return pl.pallas_call(
        body, name="ffn_fwd", grid=(t // tm, nc),
        in_specs=[row(D_MODEL), one, pl.BlockSpec((2, 1, 1, D_MODEL, n), lambda i, c: (0, c, layer, 0, 0)),
                  pl.BlockSpec((2, 1, 3, n), lambda i, c: (0, c, 0, 0)), pl.BlockSpec((n, D_MODEL), lambda i, c: (c, 0)),
                  one],
        out_specs=[row(D_MODEL), pl.BlockSpec((2, 1, tm, n), lambda i, c: (0, c, i, 0)),
                   pl.BlockSpec((2, 1, tm, n), lambda i, c: (0, c, i, 0)), row(D_MODEL), row(D_MODEL)],
        out_shape=[jax.ShapeDtypeStruct((t, D_MODEL), BF16), jax.ShapeDtypeStruct((2, nc, t, n), BF16),
                   jax.ShapeDtypeStruct((2, nc, t, n), BF16), jax.ShapeDtypeStruct((t, D_MODEL), F32),
                   jax.ShapeDtypeStruct((t, D_MODEL), F32)],
        scratch_shapes=[pltpu.VMEM((tm, D_MODEL), BF16), pltpu.VMEM((tm, D_MODEL), F32),
                        pltpu.VMEM((2, HALO3 + tm, n), F32), pltpu.VMEM((nc, 2, HALO3, n), F32)],
        compiler_params=_cp(("arbitrary", "arbitrary")),
    )(x1, gpre, wup, cw, wd, gpost)


def _od_fwd(x_in, gpre, w_in, cw, w_out, gpost):
    t = x_in.shape[0]
    tm = min(TM, t)
    ns, _, n = w_in.shape

    def body(x_ref, gpre_ref, w_ref, cw_ref, wo_ref, gpost_ref, h_ref, z_ref, cv_ref, y_ref, m_ref, xo_ref, z_s, ext):
        i = pl.program_id(0)
        xv = x_ref[...]
        h = (xv * _rsqrt_mean(xv) * gpre_ref[...]).astype(BF16)
        h_ref[...] = h
        for j in range(ns):
            z_s[:, n * j:n * (j + 1)] = jnp.dot(h, w_ref[j], preferred_element_type=F32)
        z_ref[...] = z_s[...].astype(BF16)

        @pl.when(i == 0)
        def _():
            ext[0:HALO3, :] = jnp.zeros((HALO3, SC_DIM), F32)

        ext[HALO3:HALO3 + tm, :] = z_s[:, SC_DIM:2 * SC_DIM] * z_s[:, 2 * SC_DIM:]
        cv = _conv3(cw_ref, ext, tm)
        cv_ref[...] = cv.astype(BF16)
        y = (z_s[:, :SC_DIM] * cv).astype(BF16)
        ext[0:HALO3, :] = ext[tm:tm + HALO3, :]
        y_ref[...] = y
        m = jnp.dot(y, wo_ref[...], preferred_element_type=F32)
        m_ref[...] = m
        xo_ref[...] = xv + m * _rsqrt_mean(m) * gpost_ref[...]

    return pl.pallas_call(
        body, name="od_fwd", grid=(t // tm,),
        in_specs=[_rows(tm, D_MODEL), _full((1, D_MODEL)), _full((ns, D_MODEL, n)), _full((3, SC_DIM)),
                  _full((SC_DIM, D_MODEL)), _full((1, D_MODEL))],
        out_specs=[_rows(tm, D_MODEL), _rows(tm, 3 * SC_DIM), _rows(tm, SC_DIM), _rows(tm, SC_DIM), _rows(tm, D_MODEL),
                   _rows(tm, D_MODEL)],
        out_shape=[jax.ShapeDtypeStruct((t, D_MODEL), BF16), jax.ShapeDtypeStruct((t, 3 * SC_DIM), BF16),
                   jax.ShapeDtypeStruct((t, SC_DIM), BF16), jax.ShapeDtypeStruct((t, SC_DIM), BF16),
                   jax.ShapeDtypeStruct((t, D_MODEL), F32), jax.ShapeDtypeStruct((t, D_MODEL), F32)],
        scratch_shapes=[pltpu.VMEM((tm, 3 * SC_DIM), F32), pltpu.VMEM((HALO3 + tm, SC_DIM), F32)],
        compiler_params=_cp(("arbitrary",)),
    )(x_in, gpre, w_in, cw, w_out, gpost)


def _dw2d(a, b, bm, bn):
    t, m = a.shape
    n = b.shape[1]
    tk = min(TK_DW, t)

    def body(a_ref, b_ref, o_ref):
        part = lax.dot_general(a_ref[...], b_ref[...], _CONTRACT_FIRST, preferred_element_type=F32)
        _acc_out(o_ref, pl.program_id(2) == 0, part)

    return pl.pallas_call(
        body, name="dw2d", grid=(m // bm, n // bn, t // tk),
        in_specs=[pl.BlockSpec((tk, bm), lambda i, j, k: (k, i)), pl.BlockSpec((tk, bn), lambda i, j, k: (k, j))],
        out_specs=pl.BlockSpec((bm, bn), lambda i, j, k: (i, j)),
        out_shape=jax.ShapeDtypeStruct((m, n), F32),
        compiler_params=_cp(("arbitrary", "arbitrary", "arbitrary")),
    )(a, b)


def _dw_cols(a, b, n_blk):
    t, m = a.shape
    s = b.shape[1] // n_blk
    tk = min(TK_DW, t)
    nk = t // tk

    def body(a_ref, b_ref, o_ref, ob_ref):
        part = lax.dot_general(a_ref[...], b_ref[...], _CONTRACT_FIRST, preferred_element_type=F32)
        _acc_out(o_ref.at[0], pl.program_id(1) == 0, part)

        @pl.when(pl.program_id(1) == nk - 1)
        def _():
            ob_ref[...] = o_ref[...].astype(BF16)

    spec = pl.BlockSpec((1, m, n_blk), lambda j, k: (j, 0, 0))
    return pl.pallas_call(
        body, name="dw_cols", grid=(s, nk),
        in_specs=[pl.BlockSpec((tk, m), lambda j, k: (k, 0)), pl.BlockSpec((tk, n_blk), lambda j, k: (k, j))],
        out_specs=[spec, spec],
        out_shape=[jax.ShapeDtypeStruct((s, m, n_blk), F32), jax.ShapeDtypeStruct((s, m, n_blk), BF16)],
        compiler_params=_cp(("arbitrary", "arbitrary")),
    )(a, b)


def _dw_up(h, dup, layer, buf):
    t, m = h.shape
    s, _, n = dup.shape
    tk = min(TK_DW, t)
    nk = t // tk

    def body(*refs):
        a_ref, b_ref, o_ref, ob_ref = refs[0], refs[1], refs[-2], refs[-1]
        part = lax.dot_general(a_ref[...], b_ref[0], _CONTRACT_FIRST, preferred_element_type=F32)
        _acc_out(o_ref.at[0, 0], pl.program_id(1) == 0, part)

        @pl.when(pl.program_id(1) == nk - 1)
        def _():
            ob_ref[...] = o_ref[...].astype(BF16)

    spec = pl.BlockSpec((1, 1, m, n), lambda j, k: (j, layer, 0, 0))
    return pl.pallas_call(
        body, name="dw_up", grid=(s, nk),
        in_specs=[pl.BlockSpec((tk, m), lambda j, k: (k, 0)), pl.BlockSpec((1, tk, n), lambda j, k: (j, k, 0))]
                 + ([] if buf is None else [_ANY, _ANY]),
        out_specs=[spec, spec],
        out_shape=[jax.ShapeDtypeStruct((s, 2, m, n), F32), jax.ShapeDtypeStruct((s, 2, m, n), BF16)],
        input_output_aliases={} if buf is None else {2: 0, 3: 1},
        compiler_params=_cp(("arbitrary", "arbitrary")),
    )(h, dup, *([] if buf is None else buf))


def _dw_down(act, df, layer, buf):
    nc, t, n = act.shape
    d = df.shape[1]
    tk = min(TK_DW, t)
    nk = t // tk

    def body(*refs):
        a_ref, b_ref, o_ref, ob_ref = refs[0], refs[1], refs[-2], refs[-1]
        part = lax.dot_general(a_ref[0], b_ref[...], _CONTRACT_FIRST, preferred_element_type=F32)
        part = part.reshape(2, n // 2, d)
        first = pl.program_id(1) == 0

        @pl.when(first)
        def _():
            o_ref[:, 0] = part

        @pl.when(jnp.logical_not(first))
        def _():
            o_ref[:, 0] += part

        @pl.when(pl.program_id(1) == nk - 1)
        def _():
            ob_ref[...] = o_ref[...].astype(BF16)

    spec = pl.BlockSpec((2, 1, n // 2, d), lambda c, k: (c, layer, 0, 0))
    return pl.pallas_call(
        body, name="dw_down", grid=(nc, nk),
        in_specs=[pl.BlockSpec((1, tk, n), lambda c, k: (c, k, 0)), pl.BlockSpec((tk, d), lambda c, k: (k, 0))]
                 + ([] if buf is None else [_ANY, _ANY]),
        out_specs=[spec, spec],
        out_shape=[jax.ShapeDtypeStruct((2 * nc, 2, n // 2, d), F32), jax.ShapeDtypeStruct((2 * nc, 2, n // 2, d), BF16)],
        input_output_aliases={} if buf is None else {2: 0, 3: 1},
        compiler_params=_cp(("arbitrary", "arbitrary")),
    )(act, df, *([] if buf is None else buf))


def _dz_wt_rms_bwd(dz, wt, x_in, gpre, dres):
    t, n = dz.shape
    tm = min(TM, t)

    def body(dz_ref, wt_ref, x_ref, g_ref, dres_ref, dx_ref, dg_ref):
        dh = jnp.dot(dz_ref[...], wt_ref[...], preferred_element_type=F32)
        dx, dg = _rms_bwd(x_ref[...], g_ref[...], dh)
        dx_ref[...] = dres_ref[...] + dx
        _acc_out(dg_ref, pl.program_id(0) == 0, dg)

    return pl.pallas_call(
        body, name="dz_wt_rms_bwd", grid=(t // tm,),
        in_specs=[_rows(tm, n), _full((n, D_MODEL)), _rows(tm, D_MODEL), _full((1, D_MODEL)), _rows(tm, D_MODEL)],
        out_specs=[_rows(tm, D_MODEL), _full((1, D_MODEL))],
        out_shape=[jax.ShapeDtypeStruct((t, D_MODEL), F32), jax.ShapeDtypeStruct((1, D_MODEL), F32)],
        compiler_params=_cp(("arbitrary",)),
    )(dz, wt, x_in, gpre, dres)


def _shift_matrices(shift, shift_h, tm, hb):
    row = lax.broadcasted_iota(jnp.int32, (2 * tm, tm), 0)
    col = lax.broadcasted_iota(jnp.int32, (2 * tm, tm), 1)
    hit = ((row < tm) & (col == row + 1)) | ((row >= tm) & (col == row - tm + 2))
    shift[...] = jnp.where(hit, 1.0, 0.0).astype(BF16)
    row = lax.broadcasted_iota(jnp.int32, (hb, hb), 0)
    col = lax.broadcasted_iota(jnp.int32, (hb, hb), 1)
    hit = ((row < HALO3) & (col == row - (HALO3 - 1))) | ((row >= HALO3) & (col == row - (2 * HALO3 - 2)))
    shift_h[...] = jnp.where(hit, 1.0, 0.0).astype(BF16)


def _next_rows(shift, shift_h, xb, nxt, d12_s, tm):
    d12_s[...] = jnp.dot(shift[...], xb, preferred_element_type=F32)
    edge = jnp.dot(shift_h[...], nxt, preferred_element_type=F32)
    d12_s[tm - HALO3:tm, :] += edge[0:HALO3, :]
    d12_s[2 * tm - HALO3:2 * tm, :] += edge[HALO3:2 * HALO3, :]


def _ffn_bwd(f, dxo, gpost, x_in, gpre, up, u, cw, wdt, wupt, target=None):
    t = f.shape[0]
    tm = min(TM_BWD, t)
    nt = t // tm
    nc, n = up.shape[1], up.shape[3]
    hb = 2 * HALO3

    def body(*refs):
        f_ref, dxo_ref, gpost_ref, x_ref, gpre_ref, up_ref, u_ref, cw_ref, wdt_ref, wupt_ref = refs[:10]
        n_in = 10 if target is None else 11
        n_out = 7 if target is None else 8
        df_ref, act_ref, dup_ref, dx_ref, dgpost_ref, dgpre_ref, dcw_ref = refs[n_in:n_in + 7]
        df_s, acc, du_s, dub_s, d12_s, hal, shift, shift_h = refs[n_in + n_out:]
        i, c = pl.program_id(0), pl.program_id(1)

        def incoming():
            if target is None:
                return dxo_ref[...]
            return (dxo_ref[...] - refs[10][...]) * (1.0 / D_MODEL)

        @pl.when(c == 0)
        def _():
            dy = incoming()
            df, dg = _rms_bwd(f_ref[...], gpost_ref[...], dy)
            df_s[...] = df.astype(BF16)
            df_ref[...] = df.astype(BF16)
            _acc_out(dgpost_ref, i == 0, dg)
            if target is not None:
                part = jnp.zeros((1, 128), F32) + jnp.sum(dy * dy) * (0.5 * D_MODEL)
                _acc_out(refs[n_in + 7], i == 0, part)

        @pl.when(i == 0)
        def _():
            hal[c] = jnp.zeros((2, hb, n), BF16)
            dcw_ref[0, c] = jnp.zeros((8, n), F32)
            dcw_ref[1, c] = jnp.zeros((8, n), F32)

        @pl.when(jnp.logical_and(i == 0, c == 0))
        def _():
            _shift_matrices(shift, shift_h, tm, hb)

        dact = jnp.dot(df_s[...], wdt_ref[0], preferred_element_type=F32)
        g, v = u_ref[0, 0].astype(F32), u_ref[1, 0].astype(F32)
        sg = _sigmoid(g)
        sil = g * sg
        act_ref[0] = (sil * v).astype(BF16)
        dug = dact * v * (sg + sil * (1.0 - sg))
        duv = dact * sil
        du_s[0], du_s[1] = dug, duv
        dub_s[0], dub_s[1] = dug.astype(BF16), duv.astype(BF16)
        dh = None
        for gv in range(2):
            _next_rows(shift, shift_h, dub_s[gv], hal[c, gv], d12_s, tm)
            hal[c, gv] = dub_s[gv, 0:hb, :]
            du, d1, d2 = du_s[gv], d12_s[0:tm, :], d12_s[tm:2 * tm, :]
            dup = (cw_ref[gv, 0, 2:3, :] * du + cw_ref[gv, 0, 1:2, :] * d1 + cw_ref[gv, 0, 0:1, :] * d2).astype(BF16)
            dup_ref[gv, 0] = dup
            upc = up_ref[gv, 0].astype(F32)
            dcw_ref[gv, c, 2:3, :] += jnp.sum(upc * du, axis=0, keepdims=True)
            dcw_ref[gv, c, 1:2, :] += jnp.sum(upc * d1, axis=0, keepdims=True)
            dcw_ref[gv, c, 0:1, :] += jnp.sum(upc * d2, axis=0, keepdims=True)
            part = jnp.dot(dup, wupt_ref[gv, 0], preferred_element_type=F32)
            dh = part if dh is None else dh + part
        _acc_out(acc, c == 0, dh)

        @pl.when(c == nc - 1)
        def _():
            dx, dg = _rms_bwd(x_ref[...], gpre_ref[...], acc[...])
            dx_ref[...] = incoming() + dx
            _acc_out(dgpre_ref, i == 0, dg)

    rrow = lambda w: pl.BlockSpec((tm, w), lambda i, c: (nt - 1 - i, 0))
    blk = pl.BlockSpec((2, 1, tm, n), lambda i, c: (0, c, nt - 1 - i, 0))
    one = _full((1, D_MODEL))
    return pl.pallas_call(
        body, name="ffn_bwd", grid=(nt, nc),
        in_specs=[rrow(D_MODEL), rrow(D_MODEL), one, rrow(D_MODEL), one, blk, blk,
                  pl.BlockSpec((2, 1, 3, n), lambda i, c: (0, c, 0, 0)),
                  pl.BlockSpec((1, D_MODEL, n), lambda i, c: (c, 0, 0)),
                  pl.BlockSpec((2, 1, n, D_MODEL), lambda i, c: (0, c, 0, 0))] + ([] if target is None else [rrow(D_MODEL)]),
        out_specs=[rrow(D_MODEL), pl.BlockSpec((1, tm, n), lambda i, c: (c, nt - 1 - i, 0)), blk, rrow(D_MODEL),
                   one, one, _full((2, nc, 8, n))] + ([] if target is None else [_full((1, 128))]),
        out_shape=[jax.ShapeDtypeStruct((t, D_MODEL), BF16), jax.ShapeDtypeStruct((nc, t, n), BF16),
                   jax.ShapeDtypeStruct((2, nc, t, n), BF16), jax.ShapeDtypeStruct((t, D_MODEL), F32),
                   jax.ShapeDtypeStruct((1, D_MODEL), F32), jax.ShapeDtypeStruct((1, D_MODEL), F32),
                   jax.ShapeDtypeStruct((2, nc, 8, n), F32)]
                  + ([] if target is None else [jax.ShapeDtypeStruct((1, 128), F32)]),
        scratch_shapes=[pltpu.VMEM((tm, D_MODEL), BF16), pltpu.VMEM((tm, D_MODEL), F32),
                        pltpu.VMEM((2, tm, n), F32), pltpu.VMEM((2, tm, n), BF16), pltpu.VMEM((2 * tm, n), F32),
                        pltpu.VMEM((nc, 2, hb, n), BF16), pltpu.VMEM((2 * tm, tm), BF16), pltpu.VMEM((hb, hb), BF16)],
        compiler_params=_cp(("arbitrary", "arbitrary")),
    )(f, dxo, gpost, x_in, gpre, up, u, cw, wdt, wupt, *([] if target is None else [target]))


def _od_bwd(m, dxo, gpost, x_in, gpre, z, cv, cw, wot, wint):
    t = m.shape[0]
    tm = min(TM_BWD, t)
    nt = t // tm
    hb = 2 * HALO3

    def body(m_ref, dxo_ref, gpost_ref, x_ref, gpre_ref, z_ref, cv_ref, cw_ref, wot_ref, wint_ref,
             dm_ref, dz_ref, dx_ref, dgpost_ref, dgpre_ref, dcw_ref, dcvb_s, d12_s, dz_s, hal, shift, shift_h):
        i = pl.program_id(0)
        dxo = dxo_ref[...]
        dm, dg = _rms_bwd(m_ref[...], gpost_ref[...], dxo)
        dmb = dm.astype(BF16)
        dm_ref[...] = dmb
        _acc_out(dgpost_ref, i == 0, dg)

        @pl.when(i == 0)
        def _():
            hal[...] = jnp.zeros((hb, SC_DIM), BF16)
            dcw_ref[...] = jnp.zeros((8, SC_DIM), F32)
            _shift_matrices(shift, shift_h, tm, hb)

        dy = jnp.dot(dmb, wot_ref[...], preferred_element_type=F32)
        z = z_ref[...].astype(F32)
        b, cg, u = z[:, :SC_DIM], z[:, SC_DIM:2 * SC_DIM], z[:, 2 * SC_DIM:]
        dz_s[:, 0:SC_DIM] = (dy * cv_ref[...].astype(F32)).astype(BF16)
        dcv = dy * b
        dcvb_s[...] = dcv.astype(BF16)
        _next_rows(shift, shift_h, dcvb_s[...], hal[...], d12_s, tm)
        hal[...] = dcvb_s[0:hb, :]
        d1, d2 = d12_s[0:tm, :], d12_s[tm:2 * tm, :]
        dcu = cw_ref[2:3, :] * dcv + cw_ref[1:2, :] * d1 + cw_ref[0:1, :] * d2
        cu = cg * u
        dcw_ref[2:3, :] += jnp.sum(cu * dcv, axis=0, keepdims=True)
        dcw_ref[1:2, :] += jnp.sum(cu * d1, axis=0, keepdims=True)
        dcw_ref[0:1, :] += jnp.sum(cu * d2, axis=0, keepdims=True)
        dz_s[:, SC_DIM:2 * SC_DIM] = (dcu * u).astype(BF16)
        dz_s[:, 2 * SC_DIM:3 * SC_DIM] = (dcu * cg).astype(BF16)
        dz_ref[...] = dz_s[...]
        dh = jnp.dot(dz_s[...], wint_ref[...], preferred_element_type=F32)
        dx, dg2 = _rms_bwd(x_ref[...], gpre_ref[...], dh)
        dx_ref[...] = dxo + dx
        _acc_out(dgpre_ref, i == 0, dg2)

    rrow = lambda w: pl.BlockSpec((tm, w), lambda i: (nt - 1 - i, 0))
    one = _full((1, D_MODEL))
    return pl.pallas_call(
        body, name="od_bwd", grid=(nt,),
        in_specs=[rrow(D_MODEL), rrow(D_MODEL), one, rrow(D_MODEL), one, rrow(3 * SC_DIM), rrow(SC_DIM),
                  _full((3, SC_DIM)), _full((D_MODEL, SC_DIM)), _full((3 * SC_DIM, D_MODEL))],
        out_specs=[rrow(D_MODEL), rrow(3 * SC_DIM), rrow(D_MODEL), one, one, _full((8, SC_DIM))],
        out_shape=[jax.ShapeDtypeStruct((t, D_MODEL), BF16), jax.ShapeDtypeStruct((t, 3 * SC_DIM), BF16),
                   jax.ShapeDtypeStruct((t, D_MODEL), F32), jax.ShapeDtypeStruct((1, D_MODEL), F32),
                   jax.ShapeDtypeStruct((1, D_MODEL), F32), jax.ShapeDtypeStruct((8, SC_DIM), F32)],
        scratch_shapes=[pltpu.VMEM((tm, SC_DIM), BF16), pltpu.VMEM((2 * tm, SC_DIM), F32),
                        pltpu.VMEM((tm, 3 * SC_DIM), BF16), pltpu.VMEM((hb, SC_DIM), BF16),
                        pltpu.VMEM((2 * tm, tm), BF16), pltpu.VMEM((hb, hb), BF16)],
        compiler_params=_cp(("arbitrary",)),
    )(m, dxo, gpost, x_in, gpre, z, cv, cw, wot, wint)


def _ev_bwd1(m, dxo, gpost, wot):
    t = m.shape[0]
    tm = min(TM, t)

    def body(m_ref, dxo_ref, g_ref, wot_ref, dm_ref, da_ref, do_ref, dg_ref):
        dm, dg = _rms_bwd(m_ref[...], g_ref[...], dxo_ref[...])
        dmb = dm.astype(BF16)
        dm_ref[...] = dmb
        _acc_out(dg_ref, pl.program_id(0) == 0, dg)
        dao = jnp.dot(dmb, wot_ref[...], preferred_element_type=F32)
        da_ref[...] = dao[:, :A_CH]
        do_ref[...] = dao[:, A_CH:].astype(BF16)

    return pl.pallas_call(
        body, name="ev_bwd1", grid=(t // tm,),
        in_specs=[_rows(tm, D_MODEL), _rows(tm, D_MODEL), _full((1, D_MODEL)), _full((D_MODEL, A_CH + Q_DIM))],
        out_specs=[_rows(tm, D_MODEL), _rows(tm, A_CH), _rows(tm, Q_DIM), _full((1, D_MODEL))],
        out_shape=[jax.ShapeDtypeStruct((t, D_MODEL), BF16), jax.ShapeDtypeStruct((t, A_CH), F32),
                   jax.ShapeDtypeStruct((t, Q_DIM), BF16), jax.ShapeDtypeStruct((1, D_MODEL), F32)],
        compiler_params=_cp(("arbitrary",)),
    )(m, dxo, gpost, wot)


def _conf_bwd(da, cv, zag, conv_w, ln_g, ln_b):
    t = da.shape[0]
    tm = min(TM_BWD, t)
    nt = t // tm
    rows = tm + HALO31 - 8

    def body(da_ref, c_ref, z_ref, w_ref, g_ref, lb_ref, dz_ref, dw_ref, dv_ref, ext_out, cbuf, glu_s, dglu_s):
        i = pl.program_id(0)

        @pl.when(i == 0)
        def _():
            ext_out[tm:tm + HALO31, :] = jnp.zeros((HALO31, A_CH), F32)
            dw_ref[...] = jnp.zeros((32, A_CH), F32)
            dv_ref[...] = jnp.zeros((8, A_CH), F32)

        x = c_ref[...]
        mu = jnp.mean(x, axis=-1, keepdims=True)
        xc = x - mu
        rstd = lax.rsqrt(jnp.mean(xc * xc, axis=-1, keepdims=True) + LN_EPS)
        xh = xc * rstd
        ln = xh * g_ref[...] + lb_ref[...]
        sl = _sigmoid(ln)
        dln = da_ref[...] * (sl * (1.0 + ln * (1.0 - sl)))
        dxh = dln * g_ref[...]
        dc = rstd * (dxh - jnp.mean(dxh, axis=-1, keepdims=True) - xh * jnp.mean(dxh * xh, axis=-1, keepdims=True))
        dv_ref[0:1, :] += jnp.sum(dc, axis=0, keepdims=True)
        dv_ref[1:2, :] += jnp.sum(dln * xh, axis=0, keepdims=True)
        dv_ref[2:3, :] += jnp.sum(dln, axis=0, keepdims=True)

        ext_out[0:tm, :] = dc
        _tap_copies(ext_out, cbuf, lambda b: b, rows)
        z = z_ref[...].astype(F32)
        al, sg = z[:, :A_CH], _sigmoid(z[:, A_CH:])
        glu_s[...] = al * sg
        for rs in range(0, tm, SUB):
            for cs in range(0, A_CH, 128):
                glu = glu_s[rs:rs + SUB, cs:cs + 128]
                acc = jnp.zeros((SUB, 128), F32)
                for k in range(A_CONV):
                    lag_a, lag_b = divmod(k, 8)
                    r0 = 8 * lag_a + rs
                    d = (ext_out[r0:r0 + SUB, cs:cs + 128] if lag_b == 0
                         else cbuf[lag_b - 1, r0:r0 + SUB, cs:cs + 128])
                    j = A_CONV - 1 - k
                    acc = acc + w_ref[j:j + 1, cs:cs + 128] * d
                    dw_ref[j:j + 1, cs:cs + 128] += jnp.sum(glu * d, axis=0, keepdims=True)
                dglu_s[rs:rs + SUB, cs:cs + 128] = acc
        dglu = dglu_s[...]
        ext_out[tm:tm + HALO31, :] = ext_out[0:HALO31, :]
        dz_ref[:, 0:A_CH] = (dglu * sg).astype(BF16)
        dz_ref[:, A_CH:2 * A_CH] = (dglu * al * sg * (1.0 - sg)).astype(BF16)

    rrow = lambda w: pl.BlockSpec((tm, w), lambda i: (nt - 1 - i, 0))
    return pl.pallas_call(
        body, name="conf_bwd", grid=(nt,),
        in_specs=[rrow(A_CH), rrow(A_CH), rrow(2 * A_CH), _full((32, A_CH)), _full((1, A_CH)), _full((1, A_CH))],
        out_specs=[rrow(2 * A_CH), _full((32, A_CH)), _full((8, A_CH))],
        out_shape=[jax.ShapeDtypeStruct((t, 2 * A_CH), BF16), jax.ShapeDtypeStruct((32, A_CH), F32),
                   jax.ShapeDtypeStruct((8, A_CH), F32)],
        scratch_shapes=[pltpu.VMEM((tm + HALO31, A_CH), F32), pltpu.VMEM((7, rows, A_CH), F32),
                        pltpu.VMEM((tm, A_CH), F32), pltpu.VMEM((tm, A_CH), F32)],
        compiler_params=_cp(("arbitrary",)),
    )(da, cv, zag, conv_w, ln_g, ln_b)


def _attn_bwd(q, k, v, do, sinks):
    t = q.shape[0]
    nb = min(ATT_NB, t // BLOCK)
    rows = nb * BLOCK
    ns = t // rows

    def body(s_ref, q_ref, kc_ref, kp_ref, vc_ref, vp_ref, do_ref, dq_ref, dk_ref, dv_ref, ds_ref, dkc, dvc):
        i = pl.program_id(0)
        r = ns - 1 - i

        @pl.when(i == 0)
        def _():
            dkc[...] = jnp.zeros_like(dkc)
            dvc[...] = jnp.zeros_like(dvc)
            ds_ref[...] = jnp.zeros_like(ds_ref)

        lane = lax.broadcasted_iota(jnp.int32, (1, N_Q_HEADS), 1)
        dsv = jnp.zeros((1, N_Q_HEADS), F32)
        for b in range(nb - 1, -1, -1):
            lo = BLOCK * b
            mask = _attn_mask(r == 0) if b == 0 else _attn_mask(False)
            qv, dov = q_ref[lo:lo + BLOCK, :], do_ref[lo:lo + BLOCK, :]
            kc, vc = kc_ref[lo:lo + BLOCK, :], vc_ref[lo:lo + BLOCK, :]
            kp = kp_ref[...] if b == 0 else kc_ref[lo - BLOCK:lo, :]
            vp = vp_ref[...] if b == 0 else vc_ref[lo - BLOCK:lo, :]
            for h in range(N_KV_HEADS):
                q4, do4 = _q_heads(qv, h), _q_heads(dov, h)
                k2, v2 = _kv_head(kp, kc, h), _kv_head(vp, vc, h)
                pn, ps = _attn_probs(q4, k2, mask, _sink_rows(s_ref, h))
                dp = lax.dot_general(do4, v2, _CONTRACT_LAST, preferred_element_type=F32)
                dl = jnp.sum(pn * dp, axis=-1, keepdims=True)
                dsb = (pn * (dp - dl)).astype(BF16)
                dq4 = (jnp.dot(dsb, k2, preferred_element_type=F32) * SCALE).astype(BF16)
                for g in range(GROUP):
                    c0 = HEAD_DIM * (GROUP * h + g)
                    dq_ref[lo:lo + BLOCK, c0:c0 + HEAD_DIM] = dq4[BLOCK * g:BLOCK * (g + 1), :]
                dk2 = lax.dot_general(dsb, q4, _CONTRACT_FIRST, preferred_element_type=F32) * SCALE
                dv2 = lax.dot_general(pn.astype(BF16), do4, _CONTRACT_FIRST, preferred_element_type=F32)
                dk_ref[lo:lo + BLOCK, HEAD_DIM * h:HEAD_DIM * (h + 1)] = dk2[BLOCK:, :] + dkc[h]
                dv_ref[lo:lo + BLOCK, HEAD_DIM * h:HEAD_DIM * (h + 1)] = dv2[BLOCK:, :] + dvc[h]
                dkc[h] = dk2[:BLOCK, :]
                dvc[h] = dv2[:BLOCK, :]
                srow = -ps * dl
                for g in range(GROUP):
                    dsv = dsv + jnp.where(lane == GROUP * h + g, jnp.sum(srow[BLOCK * g:BLOCK * (g + 1), :]), 0.0)
        ds_ref[...] += dsv

    cur = lambda n: pl.BlockSpec((rows, n), lambda i: (ns - 1 - i, 0))
    prev = lambda n: pl.BlockSpec((BLOCK, n), lambda i: (jnp.maximum((ns - 1 - i) * nb - 1, 0), 0))
    return pl.pallas_call(
        body, name="attn_bwd", grid=(ns,),
        in_specs=[pl.BlockSpec(memory_space=pltpu.SMEM), cur(Q_DIM), cur(KV_DIM), prev(KV_DIM), cur(KV_DIM),
                  prev(KV_DIM), cur(Q_DIM)],
        out_specs=[cur(Q_DIM), cur(KV_DIM), cur(KV_DIM), _full((1, N_Q_HEADS))],
        out_shape=[jax.ShapeDtypeStruct((t, Q_DIM), BF16), jax.ShapeDtypeStruct((t, KV_DIM), F32),
                   jax.ShapeDtypeStruct((t, KV_DIM), F32), jax.ShapeDtypeStruct((1, N_Q_HEADS), F32)],
        scratch_shapes=[pltpu.VMEM((N_KV_HEADS, BLOCK, HEAD_DIM), F32), pltpu.VMEM((N_KV_HEADS, BLOCK, HEAD_DIM), F32)],
        compiler_params=_cp(("arbitrary",)),
    )(sinks, q, k, k, v, v, do)


def _ev_dz(dzag, dq, dk, dv, rc, rsa, rsb):
    t = dzag.shape[0]
    tm = min(TM, t)

    def body(dzag_ref, dq_ref, dk_ref, dv_ref, c_ref, sa_ref, sb_ref, dz_ref):
        c, sa, sb = c_ref[...], sa_ref[...], sb_ref[...]
        dz_ref[:, 0:2 * A_CH] = dzag_ref[...]
        q0 = 2 * A_CH
        for j in range(Q_DIM // 128):
            d = dq_ref[:, 128 * j:128 * (j + 1)].astype(F32)
            dz_ref[:, q0 + 128 * j:q0 + 128 * (j + 1)] = _rope_bwd(d, c, sa, sb).astype(BF16)
        k0 = q0 + Q_DIM
        dz_ref[:, k0:k0 + KV_DIM] = _rope_bwd(dk_ref[...], c, sa, sb).astype(BF16)
        dz_ref[:, k0 + KV_DIM:k0 + 2 * KV_DIM] = dv_ref[...].astype(BF16)

    return pl.pallas_call(
        body, name="ev_dz", grid=(t // tm,),
        in_specs=[_rows(tm, 2 * A_CH), _rows(tm, Q_DIM), _rows(tm, KV_DIM), _rows(tm, KV_DIM),
                  _rows(tm, 128), _rows(tm, 128), _rows(tm, 128)],
        out_specs=_rows(tm, EVEN_IN),
        out_shape=jax.ShapeDtypeStruct((t, EVEN_IN), BF16),
        compiler_params=_cp(("arbitrary",)),
    )(dzag, dq, dk, dv, rc, rsa, rsb)


def _prep_ev(gat):
    p = {}
    w = gat["ev_w_in"][:, 0].transpose(1, 0, 2).reshape(D_MODEL, EVEN_IN)
    p["ev_w_in"], p["ev_w_in_t"] = w, w.T
    w = gat["ev_w_out"].reshape(A_CH + Q_DIM, D_MODEL)
    p["ev_w_out"], p["ev_w_out_t"] = w, w.T
    return p


def _prep_rest(gat):
    p = {}
    g = gat["od_w_in"][:, 0]
    p["od_w_in"], p["od_w_in_t"] = g, g.transpose(0, 2, 1).reshape(3 * SC_DIM, D_MODEL)
    w = gat["od_w_out"].reshape(SC_DIM, D_MODEL)
    p["od_w_out"], p["od_w_out_t"] = w, w.T
    g = gat["ffn_w_up"]
    p["ffn_w_up"] = g.reshape(2, N_DEV // 2, 2, D_MODEL, FF_N)
    p["ffn_w_up_t"] = [g[:, i].transpose(0, 2, 1).reshape(2, N_DEV // 2, FF_N, D_MODEL) for i in range(2)]
    g = gat["ffn_w_down"]
    p["ffn_w_down"] = [g[:, i].reshape(D_FF, D_MODEL) for i in range(2)]
    p["ffn_w_down_t"] = [w.reshape(N_DEV // 2, FF_N, D_MODEL).transpose(0, 2, 1) for w in p["ffn_w_down"]]
    return p


def _local_step(x, positions, target, p, rest_weights, s, token, grads_ready):
    row = lambda a, tok=None: a.reshape(1, -1) if tok is None else a.reshape(1, -1) + tok
    nc = N_DEV // 2
    rc, rsa, rsb = _rope_tables(positions)
    conv31 = jnp.pad(s["ev_a_conv_w"][0], ((0, 1), (0, 0)))
    cw_ffn = [s["ffn_conv_w"][i].reshape(3, 2, nc, FF_N).transpose(1, 2, 0, 3) for i in range(2)]
    sinks = s["ev_sinks"][0]
    big, g = {}, {}

    h0, zag, q, k, v = _ev_in(x, row(s["mix_norm_pre"][0], token), p["ev_w_in"], rc, rsa, rsb)
    cv, a = _conf_fwd(zag, conv31, s["ev_a_conv_b"], s["ev_a_ln_g"], s["ev_a_ln_b"])
    o = _attn_fwd(q, k, v, sinks)
    wo = p["ev_w_out"]
    m0, x1 = _out_post([a, o], [wo[:A_CH], wo[A_CH:]], x, row(s["mix_norm_post"][0]))
    p = {**p, **rest_weights(m0)}
    h1, up0, u0, f0, x2 = _ffn_fwd(x1, row(s["ffn_norm_pre"][0]), p["ffn_w_up"], 0, cw_ffn[0], p["ffn_w_down"][0],
                                   row(s["ffn_norm_post"][0]))
    h2, z, cv1, y, m1, x3 = _od_fwd(x2, row(s["mix_norm_pre"][1]), p["od_w_in"], s["od_conv_w"][0], p["od_w_out"],
                                    row(s["mix_norm_post"][1]))
    h3, up1, u1, f1, x4 = _ffn_fwd(x3, row(s["ffn_norm_pre"][1]), p["ffn_w_up"], 1, cw_ffn[1], p["ffn_w_down"][1],
                                   row(s["ffn_norm_post"][1]))

    def ffn_back(i, f, dxo, up, u, h, x_in, bufs, tok=None, tgt=None):
        df, act, dup, dx_in, dgpost, dgpre, dcw, *loss = _ffn_bwd(
            f, dxo, row(s["ffn_norm_post"][i], tok), x_in, row(s["ffn_norm_pre"][i]), up, u, cw_ffn[i],
            p["ffn_w_down_t"][i], p["ffn_w_up_t"][i], tgt)
        bufs = (_dw_up(h, dup.reshape(N_DEV, -1, FF_N), i, bufs[0]), _dw_down(act, df, i, bufs[1]))
        return dx_in, dgpost, dgpre, dcw[:, :, 0:3].transpose(2, 0, 1, 3).reshape(3, 2 * D_FF), bufs, loss

    dx, dgfpost1, dgfpre1, dcw1, bufs, (lpart,) = ffn_back(1, f1, x4, up1, u1, h3, x3, (None, None), None, target)

    dm1, dz, dx, dgpost1, dgpre1, dcw_od = _od_bwd(m1, dx, row(s["mix_norm_post"][1]), x2, row(s["mix_norm_pre"][1]), z,
                                                   cv1, s["od_conv_w"][0], p["od_w_out_t"], p["od_w_in_t"])
    big["od_w_out"] = _dw2d(y, dm1, SC_DIM, D_MODEL).reshape(N_DEV, -1, D_MODEL)
    big["od_w_in"], big["od_w_in:bf16"] = _dw_cols(h2, dz, 3 * SC_DIM // N_DEV)
    g["od_conv_w"] = dcw_od[None, 0:3]
    tok = grads_ready(["od_w_in", "od_w_out"], big)

    dx, dgfpost0, dgfpre0, dcw0, bufs, _ = ffn_back(0, f0, dx, up0, u0, h1, x1, bufs, tok)
    (big["ffn_w_up"], big["ffn_w_up:bf16"]), (big["ffn_w_down"], big["ffn_w_down:bf16"]) = bufs
    tok = grads_ready(["ffn_w_up", "ffn_w_down"], big)

    dm0, da, do, dgpost0 = _ev_bwd1(m0, dx, row(s["mix_norm_post"][0], tok), p["ev_w_out_t"])
    big["ev_w_out"] = jnp.concatenate([_dw2d(a, dm0, A_CH, D_MODEL), _dw2d(o, dm0, Q_DIM, D_MODEL)],
                                      axis=0).reshape(N_DEV, -1, D_MODEL)
    dzag, dcw31, dvec = _conf_bwd(da, cv, zag, conv31, s["ev_a_ln_g"], s["ev_a_ln_b"])
    dq, dk, dv, dsinks = _attn_bwd(q, k, v, do, sinks)
    dz0 = _ev_dz(dzag, dq, dk, dv, rc, rsa, rsb)
    dw_in = _dw2d(h0, dz0, D_MODEL, EVEN_IN // 2)
    big["ev_w_in"] = dw_in.reshape(D_MODEL, N_DEV, EVEN_IN // N_DEV).transpose(1, 0, 2)
    dx, dgpre0 = _dz_wt_rms_bwd(dz0, p["ev_w_in_t"], x, row(s["mix_norm_pre"][0]), dx)
    grads_ready(["ev_w_in", "ev_w_out"], big)

    g["mix_norm_pre"] = jnp.concatenate([dgpre0, dgpre1], axis=0)
    g["mix_norm_post"] = jnp.concatenate([dgpost0, dgpost1], axis=0)
    g["ffn_norm_pre"] = jnp.concatenate([dgfpre0, dgfpre1], axis=0)
    g["ffn_norm_post"] = jnp.concatenate([dgfpost0, dgfpost1], axis=0)
    g["ev_a_conv_w"] = dcw31[None, 0:A_CONV]
    g["ev_a_conv_b"], g["ev_a_ln_g"], g["ev_a_ln_b"] = dvec[0:1], dvec[1:2], dvec[2:3]
    g["ev_sinks"] = dsinks
    g["ffn_conv_w"] = jnp.stack([dcw0, dcw1])
    return lpart[0, 0], dx, big, g


MESH = pl.DeviceIdType.MESH


def _all_gather(shards, name):
    nw = len(shards)

    def body(*refs):
        x_refs, out_refs = refs[:nw], refs[nw:2 * nw]
        send_sems, recv_sems, local_sems = refs[2 * nw:]
        x, y, c = lax.axis_index("x"), lax.axis_index("y"), lax.axis_index("c")
        me, sibling = (x, y, c), (x, y, 1 - c)
        chips = [(1 - x, y), (x, 1 - y), (1 - x, 1 - y)]

        def rows(w, px, py, pc):
            m_per = shards[w].shape[0]
            return out_refs[w].at[pl.ds((4 * px + 2 * py + pc) * m_per, m_per), :]

        def copy(w, k, block, to, src=None):
            return pltpu.make_async_remote_copy(
                src_ref=rows(w, *block) if src is None else src, dst_ref=rows(w, *block),
                send_sem=send_sems.at[w, k], recv_sem=recv_sems.at[w, k], device_id=to, device_id_type=MESH)

        mine, first, passed = [], [], []
        for w in range(nw):
            cp = pltpu.make_async_copy(x_refs[w], rows(w, *me), local_sems.at[w])
            cp.start()
            mine.append(cp)
            first.append([copy(w, 0, me, sibling, src=x_refs[w])]
                         + [copy(w, 1 + j, me, (*chip, c), src=x_refs[w]) for j, chip in enumerate(chips)])
            for cp in first[w]:
                cp.start()
        for w in range(nw):
            passed.append([copy(w, 4 + j, (*chip, c), sibling) for j, chip in enumerate(chips)])
            for j, chip in enumerate(chips):
                copy(w, 1 + j, (*chip, c), me).wait_recv()
                passed[w][j].start()
        for w in range(nw):
            copy(w, 0, sibling, me).wait_recv()
            for j, chip in enumerate(chips):
                copy(w, 4 + j, (*chip, 1 - c), me).wait_recv()
            for cp in first[w] + passed[w]:
                cp.wait_send()
            mine[w].wait()

    return pl.pallas_call(
        body, name=name,
        out_shape=[jax.ShapeDtypeStruct((N_DEV * a.shape[0], a.shape[1]), a.dtype) for a in shards],
        in_specs=[_ANY] * nw, out_specs=[_ANY] * nw,
        scratch_shapes=[pltpu.SemaphoreType.DMA((nw, 7)), pltpu.SemaphoreType.DMA((nw, 7)),
                        pltpu.SemaphoreType.DMA((nw,))],
    )(*shards)


_HBM = pl.BlockSpec(memory_space=pltpu.HBM)
_SEM = pl.BlockSpec(memory_space=pltpu.SEMAPHORE)
_EFFECT = pltpu.SideEffectType.DATAFLOW_SIDE_EFFECTING
_RELATIONS = [(dx, dy, dc) for dx in (0, 1) for dy in (0, 1) for dc in (0, 1)][1:]


def _peer(rel):
    x, y, c = lax.axis_index("x"), lax.axis_index("y"), lax.axis_index("c")
    px, py, pc = x ^ rel[0], y ^ rel[1], c ^ rel[2]
    return (px, py, pc), 4 * px + 2 * py + pc, 4 * x + 2 * y + c


def _exchange_copy(k, rel, src_ref, land_ref, send_sems, recv_sems, w, scatter):
    peer, peer_idx, my_idx = _peer(rel)
    src = src_ref.at[peer_idx] if scatter else src_ref
    return pltpu.make_async_remote_copy(
        src_ref=src, dst_ref=land_ref.at[my_idx], send_sem=send_sems.at[_sends(scatter) * w + k],
        recv_sem=recv_sems.at[7 * w + k], device_id=peer, device_id_type=MESH)


def _sends(scatter):
    return 7 if scatter else 8


def _own_copy(src_ref, land_ref, send_sems, w):
    my_idx = _peer(_RELATIONS[0])[2]
    return pltpu.make_async_copy(src_ref, land_ref.at[my_idx], send_sems.at[8 * w + 7])


def _exchange_start(srcs, scatter, name):
    nw = len(srcs)
    lands = [lax.empty((N_DEV,) + (a.shape[1:] if scatter else a.shape), a.dtype) for a in srcs]

    def body(*refs):
        src_refs, land_refs = refs[:nw], refs[nw:2 * nw]
        send_sems, recv_sems = refs[2 * nw], refs[2 * nw + 1]
        token = refs[-1]
        for w in range(nw):
            for k, rel in enumerate(_RELATIONS):
                _exchange_copy(k, rel, src_refs[w], land_refs[w], send_sems, recv_sems, w, scatter).start()
            if not scatter:
                _own_copy(src_refs[w], land_refs[w], send_sems, w).start()
        token[...] = jnp.zeros_like(token)

    hbm = lambda a: pltpu.HBM(a.shape, a.dtype)
    outs = pl.pallas_call(
        body, name=name,
        out_shape=(pltpu.SemaphoreType.DMA((_sends(scatter) * nw,)), pltpu.SemaphoreType.DMA((7 * nw,)),
                   *[hbm(a) for a in srcs],
                   *[hbm(a) for a in lands], jax.ShapeDtypeStruct((8, 128), F32)),
        in_specs=[_HBM] * (2 * nw),
        out_specs=(_SEM, _SEM, *[_HBM] * (2 * nw), pl.BlockSpec(memory_space=pltpu.VMEM)),
        input_output_aliases={i: 2 + i for i in range(2 * nw)},
        compiler_params=pltpu.CompilerParams(has_side_effects=_EFFECT),
    )(*[pltpu.with_memory_space_constraint(a, pltpu.HBM) for a in srcs],
      *[pltpu.with_memory_space_constraint(a, pltpu.HBM) for a in lands])
    return outs[0], outs[1], list(outs[2:2 + nw]), list(outs[2 + nw:2 + 2 * nw]), outs[-1]


def _exchange_wait(started, scatter, after, name):
    send_sems, recv_sems, srcs, lands, _ = started
    nw = len(srcs)

    def body(*refs):
        src_refs, land_refs = refs[:nw], refs[nw:2 * nw]
        send_s, recv_s = refs[2 * nw], refs[2 * nw + 1]
        for w in range(nw):
            for k, rel in enumerate(_RELATIONS):
                cp = _exchange_copy(k, rel, src_refs[w], land_refs[w], send_s, recv_s, w, scatter)
                cp.wait_send()
                _, peer_idx, _ = _peer(rel)
                pltpu.make_async_remote_copy(
                    src_ref=src_refs[w].at[peer_idx] if scatter else src_refs[w], dst_ref=land_refs[w].at[peer_idx],
                    send_sem=send_s.at[_sends(scatter) * w + k], recv_sem=recv_s.at[7 * w + k],
                    device_id=_peer(rel)[0], device_id_type=MESH).wait_recv()
            if not scatter:
                _own_copy(src_refs[w], land_refs[w], send_s, w).wait()

    hbm = lambda a: pltpu.HBM(a.shape, a.dtype)
    outs = pl.pallas_call(
        body, name=name, out_shape=tuple(hbm(a) for a in srcs + lands),
        in_specs=[_HBM] * (2 * nw) + [_SEM, _SEM, _ANY], out_specs=tuple([_HBM] * (2 * nw)),
        input_output_aliases={i: i for i in range(2 * nw)},
        compiler_params=pltpu.CompilerParams(has_side_effects=_EFFECT),
    )(*srcs, *lands, send_sems, recv_sems, after)
    return list(outs[nw:])


def _to_bf16(a):
    _, r, l = a.shape
    tr = _row_tile(r, 512)

    def body(a_ref, o_ref):
        o_ref[...] = a_ref[...].astype(BF16)

    spec = pl.BlockSpec((1, tr, l), lambda j, i: (j, i, 0))
    return pl.pallas_call(
        body, name="to_bf16", grid=(N_DEV, r // tr), in_specs=[spec], out_specs=spec,
        out_shape=jax.ShapeDtypeStruct(a.shape, BF16), compiler_params=_cp(("arbitrary", "arbitrary")),
    )(a)


def _row_tile(rows, cap):
    best = None
    for d in range(16, min(rows, cap) + 1, 16):
        if rows % d == 0:
            best = d
    return rows if best is None else best


def _adam_math(w, g, m, v):
    bc1 = 1.0 - ADAM_B1 ** ADAM_STEP
    bc2 = 1.0 - ADAM_B2 ** ADAM_STEP
    mn = ADAM_B1 * m + (1.0 - ADAM_B1) * g
    vn = ADAM_B2 * v + (1.0 - ADAM_B2) * (g * g)
    return -ADAM_LR * ((mn / bc1) / (jnp.sqrt(vn / bc2) + ADAM_EPS) + ADAM_WD * w), mn, vn


def _adamw_rs(gp, land, w, m, v, dev):
    _, r, l = gp.shape
    tr = _row_tile(r, 256)

    def body(i_ref, g_ref, b_ref, w_ref, m_ref, v_ref, go_ref, d_ref, mo_ref, vo_ref):
        g = g_ref[0]
        for j in range(N_DEV):
            g = g + jnp.where(i_ref[0] == j, 0.0, b_ref[j].astype(F32))
        go_ref[...] = g
        d_ref[...], mo_ref[...], vo_ref[...] = _adam_math(w_ref[...], g, m_ref[...], v_ref[...])

    spec = pl.BlockSpec((tr, l), lambda i, s: (i, 0))
    return pl.pallas_call(
        body, name="adamw_rs", out_shape=[jax.ShapeDtypeStruct((r, l), F32)] * 4,
        grid_spec=pltpu.PrefetchScalarGridSpec(
            num_scalar_prefetch=1, grid=(r // tr,),
            in_specs=[pl.BlockSpec((1, tr, l), lambda i, s: (s[0], i, 0)),
                      pl.BlockSpec((N_DEV, tr, l), lambda i, s: (0, i, 0)), spec, spec, spec],
            out_specs=[spec] * 4),
        compiler_params=_cp(("arbitrary",)),
    )(dev, gp, land, w, m, v)


def _sum_blocks(a, nblk):
    m = a.shape[0] // nblk
    n = a.shape[1]

    def body(a_ref, o_ref):
        acc = a_ref[0]
        for j in range(1, nblk):
            acc = acc + a_ref[j]
        o_ref[...] = acc

    return pl.pallas_call(
        body, name="sum_blocks", out_shape=jax.ShapeDtypeStruct((m, n), a.dtype),
        in_specs=[_full((nblk, m, n))], out_specs=_full((m, n)),
    )(a.reshape(nblk, m, n))


def _adamw(w, g, m, v):
    rows, c = w.shape

    def body(w_ref, g_ref, m_ref, v_ref, d_ref, mo_ref, vo_ref):
        d_ref[...], mo_ref[...], vo_ref[...] = _adam_math(w_ref[...], g_ref[...], m_ref[...], v_ref[...])

    return pl.pallas_call(
        body, name="adamw", in_specs=[_full((rows, c))] * 4, out_specs=[_full((rows, c))] * 3,
        out_shape=[jax.ShapeDtypeStruct((rows, c), F32)] * 3,
    )(w, g, m, v)


WEIGHTS = ["mix_norm_pre", "mix_norm_post", "ffn_norm_pre", "ffn_norm_post", "ev_w_in", "ev_a_conv_w", "ev_a_conv_b",
           "ev_a_ln_g", "ev_a_ln_b", "ev_sinks", "ev_w_out", "od_w_in", "od_conv_w", "od_w_out", "ffn_w_up",
           "ffn_conv_w", "ffn_w_down"]
BIG = ["ev_w_in", "ev_w_out", "od_w_in", "od_w_out", "ffn_w_up", "ffn_w_down"]
SMALL_REPL = ["mix_norm_pre", "mix_norm_post", "ffn_norm_pre", "ffn_norm_post", "ev_a_conv_b", "ev_a_ln_g",
              "ev_a_ln_b", "ev_sinks"]
SMALL_SHARDED = ["ev_a_conv_w", "od_conv_w", "ffn_conv_w"]


def _pack(arrs, rows):
    flat = jnp.concatenate([a.reshape(-1) for a in arrs])
    return jnp.pad(flat, (0, rows * LANES - flat.shape[0])).reshape(rows, LANES)


def _unpack(packed, shapes):
    flat, out, off = packed.reshape(-1), [], 0
    for s in shapes:
        n = 1
        for d in s:
            n *= d
        out.append(flat[off:off + n].reshape(s))
        off += n
    return out


def kernel(x, positions, mix_norm_pre, mix_norm_post, ffn_norm_pre, ffn_norm_post, ev_w_in, ev_a_conv_w, ev_a_conv_b, ev_a_ln_g, ev_a_ln_b, ev_sinks, ev_w_out, od_w_in, od_conv_w, od_w_out, ffn_w_up, ffn_conv_w, ffn_w_down, loss_target, m_mix_norm_pre, m_mix_norm_post, m_ffn_norm_pre, m_ffn_norm_post, m_ev_w_in, m_ev_a_conv_w, m_ev_a_conv_b, m_ev_a_ln_g, m_ev_a_ln_b, m_ev_sinks, m_ev_w_out, m_od_w_in, m_od_conv_w, m_od_w_out, m_ffn_w_up, m_ffn_conv_w, m_ffn_w_down, v_mix_norm_pre, v_mix_norm_post, v_ffn_norm_pre, v_ffn_norm_post, v_ev_w_in, v_ev_a_conv_w, v_ev_a_conv_b, v_ev_a_ln_g, v_ev_a_ln_b, v_ev_sinks, v_ev_w_out, v_od_w_in, v_od_conv_w, v_od_w_out, v_ffn_w_up, v_ffn_conv_w, v_ffn_w_down):
    w = dict(zip(WEIGHTS, (mix_norm_pre, mix_norm_post, ffn_norm_pre, ffn_norm_post, ev_w_in, ev_a_conv_w, ev_a_conv_b,
                           ev_a_ln_g, ev_a_ln_b, ev_sinks, ev_w_out, od_w_in, od_conv_w, od_w_out, ffn_w_up, ffn_conv_w,
                           ffn_w_down)))
    mom = dict(zip(WEIGHTS, (m_mix_norm_pre, m_mix_norm_post, m_ffn_norm_pre, m_ffn_norm_post, m_ev_w_in, m_ev_a_conv_w,
                             m_ev_a_conv_b, m_ev_a_ln_g, m_ev_a_ln_b, m_ev_sinks, m_ev_w_out, m_od_w_in, m_od_conv_w,
                             m_od_w_out, m_ffn_w_up, m_ffn_conv_w, m_ffn_w_down)))
    var = dict(zip(WEIGHTS, (v_mix_norm_pre, v_mix_norm_post, v_ffn_norm_pre, v_ffn_norm_post, v_ev_w_in, v_ev_a_conv_w,
                             v_ev_a_conv_b, v_ev_a_ln_g, v_ev_a_ln_b, v_ev_sinks, v_ev_w_out, v_od_w_in, v_od_conv_w,
                             v_od_w_out, v_ffn_w_up, v_ffn_conv_w, v_ffn_w_down)))
    ix, iy, ic = lax.axis_index("x"), lax.axis_index("y"), lax.axis_index("c")
    dev = 4 * ix + 2 * iy + ic
    two = lambda a: a.reshape(-1, a.shape[-1])

    dev1 = jnp.reshape(dev, (1,)).astype(jnp.int32)
    shard = {n: two(w[n].astype(BF16)) for n in BIG}
    gathered = lambda n, a: a.reshape((N_DEV,) + w[n].shape)
    ev_names = [n for n in BIG if n.startswith("ev_")]
    ev_gat = _all_gather([shard[n] for n in ev_names] + [_pack([w[n] for n in SMALL_SHARDED], 8)], "gather_ev")
    p = _prep_ev({n: gathered(n, a) for n, a in zip(ev_names, ev_gat)})
    rest_names = [n for n in BIG if not n.startswith("ev_")]
    first = shard[rest_names[0]] + (ev_gat[0][0:1, 0:1] * 0).astype(BF16)
    started = _exchange_start([first] + [shard[n] for n in rest_names[1:]], False, "gather_start")

    def rest_weights(after):
        lands = _exchange_wait(started, False, after, "gather_wait")
        return _prep_rest({n: gathered(n, a) for n, a in zip(rest_names, lands)})

    small = {n: w[n] for n in SMALL_REPL}
    small_shapes = [w[n].shape for n in SMALL_SHARDED]
    conv_gat = ev_gat[len(ev_names)].reshape(N_DEV, 8, LANES)
    per_dev = [_unpack(conv_gat[d], small_shapes) for d in range(N_DEV)]
    for k, n in enumerate(SMALL_SHARDED):
        small[n] = jnp.concatenate([per_dev[d][k] for d in range(N_DEV)], axis=-1)

    exchanges = []

    def grads_ready(names, big):
        blocks = lambda a, n: a.reshape(N_DEV, -1, w[n].shape[-1])
        bufs = [blocks(big[n], n) for n in names]
        payload = [blocks(big[n + ":bf16"], n) if n + ":bf16" in big else _to_bf16(b) for n, b in zip(names, bufs)]
        st = _exchange_start(payload, True, "grads_start_" + names[0])
        exchanges.append((names, bufs, st))
        return st[-1][0, 0]

    lpart, grad_x, big, g = _local_step(x[0], positions[0], loss_target[0], p, rest_weights, small, started[-1][0, 0],
                                        grads_ready)
    loss = lax.psum(lpart, ("x", "y", "c"))

    grads, delta, new_m, new_v = {}, {}, {}, {}
    for names, bufs, st in exchanges:
        lands = _exchange_wait(st, True, grad_x, "grads_wait_" + names[0])
        for n, b, land in zip(names, bufs, lands):
            outs = _adamw_rs(b, land, two(w[n]), two(mom[n]), two(var[n]), dev1)
            grads[n], delta[n], new_m[n], new_v[n] = (a.reshape(w[n].shape) for a in outs)

    small_names = SMALL_REPL + SMALL_SHARDED
    s_all = _sum_blocks(_all_gather([_pack([g[n] for n in small_names], 64)], "gather_small_grads")[0], N_DEV)
    for n, a in zip(small_names, _unpack(s_all, [small[n].shape for n in small_names])):
        if n in SMALL_SHARDED:
            width = w[n].shape[-1]
            a = lax.dynamic_slice_in_dim(a, dev * width, width, axis=a.ndim - 1)
        grads[n] = a
    pk = lambda dct: _pack([dct[n] for n in small_names], 16)
    outs = _adamw(pk(w), pk(grads), pk(mom), pk(var))
    for dst, packed in zip((delta, new_m, new_v), outs):
        for n, a in zip(small_names, _unpack(packed, [w[n].shape for n in small_names])):
            dst[n] = a

    return (loss, grad_x[None], *[grads[n] for n in WEIGHTS], *[delta[n] for n in WEIGHTS],
            *[new_m[n] for n in WEIGHTS], *[new_v[n] for n in WEIGHTS])
```

```python
import jax
import jax.numpy as jnp
from jax import lax
from jax.experimental import pallas as pl
from jax.experimental.pallas import tpu as pltpu

F32, BF16 = jnp.float32, jnp.bfloat16

D_MODEL = 1024
A_CH = 512
A_CONV = 31
Q_DIM = 512
KV_DIM = 128
HEAD_DIM = 64
N_Q_HEADS = 8
N_KV_HEADS = 2
GROUP = 4
BLOCK = 128
EVEN_IN = 1792
SC_DIM = 1024
D_FF = 2816
ROPE_THETA = 500000.0
ROPE_DIM = 16
RMS_EPS = 1e-6
LN_EPS = 1e-5
SCALE = HEAD_DIM ** -0.5
NEG = -1e30

ADAM_LR, ADAM_B1, ADAM_B2, ADAM_EPS, ADAM_WD, ADAM_STEP = 0.001, 0.9, 0.999, 1e-08, 0.01, 10

N_DEV = 8
FF_N = 2 * D_FF // N_DEV
LANES = 1024
HALO3 = 8
HALO31 = 32
VMEM_LIMIT = 56 * 1024 * 1024

TM = 512
TM_BWD = 256
TK_DW = 2048
ATT_NB = 4
SUB = 128

_ANY = pl.BlockSpec(memory_space=pl.ANY)
_CONTRACT_LAST = (((1,), (1,)), ((), ()))
_CONTRACT_FIRST = (((0,), (0,)), ((), ()))


def _cp(sem, vmem=VMEM_LIMIT):
    return pltpu.CompilerParams(dimension_semantics=sem, vmem_limit_bytes=vmem)


def _full(shape):
    n = len(shape)
    return pl.BlockSpec(shape, lambda *_: (0,) * n)


def _rows(tm, n):
    return pl.BlockSpec((tm, n), lambda i, *_: (i, 0))


def _sigmoid(x):
    return 0.5 * jnp.tanh(0.5 * x) + 0.5


def _rsqrt_mean(x):
    return lax.rsqrt(jnp.mean(x * x, axis=-1, keepdims=True) + RMS_EPS)


def _rms_bwd(x, g, dy):
    r = _rsqrt_mean(x)
    xh = x * r
    dxh = dy * g
    dx = r * (dxh - xh * jnp.mean(dxh * xh, axis=-1, keepdims=True))
    return dx, jnp.sum(dy * xh, axis=0, keepdims=True)


def _acc_out(ref, first, val):
    @pl.when(first)
    def _():
        ref[...] = val

    @pl.when(jnp.logical_not(first))
    def _():
        ref[...] += val


def _rope_tables(positions):
    half = ROPE_DIM // 2
    inv_freq = ROPE_THETA ** (-(jnp.arange(half, dtype=F32) * 2.0 / ROPE_DIM))
    ang = positions.astype(F32)[:, None] * inv_freq
    cos, sin = jnp.cos(ang), jnp.sin(ang)
    t = positions.shape[0]
    one, zero = jnp.ones((t, HEAD_DIM - ROPE_DIM), F32), jnp.zeros((t, HEAD_DIM - ROPE_DIM), F32)
    z8 = jnp.zeros((t, half), F32)
    c = jnp.concatenate([cos, cos, one], axis=1)
    sa = jnp.concatenate([z8, sin, zero], axis=1)
    sb = jnp.concatenate([-sin, z8, zero], axis=1)
    return tuple(jnp.tile(a, (1, 2)) for a in (c, sa, sb))


def _rope(t, c, sa, sb):
    return t * c + pltpu.roll(t, 8, 1) * sa + pltpu.roll(t, 120, 1) * sb


def _rope_bwd(d, c, sa, sb):
    return d * c + pltpu.roll(d * sa, 120, 1) + pltpu.roll(d * sb, 8, 1)


def _ev_in(x, gpre, w_in, rc, rsa, rsb):
    t = x.shape[0]
    tm = min(TM, t)

    def body(x_ref, g_ref, w_ref, c_ref, sa_ref, sb_ref, h_ref, zag_ref, q_ref, k_ref, v_ref):
        xv = x_ref[...]
        h = (xv * _rsqrt_mean(xv) * g_ref[...]).astype(BF16)
        h_ref[...] = h
        z = jnp.dot(h, w_ref[...], preferred_element_type=F32)
        zag_ref[...] = z[:, :2 * A_CH].astype(BF16)
        c, sa, sb = c_ref[...], sa_ref[...], sb_ref[...]
        q0 = 2 * A_CH
        for j in range(Q_DIM // 128):
            q_ref[:, 128 * j:128 * (j + 1)] = _rope(z[:, q0 + 128 * j:q0 + 128 * (j + 1)], c, sa, sb).astype(BF16)
        k0 = q0 + Q_DIM
        k_ref[...] = _rope(z[:, k0:k0 + KV_DIM], c, sa, sb).astype(BF16)
        v_ref[...] = z[:, k0 + KV_DIM:k0 + 2 * KV_DIM].astype(BF16)

    return pl.pallas_call(
        body, name="ev_in", grid=(t // tm,),
        in_specs=[_rows(tm, D_MODEL), _full((1, D_MODEL)), _full((D_MODEL, EVEN_IN)),
                  _rows(tm, 128), _rows(tm, 128), _rows(tm, 128)],
        out_specs=[_rows(tm, D_MODEL), _rows(tm, 2 * A_CH), _rows(tm, Q_DIM), _rows(tm, KV_DIM), _rows(tm, KV_DIM)],
        out_shape=[jax.ShapeDtypeStruct((t, D_MODEL), BF16), jax.ShapeDtypeStruct((t, 2 * A_CH), BF16),
                   jax.ShapeDtypeStruct((t, Q_DIM), BF16), jax.ShapeDtypeStruct((t, KV_DIM), BF16),
                   jax.ShapeDtypeStruct((t, KV_DIM), BF16)],
        compiler_params=_cp(("arbitrary",)),
    )(x, gpre, w_in, rc, rsa, rsb)


def _glu(zag):
    z = zag.astype(F32)
    return z[:, :A_CH] * _sigmoid(z[:, A_CH:])


def _tap_copies(ext, cbuf, first_row, rows):
    for b in range(1, 8):
        s = first_row(b)
        cbuf[b - 1] = ext[s:s + rows, :]


def _conf_fwd(zag, conv_w, conv_b, ln_g, ln_b):
    t = zag.shape[0]
    tm = min(TM_BWD, t)
    rows = tm + HALO31 - 8

    def body(z_ref, w_ref, b_ref, g_ref, lb_ref, c_ref, a_ref, ext, cbuf):
        i = pl.program_id(0)

        @pl.when(i == 0)
        def _():
            ext[0:HALO31, :] = jnp.zeros((HALO31, A_CH), F32)

        ext[HALO31:HALO31 + tm, :] = _glu(z_ref[...])
        _tap_copies(ext, cbuf, lambda b: 8 - b, rows)
        for rs in range(0, tm, SUB):
            for cs in range(0, A_CH, 128):
                acc = jnp.zeros((SUB, 128), F32)
                for k in range(A_CONV):
                    lag_a, lag_b = divmod(k, 8)
                    r0 = HALO31 - 8 - 8 * lag_a + rs
                    src = (ext[r0 + 8:r0 + 8 + SUB, cs:cs + 128] if lag_b == 0
                           else cbuf[lag_b - 1, r0:r0 + SUB, cs:cs + 128])
                    acc = acc + w_ref[A_CONV - 1 - k:A_CONV - k, cs:cs + 128] * src
                c_ref[rs:rs + SUB, cs:cs + 128] = acc
        ext[0:HALO31, :] = ext[tm:tm + HALO31, :]
        cv = c_ref[...] + b_ref[...]
        c_ref[...] = cv
        mu = jnp.mean(cv, axis=-1, keepdims=True)
        xc = cv - mu
        ln = xc * lax.rsqrt(jnp.mean(xc * xc, axis=-1, keepdims=True) + LN_EPS) * g_ref[...] + lb_ref[...]
        a_ref[...] = (ln * _sigmoid(ln)).astype(BF16)

    return pl.pallas_call(
        body, name="conf_fwd", grid=(t // tm,),
        in_specs=[_rows(tm, 2 * A_CH), _full((32, A_CH)), _full((1, A_CH)), _full((1, A_CH)), _full((1, A_CH))],
        out_specs=[_rows(tm, A_CH), _rows(tm, A_CH)],
        out_shape=[jax.ShapeDtypeStruct((t, A_CH), F32), jax.ShapeDtypeStruct((t, A_CH), BF16)],
        scratch_shapes=[pltpu.VMEM((HALO31 + tm, A_CH), F32), pltpu.VMEM((7, rows, A_CH), F32)],
        compiler_params=_cp(("arbitrary",)),
    )(zag, conv_w, conv_b, ln_g, ln_b)


def _attn_mask(first_block):
    row = lax.broadcasted_iota(jnp.int32, (GROUP * BLOCK, 2 * BLOCK), 0) & (BLOCK - 1)
    col = lax.broadcasted_iota(jnp.int32, (GROUP * BLOCK, 2 * BLOCK), 1)
    diff = row + BLOCK - col
    return (diff >= 0) & (diff < BLOCK) & ((col >= BLOCK) | jnp.logical_not(first_block))


def _sink_rows(s_ref, h):
    grp = lax.broadcasted_iota(jnp.int32, (GROUP * BLOCK, 1), 0) >> 7
    out = jnp.full((GROUP * BLOCK, 1), s_ref[GROUP * h], F32)
    for g in range(1, GROUP):
        out = jnp.where(grp == g, s_ref[GROUP * h + g], out)
    return out


def _attn_probs(q4, k2, mask, sink):
    s = lax.dot_general(q4, k2, _CONTRACT_LAST, preferred_element_type=F32) * SCALE
    s = jnp.where(mask, s, NEG)
    m = jnp.maximum(jnp.max(s, axis=-1, keepdims=True), sink)
    p = jnp.exp(s - m)
    es = jnp.exp(sink - m)
    inv = 1.0 / (jnp.sum(p, axis=-1, keepdims=True) + es)
    return p * inv, es * inv


def _q_heads(q, h):
    return jnp.concatenate([q[:, HEAD_DIM * (GROUP * h + g):HEAD_DIM * (GROUP * h + g + 1)] for g in range(GROUP)],
                           axis=0)


def _kv_head(prev, cur, h):
    return jnp.concatenate([prev[:, HEAD_DIM * h:HEAD_DIM * (h + 1)], cur[:, HEAD_DIM * h:HEAD_DIM * (h + 1)]], axis=0)


def _attn_fwd(q, k, v, sinks):
    t = q.shape[0]
    nb = min(ATT_NB, t // BLOCK)
    rows = nb * BLOCK

    def body(s_ref, q_ref, kc_ref, kp_ref, vc_ref, vp_ref, o_ref):
        first = pl.program_id(0) == 0
        for b in range(nb):
            lo = BLOCK * b
            mask = _attn_mask(first) if b == 0 else _attn_mask(False)
            qv, kc, vc = q_ref[lo:lo + BLOCK, :], kc_ref[lo:lo + BLOCK, :], vc_ref[lo:lo + BLOCK, :]
            kp = kp_ref[...] if b == 0 else kc_ref[lo - BLOCK:lo, :]
            vp = vp_ref[...] if b == 0 else vc_ref[lo - BLOCK:lo, :]
            for h in range(N_KV_HEADS):
                pn, _ = _attn_probs(_q_heads(qv, h), _kv_head(kp, kc, h), mask, _sink_rows(s_ref, h))
                o4 = jnp.dot(pn.astype(BF16), _kv_head(vp, vc, h), preferred_element_type=F32).astype(BF16)
                for g in range(GROUP):
                    c0 = HEAD_DIM * (GROUP * h + g)
                    o_ref[lo:lo + BLOCK, c0:c0 + HEAD_DIM] = o4[BLOCK * g:BLOCK * (g + 1), :]

    cur = lambda n: pl.BlockSpec((rows, n), lambda i: (i, 0))
    prev = lambda n: pl.BlockSpec((BLOCK, n), lambda i: (jnp.maximum(i * nb - 1, 0), 0))
    return pl.pallas_call(
        body, name="attn_fwd", grid=(t // rows,),
        in_specs=[pl.BlockSpec(memory_space=pltpu.SMEM), cur(Q_DIM), cur(KV_DIM), prev(KV_DIM), cur(KV_DIM),
                  prev(KV_DIM)],
        out_specs=cur(Q_DIM),
        out_shape=jax.ShapeDtypeStruct((t, Q_DIM), BF16),
        compiler_params=_cp(("arbitrary",)),
    )(sinks, q, k, k, v, v)


def _out_post(lhs, ws, x_in, gpost):
    t = x_in.shape[0]
    tm = min(TM, t)
    n = len(lhs)

    def body(*refs):
        x_ref, g_ref, m_ref, xo_ref = refs[2 * n:]
        m = jnp.dot(refs[0][...], refs[n][...], preferred_element_type=F32)
        for j in range(1, n):
            m = m + jnp.dot(refs[j][...], refs[n + j][...], preferred_element_type=F32)
        m_ref[...] = m
        xo_ref[...] = x_ref[...] + m * _rsqrt_mean(m) * g_ref[...]

    return pl.pallas_call(
        body, name="out_post", grid=(t // tm,),
        in_specs=[_rows(tm, a.shape[1]) for a in lhs] + [_full(w.shape) for w in ws]
                 + [_rows(tm, D_MODEL), _full((1, D_MODEL))],
        out_specs=[_rows(tm, D_MODEL), _rows(tm, D_MODEL)],
        out_shape=[jax.ShapeDtypeStruct((t, D_MODEL), F32)] * 2,
        compiler_params=_cp(("arbitrary",)),
    )(*lhs, *ws, x_in, gpost)


def _conv3(w_ref, ext, tm):
    s = HALO3 - 2
    return (w_ref[0:1, :] * ext[s:s + tm, :] + w_ref[1:2, :] * ext[s + 1:s + 1 + tm, :]
            + w_ref[2:3, :] * ext[s + 2:s + 2 + tm, :])


def _ffn_fwd(x1, gpre, wup, layer, cw, wd, gpost):
    t = x1.shape[0]
    tm = min(TM, t)
    nc, n = wup.shape[1], wup.shape[4]

    def body(x_ref, gpre_ref, wup_ref, cw_ref, wd_ref, gpost_ref, h_ref, up_ref, u_ref, f_ref, xo_ref, h_s, acc, ext, hal):
        i, c = pl.program_id(0), pl.program_id(1)

        @pl.when(c == 0)
        def _():
            xv = x_ref[...]
            h = (xv * _rsqrt_mean(xv) * gpre_ref[...]).astype(BF16)
            h_s[...] = h
            h_ref[...] = h

        @pl.when(i == 0)
        def _():
            hal[c] = jnp.zeros((2, HALO3, n), F32)

        u = []
        for gv in range(2):
            up = jnp.dot(h_s[...], wup_ref[gv, 0, 0], preferred_element_type=F32)
            up_ref[gv, 0] = up.astype(BF16)
            ext[gv, 0:HALO3, :] = hal[c, gv]
            ext[gv, HALO3:HALO3 + tm, :] = up
            hal[c, gv] = ext[gv, tm:tm + HALO3, :]
            s = HALO3 - 2
            u.append(cw_ref[gv, 0, 0:1, :] * ext[gv, s:s + tm, :] + cw_ref[gv, 0, 1:2, :] * ext[gv, s + 1:s + 1 + tm, :]
                     + cw_ref[gv, 0, 2:3, :] * up)
            u_ref[gv, 0] = u[gv].astype(BF16)
        act = (u[0] * _sigmoid(u[0]) * u[1]).astype(BF16)
        part = jnp.dot(act, wd_ref[...], preferred_element_type=F32)

        @pl.when(c == 0)
        def _():
            acc[...] = part

        @pl.when(jnp.logical_and(c > 0, c < nc - 1))
        def _():
            acc[...] += part

        @pl.when(c == nc - 1)
        def _():
            f = acc[...] + part
            f_ref[...] = f
            xo_ref[...] = x_ref[...] + f * _rsqrt_mean(f) * gpost_ref[...]

    row = lambda w: pl.BlockSpec((tm, w), lambda i, c: (i, 0))
    one = _full((1, D_MODEL))
    return pl.pallas_call(
        body, name="ffn_fwd", grid=(t // tm, nc),
        in_specs=[row(D_MODEL), one, pl.BlockSpec((2, 1, 1, D_MODEL, n), lambda i, c: (0, c, layer, 0, 0)),
                  pl.BlockSpec((2, 1, 3, n), lambda i, c: (0, c, 0, 0)), pl.BlockSpec((n, D_MODEL), lambda i, c: (c, 0)),
                  one],
        out_specs=[row(D_MODEL), pl.BlockSpec((2, 1, tm, n), lambda i, c: (0, c, i, 0)),
                   pl.BlockSpec((2, 1, tm, n), lambda i, c: (0, c, i, 0)), row(D_MODEL), row(D_MODEL)],
        out_shape=[jax.ShapeDtypeStruct((t, D_MODEL), BF16), jax.ShapeDtypeStruct((2, nc, t, n), BF16),
                   jax.ShapeDtypeStruct((2, nc, t, n), BF16), jax.ShapeDtypeStruct((t, D_MODEL), F32),
                   jax.ShapeDtypeStruct((t, D_MODEL), F32)],
        scratch_shapes=[pltpu.VMEM((tm, D_MODEL), BF16), pltpu.VMEM((tm, D_MODEL), F32),
                        pltpu.VMEM((2, HALO3 + tm, n), F32), pltpu.VMEM((nc, 2, HALO3, n), F32)],
        compiler_params=_cp(("arbitrary", "arbitrary")),
    )(x1, gpre, wup, cw, wd, gpost)


def _od_fwd(x_in, gpre, w_in, cw, w_out, gpost):
    t = x_in.shape[0]
    tm = min(TM, t)
    ns, _, n = w_in.shape

    def body(x_ref, gpre_ref, w_ref, cw_ref, wo_ref, gpost_ref, h_ref, z_ref, cv_ref, y_ref, m_ref, xo_ref, z_s, ext):
        i = pl.program_id(0)
        xv = x_ref[...]
        h = (xv * _rsqrt_mean(xv) * gpre_ref[...]).astype(BF16)
        h_ref[...] = h
        for j in range(ns):
            z_s[:, n * j:n * (j + 1)] = jnp.dot(h, w_ref[j], preferred_element_type=F32)
        z_ref[...] = z_s[...].astype(BF16)

        @pl.when(i == 0)
        def _():
            ext[0:HALO3, :] = jnp.zeros((HALO3, SC_DIM), F32)

        ext[HALO3:HALO3 + tm, :] = z_s[:, SC_DIM:2 * SC_DIM] * z_s[:, 2 * SC_DIM:]
        cv = _conv3(cw_ref, ext, tm)
        cv_ref[...] = cv.astype(BF16)
        y = (z_s[:, :SC_DIM] * cv).astype(BF16)
        ext[0:HALO3, :] = ext[tm:tm + HALO3, :]
        y_ref[...] = y
        m = jnp.dot(y, wo_ref[...], preferred_element_type=F32)
        m_ref[...] = m
        xo_ref[...] = xv + m * _rsqrt_mean(m) * gpost_ref[...]

    return pl.pallas_call(
        body, name="od_fwd", grid=(t // tm,),
        in_specs=[_rows(tm, D_MODEL), _full((1, D_MODEL)), _full((ns, D_MODEL, n)), _full((3, SC_DIM)),
                  _full((SC_DIM, D_MODEL)), _full((1, D_MODEL))],
        out_specs=[_rows(tm, D_MODEL), _rows(tm, 3 * SC_DIM), _rows(tm, SC_DIM), _rows(tm, SC_DIM), _rows(tm, D_MODEL),
                   _rows(tm, D_MODEL)],
        out_shape=[jax.ShapeDtypeStruct((t, D_MODEL), BF16), jax.ShapeDtypeStruct((t, 3 * SC_DIM), BF16),
                   jax.ShapeDtypeStruct((t, SC_DIM), BF16), jax.ShapeDtypeStruct((t, SC_DIM), BF16),
                   jax.ShapeDtypeStruct((t, D_MODEL), F32), jax.ShapeDtypeStruct((t, D_MODEL), F32)],
        scratch_shapes=[pltpu.VMEM((tm, 3 * SC_DIM), F32), pltpu.VMEM((HALO3 + tm, SC_DIM), F32)],
        compiler_params=_cp(("arbitrary",)),
    )(x_in, gpre, w_in, cw, w_out, gpost)


def _dw2d(a, b, bm, bn):
    t, m = a.shape
    n = b.shape[1]
    tk = min(TK_DW, t)

    def body(a_ref, b_ref, o_ref):
        part = lax.dot_general(a_ref[...], b_ref[...], _CONTRACT_FIRST, preferred_element_type=F32)
        _acc_out(o_ref, pl.program_id(2) == 0, part)

    return pl.pallas_call(
        body, name="dw2d", grid=(m // bm, n // bn, t // tk),
        in_specs=[pl.BlockSpec((tk, bm), lambda i, j, k: (k, i)), pl.BlockSpec((tk, bn), lambda i, j, k: (k, j))],
        out_specs=pl.BlockSpec((bm, bn), lambda i, j, k: (i, j)),
        out_shape=jax.ShapeDtypeStruct((m, n), F32),
        compiler_params=_cp(("arbitrary", "arbitrary", "arbitrary")),
    )(a, b)


def _dw_cols(a, b, n_blk):
    t, m = a.shape
    s = b.shape[1] // n_blk
    tk = min(TK_DW, t)
    nk = t // tk

    def body(a_ref, b_ref, o_ref, ob_ref):
        part = lax.dot_general(a_ref[...], b_ref[...], _CONTRACT_FIRST, preferred_element_type=F32)
        _acc_out(o_ref.at[0], pl.program_id(1) == 0, part)

        @pl.when(pl.program_id(1) == nk - 1)
        def _():
            ob_ref[...] = o_ref[...].astype(BF16)

    spec = pl.BlockSpec((1, m, n_blk), lambda j, k: (j, 0, 0))
    return pl.pallas_call(
        body, name="dw_cols", grid=(s, nk),
        in_specs=[pl.BlockSpec((tk, m), lambda j, k: (k, 0)), pl.BlockSpec((tk, n_blk), lambda j, k: (k, j))],
        out_specs=[spec, spec],
        out_shape=[jax.ShapeDtypeStruct((s, m, n_blk), F32), jax.ShapeDtypeStruct((s, m, n_blk), BF16)],
        compiler_params=_cp(("arbitrary", "arbitrary")),
    )(a, b)


def _dw_up(h, dup, layer, buf):
    t, m = h.shape
    s, _, n = dup.shape
    tk = min(TK_DW, t)
    nk = t // tk

    def body(*refs):
        a_ref, b_ref, o_ref, ob_ref = refs[0], refs[1], refs[-2], refs[-1]
        part = lax.dot_general(a_ref[...], b_ref[0], _CONTRACT_FIRST, preferred_element_type=F32)
        _acc_out(o_ref.at[0, 0], pl.program_id(1) == 0, part)

        @pl.when(pl.program_id(1) == nk - 1)
        def _():
            ob_ref[...] = o_ref[...].astype(BF16)

    spec = pl.BlockSpec((1, 1, m, n), lambda j, k: (j, layer, 0, 0))
    return pl.pallas_call(
        body, name="dw_up", grid=(s, nk),
        in_specs=[pl.BlockSpec((tk, m), lambda j, k: (k, 0)), pl.BlockSpec((1, tk, n), lambda j, k: (j, k, 0))]
                 + ([] if buf is None else [_ANY, _ANY]),
        out_specs=[spec, spec],
        out_shape=[jax.ShapeDtypeStruct((s, 2, m, n), F32), jax.ShapeDtypeStruct((s, 2, m, n), BF16)],
        input_output_aliases={} if buf is None else {2: 0, 3: 1},
        compiler_params=_cp(("arbitrary", "arbitrary")),
    )(h, dup, *([] if buf is None else buf))


def _dw_down(act, df, layer, buf):
    nc, t, n = act.shape
    d = df.shape[1]
    tk = min(TK_DW, t)
    nk = t // tk

    def body(*refs):
        a_ref, b_ref, o_ref, ob_ref = refs[0], refs[1], refs[-2], refs[-1]
        part = lax.dot_general(a_ref[0], b_ref[...], _CONTRACT_FIRST, preferred_element_type=F32)
        part = part.reshape(2, n // 2, d)
        first = pl.program_id(1) == 0

        @pl.when(first)
        def _():
            o_ref[:, 0] = part

        @pl.when(jnp.logical_not(first))
        def _():
            o_ref[:, 0] += part

        @pl.when(pl.program_id(1) == nk - 1)
        def _():
            ob_ref[...] = o_ref[...].astype(BF16)

    spec = pl.BlockSpec((2, 1, n // 2, d), lambda c, k: (c, layer, 0, 0))
    return pl.pallas_call(
        body, name="dw_down", grid=(nc, nk),
        in_specs=[pl.BlockSpec((1, tk, n), lambda c, k: (c, k, 0)), pl.BlockSpec((tk, d), lambda c, k: (k, 0))]
                 + ([] if buf is None else [_ANY, _ANY]),
        out_specs=[spec, spec],
        out_shape=[jax.ShapeDtypeStruct((2 * nc, 2, n // 2, d), F32), jax.ShapeDtypeStruct((2 * nc, 2, n // 2, d), BF16)],
        input_output_aliases={} if buf is None else {2: 0, 3: 1},
        compiler_params=_cp(("arbitrary", "arbitrary")),
    )(act, df, *([] if buf is None else buf))


def _dz_wt_rms_bwd(dz, w, x_in, gpre, dres):
    t, n = dz.shape
    tm = min(TM, t)

    def body(dz_ref, wt_ref, x_ref, g_ref, dres_ref, dx_ref, dg_ref):
        dh = lax.dot_general(dz_ref[...], wt_ref[...], _CONTRACT_LAST, preferred_element_type=F32)
        dx, dg = _rms_bwd(x_ref[...], g_ref[...], dh)
        dx_ref[...] = dres_ref[...] + dx
        _acc_out(dg_ref, pl.program_id(0) == 0, dg)

    return pl.pallas_call(
        body, name="dz_wt_rms_bwd", grid=(t // tm,),
        in_specs=[_rows(tm, n), _full((D_MODEL, n)), _rows(tm, D_MODEL), _full((1, D_MODEL)), _rows(tm, D_MODEL)],
        out_specs=[_rows(tm, D_MODEL), _full((1, D_MODEL))],
        out_shape=[jax.ShapeDtypeStruct((t, D_MODEL), F32), jax.ShapeDtypeStruct((1, D_MODEL), F32)],
        compiler_params=_cp(("arbitrary",)),
    )(dz, w, x_in, gpre, dres)


def _shift_matrices(shift, shift_h, tm, hb):
    row = lax.broadcasted_iota(jnp.int32, (2 * tm, tm), 0)
    col = lax.broadcasted_iota(jnp.int32, (2 * tm, tm), 1)
    hit = ((row < tm) & (col == row + 1)) | ((row >= tm) & (col == row - tm + 2))
    shift[...] = jnp.where(hit, 1.0, 0.0).astype(BF16)
    row = lax.broadcasted_iota(jnp.int32, (hb, hb), 0)
    col = lax.broadcasted_iota(jnp.int32, (hb, hb), 1)
    hit = ((row < HALO3) & (col == row - (HALO3 - 1))) | ((row >= HALO3) & (col == row - (2 * HALO3 - 2)))
    shift_h[...] = jnp.where(hit, 1.0, 0.0).astype(BF16)


def _next_rows(shift, shift_h, xb, nxt, d12_s, tm):
    d12_s[...] = jnp.dot(shift[...], xb, preferred_element_type=F32)
    edge = jnp.dot(shift_h[...], nxt, preferred_element_type=F32)
    d12_s[tm - HALO3:tm, :] += edge[0:HALO3, :]
    d12_s[2 * tm - HALO3:2 * tm, :] += edge[HALO3:2 * HALO3, :]


def _ffn_bwd(f, dxo, gpost, x_in, gpre, up, u, cw, wd, wup, layer, target=None):
    t = f.shape[0]
    tm = min(TM_BWD, t)
    nt = t // tm
    nc, n = up.shape[1], up.shape[3]
    hb = 2 * HALO3

    def body(*refs):
        f_ref, dxo_ref, gpost_ref, x_ref, gpre_ref, up_ref, u_ref, cw_ref, wd_ref, wup_ref = refs[:10]
        n_in = 10 if target is None else 11
        n_out = 7 if target is None else 8
        df_ref, act_ref, dup_ref, dx_ref, dgpost_ref, dgpre_ref, dcw_ref = refs[n_in:n_in + 7]
        df_s, acc, du_s, dub_s, d12_s, hal, shift, shift_h = refs[n_in + n_out:]
        i, c = pl.program_id(0), pl.program_id(1)

        def incoming():
            if target is None:
                return dxo_ref[...]
            return (dxo_ref[...] - refs[10][...]) * (1.0 / D_MODEL)

        @pl.when(c == 0)
        def _():
            dy = incoming()
            df, dg = _rms_bwd(f_ref[...], gpost_ref[...], dy)
            df_s[...] = df.astype(BF16)
            df_ref[...] = df.astype(BF16)
            _acc_out(dgpost_ref, i == 0, dg)
            if target is not None:
                part = jnp.zeros((1, 128), F32) + jnp.sum(dy * dy) * (0.5 * D_MODEL)
                _acc_out(refs[n_in + 7], i == 0, part)

        @pl.when(i == 0)
        def _():
            hal[c] = jnp.zeros((2, hb, n), BF16)
            dcw_ref[0, c] = jnp.zeros((8, n), F32)
            dcw_ref[1, c] = jnp.zeros((8, n), F32)

        @pl.when(jnp.logical_and(i == 0, c == 0))
        def _():
            _shift_matrices(shift, shift_h, tm, hb)

        dact = lax.dot_general(df_s[...], wd_ref[...], _CONTRACT_LAST, preferred_element_type=F32)
        g, v = u_ref[0, 0].astype(F32), u_ref[1, 0].astype(F32)
        sg = _sigmoid(g)
        sil = g * sg
        act_ref[0] = (sil * v).astype(BF16)
        dug = dact * v * (sg + sil * (1.0 - sg))
        duv = dact * sil
        du_s[0], du_s[1] = dug, duv
        dub_s[0], dub_s[1] = dug.astype(BF16), duv.astype(BF16)
        dh = None
        for gv in range(2):
            _next_rows(shift, shift_h, dub_s[gv], hal[c, gv], d12_s, tm)
            hal[c, gv] = dub_s[gv, 0:hb, :]
            du, d1, d2 = du_s[gv], d12_s[0:tm, :], d12_s[tm:2 * tm, :]
            dup = (cw_ref[gv, 0, 2:3, :] * du + cw_ref[gv, 0, 1:2, :] * d1 + cw_ref[gv, 0, 0:1, :] * d2).astype(BF16)
            dup_ref[gv, 0] = dup
            upc = up_ref[gv, 0].astype(F32)
            dcw_ref[gv, c, 2:3, :] += jnp.sum(upc * du, axis=0, keepdims=True)
            dcw_ref[gv, c, 1:2, :] += jnp.sum(upc * d1, axis=0, keepdims=True)
            dcw_ref[gv, c, 0:1, :] += jnp.sum(upc * d2, axis=0, keepdims=True)
            part = lax.dot_general(dup, wup_ref[gv, 0, 0], _CONTRACT_LAST, preferred_element_type=F32)
            dh = part if dh is None else dh + part
        _acc_out(acc, c == 0, dh)

        @pl.when(c == nc - 1)
        def _():
            dx, dg = _rms_bwd(x_ref[...], gpre_ref[...], acc[...])
            dx_ref[...] = incoming() + dx
            _acc_out(dgpre_ref, i == 0, dg)

    rrow = lambda w: pl.BlockSpec((tm, w), lambda i, c: (nt - 1 - i, 0))
    blk = pl.BlockSpec((2, 1, tm, n), lambda i, c: (0, c, nt - 1 - i, 0))
    one = _full((1, D_MODEL))
    return pl.pallas_call(
        body, name="ffn_bwd", grid=(nt, nc),
        in_specs=[rrow(D_MODEL), rrow(D_MODEL), one, rrow(D_MODEL), one, blk, blk,
                  pl.BlockSpec((2, 1, 3, n), lambda i, c: (0, c, 0, 0)),
                  pl.BlockSpec((n, D_MODEL), lambda i, c: (c, 0)),
                  pl.BlockSpec((2, 1, 1, D_MODEL, n), lambda i, c: (0, c, layer, 0, 0))]
                 + ([] if target is None else [rrow(D_MODEL)]),
        out_specs=[rrow(D_MODEL), pl.BlockSpec((1, tm, n), lambda i, c: (c, nt - 1 - i, 0)), blk, rrow(D_MODEL),
                   one, one, _full((2, nc, 8, n))] + ([] if target is None else [_full((1, 128))]),
        out_shape=[jax.ShapeDtypeStruct((t, D_MODEL), BF16), jax.ShapeDtypeStruct((nc, t, n), BF16),
                   jax.ShapeDtypeStruct((2, nc, t, n), BF16), jax.ShapeDtypeStruct((t, D_MODEL), F32),
                   jax.ShapeDtypeStruct((1, D_MODEL), F32), jax.ShapeDtypeStruct((1, D_MODEL), F32),
                   jax.ShapeDtypeStruct((2, nc, 8, n), F32)]
                  + ([] if target is None else [jax.ShapeDtypeStruct((1, 128), F32)]),
        scratch_shapes=[pltpu.VMEM((tm, D_MODEL), BF16), pltpu.VMEM((tm, D_MODEL), F32),
                        pltpu.VMEM((2, tm, n), F32), pltpu.VMEM((2, tm, n), BF16), pltpu.VMEM((2 * tm, n), F32),
                        pltpu.VMEM((nc, 2, hb, n), BF16), pltpu.VMEM((2 * tm, tm), BF16), pltpu.VMEM((hb, hb), BF16)],
        compiler_params=_cp(("arbitrary", "arbitrary")),
    )(f, dxo, gpost, x_in, gpre, up, u, cw, wd, wup, *([] if target is None else [target]))


def _od_bwd(m, dxo, gpost, x_in, gpre, z, cv, cw, w_out, wint):
    t = m.shape[0]
    tm = min(TM_BWD, t)
    nt = t // tm
    hb = 2 * HALO3

    def body(m_ref, dxo_ref, gpost_ref, x_ref, gpre_ref, z_ref, cv_ref, cw_ref, wo_ref, wi_ref,
             dm_ref, dz_ref, dx_ref, dgpost_ref, dgpre_ref, dcw_ref, dcvb_s, d12_s, dz_s, hal, shift, shift_h):
        i = pl.program_id(0)
        dxo = dxo_ref[...]
        dm, dg = _rms_bwd(m_ref[...], gpost_ref[...], dxo)
        dmb = dm.astype(BF16)
        dm_ref[...] = dmb
        _acc_out(dgpost_ref, i == 0, dg)

        @pl.when(i == 0)
        def _():
            hal[...] = jnp.zeros((hb, SC_DIM), BF16)
            dcw_ref[...] = jnp.zeros((8, SC_DIM), F32)
            _shift_matrices(shift, shift_h, tm, hb)

        dy = lax.dot_general(dmb, wo_ref[...], _CONTRACT_LAST, preferred_element_type=F32)
        z = z_ref[...].astype(F32)
        b, cg, u = z[:, :SC_DIM], z[:, SC_DIM:2 * SC_DIM], z[:, 2 * SC_DIM:]
        dz_s[:, 0:SC_DIM] = (dy * cv_ref[...].astype(F32)).astype(BF16)
        dcv = dy * b
        dcvb_s[...] = dcv.astype(BF16)
        _next_rows(shift, shift_h, dcvb_s[...], hal[...], d12_s, tm)
        hal[...] = dcvb_s[0:hb, :]
        d1, d2 = d12_s[0:tm, :], d12_s[tm:2 * tm, :]
        dcu = cw_ref[2:3, :] * dcv + cw_ref[1:2, :] * d1 + cw_ref[0:1, :] * d2
        cu = cg * u
        dcw_ref[2:3, :] += jnp.sum(cu * dcv, axis=0, keepdims=True)
        dcw_ref[1:2, :] += jnp.sum(cu * d1, axis=0, keepdims=True)
        dcw_ref[0:1, :] += jnp.sum(cu * d2, axis=0, keepdims=True)
        dz_s[:, SC_DIM:2 * SC_DIM] = (dcu * u).astype(BF16)
        dz_s[:, 2 * SC_DIM:3 * SC_DIM] = (dcu * cg).astype(BF16)
        dz_ref[...] = dz_s[...]
        dh = jnp.dot(dz_s[...], wi_ref[...], preferred_element_type=F32)
        dx, dg2 = _rms_bwd(x_ref[...], gpre_ref[...], dh)
        dx_ref[...] = dxo + dx
        _acc_out(dgpre_ref, i == 0, dg2)

    rrow = lambda w: pl.BlockSpec((tm, w), lambda i: (nt - 1 - i, 0))
    one = _full((1, D_MODEL))
    return pl.pallas_call(
        body, name="od_bwd", grid=(nt,),
        in_specs=[rrow(D_MODEL), rrow(D_MODEL), one, rrow(D_MODEL), one, rrow(3 * SC_DIM), rrow(SC_DIM),
                  _full((3, SC_DIM)), _full((SC_DIM, D_MODEL)), _full((3 * SC_DIM, D_MODEL))],
        out_specs=[rrow(D_MODEL), rrow(3 * SC_DIM), rrow(D_MODEL), one, one, _full((8, SC_DIM))],
        out_shape=[jax.ShapeDtypeStruct((t, D_MODEL), BF16), jax.ShapeDtypeStruct((t, 3 * SC_DIM), BF16),
                   jax.ShapeDtypeStruct((t, D_MODEL), F32), jax.ShapeDtypeStruct((1, D_MODEL), F32),
                   jax.ShapeDtypeStruct((1, D_MODEL), F32), jax.ShapeDtypeStruct((8, SC_DIM), F32)],
        scratch_shapes=[pltpu.VMEM((tm, SC_DIM), BF16), pltpu.VMEM((2 * tm, SC_DIM), F32),
                        pltpu.VMEM((tm, 3 * SC_DIM), BF16), pltpu.VMEM((hb, SC_DIM), BF16),
                        pltpu.VMEM((2 * tm, tm), BF16), pltpu.VMEM((hb, hb), BF16)],
        compiler_params=_cp(("arbitrary",)),
    )(m, dxo, gpost, x_in, gpre, z, cv, cw, w_out, wint)


def _ev_bwd1(m, dxo, gpost, w_out):
    t = m.shape[0]
    tm = min(TM, t)

    def body(m_ref, dxo_ref, g_ref, wot_ref, dm_ref, da_ref, do_ref, dg_ref):
        dm, dg = _rms_bwd(m_ref[...], g_ref[...], dxo_ref[...])
        dmb = dm.astype(BF16)
        dm_ref[...] = dmb
        _acc_out(dg_ref, pl.program_id(0) == 0, dg)
        dao = lax.dot_general(dmb, wot_ref[...], _CONTRACT_LAST, preferred_element_type=F32)
        da_ref[...] = dao[:, :A_CH]
        do_ref[...] = dao[:, A_CH:].astype(BF16)

    return pl.pallas_call(
        body, name="ev_bwd1", grid=(t // tm,),
        in_specs=[_rows(tm, D_MODEL), _rows(tm, D_MODEL), _full((1, D_MODEL)), _full((A_CH + Q_DIM, D_MODEL))],
        out_specs=[_rows(tm, D_MODEL), _rows(tm, A_CH), _rows(tm, Q_DIM), _full((1, D_MODEL))],
        out_shape=[jax.ShapeDtypeStruct((t, D_MODEL), BF16), jax.ShapeDtypeStruct((t, A_CH), F32),
                   jax.ShapeDtypeStruct((t, Q_DIM), BF16), jax.ShapeDtypeStruct((1, D_MODEL), F32)],
        compiler_params=_cp(("arbitrary",)),
    )(m, dxo, gpost, w_out)


def _conf_bwd(da, cv, zag, conv_w, ln_g, ln_b):
    t = da.shape[0]
    tm = min(TM_BWD, t)
    nt = t // tm
    rows = tm + HALO31 - 8

    def body(da_ref, c_ref, z_ref, w_ref, g_ref, lb_ref, dz_ref, dw_ref, dv_ref, ext_out, cbuf, glu_s, dglu_s):
        i = pl.program_id(0)

        @pl.when(i == 0)
        def _():
            ext_out[tm:tm + HALO31, :] = jnp.zeros((HALO31, A_CH), F32)
            dw_ref[...] = jnp.zeros((32, A_CH), F32)
            dv_ref[...] = jnp.zeros((8, A_CH), F32)

        x = c_ref[...]
        mu = jnp.mean(x, axis=-1, keepdims=True)
        xc = x - mu
        rstd = lax.rsqrt(jnp.mean(xc * xc, axis=-1, keepdims=True) + LN_EPS)
        xh = xc * rstd
        ln = xh * g_ref[...] + lb_ref[...]
        sl = _sigmoid(ln)
        dln = da_ref[...] * (sl * (1.0 + ln * (1.0 - sl)))
        dxh = dln * g_ref[...]
        dc = rstd * (dxh - jnp.mean(dxh, axis=-1, keepdims=True) - xh * jnp.mean(dxh * xh, axis=-1, keepdims=True))
        dv_ref[0:1, :] += jnp.sum(dc, axis=0, keepdims=True)
        dv_ref[1:2, :] += jnp.sum(dln * xh, axis=0, keepdims=True)
        dv_ref[2:3, :] += jnp.sum(dln, axis=0, keepdims=True)

        ext_out[0:tm, :] = dc
        _tap_copies(ext_out, cbuf, lambda b: b, rows)
        z = z_ref[...].astype(F32)
        al, sg = z[:, :A_CH], _sigmoid(z[:, A_CH:])
        glu_s[...] = al * sg
        for rs in range(0, tm, SUB):
            for cs in range(0, A_CH, 128):
                glu = glu_s[rs:rs + SUB, cs:cs + 128]
                acc = jnp.zeros((SUB, 128), F32)
                for k in range(A_CONV):
                    lag_a, lag_b = divmod(k, 8)
                    r0 = 8 * lag_a + rs
                    d = (ext_out[r0:r0 + SUB, cs:cs + 128] if lag_b == 0
                         else cbuf[lag_b - 1, r0:r0 + SUB, cs:cs + 128])
                    j = A_CONV - 1 - k
                    acc = acc + w_ref[j:j + 1, cs:cs + 128] * d
                    dw_ref[j:j + 1, cs:cs + 128] += jnp.sum(glu * d, axis=0, keepdims=True)
                dglu_s[rs:rs + SUB, cs:cs + 128] = acc
        dglu = dglu_s[...]
        ext_out[tm:tm + HALO31, :] = ext_out[0:HALO31, :]
        dz_ref[:, 0:A_CH] = (dglu * sg).astype(BF16)
        dz_ref[:, A_CH:2 * A_CH] = (dglu * al * sg * (1.0 - sg)).astype(BF16)

    rrow = lambda w: pl.BlockSpec((tm, w), lambda i: (nt - 1 - i, 0))
    return pl.pallas_call(
        body, name="conf_bwd", grid=(nt,),
        in_specs=[rrow(A_CH), rrow(A_CH), rrow(2 * A_CH), _full((32, A_CH)), _full((1, A_CH)), _full((1, A_CH))],
        out_specs=[rrow(2 * A_CH), _full((32, A_CH)), _full((8, A_CH))],
        out_shape=[jax.ShapeDtypeStruct((t, 2 * A_CH), BF16), jax.ShapeDtypeStruct((32, A_CH), F32),
                   jax.ShapeDtypeStruct((8, A_CH), F32)],
        scratch_shapes=[pltpu.VMEM((tm + HALO31, A_CH), F32), pltpu.VMEM((7, rows, A_CH), F32),
                        pltpu.VMEM((tm, A_CH), F32), pltpu.VMEM((tm, A_CH), F32)],
        compiler_params=_cp(("arbitrary",)),
    )(da, cv, zag, conv_w, ln_g, ln_b)


def _attn_bwd(q, k, v, do, sinks):
    t = q.shape[0]
    nb = min(ATT_NB, t // BLOCK)
    rows = nb * BLOCK
    ns = t // rows

    def body(s_ref, q_ref, kc_ref, kp_ref, vc_ref, vp_ref, do_ref, dq_ref, dk_ref, dv_ref, ds_ref, dkc, dvc):
        i = pl.program_id(0)
        r = ns - 1 - i

        @pl.when(i == 0)
        def _():
            dkc[...] = jnp.zeros_like(dkc)
            dvc[...] = jnp.zeros_like(dvc)
            ds_ref[...] = jnp.zeros_like(ds_ref)

        lane = lax.broadcasted_iota(jnp.int32, (1, N_Q_HEADS), 1)
        dsv = jnp.zeros((1, N_Q_HEADS), F32)
        for b in range(nb - 1, -1, -1):
            lo = BLOCK * b
            mask = _attn_mask(r == 0) if b == 0 else _attn_mask(False)
            qv, dov = q_ref[lo:lo + BLOCK, :], do_ref[lo:lo + BLOCK, :]
            kc, vc = kc_ref[lo:lo + BLOCK, :], vc_ref[lo:lo + BLOCK, :]
            kp = kp_ref[...] if b == 0 else kc_ref[lo - BLOCK:lo, :]
            vp = vp_ref[...] if b == 0 else vc_ref[lo - BLOCK:lo, :]
            for h in range(N_KV_HEADS):
                q4, do4 = _q_heads(qv, h), _q_heads(dov, h)
                k2, v2 = _kv_head(kp, kc, h), _kv_head(vp, vc, h)
                pn, ps = _attn_probs(q4, k2, mask, _sink_rows(s_ref, h))
                dp = lax.dot_general(do4, v2, _CONTRACT_LAST, preferred_element_type=F32)
                dl = jnp.sum(pn * dp, axis=-1, keepdims=True)
                dsb = (pn * (dp - dl)).astype(BF16)
                dq4 = (jnp.dot(dsb, k2, preferred_element_type=F32) * SCALE).astype(BF16)
                for g in range(GROUP):
                    c0 = HEAD_DIM * (GROUP * h + g)
                    dq_ref[lo:lo + BLOCK, c0:c0 + HEAD_DIM] = dq4[BLOCK * g:BLOCK * (g + 1), :]
                dk2 = lax.dot_general(dsb, q4, _CONTRACT_FIRST, preferred_element_type=F32) * SCALE
                dv2 = lax.dot_general(pn.astype(BF16), do4, _CONTRACT_FIRST, preferred_element_type=F32)
                dk_ref[lo:lo + BLOCK, HEAD_DIM * h:HEAD_DIM * (h + 1)] = dk2[BLOCK:, :] + dkc[h]
                dv_ref[lo:lo + BLOCK, HEAD_DIM * h:HEAD_DIM * (h + 1)] = dv2[BLOCK:, :] + dvc[h]
                dkc[h] = dk2[:BLOCK, :]
                dvc[h] = dv2[:BLOCK, :]
                srow = -ps * dl
                for g in range(GROUP):
                    dsv = dsv + jnp.where(lane == GROUP * h + g, jnp.sum(srow[BLOCK * g:BLOCK * (g + 1), :]), 0.0)
        ds_ref[...] += dsv

    cur = lambda n: pl.BlockSpec((rows, n), lambda i: (ns - 1 - i, 0))
    prev = lambda n: pl.BlockSpec((BLOCK, n), lambda i: (jnp.maximum((ns - 1 - i) * nb - 1, 0), 0))
    return pl.pallas_call(
        body, name="attn_bwd", grid=(ns,),
        in_specs=[pl.BlockSpec(memory_space=pltpu.SMEM), cur(Q_DIM), cur(KV_DIM), prev(KV_DIM), cur(KV_DIM),
                  prev(KV_DIM), cur(Q_DIM)],
        out_specs=[cur(Q_DIM), cur(KV_DIM), cur(KV_DIM), _full((1, N_Q_HEADS))],
        out_shape=[jax.ShapeDtypeStruct((t, Q_DIM), BF16), jax.ShapeDtypeStruct((t, KV_DIM), F32),
                   jax.ShapeDtypeStruct((t, KV_DIM), F32), jax.ShapeDtypeStruct((1, N_Q_HEADS), F32)],
        scratch_shapes=[pltpu.VMEM((N_KV_HEADS, BLOCK, HEAD_DIM), F32), pltpu.VMEM((N_KV_HEADS, BLOCK, HEAD_DIM), F32)],
        compiler_params=_cp(("arbitrary",)),
    )(sinks, q, k, k, v, v, do)


def _ev_dz(dzag, dq, dk, dv, rc, rsa, rsb):
    t = dzag.shape[0]
    tm = min(TM, t)

    def body(dzag_ref, dq_ref, dk_ref, dv_ref, c_ref, sa_ref, sb_ref, dz_ref):
        c, sa, sb = c_ref[...], sa_ref[...], sb_ref[...]
        dz_ref[:, 0:2 * A_CH] = dzag_ref[...]
        q0 = 2 * A_CH
        for j in range(Q_DIM // 128):
            d = dq_ref[:, 128 * j:128 * (j + 1)].astype(F32)
            dz_ref[:, q0 + 128 * j:q0 + 128 * (j + 1)] = _rope_bwd(d, c, sa, sb).astype(BF16)
        k0 = q0 + Q_DIM
        dz_ref[:, k0:k0 + KV_DIM] = _rope_bwd(dk_ref[...], c, sa, sb).astype(BF16)
        dz_ref[:, k0 + KV_DIM:k0 + 2 * KV_DIM] = dv_ref[...].astype(BF16)

    return pl.pallas_call(
        body, name="ev_dz", grid=(t // tm,),
        in_specs=[_rows(tm, 2 * A_CH), _rows(tm, Q_DIM), _rows(tm, KV_DIM), _rows(tm, KV_DIM),
                  _rows(tm, 128), _rows(tm, 128), _rows(tm, 128)],
        out_specs=_rows(tm, EVEN_IN),
        out_shape=jax.ShapeDtypeStruct((t, EVEN_IN), BF16),
        compiler_params=_cp(("arbitrary",)),
    )(dzag, dq, dk, dv, rc, rsa, rsb)


def _prep_ev(gat):
    p = {}
    p["ev_w_in"] = gat["ev_w_in"][:, 0].transpose(1, 0, 2).reshape(D_MODEL, EVEN_IN)
    p["ev_w_out"] = gat["ev_w_out"].reshape(A_CH + Q_DIM, D_MODEL)
    return p


def _prep_rest(gat):
    p = {}
    g = gat["od_w_in"][:, 0]
    p["od_w_in"], p["od_w_in_t"] = g, g.transpose(0, 2, 1).reshape(3 * SC_DIM, D_MODEL)
    p["od_w_out"] = gat["od_w_out"].reshape(SC_DIM, D_MODEL)
    p["ffn_w_up"] = gat["ffn_w_up"].reshape(2, N_DEV // 2, 2, D_MODEL, FF_N)
    p["ffn_w_down"] = [gat["ffn_w_down"][:, i].reshape(D_FF, D_MODEL) for i in range(2)]
    return p


def _local_step(x, positions, target, p, rest_weights, s, token, grads_ready):
    row = lambda a, tok=None: a.reshape(1, -1) if tok is None else a.reshape(1, -1) + tok
    nc = N_DEV // 2
    rc, rsa, rsb = _rope_tables(positions)
    conv31 = jnp.pad(s["ev_a_conv_w"][0], ((0, 1), (0, 0)))
    cw_ffn = [s["ffn_conv_w"][i].reshape(3, 2, nc, FF_N).transpose(1, 2, 0, 3) for i in range(2)]
    sinks = s["ev_sinks"][0]
    big, g = {}, {}

    h0, zag, q, k, v = _ev_in(x, row(s["mix_norm_pre"][0], token), p["ev_w_in"], rc, rsa, rsb)
    cv, a = _conf_fwd(zag, conv31, s["ev_a_conv_b"], s["ev_a_ln_g"], s["ev_a_ln_b"])
    o = _attn_fwd(q, k, v, sinks)
    wo = p["ev_w_out"]
    m0, x1 = _out_post([a, o], [wo[:A_CH], wo[A_CH:]], x, row(s["mix_norm_post"][0]))
    p = {**p, **rest_weights(m0)}
    h1, up0, u0, f0, x2 = _ffn_fwd(x1, row(s["ffn_norm_pre"][0]), p["ffn_w_up"], 0, cw_ffn[0], p["ffn_w_down"][0],
                                   row(s["ffn_norm_post"][0]))
    h2, z, cv1, y, m1, x3 = _od_fwd(x2, row(s["mix_norm_pre"][1]), p["od_w_in"], s["od_conv_w"][0], p["od_w_out"],
                                    row(s["mix_norm_post"][1]))
    h3, up1, u1, f1, x4 = _ffn_fwd(x3, row(s["ffn_norm_pre"][1]), p["ffn_w_up"], 1, cw_ffn[1], p["ffn_w_down"][1],
                                   row(s["ffn_norm_post"][1]))

    def ffn_back(i, f, dxo, up, u, h, x_in, bufs, tok=None, tgt=None):
        df, act, dup, dx_in, dgpost, dgpre, dcw, *loss = _ffn_bwd(
            f, dxo, row(s["ffn_norm_post"][i], tok), x_in, row(s["ffn_norm_pre"][i]), up, u, cw_ffn[i],
            p["ffn_w_down"][i], p["ffn_w_up"], i, tgt)
        bufs = (_dw_up(h, dup.reshape(N_DEV, -1, FF_N), i, bufs[0]), _dw_down(act, df, i, bufs[1]))
        return dx_in, dgpost, dgpre, dcw[:, :, 0:3].transpose(2, 0, 1, 3).reshape(3, 2 * D_FF), bufs, loss

    dx, dgfpost1, dgfpre1, dcw1, bufs, (lpart,) = ffn_back(1, f1, x4, up1, u1, h3, x3, (None, None), None, target)

    dm1, dz, dx, dgpost1, dgpre1, dcw_od = _od_bwd(m1, dx, row(s["mix_norm_post"][1]), x2, row(s["mix_norm_pre"][1]), z,
                                                   cv1, s["od_conv_w"][0], p["od_w_out"], p["od_w_in_t"])
    big["od_w_out"] = _dw2d(y, dm1, SC_DIM, D_MODEL).reshape(N_DEV, -1, D_MODEL)
    big["od_w_in"], big["od_w_in:bf16"] = _dw_cols(h2, dz, 3 * SC_DIM // N_DEV)
    g["od_conv_w"] = dcw_od[None, 0:3]
    tok = grads_ready(["od_w_in", "od_w_out"], big)

    dx, dgfpost0, dgfpre0, dcw0, bufs, _ = ffn_back(0, f0, dx, up0, u0, h1, x1, bufs, tok)
    (big["ffn_w_up"], big["ffn_w_up:bf16"]), (big["ffn_w_down"], big["ffn_w_down:bf16"]) = bufs
    tok = grads_ready(["ffn_w_up", "ffn_w_down"], big)

    dm0, da, do, dgpost0 = _ev_bwd1(m0, dx, row(s["mix_norm_post"][0], tok), p["ev_w_out"])
    big["ev_w_out"] = jnp.concatenate([_dw2d(a, dm0, A_CH, D_MODEL), _dw2d(o, dm0, Q_DIM, D_MODEL)],
                                      axis=0).reshape(N_DEV, -1, D_MODEL)
    tok = grads_ready(["ev_w_out"], big)
    dzag, dcw31, dvec = _conf_bwd(da, cv, zag, conv31, s["ev_a_ln_g"] + tok, s["ev_a_ln_b"])
    dq, dk, dv, dsinks = _attn_bwd(q, k, v, do, sinks)
    dz0 = _ev_dz(dzag, dq, dk, dv, rc, rsa, rsb)
    dw_in = _dw2d(h0, dz0, D_MODEL, EVEN_IN // 2)
    big["ev_w_in"] = dw_in.reshape(D_MODEL, N_DEV, EVEN_IN // N_DEV).transpose(1, 0, 2)
    tok = grads_ready(["ev_w_in"], big)
    dx, dgpre0 = _dz_wt_rms_bwd(dz0, p["ev_w_in"], x, row(s["mix_norm_pre"][0], tok), dx)

    g["mix_norm_pre"] = jnp.concatenate([dgpre0, dgpre1], axis=0)
    g["mix_norm_post"] = jnp.concatenate([dgpost0, dgpost1], axis=0)
    g["ffn_norm_pre"] = jnp.concatenate([dgfpre0, dgfpre1], axis=0)
    g["ffn_norm_post"] = jnp.concatenate([dgfpost0, dgfpost1], axis=0)
    g["ev_a_conv_w"] = dcw31[None, 0:A_CONV]
    g["ev_a_conv_b"], g["ev_a_ln_g"], g["ev_a_ln_b"] = dvec[0:1], dvec[1:2], dvec[2:3]
    g["ev_sinks"] = dsinks
    g["ffn_conv_w"] = jnp.stack([dcw0, dcw1])
    return lpart[0, 0], dx, big, g


MESH = pl.DeviceIdType.MESH


def _all_gather(shards, name):
    nw = len(shards)

    def body(*refs):
        x_refs, out_refs = refs[:nw], refs[nw:2 * nw]
        send_sems, recv_sems, local_sems = refs[2 * nw:]
        x, y, c = lax.axis_index("x"), lax.axis_index("y"), lax.axis_index("c")
        me, sibling = (x, y, c), (x, y, 1 - c)
        chips = [(1 - x, y), (x, 1 - y), (1 - x, 1 - y)]

        def rows(w, px, py, pc):
            m_per = shards[w].shape[0]
            return out_refs[w].at[pl.ds((4 * px + 2 * py + pc) * m_per, m_per), :]

        def copy(w, k, block, to, src=None):
            return pltpu.make_async_remote_copy(
                src_ref=rows(w, *block) if src is None else src, dst_ref=rows(w, *block),
                send_sem=send_sems.at[w, k], recv_sem=recv_sems.at[w, k], device_id=to, device_id_type=MESH)

        mine, first, passed = [], [], []
        for w in range(nw):
            cp = pltpu.make_async_copy(x_refs[w], rows(w, *me), local_sems.at[w])
            cp.start()
            mine.append(cp)
            first.append([copy(w, 0, me, sibling, src=x_refs[w])]
                         + [copy(w, 1 + j, me, (*chip, c), src=x_refs[w]) for j, chip in enumerate(chips)])
            for cp in first[w]:
                cp.start()
        for w in range(nw):
            passed.append([copy(w, 4 + j, (*chip, c), sibling) for j, chip in enumerate(chips)])
            for j, chip in enumerate(chips):
                copy(w, 1 + j, (*chip, c), me).wait_recv()
                passed[w][j].start()
        for w in range(nw):
            copy(w, 0, sibling, me).wait_recv()
            for j, chip in enumerate(chips):
                copy(w, 4 + j, (*chip, 1 - c), me).wait_recv()
            for cp in first[w] + passed[w]:
                cp.wait_send()
            mine[w].wait()

    return pl.pallas_call(
        body, name=name,
        out_shape=[jax.ShapeDtypeStruct((N_DEV * a.shape[0], a.shape[1]), a.dtype) for a in shards],
        in_specs=[_ANY] * nw, out_specs=[_ANY] * nw,
        scratch_shapes=[pltpu.SemaphoreType.DMA((nw, 7)), pltpu.SemaphoreType.DMA((nw, 7)),
                        pltpu.SemaphoreType.DMA((nw,))],
    )(*shards)


_HBM = pl.BlockSpec(memory_space=pltpu.HBM)
_SEM = pl.BlockSpec(memory_space=pltpu.SEMAPHORE)
_EFFECT = pltpu.SideEffectType.DATAFLOW_SIDE_EFFECTING
_RELATIONS = [(dx, dy, dc) for dx in (0, 1) for dy in (0, 1) for dc in (0, 1)][1:]


def _peer(rel):
    x, y, c = lax.axis_index("x"), lax.axis_index("y"), lax.axis_index("c")
    px, py, pc = x ^ rel[0], y ^ rel[1], c ^ rel[2]
    return (px, py, pc), 4 * px + 2 * py + pc, 4 * x + 2 * y + c


def _exchange_copy(k, rel, src_ref, land_ref, send_sems, recv_sems, w, scatter):
    peer, peer_idx, my_idx = _peer(rel)
    src = src_ref.at[peer_idx] if scatter else src_ref
    return pltpu.make_async_remote_copy(
        src_ref=src, dst_ref=land_ref.at[my_idx], send_sem=send_sems.at[_sends(scatter) * w + k],
        recv_sem=recv_sems.at[7 * w + k], device_id=peer, device_id_type=MESH)


def _sends(scatter):
    return 7 if scatter else 8


def _own_copy(src_ref, land_ref, send_sems, w):
    my_idx = _peer(_RELATIONS[0])[2]
    return pltpu.make_async_copy(src_ref, land_ref.at[my_idx], send_sems.at[8 * w + 7])


def _exchange_start(srcs, scatter, name):
    nw = len(srcs)
    lands = [lax.empty((N_DEV,) + (a.shape[1:] if scatter else a.shape), a.dtype) for a in srcs]

    def body(*refs):
        src_refs, land_refs = refs[:nw], refs[nw:2 * nw]
        send_sems, recv_sems = refs[2 * nw], refs[2 * nw + 1]
        token = refs[-1]
        for w in range(nw):
            for k, rel in enumerate(_RELATIONS):
                _exchange_copy(k, rel, src_refs[w], land_refs[w], send_sems, recv_sems, w, scatter).start()
            if not scatter:
                _own_copy(src_refs[w], land_refs[w], send_sems, w).start()
        token[...] = jnp.zeros_like(token)

    hbm = lambda a: pltpu.HBM(a.shape, a.dtype)
    outs = pl.pallas_call(
        body, name=name,
        out_shape=(pltpu.SemaphoreType.DMA((_sends(scatter) * nw,)), pltpu.SemaphoreType.DMA((7 * nw,)),
                   *[hbm(a) for a in srcs],
                   *[hbm(a) for a in lands], jax.ShapeDtypeStruct((8, 128), F32)),
        in_specs=[_HBM] * (2 * nw),
        out_specs=(_SEM, _SEM, *[_HBM] * (2 * nw), pl.BlockSpec(memory_space=pltpu.VMEM)),
        input_output_aliases={i: 2 + i for i in range(2 * nw)},
        compiler_params=pltpu.CompilerParams(has_side_effects=_EFFECT),
    )(*[pltpu.with_memory_space_constraint(a, pltpu.HBM) for a in srcs],
      *[pltpu.with_memory_space_constraint(a, pltpu.HBM) for a in lands])
    return outs[0], outs[1], list(outs[2:2 + nw]), list(outs[2 + nw:2 + 2 * nw]), outs[-1]


def _exchange_wait(started, scatter, after, name):
    send_sems, recv_sems, srcs, lands, _ = started
    nw = len(srcs)

    def body(*refs):
        src_refs, land_refs = refs[:nw], refs[nw:2 * nw]
        send_s, recv_s = refs[2 * nw], refs[2 * nw + 1]
        for w in range(nw):
            for k, rel in enumerate(_RELATIONS):
                cp = _exchange_copy(k, rel, src_refs[w], land_refs[w], send_s, recv_s, w, scatter)
                cp.wait_send()
                _, peer_idx, _ = _peer(rel)
                pltpu.make_async_remote_copy(
                    src_ref=src_refs[w].at[peer_idx] if scatter else src_refs[w], dst_ref=land_refs[w].at[peer_idx],
                    send_sem=send_s.at[_sends(scatter) * w + k], recv_sem=recv_s.at[7 * w + k],
                    device_id=_peer(rel)[0], device_id_type=MESH).wait_recv()
            if not scatter:
                _own_copy(src_refs[w], land_refs[w], send_s, w).wait()

    hbm = lambda a: pltpu.HBM(a.shape, a.dtype)
    outs = pl.pallas_call(
        body, name=name, out_shape=tuple(hbm(a) for a in srcs + lands),
        in_specs=[_HBM] * (2 * nw) + [_SEM, _SEM, _ANY], out_specs=tuple([_HBM] * (2 * nw)),
        input_output_aliases={i: i for i in range(2 * nw)},
        compiler_params=pltpu.CompilerParams(has_side_effects=_EFFECT),
    )(*srcs, *lands, send_sems, recv_sems, after)
    return list(outs[nw:])


def _to_bf16(a):
    _, r, l = a.shape
    tr = _row_tile(r, 512)

    def body(a_ref, o_ref):
        o_ref[...] = a_ref[...].astype(BF16)

    spec = pl.BlockSpec((1, tr, l), lambda j, i: (j, i, 0))
    return pl.pallas_call(
        body, name="to_bf16", grid=(N_DEV, r // tr), in_specs=[spec], out_specs=spec,
        out_shape=jax.ShapeDtypeStruct(a.shape, BF16), compiler_params=_cp(("arbitrary", "arbitrary")),
    )(a)


def _row_tile(rows, cap):
    best = None
    for d in range(16, min(rows, cap) + 1, 16):
        if rows % d == 0:
            best = d
    return rows if best is None else best


def _adam_math(w, g, m, v):
    bc1 = 1.0 - ADAM_B1 ** ADAM_STEP
    bc2 = 1.0 - ADAM_B2 ** ADAM_STEP
    mn = ADAM_B1 * m + (1.0 - ADAM_B1) * g
    vn = ADAM_B2 * v + (1.0 - ADAM_B2) * (g * g)
    return -ADAM_LR * ((mn / bc1) / (jnp.sqrt(vn / bc2) + ADAM_EPS) + ADAM_WD * w), mn, vn


def _adamw_rs(gp, land, w, m, v, dev):
    _, r, l = gp.shape
    tr = _row_tile(r, 256)

    def body(i_ref, g_ref, b_ref, w_ref, m_ref, v_ref, go_ref, d_ref, mo_ref, vo_ref):
        g = g_ref[0]
        for j in range(N_DEV):
            g = g + jnp.where(i_ref[0] == j, 0.0, b_ref[j].astype(F32))
        go_ref[...] = g
        d_ref[...], mo_ref[...], vo_ref[...] = _adam_math(w_ref[...], g, m_ref[...], v_ref[...])

    spec = pl.BlockSpec((tr, l), lambda i, s: (i, 0))
    return pl.pallas_call(
        body, name="adamw_rs", out_shape=[jax.ShapeDtypeStruct((r, l), F32)] * 4,
        grid_spec=pltpu.PrefetchScalarGridSpec(
            num_scalar_prefetch=1, grid=(r // tr,),
            in_specs=[pl.BlockSpec((1, tr, l), lambda i, s: (s[0], i, 0)),
                      pl.BlockSpec((N_DEV, tr, l), lambda i, s: (0, i, 0)), spec, spec, spec],
            out_specs=[spec] * 4),
        compiler_params=_cp(("arbitrary",)),
    )(dev, gp, land, w, m, v)


def _sum_blocks(a, nblk):
    m = a.shape[0] // nblk
    n = a.shape[1]

    def body(a_ref, o_ref):
        acc = a_ref[0]
        for j in range(1, nblk):
            acc = acc + a_ref[j]
        o_ref[...] = acc

    return pl.pallas_call(
        body, name="sum_blocks", out_shape=jax.ShapeDtypeStruct((m, n), a.dtype),
        in_specs=[_full((nblk, m, n))], out_specs=_full((m, n)),
    )(a.reshape(nblk, m, n))


def _adamw(w, g, m, v):
    rows, c = w.shape

    def body(w_ref, g_ref, m_ref, v_ref, d_ref, mo_ref, vo_ref):
        d_ref[...], mo_ref[...], vo_ref[...] = _adam_math(w_ref[...], g_ref[...], m_ref[...], v_ref[...])

    return pl.pallas_call(
        body, name="adamw", in_specs=[_full((rows, c))] * 4, out_specs=[_full((rows, c))] * 3,
        out_shape=[jax.ShapeDtypeStruct((rows, c), F32)] * 3,
    )(w, g, m, v)


WEIGHTS = ["mix_norm_pre", "mix_norm_post", "ffn_norm_pre", "ffn_norm_post", "ev_w_in", "ev_a_conv_w", "ev_a_conv_b",
           "ev_a_ln_g", "ev_a_ln_b", "ev_sinks", "ev_w_out", "od_w_in", "od_conv_w", "od_w_out", "ffn_w_up",
           "ffn_conv_w", "ffn_w_down"]
BIG = ["ev_w_in", "ev_w_out", "od_w_in", "od_w_out", "ffn_w_up", "ffn_w_down"]
SMALL_REPL = ["mix_norm_pre", "mix_norm_post", "ffn_norm_pre", "ffn_norm_post", "ev_a_conv_b", "ev_a_ln_g",
              "ev_a_ln_b", "ev_sinks"]
SMALL_SHARDED = ["ev_a_conv_w", "od_conv_w", "ffn_conv_w"]


def _pack(arrs, rows):
    flat = jnp.concatenate([a.reshape(-1) for a in arrs])
    return jnp.pad(flat, (0, rows * LANES - flat.shape[0])).reshape(rows, LANES)


def _unpack(packed, shapes):
    flat, out, off = packed.reshape(-1), [], 0
    for s in shapes:
        n = 1
        for d in s:
            n *= d
        out.append(flat[off:off + n].reshape(s))
        off += n
    return out


def kernel(x, positions, mix_norm_pre, mix_norm_post, ffn_norm_pre, ffn_norm_post, ev_w_in, ev_a_conv_w, ev_a_conv_b, ev_a_ln_g, ev_a_ln_b, ev_sinks, ev_w_out, od_w_in, od_conv_w, od_w_out, ffn_w_up, ffn_conv_w, ffn_w_down, loss_target, m_mix_norm_pre, m_mix_norm_post, m_ffn_norm_pre, m_ffn_norm_post, m_ev_w_in, m_ev_a_conv_w, m_ev_a_conv_b, m_ev_a_ln_g, m_ev_a_ln_b, m_ev_sinks, m_ev_w_out, m_od_w_in, m_od_conv_w, m_od_w_out, m_ffn_w_up, m_ffn_conv_w, m_ffn_w_down, v_mix_norm_pre, v_mix_norm_post, v_ffn_norm_pre, v_ffn_norm_post, v_ev_w_in, v_ev_a_conv_w, v_ev_a_conv_b, v_ev_a_ln_g, v_ev_a_ln_b, v_ev_sinks, v_ev_w_out, v_od_w_in, v_od_conv_w, v_od_w_out, v_ffn_w_up, v_ffn_conv_w, v_ffn_w_down):
    w = dict(zip(WEIGHTS, (mix_norm_pre, mix_norm_post, ffn_norm_pre, ffn_norm_post, ev_w_in, ev_a_conv_w, ev_a_conv_b,
                           ev_a_ln_g, ev_a_ln_b, ev_sinks, ev_w_out, od_w_in, od_conv_w, od_w_out, ffn_w_up, ffn_conv_w,
                           ffn_w_down)))
    mom = dict(zip(WEIGHTS, (m_mix_norm_pre, m_mix_norm_post, m_ffn_norm_pre, m_ffn_norm_post, m_ev_w_in, m_ev_a_conv_w,
                             m_ev_a_conv_b, m_ev_a_ln_g, m_ev_a_ln_b, m_ev_sinks, m_ev_w_out, m_od_w_in, m_od_conv_w,
                             m_od_w_out, m_ffn_w_up, m_ffn_conv_w, m_ffn_w_down)))
    var = dict(zip(WEIGHTS, (v_mix_norm_pre, v_mix_norm_post, v_ffn_norm_pre, v_ffn_norm_post, v_ev_w_in, v_ev_a_conv_w,
                             v_ev_a_conv_b, v_ev_a_ln_g, v_ev_a_ln_b, v_ev_sinks, v_ev_w_out, v_od_w_in, v_od_conv_w,
                             v_od_w_out, v_ffn_w_up, v_ffn_conv_w, v_ffn_w_down)))
    ix, iy, ic = lax.axis_index("x"), lax.axis_index("y"), lax.axis_index("c")
    dev = 4 * ix + 2 * iy + ic
    two = lambda a: a.reshape(-1, a.shape[-1])

    dev1 = jnp.reshape(dev, (1,)).astype(jnp.int32)
    shard = {n: two(w[n].astype(BF16)) for n in BIG}
    gathered = lambda n, a: a.reshape((N_DEV,) + w[n].shape)
    ev_names = [n for n in BIG if n.startswith("ev_")]
    ev_gat = _all_gather([shard[n] for n in ev_names] + [_pack([w[n] for n in SMALL_SHARDED], 8)], "gather_ev")
    p = _prep_ev({n: gathered(n, a) for n, a in zip(ev_names, ev_gat)})
    rest_names = [n for n in BIG if not n.startswith("ev_")]
    first = shard[rest_names[0]] + (ev_gat[0][0:1, 0:1] * 0).astype(BF16)
    started = _exchange_start([first] + [shard[n] for n in rest_names[1:]], False, "gather_start")

    def rest_weights(after):
        lands = _exchange_wait(started, False, after, "gather_wait")
        return _prep_rest({n: gathered(n, a) for n, a in zip(rest_names, lands)})

    small = {n: w[n] for n in SMALL_REPL}
    small_shapes = [w[n].shape for n in SMALL_SHARDED]
    conv_gat = ev_gat[len(ev_names)].reshape(N_DEV, 8, LANES)
    per_dev = [_unpack(conv_gat[d], small_shapes) for d in range(N_DEV)]
    for k, n in enumerate(SMALL_SHARDED):
        small[n] = jnp.concatenate([per_dev[d][k] for d in range(N_DEV)], axis=-1)

    exchanges = []

    def grads_ready(names, big):
        blocks = lambda a, n: a.reshape(N_DEV, -1, w[n].shape[-1])
        bufs = [blocks(big[n], n) for n in names]
        payload = [blocks(big[n + ":bf16"], n) if n + ":bf16" in big else _to_bf16(b) for n, b in zip(names, bufs)]
        st = _exchange_start(payload, True, "grads_start_" + names[0])
        exchanges.append((names, bufs, st))
        return st[-1][0, 0]

    lpart, grad_x, big, g = _local_step(x[0], positions[0], loss_target[0], p, rest_weights, small, started[-1][0, 0],
                                        grads_ready)
    loss = lax.psum(lpart, ("x", "y", "c"))

    grads, delta, new_m, new_v = {}, {}, {}, {}
    for names, bufs, st in exchanges:
        lands = _exchange_wait(st, True, grad_x, "grads_wait_" + names[0])
        for n, b, land in zip(names, bufs, lands):
            outs = _adamw_rs(b, land, two(w[n]), two(mom[n]), two(var[n]), dev1)
            grads[n], delta[n], new_m[n], new_v[n] = (a.reshape(w[n].shape) for a in outs)

    small_names = SMALL_REPL + SMALL_SHARDED
    s_all = _sum_blocks(_all_gather([_pack([g[n] for n in small_names], 64)], "gather_small_grads")[0], N_DEV)
    for n, a in zip(small_names, _unpack(s_all, [small[n].shape for n in small_names])):
        if n in SMALL_SHARDED:
            width = w[n].shape[-1]
            a = lax.dynamic_slice_in_dim(a, dev * width, width, axis=a.ndim - 1)
        grads[n] = a
    pk = lambda dct: _pack([dct[n] for n in small_names], 16)
    outs = _adamw(pk(w), pk(grads), pk(mom), pk(var))
    for dst, packed in zip((delta, new_m, new_v), outs):
        for n, a in zip(small_names, _unpack(packed, [w[n].shape for n in small_names])):
            dst[n] = a

    return (loss, grad_x[None], *[grads[n] for n in WEIGHTS], *[delta[n] for n in WEIGHTS],
            *[new_m[n] for n in WEIGHTS], *[new_v[n] for n in WEIGHTS])
```

```python
import jax
import jax.numpy as jnp
from jax import lax
from jax.experimental import pallas as pl
from jax.experimental.pallas import tpu as pltpu

F32, BF16 = jnp.float32, jnp.bfloat16

D_MODEL = 1024
A_CH = 512
A_CONV = 31
Q_DIM = 512
KV_DIM = 128
HEAD_DIM = 64
N_Q_HEADS = 8
N_KV_HEADS = 2
GROUP = 4
BLOCK = 128
EVEN_IN = 1792
SC_DIM = 1024
D_FF = 2816
ROPE_THETA = 500000.0
ROPE_DIM = 16
RMS_EPS = 1e-6
LN_EPS = 1e-5
SCALE = HEAD_DIM ** -0.5
NEG = -1e30

ADAM_LR, ADAM_B1, ADAM_B2, ADAM_EPS, ADAM_WD, ADAM_STEP = 0.001, 0.9, 0.999, 1e-08, 0.01, 10

N_DEV = 8
FF_N = 2 * D_FF // N_DEV
LANES = 1024
HALO3 = 8
HALO31 = 32
VMEM_LIMIT = 56 * 1024 * 1024

TM = 512
TM_BWD = 256
TK_DW = 4096
ATT_NB = 4
SUB = 128

_ANY = pl.BlockSpec(memory_space=pl.ANY)
_CONTRACT_LAST = (((1,), (1,)), ((), ()))
_CONTRACT_FIRST = (((0,), (0,)), ((), ()))


def _cp(sem, vmem=VMEM_LIMIT):
    return pltpu.CompilerParams(dimension_semantics=sem, vmem_limit_bytes=vmem)


def _full(shape):
    n = len(shape)
    return pl.BlockSpec(shape, lambda *_: (0,) * n)


def _rows(tm, n):
    return pl.BlockSpec((tm, n), lambda i, *_: (i, 0))


def _sigmoid(x):
    return 0.5 * jnp.tanh(0.5 * x) + 0.5


def _rsqrt_mean(x):
    return lax.rsqrt(jnp.mean(x * x, axis=-1, keepdims=True) + RMS_EPS)


def _rms_bwd(x, g, dy):
    r = _rsqrt_mean(x)
    xh = x * r
    dxh = dy * g
    dx = r * (dxh - xh * jnp.mean(dxh * xh, axis=-1, keepdims=True))
    return dx, jnp.sum(dy * xh, axis=0, keepdims=True)


def _acc_out(ref, first, val):
    @pl.when(first)
    def _():
        ref[...] = val

    @pl.when(jnp.logical_not(first))
    def _():
        ref[...] += val


def _rope_tables(positions):
    half = ROPE_DIM // 2
    inv_freq = ROPE_THETA ** (-(jnp.arange(half, dtype=F32) * 2.0 / ROPE_DIM))
    ang = positions.astype(F32)[:, None] * inv_freq
    cos, sin = jnp.cos(ang), jnp.sin(ang)
    t = positions.shape[0]
    one, zero = jnp.ones((t, HEAD_DIM - ROPE_DIM), F32), jnp.zeros((t, HEAD_DIM - ROPE_DIM), F32)
    z8 = jnp.zeros((t, half), F32)
    c = jnp.concatenate([cos, cos, one], axis=1)
    sa = jnp.concatenate([z8, sin, zero], axis=1)
    sb = jnp.concatenate([-sin, z8, zero], axis=1)
    return tuple(jnp.tile(a, (1, 2)) for a in (c, sa, sb))


def _rope(t, c, sa, sb):
    return t * c + pltpu.roll(t, 8, 1) * sa + pltpu.roll(t, 120, 1) * sb


def _rope_bwd(d, c, sa, sb):
    return d * c + pltpu.roll(d * sa, 120, 1) + pltpu.roll(d * sb, 8, 1)


def _ev_in(x, gpre, w_in, rc, rsa, rsb):
    t = x.shape[0]
    tm = min(TM, t)

    def body(x_ref, g_ref, w_ref, c_ref, sa_ref, sb_ref, h_ref, zag_ref, q_ref, k_ref, v_ref):
        xv = x_ref[...]
        h = (xv * _rsqrt_mean(xv) * g_ref[...]).astype(BF16)
        h_ref[...] = h
        z = jnp.dot(h, w_ref[...], preferred_element_type=F32)
        zag_ref[...] = z[:, :2 * A_CH].astype(BF16)
        c, sa, sb = c_ref[...], sa_ref[...], sb_ref[...]
        q0 = 2 * A_CH
        for j in range(Q_DIM // 128):
            q_ref[:, 128 * j:128 * (j + 1)] = _rope(z[:, q0 + 128 * j:q0 + 128 * (j + 1)], c, sa, sb).astype(BF16)
        k0 = q0 + Q_DIM
        k_ref[...] = _rope(z[:, k0:k0 + KV_DIM], c, sa, sb).astype(BF16)
        v_ref[...] = z[:, k0 + KV_DIM:k0 + 2 * KV_DIM].astype(BF16)

    return pl.pallas_call(
        body, name="ev_in", grid=(t // tm,),
        in_specs=[_rows(tm, D_MODEL), _full((1, D_MODEL)), _full((D_MODEL, EVEN_IN)),
                  _rows(tm, 128), _rows(tm, 128), _rows(tm, 128)],
        out_specs=[_rows(tm, D_MODEL), _rows(tm, 2 * A_CH), _rows(tm, Q_DIM), _rows(tm, KV_DIM), _rows(tm, KV_DIM)],
        out_shape=[jax.ShapeDtypeStruct((t, D_MODEL), BF16), jax.ShapeDtypeStruct((t, 2 * A_CH), BF16),
                   jax.ShapeDtypeStruct((t, Q_DIM), BF16), jax.ShapeDtypeStruct((t, KV_DIM), BF16),
                   jax.ShapeDtypeStruct((t, KV_DIM), BF16)],
        compiler_params=_cp(("arbitrary",)),
    )(x, gpre, w_in, rc, rsa, rsb)


def _glu(zag):
    z = zag.astype(F32)
    return z[:, :A_CH] * _sigmoid(z[:, A_CH:])


def _tap_copies(ext, cbuf, first_row, rows):
    for b in range(1, 8):
        s = first_row(b)
        cbuf[b - 1] = ext[s:s + rows, :]


def _conf_fwd(zag, conv_w, conv_b, ln_g, ln_b):
    t = zag.shape[0]
    tm = min(TM_BWD, t)
    rows = tm + HALO31 - 8

    def body(z_ref, w_ref, b_ref, g_ref, lb_ref, c_ref, a_ref, ext, cbuf):
        i = pl.program_id(0)

        @pl.when(i == 0)
        def _():
            ext[0:HALO31, :] = jnp.zeros((HALO31, A_CH), F32)

        ext[HALO31:HALO31 + tm, :] = _glu(z_ref[...])
        _tap_copies(ext, cbuf, lambda b: 8 - b, rows)
        for rs in range(0, tm, SUB):
            for cs in range(0, A_CH, 128):
                acc = jnp.zeros((SUB, 128), F32)
                for k in range(A_CONV):
                    lag_a, lag_b = divmod(k, 8)
                    r0 = HALO31 - 8 - 8 * lag_a + rs
                    src = (ext[r0 + 8:r0 + 8 + SUB, cs:cs + 128] if lag_b == 0
                           else cbuf[lag_b - 1, r0:r0 + SUB, cs:cs + 128])
                    acc = acc + w_ref[A_CONV - 1 - k:A_CONV - k, cs:cs + 128] * src
                c_ref[rs:rs + SUB, cs:cs + 128] = acc
        ext[0:HALO31, :] = ext[tm:tm + HALO31, :]
        cv = c_ref[...] + b_ref[...]
        c_ref[...] = cv
        mu = jnp.mean(cv, axis=-1, keepdims=True)
        xc = cv - mu
        ln = xc * lax.rsqrt(jnp.mean(xc * xc, axis=-1, keepdims=True) + LN_EPS) * g_ref[...] + lb_ref[...]
        a_ref[...] = (ln * _sigmoid(ln)).astype(BF16)

    return pl.pallas_call(
        body, name="conf_fwd", grid=(t // tm,),
        in_specs=[_rows(tm, 2 * A_CH), _full((32, A_CH)), _full((1, A_CH)), _full((1, A_CH)), _full((1, A_CH))],
        out_specs=[_rows(tm, A_CH), _rows(tm, A_CH)],
        out_shape=[jax.ShapeDtypeStruct((t, A_CH), F32), jax.ShapeDtypeStruct((t, A_CH), BF16)],
        scratch_shapes=[pltpu.VMEM((HALO31 + tm, A_CH), F32), pltpu.VMEM((7, rows, A_CH), F32)],
        compiler_params=_cp(("arbitrary",)),
    )(zag, conv_w, conv_b, ln_g, ln_b)


def _attn_mask(first_block):
    row = lax.broadcasted_iota(jnp.int32, (GROUP * BLOCK, 2 * BLOCK), 0) & (BLOCK - 1)
    col = lax.broadcasted_iota(jnp.int32, (GROUP * BLOCK, 2 * BLOCK), 1)
    diff = row + BLOCK - col
    return (diff >= 0) & (diff < BLOCK) & ((col >= BLOCK) | jnp.logical_not(first_block))


def _sink_rows(s_ref, h):
    grp = lax.broadcasted_iota(jnp.int32, (GROUP * BLOCK, 1), 0) >> 7
    out = jnp.full((GROUP * BLOCK, 1), s_ref[GROUP * h], F32)
    for g in range(1, GROUP):
        out = jnp.where(grp == g, s_ref[GROUP * h + g], out)
    return out


def _attn_probs(q4, k2, mask, sink):
    s = lax.dot_general(q4, k2, _CONTRACT_LAST, preferred_element_type=F32) * SCALE
    s = jnp.where(mask, s, NEG)
    m = jnp.maximum(jnp.max(s, axis=-1, keepdims=True), sink)
    p = jnp.exp(s - m)
    es = jnp.exp(sink - m)
    inv = 1.0 / (jnp.sum(p, axis=-1, keepdims=True) + es)
    return p * inv, es * inv


def _q_heads(q, h):
    return jnp.concatenate([q[:, HEAD_DIM * (GROUP * h + g):HEAD_DIM * (GROUP * h + g + 1)] for g in range(GROUP)],
                           axis=0)


def _kv_head(prev, cur, h):
    return jnp.concatenate([prev[:, HEAD_DIM * h:HEAD_DIM * (h + 1)], cur[:, HEAD_DIM * h:HEAD_DIM * (h + 1)]], axis=0)


def _attn_fwd(q, k, v, sinks):
    t = q.shape[0]
    nb = min(ATT_NB, t // BLOCK)
    rows = nb * BLOCK

    def body(s_ref, q_ref, kc_ref, kp_ref, vc_ref, vp_ref, o_ref):
        first = pl.program_id(0) == 0
        for b in range(nb):
            lo = BLOCK * b
            mask = _attn_mask(first) if b == 0 else _attn_mask(False)
            qv, kc, vc = q_ref[lo:lo + BLOCK, :], kc_ref[lo:lo + BLOCK, :], vc_ref[lo:lo + BLOCK, :]
            kp = kp_ref[...] if b == 0 else kc_ref[lo - BLOCK:lo, :]
            vp = vp_ref[...] if b == 0 else vc_ref[lo - BLOCK:lo, :]
            for h in range(N_KV_HEADS):
                pn, _ = _attn_probs(_q_heads(qv, h), _kv_head(kp, kc, h), mask, _sink_rows(s_ref, h))
                o4 = jnp.dot(pn.astype(BF16), _kv_head(vp, vc, h), preferred_element_type=F32).astype(BF16)
                for g in range(GROUP):
                    c0 = HEAD_DIM * (GROUP * h + g)
                    o_ref[lo:lo + BLOCK, c0:c0 + HEAD_DIM] = o4[BLOCK * g:BLOCK * (g + 1), :]

    cur = lambda n: pl.BlockSpec((rows, n), lambda i: (i, 0))
    prev = lambda n: pl.BlockSpec((BLOCK, n), lambda i: (jnp.maximum(i * nb - 1, 0), 0))
    return pl.pallas_call(
        body, name="attn_fwd", grid=(t // rows,),
        in_specs=[pl.BlockSpec(memory_space=pltpu.SMEM), cur(Q_DIM), cur(KV_DIM), prev(KV_DIM), cur(KV_DIM),
                  prev(KV_DIM)],
        out_specs=cur(Q_DIM),
        out_shape=jax.ShapeDtypeStruct((t, Q_DIM), BF16),
        compiler_params=_cp(("arbitrary",)),
    )(sinks, q, k, k, v, v)


def _out_post(lhs, ws, x_in, gpost):
    t = x_in.shape[0]
    tm = min(TM, t)
    n = len(lhs)

    def body(*refs):
        x_ref, g_ref, m_ref, xo_ref = refs[2 * n:]
        m = jnp.dot(refs[0][...], refs[n][...], preferred_element_type=F32)
        for j in range(1, n):
            m = m + jnp.dot(refs[j][...], refs[n + j][...], preferred_element_type=F32)
        m_ref[...] = m
        xo_ref[...] = x_ref[...] + m * _rsqrt_mean(m) * g_ref[...]

    return pl.pallas_call(
        body, name="out_post", grid=(t // tm,),
        in_specs=[_rows(tm, a.shape[1]) for a in lhs] + [_full(w.shape) for w in ws]
                 + [_rows(tm, D_MODEL), _full((1, D_MODEL))],
        out_specs=[_rows(tm, D_MODEL), _rows(tm, D_MODEL)],
        out_shape=[jax.ShapeDtypeStruct((t, D_MODEL), F32)] * 2,
        compiler_params=_cp(("arbitrary",)),
    )(*lhs, *ws, x_in, gpost)


def _conv3(w_ref, ext, tm):
    s = HALO3 - 2
    return (w_ref[0:1, :] * ext[s:s + tm, :] + w_ref[1:2, :] * ext[s + 1:s + 1 + tm, :]
            + w_ref[2:3, :] * ext[s + 2:s + 2 + tm, :])


def _ffn_fwd(x1, gpre, wup, layer, cw, wd, gpost):
    t = x1.shape[0]
    tm = min(TM, t)
    nc, n = wup.shape[1], wup.shape[4]

    def body(x_ref, gpre_ref, wup_ref, cw_ref, wd_ref, gpost_ref, h_ref, up_ref, u_ref, f_ref, xo_ref, h_s, acc, ext, hal):
        i, c = pl.program_id(0), pl.program_id(1)

        @pl.when(c == 0)
        def _():
            xv = x_ref[...]
            h = (xv * _rsqrt_mean(xv) * gpre_ref[...]).astype(BF16)
            h_s[...] = h
            h_ref[...] = h

        @pl.when(i == 0)
        def _():
            hal[c] = jnp.zeros((2, HALO3, n), F32)

        u = []
        for gv in range(2):
            up = jnp.dot(h_s[...], wup_ref[gv, 0, 0], preferred_element_type=F32)
            up_ref[gv, 0] = up.astype(BF16)
            ext[gv, 0:HALO3, :] = hal[c, gv]
            ext[gv, HALO3:HALO3 + tm, :] = up
            hal[c, gv] = ext[gv, tm:tm + HALO3, :]
            s = HALO3 - 2
            u.append(cw_ref[gv, 0, 0:1, :] * ext[gv, s:s + tm, :] + cw_ref[gv, 0, 1:2, :] * ext[gv, s + 1:s + 1 + tm, :]
                     + cw_ref[gv, 0, 2:3, :] * up)
            u_ref[gv, 0] = u[gv].astype(BF16)
        act = (u[0] * _sigmoid(u[0]) * u[1]).astype(BF16)
        part = jnp.dot(act, wd_ref[...], preferred_element_type=F32)

        @pl.when(c == 0)
        def _():
            acc[...] = part

        @pl.when(jnp.logical_and(c > 0, c < nc - 1))
        def _():
            acc[...] += part

        @pl.when(c == nc - 1)
        def _():
            f = acc[...] + part
            f_ref[...] = f
            xo_ref[...] = x_ref[...] + f * _rsqrt_mean(f) * gpost_ref[...]

    row = lambda w: pl.BlockSpec((tm, w), lambda i, c: (i, 0))
    one = _full((1, D_MODEL))
    return pl.pallas_call(
        body, name="ffn_fwd", grid=(t // tm, nc),
        in_specs=[row(D_MODEL), one, pl.BlockSpec((2, 1, 1, D_MODEL, n), lambda i, c: (0, c, layer, 0, 0)),
                  pl.BlockSpec((2, 1, 3, n), lambda i, c: (0, c, 0, 0)), pl.BlockSpec((n, D_MODEL), lambda i, c: (c, 0)),
                  one],
        out_specs=[row(D_MODEL), pl.BlockSpec((2, 1, tm, n), lambda i, c: (0, c, i, 0)),
                   pl.BlockSpec((2, 1, tm, n), lambda i, c: (0, c, i, 0)), row(D_MODEL), row(D_MODEL)],
        out_shape=[jax.ShapeDtypeStruct((t, D_MODEL), BF16), jax.ShapeDtypeStruct((2, nc, t, n), BF16),
                   jax.ShapeDtypeStruct((2, nc, t, n), BF16), jax.ShapeDtypeStruct((t, D_MODEL), F32),
                   jax.ShapeDtypeStruct((t, D_MODEL), F32)],
        scratch_shapes=[pltpu.VMEM((tm, D_MODEL), BF16), pltpu.VMEM((tm, D_MODEL), F32),
                        pltpu.VMEM((2, HALO3 + tm, n), F32), pltpu.VMEM((nc, 2, HALO3, n), F32)],
        compiler_params=_cp(("arbitrary", "arbitrary")),
    )(x1, gpre, wup, cw, wd, gpost)


def _od_fwd(x_in, gpre, w_in, cw, w_out, gpost):
    t = x_in.shape[0]
    tm = min(TM, t)
    ns, _, n = w_in.shape

    def body(x_ref, gpre_ref, w_ref, cw_ref, wo_ref, gpost_ref, h_ref, z_ref, cv_ref, y_ref, m_ref, xo_ref, z_s, ext):
        i = pl.program_id(0)
        xv = x_ref[...]
        h = (xv * _rsqrt_mean(xv) * gpre_ref[...]).astype(BF16)
        h_ref[...] = h
        for j in range(ns):
            z_s[:, n * j:n * (j + 1)] = jnp.dot(h, w_ref[j], preferred_element_type=F32)
        z_ref[...] = z_s[...].astype(BF16)

        @pl.when(i == 0)
        def _():
            ext[0:HALO3, :] = jnp.zeros((HALO3, SC_DIM), F32)

        ext[HALO3:HALO3 + tm, :] = z_s[:, SC_DIM:2 * SC_DIM] * z_s[:, 2 * SC_DIM:]
        cv = _conv3(cw_ref, ext, tm)
        cv_ref[...] = cv.astype(BF16)
        y = (z_s[:, :SC_DIM] * cv).astype(BF16)
        ext[0:HALO3, :] = ext[tm:tm + HALO3, :]
        y_ref[...] = y
        m = jnp.dot(y, wo_ref[...], preferred_element_type=F32)
        m_ref[...] = m
        xo_ref[...] = xv + m * _rsqrt_mean(m) * gpost_ref[...]

    return pl.pallas_call(
        body, name="od_fwd", grid=(t // tm,),
        in_specs=[_rows(tm, D_MODEL), _full((1, D_MODEL)), _full((ns, D_MODEL, n)), _full((3, SC_DIM)),
                  _full((SC_DIM, D_MODEL)), _full((1, D_MODEL))],
        out_specs=[_rows(tm, D_MODEL), _rows(tm, 3 * SC_DIM), _rows(tm, SC_DIM), _rows(tm, SC_DIM), _rows(tm, D_MODEL),
                   _rows(tm, D_MODEL)],
        out_shape=[jax.ShapeDtypeStruct((t, D_MODEL), BF16), jax.ShapeDtypeStruct((t, 3 * SC_DIM), BF16),
                   jax.ShapeDtypeStruct((t, SC_DIM), BF16), jax.ShapeDtypeStruct((t, SC_DIM), BF16),
                   jax.ShapeDtypeStruct((t, D_MODEL), F32), jax.ShapeDtypeStruct((t, D_MODEL), F32)],
        scratch_shapes=[pltpu.VMEM((tm, 3 * SC_DIM), F32), pltpu.VMEM((HALO3 + tm, SC_DIM), F32)],
        compiler_params=_cp(("arbitrary",)),
    )(x_in, gpre, w_in, cw, w_out, gpost)


def _dw2d(a, b, bm, bn):
    t, m = a.shape
    n = b.shape[1]
    tk = min(TK_DW, t)

    def body(a_ref, b_ref, o_ref):
        part = lax.dot_general(a_ref[...], b_ref[...], _CONTRACT_FIRST, preferred_element_type=F32)
        _acc_out(o_ref, pl.program_id(2) == 0, part)

    return pl.pallas_call(
        body, name="dw2d", grid=(m // bm, n // bn, t // tk),
        in_specs=[pl.BlockSpec((tk, bm), lambda i, j, k: (k, i)), pl.BlockSpec((tk, bn), lambda i, j, k: (k, j))],
        out_specs=pl.BlockSpec((bm, bn), lambda i, j, k: (i, j)),
        out_shape=jax.ShapeDtypeStruct((m, n), F32),
        compiler_params=_cp(("arbitrary", "arbitrary", "arbitrary")),
    )(a, b)


def _dw_cols(a, b, n_blk):
    t, m = a.shape
    s = b.shape[1] // n_blk
    tk = min(TK_DW, t)
    nk = t // tk

    def body(a_ref, b_ref, o_ref, ob_ref):
        part = lax.dot_general(a_ref[...], b_ref[...], _CONTRACT_FIRST, preferred_element_type=F32)
        _acc_out(o_ref.at[0], pl.program_id(1) == 0, part)

        @pl.when(pl.program_id(1) == nk - 1)
        def _():
            ob_ref[...] = o_ref[...].astype(BF16)

    spec = pl.BlockSpec((1, m, n_blk), lambda j, k: (j, 0, 0))
    return pl.pallas_call(
        body, name="dw_cols", grid=(s, nk),
        in_specs=[pl.BlockSpec((tk, m), lambda j, k: (k, 0)), pl.BlockSpec((tk, n_blk), lambda j, k: (k, j))],
        out_specs=[spec, spec],
        out_shape=[jax.ShapeDtypeStruct((s, m, n_blk), F32), jax.ShapeDtypeStruct((s, m, n_blk), BF16)],
        compiler_params=_cp(("arbitrary", "arbitrary")),
    )(a, b)


def _dw_up(h, dup, layer, buf):
    t, m = h.shape
    s, _, n = dup.shape
    tk = min(TK_DW, t)
    nk = t // tk

    def body(*refs):
        a_ref, b_ref, o_ref, ob_ref = refs[0], refs[1], refs[-2], refs[-1]
        part = lax.dot_general(a_ref[...], b_ref[0], _CONTRACT_FIRST, preferred_element_type=F32)
        _acc_out(o_ref.at[0, 0], pl.program_id(1) == 0, part)

        @pl.when(pl.program_id(1) == nk - 1)
        def _():
            ob_ref[...] = o_ref[...].astype(BF16)

    spec = pl.BlockSpec((1, 1, m, n), lambda j, k: (j, layer, 0, 0))
    return pl.pallas_call(
        body, name="dw_up", grid=(s, nk),
        in_specs=[pl.BlockSpec((tk, m), lambda j, k: (k, 0)), pl.BlockSpec((1, tk, n), lambda j, k: (j, k, 0))]
                 + ([] if buf is None else [_ANY, _ANY]),
        out_specs=[spec, spec],
        out_shape=[jax.ShapeDtypeStruct((s, 2, m, n), F32), jax.ShapeDtypeStruct((s, 2, m, n), BF16)],
        input_output_aliases={} if buf is None else {2: 0, 3: 1},
        compiler_params=_cp(("arbitrary", "arbitrary")),
    )(h, dup, *([] if buf is None else buf))


def _dw_down(act, df, layer, buf):
    nc, t, n = act.shape
    d = df.shape[1]
    tk = min(TK_DW, t)
    nk = t // tk

    def body(*refs):
        a_ref, b_ref, o_ref, ob_ref = refs[0], refs[1], refs[-2], refs[-1]
        part = lax.dot_general(a_ref[0], b_ref[...], _CONTRACT_FIRST, preferred_element_type=F32)
        part = part.reshape(2, n // 2, d)
        first = pl.program_id(1) == 0

        @pl.when(first)
        def _():
            o_ref[:, 0] = part

        @pl.when(jnp.logical_not(first))
        def _():
            o_ref[:, 0] += part

        @pl.when(pl.program_id(1) == nk - 1)
        def _():
            ob_ref[...] = o_ref[...].astype(BF16)

    spec = pl.BlockSpec((2, 1, n // 2, d), lambda c, k: (c, layer, 0, 0))
    return pl.pallas_call(
        body, name="dw_down", grid=(nc, nk),
        in_specs=[pl.BlockSpec((1, tk, n), lambda c, k: (c, k, 0)), pl.BlockSpec((tk, d), lambda c, k: (k, 0))]
                 + ([] if buf is None else [_ANY, _ANY]),
        out_specs=[spec, spec],
        out_shape=[jax.ShapeDtypeStruct((2 * nc, 2, n // 2, d), F32), jax.ShapeDtypeStruct((2 * nc, 2, n // 2, d), BF16)],
        input_output_aliases={} if buf is None else {2: 0, 3: 1},
        compiler_params=_cp(("arbitrary", "arbitrary")),
    )(act, df, *([] if buf is None else buf))


def _dz_wt_rms_bwd(dz, w, x_in, gpre, dres):
    t, n = dz.shape
    tm = min(TM, t)

    def body(dz_ref, wt_ref, x_ref, g_ref, dres_ref, dx_ref, dg_ref):
        dh = lax.dot_general(dz_ref[...], wt_ref[...], _CONTRACT_LAST, preferred_element_type=F32)
        dx, dg = _rms_bwd(x_ref[...], g_ref[...], dh)
        dx_ref[...] = dres_ref[...] + dx
        _acc_out(dg_ref, pl.program_id(0) == 0, dg)

    return pl.pallas_call(
        body, name="dz_wt_rms_bwd", grid=(t // tm,),
        in_specs=[_rows(tm, n), _full((D_MODEL, n)), _rows(tm, D_MODEL), _full((1, D_MODEL)), _rows(tm, D_MODEL)],
        out_specs=[_rows(tm, D_MODEL), _full((1, D_MODEL))],
        out_shape=[jax.ShapeDtypeStruct((t, D_MODEL), F32), jax.ShapeDtypeStruct((1, D_MODEL), F32)],
        compiler_params=_cp(("arbitrary",)),
    )(dz, w, x_in, gpre, dres)


def _shift_matrices(shift, shift_h, tm, hb):
    row = lax.broadcasted_iota(jnp.int32, (2 * tm, tm), 0)
    col = lax.broadcasted_iota(jnp.int32, (2 * tm, tm), 1)
    hit = ((row < tm) & (col == row + 1)) | ((row >= tm) & (col == row - tm + 2))
    shift[...] = jnp.where(hit, 1.0, 0.0).astype(BF16)
    row = lax.broadcasted_iota(jnp.int32, (hb, hb), 0)
    col = lax.broadcasted_iota(jnp.int32, (hb, hb), 1)
    hit = ((row < HALO3) & (col == row - (HALO3 - 1))) | ((row >= HALO3) & (col == row - (2 * HALO3 - 2)))
    shift_h[...] = jnp.where(hit, 1.0, 0.0).astype(BF16)


def _next_rows(shift, shift_h, xb, nxt, d12_s, tm):
    d12_s[...] = jnp.dot(shift[...], xb, preferred_element_type=F32)
    edge = jnp.dot(shift_h[...], nxt, preferred_element_type=F32)
    d12_s[tm - HALO3:tm, :] += edge[0:HALO3, :]
    d12_s[2 * tm - HALO3:2 * tm, :] += edge[HALO3:2 * HALO3, :]


def _ffn_bwd(f, dxo, gpost, x_in, gpre, up, u, cw, wd, wup, layer, target=None):
    t = f.shape[0]
    tm = min(TM_BWD, t)
    nt = t // tm
    nc, n = up.shape[1], up.shape[3]
    hb = 2 * HALO3

    def body(*refs):
        f_ref, dxo_ref, gpost_ref, x_ref, gpre_ref, up_ref, u_ref, cw_ref, wd_ref, wup_ref = refs[:10]
        n_in = 10 if target is None else 11
        n_out = 7 if target is None else 8
        df_ref, act_ref, dup_ref, dx_ref, dgpost_ref, dgpre_ref, dcw_ref = refs[n_in:n_in + 7]
        df_s, acc, du_s, dub_s, d12_s, hal, shift, shift_h = refs[n_in + n_out:]
        i, c = pl.program_id(0), pl.program_id(1)

        def incoming():
            if target is None:
                return dxo_ref[...]
            return (dxo_ref[...] - refs[10][...]) * (1.0 / D_MODEL)

        @pl.when(c == 0)
        def _():
            dy = incoming()
            df, dg = _rms_bwd(f_ref[...], gpost_ref[...], dy)
            df_s[...] = df.astype(BF16)
            df_ref[...] = df.astype(BF16)
            _acc_out(dgpost_ref, i == 0, dg)
            if target is not None:
                part = jnp.zeros((1, 128), F32) + jnp.sum(dy * dy) * (0.5 * D_MODEL)
                _acc_out(refs[n_in + 7], i == 0, part)

        @pl.when(i == 0)
        def _():
            hal[c] = jnp.zeros((2, hb, n), BF16)
            dcw_ref[0, c] = jnp.zeros((8, n), F32)
            dcw_ref[1, c] = jnp.zeros((8, n), F32)

        @pl.when(jnp.logical_and(i == 0, c == 0))
        def _():
            _shift_matrices(shift, shift_h, tm, hb)

        dact = lax.dot_general(df_s[...], wd_ref[...], _CONTRACT_LAST, preferred_element_type=F32)
        g, v = u_ref[0, 0].astype(F32), u_ref[1, 0].astype(F32)
        sg = _sigmoid(g)
        sil = g * sg
        act_ref[0] = (sil * v).astype(BF16)
        dug = dact * v * (sg + sil * (1.0 - sg))
        duv = dact * sil
        du_s[0], du_s[1] = dug, duv
        dub_s[0], dub_s[1] = dug.astype(BF16), duv.astype(BF16)
        dh = None
        for gv in range(2):
            _next_rows(shift, shift_h, dub_s[gv], hal[c, gv], d12_s, tm)
            hal[c, gv] = dub_s[gv, 0:hb, :]
            du, d1, d2 = du_s[gv], d12_s[0:tm, :], d12_s[tm:2 * tm, :]
            dup = (cw_ref[gv, 0, 2:3, :] * du + cw_ref[gv, 0, 1:2, :] * d1 + cw_ref[gv, 0, 0:1, :] * d2).astype(BF16)
            dup_ref[gv, 0] = dup
            upc = up_ref[gv, 0].astype(F32)
            dcw_ref[gv, c, 2:3, :] += jnp.sum(upc * du, axis=0, keepdims=True)
            dcw_ref[gv, c, 1:2, :] += jnp.sum(upc * d1, axis=0, keepdims=True)
            dcw_ref[gv, c, 0:1, :] += jnp.sum(upc * d2, axis=0, keepdims=True)
            part = lax.dot_general(dup, wup_ref[gv, 0, 0], _CONTRACT_LAST, preferred_element_type=F32)
            dh = part if dh is None else dh + part
        _acc_out(acc, c == 0, dh)

        @pl.when(c == nc - 1)
        def _():
            dx, dg = _rms_bwd(x_ref[...], gpre_ref[...], acc[...])
            dx_ref[...] = incoming() + dx
            _acc_out(dgpre_ref, i == 0, dg)

    rrow = lambda w: pl.BlockSpec((tm, w), lambda i, c: (nt - 1 - i, 0))
    blk = pl.BlockSpec((2, 1, tm, n), lambda i, c: (0, c, nt - 1 - i, 0))
    one = _full((1, D_MODEL))
    return pl.pallas_call(
        body, name="ffn_bwd", grid=(nt, nc),
        in_specs=[rrow(D_MODEL), rrow(D_MODEL), one, rrow(D_MODEL), one, blk, blk,
                  pl.BlockSpec((2, 1, 3, n), lambda i, c: (0, c, 0, 0)),
                  pl.BlockSpec((n, D_MODEL), lambda i, c: (c, 0)),
                  pl.BlockSpec((2, 1, 1, D_MODEL, n), lambda i, c: (0, c, layer, 0, 0))]
                 + ([] if target is None else [rrow(D_MODEL)]),
        out_specs=[rrow(D_MODEL), pl.BlockSpec((1, tm, n), lambda i, c: (c, nt - 1 - i, 0)), blk, rrow(D_MODEL),
                   one, one, _full((2, nc, 8, n))] + ([] if target is None else [_full((1, 128))]),
        out_shape=[jax.ShapeDtypeStruct((t, D_MODEL), BF16), jax.ShapeDtypeStruct((nc, t, n), BF16),
                   jax.ShapeDtypeStruct((2, nc, t, n), BF16), jax.ShapeDtypeStruct((t, D_MODEL), F32),
                   jax.ShapeDtypeStruct((1, D_MODEL), F32), jax.ShapeDtypeStruct((1, D_MODEL), F32),
                   jax.ShapeDtypeStruct((2, nc, 8, n), F32)]
                  + ([] if target is None else [jax.ShapeDtypeStruct((1, 128), F32)]),
        scratch_shapes=[pltpu.VMEM((tm, D_MODEL), BF16), pltpu.VMEM((tm, D_MODEL), F32),
                        pltpu.VMEM((2, tm, n), F32), pltpu.VMEM((2, tm, n), BF16), pltpu.VMEM((2 * tm, n), F32),
                        pltpu.VMEM((nc, 2, hb, n), BF16), pltpu.VMEM((2 * tm, tm), BF16), pltpu.VMEM((hb, hb), BF16)],
        compiler_params=_cp(("arbitrary", "arbitrary")),
    )(f, dxo, gpost, x_in, gpre, up, u, cw, wd, wup, *([] if target is None else [target]))


def _od_bwd(m, dxo, gpost, x_in, gpre, z, cv, cw, w_out, wint):
    t = m.shape[0]
    tm = min(TM_BWD, t)
    nt = t // tm
    hb = 2 * HALO3

    def body(m_ref, dxo_ref, gpost_ref, x_ref, gpre_ref, z_ref, cv_ref, cw_ref, wo_ref, wi_ref,
             dm_ref, dz_ref, dx_ref, dgpost_ref, dgpre_ref, dcw_ref, dcvb_s, d12_s, dz_s, hal, shift, shift_h):
        i = pl.program_id(0)
        dxo = dxo_ref[...]
        dm, dg = _rms_bwd(m_ref[...], gpost_ref[...], dxo)
        dmb = dm.astype(BF16)
        dm_ref[...] = dmb
        _acc_out(dgpost_ref, i == 0, dg)

        @pl.when(i == 0)
        def _():
            hal[...] = jnp.zeros((hb, SC_DIM), BF16)
            dcw_ref[...] = jnp.zeros((8, SC_DIM), F32)
            _shift_matrices(shift, shift_h, tm, hb)

        dy = lax.dot_general(dmb, wo_ref[...], _CONTRACT_LAST, preferred_element_type=F32)
        z = z_ref[...].astype(F32)
        b, cg, u = z[:, :SC_DIM], z[:, SC_DIM:2 * SC_DIM], z[:, 2 * SC_DIM:]
        dz_s[:, 0:SC_DIM] = (dy * cv_ref[...].astype(F32)).astype(BF16)
        dcv = dy * b
        dcvb_s[...] = dcv.astype(BF16)
        _next_rows(shift, shift_h, dcvb_s[...], hal[...], d12_s, tm)
        hal[...] = dcvb_s[0:hb, :]
        d1, d2 = d12_s[0:tm, :], d12_s[tm:2 * tm, :]
        dcu = cw_ref[2:3, :] * dcv + cw_ref[1:2, :] * d1 + cw_ref[0:1, :] * d2
        cu = cg * u
        dcw_ref[2:3, :] += jnp.sum(cu * dcv, axis=0, keepdims=True)
        dcw_ref[1:2, :] += jnp.sum(cu * d1, axis=0, keepdims=True)
        dcw_ref[0:1, :] += jnp.sum(cu * d2, axis=0, keepdims=True)
        dz_s[:, SC_DIM:2 * SC_DIM] = (dcu * u).astype(BF16)
        dz_s[:, 2 * SC_DIM:3 * SC_DIM] = (dcu * cg).astype(BF16)
        dz_ref[...] = dz_s[...]
        dh = jnp.dot(dz_s[...], wi_ref[...], preferred_element_type=F32)
        dx, dg2 = _rms_bwd(x_ref[...], gpre_ref[...], dh)
        dx_ref[...] = dxo + dx
        _acc_out(dgpre_ref, i == 0, dg2)

    rrow = lambda w: pl.BlockSpec((tm, w), lambda i: (nt - 1 - i, 0))
    one = _full((1, D_MODEL))
    return pl.pallas_call(
        body, name="od_bwd", grid=(nt,),
        in_specs=[rrow(D_MODEL), rrow(D_MODEL), one, rrow(D_MODEL), one, rrow(3 * SC_DIM), rrow(SC_DIM),
                  _full((3, SC_DIM)), _full((SC_DIM, D_MODEL)), _full((3 * SC_DIM, D_MODEL))],
        out_specs=[rrow(D_MODEL), rrow(3 * SC_DIM), rrow(D_MODEL), one, one, _full((8, SC_DIM))],
        out_shape=[jax.ShapeDtypeStruct((t, D_MODEL), BF16), jax.ShapeDtypeStruct((t, 3 * SC_DIM), BF16),
                   jax.ShapeDtypeStruct((t, D_MODEL), F32), jax.ShapeDtypeStruct((1, D_MODEL), F32),
                   jax.ShapeDtypeStruct((1, D_MODEL), F32), jax.ShapeDtypeStruct((8, SC_DIM), F32)],
        scratch_shapes=[pltpu.VMEM((tm, SC_DIM), BF16), pltpu.VMEM((2 * tm, SC_DIM), F32),
                        pltpu.VMEM((tm, 3 * SC_DIM), BF16), pltpu.VMEM((hb, SC_DIM), BF16),
                        pltpu.VMEM((2 * tm, tm), BF16), pltpu.VMEM((hb, hb), BF16)],
        compiler_params=_cp(("arbitrary",)),
    )(m, dxo, gpost, x_in, gpre, z, cv, cw, w_out, wint)


def _ev_bwd1(m, dxo, gpost, w_out):
    t = m.shape[0]
    tm = min(TM, t)

    def body(m_ref, dxo_ref, g_ref, wot_ref, dm_ref, da_ref, do_ref, dg_ref):
        dm, dg = _rms_bwd(m_ref[...], g_ref[...], dxo_ref[...])
        dmb = dm.astype(BF16)
        dm_ref[...] = dmb
        _acc_out(dg_ref, pl.program_id(0) == 0, dg)
        dao = lax.dot_general(dmb, wot_ref[...], _CONTRACT_LAST, preferred_element_type=F32)
        da_ref[...] = dao[:, :A_CH]
        do_ref[...] = dao[:, A_CH:].astype(BF16)

    return pl.pallas_call(
        body, name="ev_bwd1", grid=(t // tm,),
        in_specs=[_rows(tm, D_MODEL), _rows(tm, D_MODEL), _full((1, D_MODEL)), _full((A_CH + Q_DIM, D_MODEL))],
        out_specs=[_rows(tm, D_MODEL), _rows(tm, A_CH), _rows(tm, Q_DIM), _full((1, D_MODEL))],
        out_shape=[jax.ShapeDtypeStruct((t, D_MODEL), BF16), jax.ShapeDtypeStruct((t, A_CH), F32),
                   jax.ShapeDtypeStruct((t, Q_DIM), BF16), jax.ShapeDtypeStruct((1, D_MODEL), F32)],
        compiler_params=_cp(("arbitrary",)),
    )(m, dxo, gpost, w_out)


def _conf_bwd(da, cv, zag, conv_w, ln_g, ln_b):
    t = da.shape[0]
    tm = min(TM_BWD, t)
    nt = t // tm
    rows = tm + HALO31 - 8

    def body(da_ref, c_ref, z_ref, w_ref, g_ref, lb_ref, dz_ref, dw_ref, dv_ref, ext_out, cbuf, glu_s, dglu_s):
        i = pl.program_id(0)

        @pl.when(i == 0)
        def _():
            ext_out[tm:tm + HALO31, :] = jnp.zeros((HALO31, A_CH), F32)
            dw_ref[...] = jnp.zeros((32, A_CH), F32)
            dv_ref[...] = jnp.zeros((8, A_CH), F32)

        x = c_ref[...]
        mu = jnp.mean(x, axis=-1, keepdims=True)
        xc = x - mu
        rstd = lax.rsqrt(jnp.mean(xc * xc, axis=-1, keepdims=True) + LN_EPS)
        xh = xc * rstd
        ln = xh * g_ref[...] + lb_ref[...]
        sl = _sigmoid(ln)
        dln = da_ref[...] * (sl * (1.0 + ln * (1.0 - sl)))
        dxh = dln * g_ref[...]
        dc = rstd * (dxh - jnp.mean(dxh, axis=-1, keepdims=True) - xh * jnp.mean(dxh * xh, axis=-1, keepdims=True))
        dv_ref[0:1, :] += jnp.sum(dc, axis=0, keepdims=True)
        dv_ref[1:2, :] += jnp.sum(dln * xh, axis=0, keepdims=True)
        dv_ref[2:3, :] += jnp.sum(dln, axis=0, keepdims=True)

        ext_out[0:tm, :] = dc
        _tap_copies(ext_out, cbuf, lambda b: b, rows)
        z = z_ref[...].astype(F32)
        al, sg = z[:, :A_CH], _sigmoid(z[:, A_CH:])
        glu_s[...] = al * sg
        for rs in range(0, tm, SUB):
            for cs in range(0, A_CH, 128):
                glu = glu_s[rs:rs + SUB, cs:cs + 128]
                acc = jnp.zeros((SUB, 128), F32)
                for k in range(A_CONV):
                    lag_a, lag_b = divmod(k, 8)
                    r0 = 8 * lag_a + rs
                    d = (ext_out[r0:r0 + SUB, cs:cs + 128] if lag_b == 0
                         else cbuf[lag_b - 1, r0:r0 + SUB, cs:cs + 128])
                    j = A_CONV - 1 - k
                    acc = acc + w_ref[j:j + 1, cs:cs + 128] * d
                    dw_ref[j:j + 1, cs:cs + 128] += jnp.sum(glu * d, axis=0, keepdims=True)
                dglu_s[rs:rs + SUB, cs:cs + 128] = acc
        dglu = dglu_s[...]
        ext_out[tm:tm + HALO31, :] = ext_out[0:HALO31, :]
        dz_ref[:, 0:A_CH] = (dglu * sg).astype(BF16)
        dz_ref[:, A_CH:2 * A_CH] = (dglu * al * sg * (1.0 - sg)).astype(BF16)

    rrow = lambda w: pl.BlockSpec((tm, w), lambda i: (nt - 1 - i, 0))
    return pl.pallas_call(
        body, name="conf_bwd", grid=(nt,),
        in_specs=[rrow(A_CH), rrow(A_CH), rrow(2 * A_CH), _full((32, A_CH)), _full((1, A_CH)), _full((1, A_CH))],
        out_specs=[rrow(2 * A_CH), _full((32, A_CH)), _full((8, A_CH))],
        out_shape=[jax.ShapeDtypeStruct((t, 2 * A_CH), BF16), jax.ShapeDtypeStruct((32, A_CH), F32),
                   jax.ShapeDtypeStruct((8, A_CH), F32)],
        scratch_shapes=[pltpu.VMEM((tm + HALO31, A_CH), F32), pltpu.VMEM((7, rows, A_CH), F32),
                        pltpu.VMEM((tm, A_CH), F32), pltpu.VMEM((tm, A_CH), F32)],
        compiler_params=_cp(("arbitrary",)),
    )(da, cv, zag, conv_w, ln_g, ln_b)


def _attn_bwd(q, k, v, do, sinks):
    t = q.shape[0]
    nb = min(ATT_NB, t // BLOCK)
    rows = nb * BLOCK
    ns = t // rows

    def body(s_ref, q_ref, kc_ref, kp_ref, vc_ref, vp_ref, do_ref, dq_ref, dk_ref, dv_ref, ds_ref, dkc, dvc):
        i = pl.program_id(0)
        r = ns - 1 - i

        @pl.when(i == 0)
        def _():
            dkc[...] = jnp.zeros_like(dkc)
            dvc[...] = jnp.zeros_like(dvc)
            ds_ref[...] = jnp.zeros_like(ds_ref)

        lane = lax.broadcasted_iota(jnp.int32, (1, N_Q_HEADS), 1)
        dsv = jnp.zeros((1, N_Q_HEADS), F32)
        for b in range(nb - 1, -1, -1):
            lo = BLOCK * b
            mask = _attn_mask(r == 0) if b == 0 else _attn_mask(False)
            qv, dov = q_ref[lo:lo + BLOCK, :], do_ref[lo:lo + BLOCK, :]
            kc, vc = kc_ref[lo:lo + BLOCK, :], vc_ref[lo:lo + BLOCK, :]
            kp = kp_ref[...] if b == 0 else kc_ref[lo - BLOCK:lo, :]
            vp = vp_ref[...] if b == 0 else vc_ref[lo - BLOCK:lo, :]
            for h in range(N_KV_HEADS):
                q4, do4 = _q_heads(qv, h), _q_heads(dov, h)
                k2, v2 = _kv_head(kp, kc, h), _kv_head(vp, vc, h)
                pn, ps = _attn_probs(q4, k2, mask, _sink_rows(s_ref, h))
                dp = lax.dot_general(do4, v2, _CONTRACT_LAST, preferred_element_type=F32)
                dl = jnp.sum(pn * dp, axis=-1, keepdims=True)
                dsb = (pn * (dp - dl)).astype(BF16)
                dq4 = (jnp.dot(dsb, k2, preferred_element_type=F32) * SCALE).astype(BF16)
                for g in range(GROUP):
                    c0 = HEAD_DIM * (GROUP * h + g)
                    dq_ref[lo:lo + BLOCK, c0:c0 + HEAD_DIM] = dq4[BLOCK * g:BLOCK * (g + 1), :]
                dk2 = lax.dot_general(dsb, q4, _CONTRACT_FIRST, preferred_element_type=F32) * SCALE
                dv2 = lax.dot_general(pn.astype(BF16), do4, _CONTRACT_FIRST, preferred_element_type=F32)
                dk_ref[lo:lo + BLOCK, HEAD_DIM * h:HEAD_DIM * (h + 1)] = dk2[BLOCK:, :] + dkc[h]
                dv_ref[lo:lo + BLOCK, HEAD_DIM * h:HEAD_DIM * (h + 1)] = dv2[BLOCK:, :] + dvc[h]
                dkc[h] = dk2[:BLOCK, :]
                dvc[h] = dv2[:BLOCK, :]
                srow = -ps * dl
                for g in range(GROUP):
                    dsv = dsv + jnp.where(lane == GROUP * h + g, jnp.sum(srow[BLOCK * g:BLOCK * (g + 1), :]), 0.0)
        ds_ref[...] += dsv

    cur = lambda n: pl.BlockSpec((rows, n), lambda i: (ns - 1 - i, 0))
    prev = lambda n: pl.BlockSpec((BLOCK, n), lambda i: (jnp.maximum((ns - 1 - i) * nb - 1, 0), 0))
    return pl.pallas_call(
        body, name="attn_bwd", grid=(ns,),
        in_specs=[pl.BlockSpec(memory_space=pltpu.SMEM), cur(Q_DIM), cur(KV_DIM), prev(KV_DIM), cur(KV_DIM),
                  prev(KV_DIM), cur(Q_DIM)],
        out_specs=[cur(Q_DIM), cur(KV_DIM), cur(KV_DIM), _full((1, N_Q_HEADS))],
        out_shape=[jax.ShapeDtypeStruct((t, Q_DIM), BF16), jax.ShapeDtypeStruct((t, KV_DIM), F32),
                   jax.ShapeDtypeStruct((t, KV_DIM), F32), jax.ShapeDtypeStruct((1, N_Q_HEADS), F32)],
        scratch_shapes=[pltpu.VMEM((N_KV_HEADS, BLOCK, HEAD_DIM), F32), pltpu.VMEM((N_KV_HEADS, BLOCK, HEAD_DIM), F32)],
        compiler_params=_cp(("arbitrary",)),
    )(sinks, q, k, k, v, v, do)


def _ev_dz(dzag, dq, dk, dv, rc, rsa, rsb):
    t = dzag.shape[0]
    tm = min(TM, t)

    def body(dzag_ref, dq_ref, dk_ref, dv_ref, c_ref, sa_ref, sb_ref, dz_ref):
        c, sa, sb = c_ref[...], sa_ref[...], sb_ref[...]
        dz_ref[:, 0:2 * A_CH] = dzag_ref[...]
        q0 = 2 * A_CH
        for j in range(Q_DIM // 128):
            d = dq_ref[:, 128 * j:128 * (j + 1)].astype(F32)
            dz_ref[:, q0 + 128 * j:q0 + 128 * (j + 1)] = _rope_bwd(d, c, sa, sb).astype(BF16)
        k0 = q0 + Q_DIM
        dz_ref[:, k0:k0 + KV_DIM] = _rope_bwd(dk_ref[...], c, sa, sb).astype(BF16)
        dz_ref[:, k0 + KV_DIM:k0 + 2 * KV_DIM] = dv_ref[...].astype(BF16)

    return pl.pallas_call(
        body, name="ev_dz", grid=(t // tm,),
        in_specs=[_rows(tm, 2 * A_CH), _rows(tm, Q_DIM), _rows(tm, KV_DIM), _rows(tm, KV_DIM),
                  _rows(tm, 128), _rows(tm, 128), _rows(tm, 128)],
        out_specs=_rows(tm, EVEN_IN),
        out_shape=jax.ShapeDtypeStruct((t, EVEN_IN), BF16),
        compiler_params=_cp(("arbitrary",)),
    )(dzag, dq, dk, dv, rc, rsa, rsb)


def _prep_ev(gat):
    p = {}
    p["ev_w_in"] = gat["ev_w_in"][:, 0].transpose(1, 0, 2).reshape(D_MODEL, EVEN_IN)
    p["ev_w_out"] = gat["ev_w_out"].reshape(A_CH + Q_DIM, D_MODEL)
    return p


def _prep_rest(gat):
    p = {}
    g = gat["od_w_in"][:, 0]
    p["od_w_in"], p["od_w_in_t"] = g, g.transpose(0, 2, 1).reshape(3 * SC_DIM, D_MODEL)
    p["od_w_out"] = gat["od_w_out"].reshape(SC_DIM, D_MODEL)
    p["ffn_w_up"] = gat["ffn_w_up"].reshape(2, N_DEV // 2, 2, D_MODEL, FF_N)
    p["ffn_w_down"] = [gat["ffn_w_down"][:, i].reshape(D_FF, D_MODEL) for i in range(2)]
    return p


def _local_step(x, positions, target, p, rest_weights, s, token, grads_ready):
    row = lambda a, tok=None: a.reshape(1, -1) if tok is None else a.reshape(1, -1) + tok
    nc = N_DEV // 2
    rc, rsa, rsb = _rope_tables(positions)
    conv31 = jnp.pad(s["ev_a_conv_w"][0], ((0, 1), (0, 0)))
    cw_ffn = [s["ffn_conv_w"][i].reshape(3, 2, nc, FF_N).transpose(1, 2, 0, 3) for i in range(2)]
    sinks = s["ev_sinks"][0]
    big, g = {}, {}

    h0, zag, q, k, v = _ev_in(x, row(s["mix_norm_pre"][0], token), p["ev_w_in"], rc, rsa, rsb)
    cv, a = _conf_fwd(zag, conv31, s["ev_a_conv_b"], s["ev_a_ln_g"], s["ev_a_ln_b"])
    o = _attn_fwd(q, k, v, sinks)
    wo = p["ev_w_out"]
    m0, x1 = _out_post([a, o], [wo[:A_CH], wo[A_CH:]], x, row(s["mix_norm_post"][0]))
    p = {**p, **rest_weights(m0)}
    h1, up0, u0, f0, x2 = _ffn_fwd(x1, row(s["ffn_norm_pre"][0]), p["ffn_w_up"], 0, cw_ffn[0], p["ffn_w_down"][0],
                                   row(s["ffn_norm_post"][0]))
    h2, z, cv1, y, m1, x3 = _od_fwd(x2, row(s["mix_norm_pre"][1]), p["od_w_in"], s["od_conv_w"][0], p["od_w_out"],
                                    row(s["mix_norm_post"][1]))
    h3, up1, u1, f1, x4 = _ffn_fwd(x3, row(s["ffn_norm_pre"][1]), p["ffn_w_up"], 1, cw_ffn[1], p["ffn_w_down"][1],
                                   row(s["ffn_norm_post"][1]))

    def ffn_back(i, f, dxo, up, u, h, x_in, bufs, tok=None, tgt=None):
        df, act, dup, dx_in, dgpost, dgpre, dcw, *loss = _ffn_bwd(
            f, dxo, row(s["ffn_norm_post"][i], tok), x_in, row(s["ffn_norm_pre"][i]), up, u, cw_ffn[i],
            p["ffn_w_down"][i], p["ffn_w_up"], i, tgt)
        bufs = (_dw_up(h, dup.reshape(N_DEV, -1, FF_N), i, bufs[0]), _dw_down(act, df, i, bufs[1]))
        return dx_in, dgpost, dgpre, dcw[:, :, 0:3].transpose(2, 0, 1, 3).reshape(3, 2 * D_FF), bufs, loss

    dx, dgfpost1, dgfpre1, dcw1, bufs, (lpart,) = ffn_back(1, f1, x4, up1, u1, h3, x3, (None, None), None, target)

    dm1, dz, dx, dgpost1, dgpre1, dcw_od = _od_bwd(m1, dx, row(s["mix_norm_post"][1]), x2, row(s["mix_norm_pre"][1]), z,
                                                   cv1, s["od_conv_w"][0], p["od_w_out"], p["od_w_in_t"])
    big["od_w_out"] = _dw2d(y, dm1, SC_DIM, D_MODEL).reshape(N_DEV, -1, D_MODEL)
    big["od_w_in"], big["od_w_in:bf16"] = _dw_cols(h2, dz, 3 * SC_DIM // N_DEV)
    g["od_conv_w"] = dcw_od[None, 0:3]
    tok = grads_ready(["od_w_in", "od_w_out"], big)

    dx, dgfpost0, dgfpre0, dcw0, bufs, _ = ffn_back(0, f0, dx, up0, u0, h1, x1, bufs, tok)
    (big["ffn_w_up"], big["ffn_w_up:bf16"]), (big["ffn_w_down"], big["ffn_w_down:bf16"]) = bufs
    tok = grads_ready(["ffn_w_up", "ffn_w_down"], big)

    dm0, da, do, dgpost0 = _ev_bwd1(m0, dx, row(s["mix_norm_post"][0], tok), p["ev_w_out"])
    big["ev_w_out"] = jnp.concatenate([_dw2d(a, dm0, A_CH, D_MODEL), _dw2d(o, dm0, Q_DIM, D_MODEL)],
                                      axis=0).reshape(N_DEV, -1, D_MODEL)
    tok = grads_ready(["ev_w_out"], big)
    dzag, dcw31, dvec = _conf_bwd(da, cv, zag, conv31, s["ev_a_ln_g"] + tok, s["ev_a_ln_b"])
    dq, dk, dv, dsinks = _attn_bwd(q, k, v, do, sinks)
    dz0 = _ev_dz(dzag, dq, dk, dv, rc, rsa, rsb)
    dw_in = _dw2d(h0, dz0, D_MODEL, EVEN_IN // 2)
    big["ev_w_in"] = dw_in.reshape(D_MODEL, N_DEV, EVEN_IN // N_DEV).transpose(1, 0, 2)
    tok = grads_ready(["ev_w_in"], big)
    dx, dgpre0 = _dz_wt_rms_bwd(dz0, p["ev_w_in"], x, row(s["mix_norm_pre"][0], tok), dx)

    g["mix_norm_pre"] = jnp.concatenate([dgpre0, dgpre1], axis=0)
    g["mix_norm_post"] = jnp.concatenate([dgpost0, dgpost1], axis=0)
    g["ffn_norm_pre"] = jnp.concatenate([dgfpre0, dgfpre1], axis=0)
    g["ffn_norm_post"] = jnp.concatenate([dgfpost0, dgfpost1], axis=0)
    g["ev_a_conv_w"] = dcw31[None, 0:A_CONV]
    g["ev_a_conv_b"], g["ev_a_ln_g"], g["ev_a_ln_b"] = dvec[0:1], dvec[1:2], dvec[2:3]
    g["ev_sinks"] = dsinks
    g["ffn_conv_w"] = jnp.stack([dcw0, dcw1])
    return lpart[0, 0], dx, big, g


MESH = pl.DeviceIdType.MESH


def _all_gather(shards, name):
    nw = len(shards)

    def body(*refs):
        x_refs, out_refs = refs[:nw], refs[nw:2 * nw]
        send_sems, recv_sems, local_sems = refs[2 * nw:]
        x, y, c = lax.axis_index("x"), lax.axis_index("y"), lax.axis_index("c")
        me, sibling = (x, y, c), (x, y, 1 - c)
        chips = [(1 - x, y), (x, 1 - y), (1 - x, 1 - y)]

        def rows(w, px, py, pc):
            m_per = shards[w].shape[0]
            return out_refs[w].at[pl.ds((4 * px + 2 * py + pc) * m_per, m_per), :]

        def copy(w, k, block, to, src=None):
            return pltpu.make_async_remote_copy(
                src_ref=rows(w, *block) if src is None else src, dst_ref=rows(w, *block),
                send_sem=send_sems.at[w, k], recv_sem=recv_sems.at[w, k], device_id=to, device_id_type=MESH)

        mine, first, passed = [], [], []
        for w in range(nw):
            cp = pltpu.make_async_copy(x_refs[w], rows(w, *me), local_sems.at[w])
            cp.start()
            mine.append(cp)
            first.append([copy(w, 0, me, sibling, src=x_refs[w])]
                         + [copy(w, 1 + j, me, (*chip, c), src=x_refs[w]) for j, chip in enumerate(chips)])
            for cp in first[w]:
                cp.start()
        for w in range(nw):
            passed.append([copy(w, 4 + j, (*chip, c), sibling) for j, chip in enumerate(chips)])
            for j, chip in enumerate(chips):
                copy(w, 1 + j, (*chip, c), me).wait_recv()
                passed[w][j].start()
        for w in range(nw):
            copy(w, 0, sibling, me).wait_recv()
            for j, chip in enumerate(chips):
                copy(w, 4 + j, (*chip, 1 - c), me).wait_recv()
            for cp in first[w] + passed[w]:
                cp.wait_send()
            mine[w].wait()

    return pl.pallas_call(
        body, name=name,
        out_shape=[jax.ShapeDtypeStruct((N_DEV * a.shape[0], a.shape[1]), a.dtype) for a in shards],
        in_specs=[_ANY] * nw, out_specs=[_ANY] * nw,
        scratch_shapes=[pltpu.SemaphoreType.DMA((nw, 7)), pltpu.SemaphoreType.DMA((nw, 7)),
                        pltpu.SemaphoreType.DMA((nw,))],
    )(*shards)


_HBM = pl.BlockSpec(memory_space=pltpu.HBM)
_SEM = pl.BlockSpec(memory_space=pltpu.SEMAPHORE)
_EFFECT = pltpu.SideEffectType.DATAFLOW_SIDE_EFFECTING
_RELATIONS = [(dx, dy, dc) for dx in (0, 1) for dy in (0, 1) for dc in (0, 1)][1:]


def _peer(rel):
    x, y, c = lax.axis_index("x"), lax.axis_index("y"), lax.axis_index("c")
    px, py, pc = x ^ rel[0], y ^ rel[1], c ^ rel[2]
    return (px, py, pc), 4 * px + 2 * py + pc, 4 * x + 2 * y + c


def _exchange_copy(k, rel, src_ref, land_ref, send_sems, recv_sems, w, scatter):
    peer, peer_idx, my_idx = _peer(rel)
    src = src_ref.at[peer_idx] if scatter else src_ref
    return pltpu.make_async_remote_copy(
        src_ref=src, dst_ref=land_ref.at[my_idx], send_sem=send_sems.at[_sends(scatter) * w + k],
        recv_sem=recv_sems.at[7 * w + k], device_id=peer, device_id_type=MESH)


def _sends(scatter):
    return 7 if scatter else 8


def _own_copy(src_ref, land_ref, send_sems, w):
    my_idx = _peer(_RELATIONS[0])[2]
    return pltpu.make_async_copy(src_ref, land_ref.at[my_idx], send_sems.at[8 * w + 7])


def _exchange_start(srcs, scatter, name):
    nw = len(srcs)
    lands = [lax.empty((N_DEV,) + (a.shape[1:] if scatter else a.shape), a.dtype) for a in srcs]

    def body(*refs):
        src_refs, land_refs = refs[:nw], refs[nw:2 * nw]
        send_sems, recv_sems = refs[2 * nw], refs[2 * nw + 1]
        token = refs[-1]
        for w in range(nw):
            for k, rel in enumerate(_RELATIONS):
                _exchange_copy(k, rel, src_refs[w], land_refs[w], send_sems, recv_sems, w, scatter).start()
            if not scatter:
                _own_copy(src_refs[w], land_refs[w], send_sems, w).start()
        token[...] = jnp.zeros_like(token)

    hbm = lambda a: pltpu.HBM(a.shape, a.dtype)
    outs = pl.pallas_call(
        body, name=name,
        out_shape=(pltpu.SemaphoreType.DMA((_sends(scatter) * nw,)), pltpu.SemaphoreType.DMA((7 * nw,)),
                   *[hbm(a) for a in srcs],
                   *[hbm(a) for a in lands], jax.ShapeDtypeStruct((8, 128), F32)),
        in_specs=[_HBM] * (2 * nw),
        out_specs=(_SEM, _SEM, *[_HBM] * (2 * nw), pl.BlockSpec(memory_space=pltpu.VMEM)),
        input_output_aliases={i: 2 + i for i in range(2 * nw)},
        compiler_params=pltpu.CompilerParams(has_side_effects=_EFFECT),
    )(*[pltpu.with_memory_space_constraint(a, pltpu.HBM) for a in srcs],
      *[pltpu.with_memory_space_constraint(a, pltpu.HBM) for a in lands])
    return outs[0], outs[1], list(outs[2:2 + nw]), list(outs[2 + nw:2 + 2 * nw]), outs[-1]


def _exchange_wait(started, scatter, after, name):
    send_sems, recv_sems, srcs, lands, _ = started
    nw = len(srcs)

    def body(*refs):
        src_refs, land_refs = refs[:nw], refs[nw:2 * nw]
        send_s, recv_s = refs[2 * nw], refs[2 * nw + 1]
        for w in range(nw):
            for k, rel in enumerate(_RELATIONS):
                cp = _exchange_copy(k, rel, src_refs[w], land_refs[w], send_s, recv_s, w, scatter)
                cp.wait_send()
                _, peer_idx, _ = _peer(rel)
                pltpu.make_async_remote_copy(
                    src_ref=src_refs[w].at[peer_idx] if scatter else src_refs[w], dst_ref=land_refs[w].at[peer_idx],
                    send_sem=send_s.at[_sends(scatter) * w + k], recv_sem=recv_s.at[7 * w + k],
                    device_id=_peer(rel)[0], device_id_type=MESH).wait_recv()
            if not scatter:
                _own_copy(src_refs[w], land_refs[w], send_s, w).wait()

    hbm = lambda a: pltpu.HBM(a.shape, a.dtype)
    outs = pl.pallas_call(
        body, name=name, out_shape=tuple(hbm(a) for a in srcs + lands),
        in_specs=[_HBM] * (2 * nw) + [_SEM, _SEM, _ANY], out_specs=tuple([_HBM] * (2 * nw)),
        input_output_aliases={i: i for i in range(2 * nw)},
        compiler_params=pltpu.CompilerParams(has_side_effects=_EFFECT),
    )(*srcs, *lands, send_sems, recv_sems, after)
    return list(outs[nw:])


def _to_bf16(a):
    _, r, l = a.shape
    tr = _row_tile(r, 512)

    def body(a_ref, o_ref):
        o_ref[...] = a_ref[...].astype(BF16)

    spec = pl.BlockSpec((1, tr, l), lambda j, i: (j, i, 0))
    return pl.pallas_call(
        body, name="to_bf16", grid=(N_DEV, r // tr), in_specs=[spec], out_specs=spec,
        out_shape=jax.ShapeDtypeStruct(a.shape, BF16), compiler_params=_cp(("arbitrary", "arbitrary")),
    )(a)


def _row_tile(rows, cap):
    best = None
    for d in range(16, min(rows, cap) + 1, 16):
        if rows % d == 0:
            best = d
    return rows if best is None else best


def _adam_math(w, g, m, v):
    bc1 = 1.0 - ADAM_B1 ** ADAM_STEP
    bc2 = 1.0 - ADAM_B2 ** ADAM_STEP
    mn = ADAM_B1 * m + (1.0 - ADAM_B1) * g
    vn = ADAM_B2 * v + (1.0 - ADAM_B2) * (g * g)
    return -ADAM_LR * ((mn / bc1) / (jnp.sqrt(vn / bc2) + ADAM_EPS) + ADAM_WD * w), mn, vn


def _adamw_rs(gp, land, w, m, v, dev):
    _, r, l = gp.shape
    tr = _row_tile(r, 256)

    def body(i_ref, g_ref, b_ref, w_ref, m_ref, v_ref, go_ref, d_ref, mo_ref, vo_ref):
        g = g_ref[0]
        for j in range(N_DEV):
            g = g + jnp.where(i_ref[0] == j, 0.0, b_ref[j].astype(F32))
        go_ref[...] = g
        d_ref[...], mo_ref[...], vo_ref[...] = _adam_math(w_ref[...], g, m_ref[...], v_ref[...])

    spec = pl.BlockSpec((tr, l), lambda i, s: (i, 0))
    return pl.pallas_call(
        body, name="adamw_rs", out_shape=[jax.ShapeDtypeStruct((r, l), F32)] * 4,
        grid_spec=pltpu.PrefetchScalarGridSpec(
            num_scalar_prefetch=1, grid=(r // tr,),
            in_specs=[pl.BlockSpec((1, tr, l), lambda i, s: (s[0], i, 0)),
                      pl.BlockSpec((N_DEV, tr, l), lambda i, s: (0, i, 0)), spec, spec, spec],
            out_specs=[spec] * 4),
        compiler_params=_cp(("arbitrary",)),
    )(dev, gp, land, w, m, v)


def _sum_blocks(a, nblk):
    m = a.shape[0] // nblk
    n = a.shape[1]

    def body(a_ref, o_ref):
        acc = a_ref[0]
        for j in range(1, nblk):
            acc = acc + a_ref[j]
        o_ref[...] = acc

    return pl.pallas_call(
        body, name="sum_blocks", out_shape=jax.ShapeDtypeStruct((m, n), a.dtype),
        in_specs=[_full((nblk, m, n))], out_specs=_full((m, n)),
    )(a.reshape(nblk, m, n))


def _adamw(w, g, m, v):
    rows, c = w.shape

    def body(w_ref, g_ref, m_ref, v_ref, d_ref, mo_ref, vo_ref):
        d_ref[...], mo_ref[...], vo_ref[...] = _adam_math(w_ref[...], g_ref[...], m_ref[...], v_ref[...])

    return pl.pallas_call(
        body, name="adamw", in_specs=[_full((rows, c))] * 4, out_specs=[_full((rows, c))] * 3,
        out_shape=[jax.ShapeDtypeStruct((rows, c), F32)] * 3,
    )(w, g, m, v)


WEIGHTS = ["mix_norm_pre", "mix_norm_post", "ffn_norm_pre", "ffn_norm_post", "ev_w_in", "ev_a_conv_w", "ev_a_conv_b",
           "ev_a_ln_g", "ev_a_ln_b", "ev_sinks", "ev_w_out", "od_w_in", "od_conv_w", "od_w_out", "ffn_w_up",
           "ffn_conv_w", "ffn_w_down"]
BIG = ["ev_w_in", "ev_w_out", "od_w_in", "od_w_out", "ffn_w_up", "ffn_w_down"]
SMALL_REPL = ["mix_norm_pre", "mix_norm_post", "ffn_norm_pre", "ffn_norm_post", "ev_a_conv_b", "ev_a_ln_g",
              "ev_a_ln_b", "ev_sinks"]
SMALL_SHARDED = ["ev_a_conv_w", "od_conv_w", "ffn_conv_w"]


def _pack(arrs, rows):
    flat = jnp.concatenate([a.reshape(-1) for a in arrs])
    return jnp.pad(flat, (0, rows * LANES - flat.shape[0])).reshape(rows, LANES)


def _unpack(packed, shapes):
    flat, out, off = packed.reshape(-1), [], 0
    for s in shapes:
        n = 1
        for d in s:
            n *= d
        out.append(flat[off:off + n].reshape(s))
        off += n
    return out


def kernel(x, positions, mix_norm_pre, mix_norm_post, ffn_norm_pre, ffn_norm_post, ev_w_in, ev_a_conv_w, ev_a_conv_b, ev_a_ln_g, ev_a_ln_b, ev_sinks, ev_w_out, od_w_in, od_conv_w, od_w_out, ffn_w_up, ffn_conv_w, ffn_w_down, loss_target, m_mix_norm_pre, m_mix_norm_post, m_ffn_norm_pre, m_ffn_norm_post, m_ev_w_in, m_ev_a_conv_w, m_ev_a_conv_b, m_ev_a_ln_g, m_ev_a_ln_b, m_ev_sinks, m_ev_w_out, m_od_w_in, m_od_conv_w, m_od_w_out, m_ffn_w_up, m_ffn_conv_w, m_ffn_w_down, v_mix_norm_pre, v_mix_norm_post, v_ffn_norm_pre, v_ffn_norm_post, v_ev_w_in, v_ev_a_conv_w, v_ev_a_conv_b, v_ev_a_ln_g, v_ev_a_ln_b, v_ev_sinks, v_ev_w_out, v_od_w_in, v_od_conv_w, v_od_w_out, v_ffn_w_up, v_ffn_conv_w, v_ffn_w_down):
    w = dict(zip(WEIGHTS, (mix_norm_pre, mix_norm_post, ffn_norm_pre, ffn_norm_post, ev_w_in, ev_a_conv_w, ev_a_conv_b,
                           ev_a_ln_g, ev_a_ln_b, ev_sinks, ev_w_out, od_w_in, od_conv_w, od_w_out, ffn_w_up, ffn_conv_w,
                           ffn_w_down)))
    mom = dict(zip(WEIGHTS, (m_mix_norm_pre, m_mix_norm_post, m_ffn_norm_pre, m_ffn_norm_post, m_ev_w_in, m_ev_a_conv_w,
                             m_ev_a_conv_b, m_ev_a_ln_g, m_ev_a_ln_b, m_ev_sinks, m_ev_w_out, m_od_w_in, m_od_conv_w,
                             m_od_w_out, m_ffn_w_up, m_ffn_conv_w, m_ffn_w_down)))
    var = dict(zip(WEIGHTS, (v_mix_norm_pre, v_mix_norm_post, v_ffn_norm_pre, v_ffn_norm_post, v_ev_w_in, v_ev_a_conv_w,
                             v_ev_a_conv_b, v_ev_a_ln_g, v_ev_a_ln_b, v_ev_sinks, v_ev_w_out, v_od_w_in, v_od_conv_w,
                             v_od_w_out, v_ffn_w_up, v_ffn_conv_w, v_ffn_w_down)))
    ix, iy, ic = lax.axis_index("x"), lax.axis_index("y"), lax.axis_index("c")
    dev = 4 * ix + 2 * iy + ic
    two = lambda a: a.reshape(-1, a.shape[-1])

    dev1 = jnp.reshape(dev, (1,)).astype(jnp.int32)
    shard = {n: two(w[n].astype(BF16)) for n in BIG}
    gathered = lambda n, a: a.reshape((N_DEV,) + w[n].shape)
    ev_names = [n for n in BIG if n.startswith("ev_")]
    ev_gat = _all_gather([shard[n] for n in ev_names] + [_pack([w[n] for n in SMALL_SHARDED], 8)], "gather_ev")
    p = _prep_ev({n: gathered(n, a) for n, a in zip(ev_names, ev_gat)})
    rest_names = [n for n in BIG if not n.startswith("ev_")]
    first = shard[rest_names[0]] + (ev_gat[0][0:1, 0:1] * 0).astype(BF16)
    started = _exchange_start([first] + [shard[n] for n in rest_names[1:]], False, "gather_start")

    def rest_weights(after):
        lands = _exchange_wait(started, False, after, "gather_wait")
        return _prep_rest({n: gathered(n, a) for n, a in zip(rest_names, lands)})

    small = {n: w[n] for n in SMALL_REPL}
    small_shapes = [w[n].shape for n in SMALL_SHARDED]
    conv_gat = ev_gat[len(ev_names)].reshape(N_DEV, 8, LANES)
    per_dev = [_unpack(conv_gat[d], small_shapes) for d in range(N_DEV)]
    for k, n in enumerate(SMALL_SHARDED):
        small[n] = jnp.concatenate([per_dev[d][k] for d in range(N_DEV)], axis=-1)

    exchanges = []

    def grads_ready(names, big):
        blocks = lambda a, n: a.reshape(N_DEV, -1, w[n].shape[-1])
        bufs = [blocks(big[n], n) for n in names]
        payload = [blocks(big[n + ":bf16"], n) if n + ":bf16" in big else _to_bf16(b) for n, b in zip(names, bufs)]
        st = _exchange_start(payload, True, "grads_start_" + names[0])
        exchanges.append((names, bufs, st))
        return st[-1][0, 0]

    lpart, grad_x, big, g = _local_step(x[0], positions[0], loss_target[0], p, rest_weights, small, started[-1][0, 0],
                                        grads_ready)
    loss = lax.psum(lpart, ("x", "y", "c"))

    grads, delta, new_m, new_v = {}, {}, {}, {}
    for names, bufs, st in exchanges:
        lands = _exchange_wait(st, True, grad_x, "grads_wait_" + names[0])
        for n, b, land in zip(names, bufs, lands):
            outs = _adamw_rs(b, land, two(w[n]), two(mom[n]), two(var[n]), dev1)
            grads[n], delta[n], new_m[n], new_v[n] = (a.reshape(w[n].shape) for a in outs)

    small_names = SMALL_REPL + SMALL_SHARDED
    s_all = _sum_blocks(_all_gather([_pack([g[n] for n in small_names], 64)], "gather_small_grads")[0], N_DEV)
    for n, a in zip(small_names, _unpack(s_all, [small[n].shape for n in small_names])):
        if n in SMALL_SHARDED:
            width = w[n].shape[-1]
            a = lax.dynamic_slice_in_dim(a, dev * width, width, axis=a.ndim - 1)
        grads[n] = a
    pk = lambda dct: _pack([dct[n] for n in small_names], 16)
    outs = _adamw(pk(w), pk(grads), pk(mom), pk(var))
    for dst, packed in zip((delta, new_m, new_v), outs):
        for n, a in zip(small_names, _unpack(packed, [w[n].shape for n in small_names])):
            dst[n] = a

    return (loss, grad_x[None], *[grads[n] for n in WEIGHTS], *[delta[n] for n in WEIGHTS],
            *[new_m[n] for n in WEIGHTS], *[new_v[n] for n in WEIGHTS])
```

```python
import jax
import jax.numpy as jnp
from jax import lax
from jax.experimental import pallas as pl
from jax.experimental.pallas import tpu as pltpu

F32, BF16 = jnp.float32, jnp.bfloat16

D_MODEL = 1024
A_CH = 512
A_CONV = 31
Q_DIM = 512
KV_DIM = 128
HEAD_DIM = 64
N_Q_HEADS = 8
N_KV_HEADS = 2
GROUP = 4
BLOCK = 128
EVEN_IN = 1792
SC_DIM = 1024
D_FF = 2816
ROPE_THETA = 500000.0
ROPE_DIM = 16
RMS_EPS = 1e-6
LN_EPS = 1e-5
SCALE = HEAD_DIM ** -0.5
NEG = -1e30

ADAM_LR, ADAM_B1, ADAM_B2, ADAM_EPS, ADAM_WD, ADAM_STEP = 0.001, 0.9, 0.999, 1e-08, 0.01, 10

N_DEV = 8
FF_N = 2 * D_FF // N_DEV
LANES = 1024
HALO3 = 8
HALO31 = 32
VMEM_LIMIT = 56 * 1024 * 1024

TM = 512
TM_BWD = 256
TK_DW = 4096
ATT_NB = 4
SUB = 128

_ANY = pl.BlockSpec(memory_space=pl.ANY)
_CONTRACT_LAST = (((1,), (1,)), ((), ()))
_CONTRACT_FIRST = (((0,), (0,)), ((), ()))


def _cp(sem, vmem=VMEM_LIMIT):
    return pltpu.CompilerParams(dimension_semantics=sem, vmem_limit_bytes=vmem)


def _full(shape):
    n = len(shape)
    return pl.BlockSpec(shape, lambda *_: (0,) * n)


def _rows(tm, n):
    return pl.BlockSpec((tm, n), lambda i, *_: (i, 0))


def _sigmoid(x):
    return 0.5 * jnp.tanh(0.5 * x) + 0.5


def _rsqrt_mean(x):
    return lax.rsqrt(jnp.mean(x * x, axis=-1, keepdims=True) + RMS_EPS)


def _rms_bwd(x, g, dy):
    r = _rsqrt_mean(x)
    xh = x * r
    dxh = dy * g
    dx = r * (dxh - xh * jnp.mean(dxh * xh, axis=-1, keepdims=True))
    return dx, jnp.sum(dy * xh, axis=0, keepdims=True)


def _acc_out(ref, first, val):
    @pl.when(first)
    def _():
        ref[...] = val

    @pl.when(jnp.logical_not(first))
    def _():
        ref[...] += val


def _rope_tables(positions):
    half = ROPE_DIM // 2
    inv_freq = ROPE_THETA ** (-(jnp.arange(half, dtype=F32) * 2.0 / ROPE_DIM))
    ang = positions.astype(F32)[:, None] * inv_freq
    cos, sin = jnp.cos(ang), jnp.sin(ang)
    t = positions.shape[0]
    one, zero = jnp.ones((t, HEAD_DIM - ROPE_DIM), F32), jnp.zeros((t, HEAD_DIM - ROPE_DIM), F32)
    z8 = jnp.zeros((t, half), F32)
    c = jnp.concatenate([cos, cos, one], axis=1)
    sa = jnp.concatenate([z8, sin, zero], axis=1)
    sb = jnp.concatenate([-sin, z8, zero], axis=1)
    return tuple(jnp.tile(a, (1, 2)) for a in (c, sa, sb))


def _rope(t, c, sa, sb):
    return t * c + pltpu.roll(t, 8, 1) * sa + pltpu.roll(t, 120, 1) * sb


def _rope_bwd(d, c, sa, sb):
    return d * c + pltpu.roll(d * sa, 120, 1) + pltpu.roll(d * sb, 8, 1)


def _ev_in(x, gpre, w_in, rc, rsa, rsb):
    t = x.shape[0]
    tm = min(TM, t)

    def body(x_ref, g_ref, w_ref, c_ref, sa_ref, sb_ref, h_ref, zag_ref, q_ref, k_ref, v_ref):
        xv = x_ref[...]
        h = (xv * _rsqrt_mean(xv) * g_ref[...]).astype(BF16)
        h_ref[...] = h
        z = jnp.dot(h, w_ref[...], preferred_element_type=F32)
        zag_ref[...] = z[:, :2 * A_CH].astype(BF16)
        c, sa, sb = c_ref[...], sa_ref[...], sb_ref[...]
        q0 = 2 * A_CH
        for j in range(Q_DIM // 128):
            q_ref[:, 128 * j:128 * (j + 1)] = _rope(z[:, q0 + 128 * j:q0 + 128 * (j + 1)], c, sa, sb).astype(BF16)
        k0 = q0 + Q_DIM
        k_ref[...] = _rope(z[:, k0:k0 + KV_DIM], c, sa, sb).astype(BF16)
        v_ref[...] = z[:, k0 + KV_DIM:k0 + 2 * KV_DIM].astype(BF16)

    return pl.pallas_call(
        body, name="ev_in", grid=(t // tm,),
        in_specs=[_rows(tm, D_MODEL), _full((1, D_MODEL)), _full((D_MODEL, EVEN_IN)),
                  _rows(tm, 128), _rows(tm, 128), _rows(tm, 128)],
        out_specs=[_rows(tm, D_MODEL), _rows(tm, 2 * A_CH), _rows(tm, Q_DIM), _rows(tm, KV_DIM), _rows(tm, KV_DIM)],
        out_shape=[jax.ShapeDtypeStruct((t, D_MODEL), BF16), jax.ShapeDtypeStruct((t, 2 * A_CH), BF16),
                   jax.ShapeDtypeStruct((t, Q_DIM), BF16), jax.ShapeDtypeStruct((t, KV_DIM), BF16),
                   jax.ShapeDtypeStruct((t, KV_DIM), BF16)],
        compiler_params=_cp(("arbitrary",)),
    )(x, gpre, w_in, rc, rsa, rsb)


def _glu(zag):
    z = zag.astype(F32)
    return z[:, :A_CH] * _sigmoid(z[:, A_CH:])


def _tap_copies(ext, cbuf, first_row, rows):
    for b in range(1, 8):
        s = first_row(b)
        cbuf[b - 1] = ext[s:s + rows, :]


def _conf_fwd(zag, conv_w, conv_b, ln_g, ln_b):
    t = zag.shape[0]
    tm = min(TM_BWD, t)
    rows = tm + HALO31 - 8

    def body(z_ref, w_ref, b_ref, g_ref, lb_ref, c_ref, a_ref, ext, cbuf):
        i = pl.program_id(0)

        @pl.when(i == 0)
        def _():
            ext[0:HALO31, :] = jnp.zeros((HALO31, A_CH), F32)

        ext[HALO31:HALO31 + tm, :] = _glu(z_ref[...])
        _tap_copies(ext, cbuf, lambda b: 8 - b, rows)
        for rs in range(0, tm, SUB):
            for cs in range(0, A_CH, 128):
                acc = jnp.zeros((SUB, 128), F32)
                for k in range(A_CONV):
                    lag_a, lag_b = divmod(k, 8)
                    r0 = HALO31 - 8 - 8 * lag_a + rs
                    src = (ext[r0 + 8:r0 + 8 + SUB, cs:cs + 128] if lag_b == 0
                           else cbuf[lag_b - 1, r0:r0 + SUB, cs:cs + 128])
                    acc = acc + w_ref[A_CONV - 1 - k:A_CONV - k, cs:cs + 128] * src
                c_ref[rs:rs + SUB, cs:cs + 128] = acc
        ext[0:HALO31, :] = ext[tm:tm + HALO31, :]
        cv = c_ref[...] + b_ref[...]
        c_ref[...] = cv
        mu = jnp.mean(cv, axis=-1, keepdims=True)
        xc = cv - mu
        ln = xc * lax.rsqrt(jnp.mean(xc * xc, axis=-1, keepdims=True) + LN_EPS) * g_ref[...] + lb_ref[...]
        a_ref[...] = (ln * _sigmoid(ln)).astype(BF16)

    return pl.pallas_call(
        body, name="conf_fwd", grid=(t // tm,),
        in_specs=[_rows(tm, 2 * A_CH), _full((32, A_CH)), _full((1, A_CH)), _full((1, A_CH)), _full((1, A_CH))],
        out_specs=[_rows(tm, A_CH), _rows(tm, A_CH)],
        out_shape=[jax.ShapeDtypeStruct((t, A_CH), F32), jax.ShapeDtypeStruct((t, A_CH), BF16)],
        scratch_shapes=[pltpu.VMEM((HALO31 + tm, A_CH), F32), pltpu.VMEM((7, rows, A_CH), F32)],
        compiler_params=_cp(("arbitrary",)),
    )(zag, conv_w, conv_b, ln_g, ln_b)


def _attn_mask(first_block):
    row = lax.broadcasted_iota(jnp.int32, (GROUP * BLOCK, 2 * BLOCK), 0) & (BLOCK - 1)
    col = lax.broadcasted_iota(jnp.int32, (GROUP * BLOCK, 2 * BLOCK), 1)
    diff = row + BLOCK - col
    return (diff >= 0) & (diff < BLOCK) & ((col >= BLOCK) | jnp.logical_not(first_block))


def _sink_rows(s_ref, h):
    grp = lax.broadcasted_iota(jnp.int32, (GROUP * BLOCK, 1), 0) >> 7
    out = jnp.full((GROUP * BLOCK, 1), s_ref[GROUP * h], F32)
    for g in range(1, GROUP):
        out = jnp.where(grp == g, s_ref[GROUP * h + g], out)
    return out


def _attn_probs(q4, k2, mask, sink):
    s = lax.dot_general(q4, k2, _CONTRACT_LAST, preferred_element_type=F32) * SCALE
    s = jnp.where(mask, s, NEG)
    m = jnp.maximum(jnp.max(s, axis=-1, keepdims=True), sink)
    p = jnp.exp(s - m)
    es = jnp.exp(sink - m)
    inv = 1.0 / (jnp.sum(p, axis=-1, keepdims=True) + es)
    return p * inv, es * inv


def _q_heads(q, h):
    return jnp.concatenate([q[:, HEAD_DIM * (GROUP * h + g):HEAD_DIM * (GROUP * h + g + 1)] for g in range(GROUP)],
                           axis=0)


def _kv_head(prev, cur, h):
    return jnp.concatenate([prev[:, HEAD_DIM * h:HEAD_DIM * (h + 1)], cur[:, HEAD_DIM * h:HEAD_DIM * (h + 1)]], axis=0)


def _attn_fwd(q, k, v, sinks):
    t = q.shape[0]
    nb = min(ATT_NB, t // BLOCK)
    rows = nb * BLOCK

    def body(s_ref, q_ref, kc_ref, kp_ref, vc_ref, vp_ref, o_ref):
        first = pl.program_id(0) == 0
        for b in range(nb):
            lo = BLOCK * b
            mask = _attn_mask(first) if b == 0 else _attn_mask(False)
            qv, kc, vc = q_ref[lo:lo + BLOCK, :], kc_ref[lo:lo + BLOCK, :], vc_ref[lo:lo + BLOCK, :]
            kp = kp_ref[...] if b == 0 else kc_ref[lo - BLOCK:lo, :]
            vp = vp_ref[...] if b == 0 else vc_ref[lo - BLOCK:lo, :]
            for h in range(N_KV_HEADS):
                pn, _ = _attn_probs(_q_heads(qv, h), _kv_head(kp, kc, h), mask, _sink_rows(s_ref, h))
                o4 = jnp.dot(pn.astype(BF16), _kv_head(vp, vc, h), preferred_element_type=F32).astype(BF16)
                for g in range(GROUP):
                    c0 = HEAD_DIM * (GROUP * h + g)
                    o_ref[lo:lo + BLOCK, c0:c0 + HEAD_DIM] = o4[BLOCK * g:BLOCK * (g + 1), :]

    cur = lambda n: pl.BlockSpec((rows, n), lambda i: (i, 0))
    prev = lambda n: pl.BlockSpec((BLOCK, n), lambda i: (jnp.maximum(i * nb - 1, 0), 0))
    return pl.pallas_call(
        body, name="attn_fwd", grid=(t // rows,),
        in_specs=[pl.BlockSpec(memory_space=pltpu.SMEM), cur(Q_DIM), cur(KV_DIM), prev(KV_DIM), cur(KV_DIM),
                  prev(KV_DIM)],
        out_specs=cur(Q_DIM),
        out_shape=jax.ShapeDtypeStruct((t, Q_DIM), BF16),
        compiler_params=_cp(("arbitrary",)),
    )(sinks, q, k, k, v, v)


def _out_post(lhs, ws, x_in, gpost):
    t = x_in.shape[0]
    tm = min(TM, t)
    n = len(lhs)

    def body(*refs):
        x_ref, g_ref, m_ref, xo_ref = refs[2 * n:]
        m = jnp.dot(refs[0][...], refs[n][...], preferred_element_type=F32)
        for j in range(1, n):
            m = m + jnp.dot(refs[j][...], refs[n + j][...], preferred_element_type=F32)
        m_ref[...] = m.astype(BF16)
        xo_ref[...] = x_ref[...] + m * _rsqrt_mean(m) * g_ref[...]

    return pl.pallas_call(
        body, name="out_post", grid=(t // tm,),
        in_specs=[_rows(tm, a.shape[1]) for a in lhs] + [_full(w.shape) for w in ws]
                 + [_rows(tm, D_MODEL), _full((1, D_MODEL))],
        out_specs=[_rows(tm, D_MODEL), _rows(tm, D_MODEL)],
        out_shape=[jax.ShapeDtypeStruct((t, D_MODEL), BF16), jax.ShapeDtypeStruct((t, D_MODEL), F32)],
        compiler_params=_cp(("arbitrary",)),
    )(*lhs, *ws, x_in, gpost)


def _conv3(w_ref, ext, tm):
    s = HALO3 - 2
    return (w_ref[0:1, :] * ext[s:s + tm, :] + w_ref[1:2, :] * ext[s + 1:s + 1 + tm, :]
            + w_ref[2:3, :] * ext[s + 2:s + 2 + tm, :])


def _ffn_fwd(x1, gpre, wup, layer, cw, wd, gpost):
    t = x1.shape[0]
    tm = min(TM, t)
    nc, n = wup.shape[1], wup.shape[4]

    def body(x_ref, gpre_ref, wup_ref, cw_ref, wd_ref, gpost_ref, h_ref, up_ref, u_ref, f_ref, xo_ref, h_s, acc, ext, hal):
        i, c = pl.program_id(0), pl.program_id(1)

        @pl.when(c == 0)
        def _():
            xv = x_ref[...]
            h = (xv * _rsqrt_mean(xv) * gpre_ref[...]).astype(BF16)
            h_s[...] = h
            h_ref[...] = h

        @pl.when(i == 0)
        def _():
            hal[c] = jnp.zeros((2, HALO3, n), F32)

        u = []
        for gv in range(2):
            up = jnp.dot(h_s[...], wup_ref[gv, 0, 0], preferred_element_type=F32)
            up_ref[gv, 0] = up.astype(BF16)
            ext[gv, 0:HALO3, :] = hal[c, gv]
            ext[gv, HALO3:HALO3 + tm, :] = up
            hal[c, gv] = ext[gv, tm:tm + HALO3, :]
            s = HALO3 - 2
            u.append(cw_ref[gv, 0, 0:1, :] * ext[gv, s:s + tm, :] + cw_ref[gv, 0, 1:2, :] * ext[gv, s + 1:s + 1 + tm, :]
                     + cw_ref[gv, 0, 2:3, :] * up)
            u_ref[gv, 0] = u[gv].astype(BF16)
        act = (u[0] * _sigmoid(u[0]) * u[1]).astype(BF16)
        part = jnp.dot(act, wd_ref[...], preferred_element_type=F32)

        @pl.when(c == 0)
        def _():
            acc[...] = part

        @pl.when(jnp.logical_and(c > 0, c < nc - 1))
        def _():
            acc[...] += part

        @pl.when(c == nc - 1)
        def _():
            f = acc[...] + part
            f_ref[...] = f
            xo_ref[...] = x_ref[...] + f * _rsqrt_mean(f) * gpost_ref[...]

    row = lambda w: pl.BlockSpec((tm, w), lambda i, c: (i, 0))
    one = _full((1, D_MODEL))
    return pl.pallas_call(
        body, name="ffn_fwd", grid=(t // tm, nc),
        in_specs=[row(D_MODEL), one, pl.BlockSpec((2, 1, 1, D_MODEL, n), lambda i, c: (0, c, layer, 0, 0)),
                  pl.BlockSpec((2, 1, 3, n), lambda i, c: (0, c, 0, 0)), pl.BlockSpec((n, D_MODEL), lambda i, c: (c, 0)),
                  one],
        out_specs=[row(D_MODEL), pl.BlockSpec((2, 1, tm, n), lambda i, c: (0, c, i, 0)),
                   pl.BlockSpec((2, 1, tm, n), lambda i, c: (0, c, i, 0)), row(D_MODEL), row(D_MODEL)],
        out_shape=[jax.ShapeDtypeStruct((t, D_MODEL), BF16), jax.ShapeDtypeStruct((2, nc, t, n), BF16),
                   jax.ShapeDtypeStruct((2, nc, t, n), BF16), jax.ShapeDtypeStruct((t, D_MODEL), F32),
                   jax.ShapeDtypeStruct((t, D_MODEL), F32)],
        scratch_shapes=[pltpu.VMEM((tm, D_MODEL), BF16), pltpu.VMEM((tm, D_MODEL), F32),
                        pltpu.VMEM((2, HALO3 + tm, n), F32), pltpu.VMEM((nc, 2, HALO3, n), F32)],
        compiler_params=_cp(("arbitrary", "arbitrary")),
    )(x1, gpre, wup, cw, wd, gpost)


def _od_fwd(x_in, gpre, w_in, cw, w_out, gpost):
    t = x_in.shape[0]
    tm = min(TM, t)
    ns, _, n = w_in.shape

    def body(x_ref, gpre_ref, w_ref, cw_ref, wo_ref, gpost_ref, h_ref, z_ref, cv_ref, y_ref, m_ref, xo_ref, z_s, ext):
        i = pl.program_id(0)
        xv = x_ref[...]
        h = (xv * _rsqrt_mean(xv) * gpre_ref[...]).astype(BF16)
        h_ref[...] = h
        for j in range(ns):
            z_s[:, n * j:n * (j + 1)] = jnp.dot(h, w_ref[j], preferred_element_type=F32)
        z_ref[...] = z_s[...].astype(BF16)

        @pl.when(i == 0)
        def _():
            ext[0:HALO3, :] = jnp.zeros((HALO3, SC_DIM), F32)

        ext[HALO3:HALO3 + tm, :] = z_s[:, SC_DIM:2 * SC_DIM] * z_s[:, 2 * SC_DIM:]
        cv = _conv3(cw_ref, ext, tm)
        cv_ref[...] = cv.astype(BF16)
        y = (z_s[:, :SC_DIM] * cv).astype(BF16)
        ext[0:HALO3, :] = ext[tm:tm + HALO3, :]
        y_ref[...] = y
        m = jnp.dot(y, wo_ref[...], preferred_element_type=F32)
        m_ref[...] = m
        xo_ref[...] = xv + m * _rsqrt_mean(m) * gpost_ref[...]

    return pl.pallas_call(
        body, name="od_fwd", grid=(t // tm,),
        in_specs=[_rows(tm, D_MODEL), _full((1, D_MODEL)), _full((ns, D_MODEL, n)), _full((3, SC_DIM)),
                  _full((SC_DIM, D_MODEL)), _full((1, D_MODEL))],
        out_specs=[_rows(tm, D_MODEL), _rows(tm, 3 * SC_DIM), _rows(tm, SC_DIM), _rows(tm, SC_DIM), _rows(tm, D_MODEL),
                   _rows(tm, D_MODEL)],
        out_shape=[jax.ShapeDtypeStruct((t, D_MODEL), BF16), jax.ShapeDtypeStruct((t, 3 * SC_DIM), BF16),
                   jax.ShapeDtypeStruct((t, SC_DIM), BF16), jax.ShapeDtypeStruct((t, SC_DIM), BF16),
                   jax.ShapeDtypeStruct((t, D_MODEL), F32), jax.ShapeDtypeStruct((t, D_MODEL), F32)],
        scratch_shapes=[pltpu.VMEM((tm, 3 * SC_DIM), F32), pltpu.VMEM((HALO3 + tm, SC_DIM), F32)],
        compiler_params=_cp(("arbitrary",)),
    )(x_in, gpre, w_in, cw, w_out, gpost)


def _dw2d(a, b, bm, bn):
    t, m = a.shape
    n = b.shape[1]
    tk = min(TK_DW, t)

    def body(a_ref, b_ref, o_ref):
        part = lax.dot_general(a_ref[...], b_ref[...], _CONTRACT_FIRST, preferred_element_type=F32)
        _acc_out(o_ref, pl.program_id(2) == 0, part)

    return pl.pallas_call(
        body, name="dw2d", grid=(m // bm, n // bn, t // tk),
        in_specs=[pl.BlockSpec((tk, bm), lambda i, j, k: (k, i)), pl.BlockSpec((tk, bn), lambda i, j, k: (k, j))],
        out_specs=pl.BlockSpec((bm, bn), lambda i, j, k: (i, j)),
        out_shape=jax.ShapeDtypeStruct((m, n), F32),
        compiler_params=_cp(("arbitrary", "arbitrary", "arbitrary")),
    )(a, b)


def _dw_cols(a, b, n_blk):
    t, m = a.shape
    s = b.shape[1] // n_blk
    tk = min(TK_DW, t)
    nk = t // tk

    def body(a_ref, b_ref, o_ref, ob_ref):
        part = lax.dot_general(a_ref[...], b_ref[...], _CONTRACT_FIRST, preferred_element_type=F32)
        _acc_out(o_ref.at[0], pl.program_id(1) == 0, part)

        @pl.when(pl.program_id(1) == nk - 1)
        def _():
            ob_ref[...] = o_ref[...].astype(BF16)

    spec = pl.BlockSpec((1, m, n_blk), lambda j, k: (j, 0, 0))
    return pl.pallas_call(
        body, name="dw_cols", grid=(s, nk),
        in_specs=[pl.BlockSpec((tk, m), lambda j, k: (k, 0)), pl.BlockSpec((tk, n_blk), lambda j, k: (k, j))],
        out_specs=[spec, spec],
        out_shape=[jax.ShapeDtypeStruct((s, m, n_blk), F32), jax.ShapeDtypeStruct((s, m, n_blk), BF16)],
        compiler_params=_cp(("arbitrary", "arbitrary")),
    )(a, b)


def _dw_up(h, dup, layer, buf):
    t, m = h.shape
    s, _, n = dup.shape
    tk = min(TK_DW, t)
    nk = t // tk

    def body(*refs):
        a_ref, b_ref, o_ref, ob_ref = refs[0], refs[1], refs[-2], refs[-1]
        part = lax.dot_general(a_ref[...], b_ref[0], _CONTRACT_FIRST, preferred_element_type=F32)
        _acc_out(o_ref.at[0, 0], pl.program_id(1) == 0, part)

        @pl.when(pl.program_id(1) == nk - 1)
        def _():
            ob_ref[...] = o_ref[...].astype(BF16)

    spec = pl.BlockSpec((1, 1, m, n), lambda j, k: (j, layer, 0, 0))
    return pl.pallas_call(
        body, name="dw_up", grid=(s, nk),
        in_specs=[pl.BlockSpec((tk, m), lambda j, k: (k, 0)), pl.BlockSpec((1, tk, n), lambda j, k: (j, k, 0))]
                 + ([] if buf is None else [_ANY, _ANY]),
        out_specs=[spec, spec],
        out_shape=[jax.ShapeDtypeStruct((s, 2, m, n), F32), jax.ShapeDtypeStruct((s, 2, m, n), BF16)],
        input_output_aliases={} if buf is None else {2: 0, 3: 1},
        compiler_params=_cp(("arbitrary", "arbitrary")),
    )(h, dup, *([] if buf is None else buf))


def _dw_down(act, df, layer, buf):
    nc, t, n = act.shape
    d = df.shape[1]
    tk = min(TK_DW, t)
    nk = t // tk

    def body(*refs):
        a_ref, b_ref, o_ref, ob_ref = refs[0], refs[1], refs[-2], refs[-1]
        part = lax.dot_general(a_ref[0], b_ref[...], _CONTRACT_FIRST, preferred_element_type=F32)
        part = part.reshape(2, n // 2, d)
        first = pl.program_id(1) == 0

        @pl.when(first)
        def _():
            o_ref[:, 0] = part

        @pl.when(jnp.logical_not(first))
        def _():
            o_ref[:, 0] += part

        @pl.when(pl.program_id(1) == nk - 1)
        def _():
            ob_ref[...] = o_ref[...].astype(BF16)

    spec = pl.BlockSpec((2, 1, n // 2, d), lambda c, k: (c, layer, 0, 0))
    return pl.pallas_call(
        body, name="dw_down", grid=(nc, nk),
        in_specs=[pl.BlockSpec((1, tk, n), lambda c, k: (c, k, 0)), pl.BlockSpec((tk, d), lambda c, k: (k, 0))]
                 + ([] if buf is None else [_ANY, _ANY]),
        out_specs=[spec, spec],
        out_shape=[jax.ShapeDtypeStruct((2 * nc, 2, n // 2, d), F32), jax.ShapeDtypeStruct((2 * nc, 2, n // 2, d), BF16)],
        input_output_aliases={} if buf is None else {2: 0, 3: 1},
        compiler_params=_cp(("arbitrary", "arbitrary")),
    )(act, df, *([] if buf is None else buf))


def _dz_wt_rms_bwd(dz, w, x_in, gpre, dres):
    t, n = dz.shape
    tm = min(TM, t)

    def body(dz_ref, wt_ref, x_ref, g_ref, dres_ref, dx_ref, dg_ref):
        dh = lax.dot_general(dz_ref[...], wt_ref[...], _CONTRACT_LAST, preferred_element_type=F32)
        dx, dg = _rms_bwd(x_ref[...], g_ref[...], dh)
        dx_ref[...] = dres_ref[...] + dx
        _acc_out(dg_ref, pl.program_id(0) == 0, dg)

    return pl.pallas_call(
        body, name="dz_wt_rms_bwd", grid=(t // tm,),
        in_specs=[_rows(tm, n), _full((D_MODEL, n)), _rows(tm, D_MODEL), _full((1, D_MODEL)), _rows(tm, D_MODEL)],
        out_specs=[_rows(tm, D_MODEL), _full((1, D_MODEL))],
        out_shape=[jax.ShapeDtypeStruct((t, D_MODEL), F32), jax.ShapeDtypeStruct((1, D_MODEL), F32)],
        compiler_params=_cp(("arbitrary",)),
    )(dz, w, x_in, gpre, dres)


def _shift_matrices(shift, shift_h, tm, hb):
    row = lax.broadcasted_iota(jnp.int32, (2 * tm, tm), 0)
    col = lax.broadcasted_iota(jnp.int32, (2 * tm, tm), 1)
    hit = ((row < tm) & (col == row + 1)) | ((row >= tm) & (col == row - tm + 2))
    shift[...] = jnp.where(hit, 1.0, 0.0).astype(BF16)
    row = lax.broadcasted_iota(jnp.int32, (hb, hb), 0)
    col = lax.broadcasted_iota(jnp.int32, (hb, hb), 1)
    hit = ((row < HALO3) & (col == row - (HALO3 - 1))) | ((row >= HALO3) & (col == row - (2 * HALO3 - 2)))
    shift_h[...] = jnp.where(hit, 1.0, 0.0).astype(BF16)


def _next_rows(shift, shift_h, xb, nxt, d12_s, tm):
    d12_s[...] = jnp.dot(shift[...], xb, preferred_element_type=F32)
    edge = jnp.dot(shift_h[...], nxt, preferred_element_type=F32)
    d12_s[tm - HALO3:tm, :] += edge[0:HALO3, :]
    d12_s[2 * tm - HALO3:2 * tm, :] += edge[HALO3:2 * HALO3, :]


def _ffn_bwd(f, dxo, gpost, x_in, gpre, up, u, cw, wd, wup, layer, target=None):
    t = f.shape[0]
    tm = min(TM_BWD, t)
    nt = t // tm
    nc, n = up.shape[1], up.shape[3]
    hb = 2 * HALO3

    def body(*refs):
        f_ref, dxo_ref, gpost_ref, x_ref, gpre_ref, up_ref, u_ref, cw_ref, wd_ref, wup_ref = refs[:10]
        n_in = 10 if target is None else 11
        n_out = 7 if target is None else 8
        df_ref, act_ref, dup_ref, dx_ref, dgpost_ref, dgpre_ref, dcw_ref = refs[n_in:n_in + 7]
        df_s, acc, du_s, dub_s, d12_s, hal, shift, shift_h = refs[n_in + n_out:]
        i, c = pl.program_id(0), pl.program_id(1)

        def incoming():
            if target is None:
                return dxo_ref[...]
            return (dxo_ref[...] - refs[10][...]) * (1.0 / D_MODEL)

        @pl.when(c == 0)
        def _():
            dy = incoming()
            df, dg = _rms_bwd(f_ref[...], gpost_ref[...], dy)
            df_s[...] = df.astype(BF16)
            df_ref[...] = df.astype(BF16)
            _acc_out(dgpost_ref, i == 0, dg)
            if target is not None:
                part = jnp.zeros((1, 128), F32) + jnp.sum(dy * dy) * (0.5 * D_MODEL)
                _acc_out(refs[n_in + 7], i == 0, part)

        @pl.when(i == 0)
        def _():
            hal[c] = jnp.zeros((2, hb, n), BF16)
            dcw_ref[0, c] = jnp.zeros((8, n), F32)
            dcw_ref[1, c] = jnp.zeros((8, n), F32)

        @pl.when(jnp.logical_and(i == 0, c == 0))
        def _():
            _shift_matrices(shift, shift_h, tm, hb)

        dact = lax.dot_general(df_s[...], wd_ref[...], _CONTRACT_LAST, preferred_element_type=F32)
        g, v = u_ref[0, 0].astype(F32), u_ref[1, 0].astype(F32)
        sg = _sigmoid(g)
        sil = g * sg
        act_ref[0] = (sil * v).astype(BF16)
        dug = dact * v * (sg + sil * (1.0 - sg))
        duv = dact * sil
        du_s[0], du_s[1] = dug, duv
        dub_s[0], dub_s[1] = dug.astype(BF16), duv.astype(BF16)
        dh = None
        for gv in range(2):
            _next_rows(shift, shift_h, dub_s[gv], hal[c, gv], d12_s, tm)
            hal[c, gv] = dub_s[gv, 0:hb, :]
            du, d1, d2 = du_s[gv], d12_s[0:tm, :], d12_s[tm:2 * tm, :]
            dup = (cw_ref[gv, 0, 2:3, :] * du + cw_ref[gv, 0, 1:2, :] * d1 + cw_ref[gv, 0, 0:1, :] * d2).astype(BF16)
            dup_ref[gv, 0] = dup
            upc = up_ref[gv, 0].astype(F32)
            dcw_ref[gv, c, 2:3, :] += jnp.sum(upc * du, axis=0, keepdims=True)
            dcw_ref[gv, c, 1:2, :] += jnp.sum(upc * d1, axis=0, keepdims=True)
            dcw_ref[gv, c, 0:1, :] += jnp.sum(upc * d2, axis=0, keepdims=True)
            part = lax.dot_general(dup, wup_ref[gv, 0, 0], _CONTRACT_LAST, preferred_element_type=F32)
            dh = part if dh is None else dh + part
        _acc_out(acc, c == 0, dh)

        @pl.when(c == nc - 1)
        def _():
            dx, dg = _rms_bwd(x_ref[...], gpre_ref[...], acc[...])
            dx_ref[...] = incoming() + dx
            _acc_out(dgpre_ref, i == 0, dg)

    rrow = lambda w: pl.BlockSpec((tm, w), lambda i, c: (nt - 1 - i, 0))
    blk = pl.BlockSpec((2, 1, tm, n), lambda i, c: (0, c, nt - 1 - i, 0))
    one = _full((1, D_MODEL))
    return pl.pallas_call(
        body, name="ffn_bwd", grid=(nt, nc),
        in_specs=[rrow(D_MODEL), rrow(D_MODEL), one, rrow(D_MODEL), one, blk, blk,
                  pl.BlockSpec((2, 1, 3, n), lambda i, c: (0, c, 0, 0)),
                  pl.BlockSpec((n, D_MODEL), lambda i, c: (c, 0)),
                  pl.BlockSpec((2, 1, 1, D_MODEL, n), lambda i, c: (0, c, layer, 0, 0))]
                 + ([] if target is None else [rrow(D_MODEL)]),
        out_specs=[rrow(D_MODEL), pl.BlockSpec((1, tm, n), lambda i, c: (c, nt - 1 - i, 0)), blk, rrow(D_MODEL),
                   one, one, _full((2, nc, 8, n))] + ([] if target is None else [_full((1, 128))]),
        out_shape=[jax.ShapeDtypeStruct((t, D_MODEL), BF16), jax.ShapeDtypeStruct((nc, t, n), BF16),
                   jax.ShapeDtypeStruct((2, nc, t, n), BF16), jax.ShapeDtypeStruct((t, D_MODEL), F32),
                   jax.ShapeDtypeStruct((1, D_MODEL), F32), jax.ShapeDtypeStruct((1, D_MODEL), F32),
                   jax.ShapeDtypeStruct((2, nc, 8, n), F32)]
                  + ([] if target is None else [jax.ShapeDtypeStruct((1, 128), F32)]),
        scratch_shapes=[pltpu.VMEM((tm, D_MODEL), BF16), pltpu.VMEM((tm, D_MODEL), F32),
                        pltpu.VMEM((2, tm, n), F32), pltpu.VMEM((2, tm, n), BF16), pltpu.VMEM((2 * tm, n), F32),
                        pltpu.VMEM((nc, 2, hb, n), BF16), pltpu.VMEM((2 * tm, tm), BF16), pltpu.VMEM((hb, hb), BF16)],
        compiler_params=_cp(("arbitrary", "arbitrary")),
    )(f, dxo, gpost, x_in, gpre, up, u, cw, wd, wup, *([] if target is None else [target]))


def _od_bwd(m, dxo, gpost, x_in, gpre, z, cv, cw, w_out, wint):
    t = m.shape[0]
    tm = min(TM_BWD, t)
    nt = t // tm
    hb = 2 * HALO3

    def body(m_ref, dxo_ref, gpost_ref, x_ref, gpre_ref, z_ref, cv_ref, cw_ref, wo_ref, wi_ref,
             dm_ref, dz_ref, dx_ref, dgpost_ref, dgpre_ref, dcw_ref, dcvb_s, d12_s, dz_s, hal, shift, shift_h):
        i = pl.program_id(0)
        dxo = dxo_ref[...]
        dm, dg = _rms_bwd(m_ref[...], gpost_ref[...], dxo)
        dmb = dm.astype(BF16)
        dm_ref[...] = dmb
        _acc_out(dgpost_ref, i == 0, dg)

        @pl.when(i == 0)
        def _():
            hal[...] = jnp.zeros((hb, SC_DIM), BF16)
            dcw_ref[...] = jnp.zeros((8, SC_DIM), F32)
            _shift_matrices(shift, shift_h, tm, hb)

        dy = lax.dot_general(dmb, wo_ref[...], _CONTRACT_LAST, preferred_element_type=F32)
        z = z_ref[...].astype(F32)
        b, cg, u = z[:, :SC_DIM], z[:, SC_DIM:2 * SC_DIM], z[:, 2 * SC_DIM:]
        dz_s[:, 0:SC_DIM] = (dy * cv_ref[...].astype(F32)).astype(BF16)
        dcv = dy * b
        dcvb_s[...] = dcv.astype(BF16)
        _next_rows(shift, shift_h, dcvb_s[...], hal[...], d12_s, tm)
        hal[...] = dcvb_s[0:hb, :]
        d1, d2 = d12_s[0:tm, :], d12_s[tm:2 * tm, :]
        dcu = cw_ref[2:3, :] * dcv + cw_ref[1:2, :] * d1 + cw_ref[0:1, :] * d2
        cu = cg * u
        dcw_ref[2:3, :] += jnp.sum(cu * dcv, axis=0, keepdims=True)
        dcw_ref[1:2, :] += jnp.sum(cu * d1, axis=0, keepdims=True)
        dcw_ref[0:1, :] += jnp.sum(cu * d2, axis=0, keepdims=True)
        dz_s[:, SC_DIM:2 * SC_DIM] = (dcu * u).astype(BF16)
        dz_s[:, 2 * SC_DIM:3 * SC_DIM] = (dcu * cg).astype(BF16)
        dz_ref[...] = dz_s[...]
        dh = jnp.dot(dz_s[...], wi_ref[...], preferred_element_type=F32)
        dx, dg2 = _rms_bwd(x_ref[...], gpre_ref[...], dh)
        dx_ref[...] = dxo + dx
        _acc_out(dgpre_ref, i == 0, dg2)

    rrow = lambda w: pl.BlockSpec((tm, w), lambda i: (nt - 1 - i, 0))
    one = _full((1, D_MODEL))
    return pl.pallas_call(
        body, name="od_bwd", grid=(nt,),
        in_specs=[rrow(D_MODEL), rrow(D_MODEL), one, rrow(D_MODEL), one, rrow(3 * SC_DIM), rrow(SC_DIM),
                  _full((3, SC_DIM)), _full((SC_DIM, D_MODEL)), _full((3 * SC_DIM, D_MODEL))],
        out_specs=[rrow(D_MODEL), rrow(3 * SC_DIM), rrow(D_MODEL), one, one, _full((8, SC_DIM))],
        out_shape=[jax.ShapeDtypeStruct((t, D_MODEL), BF16), jax.ShapeDtypeStruct((t, 3 * SC_DIM), BF16),
                   jax.ShapeDtypeStruct((t, D_MODEL), F32), jax.ShapeDtypeStruct((1, D_MODEL), F32),
                   jax.ShapeDtypeStruct((1, D_MODEL), F32), jax.ShapeDtypeStruct((8, SC_DIM), F32)],
        scratch_shapes=[pltpu.VMEM((tm, SC_DIM), BF16), pltpu.VMEM((2 * tm, SC_DIM), F32),
                        pltpu.VMEM((tm, 3 * SC_DIM), BF16), pltpu.VMEM((hb, SC_DIM), BF16),
                        pltpu.VMEM((2 * tm, tm), BF16), pltpu.VMEM((hb, hb), BF16)],
        compiler_params=_cp(("arbitrary",)),
    )(m, dxo, gpost, x_in, gpre, z, cv, cw, w_out, wint)


def _ev_bwd1(m, dxo, gpost, w_out):
    t = m.shape[0]
    tm = min(TM, t)

    def body(m_ref, dxo_ref, g_ref, wot_ref, dm_ref, da_ref, do_ref, dg_ref):
        dm, dg = _rms_bwd(m_ref[...].astype(F32), g_ref[...], dxo_ref[...])
        dmb = dm.astype(BF16)
        dm_ref[...] = dmb
        _acc_out(dg_ref, pl.program_id(0) == 0, dg)
        dao = lax.dot_general(dmb, wot_ref[...], _CONTRACT_LAST, preferred_element_type=F32)
        da_ref[...] = dao[:, :A_CH]
        do_ref[...] = dao[:, A_CH:].astype(BF16)

    return pl.pallas_call(
        body, name="ev_bwd1", grid=(t // tm,),
        in_specs=[_rows(tm, D_MODEL), _rows(tm, D_MODEL), _full((1, D_MODEL)), _full((A_CH + Q_DIM, D_MODEL))],
        out_specs=[_rows(tm, D_MODEL), _rows(tm, A_CH), _rows(tm, Q_DIM), _full((1, D_MODEL))],
        out_shape=[jax.ShapeDtypeStruct((t, D_MODEL), BF16), jax.ShapeDtypeStruct((t, A_CH), F32),
                   jax.ShapeDtypeStruct((t, Q_DIM), BF16), jax.ShapeDtypeStruct((1, D_MODEL), F32)],
        compiler_params=_cp(("arbitrary",)),
    )(m, dxo, gpost, w_out)


def _conf_bwd(da, cv, zag, conv_w, ln_g, ln_b):
    t = da.shape[0]
    tm = min(TM_BWD, t)
    nt = t // tm
    rows = tm + HALO31 - 8

    def body(da_ref, c_ref, z_ref, w_ref, g_ref, lb_ref, dz_ref, dw_ref, dv_ref, ext_out, cbuf, glu_s, dglu_s):
        i = pl.program_id(0)

        @pl.when(i == 0)
        def _():
            ext_out[tm:tm + HALO31, :] = jnp.zeros((HALO31, A_CH), F32)
            dw_ref[...] = jnp.zeros((32, A_CH), F32)
            dv_ref[...] = jnp.zeros((8, A_CH), F32)

        x = c_ref[...]
        mu = jnp.mean(x, axis=-1, keepdims=True)
        xc = x - mu
        rstd = lax.rsqrt(jnp.mean(xc * xc, axis=-1, keepdims=True) + LN_EPS)
        xh = xc * rstd
        ln = xh * g_ref[...] + lb_ref[...]
        sl = _sigmoid(ln)
        dln = da_ref[...] * (sl * (1.0 + ln * (1.0 - sl)))
        dxh = dln * g_ref[...]
        dc = rstd * (dxh - jnp.mean(dxh, axis=-1, keepdims=True) - xh * jnp.mean(dxh * xh, axis=-1, keepdims=True))
        dv_ref[0:1, :] += jnp.sum(dc, axis=0, keepdims=True)
        dv_ref[1:2, :] += jnp.sum(dln * xh, axis=0, keepdims=True)
        dv_ref[2:3, :] += jnp.sum(dln, axis=0, keepdims=True)

        ext_out[0:tm, :] = dc
        _tap_copies(ext_out, cbuf, lambda b: b, rows)
        z = z_ref[...].astype(F32)
        al, sg = z[:, :A_CH], _sigmoid(z[:, A_CH:])
        glu_s[...] = al * sg
        for rs in range(0, tm, SUB):
            for cs in range(0, A_CH, 128):
                glu = glu_s[rs:rs + SUB, cs:cs + 128]
                acc = jnp.zeros((SUB, 128), F32)
                for k in range(A_CONV):
                    lag_a, lag_b = divmod(k, 8)
                    r0 = 8 * lag_a + rs
                    d = (ext_out[r0:r0 + SUB, cs:cs + 128] if lag_b == 0
                         else cbuf[lag_b - 1, r0:r0 + SUB, cs:cs + 128])
                    j = A_CONV - 1 - k
                    acc = acc + w_ref[j:j + 1, cs:cs + 128] * d
                    dw_ref[j:j + 1, cs:cs + 128] += jnp.sum(glu * d, axis=0, keepdims=True)
                dglu_s[rs:rs + SUB, cs:cs + 128] = acc
        dglu = dglu_s[...]
        ext_out[tm:tm + HALO31, :] = ext_out[0:HALO31, :]
        dz_ref[:, 0:A_CH] = (dglu * sg).astype(BF16)
        dz_ref[:, A_CH:2 * A_CH] = (dglu * al * sg * (1.0 - sg)).astype(BF16)

    rrow = lambda w: pl.BlockSpec((tm, w), lambda i: (nt - 1 - i, 0))
    return pl.pallas_call(
        body, name="conf_bwd", grid=(nt,),
        in_specs=[rrow(A_CH), rrow(A_CH), rrow(2 * A_CH), _full((32, A_CH)), _full((1, A_CH)), _full((1, A_CH))],
        out_specs=[rrow(2 * A_CH), _full((32, A_CH)), _full((8, A_CH))],
        out_shape=[jax.ShapeDtypeStruct((t, 2 * A_CH), BF16), jax.ShapeDtypeStruct((32, A_CH), F32),
                   jax.ShapeDtypeStruct((8, A_CH), F32)],
        scratch_shapes=[pltpu.VMEM((tm + HALO31, A_CH), F32), pltpu.VMEM((7, rows, A_CH), F32),
                        pltpu.VMEM((tm, A_CH), F32), pltpu.VMEM((tm, A_CH), F32)],
        compiler_params=_cp(("arbitrary",)),
    )(da, cv, zag, conv_w, ln_g, ln_b)


def _attn_bwd(q, k, v, do, sinks):
    t = q.shape[0]
    nb = min(ATT_NB, t // BLOCK)
    rows = nb * BLOCK
    ns = t // rows

    def body(s_ref, q_ref, kc_ref, kp_ref, vc_ref, vp_ref, do_ref, dq_ref, dk_ref, dv_ref, ds_ref, dkc, dvc):
        i = pl.program_id(0)
        r = ns - 1 - i

        @pl.when(i == 0)
        def _():
            dkc[...] = jnp.zeros_like(dkc)
            dvc[...] = jnp.zeros_like(dvc)
            ds_ref[...] = jnp.zeros_like(ds_ref)

        lane = lax.broadcasted_iota(jnp.int32, (1, N_Q_HEADS), 1)
        dsv = jnp.zeros((1, N_Q_HEADS), F32)
        for b in range(nb - 1, -1, -1):
            lo = BLOCK * b
            mask = _attn_mask(r == 0) if b == 0 else _attn_mask(False)
            qv, dov = q_ref[lo:lo + BLOCK, :], do_ref[lo:lo + BLOCK, :]
            kc, vc = kc_ref[lo:lo + BLOCK, :], vc_ref[lo:lo + BLOCK, :]
            kp = kp_ref[...] if b == 0 else kc_ref[lo - BLOCK:lo, :]
            vp = vp_ref[...] if b == 0 else vc_ref[lo - BLOCK:lo, :]
            for h in range(N_KV_HEADS):
                q4, do4 = _q_heads(qv, h), _q_heads(dov, h)
                k2, v2 = _kv_head(kp, kc, h), _kv_head(vp, vc, h)
                pn, ps = _attn_probs(q4, k2, mask, _sink_rows(s_ref, h))
                dp = lax.dot_general(do4, v2, _CONTRACT_LAST, preferred_element_type=F32)
                dl = jnp.sum(pn * dp, axis=-1, keepdims=True)
                dsb = (pn * (dp - dl)).astype(BF16)
                dq4 = (jnp.dot(dsb, k2, preferred_element_type=F32) * SCALE).astype(BF16)
                for g in range(GROUP):
                    c0 = HEAD_DIM * (GROUP * h + g)
                    dq_ref[lo:lo + BLOCK, c0:c0 + HEAD_DIM] = dq4[BLOCK * g:BLOCK * (g + 1), :]
                dk2 = lax.dot_general(dsb, q4, _CONTRACT_FIRST, preferred_element_type=F32) * SCALE
                dv2 = lax.dot_general(pn.astype(BF16), do4, _CONTRACT_FIRST, preferred_element_type=F32)
                dk_ref[lo:lo + BLOCK, HEAD_DIM * h:HEAD_DIM * (h + 1)] = dk2[BLOCK:, :] + dkc[h]
                dv_ref[lo:lo + BLOCK, HEAD_DIM * h:HEAD_DIM * (h + 1)] = dv2[BLOCK:, :] + dvc[h]
                dkc[h] = dk2[:BLOCK, :]
                dvc[h] = dv2[:BLOCK, :]
                srow = -ps * dl
                for g in range(GROUP):
                    dsv = dsv + jnp.where(lane == GROUP * h + g, jnp.sum(srow[BLOCK * g:BLOCK * (g + 1), :]), 0.0)
        ds_ref[...] += dsv

    cur = lambda n: pl.BlockSpec((rows, n), lambda i: (ns - 1 - i, 0))
    prev = lambda n: pl.BlockSpec((BLOCK, n), lambda i: (jnp.maximum((ns - 1 - i) * nb - 1, 0), 0))
    return pl.pallas_call(
        body, name="attn_bwd", grid=(ns,),
        in_specs=[pl.BlockSpec(memory_space=pltpu.SMEM), cur(Q_DIM), cur(KV_DIM), prev(KV_DIM), cur(KV_DIM),
                  prev(KV_DIM), cur(Q_DIM)],
        out_specs=[cur(Q_DIM), cur(KV_DIM), cur(KV_DIM), _full((1, N_Q_HEADS))],
        out_shape=[jax.ShapeDtypeStruct((t, Q_DIM), BF16), jax.ShapeDtypeStruct((t, KV_DIM), F32),
                   jax.ShapeDtypeStruct((t, KV_DIM), F32), jax.ShapeDtypeStruct((1, N_Q_HEADS), F32)],
        scratch_shapes=[pltpu.VMEM((N_KV_HEADS, BLOCK, HEAD_DIM), F32), pltpu.VMEM((N_KV_HEADS, BLOCK, HEAD_DIM), F32)],
        compiler_params=_cp(("arbitrary",)),
    )(sinks, q, k, k, v, v, do)


def _ev_dz(dzag, dq, dk, dv, rc, rsa, rsb):
    t = dzag.shape[0]
    tm = min(TM, t)

    def body(dzag_ref, dq_ref, dk_ref, dv_ref, c_ref, sa_ref, sb_ref, dz_ref):
        c, sa, sb = c_ref[...], sa_ref[...], sb_ref[...]
        dz_ref[:, 0:2 * A_CH] = dzag_ref[...]
        q0 = 2 * A_CH
        for j in range(Q_DIM // 128):
            d = dq_ref[:, 128 * j:128 * (j + 1)].astype(F32)
            dz_ref[:, q0 + 128 * j:q0 + 128 * (j + 1)] = _rope_bwd(d, c, sa, sb).astype(BF16)
        k0 = q0 + Q_DIM
        dz_ref[:, k0:k0 + KV_DIM] = _rope_bwd(dk_ref[...], c, sa, sb).astype(BF16)
        dz_ref[:, k0 + KV_DIM:k0 + 2 * KV_DIM] = dv_ref[...].astype(BF16)

    return pl.pallas_call(
        body, name="ev_dz", grid=(t // tm,),
        in_specs=[_rows(tm, 2 * A_CH), _rows(tm, Q_DIM), _rows(tm, KV_DIM), _rows(tm, KV_DIM),
                  _rows(tm, 128), _rows(tm, 128), _rows(tm, 128)],
        out_specs=_rows(tm, EVEN_IN),
        out_shape=jax.ShapeDtypeStruct((t, EVEN_IN), BF16),
        compiler_params=_cp(("arbitrary",)),
    )(dzag, dq, dk, dv, rc, rsa, rsb)


def _prep_ev(gat):
    p = {}
    p["ev_w_in"] = gat["ev_w_in"][:, 0].transpose(1, 0, 2).reshape(D_MODEL, EVEN_IN)
    p["ev_w_out"] = gat["ev_w_out"].reshape(A_CH + Q_DIM, D_MODEL)
    return p


def _prep_rest(gat):
    p = {}
    g = gat["od_w_in"][:, 0]
    p["od_w_in"] = g.transpose(1, 0, 2).reshape(1, D_MODEL, 3 * SC_DIM)
    p["od_w_in_t"] = g.transpose(0, 2, 1).reshape(3 * SC_DIM, D_MODEL)
    p["od_w_out"] = gat["od_w_out"].reshape(SC_DIM, D_MODEL)
    p["ffn_w_up"] = gat["ffn_w_up"].reshape(2, N_DEV // 2, 2, D_MODEL, FF_N)
    p["ffn_w_down"] = [gat["ffn_w_down"][:, i].reshape(D_FF, D_MODEL) for i in range(2)]
    return p


def _local_step(x, positions, target, p, rest_weights, s, token, grads_ready):
    row = lambda a, tok=None: a.reshape(1, -1) if tok is None else a.reshape(1, -1) + tok
    nc = N_DEV // 2
    rc, rsa, rsb = _rope_tables(positions)
    conv31 = jnp.pad(s["ev_a_conv_w"][0], ((0, 1), (0, 0)))
    cw_ffn = [s["ffn_conv_w"][i].reshape(3, 2, nc, FF_N).transpose(1, 2, 0, 3) for i in range(2)]
    sinks = s["ev_sinks"][0]
    big, g = {}, {}

    h0, zag, q, k, v = _ev_in(x, row(s["mix_norm_pre"][0], token), p["ev_w_in"], rc, rsa, rsb)
    cv, a = _conf_fwd(zag, conv31, s["ev_a_conv_b"], s["ev_a_ln_g"], s["ev_a_ln_b"])
    o = _attn_fwd(q, k, v, sinks)
    wo = p["ev_w_out"]
    m0, x1 = _out_post([a, o], [wo[:A_CH], wo[A_CH:]], x, row(s["mix_norm_post"][0]))
    p = {**p, **rest_weights(m0)}
    h1, up0, u0, f0, x2 = _ffn_fwd(x1, row(s["ffn_norm_pre"][0]), p["ffn_w_up"], 0, cw_ffn[0], p["ffn_w_down"][0],
                                   row(s["ffn_norm_post"][0]))
    h2, z, cv1, y, m1, x3 = _od_fwd(x2, row(s["mix_norm_pre"][1]), p["od_w_in"], s["od_conv_w"][0], p["od_w_out"],
                                    row(s["mix_norm_post"][1]))
    h3, up1, u1, f1, x4 = _ffn_fwd(x3, row(s["ffn_norm_pre"][1]), p["ffn_w_up"], 1, cw_ffn[1], p["ffn_w_down"][1],
                                   row(s["ffn_norm_post"][1]))

    def ffn_back(i, f, dxo, up, u, h, x_in, bufs, tok=None, tgt=None):
        df, act, dup, dx_in, dgpost, dgpre, dcw, *loss = _ffn_bwd(
            f, dxo, row(s["ffn_norm_post"][i], tok), x_in, row(s["ffn_norm_pre"][i]), up, u, cw_ffn[i],
            p["ffn_w_down"][i], p["ffn_w_up"], i, tgt)
        bufs = (_dw_up(h, dup.reshape(N_DEV, -1, FF_N), i, bufs[0]), _dw_down(act, df, i, bufs[1]))
        return dx_in, dgpost, dgpre, dcw[:, :, 0:3].transpose(2, 0, 1, 3).reshape(3, 2 * D_FF), bufs, loss

    dx, dgfpost1, dgfpre1, dcw1, bufs, (lpart,) = ffn_back(1, f1, x4, up1, u1, h3, x3, (None, None), None, target)

    dm1, dz, dx, dgpost1, dgpre1, dcw_od = _od_bwd(m1, dx, row(s["mix_norm_post"][1]), x2, row(s["mix_norm_pre"][1]), z,
                                                   cv1, s["od_conv_w"][0], p["od_w_out"], p["od_w_in_t"])
    big["od_w_out"] = _dw2d(y, dm1, SC_DIM, D_MODEL).reshape(N_DEV, -1, D_MODEL)
    big["od_w_in"], big["od_w_in:bf16"] = _dw_cols(h2, dz, 3 * SC_DIM // N_DEV)
    g["od_conv_w"] = dcw_od[None, 0:3]
    tok = grads_ready(["od_w_in", "od_w_out"], big)

    dx, dgfpost0, dgfpre0, dcw0, bufs, _ = ffn_back(0, f0, dx, up0, u0, h1, x1, bufs, tok)
    (big["ffn_w_up"], big["ffn_w_up:bf16"]), (big["ffn_w_down"], big["ffn_w_down:bf16"]) = bufs
    tok = grads_ready(["ffn_w_up", "ffn_w_down"], big)

    dm0, da, do, dgpost0 = _ev_bwd1(m0, dx, row(s["mix_norm_post"][0], tok), p["ev_w_out"])
    big["ev_w_out"] = jnp.concatenate([_dw2d(a, dm0, A_CH, D_MODEL), _dw2d(o, dm0, Q_DIM, D_MODEL)],
                                      axis=0).reshape(N_DEV, -1, D_MODEL)
    tok = grads_ready(["ev_w_out"], big)
    dzag, dcw31, dvec = _conf_bwd(da, cv, zag, conv31, s["ev_a_ln_g"] + tok, s["ev_a_ln_b"])
    dq, dk, dv, dsinks = _attn_bwd(q, k, v, do, sinks)
    dz0 = _ev_dz(dzag, dq, dk, dv, rc, rsa, rsb)
    dw_in = _dw2d(h0, dz0, D_MODEL, EVEN_IN // 2)
    big["ev_w_in"] = dw_in.reshape(D_MODEL, N_DEV, EVEN_IN // N_DEV).transpose(1, 0, 2)
    tok = grads_ready(["ev_w_in"], big)
    dx, dgpre0 = _dz_wt_rms_bwd(dz0, p["ev_w_in"], x, row(s["mix_norm_pre"][0], tok), dx)

    g["mix_norm_pre"] = jnp.concatenate([dgpre0, dgpre1], axis=0)
    g["mix_norm_post"] = jnp.concatenate([dgpost0, dgpost1], axis=0)
    g["ffn_norm_pre"] = jnp.concatenate([dgfpre0, dgfpre1], axis=0)
    g["ffn_norm_post"] = jnp.concatenate([dgfpost0, dgfpost1], axis=0)
    g["ev_a_conv_w"] = dcw31[None, 0:A_CONV]
    g["ev_a_conv_b"], g["ev_a_ln_g"], g["ev_a_ln_b"] = dvec[0:1], dvec[1:2], dvec[2:3]
    g["ev_sinks"] = dsinks
    g["ffn_conv_w"] = jnp.stack([dcw0, dcw1])
    return lpart[0, 0], dx, big, g


MESH = pl.DeviceIdType.MESH


def _all_gather(shards, name):
    nw = len(shards)

    def body(*refs):
        x_refs, out_refs = refs[:nw], refs[nw:2 * nw]
        send_sems, recv_sems, local_sems = refs[2 * nw:]
        x, y, c = lax.axis_index("x"), lax.axis_index("y"), lax.axis_index("c")
        me, sibling = (x, y, c), (x, y, 1 - c)
        chips = [(1 - x, y), (x, 1 - y), (1 - x, 1 - y)]

        def rows(w, px, py, pc):
            m_per = shards[w].shape[0]
            return out_refs[w].at[pl.ds((4 * px + 2 * py + pc) * m_per, m_per), :]

        def copy(w, k, block, to, src=None):
            return pltpu.make_async_remote_copy(
                src_ref=rows(w, *block) if src is None else src, dst_ref=rows(w, *block),
                send_sem=send_sems.at[w, k], recv_sem=recv_sems.at[w, k], device_id=to, device_id_type=MESH)

        mine, first, passed = [], [], []
        for w in range(nw):
            cp = pltpu.make_async_copy(x_refs[w], rows(w, *me), local_sems.at[w])
            cp.start()
            mine.append(cp)
            first.append([copy(w, 0, me, sibling, src=x_refs[w])]
                         + [copy(w, 1 + j, me, (*chip, c), src=x_refs[w]) for j, chip in enumerate(chips)])
            for cp in first[w]:
                cp.start()
        for w in range(nw):
            passed.append([copy(w, 4 + j, (*chip, c), sibling) for j, chip in enumerate(chips)])
            for j, chip in enumerate(chips):
                copy(w, 1 + j, (*chip, c), me).wait_recv()
                passed[w][j].start()
        for w in range(nw):
            copy(w, 0, sibling, me).wait_recv()
            for j, chip in enumerate(chips):
                copy(w, 4 + j, (*chip, 1 - c), me).wait_recv()
            for cp in first[w] + passed[w]:
                cp.wait_send()
            mine[w].wait()

    return pl.pallas_call(
        body, name=name,
        out_shape=[jax.ShapeDtypeStruct((N_DEV * a.shape[0], a.shape[1]), a.dtype) for a in shards],
        in_specs=[_ANY] * nw, out_specs=[_ANY] * nw,
        scratch_shapes=[pltpu.SemaphoreType.DMA((nw, 7)), pltpu.SemaphoreType.DMA((nw, 7)),
                        pltpu.SemaphoreType.DMA((nw,))],
    )(*shards)


_HBM = pl.BlockSpec(memory_space=pltpu.HBM)
_SEM = pl.BlockSpec(memory_space=pltpu.SEMAPHORE)
_EFFECT = pltpu.SideEffectType.DATAFLOW_SIDE_EFFECTING
_RELATIONS = [(dx, dy, dc) for dx in (0, 1) for dy in (0, 1) for dc in (0, 1)][1:]


def _peer(rel):
    x, y, c = lax.axis_index("x"), lax.axis_index("y"), lax.axis_index("c")
    px, py, pc = x ^ rel[0], y ^ rel[1], c ^ rel[2]
    return (px, py, pc), 4 * px + 2 * py + pc, 4 * x + 2 * y + c


def _exchange_copy(k, rel, src_ref, land_ref, send_sems, recv_sems, w, scatter):
    peer, peer_idx, my_idx = _peer(rel)
    src = src_ref.at[peer_idx] if scatter else src_ref
    return pltpu.make_async_remote_copy(
        src_ref=src, dst_ref=land_ref.at[my_idx], send_sem=send_sems.at[_sends(scatter) * w + k],
        recv_sem=recv_sems.at[7 * w + k], device_id=peer, device_id_type=MESH)


def _sends(scatter):
    return 7 if scatter else 8


def _own_copy(src_ref, land_ref, send_sems, w):
    my_idx = _peer(_RELATIONS[0])[2]
    return pltpu.make_async_copy(src_ref, land_ref.at[my_idx], send_sems.at[8 * w + 7])


def _exchange_start(srcs, scatter, name):
    nw = len(srcs)
    lands = [lax.empty((N_DEV,) + (a.shape[1:] if scatter else a.shape), a.dtype) for a in srcs]

    def body(*refs):
        src_refs, land_refs = refs[:nw], refs[nw:2 * nw]
        send_sems, recv_sems = refs[2 * nw], refs[2 * nw + 1]
        token = refs[-1]
        for w in range(nw):
            for k, rel in enumerate(_RELATIONS):
                _exchange_copy(k, rel, src_refs[w], land_refs[w], send_sems, recv_sems, w, scatter).start()
            if not scatter:
                _own_copy(src_refs[w], land_refs[w], send_sems, w).start()
        token[...] = jnp.zeros_like(token)

    hbm = lambda a: pltpu.HBM(a.shape, a.dtype)
    outs = pl.pallas_call(
        body, name=name,
        out_shape=(pltpu.SemaphoreType.DMA((_sends(scatter) * nw,)), pltpu.SemaphoreType.DMA((7 * nw,)),
                   *[hbm(a) for a in srcs],
                   *[hbm(a) for a in lands], jax.ShapeDtypeStruct((8, 128), F32)),
        in_specs=[_HBM] * (2 * nw),
        out_specs=(_SEM, _SEM, *[_HBM] * (2 * nw), pl.BlockSpec(memory_space=pltpu.VMEM)),
        input_output_aliases={i: 2 + i for i in range(2 * nw)},
        compiler_params=pltpu.CompilerParams(has_side_effects=_EFFECT),
    )(*[pltpu.with_memory_space_constraint(a, pltpu.HBM) for a in srcs],
      *[pltpu.with_memory_space_constraint(a, pltpu.HBM) for a in lands])
    return outs[0], outs[1], list(outs[2:2 + nw]), list(outs[2 + nw:2 + 2 * nw]), outs[-1]


def _exchange_wait(started, scatter, after, name):
    send_sems, recv_sems, srcs, lands, _ = started
    nw = len(srcs)

    def body(*refs):
        src_refs, land_refs = refs[:nw], refs[nw:2 * nw]
        send_s, recv_s = refs[2 * nw], refs[2 * nw + 1]
        for w in range(nw):
            for k, rel in enumerate(_RELATIONS):
                cp = _exchange_copy(k, rel, src_refs[w], land_refs[w], send_s, recv_s, w, scatter)
                cp.wait_send()
                _, peer_idx, _ = _peer(rel)
                pltpu.make_async_remote_copy(
                    src_ref=src_refs[w].at[peer_idx] if scatter else src_refs[w], dst_ref=land_refs[w].at[peer_idx],
                    send_sem=send_s.at[_sends(scatter) * w + k], recv_sem=recv_s.at[7 * w + k],
                    device_id=_peer(rel)[0], device_id_type=MESH).wait_recv()
            if not scatter:
                _own_copy(src_refs[w], land_refs[w], send_s, w).wait()

    hbm = lambda a: pltpu.HBM(a.shape, a.dtype)
    outs = pl.pallas_call(
        body, name=name, out_shape=tuple(hbm(a) for a in srcs + lands),
        in_specs=[_HBM] * (2 * nw) + [_SEM, _SEM, _ANY], out_specs=tuple([_HBM] * (2 * nw)),
        input_output_aliases={i: i for i in range(2 * nw)},
        compiler_params=pltpu.CompilerParams(has_side_effects=_EFFECT),
    )(*srcs, *lands, send_sems, recv_sems, after)
    return list(outs[nw:])


def _to_bf16(a):
    _, r, l = a.shape
    tr = _row_tile(r, 512)

    def body(a_ref, o_ref):
        o_ref[...] = a_ref[...].astype(BF16)

    spec = pl.BlockSpec((1, tr, l), lambda j, i: (j, i, 0))
    return pl.pallas_call(
        body, name="to_bf16", grid=(N_DEV, r // tr), in_specs=[spec], out_specs=spec,
        out_shape=jax.ShapeDtypeStruct(a.shape, BF16), compiler_params=_cp(("arbitrary", "arbitrary")),
    )(a)


def _row_tile(rows, cap):
    best = None
    for d in range(16, min(rows, cap) + 1, 16):
        if rows % d == 0:
            best = d
    return rows if best is None else best


def _adam_math(w, g, m, v):
    bc1 = 1.0 - ADAM_B1 ** ADAM_STEP
    bc2 = 1.0 - ADAM_B2 ** ADAM_STEP
    mn = ADAM_B1 * m + (1.0 - ADAM_B1) * g
    vn = ADAM_B2 * v + (1.0 - ADAM_B2) * (g * g)
    return -ADAM_LR * ((mn / bc1) / (jnp.sqrt(vn / bc2) + ADAM_EPS) + ADAM_WD * w), mn, vn


def _adamw_rs(gp, land, w, m, v, dev):
    _, r, l = gp.shape
    tr = _row_tile(r, 256)

    def body(i_ref, g_ref, b_ref, w_ref, m_ref, v_ref, go_ref, d_ref, mo_ref, vo_ref):
        g = g_ref[0]
        for j in range(N_DEV):
            g = g + jnp.where(i_ref[0] == j, 0.0, b_ref[j].astype(F32))
        go_ref[...] = g
        d_ref[...], mo_ref[...], vo_ref[...] = _adam_math(w_ref[...], g, m_ref[...], v_ref[...])

    spec = pl.BlockSpec((tr, l), lambda i, s: (i, 0))
    return pl.pallas_call(
        body, name="adamw_rs", out_shape=[jax.ShapeDtypeStruct((r, l), F32)] * 4,
        grid_spec=pltpu.PrefetchScalarGridSpec(
            num_scalar_prefetch=1, grid=(r // tr,),
            in_specs=[pl.BlockSpec((1, tr, l), lambda i, s: (s[0], i, 0)),
                      pl.BlockSpec((N_DEV, tr, l), lambda i, s: (0, i, 0)), spec, spec, spec],
            out_specs=[spec] * 4),
        compiler_params=_cp(("arbitrary",)),
    )(dev, gp, land, w, m, v)


def _sum_blocks(a, nblk):
    m = a.shape[0] // nblk
    n = a.shape[1]

    def body(a_ref, o_ref):
        acc = a_ref[0]
        for j in range(1, nblk):
            acc = acc + a_ref[j]
        o_ref[...] = acc

    return pl.pallas_call(
        body, name="sum_blocks", out_shape=jax.ShapeDtypeStruct((m, n), a.dtype),
        in_specs=[_full((nblk, m, n))], out_specs=_full((m, n)),
    )(a.reshape(nblk, m, n))


def _adamw(w, g, m, v):
    rows, c = w.shape

    def body(w_ref, g_ref, m_ref, v_ref, d_ref, mo_ref, vo_ref):
        d_ref[...], mo_ref[...], vo_ref[...] = _adam_math(w_ref[...], g_ref[...], m_ref[...], v_ref[...])

    return pl.pallas_call(
        body, name="adamw", in_specs=[_full((rows, c))] * 4, out_specs=[_full((rows, c))] * 3,
        out_shape=[jax.ShapeDtypeStruct((rows, c), F32)] * 3,
    )(w, g, m, v)


WEIGHTS = ["mix_norm_pre", "mix_norm_post", "ffn_norm_pre", "ffn_norm_post", "ev_w_in", "ev_a_conv_w", "ev_a_conv_b",
           "ev_a_ln_g", "ev_a_ln_b", "ev_sinks", "ev_w_out", "od_w_in", "od_conv_w", "od_w_out", "ffn_w_up",
           "ffn_conv_w", "ffn_w_down"]
BIG = ["ev_w_in", "ev_w_out", "od_w_in", "od_w_out", "ffn_w_up", "ffn_w_down"]
SMALL_REPL = ["mix_norm_pre", "mix_norm_post", "ffn_norm_pre", "ffn_norm_post", "ev_a_conv_b", "ev_a_ln_g",
              "ev_a_ln_b", "ev_sinks"]
SMALL_SHARDED = ["ev_a_conv_w", "od_conv_w", "ffn_conv_w"]


def _pack(arrs, rows):
    flat = jnp.concatenate([a.reshape(-1) for a in arrs])
    return jnp.pad(flat, (0, rows * LANES - flat.shape[0])).reshape(rows, LANES)


def _unpack(packed, shapes):
    flat, out, off = packed.reshape(-1), [], 0
    for s in shapes:
        n = 1
        for d in s:
            n *= d
        out.append(flat[off:off + n].reshape(s))
        off += n
    return out


def kernel(x, positions, mix_norm_pre, mix_norm_post, ffn_norm_pre, ffn_norm_post, ev_w_in, ev_a_conv_w, ev_a_conv_b, ev_a_ln_g, ev_a_ln_b, ev_sinks, ev_w_out, od_w_in, od_conv_w, od_w_out, ffn_w_up, ffn_conv_w, ffn_w_down, loss_target, m_mix_norm_pre, m_mix_norm_post, m_ffn_norm_pre, m_ffn_norm_post, m_ev_w_in, m_ev_a_conv_w, m_ev_a_conv_b, m_ev_a_ln_g, m_ev_a_ln_b, m_ev_sinks, m_ev_w_out, m_od_w_in, m_od_conv_w, m_od_w_out, m_ffn_w_up, m_ffn_conv_w, m_ffn_w_down, v_mix_norm_pre, v_mix_norm_post, v_ffn_norm_pre, v_ffn_norm_post, v_ev_w_in, v_ev_a_conv_w, v_ev_a_conv_b, v_ev_a_ln_g, v_ev_a_ln_b, v_ev_sinks, v_ev_w_out, v_od_w_in, v_od_conv_w, v_od_w_out, v_ffn_w_up, v_ffn_conv_w, v_ffn_w_down):
    w = dict(zip(WEIGHTS, (mix_norm_pre, mix_norm_post, ffn_norm_pre, ffn_norm_post, ev_w_in, ev_a_conv_w, ev_a_conv_b,
                           ev_a_ln_g, ev_a_ln_b, ev_sinks, ev_w_out, od_w_in, od_conv_w, od_w_out, ffn_w_up, ffn_conv_w,
                           ffn_w_down)))
    mom = dict(zip(WEIGHTS, (m_mix_norm_pre, m_mix_norm_post, m_ffn_norm_pre, m_ffn_norm_post, m_ev_w_in, m_ev_a_conv_w,
                             m_ev_a_conv_b, m_ev_a_ln_g, m_ev_a_ln_b, m_ev_sinks, m_ev_w_out, m_od_w_in, m_od_conv_w,
                             m_od_w_out, m_ffn_w_up, m_ffn_conv_w, m_ffn_w_down)))
    var = dict(zip(WEIGHTS, (v_mix_norm_pre, v_mix_norm_post, v_ffn_norm_pre, v_ffn_norm_post, v_ev_w_in, v_ev_a_conv_w,
                             v_ev_a_conv_b, v_ev_a_ln_g, v_ev_a_ln_b, v_ev_sinks, v_ev_w_out, v_od_w_in, v_od_conv_w,
                             v_od_w_out, v_ffn_w_up, v_ffn_conv_w, v_ffn_w_down)))
    ix, iy, ic = lax.axis_index("x"), lax.axis_index("y"), lax.axis_index("c")
    dev = 4 * ix + 2 * iy + ic
    two = lambda a: a.reshape(-1, a.shape[-1])

    dev1 = jnp.reshape(dev, (1,)).astype(jnp.int32)
    shard = {n: two(w[n].astype(BF16)) for n in BIG}
    gathered = lambda n, a: a.reshape((N_DEV,) + w[n].shape)
    ev_names = [n for n in BIG if n.startswith("ev_")]
    ev_gat = _all_gather([shard[n] for n in ev_names] + [_pack([w[n] for n in SMALL_SHARDED], 8)], "gather_ev")
    p = _prep_ev({n: gathered(n, a) for n, a in zip(ev_names, ev_gat)})
    rest_names = [n for n in BIG if not n.startswith("ev_")]
    first = shard[rest_names[0]] + (ev_gat[0][0:1, 0:1] * 0).astype(BF16)
    started = _exchange_start([first] + [shard[n] for n in rest_names[1:]], False, "gather_start")

    def rest_weights(after):
        lands = _exchange_wait(started, False, after, "gather_wait")
        return _prep_rest({n: gathered(n, a) for n, a in zip(rest_names, lands)})

    small = {n: w[n] for n in SMALL_REPL}
    small_shapes = [w[n].shape for n in SMALL_SHARDED]
    conv_gat = ev_gat[len(ev_names)].reshape(N_DEV, 8, LANES)
    per_dev = [_unpack(conv_gat[d], small_shapes) for d in range(N_DEV)]
    for k, n in enumerate(SMALL_SHARDED):
        small[n] = jnp.concatenate([per_dev[d][k] for d in range(N_DEV)], axis=-1)

    exchanges = []

    def grads_ready(names, big):
        blocks = lambda a, n: a.reshape(N_DEV, -1, w[n].shape[-1])
        bufs = [blocks(big[n], n) for n in names]
        payload = [blocks(big[n + ":bf16"], n) if n + ":bf16" in big else _to_bf16(b) for n, b in zip(names, bufs)]
        st = _exchange_start(payload, True, "grads_start_" + names[0])
        exchanges.append((names, bufs, st))
        return st[-1][0, 0]

    lpart, grad_x, big, g = _local_step(x[0], positions[0], loss_target[0], p, rest_weights, small, started[-1][0, 0],
                                        grads_ready)
    loss = lax.psum(lpart, ("x", "y", "c"))

    grads, delta, new_m, new_v = {}, {}, {}, {}
    for names, bufs, st in exchanges:
        lands = _exchange_wait(st, True, grad_x, "grads_wait_" + names[0])
        for n, b, land in zip(names, bufs, lands):
            outs = _adamw_rs(b, land, two(w[n]), two(mom[n]), two(var[n]), dev1)
            grads[n], delta[n], new_m[n], new_v[n] = (a.reshape(w[n].shape) for a in outs)

    small_names = SMALL_REPL + SMALL_SHARDED
    s_all = _sum_blocks(_all_gather([_pack([g[n] for n in small_names], 64)], "gather_small_grads")[0], N_DEV)
    for n, a in zip(small_names, _unpack(s_all, [small[n].shape for n in small_names])):
        if n in SMALL_SHARDED:
            width = w[n].shape[-1]
            a = lax.dynamic_slice_in_dim(a, dev * width, width, axis=a.ndim - 1)
        grads[n] = a
    pk = lambda dct: _pack([dct[n] for n in small_names], 16)
    outs = _adamw(pk(w), pk(grads), pk(mom), pk(var))
    for dst, packed in zip((delta, new_m, new_v), outs):
        for n, a in zip(small_names, _unpack(packed, [w[n].shape for n in small_names])):
            dst[n] = a

    return (loss, grad_x[None], *[grads[n] for n in WEIGHTS], *[delta[n] for n in WEIGHTS],
            *[new_m[n] for n in WEIGHTS], *[new_v[n] for n in WEIGHTS])
```

```python
import jax
import jax.numpy as jnp
from jax import lax
from jax.experimental import pallas as pl
from jax.experimental.pallas import tpu as pltpu

F32, BF16 = jnp.float32, jnp.bfloat16

D_MODEL = 1024
A_CH = 512
A_CONV = 31
Q_DIM = 512
KV_DIM = 128
HEAD_DIM = 64
N_Q_HEADS = 8
N_KV_HEADS = 2
GROUP = 4
BLOCK = 128
EVEN_IN = 1792
SC_DIM = 1024
D_FF = 2816
ROPE_THETA = 500000.0
ROPE_DIM = 16
RMS_EPS = 1e-6
LN_EPS = 1e-5
SCALE = HEAD_DIM ** -0.5
NEG = -1e30

ADAM_LR, ADAM_B1, ADAM_B2, ADAM_EPS, ADAM_WD, ADAM_STEP = 0.001, 0.9, 0.999, 1e-08, 0.01, 10

N_DEV = 8
FF_N = 2 * D_FF // N_DEV
LANES = 1024
HALO3 = 8
HALO31 = 32
VMEM_LIMIT = 56 * 1024 * 1024

TM = 512
TM_BWD = 256
TK_DW = 4096
ATT_NB = 4
SUB = 128

_ANY = pl.BlockSpec(memory_space=pl.ANY)
_CONTRACT_LAST = (((1,), (1,)), ((), ()))
_CONTRACT_FIRST = (((0,), (0,)), ((), ()))


def _cp(sem, vmem=VMEM_LIMIT):
    return pltpu.CompilerParams(dimension_semantics=sem, vmem_limit_bytes=vmem)


def _full(shape):
    n = len(shape)
    return pl.BlockSpec(shape, lambda *_: (0,) * n)


def _rows(tm, n):
    return pl.BlockSpec((tm, n), lambda i, *_: (i, 0))


def _sigmoid(x):
    return 0.5 * jnp.tanh(0.5 * x) + 0.5


def _rsqrt_mean(x):
    return lax.rsqrt(jnp.mean(x * x, axis=-1, keepdims=True) + RMS_EPS)


def _rms_bwd(x, g, dy):
    r = _rsqrt_mean(x)
    xh = x * r
    dxh = dy * g
    dx = r * (dxh - xh * jnp.mean(dxh * xh, axis=-1, keepdims=True))
    return dx, jnp.sum(dy * xh, axis=0, keepdims=True)


def _acc_out(ref, first, val):
    @pl.when(first)
    def _():
        ref[...] = val

    @pl.when(jnp.logical_not(first))
    def _():
        ref[...] += val


def _rope_tables(positions):
    half = ROPE_DIM // 2
    inv_freq = ROPE_THETA ** (-(jnp.arange(half, dtype=F32) * 2.0 / ROPE_DIM))
    lane = jnp.arange(2 * HEAD_DIM) % HEAD_DIM
    ang = positions.astype(F32)[:, None] * jnp.tile(inv_freq, 2 * HEAD_DIM // half)[None, :]
    cos, sin = jnp.cos(ang), jnp.sin(ang)
    rot, lo = (lane < ROPE_DIM)[None, :], (lane < half)[None, :]
    c = jnp.where(rot, cos, 1.0)
    sa = jnp.where(rot & ~lo, sin, 0.0)
    sb = jnp.where(lo, -sin, 0.0)
    return c, sa, sb


def _rope(t, c, sa, sb):
    return t * c + pltpu.roll(t, 8, 1) * sa + pltpu.roll(t, 120, 1) * sb


def _rope_bwd(d, c, sa, sb):
    return d * c + pltpu.roll(d * sa, 120, 1) + pltpu.roll(d * sb, 8, 1)


def _ev_in(x, gpre, w_in, rc, rsa, rsb):
    t = x.shape[0]
    tm = min(TM, t)

    def body(x_ref, g_ref, w_ref, c_ref, sa_ref, sb_ref, h_ref, zag_ref, q_ref, k_ref, v_ref):
        xv = x_ref[...]
        h = (xv * _rsqrt_mean(xv) * g_ref[...]).astype(BF16)
        h_ref[...] = h
        z = jnp.dot(h, w_ref[...], preferred_element_type=F32)
        zag_ref[...] = z[:, :2 * A_CH].astype(BF16)
        c, sa, sb = c_ref[...], sa_ref[...], sb_ref[...]
        q0 = 2 * A_CH
        for j in range(Q_DIM // 128):
            q_ref[:, 128 * j:128 * (j + 1)] = _rope(z[:, q0 + 128 * j:q0 + 128 * (j + 1)], c, sa, sb).astype(BF16)
        k0 = q0 + Q_DIM
        k_ref[...] = _rope(z[:, k0:k0 + KV_DIM], c, sa, sb).astype(BF16)
        v_ref[...] = z[:, k0 + KV_DIM:k0 + 2 * KV_DIM].astype(BF16)

    return pl.pallas_call(
        body, name="ev_in", grid=(t // tm,),
        in_specs=[_rows(tm, D_MODEL), _full((1, D_MODEL)), _full((D_MODEL, EVEN_IN)),
                  _rows(tm, 128), _rows(tm, 128), _rows(tm, 128)],
        out_specs=[_rows(tm, D_MODEL), _rows(tm, 2 * A_CH), _rows(tm, Q_DIM), _rows(tm, KV_DIM), _rows(tm, KV_DIM)],
        out_shape=[jax.ShapeDtypeStruct((t, D_MODEL), BF16), jax.ShapeDtypeStruct((t, 2 * A_CH), BF16),
                   jax.ShapeDtypeStruct((t, Q_DIM), BF16), jax.ShapeDtypeStruct((t, KV_DIM), BF16),
                   jax.ShapeDtypeStruct((t, KV_DIM), BF16)],
        compiler_params=_cp(("arbitrary",)),
    )(x, gpre, w_in, rc, rsa, rsb)


def _glu(zag):
    z = zag.astype(F32)
    return z[:, :A_CH] * _sigmoid(z[:, A_CH:])


def _tap_copies(ext, cbuf, first_row, rows):
    for b in range(1, 8):
        s = first_row(b)
        cbuf[b - 1] = ext[s:s + rows, :]


def _conf_fwd(zag, conv_w, conv_b, ln_g, ln_b):
    t = zag.shape[0]
    tm = min(TM_BWD, t)
    rows = tm + HALO31 - 8

    def body(z_ref, w_ref, b_ref, g_ref, lb_ref, c_ref, a_ref, ext, cbuf):
        i = pl.program_id(0)

        @pl.when(i == 0)
        def _():
            ext[0:HALO31, :] = jnp.zeros((HALO31, A_CH), F32)

        ext[HALO31:HALO31 + tm, :] = _glu(z_ref[...])
        _tap_copies(ext, cbuf, lambda b: 8 - b, rows)
        for rs in range(0, tm, SUB):
            for cs in range(0, A_CH, 128):
                acc = jnp.zeros((SUB, 128), F32)
                for k in range(A_CONV):
                    lag_a, lag_b = divmod(k, 8)
                    r0 = HALO31 - 8 - 8 * lag_a + rs
                    src = (ext[r0 + 8:r0 + 8 + SUB, cs:cs + 128] if lag_b == 0
                           else cbuf[lag_b - 1, r0:r0 + SUB, cs:cs + 128])
                    acc = acc + w_ref[A_CONV - 1 - k:A_CONV - k, cs:cs + 128] * src
                c_ref[rs:rs + SUB, cs:cs + 128] = acc
        ext[0:HALO31, :] = ext[tm:tm + HALO31, :]
        cv = c_ref[...] + b_ref[...]
        c_ref[...] = cv
        mu = jnp.mean(cv, axis=-1, keepdims=True)
        xc = cv - mu
        ln = xc * lax.rsqrt(jnp.mean(xc * xc, axis=-1, keepdims=True) + LN_EPS) * g_ref[...] + lb_ref[...]
        a_ref[...] = (ln * _sigmoid(ln)).astype(BF16)

    return pl.pallas_call(
        body, name="conf_fwd", grid=(t // tm,),
        in_specs=[_rows(tm, 2 * A_CH), _full((32, A_CH)), _full((1, A_CH)), _full((1, A_CH)), _full((1, A_CH))],
        out_specs=[_rows(tm, A_CH), _rows(tm, A_CH)],
        out_shape=[jax.ShapeDtypeStruct((t, A_CH), F32), jax.ShapeDtypeStruct((t, A_CH), BF16)],
        scratch_shapes=[pltpu.VMEM((HALO31 + tm, A_CH), F32), pltpu.VMEM((7, rows, A_CH), F32)],
        compiler_params=_cp(("arbitrary",)),
    )(zag, conv_w, conv_b, ln_g, ln_b)


def _attn_mask(first_block):
    row = lax.broadcasted_iota(jnp.int32, (GROUP * BLOCK, 2 * BLOCK), 0) & (BLOCK - 1)
    col = lax.broadcasted_iota(jnp.int32, (GROUP * BLOCK, 2 * BLOCK), 1)
    diff = row + BLOCK - col
    return (diff >= 0) & (diff < BLOCK) & ((col >= BLOCK) | jnp.logical_not(first_block))


def _sink_rows(s_ref, h):
    grp = lax.broadcasted_iota(jnp.int32, (GROUP * BLOCK, 1), 0) >> 7
    out = jnp.full((GROUP * BLOCK, 1), s_ref[GROUP * h], F32)
    for g in range(1, GROUP):
        out = jnp.where(grp == g, s_ref[GROUP * h + g], out)
    return out


def _attn_probs(q4, k2, mask, sink):
    s = lax.dot_general(q4, k2, _CONTRACT_LAST, preferred_element_type=F32) * SCALE
    s = jnp.where(mask, s, NEG)
    m = jnp.maximum(jnp.max(s, axis=-1, keepdims=True), sink)
    p = jnp.exp(s - m)
    es = jnp.exp(sink - m)
    inv = 1.0 / (jnp.sum(p, axis=-1, keepdims=True) + es)
    return p * inv, es * inv


def _q_heads(q, h):
    return jnp.concatenate([q[:, HEAD_DIM * (GROUP * h + g):HEAD_DIM * (GROUP * h + g + 1)] for g in range(GROUP)],
                           axis=0)


def _kv_head(prev, cur, h):
    return jnp.concatenate([prev[:, HEAD_DIM * h:HEAD_DIM * (h + 1)], cur[:, HEAD_DIM * h:HEAD_DIM * (h + 1)]], axis=0)


def _attn_fwd(q, k, v, sinks):
    t = q.shape[0]
    nb = min(ATT_NB, t // BLOCK)
    rows = nb * BLOCK

    def body(s_ref, q_ref, kc_ref, kp_ref, vc_ref, vp_ref, o_ref):
        first = pl.program_id(0) == 0
        for b in range(nb):
            lo = BLOCK * b
            mask = _attn_mask(first) if b == 0 else _attn_mask(False)
            qv, kc, vc = q_ref[lo:lo + BLOCK, :], kc_ref[lo:lo + BLOCK, :], vc_ref[lo:lo + BLOCK, :]
            kp = kp_ref[...] if b == 0 else kc_ref[lo - BLOCK:lo, :]
            vp = vp_ref[...] if b == 0 else vc_ref[lo - BLOCK:lo, :]
            for h in range(N_KV_HEADS):
                pn, _ = _attn_probs(_q_heads(qv, h), _kv_head(kp, kc, h), mask, _sink_rows(s_ref, h))
                o4 = jnp.dot(pn.astype(BF16), _kv_head(vp, vc, h), preferred_element_type=F32).astype(BF16)
                for g in range(GROUP):
                    c0 = HEAD_DIM * (GROUP * h + g)
                    o_ref[lo:lo + BLOCK, c0:c0 + HEAD_DIM] = o4[BLOCK * g:BLOCK * (g + 1), :]

    cur = lambda n: pl.BlockSpec((rows, n), lambda i: (i, 0))
    prev = lambda n: pl.BlockSpec((BLOCK, n), lambda i: (jnp.maximum(i * nb - 1, 0), 0))
    return pl.pallas_call(
        body, name="attn_fwd", grid=(t // rows,),
        in_specs=[pl.BlockSpec(memory_space=pltpu.SMEM), cur(Q_DIM), cur(KV_DIM), prev(KV_DIM), cur(KV_DIM),
                  prev(KV_DIM)],
        out_specs=cur(Q_DIM),
        out_shape=jax.ShapeDtypeStruct((t, Q_DIM), BF16),
        compiler_params=_cp(("arbitrary",)),
    )(sinks, q, k, k, v, v)


def _out_post(lhs, ws, x_in, gpost):
    t = x_in.shape[0]
    tm = min(TM, t)
    n = len(lhs)

    def body(*refs):
        x_ref, g_ref, m_ref, xo_ref = refs[2 * n:]
        m = jnp.dot(refs[0][...], refs[n][...], preferred_element_type=F32)
        for j in range(1, n):
            m = m + jnp.dot(refs[j][...], refs[n + j][...], preferred_element_type=F32)
        m_ref[...] = m.astype(BF16)
        xo_ref[...] = x_ref[...] + m * _rsqrt_mean(m) * g_ref[...]

    return pl.pallas_call(
        body, name="out_post", grid=(t // tm,),
        in_specs=[_rows(tm, a.shape[1]) for a in lhs] + [_full(w.shape) for w in ws]
                 + [_rows(tm, D_MODEL), _full((1, D_MODEL))],
        out_specs=[_rows(tm, D_MODEL), _rows(tm, D_MODEL)],
        out_shape=[jax.ShapeDtypeStruct((t, D_MODEL), BF16), jax.ShapeDtypeStruct((t, D_MODEL), F32)],
        compiler_params=_cp(("arbitrary",)),
    )(*lhs, *ws, x_in, gpost)


def _conv3(w_ref, ext, tm):
    s = HALO3 - 2
    return (w_ref[0:1, :] * ext[s:s + tm, :] + w_ref[1:2, :] * ext[s + 1:s + 1 + tm, :]
            + w_ref[2:3, :] * ext[s + 2:s + 2 + tm, :])


def _ffn_fwd(x1, gpre, wup, layer, cw, wd, gpost):
    t = x1.shape[0]
    tm = min(TM, t)
    nc, n = wup.shape[1], wup.shape[4]

    def body(x_ref, gpre_ref, wup_ref, cw_ref, wd_ref, gpost_ref, h_ref, up_ref, u_ref, f_ref, xo_ref, h_s, acc, ext, hal):
        i, c = pl.program_id(0), pl.program_id(1)

        @pl.when(c == 0)
        def _():
            xv = x_ref[...]
            h = (xv * _rsqrt_mean(xv) * gpre_ref[...]).astype(BF16)
            h_s[...] = h
            h_ref[...] = h

        @pl.when(i == 0)
        def _():
            hal[c] = jnp.zeros((2, HALO3, n), F32)

        u = []
        for gv in range(2):
            up = jnp.dot(h_s[...], wup_ref[gv, 0, 0], preferred_element_type=F32)
            up_ref[gv, 0] = up.astype(BF16)
            ext[gv, 0:HALO3, :] = hal[c, gv]
            ext[gv, HALO3:HALO3 + tm, :] = up
            hal[c, gv] = ext[gv, tm:tm + HALO3, :]
            s = HALO3 - 2
            u.append(cw_ref[gv, 0, 0:1, :] * ext[gv, s:s + tm, :] + cw_ref[gv, 0, 1:2, :] * ext[gv, s + 1:s + 1 + tm, :]
                     + cw_ref[gv, 0, 2:3, :] * up)
            u_ref[gv, 0] = u[gv].astype(BF16)
        act = (u[0] * _sigmoid(u[0]) * u[1]).astype(BF16)
        part = jnp.dot(act, wd_ref[...], preferred_element_type=F32)

        @pl.when(c == 0)
        def _():
            acc[...] = part

        @pl.when(jnp.logical_and(c > 0, c < nc - 1))
        def _():
            acc[...] += part

        @pl.when(c == nc - 1)
        def _():
            f = acc[...] + part
            f_ref[...] = f
            xo_ref[...] = x_ref[...] + f * _rsqrt_mean(f) * gpost_ref[...]

    row = lambda w: pl.BlockSpec((tm, w), lambda i, c: (i, 0))
    one = _full((1, D_MODEL))
    return pl.pallas_call(
        body, name="ffn_fwd", grid=(t // tm, nc),
        in_specs=[row(D_MODEL), one, pl.BlockSpec((2, 1, 1, D_MODEL, n), lambda i, c: (0, c, layer, 0, 0)),
                  pl.BlockSpec((2, 1, 3, n), lambda i, c: (0, c, 0, 0)), pl.BlockSpec((n, D_MODEL), lambda i, c: (c, 0)),
                  one],
        out_specs=[row(D_MODEL), pl.BlockSpec((2, 1, tm, n), lambda i, c: (0, c, i, 0)),
                   pl.BlockSpec((2, 1, tm, n), lambda i, c: (0, c, i, 0)), row(D_MODEL), row(D_MODEL)],
        out_shape=[jax.ShapeDtypeStruct((t, D_MODEL), BF16), jax.ShapeDtypeStruct((2, nc, t, n), BF16),
                   jax.ShapeDtypeStruct((2, nc, t, n), BF16), jax.ShapeDtypeStruct((t, D_MODEL), F32),
                   jax.ShapeDtypeStruct((t, D_MODEL), F32)],
        scratch_shapes=[pltpu.VMEM((tm, D_MODEL), BF16), pltpu.VMEM((tm, D_MODEL), F32),
                        pltpu.VMEM((2, HALO3 + tm, n), F32), pltpu.VMEM((nc, 2, HALO3, n), F32)],
        compiler_params=_cp(("arbitrary", "arbitrary")),
    )(x1, gpre, wup, cw, wd, gpost)


def _od_fwd(x_in, gpre, w_in, cw, w_out, gpost):
    t = x_in.shape[0]
    tm = min(TM, t)
    ns, _, n = w_in.shape

    def body(x_ref, gpre_ref, w_ref, cw_ref, wo_ref, gpost_ref, h_ref, z_ref, cv_ref, y_ref, m_ref, xo_ref, z_s, ext):
        i = pl.program_id(0)
        xv = x_ref[...]
        h = (xv * _rsqrt_mean(xv) * gpre_ref[...]).astype(BF16)
        h_ref[...] = h
        for j in range(ns):
            z_s[:, n * j:n * (j + 1)] = jnp.dot(h, w_ref[j], preferred_element_type=F32)
        z_ref[...] = z_s[...].astype(BF16)

        @pl.when(i == 0)
        def _():
            ext[0:HALO3, :] = jnp.zeros((HALO3, SC_DIM), F32)

        ext[HALO3:HALO3 + tm, :] = z_s[:, SC_DIM:2 * SC_DIM] * z_s[:, 2 * SC_DIM:]
        cv = _conv3(cw_ref, ext, tm)
        cv_ref[...] = cv.astype(BF16)
        y = (z_s[:, :SC_DIM] * cv).astype(BF16)
        ext[0:HALO3, :] = ext[tm:tm + HALO3, :]
        y_ref[...] = y
        m = jnp.dot(y, wo_ref[...], preferred_element_type=F32)
        m_ref[...] = m
        xo_ref[...] = xv + m * _rsqrt_mean(m) * gpost_ref[...]

    return pl.pallas_call(
        body, name="od_fwd", grid=(t // tm,),
        in_specs=[_rows(tm, D_MODEL), _full((1, D_MODEL)), _full((ns, D_MODEL, n)), _full((3, SC_DIM)),
                  _full((SC_DIM, D_MODEL)), _full((1, D_MODEL))],
        out_specs=[_rows(tm, D_MODEL), _rows(tm, 3 * SC_DIM), _rows(tm, SC_DIM), _rows(tm, SC_DIM), _rows(tm, D_MODEL),
                   _rows(tm, D_MODEL)],
        out_shape=[jax.ShapeDtypeStruct((t, D_MODEL), BF16), jax.ShapeDtypeStruct((t, 3 * SC_DIM), BF16),
                   jax.ShapeDtypeStruct((t, SC_DIM), BF16), jax.ShapeDtypeStruct((t, SC_DIM), BF16),
                   jax.ShapeDtypeStruct((t, D_MODEL), F32), jax.ShapeDtypeStruct((t, D_MODEL), F32)],
        scratch_shapes=[pltpu.VMEM((tm, 3 * SC_DIM), F32), pltpu.VMEM((HALO3 + tm, SC_DIM), F32)],
        compiler_params=_cp(("arbitrary",)),
    )(x_in, gpre, w_in, cw, w_out, gpost)


def _dw2d(a, b, bm, bn):
    t, m = a.shape
    n = b.shape[1]
    tk = min(TK_DW, t)

    def body(a_ref, b_ref, o_ref):
        part = lax.dot_general(a_ref[...], b_ref[...], _CONTRACT_FIRST, preferred_element_type=F32)
        _acc_out(o_ref, pl.program_id(2) == 0, part)

    return pl.pallas_call(
        body, name="dw2d", grid=(m // bm, n // bn, t // tk),
        in_specs=[pl.BlockSpec((tk, bm), lambda i, j, k: (k, i)), pl.BlockSpec((tk, bn), lambda i, j, k: (k, j))],
        out_specs=pl.BlockSpec((bm, bn), lambda i, j, k: (i, j)),
        out_shape=jax.ShapeDtypeStruct((m, n), F32),
        compiler_params=_cp(("arbitrary", "arbitrary", "arbitrary")),
    )(a, b)


def _dw_cols(a, b, n_blk):
    t, m = a.shape
    s = b.shape[1] // n_blk
    tk = min(TK_DW, t)
    nk = t // tk

    def body(a_ref, b_ref, o_ref, ob_ref):
        part = lax.dot_general(a_ref[...], b_ref[...], _CONTRACT_FIRST, preferred_element_type=F32)
        _acc_out(o_ref.at[0], pl.program_id(1) == 0, part)

        @pl.when(pl.program_id(1) == nk - 1)
        def _():
            ob_ref[...] = o_ref[...].astype(BF16)

    spec = pl.BlockSpec((1, m, n_blk), lambda j, k: (j, 0, 0))
    return pl.pallas_call(
        body, name="dw_cols", grid=(s, nk),
        in_specs=[pl.BlockSpec((tk, m), lambda j, k: (k, 0)), pl.BlockSpec((tk, n_blk), lambda j, k: (k, j))],
        out_specs=[spec, spec],
        out_shape=[jax.ShapeDtypeStruct((s, m, n_blk), F32), jax.ShapeDtypeStruct((s, m, n_blk), BF16)],
        compiler_params=_cp(("arbitrary", "arbitrary")),
    )(a, b)


def _dw_up(h, dup, layer, buf):
    t, m = h.shape
    s, _, n = dup.shape
    tk = min(TK_DW, t)
    nk = t // tk

    def body(*refs):
        a_ref, b_ref, o_ref, ob_ref = refs[0], refs[1], refs[-2], refs[-1]
        part = lax.dot_general(a_ref[...], b_ref[0], _CONTRACT_FIRST, preferred_element_type=F32)
        _acc_out(o_ref.at[0, 0], pl.program_id(1) == 0, part)

        @pl.when(pl.program_id(1) == nk - 1)
        def _():
            ob_ref[...] = o_ref[...].astype(BF16)

    spec = pl.BlockSpec((1, 1, m, n), lambda j, k: (j, layer, 0, 0))
    return pl.pallas_call(
        body, name="dw_up", grid=(s, nk),
        in_specs=[pl.BlockSpec((tk, m), lambda j, k: (k, 0)), pl.BlockSpec((1, tk, n), lambda j, k: (j, k, 0))]
                 + ([] if buf is None else [_ANY, _ANY]),
        out_specs=[spec, spec],
        out_shape=[jax.ShapeDtypeStruct((s, 2, m, n), F32), jax.ShapeDtypeStruct((s, 2, m, n), BF16)],
        input_output_aliases={} if buf is None else {2: 0, 3: 1},
        compiler_params=_cp(("arbitrary", "arbitrary")),
    )(h, dup, *([] if buf is None else buf))


def _dw_down(act, df, layer, buf):
    nc, t, n = act.shape
    d = df.shape[1]
    tk = min(TK_DW, t)
    nk = t // tk

    def body(*refs):
        a_ref, b_ref, o_ref, ob_ref = refs[0], refs[1], refs[-2], refs[-1]
        part = lax.dot_general(a_ref[0], b_ref[...], _CONTRACT_FIRST, preferred_element_type=F32)
        part = part.reshape(2, n // 2, d)
        first = pl.program_id(1) == 0

        @pl.when(first)
        def _():
            o_ref[:, 0] = part

        @pl.when(jnp.logical_not(first))
        def _():
            o_ref[:, 0] += part

        @pl.when(pl.program_id(1) == nk - 1)
        def _():
            ob_ref[...] = o_ref[...].astype(BF16)

    spec = pl.BlockSpec((2, 1, n // 2, d), lambda c, k: (c, layer, 0, 0))
    return pl.pallas_call(
        body, name="dw_down", grid=(nc, nk),
        in_specs=[pl.BlockSpec((1, tk, n), lambda c, k: (c, k, 0)), pl.BlockSpec((tk, d), lambda c, k: (k, 0))]
                 + ([] if buf is None else [_ANY, _ANY]),
        out_specs=[spec, spec],
        out_shape=[jax.ShapeDtypeStruct((2 * nc, 2, n // 2, d), F32), jax.ShapeDtypeStruct((2 * nc, 2, n // 2, d), BF16)],
        input_output_aliases={} if buf is None else {2: 0, 3: 1},
        compiler_params=_cp(("arbitrary", "arbitrary")),
    )(act, df, *([] if buf is None else buf))


def _dz_wt_rms_bwd(dz, w, x_in, gpre, dres):
    t, n = dz.shape
    tm = min(TM, t)

    def body(dz_ref, wt_ref, x_ref, g_ref, dres_ref, dx_ref, dg_ref):
        dh = lax.dot_general(dz_ref[...], wt_ref[...], _CONTRACT_LAST, preferred_element_type=F32)
        dx, dg = _rms_bwd(x_ref[...], g_ref[...], dh)
        dx_ref[...] = dres_ref[...] + dx
        _acc_out(dg_ref, pl.program_id(0) == 0, dg)

    return pl.pallas_call(
        body, name="dz_wt_rms_bwd", grid=(t // tm,),
        in_specs=[_rows(tm, n), _full((D_MODEL, n)), _rows(tm, D_MODEL), _full((1, D_MODEL)), _rows(tm, D_MODEL)],
        out_specs=[_rows(tm, D_MODEL), _full((1, D_MODEL))],
        out_shape=[jax.ShapeDtypeStruct((t, D_MODEL), F32), jax.ShapeDtypeStruct((1, D_MODEL), F32)],
        compiler_params=_cp(("arbitrary",)),
    )(dz, w, x_in, gpre, dres)


def _shift_matrices(shift, shift_h, tm, hb):
    row = lax.broadcasted_iota(jnp.int32, (2 * tm, tm), 0)
    col = lax.broadcasted_iota(jnp.int32, (2 * tm, tm), 1)
    hit = ((row < tm) & (col == row + 1)) | ((row >= tm) & (col == row - tm + 2))
    shift[...] = jnp.where(hit, 1.0, 0.0).astype(BF16)
    row = lax.broadcasted_iota(jnp.int32, (hb, hb), 0)
    col = lax.broadcasted_iota(jnp.int32, (hb, hb), 1)
    hit = ((row < HALO3) & (col == row - (HALO3 - 1))) | ((row >= HALO3) & (col == row - (2 * HALO3 - 2)))
    shift_h[...] = jnp.where(hit, 1.0, 0.0).astype(BF16)


def _next_rows(shift, shift_h, xb, nxt, d12_s, tm):
    d12_s[...] = jnp.dot(shift[...], xb, preferred_element_type=F32)
    edge = jnp.dot(shift_h[...], nxt, preferred_element_type=F32)
    d12_s[tm - HALO3:tm, :] += edge[0:HALO3, :]
    d12_s[2 * tm - HALO3:2 * tm, :] += edge[HALO3:2 * HALO3, :]


def _ffn_bwd(f, dxo, gpost, x_in, gpre, up, u, cw, wd, wup, layer, target=None):
    t = f.shape[0]
    tm = min(TM_BWD, t)
    nt = t // tm
    nc, n = up.shape[1], up.shape[3]
    hb = 2 * HALO3

    def body(*refs):
        f_ref, dxo_ref, gpost_ref, x_ref, gpre_ref, up_ref, u_ref, cw_ref, wd_ref, wup_ref = refs[:10]
        n_in = 10 if target is None else 11
        n_out = 7 if target is None else 8
        df_ref, act_ref, dup_ref, dx_ref, dgpost_ref, dgpre_ref, dcw_ref = refs[n_in:n_in + 7]
        df_s, acc, du_s, dub_s, d12_s, hal, shift, shift_h = refs[n_in + n_out:]
        i, c = pl.program_id(0), pl.program_id(1)

        def incoming():
            if target is None:
                return dxo_ref[...]
            return (dxo_ref[...] - refs[10][...]) * (1.0 / D_MODEL)

        @pl.when(c == 0)
        def _():
            dy = incoming()
            df, dg = _rms_bwd(f_ref[...], gpost_ref[...], dy)
            df_s[...] = df.astype(BF16)
            df_ref[...] = df.astype(BF16)
            _acc_out(dgpost_ref, i == 0, dg)
            if target is not None:
                part = jnp.zeros((1, 128), F32) + jnp.sum(dy * dy) * (0.5 * D_MODEL)
                _acc_out(refs[n_in + 7], i == 0, part)

        @pl.when(i == 0)
        def _():
            hal[c] = jnp.zeros((2, hb, n), BF16)
            dcw_ref[0, c] = jnp.zeros((8, n), F32)
            dcw_ref[1, c] = jnp.zeros((8, n), F32)

        @pl.when(jnp.logical_and(i == 0, c == 0))
        def _():
            _shift_matrices(shift, shift_h, tm, hb)

        dact = lax.dot_general(df_s[...], wd_ref[...], _CONTRACT_LAST, preferred_element_type=F32)
        g, v = u_ref[0, 0].astype(F32), u_ref[1, 0].astype(F32)
        sg = _sigmoid(g)
        sil = g * sg
        act_ref[0] = (sil * v).astype(BF16)
        dug = dact * v * (sg + sil * (1.0 - sg))
        duv = dact * sil
        du_s[0], du_s[1] = dug, duv
        dub_s[0], dub_s[1] = dug.astype(BF16), duv.astype(BF16)
        dh = None
        for gv in range(2):
            _next_rows(shift, shift_h, dub_s[gv], hal[c, gv], d12_s, tm)
            hal[c, gv] = dub_s[gv, 0:hb, :]
            du, d1, d2 = du_s[gv], d12_s[0:tm, :], d12_s[tm:2 * tm, :]
            dup = (cw_ref[gv, 0, 2:3, :] * du + cw_ref[gv, 0, 1:2, :] * d1 + cw_ref[gv, 0, 0:1, :] * d2).astype(BF16)
            dup_ref[gv, 0] = dup
            upc = up_ref[gv, 0].astype(F32)
            dcw_ref[gv, c, 2:3, :] += jnp.sum(upc * du, axis=0, keepdims=True)
            dcw_ref[gv, c, 1:2, :] += jnp.sum(upc * d1, axis=0, keepdims=True)
            dcw_ref[gv, c, 0:1, :] += jnp.sum(upc * d2, axis=0, keepdims=True)
            part = lax.dot_general(dup, wup_ref[gv, 0, 0], _CONTRACT_LAST, preferred_element_type=F32)
            dh = part if dh is None else dh + part
        _acc_out(acc, c == 0, dh)

        @pl.when(c == nc - 1)
        def _():
            dx, dg = _rms_bwd(x_ref[...], gpre_ref[...], acc[...])
            dx_ref[...] = incoming() + dx
            _acc_out(dgpre_ref, i == 0, dg)

    rrow = lambda w: pl.BlockSpec((tm, w), lambda i, c: (nt - 1 - i, 0))
    blk = pl.BlockSpec((2, 1, tm, n), lambda i, c: (0, c, nt - 1 - i, 0))
    one = _full((1, D_MODEL))
    return pl.pallas_call(
        body, name="ffn_bwd", grid=(nt, nc),
        in_specs=[rrow(D_MODEL), rrow(D_MODEL), one, rrow(D_MODEL), one, blk, blk,
                  pl.BlockSpec((2, 1, 3, n), lambda i, c: (0, c, 0, 0)),
                  pl.BlockSpec((n, D_MODEL), lambda i, c: (c, 0)),
                  pl.BlockSpec((2, 1, 1, D_MODEL, n), lambda i, c: (0, c, layer, 0, 0))]
                 + ([] if target is None else [rrow(D_MODEL)]),
        out_specs=[rrow(D_MODEL), pl.BlockSpec((1, tm, n), lambda i, c: (c, nt - 1 - i, 0)), blk, rrow(D_MODEL),
                   one, one, _full((2, nc, 8, n))] + ([] if target is None else [_full((1, 128))]),
        out_shape=[jax.ShapeDtypeStruct((t, D_MODEL), BF16), jax.ShapeDtypeStruct((nc, t, n), BF16),
                   jax.ShapeDtypeStruct((2, nc, t, n), BF16), jax.ShapeDtypeStruct((t, D_MODEL), F32),
                   jax.ShapeDtypeStruct((1, D_MODEL), F32), jax.ShapeDtypeStruct((1, D_MODEL), F32),
                   jax.ShapeDtypeStruct((2, nc, 8, n), F32)]
                  + ([] if target is None else [jax.ShapeDtypeStruct((1, 128), F32)]),
        scratch_shapes=[pltpu.VMEM((tm, D_MODEL), BF16), pltpu.VMEM((tm, D_MODEL), F32),
                        pltpu.VMEM((2, tm, n), F32), pltpu.VMEM((2, tm, n), BF16), pltpu.VMEM((2 * tm, n), F32),
                        pltpu.VMEM((nc, 2, hb, n), BF16), pltpu.VMEM((2 * tm, tm), BF16), pltpu.VMEM((hb, hb), BF16)],
        compiler_params=_cp(("arbitrary", "arbitrary")),
    )(f, dxo, gpost, x_in, gpre, up, u, cw, wd, wup, *([] if target is None else [target]))


def _od_bwd(m, dxo, gpost, x_in, gpre, z, cv, cw, w_out, wint):
    t = m.shape[0]
    tm = min(TM_BWD, t)
    nt = t // tm
    hb = 2 * HALO3

    def body(m_ref, dxo_ref, gpost_ref, x_ref, gpre_ref, z_ref, cv_ref, cw_ref, wo_ref, wi_ref,
             dm_ref, dz_ref, dx_ref, dgpost_ref, dgpre_ref, dcw_ref, dcvb_s, d12_s, dz_s, hal, shift, shift_h):
        i = pl.program_id(0)
        dxo = dxo_ref[...]
        dm, dg = _rms_bwd(m_ref[...], gpost_ref[...], dxo)
        dmb = dm.astype(BF16)
        dm_ref[...] = dmb
        _acc_out(dgpost_ref, i == 0, dg)

        @pl.when(i == 0)
        def _():
            hal[...] = jnp.zeros((hb, SC_DIM), BF16)
            dcw_ref[...] = jnp.zeros((8, SC_DIM), F32)
            _shift_matrices(shift, shift_h, tm, hb)

        dy = lax.dot_general(dmb, wo_ref[...], _CONTRACT_LAST, preferred_element_type=F32)
        z = z_ref[...].astype(F32)
        b, cg, u = z[:, :SC_DIM], z[:, SC_DIM:2 * SC_DIM], z[:, 2 * SC_DIM:]
        dz_s[:, 0:SC_DIM] = (dy * cv_ref[...].astype(F32)).astype(BF16)
        dcv = dy * b
        dcvb_s[...] = dcv.astype(BF16)
        _next_rows(shift, shift_h, dcvb_s[...], hal[...], d12_s, tm)
        hal[...] = dcvb_s[0:hb, :]
        d1, d2 = d12_s[0:tm, :], d12_s[tm:2 * tm, :]
        dcu = cw_ref[2:3, :] * dcv + cw_ref[1:2, :] * d1 + cw_ref[0:1, :] * d2
        cu = cg * u
        dcw_ref[2:3, :] += jnp.sum(cu * dcv, axis=0, keepdims=True)
        dcw_ref[1:2, :] += jnp.sum(cu * d1, axis=0, keepdims=True)
        dcw_ref[0:1, :] += jnp.sum(cu * d2, axis=0, keepdims=True)
        dz_s[:, SC_DIM:2 * SC_DIM] = (dcu * u).astype(BF16)
        dz_s[:, 2 * SC_DIM:3 * SC_DIM] = (dcu * cg).astype(BF16)
        dz_ref[...] = dz_s[...]
        dh = jnp.dot(dz_s[...], wi_ref[...], preferred_element_type=F32)
        dx, dg2 = _rms_bwd(x_ref[...], gpre_ref[...], dh)
        dx_ref[...] = dxo + dx
        _acc_out(dgpre_ref, i == 0, dg2)

    rrow = lambda w: pl.BlockSpec((tm, w), lambda i: (nt - 1 - i, 0))
    one = _full((1, D_MODEL))
    return pl.pallas_call(
        body, name="od_bwd", grid=(nt,),
        in_specs=[rrow(D_MODEL), rrow(D_MODEL), one, rrow(D_MODEL), one, rrow(3 * SC_DIM), rrow(SC_DIM),
                  _full((3, SC_DIM)), _full((SC_DIM, D_MODEL)), _full((3 * SC_DIM, D_MODEL))],
        out_specs=[rrow(D_MODEL), rrow(3 * SC_DIM), rrow(D_MODEL), one, one, _full((8, SC_DIM))],
        out_shape=[jax.ShapeDtypeStruct((t, D_MODEL), BF16), jax.ShapeDtypeStruct((t, 3 * SC_DIM), BF16),
                   jax.ShapeDtypeStruct((t, D_MODEL), F32), jax.ShapeDtypeStruct((1, D_MODEL), F32),
                   jax.ShapeDtypeStruct((1, D_MODEL), F32), jax.ShapeDtypeStruct((8, SC_DIM), F32)],
        scratch_shapes=[pltpu.VMEM((tm, SC_DIM), BF16), pltpu.VMEM((2 * tm, SC_DIM), F32),
                        pltpu.VMEM((tm, 3 * SC_DIM), BF16), pltpu.VMEM((hb, SC_DIM), BF16),
                        pltpu.VMEM((2 * tm, tm), BF16), pltpu.VMEM((hb, hb), BF16)],
        compiler_params=_cp(("arbitrary",)),
    )(m, dxo, gpost, x_in, gpre, z, cv, cw, w_out, wint)


def _ev_bwd1(m, dxo, gpost, w_out):
    t = m.shape[0]
    tm = min(TM, t)

    def body(m_ref, dxo_ref, g_ref, wot_ref, dm_ref, da_ref, do_ref, dg_ref):
        dm, dg = _rms_bwd(m_ref[...].astype(F32), g_ref[...], dxo_ref[...])
        dmb = dm.astype(BF16)
        dm_ref[...] = dmb
        _acc_out(dg_ref, pl.program_id(0) == 0, dg)
        dao = lax.dot_general(dmb, wot_ref[...], _CONTRACT_LAST, preferred_element_type=F32)
        da_ref[...] = dao[:, :A_CH]
        do_ref[...] = dao[:, A_CH:].astype(BF16)

    return pl.pallas_call(
        body, name="ev_bwd1", grid=(t // tm,),
        in_specs=[_rows(tm, D_MODEL), _rows(tm, D_MODEL), _full((1, D_MODEL)), _full((A_CH + Q_DIM, D_MODEL))],
        out_specs=[_rows(tm, D_MODEL), _rows(tm, A_CH), _rows(tm, Q_DIM), _full((1, D_MODEL))],
        out_shape=[jax.ShapeDtypeStruct((t, D_MODEL), BF16), jax.ShapeDtypeStruct((t, A_CH), F32),
                   jax.ShapeDtypeStruct((t, Q_DIM), BF16), jax.ShapeDtypeStruct((1, D_MODEL), F32)],
        compiler_params=_cp(("arbitrary",)),
    )(m, dxo, gpost, w_out)


def _conf_bwd(da, cv, zag, conv_w, ln_g, ln_b):
    t = da.shape[0]
    tm = min(TM_BWD, t)
    nt = t // tm
    rows = tm + HALO31 - 8

    def body(da_ref, c_ref, z_ref, w_ref, g_ref, lb_ref, dz_ref, dw_ref, dv_ref, ext_out, cbuf, glu_s, dglu_s):
        i = pl.program_id(0)

        @pl.when(i == 0)
        def _():
            ext_out[tm:tm + HALO31, :] = jnp.zeros((HALO31, A_CH), F32)
            dw_ref[...] = jnp.zeros((32, A_CH), F32)
            dv_ref[...] = jnp.zeros((8, A_CH), F32)

        x = c_ref[...]
        mu = jnp.mean(x, axis=-1, keepdims=True)
        xc = x - mu
        rstd = lax.rsqrt(jnp.mean(xc * xc, axis=-1, keepdims=True) + LN_EPS)
        xh = xc * rstd
        ln = xh * g_ref[...] + lb_ref[...]
        sl = _sigmoid(ln)
        dln = da_ref[...] * (sl * (1.0 + ln * (1.0 - sl)))
        dxh = dln * g_ref[...]
        dc = rstd * (dxh - jnp.mean(dxh, axis=-1, keepdims=True) - xh * jnp.mean(dxh * xh, axis=-1, keepdims=True))
        dv_ref[0:1, :] += jnp.sum(dc, axis=0, keepdims=True)
        dv_ref[1:2, :] += jnp.sum(dln * xh, axis=0, keepdims=True)
        dv_ref[2:3, :] += jnp.sum(dln, axis=0, keepdims=True)

        ext_out[0:tm, :] = dc
        _tap_copies(ext_out, cbuf, lambda b: b, rows)
        z = z_ref[...].astype(F32)
        al, sg = z[:, :A_CH], _sigmoid(z[:, A_CH:])
        glu_s[...] = al * sg
        for rs in range(0, tm, SUB):
            for cs in range(0, A_CH, 128):
                glu = glu_s[rs:rs + SUB, cs:cs + 128]
                acc = jnp.zeros((SUB, 128), F32)
                for k in range(A_CONV):
                    lag_a, lag_b = divmod(k, 8)
                    r0 = 8 * lag_a + rs
                    d = (ext_out[r0:r0 + SUB, cs:cs + 128] if lag_b == 0
                         else cbuf[lag_b - 1, r0:r0 + SUB, cs:cs + 128])
                    j = A_CONV - 1 - k
                    acc = acc + w_ref[j:j + 1, cs:cs + 128] * d
                    dw_ref[j:j + 1, cs:cs + 128] += jnp.sum(glu * d, axis=0, keepdims=True)
                dglu_s[rs:rs + SUB, cs:cs + 128] = acc
        dglu = dglu_s[...]
        ext_out[tm:tm + HALO31, :] = ext_out[0:HALO31, :]
        dz_ref[:, 0:A_CH] = (dglu * sg).astype(BF16)
        dz_ref[:, A_CH:2 * A_CH] = (dglu * al * sg * (1.0 - sg)).astype(BF16)

    rrow = lambda w: pl.BlockSpec((tm, w), lambda i: (nt - 1 - i, 0))
    return pl.pallas_call(
        body, name="conf_bwd", grid=(nt,),
        in_specs=[rrow(A_CH), rrow(A_CH), rrow(2 * A_CH), _full((32, A_CH)), _full((1, A_CH)), _full((1, A_CH))],
        out_specs=[rrow(2 * A_CH), _full((32, A_CH)), _full((8, A_CH))],
        out_shape=[jax.ShapeDtypeStruct((t, 2 * A_CH), BF16), jax.ShapeDtypeStruct((32, A_CH), F32),
                   jax.ShapeDtypeStruct((8, A_CH), F32)],
        scratch_shapes=[pltpu.VMEM((tm + HALO31, A_CH), F32), pltpu.VMEM((7, rows, A_CH), F32),
                        pltpu.VMEM((tm, A_CH), F32), pltpu.VMEM((tm, A_CH), F32)],
        compiler_params=_cp(("arbitrary",)),
    )(da, cv, zag, conv_w, ln_g, ln_b)


def _attn_bwd(q, k, v, do, sinks):
    t = q.shape[0]
    nb = min(ATT_NB, t // BLOCK)
    rows = nb * BLOCK
    ns = t // rows

    def body(s_ref, q_ref, kc_ref, kp_ref, vc_ref, vp_ref, do_ref, dq_ref, dk_ref, dv_ref, ds_ref, dkc, dvc):
        i = pl.program_id(0)
        r = ns - 1 - i

        @pl.when(i == 0)
        def _():
            dkc[...] = jnp.zeros_like(dkc)
            dvc[...] = jnp.zeros_like(dvc)
            ds_ref[...] = jnp.zeros_like(ds_ref)

        lane = lax.broadcasted_iota(jnp.int32, (1, N_Q_HEADS), 1)
        dsv = jnp.zeros((1, N_Q_HEADS), F32)
        for b in range(nb - 1, -1, -1):
            lo = BLOCK * b
            mask = _attn_mask(r == 0) if b == 0 else _attn_mask(False)
            qv, dov = q_ref[lo:lo + BLOCK, :], do_ref[lo:lo + BLOCK, :]
            kc, vc = kc_ref[lo:lo + BLOCK, :], vc_ref[lo:lo + BLOCK, :]
            kp = kp_ref[...] if b == 0 else kc_ref[lo - BLOCK:lo, :]
            vp = vp_ref[...] if b == 0 else vc_ref[lo - BLOCK:lo, :]
            for h in range(N_KV_HEADS):
                q4, do4 = _q_heads(qv, h), _q_heads(dov, h)
                k2, v2 = _kv_head(kp, kc, h), _kv_head(vp, vc, h)
                pn, ps = _attn_probs(q4, k2, mask, _sink_rows(s_ref, h))
                dp = lax.dot_general(do4, v2, _CONTRACT_LAST, preferred_element_type=F32)
                dl = jnp.sum(pn * dp, axis=-1, keepdims=True)
                dsb = (pn * (dp - dl)).astype(BF16)
                dq4 = (jnp.dot(dsb, k2, preferred_element_type=F32) * SCALE).astype(BF16)
                for g in range(GROUP):
                    c0 = HEAD_DIM * (GROUP * h + g)
                    dq_ref[lo:lo + BLOCK, c0:c0 + HEAD_DIM] = dq4[BLOCK * g:BLOCK * (g + 1), :]
                dk2 = lax.dot_general(dsb, q4, _CONTRACT_FIRST, preferred_element_type=F32) * SCALE
                dv2 = lax.dot_general(pn.astype(BF16), do4, _CONTRACT_FIRST, preferred_element_type=F32)
                dk_ref[lo:lo + BLOCK, HEAD_DIM * h:HEAD_DIM * (h + 1)] = dk2[BLOCK:, :] + dkc[h]
                dv_ref[lo:lo + BLOCK, HEAD_DIM * h:HEAD_DIM * (h + 1)] = dv2[BLOCK:, :] + dvc[h]
                dkc[h] = dk2[:BLOCK, :]
                dvc[h] = dv2[:BLOCK, :]
                srow = -ps * dl
                for g in range(GROUP):
                    dsv = dsv + jnp.where(lane == GROUP * h + g, jnp.sum(srow[BLOCK * g:BLOCK * (g + 1), :]), 0.0)
        ds_ref[...] += dsv

    cur = lambda n: pl.BlockSpec((rows, n), lambda i: (ns - 1 - i, 0))
    prev = lambda n: pl.BlockSpec((BLOCK, n), lambda i: (jnp.maximum((ns - 1 - i) * nb - 1, 0), 0))
    return pl.pallas_call(
        body, name="attn_bwd", grid=(ns,),
        in_specs=[pl.BlockSpec(memory_space=pltpu.SMEM), cur(Q_DIM), cur(KV_DIM), prev(KV_DIM), cur(KV_DIM),
                  prev(KV_DIM), cur(Q_DIM)],
        out_specs=[cur(Q_DIM), cur(KV_DIM), cur(KV_DIM), _full((1, N_Q_HEADS))],
        out_shape=[jax.ShapeDtypeStruct((t, Q_DIM), BF16), jax.ShapeDtypeStruct((t, KV_DIM), F32),
                   jax.ShapeDtypeStruct((t, KV_DIM), F32), jax.ShapeDtypeStruct((1, N_Q_HEADS), F32)],
        scratch_shapes=[pltpu.VMEM((N_KV_HEADS, BLOCK, HEAD_DIM), F32), pltpu.VMEM((N_KV_HEADS, BLOCK, HEAD_DIM), F32)],
        compiler_params=_cp(("arbitrary",)),
    )(sinks, q, k, k, v, v, do)


def _ev_dz(dzag, dq, dk, dv, rc, rsa, rsb):
    t = dzag.shape[0]
    tm = min(TM, t)

    def body(dzag_ref, dq_ref, dk_ref, dv_ref, c_ref, sa_ref, sb_ref, dz_ref):
        c, sa, sb = c_ref[...], sa_ref[...], sb_ref[...]
        dz_ref[:, 0:2 * A_CH] = dzag_ref[...]
        q0 = 2 * A_CH
        for j in range(Q_DIM // 128):
            d = dq_ref[:, 128 * j:128 * (j + 1)].astype(F32)
            dz_ref[:, q0 + 128 * j:q0 + 128 * (j + 1)] = _rope_bwd(d, c, sa, sb).astype(BF16)
        k0 = q0 + Q_DIM
        dz_ref[:, k0:k0 + KV_DIM] = _rope_bwd(dk_ref[...], c, sa, sb).astype(BF16)
        dz_ref[:, k0 + KV_DIM:k0 + 2 * KV_DIM] = dv_ref[...].astype(BF16)

    return pl.pallas_call(
        body, name="ev_dz", grid=(t // tm,),
        in_specs=[_rows(tm, 2 * A_CH), _rows(tm, Q_DIM), _rows(tm, KV_DIM), _rows(tm, KV_DIM),
                  _rows(tm, 128), _rows(tm, 128), _rows(tm, 128)],
        out_specs=_rows(tm, EVEN_IN),
        out_shape=jax.ShapeDtypeStruct((t, EVEN_IN), BF16),
        compiler_params=_cp(("arbitrary",)),
    )(dzag, dq, dk, dv, rc, rsa, rsb)


def _prep_ev(gat):
    p = {}
    p["ev_w_in"] = gat["ev_w_in"][:, 0].transpose(1, 0, 2).reshape(D_MODEL, EVEN_IN)
    p["ev_w_out"] = gat["ev_w_out"].reshape(A_CH + Q_DIM, D_MODEL)
    return p


def _prep_rest(gat):
    p = {}
    g = gat["od_w_in"][:, 0]
    p["od_w_in"] = g.transpose(1, 0, 2).reshape(1, D_MODEL, 3 * SC_DIM)
    p["od_w_in_t"] = g.transpose(0, 2, 1).reshape(3 * SC_DIM, D_MODEL)
    p["od_w_out"] = gat["od_w_out"].reshape(SC_DIM, D_MODEL)
    p["ffn_w_up"] = gat["ffn_w_up"].reshape(2, N_DEV // 2, 2, D_MODEL, FF_N)
    p["ffn_w_down"] = [gat["ffn_w_down"][:, i].reshape(D_FF, D_MODEL) for i in range(2)]
    return p


def _local_step(x, positions, target, p, rest_weights, s, token, grads_ready):
    row = lambda a, tok=None: a.reshape(1, -1) if tok is None else a.reshape(1, -1) + tok
    nc = N_DEV // 2
    rc, rsa, rsb = _rope_tables(positions)
    conv31 = jnp.pad(s["ev_a_conv_w"][0], ((0, 1), (0, 0)))
    cw_ffn = [s["ffn_conv_w"][i].reshape(3, 2, nc, FF_N).transpose(1, 2, 0, 3) for i in range(2)]
    sinks = s["ev_sinks"][0]
    big, g = {}, {}

    h0, zag, q, k, v = _ev_in(x, row(s["mix_norm_pre"][0], token), p["ev_w_in"], rc, rsa, rsb)
    cv, a = _conf_fwd(zag, conv31, s["ev_a_conv_b"], s["ev_a_ln_g"], s["ev_a_ln_b"])
    o = _attn_fwd(q, k, v, sinks)
    wo = p["ev_w_out"]
    m0, x1 = _out_post([a, o], [wo[:A_CH], wo[A_CH:]], x, row(s["mix_norm_post"][0]))
    p = {**p, **rest_weights(m0)}
    h1, up0, u0, f0, x2 = _ffn_fwd(x1, row(s["ffn_norm_pre"][0]), p["ffn_w_up"], 0, cw_ffn[0], p["ffn_w_down"][0],
                                   row(s["ffn_norm_post"][0]))
    h2, z, cv1, y, m1, x3 = _od_fwd(x2, row(s["mix_norm_pre"][1]), p["od_w_in"], s["od_conv_w"][0], p["od_w_out"],
                                    row(s["mix_norm_post"][1]))
    h3, up1, u1, f1, x4 = _ffn_fwd(x3, row(s["ffn_norm_pre"][1]), p["ffn_w_up"], 1, cw_ffn[1], p["ffn_w_down"][1],
                                   row(s["ffn_norm_post"][1]))

    def ffn_back(i, f, dxo, up, u, h, x_in, bufs, tok=None, tgt=None):
        df, act, dup, dx_in, dgpost, dgpre, dcw, *loss = _ffn_bwd(
            f, dxo, row(s["ffn_norm_post"][i], tok), x_in, row(s["ffn_norm_pre"][i]), up, u, cw_ffn[i],
            p["ffn_w_down"][i], p["ffn_w_up"], i, tgt)
        bufs = (_dw_up(h, dup.reshape(N_DEV, -1, FF_N), i, bufs[0]), _dw_down(act, df, i, bufs[1]))
        return dx_in, dgpost, dgpre, dcw[:, :, 0:3].transpose(2, 0, 1, 3).reshape(3, 2 * D_FF), bufs, loss

    dx, dgfpost1, dgfpre1, dcw1, bufs, (lpart,) = ffn_back(1, f1, x4, up1, u1, h3, x3, (None, None), None, target)

    dm1, dz, dx, dgpost1, dgpre1, dcw_od = _od_bwd(m1, dx, row(s["mix_norm_post"][1]), x2, row(s["mix_norm_pre"][1]), z,
                                                   cv1, s["od_conv_w"][0], p["od_w_out"], p["od_w_in_t"])
    big["od_w_out"] = _dw2d(y, dm1, SC_DIM, D_MODEL).reshape(N_DEV, -1, D_MODEL)
    big["od_w_in"], big["od_w_in:bf16"] = _dw_cols(h2, dz, 3 * SC_DIM // N_DEV)
    g["od_conv_w"] = dcw_od[None, 0:3]
    tok = grads_ready(["od_w_in", "od_w_out"], big)

    dx, dgfpost0, dgfpre0, dcw0, bufs, _ = ffn_back(0, f0, dx, up0, u0, h1, x1, bufs, tok)
    (big["ffn_w_up"], big["ffn_w_up:bf16"]), (big["ffn_w_down"], big["ffn_w_down:bf16"]) = bufs
    tok = grads_ready(["ffn_w_up", "ffn_w_down"], big)

    dm0, da, do, dgpost0 = _ev_bwd1(m0, dx, row(s["mix_norm_post"][0], tok), p["ev_w_out"])
    big["ev_w_out"] = jnp.concatenate([_dw2d(a, dm0, A_CH, D_MODEL), _dw2d(o, dm0, Q_DIM, D_MODEL)],
                                      axis=0).reshape(N_DEV, -1, D_MODEL)
    tok = grads_ready(["ev_w_out"], big)
    dzag, dcw31, dvec = _conf_bwd(da, cv, zag, conv31, s["ev_a_ln_g"] + tok, s["ev_a_ln_b"])
    dq, dk, dv, dsinks = _attn_bwd(q, k, v, do, sinks)
    dz0 = _ev_dz(dzag, dq, dk, dv, rc, rsa, rsb)
    dw_in = _dw2d(h0, dz0, D_MODEL, EVEN_IN // 2)
    big["ev_w_in"] = dw_in.reshape(D_MODEL, N_DEV, EVEN_IN // N_DEV).transpose(1, 0, 2)
    tok = grads_ready(["ev_w_in"], big)
    dx, dgpre0 = _dz_wt_rms_bwd(dz0, p["ev_w_in"], x, row(s["mix_norm_pre"][0], tok), dx)

    g["mix_norm_pre"] = jnp.concatenate([dgpre0, dgpre1], axis=0)
    g["mix_norm_post"] = jnp.concatenate([dgpost0, dgpost1], axis=0)
    g["ffn_norm_pre"] = jnp.concatenate([dgfpre0, dgfpre1], axis=0)
    g["ffn_norm_post"] = jnp.concatenate([dgfpost0, dgfpost1], axis=0)
    g["ev_a_conv_w"] = dcw31[None, 0:A_CONV]
    g["ev_a_conv_b"], g["ev_a_ln_g"], g["ev_a_ln_b"] = dvec[0:1], dvec[1:2], dvec[2:3]
    g["ev_sinks"] = dsinks
    g["ffn_conv_w"] = jnp.stack([dcw0, dcw1])
    return lpart[0, 0], dx, big, g


MESH = pl.DeviceIdType.MESH


def _all_gather(shards, name):
    nw = len(shards)

    def body(*refs):
        x_refs, out_refs = refs[:nw], refs[nw:2 * nw]
        send_sems, recv_sems, local_sems = refs[2 * nw:]
        x, y, c = lax.axis_index("x"), lax.axis_index("y"), lax.axis_index("c")
        me, sibling = (x, y, c), (x, y, 1 - c)
        chips = [(1 - x, y), (x, 1 - y), (1 - x, 1 - y)]

        def rows(w, px, py, pc):
            m_per = shards[w].shape[0]
            return out_refs[w].at[pl.ds((4 * px + 2 * py + pc) * m_per, m_per), :]

        def copy(w, k, block, to, src=None):
            return pltpu.make_async_remote_copy(
                src_ref=rows(w, *block) if src is None else src, dst_ref=rows(w, *block),
                send_sem=send_sems.at[w, k], recv_sem=recv_sems.at[w, k], device_id=to, device_id_type=MESH)

        mine, first, passed = [], [], []
        for w in range(nw):
            cp = pltpu.make_async_copy(x_refs[w], rows(w, *me), local_sems.at[w])
            cp.start()
            mine.append(cp)
            first.append([copy(w, 0, me, sibling, src=x_refs[w])]
                         + [copy(w, 1 + j, me, (*chip, c), src=x_refs[w]) for j, chip in enumerate(chips)])
            for cp in first[w]:
                cp.start()
        for w in range(nw):
            passed.append([copy(w, 4 + j, (*chip, c), sibling) for j, chip in enumerate(chips)])
            for j, chip in enumerate(chips):
                copy(w, 1 + j, (*chip, c), me).wait_recv()
                passed[w][j].start()
        for w in range(nw):
            copy(w, 0, sibling, me).wait_recv()
            for j, chip in enumerate(chips):
                copy(w, 4 + j, (*chip, 1 - c), me).wait_recv()
            for cp in first[w] + passed[w]:
                cp.wait_send()
            mine[w].wait()

    return pl.pallas_call(
        body, name=name,
        out_shape=[jax.ShapeDtypeStruct((N_DEV * a.shape[0], a.shape[1]), a.dtype) for a in shards],
        in_specs=[_ANY] * nw, out_specs=[_ANY] * nw,
        scratch_shapes=[pltpu.SemaphoreType.DMA((nw, 7)), pltpu.SemaphoreType.DMA((nw, 7)),
                        pltpu.SemaphoreType.DMA((nw,))],
    )(*shards)


_HBM = pl.BlockSpec(memory_space=pltpu.HBM)
_SEM = pl.BlockSpec(memory_space=pltpu.SEMAPHORE)
_EFFECT = pltpu.SideEffectType.DATAFLOW_SIDE_EFFECTING
_RELATIONS = [(dx, dy, dc) for dx in (0, 1) for dy in (0, 1) for dc in (0, 1)][1:]


def _peer(rel):
    x, y, c = lax.axis_index("x"), lax.axis_index("y"), lax.axis_index("c")
    px, py, pc = x ^ rel[0], y ^ rel[1], c ^ rel[2]
    return (px, py, pc), 4 * px + 2 * py + pc, 4 * x + 2 * y + c


def _exchange_copy(k, rel, src_ref, land_ref, send_sems, recv_sems, w, scatter):
    peer, peer_idx, my_idx = _peer(rel)
    src = src_ref.at[peer_idx] if scatter else src_ref
    return pltpu.make_async_remote_copy(
        src_ref=src, dst_ref=land_ref.at[my_idx], send_sem=send_sems.at[_sends(scatter) * w + k],
        recv_sem=recv_sems.at[7 * w + k], device_id=peer, device_id_type=MESH)


def _sends(scatter):
    return 7 if scatter else 8


def _own_copy(src_ref, land_ref, send_sems, w):
    my_idx = _peer(_RELATIONS[0])[2]
    return pltpu.make_async_copy(src_ref, land_ref.at[my_idx], send_sems.at[8 * w + 7])


def _exchange_start(srcs, scatter, name):
    nw = len(srcs)
    lands = [lax.empty((N_DEV,) + (a.shape[1:] if scatter else a.shape), a.dtype) for a in srcs]

    def body(*refs):
        src_refs, land_refs = refs[:nw], refs[nw:2 * nw]
        send_sems, recv_sems = refs[2 * nw], refs[2 * nw + 1]
        token = refs[-1]
        for w in range(nw):
            for k, rel in enumerate(_RELATIONS):
                _exchange_copy(k, rel, src_refs[w], land_refs[w], send_sems, recv_sems, w, scatter).start()
            if not scatter:
                _own_copy(src_refs[w], land_refs[w], send_sems, w).start()
        token[...] = jnp.zeros_like(token)

    hbm = lambda a: pltpu.HBM(a.shape, a.dtype)
    outs = pl.pallas_call(
        body, name=name,
        out_shape=(pltpu.SemaphoreType.DMA((_sends(scatter) * nw,)), pltpu.SemaphoreType.DMA((7 * nw,)),
                   *[hbm(a) for a in srcs],
                   *[hbm(a) for a in lands], jax.ShapeDtypeStruct((8, 128), F32)),
        in_specs=[_HBM] * (2 * nw),
        out_specs=(_SEM, _SEM, *[_HBM] * (2 * nw), pl.BlockSpec(memory_space=pltpu.VMEM)),
        input_output_aliases={i: 2 + i for i in range(2 * nw)},
        compiler_params=pltpu.CompilerParams(has_side_effects=_EFFECT),
    )(*[pltpu.with_memory_space_constraint(a, pltpu.HBM) for a in srcs],
      *[pltpu.with_memory_space_constraint(a, pltpu.HBM) for a in lands])
    return outs[0], outs[1], list(outs[2:2 + nw]), list(outs[2 + nw:2 + 2 * nw]), outs[-1]


def _exchange_wait(started, scatter, after, name):
    send_sems, recv_sems, srcs, lands, _ = started
    nw = len(srcs)

    def body(*refs):
        src_refs, land_refs = refs[:nw], refs[nw:2 * nw]
        send_s, recv_s = refs[2 * nw], refs[2 * nw + 1]
        for w in range(nw):
            for k, rel in enumerate(_RELATIONS):
                cp = _exchange_copy(k, rel, src_refs[w], land_refs[w], send_s, recv_s, w, scatter)
                cp.wait_send()
                _, peer_idx, _ = _peer(rel)
                pltpu.make_async_remote_copy(
                    src_ref=src_refs[w].at[peer_idx] if scatter else src_refs[w], dst_ref=land_refs[w].at[peer_idx],
                    send_sem=send_s.at[_sends(scatter) * w + k], recv_sem=recv_s.at[7 * w + k],
                    device_id=_peer(rel)[0], device_id_type=MESH).wait_recv()
            if not scatter:
                _own_copy(src_refs[w], land_refs[w], send_s, w).wait()

    hbm = lambda a: pltpu.HBM(a.shape, a.dtype)
    outs = pl.pallas_call(
        body, name=name, out_shape=tuple(hbm(a) for a in srcs + lands),
        in_specs=[_HBM] * (2 * nw) + [_SEM, _SEM, _ANY], out_specs=tuple([_HBM] * (2 * nw)),
        input_output_aliases={i: i for i in range(2 * nw)},
        compiler_params=pltpu.CompilerParams(has_side_effects=_EFFECT),
    )(*srcs, *lands, send_sems, recv_sems, after)
    return list(outs[nw:])


def _to_bf16(a):
    _, r, l = a.shape
    tr = _row_tile(r, 512)

    def body(a_ref, o_ref):
        o_ref[...] = a_ref[...].astype(BF16)

    spec = pl.BlockSpec((1, tr, l), lambda j, i: (j, i, 0))
    return pl.pallas_call(
        body, name="to_bf16", grid=(N_DEV, r // tr), in_specs=[spec], out_specs=spec,
        out_shape=jax.ShapeDtypeStruct(a.shape, BF16), compiler_params=_cp(("arbitrary", "arbitrary")),
    )(a)


def _row_tile(rows, cap):
    best = None
    for d in range(16, min(rows, cap) + 1, 16):
        if rows % d == 0:
            best = d
    return rows if best is None else best


def _adam_math(w, g, m, v):
    bc1 = 1.0 - ADAM_B1 ** ADAM_STEP
    bc2 = 1.0 - ADAM_B2 ** ADAM_STEP
    mn = ADAM_B1 * m + (1.0 - ADAM_B1) * g
    vn = ADAM_B2 * v + (1.0 - ADAM_B2) * (g * g)
    return -ADAM_LR * ((mn / bc1) / (jnp.sqrt(vn / bc2) + ADAM_EPS) + ADAM_WD * w), mn, vn


def _adamw_rs(gp, land, w, m, v, dev):
    _, r, l = gp.shape
    tr = _row_tile(r, 256)

    def body(i_ref, g_ref, b_ref, w_ref, m_ref, v_ref, go_ref, d_ref, mo_ref, vo_ref):
        g = g_ref[0]
        for j in range(N_DEV):
            g = g + jnp.where(i_ref[0] == j, 0.0, b_ref[j].astype(F32))
        go_ref[...] = g
        d_ref[...], mo_ref[...], vo_ref[...] = _adam_math(w_ref[...], g, m_ref[...], v_ref[...])

    spec = pl.BlockSpec((tr, l), lambda i, s: (i, 0))
    return pl.pallas_call(
        body, name="adamw_rs", out_shape=[jax.ShapeDtypeStruct((r, l), F32)] * 4,
        grid_spec=pltpu.PrefetchScalarGridSpec(
            num_scalar_prefetch=1, grid=(r // tr,),
            in_specs=[pl.BlockSpec((1, tr, l), lambda i, s: (s[0], i, 0)),
                      pl.BlockSpec((N_DEV, tr, l), lambda i, s: (0, i, 0)), spec, spec, spec],
            out_specs=[spec] * 4),
        compiler_params=_cp(("arbitrary",)),
    )(dev, gp, land, w, m, v)


def _sum_blocks(a, nblk):
    m = a.shape[0] // nblk
    n = a.shape[1]

    def body(a_ref, o_ref):
        acc = a_ref[0]
        for j in range(1, nblk):
            acc = acc + a_ref[j]
        o_ref[...] = acc

    return pl.pallas_call(
        body, name="sum_blocks", out_shape=jax.ShapeDtypeStruct((m, n), a.dtype),
        in_specs=[_full((nblk, m, n))], out_specs=_full((m, n)),
    )(a.reshape(nblk, m, n))


def _adamw(w, g, m, v):
    rows, c = w.shape

    def body(w_ref, g_ref, m_ref, v_ref, d_ref, mo_ref, vo_ref):
        d_ref[...], mo_ref[...], vo_ref[...] = _adam_math(w_ref[...], g_ref[...], m_ref[...], v_ref[...])

    return pl.pallas_call(
        body, name="adamw", in_specs=[_full((rows, c))] * 4, out_specs=[_full((rows, c))] * 3,
        out_shape=[jax.ShapeDtypeStruct((rows, c), F32)] * 3,
    )(w, g, m, v)


WEIGHTS = ["mix_norm_pre", "mix_norm_post", "ffn_norm_pre", "ffn_norm_post", "ev_w_in", "ev_a_conv_w", "ev_a_conv_b",
           "ev_a_ln_g", "ev_a_ln_b", "ev_sinks", "ev_w_out", "od_w_in", "od_conv_w", "od_w_out", "ffn_w_up",
           "ffn_conv_w", "ffn_w_down"]
BIG = ["ev_w_in", "ev_w_out", "od_w_in", "od_w_out", "ffn_w_up", "ffn_w_down"]
SMALL_REPL = ["mix_norm_pre", "mix_norm_post", "ffn_norm_pre", "ffn_norm_post", "ev_a_conv_b", "ev_a_ln_g",
              "ev_a_ln_b", "ev_sinks"]
SMALL_SHARDED = ["ev_a_conv_w", "od_conv_w", "ffn_conv_w"]


def _pack(arrs, rows):
    flat = jnp.concatenate([a.reshape(-1) for a in arrs])
    return jnp.pad(flat, (0, rows * LANES - flat.shape[0])).reshape(rows, LANES)


def _unpack(packed, shapes):
    flat, out, off = packed.reshape(-1), [], 0
    for s in shapes:
        n = 1
        for d in s:
            n *= d
        out.append(flat[off:off + n].reshape(s))
        off += n
    return out


def kernel(x, positions, mix_norm_pre, mix_norm_post, ffn_norm_pre, ffn_norm_post, ev_w_in, ev_a_conv_w, ev_a_conv_b, ev_a_ln_g, ev_a_ln_b, ev_sinks, ev_w_out, od_w_in, od_conv_w, od_w_out, ffn_w_up, ffn_conv_w, ffn_w_down, loss_target, m_mix_norm_pre, m_mix_norm_post, m_ffn_norm_pre, m_ffn_norm_post, m_ev_w_in, m_ev_a_conv_w, m_ev_a_conv_b, m_ev_a_ln_g, m_ev_a_ln_b, m_ev_sinks, m_ev_w_out, m_od_w_in, m_od_conv_w, m_od_w_out, m_ffn_w_up, m_ffn_conv_w, m_ffn_w_down, v_mix_norm_pre, v_mix_norm_post, v_ffn_norm_pre, v_ffn_norm_post, v_ev_w_in, v_ev_a_conv_w, v_ev_a_conv_b, v_ev_a_ln_g, v_ev_a_ln_b, v_ev_sinks, v_ev_w_out, v_od_w_in, v_od_conv_w, v_od_w_out, v_ffn_w_up, v_ffn_conv_w, v_ffn_w_down):
    w = dict(zip(WEIGHTS, (mix_norm_pre, mix_norm_post, ffn_norm_pre, ffn_norm_post, ev_w_in, ev_a_conv_w, ev_a_conv_b,
                           ev_a_ln_g, ev_a_ln_b, ev_sinks, ev_w_out, od_w_in, od_conv_w, od_w_out, ffn_w_up, ffn_conv_w,
                           ffn_w_down)))
    mom = dict(zip(WEIGHTS, (m_mix_norm_pre, m_mix_norm_post, m_ffn_norm_pre, m_ffn_norm_post, m_ev_w_in, m_ev_a_conv_w,
                             m_ev_a_conv_b, m_ev_a_ln_g, m_ev_a_ln_b, m_ev_sinks, m_ev_w_out, m_od_w_in, m_od_conv_w,
                             m_od_w_out, m_ffn_w_up, m_ffn_conv_w, m_ffn_w_down)))
    var = dict(zip(WEIGHTS, (v_mix_norm_pre, v_mix_norm_post, v_ffn_norm_pre, v_ffn_norm_post, v_ev_w_in, v_ev_a_conv_w,
                             v_ev_a_conv_b, v_ev_a_ln_g, v_ev_a_ln_b, v_ev_sinks, v_ev_w_out, v_od_w_in, v_od_conv_w,
                             v_od_w_out, v_ffn_w_up, v_ffn_conv_w, v_ffn_w_down)))
    ix, iy, ic = lax.axis_index("x"), lax.axis_index("y"), lax.axis_index("c")
    dev = 4 * ix + 2 * iy + ic
    two = lambda a: a.reshape(-1, a.shape[-1])

    dev1 = jnp.reshape(dev, (1,)).astype(jnp.int32)
    shard = {n: two(w[n].astype(BF16)) for n in BIG}
    gathered = lambda n, a: a.reshape((N_DEV,) + w[n].shape)
    ev_names = [n for n in BIG if n.startswith("ev_")]
    ev_gat = _all_gather([shard[n] for n in ev_names] + [_pack([w[n] for n in SMALL_SHARDED], 8)], "gather_ev")
    p = _prep_ev({n: gathered(n, a) for n, a in zip(ev_names, ev_gat)})
    rest_names = [n for n in BIG if not n.startswith("ev_")]
    first = shard[rest_names[0]] + (ev_gat[0][0:1, 0:1] * 0).astype(BF16)
    started = _exchange_start([first] + [shard[n] for n in rest_names[1:]], False, "gather_start")

    def rest_weights(after):
        lands = _exchange_wait(started, False, after, "gather_wait")
        return _prep_rest({n: gathered(n, a) for n, a in zip(rest_names, lands)})

    small = {n: w[n] for n in SMALL_REPL}
    small_shapes = [w[n].shape for n in SMALL_SHARDED]
    conv_gat = ev_gat[len(ev_names)].reshape(N_DEV, 8, LANES)
    per_dev = [_unpack(conv_gat[d], small_shapes) for d in range(N_DEV)]
    for k, n in enumerate(SMALL_SHARDED):
        small[n] = jnp.concatenate([per_dev[d][k] for d in range(N_DEV)], axis=-1)

    exchanges = []

    def grads_ready(names, big):
        blocks = lambda a, n: a.reshape(N_DEV, -1, w[n].shape[-1])
        bufs = [blocks(big[n], n) for n in names]
        payload = [blocks(big[n + ":bf16"], n) if n + ":bf16" in big else _to_bf16(b) for n, b in zip(names, bufs)]
        st = _exchange_start(payload, True, "grads_start_" + names[0])
        exchanges.append((names, bufs, st))
        return st[-1][0, 0]

    lpart, grad_x, big, g = _local_step(x[0], positions[0], loss_target[0], p, rest_weights, small, started[-1][0, 0],
                                        grads_ready)
    loss = lax.psum(lpart, ("x", "y", "c"))

    grads, delta, new_m, new_v = {}, {}, {}, {}
    for names, bufs, st in exchanges:
        lands = _exchange_wait(st, True, grad_x, "grads_wait_" + names[0])
        for n, b, land in zip(names, bufs, lands):
            outs = _adamw_rs(b, land, two(w[n]), two(mom[n]), two(var[n]), dev1)
            grads[n], delta[n], new_m[n], new_v[n] = (a.reshape(w[n].shape) for a in outs)

    small_names = SMALL_REPL + SMALL_SHARDED
    s_all = _sum_blocks(_all_gather([_pack([g[n] for n in small_names], 64)], "gather_small_grads")[0], N_DEV)
    for n, a in zip(small_names, _unpack(s_all, [small[n].shape for n in small_names])):
        if n in SMALL_SHARDED:
            width = w[n].shape[-1]
            a = lax.dynamic_slice_in_dim(a, dev * width, width, axis=a.ndim - 1)
        grads[n] = a
    pk = lambda dct: _pack([dct[n] for n in small_names], 16)
    outs = _adamw(pk(w), pk(grads), pk(mom), pk(var))
    for dst, packed in zip((delta, new_m, new_v), outs):
        for n, a in zip(small_names, _unpack(packed, [w[n].shape for n in small_names])):
            dst[n] = a

    return (loss, grad_x[None], *[grads[n] for n in WEIGHTS], *[delta[n] for n in WEIGHTS],
            *[new_m[n] for n in WEIGHTS], *[new_v[n] for n in WEIGHTS])
```

```python
import jax
import jax.numpy as jnp
from jax import lax
from jax.experimental import pallas as pl
from jax.experimental.pallas import tpu as pltpu

F32, BF16 = jnp.float32, jnp.bfloat16

D_MODEL = 1024
A_CH = 512
A_CONV = 31
Q_DIM = 512
KV_DIM = 128
HEAD_DIM = 64
N_Q_HEADS = 8
N_KV_HEADS = 2
GROUP = 4
BLOCK = 128
EVEN_IN = 1792
SC_DIM = 1024
D_FF = 2816
ROPE_THETA = 500000.0
ROPE_DIM = 16
RMS_EPS = 1e-6
LN_EPS = 1e-5
SCALE = HEAD_DIM ** -0.5
NEG = -1e30

ADAM_LR, ADAM_B1, ADAM_B2, ADAM_EPS, ADAM_WD, ADAM_STEP = 0.001, 0.9, 0.999, 1e-08, 0.01, 10

N_DEV = 8
FF_N = 2 * D_FF // N_DEV
LANES = 1024
HALO3 = 8
HALO31 = 32
VMEM_LIMIT = 56 * 1024 * 1024

TM = 512
TM_BWD = 256
TK_DW = 4096
ATT_NB = 4
SUB = 128

_ANY = pl.BlockSpec(memory_space=pl.ANY)
_CONTRACT_LAST = (((1,), (1,)), ((), ()))
_CONTRACT_FIRST = (((0,), (0,)), ((), ()))


def _cp(sem, vmem=VMEM_LIMIT):
    return pltpu.CompilerParams(dimension_semantics=sem, vmem_limit_bytes=vmem)


def _full(shape):
    n = len(shape)
    return pl.BlockSpec(shape, lambda *_: (0,) * n)


def _rows(tm, n):
    return pl.BlockSpec((tm, n), lambda i, *_: (i, 0))


def _sigmoid(x):
    return 0.5 * jnp.tanh(0.5 * x) + 0.5


def _rsqrt_mean(x):
    return lax.rsqrt(jnp.mean(x * x, axis=-1, keepdims=True) + RMS_EPS)


def _rms_bwd(x, g, dy):
    r = _rsqrt_mean(x)
    xh = x * r
    dxh = dy * g
    dx = r * (dxh - xh * jnp.mean(dxh * xh, axis=-1, keepdims=True))
    return dx, jnp.sum(dy * xh, axis=0, keepdims=True)


def _acc_out(ref, first, val):
    @pl.when(first)
    def _():
        ref[...] = val

    @pl.when(jnp.logical_not(first))
    def _():
        ref[...] += val


def _rope_tables(positions):
    half = ROPE_DIM // 2
    inv_freq = ROPE_THETA ** (-(jnp.arange(half, dtype=F32) * 2.0 / ROPE_DIM))
    ang = positions.astype(F32)[:, None] * inv_freq
    cos, sin = jnp.cos(ang), jnp.sin(ang)
    t = positions.shape[0]
    one, zero = jnp.ones((t, HEAD_DIM - ROPE_DIM), F32), jnp.zeros((t, HEAD_DIM - ROPE_DIM), F32)
    z8 = jnp.zeros((t, half), F32)
    c = jnp.concatenate([cos, cos, one], axis=1)
    sa = jnp.concatenate([z8, sin, zero], axis=1)
    sb = jnp.concatenate([-sin, z8, zero], axis=1)
    return tuple(jnp.tile(a, (1, 2)) for a in (c, sa, sb))


def _rope(t, c, sa, sb):
    return t * c + pltpu.roll(t, 8, 1) * sa + pltpu.roll(t, 120, 1) * sb


def _rope_bwd(d, c, sa, sb):
    return d * c + pltpu.roll(d * sa, 120, 1) + pltpu.roll(d * sb, 8, 1)


def _ev_in(x, gpre, w_in, rc, rsa, rsb):
    t = x.shape[0]
    tm = min(TM, t)

    def body(x_ref, g_ref, w_ref, c_ref, sa_ref, sb_ref, h_ref, zag_ref, q_ref, k_ref, v_ref):
        xv = x_ref[...]
        h = (xv * _rsqrt_mean(xv) * g_ref[...]).astype(BF16)
        h_ref[...] = h
        z = jnp.dot(h, w_ref[...], preferred_element_type=F32)
        zag_ref[...] = z[:, :2 * A_CH].astype(BF16)
        c, sa, sb = c_ref[...], sa_ref[...], sb_ref[...]
        q0 = 2 * A_CH
        for j in range(Q_DIM // 128):
            q_ref[:, 128 * j:128 * (j + 1)] = _rope(z[:, q0 + 128 * j:q0 + 128 * (j + 1)], c, sa, sb).astype(BF16)
        k0 = q0 + Q_DIM
        k_ref[...] = _rope(z[:, k0:k0 + KV_DIM], c, sa, sb).astype(BF16)
        v_ref[...] = z[:, k0 + KV_DIM:k0 + 2 * KV_DIM].astype(BF16)

    return pl.pallas_call(
        body, name="ev_in", grid=(t // tm,),
        in_specs=[_rows(tm, D_MODEL), _full((1, D_MODEL)), _full((D_MODEL, EVEN_IN)),
                  _rows(tm, 128), _rows(tm, 128), _rows(tm, 128)],
        out_specs=[_rows(tm, D_MODEL), _rows(tm, 2 * A_CH), _rows(tm, Q_DIM), _rows(tm, KV_DIM), _rows(tm, KV_DIM)],
        out_shape=[jax.ShapeDtypeStruct((t, D_MODEL), BF16), jax.ShapeDtypeStruct((t, 2 * A_CH), BF16),
                   jax.ShapeDtypeStruct((t, Q_DIM), BF16), jax.ShapeDtypeStruct((t, KV_DIM), BF16),
                   jax.ShapeDtypeStruct((t, KV_DIM), BF16)],
        compiler_params=_cp(("arbitrary",)),
    )(x, gpre, w_in, rc, rsa, rsb)


def _glu(zag):
    z = zag.astype(F32)
    return z[:, :A_CH] * _sigmoid(z[:, A_CH:])


def _tap_copies(ext, cbuf, first_row, rows):
    for b in range(1, 8):
        s = first_row(b)
        cbuf[b - 1] = ext[s:s + rows, :]


def _conf_fwd(zag, conv_w, conv_b, ln_g, ln_b):
    t = zag.shape[0]
    tm = min(TM, t)
    rows = tm + HALO31 - 8

    def body(z_ref, w_ref, b_ref, g_ref, lb_ref, c_ref, a_ref, ext, cbuf):
        i = pl.program_id(0)

        @pl.when(i == 0)
        def _():
            ext[0:HALO31, :] = jnp.zeros((HALO31, A_CH), F32)

        ext[HALO31:HALO31 + tm, :] = _glu(z_ref[...])
        _tap_copies(ext, cbuf, lambda b: 8 - b, rows)
        for rs in range(0, tm, SUB):
            for cs in range(0, A_CH, 128):
                acc = jnp.zeros((SUB, 128), F32)
                for k in range(A_CONV):
                    lag_a, lag_b = divmod(k, 8)
                    r0 = HALO31 - 8 - 8 * lag_a + rs
                    src = (ext[r0 + 8:r0 + 8 + SUB, cs:cs + 128] if lag_b == 0
                           else cbuf[lag_b - 1, r0:r0 + SUB, cs:cs + 128])
                    acc = acc + w_ref[A_CONV - 1 - k:A_CONV - k, cs:cs + 128] * src
                c_ref[rs:rs + SUB, cs:cs + 128] = acc
        ext[0:HALO31, :] = ext[tm:tm + HALO31, :]
        cv = c_ref[...] + b_ref[...]
        c_ref[...] = cv
        mu = jnp.mean(cv, axis=-1, keepdims=True)
        xc = cv - mu
        ln = xc * lax.rsqrt(jnp.mean(xc * xc, axis=-1, keepdims=True) + LN_EPS) * g_ref[...] + lb_ref[...]
        a_ref[...] = (ln * _sigmoid(ln)).astype(BF16)

    return pl.pallas_call(
        body, name="conf_fwd", grid=(t // tm,),
        in_specs=[_rows(tm, 2 * A_CH), _full((32, A_CH)), _full((1, A_CH)), _full((1, A_CH)), _full((1, A_CH))],
        out_specs=[_rows(tm, A_CH), _rows(tm, A_CH)],
        out_shape=[jax.ShapeDtypeStruct((t, A_CH), F32), jax.ShapeDtypeStruct((t, A_CH), BF16)],
        scratch_shapes=[pltpu.VMEM((HALO31 + tm, A_CH), F32), pltpu.VMEM((7, rows, A_CH), F32)],
        compiler_params=_cp(("arbitrary",)),
    )(zag, conv_w, conv_b, ln_g, ln_b)


def _attn_mask(first_block):
    row = lax.broadcasted_iota(jnp.int32, (GROUP * BLOCK, 2 * BLOCK), 0) & (BLOCK - 1)
    col = lax.broadcasted_iota(jnp.int32, (GROUP * BLOCK, 2 * BLOCK), 1)
    diff = row + BLOCK - col
    return (diff >= 0) & (diff < BLOCK) & ((col >= BLOCK) | jnp.logical_not(first_block))


def _sink_rows(s_ref, h):
    grp = lax.broadcasted_iota(jnp.int32, (GROUP * BLOCK, 1), 0) >> 7
    out = jnp.full((GROUP * BLOCK, 1), s_ref[GROUP * h], F32)
    for g in range(1, GROUP):
        out = jnp.where(grp == g, s_ref[GROUP * h + g], out)
    return out


def _attn_probs(q4, k2, mask, sink):
    s = lax.dot_general(q4, k2, _CONTRACT_LAST, preferred_element_type=F32) * SCALE
    s = jnp.where(mask, s, NEG)
    m = jnp.maximum(jnp.max(s, axis=-1, keepdims=True), sink)
    p = jnp.exp(s - m)
    es = jnp.exp(sink - m)
    inv = 1.0 / (jnp.sum(p, axis=-1, keepdims=True) + es)
    return p * inv, es * inv


def _q_heads(q, h):
    return jnp.concatenate([q[:, HEAD_DIM * (GROUP * h + g):HEAD_DIM * (GROUP * h + g + 1)] for g in range(GROUP)],
                           axis=0)


def _kv_head(prev, cur, h):
    return jnp.concatenate([prev[:, HEAD_DIM * h:HEAD_DIM * (h + 1)], cur[:, HEAD_DIM * h:HEAD_DIM * (h + 1)]], axis=0)


def _attn_fwd(q, k, v, sinks):
    t = q.shape[0]
    nb = min(ATT_NB, t // BLOCK)
    rows = nb * BLOCK

    def body(s_ref, q_ref, kc_ref, kp_ref, vc_ref, vp_ref, o_ref):
        first = pl.program_id(0) == 0
        for b in range(nb):
            lo = BLOCK * b
            mask = _attn_mask(first) if b == 0 else _attn_mask(False)
            qv, kc, vc = q_ref[lo:lo + BLOCK, :], kc_ref[lo:lo + BLOCK, :], vc_ref[lo:lo + BLOCK, :]
            kp = kp_ref[...] if b == 0 else kc_ref[lo - BLOCK:lo, :]
            vp = vp_ref[...] if b == 0 else vc_ref[lo - BLOCK:lo, :]
            for h in range(N_KV_HEADS):
                pn, _ = _attn_probs(_q_heads(qv, h), _kv_head(kp, kc, h), mask, _sink_rows(s_ref, h))
                o4 = jnp.dot(pn.astype(BF16), _kv_head(vp, vc, h), preferred_element_type=F32).astype(BF16)
                for g in range(GROUP):
                    c0 = HEAD_DIM * (GROUP * h + g)
                    o_ref[lo:lo + BLOCK, c0:c0 + HEAD_DIM] = o4[BLOCK * g:BLOCK * (g + 1), :]

    cur = lambda n: pl.BlockSpec((rows, n), lambda i: (i, 0))
    prev = lambda n: pl.BlockSpec((BLOCK, n), lambda i: (jnp.maximum(i * nb - 1, 0), 0))
    return pl.pallas_call(
        body, name="attn_fwd", grid=(t // rows,),
        in_specs=[pl.BlockSpec(memory_space=pltpu.SMEM), cur(Q_DIM), cur(KV_DIM), prev(KV_DIM), cur(KV_DIM),
                  prev(KV_DIM)],
        out_specs=cur(Q_DIM),
        out_shape=jax.ShapeDtypeStruct((t, Q_DIM), BF16),
        compiler_params=_cp(("arbitrary",)),
    )(sinks, q, k, k, v, v)


def _out_post(lhs, ws, x_in, gpost):
    t = x_in.shape[0]
    tm = min(TM, t)
    n = len(lhs)

    def body(*refs):
        x_ref, g_ref, m_ref, xo_ref = refs[2 * n:]
        m = jnp.dot(refs[0][...], refs[n][...], preferred_element_type=F32)
        for j in range(1, n):
            m = m + jnp.dot(refs[j][...], refs[n + j][...], preferred_element_type=F32)
        m_ref[...] = m.astype(BF16)
        xo_ref[...] = x_ref[...] + m * _rsqrt_mean(m) * g_ref[...]

    return pl.pallas_call(
        body, name="out_post", grid=(t // tm,),
        in_specs=[_rows(tm, a.shape[1]) for a in lhs] + [_full(w.shape) for w in ws]
                 + [_rows(tm, D_MODEL), _full((1, D_MODEL))],
        out_specs=[_rows(tm, D_MODEL), _rows(tm, D_MODEL)],
        out_shape=[jax.ShapeDtypeStruct((t, D_MODEL), BF16), jax.ShapeDtypeStruct((t, D_MODEL), F32)],
        compiler_params=_cp(("arbitrary",)),
    )(*lhs, *ws, x_in, gpost)


def _conv3(w_ref, ext, tm):
    s = HALO3 - 2
    return (w_ref[0:1, :] * ext[s:s + tm, :] + w_ref[1:2, :] * ext[s + 1:s + 1 + tm, :]
            + w_ref[2:3, :] * ext[s + 2:s + 2 + tm, :])


def _ffn_fwd(x1, gpre, wup, layer, cw, wd, gpost):
    t = x1.shape[0]
    tm = min(TM, t)
    nc, n = wup.shape[1], wup.shape[4]

    def body(x_ref, gpre_ref, wup_ref, cw_ref, wd_ref, gpost_ref, h_ref, up_ref, u_ref, f_ref, xo_ref, h_s, acc, ext, hal):
        i, c = pl.program_id(0), pl.program_id(1)

        @pl.when(c == 0)
        def _():
            xv = x_ref[...]
            h = (xv * _rsqrt_mean(xv) * gpre_ref[...]).astype(BF16)
            h_s[...] = h
            h_ref[...] = h

        @pl.when(i == 0)
        def _():
            hal[c] = jnp.zeros((2, HALO3, n), F32)

        u = []
        for gv in range(2):
            up = jnp.dot(h_s[...], wup_ref[gv, 0, 0], preferred_element_type=F32)
            up_ref[gv, 0] = up.astype(BF16)
            ext[gv, 0:HALO3, :] = hal[c, gv]
            ext[gv, HALO3:HALO3 + tm, :] = up
            hal[c, gv] = ext[gv, tm:tm + HALO3, :]
            s = HALO3 - 2
            u.append(cw_ref[gv, 0, 0:1, :] * ext[gv, s:s + tm, :] + cw_ref[gv, 0, 1:2, :] * ext[gv, s + 1:s + 1 + tm, :]
                     + cw_ref[gv, 0, 2:3, :] * up)
            u_ref[gv, 0] = u[gv].astype(BF16)
        act = (u[0] * _sigmoid(u[0]) * u[1]).astype(BF16)
        part = jnp.dot(act, wd_ref[...], preferred_element_type=F32)

        @pl.when(c == 0)
        def _():
            acc[...] = part

        @pl.when(jnp.logical_and(c > 0, c < nc - 1))
        def _():
            acc[...] += part

        @pl.when(c == nc - 1)
        def _():
            f = acc[...] + part
            f_ref[...] = f
            xo_ref[...] = x_ref[...] + f * _rsqrt_mean(f) * gpost_ref[...]

    row = lambda w: pl.BlockSpec((tm, w), lambda i, c: (i, 0))
    one = _full((1, D_MODEL))
    return pl.pallas_call(
        body, name="ffn_fwd", grid=(t // tm, nc),
        in_specs=[row(D_MODEL), one, pl.BlockSpec((2, 1, 1, D_MODEL, n), lambda i, c: (0, c, layer, 0, 0)),
                  pl.BlockSpec((2, 1, 3, n), lambda i, c: (0, c, 0, 0)), pl.BlockSpec((n, D_MODEL), lambda i, c: (c, 0)),
                  one],
        out_specs=[row(D_MODEL), pl.BlockSpec((2, 1, tm, n), lambda i, c: (0, c, i, 0)),
                   pl.BlockSpec((2, 1, tm, n), lambda i, c: (0, c, i, 0)), row(D_MODEL), row(D_MODEL)],
        out_shape=[jax.ShapeDtypeStruct((t, D_MODEL), BF16), jax.ShapeDtypeStruct((2, nc, t, n), BF16),
                   jax.ShapeDtypeStruct((2, nc, t, n), BF16), jax.ShapeDtypeStruct((t, D_MODEL), F32),
                   jax.ShapeDtypeStruct((t, D_MODEL), F32)],
        scratch_shapes=[pltpu.VMEM((tm, D_MODEL), BF16), pltpu.VMEM((tm, D_MODEL), F32),
                        pltpu.VMEM((2, HALO3 + tm, n), F32), pltpu.VMEM((nc, 2, HALO3, n), F32)],
        compiler_params=_cp(("arbitrary", "arbitrary")),
    )(x1, gpre, wup, cw, wd, gpost)


def _od_fwd(x_in, gpre, w_in, cw, w_out, gpost):
    t = x_in.shape[0]
    tm = min(TM, t)
    ns, _, n = w_in.shape

    def body(x_ref, gpre_ref, w_ref, cw_ref, wo_ref, gpost_ref, h_ref, z_ref, cv_ref, y_ref, m_ref, xo_ref, z_s, ext):
        i = pl.program_id(0)
        xv = x_ref[...]
        h = (xv * _rsqrt_mean(xv) * gpre_ref[...]).astype(BF16)
        h_ref[...] = h
        for j in range(ns):
            z_s[:, n * j:n * (j + 1)] = jnp.dot(h, w_ref[j], preferred_element_type=F32)
        z_ref[...] = z_s[...].astype(BF16)

        @pl.when(i == 0)
        def _():
            ext[0:HALO3, :] = jnp.zeros((HALO3, SC_DIM), F32)

        ext[HALO3:HALO3 + tm, :] = z_s[:, SC_DIM:2 * SC_DIM] * z_s[:, 2 * SC_DIM:]
        cv = _conv3(cw_ref, ext, tm)
        cv_ref[...] = cv.astype(BF16)
        y = (z_s[:, :SC_DIM] * cv).astype(BF16)
        ext[0:HALO3, :] = ext[tm:tm + HALO3, :]
        y_ref[...] = y
        m = jnp.dot(y, wo_ref[...], preferred_element_type=F32)
        m_ref[...] = m
        xo_ref[...] = xv + m * _rsqrt_mean(m) * gpost_ref[...]

    return pl.pallas_call(
        body, name="od_fwd", grid=(t // tm,),
        in_specs=[_rows(tm, D_MODEL), _full((1, D_MODEL)), _full((ns, D_MODEL, n)), _full((3, SC_DIM)),
                  _full((SC_DIM, D_MODEL)), _full((1, D_MODEL))],
        out_specs=[_rows(tm, D_MODEL), _rows(tm, 3 * SC_DIM), _rows(tm, SC_DIM), _rows(tm, SC_DIM), _rows(tm, D_MODEL),
                   _rows(tm, D_MODEL)],
        out_shape=[jax.ShapeDtypeStruct((t, D_MODEL), BF16), jax.ShapeDtypeStruct((t, 3 * SC_DIM), BF16),
                   jax.ShapeDtypeStruct((t, SC_DIM), BF16), jax.ShapeDtypeStruct((t, SC_DIM), BF16),
                   jax.ShapeDtypeStruct((t, D_MODEL), F32), jax.ShapeDtypeStruct((t, D_MODEL), F32)],
        scratch_shapes=[pltpu.VMEM((tm, 3 * SC_DIM), F32), pltpu.VMEM((HALO3 + tm, SC_DIM), F32)],
        compiler_params=_cp(("arbitrary",)),
    )(x_in, gpre, w_in, cw, w_out, gpost)


def _dw2d(a, b, bm, bn):
    t, m = a.shape
    n = b.shape[1]
    tk = min(TK_DW, t)

    def body(a_ref, b_ref, o_ref):
        part = lax.dot_general(a_ref[...], b_ref[...], _CONTRACT_FIRST, preferred_element_type=F32)
        _acc_out(o_ref, pl.program_id(2) == 0, part)

    return pl.pallas_call(
        body, name="dw2d", grid=(m // bm, n // bn, t // tk),
        in_specs=[pl.BlockSpec((tk, bm), lambda i, j, k: (k, i)), pl.BlockSpec((tk, bn), lambda i, j, k: (k, j))],
        out_specs=pl.BlockSpec((bm, bn), lambda i, j, k: (i, j)),
        out_shape=jax.ShapeDtypeStruct((m, n), F32),
        compiler_params=_cp(("arbitrary", "arbitrary", "arbitrary")),
    )(a, b)


def _dw_cols(a, b, n_blk):
    t, m = a.shape
    s = b.shape[1] // n_blk
    tk = min(TK_DW, t)
    nk = t // tk

    def body(a_ref, b_ref, o_ref, ob_ref):
        part = lax.dot_general(a_ref[...], b_ref[...], _CONTRACT_FIRST, preferred_element_type=F32)
        _acc_out(o_ref.at[0], pl.program_id(1) == 0, part)

        @pl.when(pl.program_id(1) == nk - 1)
        def _():
            ob_ref[...] = o_ref[...].astype(BF16)

    spec = pl.BlockSpec((1, m, n_blk), lambda j, k: (j, 0, 0))
    return pl.pallas_call(
        body, name="dw_cols", grid=(s, nk),
        in_specs=[pl.BlockSpec((tk, m), lambda j, k: (k, 0)), pl.BlockSpec((tk, n_blk), lambda j, k: (k, j))],
        out_specs=[spec, spec],
        out_shape=[jax.ShapeDtypeStruct((s, m, n_blk), F32), jax.ShapeDtypeStruct((s, m, n_blk), BF16)],
        compiler_params=_cp(("arbitrary", "arbitrary")),
    )(a, b)


def _dw_up(h, dup, layer, buf):
    t, m = h.shape
    s, _, n = dup.shape
    tk = min(TK_DW, t)
    nk = t // tk

    def body(*refs):
        a_ref, b_ref, o_ref, ob_ref = refs[0], refs[1], refs[-2], refs[-1]
        part = lax.dot_general(a_ref[...], b_ref[0], _CONTRACT_FIRST, preferred_element_type=F32)
        _acc_out(o_ref.at[0, 0], pl.program_id(1) == 0, part)

        @pl.when(pl.program_id(1) == nk - 1)
        def _():
            ob_ref[...] = o_ref[...].astype(BF16)

    spec = pl.BlockSpec((1, 1, m, n), lambda j, k: (j, layer, 0, 0))
    return pl.pallas_call(
        body, name="dw_up", grid=(s, nk),
        in_specs=[pl.BlockSpec((tk, m), lambda j, k: (k, 0)), pl.BlockSpec((1, tk, n), lambda j, k: (j, k, 0))]
                 + ([] if buf is None else [_ANY, _ANY]),
        out_specs=[spec, spec],
        out_shape=[jax.ShapeDtypeStruct((s, 2, m, n), F32), jax.ShapeDtypeStruct((s, 2, m, n), BF16)],
        input_output_aliases={} if buf is None else {2: 0, 3: 1},
        compiler_params=_cp(("arbitrary", "arbitrary")),
    )(h, dup, *([] if buf is None else buf))


def _dw_down(act, df, layer, buf):
    nc, t, n = act.shape
    d = df.shape[1]
    tk = min(TK_DW, t)
    nk = t // tk

    def body(*refs):
        a_ref, b_ref, o_ref, ob_ref = refs[0], refs[1], refs[-2], refs[-1]
        part = lax.dot_general(a_ref[0], b_ref[...], _CONTRACT_FIRST, preferred_element_type=F32)
        part = part.reshape(2, n // 2, d)
        first = pl.program_id(1) == 0

        @pl.when(first)
        def _():
            o_ref[:, 0] = part

        @pl.when(jnp.logical_not(first))
        def _():
            o_ref[:, 0] += part

        @pl.when(pl.program_id(1) == nk - 1)
        def _():
            ob_ref[...] = o_ref[...].astype(BF16)

    spec = pl.BlockSpec((2, 1, n // 2, d), lambda c, k: (c, layer, 0, 0))
    return pl.pallas_call(
        body, name="dw_down", grid=(nc, nk),
        in_specs=[pl.BlockSpec((1, tk, n), lambda c, k: (c, k, 0)), pl.BlockSpec((tk, d), lambda c, k: (k, 0))]
                 + ([] if buf is None else [_ANY, _ANY]),
        out_specs=[spec, spec],
        out_shape=[jax.ShapeDtypeStruct((2 * nc, 2, n // 2, d), F32), jax.ShapeDtypeStruct((2 * nc, 2, n // 2, d), BF16)],
        input_output_aliases={} if buf is None else {2: 0, 3: 1},
        compiler_params=_cp(("arbitrary", "arbitrary")),
    )(act, df, *([] if buf is None else buf))


def _dz_wt_rms_bwd(dz, w, x_in, gpre, dres):
    t, n = dz.shape
    tm = min(TM, t)

    def body(dz_ref, wt_ref, x_ref, g_ref, dres_ref, dx_ref, dg_ref):
        dh = lax.dot_general(dz_ref[...], wt_ref[...], _CONTRACT_LAST, preferred_element_type=F32)
        dx, dg = _rms_bwd(x_ref[...], g_ref[...], dh)
        dx_ref[...] = dres_ref[...] + dx
        _acc_out(dg_ref, pl.program_id(0) == 0, dg)

    return pl.pallas_call(
        body, name="dz_wt_rms_bwd", grid=(t // tm,),
        in_specs=[_rows(tm, n), _full((D_MODEL, n)), _rows(tm, D_MODEL), _full((1, D_MODEL)), _rows(tm, D_MODEL)],
        out_specs=[_rows(tm, D_MODEL), _full((1, D_MODEL))],
        out_shape=[jax.ShapeDtypeStruct((t, D_MODEL), F32), jax.ShapeDtypeStruct((1, D_MODEL), F32)],
        compiler_params=_cp(("arbitrary",)),
    )(dz, w, x_in, gpre, dres)


def _shift_matrices(shift, shift_h, tm, hb):
    row = lax.broadcasted_iota(jnp.int32, (2 * tm, tm), 0)
    col = lax.broadcasted_iota(jnp.int32, (2 * tm, tm), 1)
    hit = ((row < tm) & (col == row + 1)) | ((row >= tm) & (col == row - tm + 2))
    shift[...] = jnp.where(hit, 1.0, 0.0).astype(BF16)
    row = lax.broadcasted_iota(jnp.int32, (hb, hb), 0)
    col = lax.broadcasted_iota(jnp.int32, (hb, hb), 1)
    hit = ((row < HALO3) & (col == row - (HALO3 - 1))) | ((row >= HALO3) & (col == row - (2 * HALO3 - 2)))
    shift_h[...] = jnp.where(hit, 1.0, 0.0).astype(BF16)


def _next_rows(shift, shift_h, xb, nxt, d12_s, tm):
    d12_s[...] = jnp.dot(shift[...], xb, preferred_element_type=F32)
    edge = jnp.dot(shift_h[...], nxt, preferred_element_type=F32)
    d12_s[tm - HALO3:tm, :] += edge[0:HALO3, :]
    d12_s[2 * tm - HALO3:2 * tm, :] += edge[HALO3:2 * HALO3, :]


def _ffn_bwd(f, dxo, gpost, x_in, gpre, up, u, cw, wd, wup, layer, target=None):
    t = f.shape[0]
    tm = min(TM_BWD, t)
    nt = t // tm
    nc, n = up.shape[1], up.shape[3]
    hb = 2 * HALO3

    def body(*refs):
        f_ref, dxo_ref, gpost_ref, x_ref, gpre_ref, up_ref, u_ref, cw_ref, wd_ref, wup_ref = refs[:10]
        n_in = 10 if target is None else 11
        n_out = 7 if target is None else 8
        df_ref, act_ref, dup_ref, dx_ref, dgpost_ref, dgpre_ref, dcw_ref = refs[n_in:n_in + 7]
        df_s, acc, du_s, dub_s, d12_s, hal, shift, shift_h = refs[n_in + n_out:]
        i, c = pl.program_id(0), pl.program_id(1)

        def incoming():
            if target is None:
                return dxo_ref[...]
            return (dxo_ref[...] - refs[10][...]) * (1.0 / D_MODEL)

        @pl.when(c == 0)
        def _():
            dy = incoming()
            df, dg = _rms_bwd(f_ref[...], gpost_ref[...], dy)
            df_s[...] = df.astype(BF16)
            df_ref[...] = df.astype(BF16)
            _acc_out(dgpost_ref, i == 0, dg)
            if target is not None:
                part = jnp.zeros((1, 128), F32) + jnp.sum(dy * dy) * (0.5 * D_MODEL)
                _acc_out(refs[n_in + 7], i == 0, part)

        @pl.when(i == 0)
        def _():
            hal[c] = jnp.zeros((2, hb, n), BF16)
            dcw_ref[0, c] = jnp.zeros((8, n), F32)
            dcw_ref[1, c] = jnp.zeros((8, n), F32)

        @pl.when(jnp.logical_and(i == 0, c == 0))
        def _():
            _shift_matrices(shift, shift_h, tm, hb)

        dact = lax.dot_general(df_s[...], wd_ref[...], _CONTRACT_LAST, preferred_element_type=F32)
        g, v = u_ref[0, 0].astype(F32), u_ref[1, 0].astype(F32)
        sg = _sigmoid(g)
        sil = g * sg
        act_ref[0] = (sil * v).astype(BF16)
        dug = dact * v * (sg + sil * (1.0 - sg))
        duv = dact * sil
        du_s[0], du_s[1] = dug, duv
        dub_s[0], dub_s[1] = dug.astype(BF16), duv.astype(BF16)
        dh = None
        for gv in range(2):
            _next_rows(shift, shift_h, dub_s[gv], hal[c, gv], d12_s, tm)
            hal[c, gv] = dub_s[gv, 0:hb, :]
            du, d1, d2 = du_s[gv], d12_s[0:tm, :], d12_s[tm:2 * tm, :]
            dup = (cw_ref[gv, 0, 2:3, :] * du + cw_ref[gv, 0, 1:2, :] * d1 + cw_ref[gv, 0, 0:1, :] * d2).astype(BF16)
            dup_ref[gv, 0] = dup
            upc = up_ref[gv, 0].astype(F32)
            dcw_ref[gv, c, 2:3, :] += jnp.sum(upc * du, axis=0, keepdims=True)
            dcw_ref[gv, c, 1:2, :] += jnp.sum(upc * d1, axis=0, keepdims=True)
            dcw_ref[gv, c, 0:1, :] += jnp.sum(upc * d2, axis=0, keepdims=True)
            part = lax.dot_general(dup, wup_ref[gv, 0, 0], _CONTRACT_LAST, preferred_element_type=F32)
            dh = part if dh is None else dh + part
        _acc_out(acc, c == 0, dh)

        @pl.when(c == nc - 1)
        def _():
            dx, dg = _rms_bwd(x_ref[...], gpre_ref[...], acc[...])
            dx_ref[...] = incoming() + dx
            _acc_out(dgpre_ref, i == 0, dg)

    rrow = lambda w: pl.BlockSpec((tm, w), lambda i, c: (nt - 1 - i, 0))
    blk = pl.BlockSpec((2, 1, tm, n), lambda i, c: (0, c, nt - 1 - i, 0))
    one = _full((1, D_MODEL))
    return pl.pallas_call(
        body, name="ffn_bwd", grid=(nt, nc),
        in_specs=[rrow(D_MODEL), rrow(D_MODEL), one, rrow(D_MODEL), one, blk, blk,
                  pl.BlockSpec((2, 1, 3, n), lambda i, c: (0, c, 0, 0)),
                  pl.BlockSpec((n, D_MODEL), lambda i, c: (c, 0)),
                  pl.BlockSpec((2, 1, 1, D_MODEL, n), lambda i, c: (0, c, layer, 0, 0))]
                 + ([] if target is None else [rrow(D_MODEL)]),
        out_specs=[rrow(D_MODEL), pl.BlockSpec((1, tm, n), lambda i, c: (c, nt - 1 - i, 0)), blk, rrow(D_MODEL),
                   one, one, _full((2, nc, 8, n))] + ([] if target is None else [_full((1, 128))]),
        out_shape=[jax.ShapeDtypeStruct((t, D_MODEL), BF16), jax.ShapeDtypeStruct((nc, t, n), BF16),
                   jax.ShapeDtypeStruct((2, nc, t, n), BF16), jax.ShapeDtypeStruct((t, D_MODEL), F32),
                   jax.ShapeDtypeStruct((1, D_MODEL), F32), jax.ShapeDtypeStruct((1, D_MODEL), F32),
                   jax.ShapeDtypeStruct((2, nc, 8, n), F32)]
                  + ([] if target is None else [jax.ShapeDtypeStruct((1, 128), F32)]),
        scratch_shapes=[pltpu.VMEM((tm, D_MODEL), BF16), pltpu.VMEM((tm, D_MODEL), F32),
                        pltpu.VMEM((2, tm, n), F32), pltpu.VMEM((2, tm, n), BF16), pltpu.VMEM((2 * tm, n), F32),
                        pltpu.VMEM((nc, 2, hb, n), BF16), pltpu.VMEM((2 * tm, tm), BF16), pltpu.VMEM((hb, hb), BF16)],
        compiler_params=_cp(("arbitrary", "arbitrary")),
    )(f, dxo, gpost, x_in, gpre, up, u, cw, wd, wup, *([] if target is None else [target]))


def _od_bwd(m, dxo, gpost, x_in, gpre, z, cv, cw, w_out, wint):
    t = m.shape[0]
    tm = min(TM_BWD, t)
    nt = t // tm
    hb = 2 * HALO3

    def body(m_ref, dxo_ref, gpost_ref, x_ref, gpre_ref, z_ref, cv_ref, cw_ref, wo_ref, wi_ref,
             dm_ref, dz_ref, dx_ref, dgpost_ref, dgpre_ref, dcw_ref, dcvb_s, d12_s, dz_s, hal, shift, shift_h):
        i = pl.program_id(0)
        dxo = dxo_ref[...]
        dm, dg = _rms_bwd(m_ref[...], gpost_ref[...], dxo)
        dmb = dm.astype(BF16)
        dm_ref[...] = dmb
        _acc_out(dgpost_ref, i == 0, dg)

        @pl.when(i == 0)
        def _():
            hal[...] = jnp.zeros((hb, SC_DIM), BF16)
            dcw_ref[...] = jnp.zeros((8, SC_DIM), F32)
            _shift_matrices(shift, shift_h, tm, hb)

        dy = lax.dot_general(dmb, wo_ref[...], _CONTRACT_LAST, preferred_element_type=F32)
        z = z_ref[...].astype(F32)
        b, cg, u = z[:, :SC_DIM], z[:, SC_DIM:2 * SC_DIM], z[:, 2 * SC_DIM:]
        dz_s[:, 0:SC_DIM] = (dy * cv_ref[...].astype(F32)).astype(BF16)
        dcv = dy * b
        dcvb_s[...] = dcv.astype(BF16)
        _next_rows(shift, shift_h, dcvb_s[...], hal[...], d12_s, tm)
        hal[...] = dcvb_s[0:hb, :]
        d1, d2 = d12_s[0:tm, :], d12_s[tm:2 * tm, :]
        dcu = cw_ref[2:3, :] * dcv + cw_ref[1:2, :] * d1 + cw_ref[0:1, :] * d2
        cu = cg * u
        dcw_ref[2:3, :] += jnp.sum(cu * dcv, axis=0, keepdims=True)
        dcw_ref[1:2, :] += jnp.sum(cu * d1, axis=0, keepdims=True)
        dcw_ref[0:1, :] += jnp.sum(cu * d2, axis=0, keepdims=True)
        dz_s[:, SC_DIM:2 * SC_DIM] = (dcu * u).astype(BF16)
        dz_s[:, 2 * SC_DIM:3 * SC_DIM] = (dcu * cg).astype(BF16)
        dz_ref[...] = dz_s[...]
        dh = jnp.dot(dz_s[...], wi_ref[...], preferred_element_type=F32)
        dx, dg2 = _rms_bwd(x_ref[...], gpre_ref[...], dh)
        dx_ref[...] = dxo + dx
        _acc_out(dgpre_ref, i == 0, dg2)

    rrow = lambda w: pl.BlockSpec((tm, w), lambda i: (nt - 1 - i, 0))
    one = _full((1, D_MODEL))
    return pl.pallas_call(
        body, name="od_bwd", grid=(nt,),
        in_specs=[rrow(D_MODEL), rrow(D_MODEL), one, rrow(D_MODEL), one, rrow(3 * SC_DIM), rrow(SC_DIM),
                  _full((3, SC_DIM)), _full((SC_DIM, D_MODEL)), _full((3 * SC_DIM, D_MODEL))],
        out_specs=[rrow(D_MODEL), rrow(3 * SC_DIM), rrow(D_MODEL), one, one, _full((8, SC_DIM))],
        out_shape=[jax.ShapeDtypeStruct((t, D_MODEL), BF16), jax.ShapeDtypeStruct((t, 3 * SC_DIM), BF16),
                   jax.ShapeDtypeStruct((t, D_MODEL), F32), jax.ShapeDtypeStruct((1, D_MODEL), F32),
                   jax.ShapeDtypeStruct((1, D_MODEL), F32), jax.ShapeDtypeStruct((8, SC_DIM), F32)],
        scratch_shapes=[pltpu.VMEM((tm, SC_DIM), BF16), pltpu.VMEM((2 * tm, SC_DIM), F32),
                        pltpu.VMEM((tm, 3 * SC_DIM), BF16), pltpu.VMEM((hb, SC_DIM), BF16),
                        pltpu.VMEM((2 * tm, tm), BF16), pltpu.VMEM((hb, hb), BF16)],
        compiler_params=_cp(("arbitrary",)),
    )(m, dxo, gpost, x_in, gpre, z, cv, cw, w_out, wint)


def _ev_bwd1(m, dxo, gpost, w_out):
    t = m.shape[0]
    tm = min(TM, t)

    def body(m_ref, dxo_ref, g_ref, wot_ref, dm_ref, da_ref, do_ref, dg_ref):
        dm, dg = _rms_bwd(m_ref[...].astype(F32), g_ref[...], dxo_ref[...])
        dmb = dm.astype(BF16)
        dm_ref[...] = dmb
        _acc_out(dg_ref, pl.program_id(0) == 0, dg)
        dao = lax.dot_general(dmb, wot_ref[...], _CONTRACT_LAST, preferred_element_type=F32)
        da_ref[...] = dao[:, :A_CH]
        do_ref[...] = dao[:, A_CH:].astype(BF16)

    return pl.pallas_call(
        body, name="ev_bwd1", grid=(t // tm,),
        in_specs=[_rows(tm, D_MODEL), _rows(tm, D_MODEL), _full((1, D_MODEL)), _full((A_CH + Q_DIM, D_MODEL))],
        out_specs=[_rows(tm, D_MODEL), _rows(tm, A_CH), _rows(tm, Q_DIM), _full((1, D_MODEL))],
        out_shape=[jax.ShapeDtypeStruct((t, D_MODEL), BF16), jax.ShapeDtypeStruct((t, A_CH), F32),
                   jax.ShapeDtypeStruct((t, Q_DIM), BF16), jax.ShapeDtypeStruct((1, D_MODEL), F32)],
        compiler_params=_cp(("arbitrary",)),
    )(m, dxo, gpost, w_out)


def _conf_bwd(da, cv, zag, conv_w, ln_g, ln_b):
    t = da.shape[0]
    tm = min(TM, t)
    nt = t // tm
    rows = tm + HALO31 - 8

    def body(da_ref, c_ref, z_ref, w_ref, g_ref, lb_ref, dz_ref, dw_ref, dv_ref, ext_out, cbuf, glu_s, dglu_s):
        i = pl.program_id(0)

        @pl.when(i == 0)
        def _():
            ext_out[tm:tm + HALO31, :] = jnp.zeros((HALO31, A_CH), F32)
            dw_ref[...] = jnp.zeros((32, A_CH), F32)
            dv_ref[...] = jnp.zeros((8, A_CH), F32)

        x = c_ref[...]
        mu = jnp.mean(x, axis=-1, keepdims=True)
        xc = x - mu
        rstd = lax.rsqrt(jnp.mean(xc * xc, axis=-1, keepdims=True) + LN_EPS)
        xh = xc * rstd
        ln = xh * g_ref[...] + lb_ref[...]
        sl = _sigmoid(ln)
        dln = da_ref[...] * (sl * (1.0 + ln * (1.0 - sl)))
        dxh = dln * g_ref[...]
        dc = rstd * (dxh - jnp.mean(dxh, axis=-1, keepdims=True) - xh * jnp.mean(dxh * xh, axis=-1, keepdims=True))
        dv_ref[0:1, :] += jnp.sum(dc, axis=0, keepdims=True)
        dv_ref[1:2, :] += jnp.sum(dln * xh, axis=0, keepdims=True)
        dv_ref[2:3, :] += jnp.sum(dln, axis=0, keepdims=True)

        ext_out[0:tm, :] = dc
        _tap_copies(ext_out, cbuf, lambda b: b, rows)
        z = z_ref[...].astype(F32)
        al, sg = z[:, :A_CH], _sigmoid(z[:, A_CH:])
        glu_s[...] = al * sg
        for rs in range(0, tm, SUB):
            for cs in range(0, A_CH, 128):
                glu = glu_s[rs:rs + SUB, cs:cs + 128]
                acc = jnp.zeros((SUB, 128), F32)
                for k in range(A_CONV):
                    lag_a, lag_b = divmod(k, 8)
                    r0 = 8 * lag_a + rs
                    d = (ext_out[r0:r0 + SUB, cs:cs + 128] if lag_b == 0
                         else cbuf[lag_b - 1, r0:r0 + SUB, cs:cs + 128])
                    j = A_CONV - 1 - k
                    acc = acc + w_ref[j:j + 1, cs:cs + 128] * d
                    dw_ref[j:j + 1, cs:cs + 128] += jnp.sum(glu * d, axis=0, keepdims=True)
                dglu_s[rs:rs + SUB, cs:cs + 128] = acc
        dglu = dglu_s[...]
        ext_out[tm:tm + HALO31, :] = ext_out[0:HALO31, :]
        dz_ref[:, 0:A_CH] = (dglu * sg).astype(BF16)
        dz_ref[:, A_CH:2 * A_CH] = (dglu * al * sg * (1.0 - sg)).astype(BF16)

    rrow = lambda w: pl.BlockSpec((tm, w), lambda i: (nt - 1 - i, 0))
    return pl.pallas_call(
        body, name="conf_bwd", grid=(nt,),
        in_specs=[rrow(A_CH), rrow(A_CH), rrow(2 * A_CH), _full((32, A_CH)), _full((1, A_CH)), _full((1, A_CH))],
        out_specs=[rrow(2 * A_CH), _full((32, A_CH)), _full((8, A_CH))],
        out_shape=[jax.ShapeDtypeStruct((t, 2 * A_CH), BF16), jax.ShapeDtypeStruct((32, A_CH), F32),
                   jax.ShapeDtypeStruct((8, A_CH), F32)],
        scratch_shapes=[pltpu.VMEM((tm + HALO31, A_CH), F32), pltpu.VMEM((7, rows, A_CH), F32),
                        pltpu.VMEM((tm, A_CH), F32), pltpu.VMEM((tm, A_CH), F32)],
        compiler_params=_cp(("arbitrary",)),
    )(da, cv, zag, conv_w, ln_g, ln_b)


def _attn_bwd(q, k, v, do, sinks):
    t = q.shape[0]
    nb = min(ATT_NB, t // BLOCK)
    rows = nb * BLOCK
    ns = t // rows

    def body(s_ref, q_ref, kc_ref, kp_ref, vc_ref, vp_ref, do_ref, dq_ref, dk_ref, dv_ref, ds_ref, dkc, dvc):
        i = pl.program_id(0)
        r = ns - 1 - i

        @pl.when(i == 0)
        def _():
            dkc[...] = jnp.zeros_like(dkc)
            dvc[...] = jnp.zeros_like(dvc)
            ds_ref[...] = jnp.zeros_like(ds_ref)

        lane = lax.broadcasted_iota(jnp.int32, (1, N_Q_HEADS), 1)
        dsv = jnp.zeros((1, N_Q_HEADS), F32)
        for b in range(nb - 1, -1, -1):
            lo = BLOCK * b
            mask = _attn_mask(r == 0) if b == 0 else _attn_mask(False)
            qv, dov = q_ref[lo:lo + BLOCK, :], do_ref[lo:lo + BLOCK, :]
            kc, vc = kc_ref[lo:lo + BLOCK, :], vc_ref[lo:lo + BLOCK, :]
            kp = kp_ref[...] if b == 0 else kc_ref[lo - BLOCK:lo, :]
            vp = vp_ref[...] if b == 0 else vc_ref[lo - BLOCK:lo, :]
            for h in range(N_KV_HEADS):
                q4, do4 = _q_heads(qv, h), _q_heads(dov, h)
                k2, v2 = _kv_head(kp, kc, h), _kv_head(vp, vc, h)
                pn, ps = _attn_probs(q4, k2, mask, _sink_rows(s_ref, h))
                dp = lax.dot_general(do4, v2, _CONTRACT_LAST, preferred_element_type=F32)
                dl = jnp.sum(pn * dp, axis=-1, keepdims=True)
                dsb = (pn * (dp - dl)).astype(BF16)
                dq4 = (jnp.dot(dsb, k2, preferred_element_type=F32) * SCALE).astype(BF16)
                for g in range(GROUP):
                    c0 = HEAD_DIM * (GROUP * h + g)
                    dq_ref[lo:lo + BLOCK, c0:c0 + HEAD_DIM] = dq4[BLOCK * g:BLOCK * (g + 1), :]
                dk2 = lax.dot_general(dsb, q4, _CONTRACT_FIRST, preferred_element_type=F32) * SCALE
                dv2 = lax.dot_general(pn.astype(BF16), do4, _CONTRACT_FIRST, preferred_element_type=F32)
                dk_ref[lo:lo + BLOCK, HEAD_DIM * h:HEAD_DIM * (h + 1)] = dk2[BLOCK:, :] + dkc[h]
                dv_ref[lo:lo + BLOCK, HEAD_DIM * h:HEAD_DIM * (h + 1)] = dv2[BLOCK:, :] + dvc[h]
                dkc[h] = dk2[:BLOCK, :]
                dvc[h] = dv2[:BLOCK, :]
                srow = -ps * dl
                for g in range(GROUP):
                    dsv = dsv + jnp.where(lane == GROUP * h + g, jnp.sum(srow[BLOCK * g:BLOCK * (g + 1), :]), 0.0)
        ds_ref[...] += dsv

    cur = lambda n: pl.BlockSpec((rows, n), lambda i: (ns - 1 - i, 0))
    prev = lambda n: pl.BlockSpec((BLOCK, n), lambda i: (jnp.maximum((ns - 1 - i) * nb - 1, 0), 0))
    return pl.pallas_call(
        body, name="attn_bwd", grid=(ns,),
        in_specs=[pl.BlockSpec(memory_space=pltpu.SMEM), cur(Q_DIM), cur(KV_DIM), prev(KV_DIM), cur(KV_DIM),
                  prev(KV_DIM), cur(Q_DIM)],
        out_specs=[cur(Q_DIM), cur(KV_DIM), cur(KV_DIM), _full((1, N_Q_HEADS))],
        out_shape=[jax.ShapeDtypeStruct((t, Q_DIM), BF16), jax.ShapeDtypeStruct((t, KV_DIM), F32),
                   jax.ShapeDtypeStruct((t, KV_DIM), F32), jax.ShapeDtypeStruct((1, N_Q_HEADS), F32)],
        scratch_shapes=[pltpu.VMEM((N_KV_HEADS, BLOCK, HEAD_DIM), F32), pltpu.VMEM((N_KV_HEADS, BLOCK, HEAD_DIM), F32)],
        compiler_params=_cp(("arbitrary",)),
    )(sinks, q, k, k, v, v, do)


def _ev_dz(dzag, dq, dk, dv, rc, rsa, rsb):
    t = dzag.shape[0]
    tm = min(TM, t)

    def body(dzag_ref, dq_ref, dk_ref, dv_ref, c_ref, sa_ref, sb_ref, dz_ref):
        c, sa, sb = c_ref[...], sa_ref[...], sb_ref[...]
        dz_ref[:, 0:2 * A_CH] = dzag_ref[...]
        q0 = 2 * A_CH
        for j in range(Q_DIM // 128):
            d = dq_ref[:, 128 * j:128 * (j + 1)].astype(F32)
            dz_ref[:, q0 + 128 * j:q0 + 128 * (j + 1)] = _rope_bwd(d, c, sa, sb).astype(BF16)
        k0 = q0 + Q_DIM
        dz_ref[:, k0:k0 + KV_DIM] = _rope_bwd(dk_ref[...], c, sa, sb).astype(BF16)
        dz_ref[:, k0 + KV_DIM:k0 + 2 * KV_DIM] = dv_ref[...].astype(BF16)

    return pl.pallas_call(
        body, name="ev_dz", grid=(t // tm,),
        in_specs=[_rows(tm, 2 * A_CH), _rows(tm, Q_DIM), _rows(tm, KV_DIM), _rows(tm, KV_DIM),
                  _rows(tm, 128), _rows(tm, 128), _rows(tm, 128)],
        out_specs=_rows(tm, EVEN_IN),
        out_shape=jax.ShapeDtypeStruct((t, EVEN_IN), BF16),
        compiler_params=_cp(("arbitrary",)),
    )(dzag, dq, dk, dv, rc, rsa, rsb)


def _prep_ev(gat):
    p = {}
    p["ev_w_in"] = gat["ev_w_in"][:, 0].transpose(1, 0, 2).reshape(D_MODEL, EVEN_IN)
    p["ev_w_out"] = gat["ev_w_out"].reshape(A_CH + Q_DIM, D_MODEL)
    return p


def _prep_rest(gat):
    p = {}
    g = gat["od_w_in"][:, 0]
    p["od_w_in"] = g.transpose(1, 0, 2).reshape(1, D_MODEL, 3 * SC_DIM)
    p["od_w_in_t"] = g.transpose(0, 2, 1).reshape(3 * SC_DIM, D_MODEL)
    p["od_w_out"] = gat["od_w_out"].reshape(SC_DIM, D_MODEL)
    p["ffn_w_up"] = gat["ffn_w_up"].reshape(2, N_DEV // 2, 2, D_MODEL, FF_N)
    p["ffn_w_down"] = [gat["ffn_w_down"][:, i].reshape(D_FF, D_MODEL) for i in range(2)]
    return p


def _local_step(x, positions, target, p, rest_weights, s, token, grads_ready):
    row = lambda a, tok=None: a.reshape(1, -1) if tok is None else a.reshape(1, -1) + tok
    nc = N_DEV // 2
    rc, rsa, rsb = _rope_tables(positions)
    conv31 = jnp.pad(s["ev_a_conv_w"][0], ((0, 1), (0, 0)))
    cw_ffn = [s["ffn_conv_w"][i].reshape(3, 2, nc, FF_N).transpose(1, 2, 0, 3) for i in range(2)]
    sinks = s["ev_sinks"][0]
    big, g = {}, {}

    h0, zag, q, k, v = _ev_in(x, row(s["mix_norm_pre"][0], token), p["ev_w_in"], rc, rsa, rsb)
    cv, a = _conf_fwd(zag, conv31, s["ev_a_conv_b"], s["ev_a_ln_g"], s["ev_a_ln_b"])
    o = _attn_fwd(q, k, v, sinks)
    wo = p["ev_w_out"]
    m0, x1 = _out_post([a, o], [wo[:A_CH], wo[A_CH:]], x, row(s["mix_norm_post"][0]))
    p = {**p, **rest_weights(m0)}
    h1, up0, u0, f0, x2 = _ffn_fwd(x1, row(s["ffn_norm_pre"][0]), p["ffn_w_up"], 0, cw_ffn[0], p["ffn_w_down"][0],
                                   row(s["ffn_norm_post"][0]))
    h2, z, cv1, y, m1, x3 = _od_fwd(x2, row(s["mix_norm_pre"][1]), p["od_w_in"], s["od_conv_w"][0], p["od_w_out"],
                                    row(s["mix_norm_post"][1]))
    h3, up1, u1, f1, x4 = _ffn_fwd(x3, row(s["ffn_norm_pre"][1]), p["ffn_w_up"], 1, cw_ffn[1], p["ffn_w_down"][1],
                                   row(s["ffn_norm_post"][1]))

    def ffn_back(i, f, dxo, up, u, h, x_in, bufs, tok=None, tgt=None):
        df, act, dup, dx_in, dgpost, dgpre, dcw, *loss = _ffn_bwd(
            f, dxo, row(s["ffn_norm_post"][i], tok), x_in, row(s["ffn_norm_pre"][i]), up, u, cw_ffn[i],
            p["ffn_w_down"][i], p["ffn_w_up"], i, tgt)
        bufs = (_dw_up(h, dup.reshape(N_DEV, -1, FF_N), i, bufs[0]), _dw_down(act, df, i, bufs[1]))
        return dx_in, dgpost, dgpre, dcw[:, :, 0:3].transpose(2, 0, 1, 3).reshape(3, 2 * D_FF), bufs, loss

    dx, dgfpost1, dgfpre1, dcw1, bufs, (lpart,) = ffn_back(1, f1, x4, up1, u1, h3, x3, (None, None), None, target)

    dm1, dz, dx, dgpost1, dgpre1, dcw_od = _od_bwd(m1, dx, row(s["mix_norm_post"][1]), x2, row(s["mix_norm_pre"][1]), z,
                                                   cv1, s["od_conv_w"][0], p["od_w_out"], p["od_w_in_t"])
    big["od_w_out"] = _dw2d(y, dm1, SC_DIM, D_MODEL).reshape(N_DEV, -1, D_MODEL)
    big["od_w_in"], big["od_w_in:bf16"] = _dw_cols(h2, dz, 3 * SC_DIM // N_DEV)
    g["od_conv_w"] = dcw_od[None, 0:3]
    tok = grads_ready(["od_w_in", "od_w_out"], big)

    dx, dgfpost0, dgfpre0, dcw0, bufs, _ = ffn_back(0, f0, dx, up0, u0, h1, x1, bufs, tok)
    (big["ffn_w_up"], big["ffn_w_up:bf16"]), (big["ffn_w_down"], big["ffn_w_down:bf16"]) = bufs
    tok = grads_ready(["ffn_w_up", "ffn_w_down"], big)

    dm0, da, do, dgpost0 = _ev_bwd1(m0, dx, row(s["mix_norm_post"][0], tok), p["ev_w_out"])
    big["ev_w_out"] = jnp.concatenate([_dw2d(a, dm0, A_CH, D_MODEL), _dw2d(o, dm0, Q_DIM, D_MODEL)],
                                      axis=0).reshape(N_DEV, -1, D_MODEL)
    tok = grads_ready(["ev_w_out"], big)
    dzag, dcw31, dvec = _conf_bwd(da, cv, zag, conv31, s["ev_a_ln_g"] + tok, s["ev_a_ln_b"])
    dq, dk, dv, dsinks = _attn_bwd(q, k, v, do, sinks)
    dz0 = _ev_dz(dzag, dq, dk, dv, rc, rsa, rsb)
    dw_in = _dw2d(h0, dz0, D_MODEL, EVEN_IN // 2)
    big["ev_w_in"] = dw_in.reshape(D_MODEL, N_DEV, EVEN_IN // N_DEV).transpose(1, 0, 2)
    tok = grads_ready(["ev_w_in"], big)
    dx, dgpre0 = _dz_wt_rms_bwd(dz0, p["ev_w_in"], x, row(s["mix_norm_pre"][0], tok), dx)

    g["mix_norm_pre"] = jnp.concatenate([dgpre0, dgpre1], axis=0)
    g["mix_norm_post"] = jnp.concatenate([dgpost0, dgpost1], axis=0)
    g["ffn_norm_pre"] = jnp.concatenate([dgfpre0, dgfpre1], axis=0)
    g["ffn_norm_post"] = jnp.concatenate([dgfpost0, dgfpost1], axis=0)
    g["ev_a_conv_w"] = dcw31[None, 0:A_CONV]
    g["ev_a_conv_b"], g["ev_a_ln_g"], g["ev_a_ln_b"] = dvec[0:1], dvec[1:2], dvec[2:3]
    g["ev_sinks"] = dsinks
    g["ffn_conv_w"] = jnp.stack([dcw0, dcw1])
    return lpart[0, 0], dx, big, g


MESH = pl.DeviceIdType.MESH


def _all_gather(shards, name):
    nw = len(shards)

    def body(*refs):
        x_refs, out_refs = refs[:nw], refs[nw:2 * nw]
        send_sems, recv_sems, local_sems = refs[2 * nw:]
        x, y, c = lax.axis_index("x"), lax.axis_index("y"), lax.axis_index("c")
        me, sibling = (x, y, c), (x, y, 1 - c)
        chips = [(1 - x, y), (x, 1 - y), (1 - x, 1 - y)]

        def rows(w, px, py, pc):
            m_per = shards[w].shape[0]
            return out_refs[w].at[pl.ds((4 * px + 2 * py + pc) * m_per, m_per), :]

        def copy(w, k, block, to, src=None):
            return pltpu.make_async_remote_copy(
                src_ref=rows(w, *block) if src is None else src, dst_ref=rows(w, *block),
                send_sem=send_sems.at[w, k], recv_sem=recv_sems.at[w, k], device_id=to, device_id_type=MESH)

        mine, first, passed = [], [], []
        for w in range(nw):
            cp = pltpu.make_async_copy(x_refs[w], rows(w, *me), local_sems.at[w])
            cp.start()
            mine.append(cp)
            first.append([copy(w, 0, me, sibling, src=x_refs[w])]
                         + [copy(w, 1 + j, me, (*chip, c), src=x_refs[w]) for j, chip in enumerate(chips)])
            for cp in first[w]:
                cp.start()
        for w in range(nw):
            passed.append([copy(w, 4 + j, (*chip, c), sibling) for j, chip in enumerate(chips)])
            for j, chip in enumerate(chips):
                copy(w, 1 + j, (*chip, c), me).wait_recv()
                passed[w][j].start()
        for w in range(nw):
            copy(w, 0, sibling, me).wait_recv()
            for j, chip in enumerate(chips):
                copy(w, 4 + j, (*chip, 1 - c), me).wait_recv()
            for cp in first[w] + passed[w]:
                cp.wait_send()
            mine[w].wait()

    return pl.pallas_call(
        body, name=name,
        out_shape=[jax.ShapeDtypeStruct((N_DEV * a.shape[0], a.shape[1]), a.dtype) for a in shards],
        in_specs=[_ANY] * nw, out_specs=[_ANY] * nw,
        scratch_shapes=[pltpu.SemaphoreType.DMA((nw, 7)), pltpu.SemaphoreType.DMA((nw, 7)),
                        pltpu.SemaphoreType.DMA((nw,))],
    )(*shards)


_HBM = pl.BlockSpec(memory_space=pltpu.HBM)
_SEM = pl.BlockSpec(memory_space=pltpu.SEMAPHORE)
_EFFECT = pltpu.SideEffectType.DATAFLOW_SIDE_EFFECTING
_RELATIONS = [(dx, dy, dc) for dx in (0, 1) for dy in (0, 1) for dc in (0, 1)][1:]


def _peer(rel):
    x, y, c = lax.axis_index("x"), lax.axis_index("y"), lax.axis_index("c")
    px, py, pc = x ^ rel[0], y ^ rel[1], c ^ rel[2]
    return (px, py, pc), 4 * px + 2 * py + pc, 4 * x + 2 * y + c


def _exchange_copy(k, rel, src_ref, land_ref, send_sems, recv_sems, w, scatter):
    peer, peer_idx, my_idx = _peer(rel)
    src = src_ref.at[peer_idx] if scatter else src_ref
    return pltpu.make_async_remote_copy(
        src_ref=src, dst_ref=land_ref.at[my_idx], send_sem=send_sems.at[_sends(scatter) * w + k],
        recv_sem=recv_sems.at[7 * w + k], device_id=peer, device_id_type=MESH)


def _sends(scatter):
    return 7 if scatter else 8


def _own_copy(src_ref, land_ref, send_sems, w):
    my_idx = _peer(_RELATIONS[0])[2]
    return pltpu.make_async_copy(src_ref, land_ref.at[my_idx], send_sems.at[8 * w + 7])


def _exchange_start(srcs, scatter, name):
    nw = len(srcs)
    lands = [lax.empty((N_DEV,) + (a.shape[1:] if scatter else a.shape), a.dtype) for a in srcs]

    def body(*refs):
        src_refs, land_refs = refs[:nw], refs[nw:2 * nw]
        send_sems, recv_sems = refs[2 * nw], refs[2 * nw + 1]
        token = refs[-1]
        for w in range(nw):
            for k, rel in enumerate(_RELATIONS):
                _exchange_copy(k, rel, src_refs[w], land_refs[w], send_sems, recv_sems, w, scatter).start()
            if not scatter:
                _own_copy(src_refs[w], land_refs[w], send_sems, w).start()
        token[...] = jnp.zeros_like(token)

    hbm = lambda a: pltpu.HBM(a.shape, a.dtype)
    outs = pl.pallas_call(
        body, name=name,
        out_shape=(pltpu.SemaphoreType.DMA((_sends(scatter) * nw,)), pltpu.SemaphoreType.DMA((7 * nw,)),
                   *[hbm(a) for a in srcs],
                   *[hbm(a) for a in lands], jax.ShapeDtypeStruct((8, 128), F32)),
        in_specs=[_HBM] * (2 * nw),
        out_specs=(_SEM, _SEM, *[_HBM] * (2 * nw), pl.BlockSpec(memory_space=pltpu.VMEM)),
        input_output_aliases={i: 2 + i for i in range(2 * nw)},
        compiler_params=pltpu.CompilerParams(has_side_effects=_EFFECT),
    )(*[pltpu.with_memory_space_constraint(a, pltpu.HBM) for a in srcs],
      *[pltpu.with_memory_space_constraint(a, pltpu.HBM) for a in lands])
    return outs[0], outs[1], list(outs[2:2 + nw]), list(outs[2 + nw:2 + 2 * nw]), outs[-1]


def _exchange_wait(started, scatter, after, name):
    send_sems, recv_sems, srcs, lands, _ = started
    nw = len(srcs)

    def body(*refs):
        src_refs, land_refs = refs[:nw], refs[nw:2 * nw]
        send_s, recv_s = refs[2 * nw], refs[2 * nw + 1]
        for w in range(nw):
            for k, rel in enumerate(_RELATIONS):
                cp = _exchange_copy(k, rel, src_refs[w], land_refs[w], send_s, recv_s, w, scatter)
                cp.wait_send()
                _, peer_idx, _ = _peer(rel)
                pltpu.make_async_remote_copy(
                    src_ref=src_refs[w].at[peer_idx] if scatter else src_refs[w], dst_ref=land_refs[w].at[peer_idx],
                    send_sem=send_s.at[_sends(scatter) * w + k], recv_sem=recv_s.at[7 * w + k],
                    device_id=_peer(rel)[0], device_id_type=MESH).wait_recv()
            if not scatter:
                _own_copy(src_refs[w], land_refs[w], send_s, w).wait()

    hbm = lambda a: pltpu.HBM(a.shape, a.dtype)
    outs = pl.pallas_call(
        body, name=name, out_shape=tuple(hbm(a) for a in srcs + lands),
        in_specs=[_HBM] * (2 * nw) + [_SEM, _SEM, _ANY], out_specs=tuple([_HBM] * (2 * nw)),
        input_output_aliases={i: i for i in range(2 * nw)},
        compiler_params=pltpu.CompilerParams(has_side_effects=_EFFECT),
    )(*srcs, *lands, send_sems, recv_sems, after)
    return list(outs[nw:])


def _to_bf16(a):
    _, r, l = a.shape
    tr = _row_tile(r, 512)

    def body(a_ref, o_ref):
        o_ref[...] = a_ref[...].astype(BF16)

    spec = pl.BlockSpec((1, tr, l), lambda j, i: (j, i, 0))
    return pl.pallas_call(
        body, name="to_bf16", grid=(N_DEV, r // tr), in_specs=[spec], out_specs=spec,
        out_shape=jax.ShapeDtypeStruct(a.shape, BF16), compiler_params=_cp(("arbitrary", "arbitrary")),
    )(a)


def _row_tile(rows, cap):
    best = None
    for d in range(16, min(rows, cap) + 1, 16):
        if rows % d == 0:
            best = d
    return rows if best is None else best


def _adam_math(w, g, m, v):
    bc1 = 1.0 - ADAM_B1 ** ADAM_STEP
    bc2 = 1.0 - ADAM_B2 ** ADAM_STEP
    mn = ADAM_B1 * m + (1.0 - ADAM_B1) * g
    vn = ADAM_B2 * v + (1.0 - ADAM_B2) * (g * g)
    return -ADAM_LR * ((mn / bc1) / (jnp.sqrt(vn / bc2) + ADAM_EPS) + ADAM_WD * w), mn, vn


def _adamw_rs(gp, land, w, m, v, dev):
    _, r, l = gp.shape
    tr = _row_tile(r, 256)

    def body(i_ref, g_ref, b_ref, w_ref, m_ref, v_ref, go_ref, d_ref, mo_ref, vo_ref):
        g = g_ref[0]
        for j in range(N_DEV):
            g = g + jnp.where(i_ref[0] == j, 0.0, b_ref[j].astype(F32))
        go_ref[...] = g
        d_ref[...], mo_ref[...], vo_ref[...] = _adam_math(w_ref[...], g, m_ref[...], v_ref[...])

    spec = pl.BlockSpec((tr, l), lambda i, s: (i, 0))
    return pl.pallas_call(
        body, name="adamw_rs", out_shape=[jax.ShapeDtypeStruct((r, l), F32)] * 4,
        grid_spec=pltpu.PrefetchScalarGridSpec(
            num_scalar_prefetch=1, grid=(r // tr,),
            in_specs=[pl.BlockSpec((1, tr, l), lambda i, s: (s[0], i, 0)),
                      pl.BlockSpec((N_DEV, tr, l), lambda i, s: (0, i, 0)), spec, spec, spec],
            out_specs=[spec] * 4),
        compiler_params=_cp(("arbitrary",)),
    )(dev, gp, land, w, m, v)


def _sum_blocks(a, nblk):
    m = a.shape[0] // nblk
    n = a.shape[1]

    def body(a_ref, o_ref):
        acc = a_ref[0]
        for j in range(1, nblk):
            acc = acc + a_ref[j]
        o_ref[...] = acc

    return pl.pallas_call(
        body, name="sum_blocks", out_shape=jax.ShapeDtypeStruct((m, n), a.dtype),
        in_specs=[_full((nblk, m, n))], out_specs=_full((m, n)),
    )(a.reshape(nblk, m, n))


def _adamw(w, g, m, v):
    rows, c = w.shape

    def body(w_ref, g_ref, m_ref, v_ref, d_ref, mo_ref, vo_ref):
        d_ref[...], mo_ref[...], vo_ref[...] = _adam_math(w_ref[...], g_ref[...], m_ref[...], v_ref[...])

    return pl.pallas_call(
        body, name="adamw", in_specs=[_full((rows, c))] * 4, out_specs=[_full((rows, c))] * 3,
        out_shape=[jax.ShapeDtypeStruct((rows, c), F32)] * 3,
    )(w, g, m, v)


WEIGHTS = ["mix_norm_pre", "mix_norm_post", "ffn_norm_pre", "ffn_norm_post", "ev_w_in", "ev_a_conv_w", "ev_a_conv_b",
           "ev_a_ln_g", "ev_a_ln_b", "ev_sinks", "ev_w_out", "od_w_in", "od_conv_w", "od_w_out", "ffn_w_up",
           "ffn_conv_w", "ffn_w_down"]
BIG = ["ev_w_in", "ev_w_out", "od_w_in", "od_w_out", "ffn_w_up", "ffn_w_down"]
SMALL_REPL = ["mix_norm_pre", "mix_norm_post", "ffn_norm_pre", "ffn_norm_post", "ev_a_conv_b", "ev_a_ln_g",
              "ev_a_ln_b", "ev_sinks"]
SMALL_SHARDED = ["ev_a_conv_w", "od_conv_w", "ffn_conv_w"]


def _pack(arrs, rows):
    flat = jnp.concatenate([a.reshape(-1) for a in arrs])
    return jnp.pad(flat, (0, rows * LANES - flat.shape[0])).reshape(rows, LANES)


def _unpack(packed, shapes):
    flat, out, off = packed.reshape(-1), [], 0
    for s in shapes:
        n = 1
        for d in s:
            n *= d
        out.append(flat[off:off + n].reshape(s))
        off += n
    return out


def kernel(x, positions, mix_norm_pre, mix_norm_post, ffn_norm_pre, ffn_norm_post, ev_w_in, ev_a_conv_w, ev_a_conv_b, ev_a_ln_g, ev_a_ln_b, ev_sinks, ev_w_out, od_w_in, od_conv_w, od_w_out, ffn_w_up, ffn_conv_w, ffn_w_down, loss_target, m_mix_norm_pre, m_mix_norm_post, m_ffn_norm_pre, m_ffn_norm_post, m_ev_w_in, m_ev_a_conv_w, m_ev_a_conv_b, m_ev_a_ln_g, m_ev_a_ln_b, m_ev_sinks, m_ev_w_out, m_od_w_in, m_od_conv_w, m_od_w_out, m_ffn_w_up, m_ffn_conv_w, m_ffn_w_down, v_mix_norm_pre, v_mix_norm_post, v_ffn_norm_pre, v_ffn_norm_post, v_ev_w_in, v_ev_a_conv_w, v_ev_a_conv_b, v_ev_a_ln_g, v_ev_a_ln_b, v_ev_sinks, v_ev_w_out, v_od_w_in, v_od_conv_w, v_od_w_out, v_ffn_w_up, v_ffn_conv_w, v_ffn_w_down):
    w = dict(zip(WEIGHTS, (mix_norm_pre, mix_norm_post, ffn_norm_pre, ffn_norm_post, ev_w_in, ev_a_conv_w, ev_a_conv_b,
                           ev_a_ln_g, ev_a_ln_b, ev_sinks, ev_w_out, od_w_in, od_conv_w, od_w_out, ffn_w_up, ffn_conv_w,
                           ffn_w_down)))
    mom = dict(zip(WEIGHTS, (m_mix_norm_pre, m_mix_norm_post, m_ffn_norm_pre, m_ffn_norm_post, m_ev_w_in, m_ev_a_conv_w,
                             m_ev_a_conv_b, m_ev_a_ln_g, m_ev_a_ln_b, m_ev_sinks, m_ev_w_out, m_od_w_in, m_od_conv_w,
                             m_od_w_out, m_ffn_w_up, m_ffn_conv_w, m_ffn_w_down)))
    var = dict(zip(WEIGHTS, (v_mix_norm_pre, v_mix_norm_post, v_ffn_norm_pre, v_ffn_norm_post, v_ev_w_in, v_ev_a_conv_w,
                             v_ev_a_conv_b, v_ev_a_ln_g, v_ev_a_ln_b, v_ev_sinks, v_ev_w_out, v_od_w_in, v_od_conv_w,
                             v_od_w_out, v_ffn_w_up, v_ffn_conv_w, v_ffn_w_down)))
    ix, iy, ic = lax.axis_index("x"), lax.axis_index("y"), lax.axis_index("c")
    dev = 4 * ix + 2 * iy + ic
    two = lambda a: a.reshape(-1, a.shape[-1])

    dev1 = jnp.reshape(dev, (1,)).astype(jnp.int32)
    shard = {n: two(w[n].astype(BF16)) for n in BIG}
    gathered = lambda n, a: a.reshape((N_DEV,) + w[n].shape)
    ev_names = [n for n in BIG if n.startswith("ev_")]
    ev_gat = _all_gather([shard[n] for n in ev_names] + [_pack([w[n] for n in SMALL_SHARDED], 8)], "gather_ev")
    p = _prep_ev({n: gathered(n, a) for n, a in zip(ev_names, ev_gat)})
    rest_names = [n for n in BIG if not n.startswith("ev_")]
    first = shard[rest_names[0]] + (ev_gat[0][0:1, 0:1] * 0).astype(BF16)
    started = _exchange_start([first] + [shard[n] for n in rest_names[1:]], False, "gather_start")

    def rest_weights(after):
        lands = _exchange_wait(started, False, after, "gather_wait")
        return _prep_rest({n: gathered(n, a) for n, a in zip(rest_names, lands)})

    small = {n: w[n] for n in SMALL_REPL}
    small_shapes = [w[n].shape for n in SMALL_SHARDED]
    conv_gat = ev_gat[len(ev_names)].reshape(N_DEV, 8, LANES)
    per_dev = [_unpack(conv_gat[d], small_shapes) for d in range(N_DEV)]
    for k, n in enumerate(SMALL_SHARDED):
        small[n] = jnp.concatenate([per_dev[d][k] for d in range(N_DEV)], axis=-1)

    exchanges = []

    def grads_ready(names, big):
        blocks = lambda a, n: a.reshape(N_DEV, -1, w[n].shape[-1])
        bufs = [blocks(big[n], n) for n in names]
        payload = [blocks(big[n + ":bf16"], n) if n + ":bf16" in big else _to_bf16(b) for n, b in zip(names, bufs)]
        st = _exchange_start(payload, True, "grads_start_" + names[0])
        exchanges.append((names, bufs, st))
        return st[-1][0, 0]

    lpart, grad_x, big, g = _local_step(x[0], positions[0], loss_target[0], p, rest_weights, small, started[-1][0, 0],
                                        grads_ready)
    loss = lax.psum(lpart, ("x", "y", "c"))

    grads, delta, new_m, new_v = {}, {}, {}, {}
    for names, bufs, st in exchanges:
        lands = _exchange_wait(st, True, grad_x, "grads_wait_" + names[0])
        for n, b, land in zip(names, bufs, lands):
            outs = _adamw_rs(b, land, two(w[n]), two(mom[n]), two(var[n]), dev1)
            grads[n], delta[n], new_m[n], new_v[n] = (a.reshape(w[n].shape) for a in outs)

    small_names = SMALL_REPL + SMALL_SHARDED
    s_all = _sum_blocks(_all_gather([_pack([g[n] for n in small_names], 64)], "gather_small_grads")[0], N_DEV)
    for n, a in zip(small_names, _unpack(s_all, [small[n].shape for n in small_names])):
        if n in SMALL_SHARDED:
            width = w[n].shape[-1]
            a = lax.dynamic_slice_in_dim(a, dev * width, width, axis=a.ndim - 1)
        grads[n] = a
    pk = lambda dct: _pack([dct[n] for n in small_names], 16)
    outs = _adamw(pk(w), pk(grads), pk(mom), pk(var))
    for dst, packed in zip((delta, new_m, new_v), outs):
        for n, a in zip(small_names, _unpack(packed, [w[n].shape for n in small_names])):
            dst[n] = a

    return (loss, grad_x[None], *[grads[n] for n in WEIGHTS], *[delta[n] for n in WEIGHTS],
            *[new_m[n] for n in WEIGHTS], *[new_v[n] for n in WEIGHTS])
```

```python
import jax
import jax.numpy as jnp
from jax import lax
from jax.experimental import pallas as pl
from jax.experimental.pallas import tpu as pltpu

F32, BF16 = jnp.float32, jnp.bfloat16

D_MODEL = 1024
A_CH = 512
A_CONV = 31
Q_DIM = 512
KV_DIM = 128
HEAD_DIM = 64
N_Q_HEADS = 8
N_KV_HEADS = 2
GROUP = 4
BLOCK = 128
EVEN_IN = 1792
SC_DIM = 1024
D_FF = 2816
ROPE_THETA = 500000.0
ROPE_DIM = 16
RMS_EPS = 1e-6
LN_EPS = 1e-5
SCALE = HEAD_DIM ** -0.5
NEG = -1e30

ADAM_LR, ADAM_B1, ADAM_B2, ADAM_EPS, ADAM_WD, ADAM_STEP = 0.001, 0.9, 0.999, 1e-08, 0.01, 10

N_DEV = 8
FF_N = 2 * D_FF // N_DEV
LANES = 1024
HALO3 = 8
HALO31 = 32
VMEM_LIMIT = 56 * 1024 * 1024

TM = 512
TM_BWD = 256
TK_DW = 4096
ATT_NB = 4
SUB = 128

_ANY = pl.BlockSpec(memory_space=pl.ANY)
_CONTRACT_LAST = (((1,), (1,)), ((), ()))
_CONTRACT_FIRST = (((0,), (0,)), ((), ()))


def _cp(sem, vmem=VMEM_LIMIT):
    return pltpu.CompilerParams(dimension_semantics=sem, vmem_limit_bytes=vmem)


def _full(shape):
    n = len(shape)
    return pl.BlockSpec(shape, lambda *_: (0,) * n)


def _rows(tm, n):
    return pl.BlockSpec((tm, n), lambda i, *_: (i, 0))


def _sigmoid(x):
    return 0.5 * jnp.tanh(0.5 * x) + 0.5


def _rsqrt_mean(x):
    return lax.rsqrt(jnp.mean(x * x, axis=-1, keepdims=True) + RMS_EPS)


def _rms_bwd(x, g, dy):
    r = _rsqrt_mean(x)
    xh = x * r
    dxh = dy * g
    dx = r * (dxh - xh * jnp.mean(dxh * xh, axis=-1, keepdims=True))
    return dx, jnp.sum(dy * xh, axis=0, keepdims=True)


def _acc_out(ref, first, val):
    @pl.when(first)
    def _():
        ref[...] = val

    @pl.when(jnp.logical_not(first))
    def _():
        ref[...] += val


def _rope_tables(positions):
    half = ROPE_DIM // 2
    inv_freq = ROPE_THETA ** (-(jnp.arange(half, dtype=F32) * 2.0 / ROPE_DIM))
    ang = positions.astype(F32)[:, None] * inv_freq
    cos, sin = jnp.cos(ang), jnp.sin(ang)
    t = positions.shape[0]
    one, zero = jnp.ones((t, HEAD_DIM - ROPE_DIM), F32), jnp.zeros((t, HEAD_DIM - ROPE_DIM), F32)
    z8 = jnp.zeros((t, half), F32)
    c = jnp.concatenate([cos, cos, one], axis=1)
    sa = jnp.concatenate([z8, sin, zero], axis=1)
    sb = jnp.concatenate([-sin, z8, zero], axis=1)
    return tuple(jnp.tile(a, (1, 2)) for a in (c, sa, sb))


def _rope(t, c, sa, sb):
    return t * c + pltpu.roll(t, 8, 1) * sa + pltpu.roll(t, 120, 1) * sb


def _rope_bwd(d, c, sa, sb):
    return d * c + pltpu.roll(d * sa, 120, 1) + pltpu.roll(d * sb, 8, 1)


def _ev_in(x, gpre, w_in, rc, rsa, rsb):
    t = x.shape[0]
    tm = min(TM, t)

    def body(x_ref, g_ref, w_ref, c_ref, sa_ref, sb_ref, h_ref, zag_ref, q_ref, k_ref, v_ref):
        xv = x_ref[...]
        h = (xv * _rsqrt_mean(xv) * g_ref[...]).astype(BF16)
        h_ref[...] = h
        z = jnp.dot(h, w_ref[...], preferred_element_type=F32)
        zag_ref[...] = z[:, :2 * A_CH].astype(BF16)
        c, sa, sb = c_ref[...], sa_ref[...], sb_ref[...]
        q0 = 2 * A_CH
        for j in range(Q_DIM // 128):
            q_ref[:, 128 * j:128 * (j + 1)] = _rope(z[:, q0 + 128 * j:q0 + 128 * (j + 1)], c, sa, sb).astype(BF16)
        k0 = q0 + Q_DIM
        k_ref[...] = _rope(z[:, k0:k0 + KV_DIM], c, sa, sb).astype(BF16)
        v_ref[...] = z[:, k0 + KV_DIM:k0 + 2 * KV_DIM].astype(BF16)

    return pl.pallas_call(
        body, name="ev_in", grid=(t // tm,),
        in_specs=[_rows(tm, D_MODEL), _full((1, D_MODEL)), _full((D_MODEL, EVEN_IN)),
                  _rows(tm, 128), _rows(tm, 128), _rows(tm, 128)],
        out_specs=[_rows(tm, D_MODEL), _rows(tm, 2 * A_CH), _rows(tm, Q_DIM), _rows(tm, KV_DIM), _rows(tm, KV_DIM)],
        out_shape=[jax.ShapeDtypeStruct((t, D_MODEL), BF16), jax.ShapeDtypeStruct((t, 2 * A_CH), BF16),
                   jax.ShapeDtypeStruct((t, Q_DIM), BF16), jax.ShapeDtypeStruct((t, KV_DIM), BF16),
                   jax.ShapeDtypeStruct((t, KV_DIM), BF16)],
        compiler_params=_cp(("arbitrary",)),
    )(x, gpre, w_in, rc, rsa, rsb)


def _glu(zag):
    z = zag.astype(F32)
    return z[:, :A_CH] * _sigmoid(z[:, A_CH:])


def _tap_copies(ext, cbuf, first_row, rows):
    for b in range(1, 8):
        s = first_row(b)
        cbuf[b - 1] = ext[s:s + rows, :]


def _conf_fwd(zag, conv_w, conv_b, ln_g, ln_b):
    t = zag.shape[0]
    tm = min(TM, t)
    rows = tm + HALO31 - 8

    def body(z_ref, w_ref, b_ref, g_ref, lb_ref, c_ref, a_ref, ext, cbuf):
        i = pl.program_id(0)

        @pl.when(i == 0)
        def _():
            ext[0:HALO31, :] = jnp.zeros((HALO31, A_CH), F32)

        ext[HALO31:HALO31 + tm, :] = _glu(z_ref[...])
        _tap_copies(ext, cbuf, lambda b: 8 - b, rows)
        for rs in range(0, tm, SUB):
            for cs in range(0, A_CH, 128):
                acc = jnp.zeros((SUB, 128), F32)
                for k in range(A_CONV):
                    lag_a, lag_b = divmod(k, 8)
                    r0 = HALO31 - 8 - 8 * lag_a + rs
                    src = (ext[r0 + 8:r0 + 8 + SUB, cs:cs + 128] if lag_b == 0
                           else cbuf[lag_b - 1, r0:r0 + SUB, cs:cs + 128])
                    acc = acc + w_ref[A_CONV - 1 - k:A_CONV - k, cs:cs + 128] * src
                c_ref[rs:rs + SUB, cs:cs + 128] = acc
        ext[0:HALO31, :] = ext[tm:tm + HALO31, :]
        cv = c_ref[...] + b_ref[...]
        c_ref[...] = cv
        mu = jnp.mean(cv, axis=-1, keepdims=True)
        xc = cv - mu
        ln = xc * lax.rsqrt(jnp.mean(xc * xc, axis=-1, keepdims=True) + LN_EPS) * g_ref[...] + lb_ref[...]
        a_ref[...] = (ln * _sigmoid(ln)).astype(BF16)

    return pl.pallas_call(
        body, name="conf_fwd", grid=(t // tm,),
        in_specs=[_rows(tm, 2 * A_CH), _full((32, A_CH)), _full((1, A_CH)), _full((1, A_CH)), _full((1, A_CH))],
        out_specs=[_rows(tm, A_CH), _rows(tm, A_CH)],
        out_shape=[jax.ShapeDtypeStruct((t, A_CH), F32), jax.ShapeDtypeStruct((t, A_CH), BF16)],
        scratch_shapes=[pltpu.VMEM((HALO31 + tm, A_CH), F32), pltpu.VMEM((7, rows, A_CH), F32)],
        compiler_params=_cp(("arbitrary",)),
    )(zag, conv_w, conv_b, ln_g, ln_b)


def _attn_mask(first_block):
    row = lax.broadcasted_iota(jnp.int32, (GROUP * BLOCK, 2 * BLOCK), 0) & (BLOCK - 1)
    col = lax.broadcasted_iota(jnp.int32, (GROUP * BLOCK, 2 * BLOCK), 1)
    diff = row + BLOCK - col
    return (diff >= 0) & (diff < BLOCK) & ((col >= BLOCK) | jnp.logical_not(first_block))


def _sink_rows(s_ref, h):
    grp = lax.broadcasted_iota(jnp.int32, (GROUP * BLOCK, 1), 0) >> 7
    out = jnp.full((GROUP * BLOCK, 1), s_ref[GROUP * h], F32)
    for g in range(1, GROUP):
        out = jnp.where(grp == g, s_ref[GROUP * h + g], out)
    return out


def _attn_probs(q4, k2, mask, sink):
    s = lax.dot_general(q4, k2, _CONTRACT_LAST, preferred_element_type=F32) * SCALE
    s = jnp.where(mask, s, NEG)
    m = jnp.maximum(jnp.max(s, axis=-1, keepdims=True), sink)
    p = jnp.exp(s - m)
    es = jnp.exp(sink - m)
    inv = 1.0 / (jnp.sum(p, axis=-1, keepdims=True) + es)
    return p * inv, es * inv


def _q_heads(q, h):
    return jnp.concatenate([q[:, HEAD_DIM * (GROUP * h + g):HEAD_DIM * (GROUP * h + g + 1)] for g in range(GROUP)],
                           axis=0)


def _kv_head(prev, cur, h):
    return jnp.concatenate([prev[:, HEAD_DIM * h:HEAD_DIM * (h + 1)], cur[:, HEAD_DIM * h:HEAD_DIM * (h + 1)]], axis=0)


def _attn_fwd(q, k, v, sinks):
    t = q.shape[0]
    nb = min(ATT_NB, t // BLOCK)
    rows = nb * BLOCK

    def body(s_ref, q_ref, kc_ref, kp_ref, vc_ref, vp_ref, o_ref):
        first = pl.program_id(0) == 0
        for b in range(nb):
            lo = BLOCK * b
            mask = _attn_mask(first) if b == 0 else _attn_mask(False)
            qv, kc, vc = q_ref[lo:lo + BLOCK, :], kc_ref[lo:lo + BLOCK, :], vc_ref[lo:lo + BLOCK, :]
            kp = kp_ref[...] if b == 0 else kc_ref[lo - BLOCK:lo, :]
            vp = vp_ref[...] if b == 0 else vc_ref[lo - BLOCK:lo, :]
            for h in range(N_KV_HEADS):
                pn, _ = _attn_probs(_q_heads(qv, h), _kv_head(kp, kc, h), mask, _sink_rows(s_ref, h))
                o4 = jnp.dot(pn.astype(BF16), _kv_head(vp, vc, h), preferred_element_type=F32).astype(BF16)
                for g in range(GROUP):
                    c0 = HEAD_DIM * (GROUP * h + g)
                    o_ref[lo:lo + BLOCK, c0:c0 + HEAD_DIM] = o4[BLOCK * g:BLOCK * (g + 1), :]

    cur = lambda n: pl.BlockSpec((rows, n), lambda i: (i, 0))
    prev = lambda n: pl.BlockSpec((BLOCK, n), lambda i: (jnp.maximum(i * nb - 1, 0), 0))
    return pl.pallas_call(
        body, name="attn_fwd", grid=(t // rows,),
        in_specs=[pl.BlockSpec(memory_space=pltpu.SMEM), cur(Q_DIM), cur(KV_DIM), prev(KV_DIM), cur(KV_DIM),
                  prev(KV_DIM)],
        out_specs=cur(Q_DIM),
        out_shape=jax.ShapeDtypeStruct((t, Q_DIM), BF16),
        compiler_params=_cp(("arbitrary",)),
    )(sinks, q, k, k, v, v)


def _out_post(lhs, ws, x_in, gpost):
    t = x_in.shape[0]
    tm = min(TM, t)
    n = len(lhs)

    def body(*refs):
        x_ref, g_ref, m_ref, xo_ref = refs[2 * n:]
        m = jnp.dot(refs[0][...], refs[n][...], preferred_element_type=F32)
        for j in range(1, n):
            m = m + jnp.dot(refs[j][...], refs[n + j][...], preferred_element_type=F32)
        m_ref[...] = m.astype(BF16)
        xo_ref[...] = x_ref[...] + m * _rsqrt_mean(m) * g_ref[...]

    return pl.pallas_call(
        body, name="out_post", grid=(t // tm,),
        in_specs=[_rows(tm, a.shape[1]) for a in lhs] + [_full(w.shape) for w in ws]
                 + [_rows(tm, D_MODEL), _full((1, D_MODEL))],
        out_specs=[_rows(tm, D_MODEL), _rows(tm, D_MODEL)],
        out_shape=[jax.ShapeDtypeStruct((t, D_MODEL), BF16), jax.ShapeDtypeStruct((t, D_MODEL), F32)],
        compiler_params=_cp(("arbitrary",)),
    )(*lhs, *ws, x_in, gpost)


def _conv3(w_ref, ext, tm):
    s = HALO3 - 2
    return (w_ref[0:1, :] * ext[s:s + tm, :] + w_ref[1:2, :] * ext[s + 1:s + 1 + tm, :]
            + w_ref[2:3, :] * ext[s + 2:s + 2 + tm, :])


def _ffn_fwd(x1, gpre, wup, layer, cw, wd, gpost):
    t = x1.shape[0]
    tm = min(TM, t)
    nc, n = wup.shape[1], wup.shape[4]

    def body(x_ref, gpre_ref, wup_ref, cw_ref, wd_ref, gpost_ref, h_ref, up_ref, u_ref, f_ref, xo_ref, h_s, acc, ext, hal):
        i, c = pl.program_id(0), pl.program_id(1)

        @pl.when(c == 0)
        def _():
            xv = x_ref[...]
            h = (xv * _rsqrt_mean(xv) * gpre_ref[...]).astype(BF16)
            h_s[...] = h
            h_ref[...] = h

        @pl.when(i == 0)
        def _():
            hal[c] = jnp.zeros((2, HALO3, n), F32)

        u = []
        for gv in range(2):
            up = jnp.dot(h_s[...], wup_ref[gv, 0, 0], preferred_element_type=F32)
            up_ref[gv, 0] = up.astype(BF16)
            ext[gv, 0:HALO3, :] = hal[c, gv]
            ext[gv, HALO3:HALO3 + tm, :] = up
            hal[c, gv] = ext[gv, tm:tm + HALO3, :]
            s = HALO3 - 2
            u.append(cw_ref[gv, 0, 0:1, :] * ext[gv, s:s + tm, :] + cw_ref[gv, 0, 1:2, :] * ext[gv, s + 1:s + 1 + tm, :]
                     + cw_ref[gv, 0, 2:3, :] * up)
            u_ref[gv, 0] = u[gv].astype(BF16)
        act = (u[0] * _sigmoid(u[0]) * u[1]).astype(BF16)
        part = jnp.dot(act, wd_ref[...], preferred_element_type=F32)

        @pl.when(c == 0)
        def _():
            acc[...] = part

        @pl.when(jnp.logical_and(c > 0, c < nc - 1))
        def _():
            acc[...] += part

        @pl.when(c == nc - 1)
        def _():
            f = acc[...] + part
            f_ref[...] = f
            xo_ref[...] = x_ref[...] + f * _rsqrt_mean(f) * gpost_ref[...]

    row = lambda w: pl.BlockSpec((tm, w), lambda i, c: (i, 0))
    one = _full((1, D_MODEL))
    return pl.pallas_call(
        body, name="ffn_fwd", grid=(t // tm, nc),
        in_specs=[row(D_MODEL), one, pl.BlockSpec((2, 1, 1, D_MODEL, n), lambda i, c: (0, c, layer, 0, 0)),
                  pl.BlockSpec((2, 1, 3, n), lambda i, c: (0, c, 0, 0)), pl.BlockSpec((n, D_MODEL), lambda i, c: (c, 0)),
                  one],
        out_specs=[row(D_MODEL), pl.BlockSpec((2, 1, tm, n), lambda i, c: (0, c, i, 0)),
                   pl.BlockSpec((2, 1, tm, n), lambda i, c: (0, c, i, 0)), row(D_MODEL), row(D_MODEL)],
        out_shape=[jax.ShapeDtypeStruct((t, D_MODEL), BF16), jax.ShapeDtypeStruct((2, nc, t, n), BF16),
                   jax.ShapeDtypeStruct((2, nc, t, n), BF16), jax.ShapeDtypeStruct((t, D_MODEL), F32),
                   jax.ShapeDtypeStruct((t, D_MODEL), F32)],
        scratch_shapes=[pltpu.VMEM((tm, D_MODEL), BF16), pltpu.VMEM((tm, D_MODEL), F32),
                        pltpu.VMEM((2, HALO3 + tm, n), F32), pltpu.VMEM((nc, 2, HALO3, n), F32)],
        compiler_params=_cp(("arbitrary", "arbitrary")),
    )(x1, gpre, wup, cw, wd, gpost)


def _od_fwd(x_in, gpre, w_in, cw, w_out, gpost):
    t = x_in.shape[0]
    tm = min(TM, t)
    ns, _, n = w_in.shape

    def body(x_ref, gpre_ref, w_ref, cw_ref, wo_ref, gpost_ref, h_ref, z_ref, cv_ref, y_ref, m_ref, xo_ref, z_s, ext):
        i = pl.program_id(0)
        xv = x_ref[...]
        h = (xv * _rsqrt_mean(xv) * gpre_ref[...]).astype(BF16)
        h_ref[...] = h
        for j in range(ns):
            z_s[:, n * j:n * (j + 1)] = jnp.dot(h, w_ref[j], preferred_element_type=F32)
        z_ref[...] = z_s[...].astype(BF16)

        @pl.when(i == 0)
        def _():
            ext[0:HALO3, :] = jnp.zeros((HALO3, SC_DIM), F32)

        ext[HALO3:HALO3 + tm, :] = z_s[:, SC_DIM:2 * SC_DIM] * z_s[:, 2 * SC_DIM:]
        cv = _conv3(cw_ref, ext, tm)
        cv_ref[...] = cv.astype(BF16)
        y = (z_s[:, :SC_DIM] * cv).astype(BF16)
        ext[0:HALO3, :] = ext[tm:tm + HALO3, :]
        y_ref[...] = y
        m = jnp.dot(y, wo_ref[...], preferred_element_type=F32)
        m_ref[...] = m
        xo_ref[...] = xv + m * _rsqrt_mean(m) * gpost_ref[...]

    return pl.pallas_call(
        body, name="od_fwd", grid=(t // tm,),
        in_specs=[_rows(tm, D_MODEL), _full((1, D_MODEL)), _full((ns, D_MODEL, n)), _full((3, SC_DIM)),
                  _full((SC_DIM, D_MODEL)), _full((1, D_MODEL))],
        out_specs=[_rows(tm, D_MODEL), _rows(tm, 3 * SC_DIM), _rows(tm, SC_DIM), _rows(tm, SC_DIM), _rows(tm, D_MODEL),
                   _rows(tm, D_MODEL)],
        out_shape=[jax.ShapeDtypeStruct((t, D_MODEL), BF16), jax.ShapeDtypeStruct((t, 3 * SC_DIM), BF16),
                   jax.ShapeDtypeStruct((t, SC_DIM), BF16), jax.ShapeDtypeStruct((t, SC_DIM), BF16),
                   jax.ShapeDtypeStruct((t, D_MODEL), F32), jax.ShapeDtypeStruct((t, D_MODEL), F32)],
        scratch_shapes=[pltpu.VMEM((tm, 3 * SC_DIM), F32), pltpu.VMEM((HALO3 + tm, SC_DIM), F32)],
        compiler_params=_cp(("arbitrary",)),
    )(x_in, gpre, w_in, cw, w_out, gpost)


def _dw2d(a, b, bm, bn):
    t, m = a.shape
    n = b.shape[1]
    tk = min(TK_DW, t)

    def body(a_ref, b_ref, o_ref):
        part = lax.dot_general(a_ref[...], b_ref[...], _CONTRACT_FIRST, preferred_element_type=F32)
        _acc_out(o_ref, pl.program_id(2) == 0, part)

    return pl.pallas_call(
        body, name="dw2d", grid=(m // bm, n // bn, t // tk),
        in_specs=[pl.BlockSpec((tk, bm), lambda i, j, k: (k, i)), pl.BlockSpec((tk, bn), lambda i, j, k: (k, j))],
        out_specs=pl.BlockSpec((bm, bn), lambda i, j, k: (i, j)),
        out_shape=jax.ShapeDtypeStruct((m, n), F32),
        compiler_params=_cp(("arbitrary", "arbitrary", "arbitrary")),
    )(a, b)


def _dw_cols(a, b, n_blk):
    t, m = a.shape
    s = b.shape[1] // n_blk
    tk = min(TK_DW, t)
    nk = t // tk

    def body(a_ref, b_ref, o_ref, ob_ref):
        part = lax.dot_general(a_ref[...], b_ref[...], _CONTRACT_FIRST, preferred_element_type=F32)
        for j in range(2):
            _acc_out(o_ref.at[j], pl.program_id(1) == 0, part[:, n_blk * j:n_blk * (j + 1)])

        @pl.when(pl.program_id(1) == nk - 1)
        def _():
            ob_ref[...] = o_ref[...].astype(BF16)

    spec = pl.BlockSpec((2, m, n_blk), lambda j, k: (j, 0, 0))
    return pl.pallas_call(
        body, name="dw_cols", grid=(s // 2, nk),
        in_specs=[pl.BlockSpec((tk, m), lambda j, k: (k, 0)), pl.BlockSpec((tk, 2 * n_blk), lambda j, k: (k, j))],
        out_specs=[spec, spec],
        out_shape=[jax.ShapeDtypeStruct((s, m, n_blk), F32), jax.ShapeDtypeStruct((s, m, n_blk), BF16)],
        compiler_params=_cp(("arbitrary", "arbitrary")),
    )(a, b)


def _dw_up(h, dup, layer, buf):
    t, m = h.shape
    s, _, n = dup.shape
    tk = min(TK_DW, t)
    nk = t // tk

    def body(*refs):
        a_ref, b_ref, o_ref, ob_ref = refs[0], refs[1], refs[-2], refs[-1]
        part = lax.dot_general(a_ref[...], b_ref[0], _CONTRACT_FIRST, preferred_element_type=F32)
        _acc_out(o_ref.at[0, 0], pl.program_id(1) == 0, part)

        @pl.when(pl.program_id(1) == nk - 1)
        def _():
            ob_ref[...] = o_ref[...].astype(BF16)

    spec = pl.BlockSpec((1, 1, m, n), lambda j, k: (j, layer, 0, 0))
    return pl.pallas_call(
        body, name="dw_up", grid=(s, nk),
        in_specs=[pl.BlockSpec((tk, m), lambda j, k: (k, 0)), pl.BlockSpec((1, tk, n), lambda j, k: (j, k, 0))]
                 + ([] if buf is None else [_ANY, _ANY]),
        out_specs=[spec, spec],
        out_shape=[jax.ShapeDtypeStruct((s, 2, m, n), F32), jax.ShapeDtypeStruct((s, 2, m, n), BF16)],
        input_output_aliases={} if buf is None else {2: 0, 3: 1},
        compiler_params=_cp(("arbitrary", "arbitrary")),
    )(h, dup, *([] if buf is None else buf))


def _dw_down(act, df, layer, buf):
    nc, t, n = act.shape
    d = df.shape[1]
    tk = min(TK_DW, t)
    nk = t // tk

    def body(*refs):
        a_ref, b_ref, o_ref, ob_ref = refs[0], refs[1], refs[-2], refs[-1]
        part = lax.dot_general(a_ref[0], b_ref[...], _CONTRACT_FIRST, preferred_element_type=F32)
        part = part.reshape(2, n // 2, d)
        first = pl.program_id(1) == 0

        @pl.when(first)
        def _():
            o_ref[:, 0] = part

        @pl.when(jnp.logical_not(first))
        def _():
            o_ref[:, 0] += part

        @pl.when(pl.program_id(1) == nk - 1)
        def _():
            ob_ref[...] = o_ref[...].astype(BF16)

    spec = pl.BlockSpec((2, 1, n // 2, d), lambda c, k: (c, layer, 0, 0))
    return pl.pallas_call(
        body, name="dw_down", grid=(nc, nk),
        in_specs=[pl.BlockSpec((1, tk, n), lambda c, k: (c, k, 0)), pl.BlockSpec((tk, d), lambda c, k: (k, 0))]
                 + ([] if buf is None else [_ANY, _ANY]),
        out_specs=[spec, spec],
        out_shape=[jax.ShapeDtypeStruct((2 * nc, 2, n // 2, d), F32), jax.ShapeDtypeStruct((2 * nc, 2, n // 2, d), BF16)],
        input_output_aliases={} if buf is None else {2: 0, 3: 1},
        compiler_params=_cp(("arbitrary", "arbitrary")),
    )(act, df, *([] if buf is None else buf))


def _dz_wt_rms_bwd(dz, w, x_in, gpre, dres):
    t, n = dz.shape
    tm = min(TM, t)

    def body(dz_ref, wt_ref, x_ref, g_ref, dres_ref, dx_ref, dg_ref):
        dh = lax.dot_general(dz_ref[...], wt_ref[...], _CONTRACT_LAST, preferred_element_type=F32)
        dx, dg = _rms_bwd(x_ref[...], g_ref[...], dh)
        dx_ref[...] = dres_ref[...] + dx
        _acc_out(dg_ref, pl.program_id(0) == 0, dg)

    return pl.pallas_call(
        body, name="dz_wt_rms_bwd", grid=(t // tm,),
        in_specs=[_rows(tm, n), _full((D_MODEL, n)), _rows(tm, D_MODEL), _full((1, D_MODEL)), _rows(tm, D_MODEL)],
        out_specs=[_rows(tm, D_MODEL), _full((1, D_MODEL))],
        out_shape=[jax.ShapeDtypeStruct((t, D_MODEL), F32), jax.ShapeDtypeStruct((1, D_MODEL), F32)],
        compiler_params=_cp(("arbitrary",)),
    )(dz, w, x_in, gpre, dres)


def _shift_matrices(shift, shift_h, tm, hb):
    row = lax.broadcasted_iota(jnp.int32, (2 * tm, tm), 0)
    col = lax.broadcasted_iota(jnp.int32, (2 * tm, tm), 1)
    hit = ((row < tm) & (col == row + 1)) | ((row >= tm) & (col == row - tm + 2))
    shift[...] = jnp.where(hit, 1.0, 0.0).astype(BF16)
    row = lax.broadcasted_iota(jnp.int32, (hb, hb), 0)
    col = lax.broadcasted_iota(jnp.int32, (hb, hb), 1)
    hit = ((row < HALO3) & (col == row - (HALO3 - 1))) | ((row >= HALO3) & (col == row - (2 * HALO3 - 2)))
    shift_h[...] = jnp.where(hit, 1.0, 0.0).astype(BF16)


def _next_rows(shift, shift_h, xb, nxt, d12_s, tm):
    d12_s[...] = jnp.dot(shift[...], xb, preferred_element_type=F32)
    edge = jnp.dot(shift_h[...], nxt, preferred_element_type=F32)
    d12_s[tm - HALO3:tm, :] += edge[0:HALO3, :]
    d12_s[2 * tm - HALO3:2 * tm, :] += edge[HALO3:2 * HALO3, :]


def _ffn_bwd(f, dxo, gpost, x_in, gpre, up, u, cw, wd, wup, layer, target=None):
    t = f.shape[0]
    tm = min(TM_BWD, t)
    nt = t // tm
    nc, n = up.shape[1], up.shape[3]
    hb = 2 * HALO3

    def body(*refs):
        f_ref, dxo_ref, gpost_ref, x_ref, gpre_ref, up_ref, u_ref, cw_ref, wd_ref, wup_ref = refs[:10]
        n_in = 10 if target is None else 11
        n_out = 7 if target is None else 8
        df_ref, act_ref, dup_ref, dx_ref, dgpost_ref, dgpre_ref, dcw_ref = refs[n_in:n_in + 7]
        df_s, acc, du_s, dub_s, d12_s, hal, shift, shift_h = refs[n_in + n_out:]
        i, c = pl.program_id(0), pl.program_id(1)

        def incoming():
            if target is None:
                return dxo_ref[...]
            return (dxo_ref[...] - refs[10][...]) * (1.0 / D_MODEL)

        @pl.when(c == 0)
        def _():
            dy = incoming()
            df, dg = _rms_bwd(f_ref[...], gpost_ref[...], dy)
            df_s[...] = df.astype(BF16)
            df_ref[...] = df.astype(BF16)
            _acc_out(dgpost_ref, i == 0, dg)
            if target is not None:
                part = jnp.zeros((1, 128), F32) + jnp.sum(dy * dy) * (0.5 * D_MODEL)
                _acc_out(refs[n_in + 7], i == 0, part)

        @pl.when(i == 0)
        def _():
            hal[c] = jnp.zeros((2, hb, n), BF16)
            dcw_ref[0, c] = jnp.zeros((8, n), F32)
            dcw_ref[1, c] = jnp.zeros((8, n), F32)

        @pl.when(jnp.logical_and(i == 0, c == 0))
        def _():
            _shift_matrices(shift, shift_h, tm, hb)

        dact = lax.dot_general(df_s[...], wd_ref[...], _CONTRACT_LAST, preferred_element_type=F32)
        g, v = u_ref[0, 0].astype(F32), u_ref[1, 0].astype(F32)
        sg = _sigmoid(g)
        sil = g * sg
        act_ref[0] = (sil * v).astype(BF16)
        dug = dact * v * (sg + sil * (1.0 - sg))
        duv = dact * sil
        du_s[0], du_s[1] = dug, duv
        dub_s[0], dub_s[1] = dug.astype(BF16), duv.astype(BF16)
        dh = None
        for gv in range(2):
            _next_rows(shift, shift_h, dub_s[gv], hal[c, gv], d12_s, tm)
            hal[c, gv] = dub_s[gv, 0:hb, :]
            du, d1, d2 = du_s[gv], d12_s[0:tm, :], d12_s[tm:2 * tm, :]
            dup = (cw_ref[gv, 0, 2:3, :] * du + cw_ref[gv, 0, 1:2, :] * d1 + cw_ref[gv, 0, 0:1, :] * d2).astype(BF16)
            dup_ref[gv, 0] = dup
            upc = up_ref[gv, 0].astype(F32)
            dcw_ref[gv, c, 2:3, :] += jnp.sum(upc * du, axis=0, keepdims=True)
            dcw_ref[gv, c, 1:2, :] += jnp.sum(upc * d1, axis=0, keepdims=True)
            dcw_ref[gv, c, 0:1, :] += jnp.sum(upc * d2, axis=0, keepdims=True)
            part = lax.dot_general(dup, wup_ref[gv, 0, 0], _CONTRACT_LAST, preferred_element_type=F32)
            dh = part if dh is None else dh + part
        _acc_out(acc, c == 0, dh)

        @pl.when(c == nc - 1)
        def _():
            dx, dg = _rms_bwd(x_ref[...], gpre_ref[...], acc[...])
            dx_ref[...] = incoming() + dx
            _acc_out(dgpre_ref, i == 0, dg)

    rrow = lambda w: pl.BlockSpec((tm, w), lambda i, c: (nt - 1 - i, 0))
    blk = pl.BlockSpec((2, 1, tm, n), lambda i, c: (0, c, nt - 1 - i, 0))
    one = _full((1, D_MODEL))
    return pl.pallas_call(
        body, name="ffn_bwd", grid=(nt, nc),
        in_specs=[rrow(D_MODEL), rrow(D_MODEL), one, rrow(D_MODEL), one, blk, blk,
                  pl.BlockSpec((2, 1, 3, n), lambda i, c: (0, c, 0, 0)),
                  pl.BlockSpec((n, D_MODEL), lambda i, c: (c, 0)),
                  pl.BlockSpec((2, 1, 1, D_MODEL, n), lambda i, c: (0, c, layer, 0, 0))]
                 + ([] if target is None else [rrow(D_MODEL)]),
        out_specs=[rrow(D_MODEL), pl.BlockSpec((1, tm, n), lambda i, c: (c, nt - 1 - i, 0)), blk, rrow(D_MODEL),
                   one, one, _full((2, nc, 8, n))] + ([] if target is None else [_full((1, 128))]),
        out_shape=[jax.ShapeDtypeStruct((t, D_MODEL), BF16), jax.ShapeDtypeStruct((nc, t, n), BF16),
                   jax.ShapeDtypeStruct((2, nc, t, n), BF16), jax.ShapeDtypeStruct((t, D_MODEL), F32),
                   jax.ShapeDtypeStruct((1, D_MODEL), F32), jax.ShapeDtypeStruct((1, D_MODEL), F32),
                   jax.ShapeDtypeStruct((2, nc, 8, n), F32)]
                  + ([] if target is None else [jax.ShapeDtypeStruct((1, 128), F32)]),
        scratch_shapes=[pltpu.VMEM((tm, D_MODEL), BF16), pltpu.VMEM((tm, D_MODEL), F32),
                        pltpu.VMEM((2, tm, n), F32), pltpu.VMEM((2, tm, n), BF16), pltpu.VMEM((2 * tm, n), F32),
                        pltpu.VMEM((nc, 2, hb, n), BF16), pltpu.VMEM((2 * tm, tm), BF16), pltpu.VMEM((hb, hb), BF16)],
        compiler_params=_cp(("arbitrary", "arbitrary")),
    )(f, dxo, gpost, x_in, gpre, up, u, cw, wd, wup, *([] if target is None else [target]))


def _od_bwd(m, dxo, gpost, x_in, gpre, z, cv, cw, w_out, wint):
    t = m.shape[0]
    tm = min(TM_BWD, t)
    nt = t // tm
    hb = 2 * HALO3

    def body(m_ref, dxo_ref, gpost_ref, x_ref, gpre_ref, z_ref, cv_ref, cw_ref, wo_ref, wi_ref,
             dm_ref, dz_ref, dx_ref, dgpost_ref, dgpre_ref, dcw_ref, dcvb_s, d12_s, dz_s, hal, shift, shift_h):
        i = pl.program_id(0)
        dxo = dxo_ref[...]
        dm, dg = _rms_bwd(m_ref[...], gpost_ref[...], dxo)
        dmb = dm.astype(BF16)
        dm_ref[...] = dmb
        _acc_out(dgpost_ref, i == 0, dg)

        @pl.when(i == 0)
        def _():
            hal[...] = jnp.zeros((hb, SC_DIM), BF16)
            dcw_ref[...] = jnp.zeros((8, SC_DIM), F32)
            _shift_matrices(shift, shift_h, tm, hb)

        dy = lax.dot_general(dmb, wo_ref[...], _CONTRACT_LAST, preferred_element_type=F32)
        z = z_ref[...].astype(F32)
        b, cg, u = z[:, :SC_DIM], z[:, SC_DIM:2 * SC_DIM], z[:, 2 * SC_DIM:]
        dz_s[:, 0:SC_DIM] = (dy * cv_ref[...].astype(F32)).astype(BF16)
        dcv = dy * b
        dcvb_s[...] = dcv.astype(BF16)
        _next_rows(shift, shift_h, dcvb_s[...], hal[...], d12_s, tm)
        hal[...] = dcvb_s[0:hb, :]
        d1, d2 = d12_s[0:tm, :], d12_s[tm:2 * tm, :]
        dcu = cw_ref[2:3, :] * dcv + cw_ref[1:2, :] * d1 + cw_ref[0:1, :] * d2
        cu = cg * u
        dcw_ref[2:3, :] += jnp.sum(cu * dcv, axis=0, keepdims=True)
        dcw_ref[1:2, :] += jnp.sum(cu * d1, axis=0, keepdims=True)
        dcw_ref[0:1, :] += jnp.sum(cu * d2, axis=0, keepdims=True)
        dz_s[:, SC_DIM:2 * SC_DIM] = (dcu * u).astype(BF16)
        dz_s[:, 2 * SC_DIM:3 * SC_DIM] = (dcu * cg).astype(BF16)
        dz_ref[...] = dz_s[...]
        dh = jnp.dot(dz_s[...], wi_ref[...], preferred_element_type=F32)
        dx, dg2 = _rms_bwd(x_ref[...], gpre_ref[...], dh)
        dx_ref[...] = dxo + dx
        _acc_out(dgpre_ref, i == 0, dg2)

    rrow = lambda w: pl.BlockSpec((tm, w), lambda i: (nt - 1 - i, 0))
    one = _full((1, D_MODEL))
    return pl.pallas_call(
        body, name="od_bwd", grid=(nt,),
        in_specs=[rrow(D_MODEL), rrow(D_MODEL), one, rrow(D_MODEL), one, rrow(3 * SC_DIM), rrow(SC_DIM),
                  _full((3, SC_DIM)), _full((SC_DIM, D_MODEL)), _full((3 * SC_DIM, D_MODEL))],
        out_specs=[rrow(D_MODEL), rrow(3 * SC_DIM), rrow(D_MODEL), one, one, _full((8, SC_DIM))],
        out_shape=[jax.ShapeDtypeStruct((t, D_MODEL), BF16), jax.ShapeDtypeStruct((t, 3 * SC_DIM), BF16),
                   jax.ShapeDtypeStruct((t, D_MODEL), F32), jax.ShapeDtypeStruct((1, D_MODEL), F32),
                   jax.ShapeDtypeStruct((1, D_MODEL), F32), jax.ShapeDtypeStruct((8, SC_DIM), F32)],
        scratch_shapes=[pltpu.VMEM((tm, SC_DIM), BF16), pltpu.VMEM((2 * tm, SC_DIM), F32),
                        pltpu.VMEM((tm, 3 * SC_DIM), BF16), pltpu.VMEM((hb, SC_DIM), BF16),
                        pltpu.VMEM((2 * tm, tm), BF16), pltpu.VMEM((hb, hb), BF16)],
        compiler_params=_cp(("arbitrary",)),
    )(m, dxo, gpost, x_in, gpre, z, cv, cw, w_out, wint)


def _ev_bwd1(m, dxo, gpost, w_out):
    t = m.shape[0]
    tm = min(TM, t)

    def body(m_ref, dxo_ref, g_ref, wot_ref, dm_ref, da_ref, do_ref, dg_ref):
        dm, dg = _rms_bwd(m_ref[...].astype(F32), g_ref[...], dxo_ref[...])
        dmb = dm.astype(BF16)
        dm_ref[...] = dmb
        _acc_out(dg_ref, pl.program_id(0) == 0, dg)
        dao = lax.dot_general(dmb, wot_ref[...], _CONTRACT_LAST, preferred_element_type=F32)
        da_ref[...] = dao[:, :A_CH]
        do_ref[...] = dao[:, A_CH:].astype(BF16)

    return pl.pallas_call(
        body, name="ev_bwd1", grid=(t // tm,),
        in_specs=[_rows(tm, D_MODEL), _rows(tm, D_MODEL), _full((1, D_MODEL)), _full((A_CH + Q_DIM, D_MODEL))],
        out_specs=[_rows(tm, D_MODEL), _rows(tm, A_CH), _rows(tm, Q_DIM), _full((1, D_MODEL))],
        out_shape=[jax.ShapeDtypeStruct((t, D_MODEL), BF16), jax.ShapeDtypeStruct((t, A_CH), F32),
                   jax.ShapeDtypeStruct((t, Q_DIM), BF16), jax.ShapeDtypeStruct((1, D_MODEL), F32)],
        compiler_params=_cp(("arbitrary",)),
    )(m, dxo, gpost, w_out)


def _conf_bwd(da, cv, zag, conv_w, ln_g, ln_b):
    t = da.shape[0]
    tm = min(TM, t)
    nt = t // tm
    rows = tm + HALO31 - 8

    def body(da_ref, c_ref, z_ref, w_ref, g_ref, lb_ref, dz_ref, dw_ref, dv_ref, ext_out, cbuf, glu_s, dglu_s):
        i = pl.program_id(0)

        @pl.when(i == 0)
        def _():
            ext_out[tm:tm + HALO31, :] = jnp.zeros((HALO31, A_CH), F32)
            dw_ref[...] = jnp.zeros((32, A_CH), F32)
            dv_ref[...] = jnp.zeros((8, A_CH), F32)

        x = c_ref[...]
        mu = jnp.mean(x, axis=-1, keepdims=True)
        xc = x - mu
        rstd = lax.rsqrt(jnp.mean(xc * xc, axis=-1, keepdims=True) + LN_EPS)
        xh = xc * rstd
        ln = xh * g_ref[...] + lb_ref[...]
        sl = _sigmoid(ln)
        dln = da_ref[...] * (sl * (1.0 + ln * (1.0 - sl)))
        dxh = dln * g_ref[...]
        dc = rstd * (dxh - jnp.mean(dxh, axis=-1, keepdims=True) - xh * jnp.mean(dxh * xh, axis=-1, keepdims=True))
        dv_ref[0:1, :] += jnp.sum(dc, axis=0, keepdims=True)
        dv_ref[1:2, :] += jnp.sum(dln * xh, axis=0, keepdims=True)
        dv_ref[2:3, :] += jnp.sum(dln, axis=0, keepdims=True)

        ext_out[0:tm, :] = dc
        _tap_copies(ext_out, cbuf, lambda b: b, rows)
        z = z_ref[...].astype(F32)
        al, sg = z[:, :A_CH], _sigmoid(z[:, A_CH:])
        glu_s[...] = al * sg
        for rs in range(0, tm, SUB):
            for cs in range(0, A_CH, 128):
                glu = glu_s[rs:rs + SUB, cs:cs + 128]
                acc = jnp.zeros((SUB, 128), F32)
                for k in range(A_CONV):
                    lag_a, lag_b = divmod(k, 8)
                    r0 = 8 * lag_a + rs
                    d = (ext_out[r0:r0 + SUB, cs:cs + 128] if lag_b == 0
                         else cbuf[lag_b - 1, r0:r0 + SUB, cs:cs + 128])
                    j = A_CONV - 1 - k
                    acc = acc + w_ref[j:j + 1, cs:cs + 128] * d
                    dw_ref[j:j + 1, cs:cs + 128] += jnp.sum(glu * d, axis=0, keepdims=True)
                dglu_s[rs:rs + SUB, cs:cs + 128] = acc
        dglu = dglu_s[...]
        ext_out[tm:tm + HALO31, :] = ext_out[0:HALO31, :]
        dz_ref[:, 0:A_CH] = (dglu * sg).astype(BF16)
        dz_ref[:, A_CH:2 * A_CH] = (dglu * al * sg * (1.0 - sg)).astype(BF16)

    rrow = lambda w: pl.BlockSpec((tm, w), lambda i: (nt - 1 - i, 0))
    return pl.pallas_call(
        body, name="conf_bwd", grid=(nt,),
        in_specs=[rrow(A_CH), rrow(A_CH), rrow(2 * A_CH), _full((32, A_CH)), _full((1, A_CH)), _full((1, A_CH))],
        out_specs=[rrow(2 * A_CH), _full((32, A_CH)), _full((8, A_CH))],
        out_shape=[jax.ShapeDtypeStruct((t, 2 * A_CH), BF16), jax.ShapeDtypeStruct((32, A_CH), F32),
                   jax.ShapeDtypeStruct((8, A_CH), F32)],
        scratch_shapes=[pltpu.VMEM((tm + HALO31, A_CH), F32), pltpu.VMEM((7, rows, A_CH), F32),
                        pltpu.VMEM((tm, A_CH), F32), pltpu.VMEM((tm, A_CH), F32)],
        compiler_params=_cp(("arbitrary",)),
    )(da, cv, zag, conv_w, ln_g, ln_b)


def _attn_bwd(q, k, v, do, sinks):
    t = q.shape[0]
    nb = min(ATT_NB, t // BLOCK)
    rows = nb * BLOCK
    ns = t // rows

    def body(s_ref, q_ref, kc_ref, kp_ref, vc_ref, vp_ref, do_ref, dq_ref, dk_ref, dv_ref, ds_ref, dkc, dvc):
        i = pl.program_id(0)
        r = ns - 1 - i

        @pl.when(i == 0)
        def _():
            dkc[...] = jnp.zeros_like(dkc)
            dvc[...] = jnp.zeros_like(dvc)
            ds_ref[...] = jnp.zeros_like(ds_ref)

        lane = lax.broadcasted_iota(jnp.int32, (1, N_Q_HEADS), 1)
        dsv = jnp.zeros((1, N_Q_HEADS), F32)
        for b in range(nb - 1, -1, -1):
            lo = BLOCK * b
            mask = _attn_mask(r == 0) if b == 0 else _attn_mask(False)
            qv, dov = q_ref[lo:lo + BLOCK, :], do_ref[lo:lo + BLOCK, :]
            kc, vc = kc_ref[lo:lo + BLOCK, :], vc_ref[lo:lo + BLOCK, :]
            kp = kp_ref[...] if b == 0 else kc_ref[lo - BLOCK:lo, :]
            vp = vp_ref[...] if b == 0 else vc_ref[lo - BLOCK:lo, :]
            for h in range(N_KV_HEADS):
                q4, do4 = _q_heads(qv, h), _q_heads(dov, h)
                k2, v2 = _kv_head(kp, kc, h), _kv_head(vp, vc, h)
                pn, ps = _attn_probs(q4, k2, mask, _sink_rows(s_ref, h))
                dp = lax.dot_general(do4, v2, _CONTRACT_LAST, preferred_element_type=F32)
                dl = jnp.sum(pn * dp, axis=-1, keepdims=True)
                dsb = (pn * (dp - dl)).astype(BF16)
                dq4 = (jnp.dot(dsb, k2, preferred_element_type=F32) * SCALE).astype(BF16)
                for g in range(GROUP):
                    c0 = HEAD_DIM * (GROUP * h + g)
                    dq_ref[lo:lo + BLOCK, c0:c0 + HEAD_DIM] = dq4[BLOCK * g:BLOCK * (g + 1), :]
                dk2 = lax.dot_general(dsb, q4, _CONTRACT_FIRST, preferred_element_type=F32) * SCALE
                dv2 = lax.dot_general(pn.astype(BF16), do4, _CONTRACT_FIRST, preferred_element_type=F32)
                dk_ref[lo:lo + BLOCK, HEAD_DIM * h:HEAD_DIM * (h + 1)] = dk2[BLOCK:, :] + dkc[h]
                dv_ref[lo:lo + BLOCK, HEAD_DIM * h:HEAD_DIM * (h + 1)] = dv2[BLOCK:, :] + dvc[h]
                dkc[h] = dk2[:BLOCK, :]
                dvc[h] = dv2[:BLOCK, :]
                srow = -ps * dl
                for g in range(GROUP):
                    dsv = dsv + jnp.where(lane == GROUP * h + g, jnp.sum(srow[BLOCK * g:BLOCK * (g + 1), :]), 0.0)
        ds_ref[...] += dsv

    cur = lambda n: pl.BlockSpec((rows, n), lambda i: (ns - 1 - i, 0))
    prev = lambda n: pl.BlockSpec((BLOCK, n), lambda i: (jnp.maximum((ns - 1 - i) * nb - 1, 0), 0))
    return pl.pallas_call(
        body, name="attn_bwd", grid=(ns,),
        in_specs=[pl.BlockSpec(memory_space=pltpu.SMEM), cur(Q_DIM), cur(KV_DIM), prev(KV_DIM), cur(KV_DIM),
                  prev(KV_DIM), cur(Q_DIM)],
        out_specs=[cur(Q_DIM), cur(KV_DIM), cur(KV_DIM), _full((1, N_Q_HEADS))],
        out_shape=[jax.ShapeDtypeStruct((t, Q_DIM), BF16), jax.ShapeDtypeStruct((t, KV_DIM), F32),
                   jax.ShapeDtypeStruct((t, KV_DIM), F32), jax.ShapeDtypeStruct((1, N_Q_HEADS), F32)],
        scratch_shapes=[pltpu.VMEM((N_KV_HEADS, BLOCK, HEAD_DIM), F32), pltpu.VMEM((N_KV_HEADS, BLOCK, HEAD_DIM), F32)],
        compiler_params=_cp(("arbitrary",)),
    )(sinks, q, k, k, v, v, do)


def _ev_dz(dzag, dq, dk, dv, rc, rsa, rsb):
    t = dzag.shape[0]
    tm = min(TM, t)

    def body(dzag_ref, dq_ref, dk_ref, dv_ref, c_ref, sa_ref, sb_ref, dz_ref):
        c, sa, sb = c_ref[...], sa_ref[...], sb_ref[...]
        dz_ref[:, 0:2 * A_CH] = dzag_ref[...]
        q0 = 2 * A_CH
        for j in range(Q_DIM // 128):
            d = dq_ref[:, 128 * j:128 * (j + 1)].astype(F32)
            dz_ref[:, q0 + 128 * j:q0 + 128 * (j + 1)] = _rope_bwd(d, c, sa, sb).astype(BF16)
        k0 = q0 + Q_DIM
        dz_ref[:, k0:k0 + KV_DIM] = _rope_bwd(dk_ref[...], c, sa, sb).astype(BF16)
        dz_ref[:, k0 + KV_DIM:k0 + 2 * KV_DIM] = dv_ref[...].astype(BF16)

    return pl.pallas_call(
        body, name="ev_dz", grid=(t // tm,),
        in_specs=[_rows(tm, 2 * A_CH), _rows(tm, Q_DIM), _rows(tm, KV_DIM), _rows(tm, KV_DIM),
                  _rows(tm, 128), _rows(tm, 128), _rows(tm, 128)],
        out_specs=_rows(tm, EVEN_IN),
        out_shape=jax.ShapeDtypeStruct((t, EVEN_IN), BF16),
        compiler_params=_cp(("arbitrary",)),
    )(dzag, dq, dk, dv, rc, rsa, rsb)


def _prep_ev(gat):
    p = {}
    p["ev_w_in"] = gat["ev_w_in"][:, 0].transpose(1, 0, 2).reshape(D_MODEL, EVEN_IN)
    p["ev_w_out"] = gat["ev_w_out"].reshape(A_CH + Q_DIM, D_MODEL)
    return p


def _prep_rest(gat):
    p = {}
    g = gat["od_w_in"][:, 0]
    p["od_w_in"] = g.transpose(1, 0, 2).reshape(1, D_MODEL, 3 * SC_DIM)
    p["od_w_in_t"] = g.transpose(0, 2, 1).reshape(3 * SC_DIM, D_MODEL)
    p["od_w_out"] = gat["od_w_out"].reshape(SC_DIM, D_MODEL)
    p["ffn_w_up"] = gat["ffn_w_up"].reshape(2, N_DEV // 2, 2, D_MODEL, FF_N)
    p["ffn_w_down"] = [gat["ffn_w_down"][:, i].reshape(D_FF, D_MODEL) for i in range(2)]
    return p


def _local_step(x, positions, target, p, rest_weights, s, token, grads_ready):
    row = lambda a, tok=None: a.reshape(1, -1) if tok is None else a.reshape(1, -1) + tok
    nc = N_DEV // 2
    rc, rsa, rsb = _rope_tables(positions)
    conv31 = jnp.pad(s["ev_a_conv_w"][0], ((0, 1), (0, 0)))
    cw_ffn = [s["ffn_conv_w"][i].reshape(3, 2, nc, FF_N).transpose(1, 2, 0, 3) for i in range(2)]
    sinks = s["ev_sinks"][0]
    big, g = {}, {}

    h0, zag, q, k, v = _ev_in(x, row(s["mix_norm_pre"][0], token), p["ev_w_in"], rc, rsa, rsb)
    cv, a = _conf_fwd(zag, conv31, s["ev_a_conv_b"], s["ev_a_ln_g"], s["ev_a_ln_b"])
    o = _attn_fwd(q, k, v, sinks)
    wo = p["ev_w_out"]
    m0, x1 = _out_post([a, o], [wo[:A_CH], wo[A_CH:]], x, row(s["mix_norm_post"][0]))
    p = {**p, **rest_weights(m0)}
    h1, up0, u0, f0, x2 = _ffn_fwd(x1, row(s["ffn_norm_pre"][0]), p["ffn_w_up"], 0, cw_ffn[0], p["ffn_w_down"][0],
                                   row(s["ffn_norm_post"][0]))
    h2, z, cv1, y, m1, x3 = _od_fwd(x2, row(s["mix_norm_pre"][1]), p["od_w_in"], s["od_conv_w"][0], p["od_w_out"],
                                    row(s["mix_norm_post"][1]))
    h3, up1, u1, f1, x4 = _ffn_fwd(x3, row(s["ffn_norm_pre"][1]), p["ffn_w_up"], 1, cw_ffn[1], p["ffn_w_down"][1],
                                   row(s["ffn_norm_post"][1]))

    def ffn_back(i, f, dxo, up, u, h, x_in, bufs, tok=None, tgt=None):
        df, act, dup, dx_in, dgpost, dgpre, dcw, *loss = _ffn_bwd(
            f, dxo, row(s["ffn_norm_post"][i], tok), x_in, row(s["ffn_norm_pre"][i]), up, u, cw_ffn[i],
            p["ffn_w_down"][i], p["ffn_w_up"], i, tgt)
        bufs = (_dw_up(h, dup.reshape(N_DEV, -1, FF_N), i, bufs[0]), _dw_down(act, df, i, bufs[1]))
        return dx_in, dgpost, dgpre, dcw[:, :, 0:3].transpose(2, 0, 1, 3).reshape(3, 2 * D_FF), bufs, loss

    dx, dgfpost1, dgfpre1, dcw1, bufs, (lpart,) = ffn_back(1, f1, x4, up1, u1, h3, x3, (None, None), None, target)

    dm1, dz, dx, dgpost1, dgpre1, dcw_od = _od_bwd(m1, dx, row(s["mix_norm_post"][1]), x2, row(s["mix_norm_pre"][1]), z,
                                                   cv1, s["od_conv_w"][0], p["od_w_out"], p["od_w_in_t"])
    big["od_w_out"] = _dw2d(y, dm1, SC_DIM, D_MODEL).reshape(N_DEV, -1, D_MODEL)
    big["od_w_in"], big["od_w_in:bf16"] = _dw_cols(h2, dz, 3 * SC_DIM // N_DEV)
    g["od_conv_w"] = dcw_od[None, 0:3]
    tok = grads_ready(["od_w_in", "od_w_out"], big)

    dx, dgfpost0, dgfpre0, dcw0, bufs, _ = ffn_back(0, f0, dx, up0, u0, h1, x1, bufs, tok)
    (big["ffn_w_up"], big["ffn_w_up:bf16"]), (big["ffn_w_down"], big["ffn_w_down:bf16"]) = bufs
    tok = grads_ready(["ffn_w_up", "ffn_w_down"], big)

    dm0, da, do, dgpost0 = _ev_bwd1(m0, dx, row(s["mix_norm_post"][0], tok), p["ev_w_out"])
    big["ev_w_out"] = jnp.concatenate([_dw2d(a, dm0, A_CH, D_MODEL), _dw2d(o, dm0, Q_DIM, D_MODEL)],
                                      axis=0).reshape(N_DEV, -1, D_MODEL)
    tok = grads_ready(["ev_w_out"], big)
    dzag, dcw31, dvec = _conf_bwd(da, cv, zag, conv31, s["ev_a_ln_g"] + tok, s["ev_a_ln_b"])
    dq, dk, dv, dsinks = _attn_bwd(q, k, v, do, sinks)
    dz0 = _ev_dz(dzag, dq, dk, dv, rc, rsa, rsb)
    dw_in = _dw2d(h0, dz0, D_MODEL, EVEN_IN // 2)
    big["ev_w_in"] = dw_in.reshape(D_MODEL, N_DEV, EVEN_IN // N_DEV).transpose(1, 0, 2)
    tok = grads_ready(["ev_w_in"], big)
    dx, dgpre0 = _dz_wt_rms_bwd(dz0, p["ev_w_in"], x, row(s["mix_norm_pre"][0], tok), dx)

    g["mix_norm_pre"] = jnp.concatenate([dgpre0, dgpre1], axis=0)
    g["mix_norm_post"] = jnp.concatenate([dgpost0, dgpost1], axis=0)
    g["ffn_norm_pre"] = jnp.concatenate([dgfpre0, dgfpre1], axis=0)
    g["ffn_norm_post"] = jnp.concatenate([dgfpost0, dgfpost1], axis=0)
    g["ev_a_conv_w"] = dcw31[None, 0:A_CONV]
    g["ev_a_conv_b"], g["ev_a_ln_g"], g["ev_a_ln_b"] = dvec[0:1], dvec[1:2], dvec[2:3]
    g["ev_sinks"] = dsinks
    g["ffn_conv_w"] = jnp.stack([dcw0, dcw1])
    return lpart[0, 0], dx, big, g


MESH = pl.DeviceIdType.MESH


def _all_gather(shards, name):
    nw = len(shards)

    def body(*refs):
        x_refs, out_refs = refs[:nw], refs[nw:2 * nw]
        send_sems, recv_sems, local_sems = refs[2 * nw:]
        x, y, c = lax.axis_index("x"), lax.axis_index("y"), lax.axis_index("c")
        me, sibling = (x, y, c), (x, y, 1 - c)
        chips = [(1 - x, y), (x, 1 - y), (1 - x, 1 - y)]

        def rows(w, px, py, pc):
            m_per = shards[w].shape[0]
            return out_refs[w].at[pl.ds((4 * px + 2 * py + pc) * m_per, m_per), :]

        def copy(w, k, block, to, src=None):
            return pltpu.make_async_remote_copy(
                src_ref=rows(w, *block) if src is None else src, dst_ref=rows(w, *block),
                send_sem=send_sems.at[w, k], recv_sem=recv_sems.at[w, k], device_id=to, device_id_type=MESH)

        mine, first, passed = [], [], []
        for w in range(nw):
            cp = pltpu.make_async_copy(x_refs[w], rows(w, *me), local_sems.at[w])
            cp.start()
            mine.append(cp)
            first.append([copy(w, 0, me, sibling, src=x_refs[w])]
                         + [copy(w, 1 + j, me, (*chip, c), src=x_refs[w]) for j, chip in enumerate(chips)])
            for cp in first[w]:
                cp.start()
        for w in range(nw):
            passed.append([copy(w, 4 + j, (*chip, c), sibling) for j, chip in enumerate(chips)])
            for j, chip in enumerate(chips):
                copy(w, 1 + j, (*chip, c), me).wait_recv()
                passed[w][j].start()
        for w in range(nw):
            copy(w, 0, sibling, me).wait_recv()
            for j, chip in enumerate(chips):
                copy(w, 4 + j, (*chip, 1 - c), me).wait_recv()
            for cp in first[w] + passed[w]:
                cp.wait_send()
            mine[w].wait()

    return pl.pallas_call(
        body, name=name,
        out_shape=[jax.ShapeDtypeStruct((N_DEV * a.shape[0], a.shape[1]), a.dtype) for a in shards],
        in_specs=[_ANY] * nw, out_specs=[_ANY] * nw,
        scratch_shapes=[pltpu.SemaphoreType.DMA((nw, 7)), pltpu.SemaphoreType.DMA((nw, 7)),
                        pltpu.SemaphoreType.DMA((nw,))],
    )(*shards)


_HBM = pl.BlockSpec(memory_space=pltpu.HBM)
_SEM = pl.BlockSpec(memory_space=pltpu.SEMAPHORE)
_EFFECT = pltpu.SideEffectType.DATAFLOW_SIDE_EFFECTING
_RELATIONS = [(dx, dy, dc) for dx in (0, 1) for dy in (0, 1) for dc in (0, 1)][1:]


def _peer(rel):
    x, y, c = lax.axis_index("x"), lax.axis_index("y"), lax.axis_index("c")
    px, py, pc = x ^ rel[0], y ^ rel[1], c ^ rel[2]
    return (px, py, pc), 4 * px + 2 * py + pc, 4 * x + 2 * y + c


def _exchange_copy(k, rel, src_ref, land_ref, send_sems, recv_sems, w, scatter):
    peer, peer_idx, my_idx = _peer(rel)
    src = src_ref.at[peer_idx] if scatter else src_ref
    return pltpu.make_async_remote_copy(
        src_ref=src, dst_ref=land_ref.at[my_idx], send_sem=send_sems.at[_sends(scatter) * w + k],
        recv_sem=recv_sems.at[7 * w + k], device_id=peer, device_id_type=MESH)


def _sends(scatter):
    return 7 if scatter else 8


def _own_copy(src_ref, land_ref, send_sems, w):
    my_idx = _peer(_RELATIONS[0])[2]
    return pltpu.make_async_copy(src_ref, land_ref.at[my_idx], send_sems.at[8 * w + 7])


def _exchange_start(srcs, scatter, name):
    nw = len(srcs)
    lands = [lax.empty((N_DEV,) + (a.shape[1:] if scatter else a.shape), a.dtype) for a in srcs]

    def body(*refs):
        src_refs, land_refs = refs[:nw], refs[nw:2 * nw]
        send_sems, recv_sems = refs[2 * nw], refs[2 * nw + 1]
        token = refs[-1]
        for w in range(nw):
            for k, rel in enumerate(_RELATIONS):
                _exchange_copy(k, rel, src_refs[w], land_refs[w], send_sems, recv_sems, w, scatter).start()
            if not scatter:
                _own_copy(src_refs[w], land_refs[w], send_sems, w).start()
        token[...] = jnp.zeros_like(token)

    hbm = lambda a: pltpu.HBM(a.shape, a.dtype)
    outs = pl.pallas_call(
        body, name=name,
        out_shape=(pltpu.SemaphoreType.DMA((_sends(scatter) * nw,)), pltpu.SemaphoreType.DMA((7 * nw,)),
                   *[hbm(a) for a in srcs],
                   *[hbm(a) for a in lands], jax.ShapeDtypeStruct((8, 128), F32)),
        in_specs=[_HBM] * (2 * nw),
        out_specs=(_SEM, _SEM, *[_HBM] * (2 * nw), pl.BlockSpec(memory_space=pltpu.VMEM)),
        input_output_aliases={i: 2 + i for i in range(2 * nw)},
        compiler_params=pltpu.CompilerParams(has_side_effects=_EFFECT),
    )(*[pltpu.with_memory_space_constraint(a, pltpu.HBM) for a in srcs],
      *[pltpu.with_memory_space_constraint(a, pltpu.HBM) for a in lands])
    return outs[0], outs[1], list(outs[2:2 + nw]), list(outs[2 + nw:2 + 2 * nw]), outs[-1]


def _exchange_wait(started, scatter, after, name):
    send_sems, recv_sems, srcs, lands, _ = started
    nw = len(srcs)

    def body(*refs):
        src_refs, land_refs = refs[:nw], refs[nw:2 * nw]
        send_s, recv_s = refs[2 * nw], refs[2 * nw + 1]
        for w in range(nw):
            for k, rel in enumerate(_RELATIONS):
                cp = _exchange_copy(k, rel, src_refs[w], land_refs[w], send_s, recv_s, w, scatter)
                cp.wait_send()
                _, peer_idx, _ = _peer(rel)
                pltpu.make_async_remote_copy(
                    src_ref=src_refs[w].at[peer_idx] if scatter else src_refs[w], dst_ref=land_refs[w].at[peer_idx],
                    send_sem=send_s.at[_sends(scatter) * w + k], recv_sem=recv_s.at[7 * w + k],
                    device_id=_peer(rel)[0], device_id_type=MESH).wait_recv()
            if not scatter:
                _own_copy(src_refs[w], land_refs[w], send_s, w).wait()

    hbm = lambda a: pltpu.HBM(a.shape, a.dtype)
    outs = pl.pallas_call(
        body, name=name, out_shape=tuple(hbm(a) for a in srcs + lands),
        in_specs=[_HBM] * (2 * nw) + [_SEM, _SEM, _ANY], out_specs=tuple([_HBM] * (2 * nw)),
        input_output_aliases={i: i for i in range(2 * nw)},
        compiler_params=pltpu.CompilerParams(has_side_effects=_EFFECT),
    )(*srcs, *lands, send_sems, recv_sems, after)
    return list(outs[nw:])


def _to_bf16(a):
    _, r, l = a.shape
    tr = _row_tile(r, 512)

    def body(a_ref, o_ref):
        o_ref[...] = a_ref[...].astype(BF16)

    spec = pl.BlockSpec((1, tr, l), lambda j, i: (j, i, 0))
    return pl.pallas_call(
        body, name="to_bf16", grid=(N_DEV, r // tr), in_specs=[spec], out_specs=spec,
        out_shape=jax.ShapeDtypeStruct(a.shape, BF16), compiler_params=_cp(("arbitrary", "arbitrary")),
    )(a)


def _row_tile(rows, cap):
    best = None
    for d in range(16, min(rows, cap) + 1, 16):
        if rows % d == 0:
            best = d
    return rows if best is None else best


def _adam_math(w, g, m, v):
    bc1 = 1.0 - ADAM_B1 ** ADAM_STEP
    bc2 = 1.0 - ADAM_B2 ** ADAM_STEP
    mn = ADAM_B1 * m + (1.0 - ADAM_B1) * g
    vn = ADAM_B2 * v + (1.0 - ADAM_B2) * (g * g)
    return -ADAM_LR * ((mn / bc1) / (jnp.sqrt(vn / bc2) + ADAM_EPS) + ADAM_WD * w), mn, vn


def _adamw_rs(gp, land, w, m, v, dev):
    _, r, l = gp.shape
    tr = _row_tile(r, 256)

    def body(i_ref, g_ref, b_ref, w_ref, m_ref, v_ref, go_ref, d_ref, mo_ref, vo_ref):
        g = g_ref[0]
        for j in range(N_DEV):
            g = g + jnp.where(i_ref[0] == j, 0.0, b_ref[j].astype(F32))
        go_ref[...] = g
        d_ref[...], mo_ref[...], vo_ref[...] = _adam_math(w_ref[...], g, m_ref[...], v_ref[...])

    spec = pl.BlockSpec((tr, l), lambda i, s: (i, 0))
    return pl.pallas_call(
        body, name="adamw_rs", out_shape=[jax.ShapeDtypeStruct((r, l), F32)] * 4,
        grid_spec=pltpu.PrefetchScalarGridSpec(
            num_scalar_prefetch=1, grid=(r // tr,),
            in_specs=[pl.BlockSpec((1, tr, l), lambda i, s: (s[0], i, 0)),
                      pl.BlockSpec((N_DEV, tr, l), lambda i, s: (0, i, 0)), spec, spec, spec],
            out_specs=[spec] * 4),
        compiler_params=_cp(("arbitrary",)),
    )(dev, gp, land, w, m, v)


def _sum_blocks(a, nblk):
    m = a.shape[0] // nblk
    n = a.shape[1]

    def body(a_ref, o_ref):
        acc = a_ref[0]
        for j in range(1, nblk):
            acc = acc + a_ref[j]
        o_ref[...] = acc

    return pl.pallas_call(
        body, name="sum_blocks", out_shape=jax.ShapeDtypeStruct((m, n), a.dtype),
        in_specs=[_full((nblk, m, n))], out_specs=_full((m, n)),
    )(a.reshape(nblk, m, n))


def _adamw(w, g, m, v):
    rows, c = w.shape

    def body(w_ref, g_ref, m_ref, v_ref, d_ref, mo_ref, vo_ref):
        d_ref[...], mo_ref[...], vo_ref[...] = _adam_math(w_ref[...], g_ref[...], m_ref[...], v_ref[...])

    return pl.pallas_call(
        body, name="adamw", in_specs=[_full((rows, c))] * 4, out_specs=[_full((rows, c))] * 3,
        out_shape=[jax.ShapeDtypeStruct((rows, c), F32)] * 3,
    )(w, g, m, v)


WEIGHTS = ["mix_norm_pre", "mix_norm_post", "ffn_norm_pre", "ffn_norm_post", "ev_w_in", "ev_a_conv_w", "ev_a_conv_b",
           "ev_a_ln_g", "ev_a_ln_b", "ev_sinks", "ev_w_out", "od_w_in", "od_conv_w", "od_w_out", "ffn_w_up",
           "ffn_conv_w", "ffn_w_down"]
BIG = ["ev_w_in", "ev_w_out", "od_w_in", "od_w_out", "ffn_w_up", "ffn_w_down"]
SMALL_REPL = ["mix_norm_pre", "mix_norm_post", "ffn_norm_pre", "ffn_norm_post", "ev_a_conv_b", "ev_a_ln_g",
              "ev_a_ln_b", "ev_sinks"]
SMALL_SHARDED = ["ev_a_conv_w", "od_conv_w", "ffn_conv_w"]


def _pack(arrs, rows):
    flat = jnp.concatenate([a.reshape(-1) for a in arrs])
    return jnp.pad(flat, (0, rows * LANES - flat.shape[0])).reshape(rows, LANES)


def _unpack(packed, shapes):
    flat, out, off = packed.reshape(-1), [], 0
    for s in shapes:
        n = 1
        for d in s:
            n *= d
        out.append(flat[off:off + n].reshape(s))
        off += n
    return out


def kernel(x, positions, mix_norm_pre, mix_norm_post, ffn_norm_pre, ffn_norm_post, ev_w_in, ev_a_conv_w, ev_a_conv_b, ev_a_ln_g, ev_a_ln_b, ev_sinks, ev_w_out, od_w_in, od_conv_w, od_w_out, ffn_w_up, ffn_conv_w, ffn_w_down, loss_target, m_mix_norm_pre, m_mix_norm_post, m_ffn_norm_pre, m_ffn_norm_post, m_ev_w_in, m_ev_a_conv_w, m_ev_a_conv_b, m_ev_a_ln_g, m_ev_a_ln_b, m_ev_sinks, m_ev_w_out, m_od_w_in, m_od_conv_w, m_od_w_out, m_ffn_w_up, m_ffn_conv_w, m_ffn_w_down, v_mix_norm_pre, v_mix_norm_post, v_ffn_norm_pre, v_ffn_norm_post, v_ev_w_in, v_ev_a_conv_w, v_ev_a_conv_b, v_ev_a_ln_g, v_ev_a_ln_b, v_ev_sinks, v_ev_w_out, v_od_w_in, v_od_conv_w, v_od_w_out, v_ffn_w_up, v_ffn_conv_w, v_ffn_w_down):
    w = dict(zip(WEIGHTS, (mix_norm_pre, mix_norm_post, ffn_norm_pre, ffn_norm_post, ev_w_in, ev_a_conv_w, ev_a_conv_b,
                           ev_a_ln_g, ev_a_ln_b, ev_sinks, ev_w_out, od_w_in, od_conv_w, od_w_out, ffn_w_up, ffn_conv_w,
                           ffn_w_down)))
    mom = dict(zip(WEIGHTS, (m_mix_norm_pre, m_mix_norm_post, m_ffn_norm_pre, m_ffn_norm_post, m_ev_w_in, m_ev_a_conv_w,
                             m_ev_a_conv_b, m_ev_a_ln_g, m_ev_a_ln_b, m_ev_sinks, m_ev_w_out, m_od_w_in, m_od_conv_w,
                             m_od_w_out, m_ffn_w_up, m_ffn_conv_w, m_ffn_w_down)))
    var = dict(zip(WEIGHTS, (v_mix_norm_pre, v_mix_norm_post, v_ffn_norm_pre, v_ffn_norm_post, v_ev_w_in, v_ev_a_conv_w,
                             v_ev_a_conv_b, v_ev_a_ln_g, v_ev_a_ln_b, v_ev_sinks, v_ev_w_out, v_od_w_in, v_od_conv_w,
                             v_od_w_out, v_ffn_w_up, v_ffn_conv_w, v_ffn_w_down)))
    ix, iy, ic = lax.axis_index("x"), lax.axis_index("y"), lax.axis_index("c")
    dev = 4 * ix + 2 * iy + ic
    two = lambda a: a.reshape(-1, a.shape[-1])

    dev1 = jnp.reshape(dev, (1,)).astype(jnp.int32)
    shard = {n: two(w[n].astype(BF16)) for n in BIG}
    gathered = lambda n, a: a.reshape((N_DEV,) + w[n].shape)
    ev_names = [n for n in BIG if n.startswith("ev_")]
    ev_gat = _all_gather([shard[n] for n in ev_names] + [_pack([w[n] for n in SMALL_SHARDED], 8)], "gather_ev")
    p = _prep_ev({n: gathered(n, a) for n, a in zip(ev_names, ev_gat)})
    rest_names = [n for n in BIG if not n.startswith("ev_")]
    first = shard[rest_names[0]] + (ev_gat[0][0:1, 0:1] * 0).astype(BF16)
    started = _exchange_start([first] + [shard[n] for n in rest_names[1:]], False, "gather_start")

    def rest_weights(after):
        lands = _exchange_wait(started, False, after, "gather_wait")
        return _prep_rest({n: gathered(n, a) for n, a in zip(rest_names, lands)})

    small = {n: w[n] for n in SMALL_REPL}
    small_shapes = [w[n].shape for n in SMALL_SHARDED]
    conv_gat = ev_gat[len(ev_names)].reshape(N_DEV, 8, LANES)
    per_dev = [_unpack(conv_gat[d], small_shapes) for d in range(N_DEV)]
    for k, n in enumerate(SMALL_SHARDED):
        small[n] = jnp.concatenate([per_dev[d][k] for d in range(N_DEV)], axis=-1)

    exchanges = []

    def grads_ready(names, big):
        blocks = lambda a, n: a.reshape(N_DEV, -1, w[n].shape[-1])
        bufs = [blocks(big[n], n) for n in names]
        payload = [blocks(big[n + ":bf16"], n) if n + ":bf16" in big else _to_bf16(b) for n, b in zip(names, bufs)]
        st = _exchange_start(payload, True, "grads_start_" + names[0])
        exchanges.append((names, bufs, st))
        return st[-1][0, 0]

    lpart, grad_x, big, g = _local_step(x[0], positions[0], loss_target[0], p, rest_weights, small, started[-1][0, 0],
                                        grads_ready)
    loss = lax.psum(lpart, ("x", "y", "c"))

    grads, delta, new_m, new_v = {}, {}, {}, {}
    for names, bufs, st in exchanges:
        lands = _exchange_wait(st, True, grad_x, "grads_wait_" + names[0])
        for n, b, land in zip(names, bufs, lands):
            outs = _adamw_rs(b, land, two(w[n]), two(mom[n]), two(var[n]), dev1)
            grads[n], delta[n], new_m[n], new_v[n] = (a.reshape(w[n].shape) for a in outs)

    small_names = SMALL_REPL + SMALL_SHARDED
    s_all = _sum_blocks(_all_gather([_pack([g[n] for n in small_names], 64)], "gather_small_grads")[0], N_DEV)
    for n, a in zip(small_names, _unpack(s_all, [small[n].shape for n in small_names])):
        if n in SMALL_SHARDED:
            width = w[n].shape[-1]
            a = lax.dynamic_slice_in_dim(a, dev * width, width, axis=a.ndim - 1)
        grads[n] = a
    pk = lambda dct: _pack([dct[n] for n in small_names], 16)
    outs = _adamw(pk(w), pk(grads), pk(mom), pk(var))
    for dst, packed in zip((delta, new_m, new_v), outs):
        for n, a in zip(small_names, _unpack(packed, [w[n].shape for n in small_names])):
            dst[n] = a

    return (loss, grad_x[None], *[grads[n] for n in WEIGHTS], *[delta[n] for n in WEIGHTS],
            *[new_m[n] for n in WEIGHTS], *[new_v[n] for n in WEIGHTS])
```

```python
import jax
import jax.numpy as jnp
from jax import lax
from jax.experimental import pallas as pl
from jax.experimental.pallas import tpu as pltpu

F32, BF16 = jnp.float32, jnp.bfloat16

D_MODEL = 1024
A_CH = 512
A_CONV = 31
Q_DIM = 512
KV_DIM = 128
HEAD_DIM = 64
N_Q_HEADS = 8
N_KV_HEADS = 2
GROUP = 4
BLOCK = 128
EVEN_IN = 1792
SC_DIM = 1024
D_FF = 2816
ROPE_THETA = 500000.0
ROPE_DIM = 16
RMS_EPS = 1e-6
LN_EPS = 1e-5
SCALE = HEAD_DIM ** -0.5
NEG = -1e30

ADAM_LR, ADAM_B1, ADAM_B2, ADAM_EPS, ADAM_WD, ADAM_STEP = 0.001, 0.9, 0.999, 1e-08, 0.01, 10

N_DEV = 8
FF_N = 2 * D_FF // N_DEV
LANES = 1024
HALO3 = 8
HALO31 = 32
VMEM_LIMIT = 56 * 1024 * 1024

TM = 512
TM_ROW = 1024
TM_BWD = 256
TK_DW = 4096
ATT_NB = 4
SUB = 128

_ANY = pl.BlockSpec(memory_space=pl.ANY)
_CONTRACT_LAST = (((1,), (1,)), ((), ()))
_CONTRACT_FIRST = (((0,), (0,)), ((), ()))


def _cp(sem, vmem=VMEM_LIMIT):
    return pltpu.CompilerParams(dimension_semantics=sem, vmem_limit_bytes=vmem)


def _full(shape):
    n = len(shape)
    return pl.BlockSpec(shape, lambda *_: (0,) * n)


def _rows(tm, n):
    return pl.BlockSpec((tm, n), lambda i, *_: (i, 0))


def _sigmoid(x):
    return 0.5 * jnp.tanh(0.5 * x) + 0.5


def _rsqrt_mean(x):
    return lax.rsqrt(jnp.mean(x * x, axis=-1, keepdims=True) + RMS_EPS)


def _rms_bwd(x, g, dy):
    r = _rsqrt_mean(x)
    xh = x * r
    dxh = dy * g
    dx = r * (dxh - xh * jnp.mean(dxh * xh, axis=-1, keepdims=True))
    return dx, jnp.sum(dy * xh, axis=0, keepdims=True)


def _acc_out(ref, first, val):
    @pl.when(first)
    def _():
        ref[...] = val

    @pl.when(jnp.logical_not(first))
    def _():
        ref[...] += val


def _rope_tables(positions):
    half = ROPE_DIM // 2
    inv_freq = ROPE_THETA ** (-(jnp.arange(half, dtype=F32) * 2.0 / ROPE_DIM))
    ang = positions.astype(F32)[:, None] * inv_freq
    cos, sin = jnp.cos(ang), jnp.sin(ang)
    t = positions.shape[0]
    one, zero = jnp.ones((t, HEAD_DIM - ROPE_DIM), F32), jnp.zeros((t, HEAD_DIM - ROPE_DIM), F32)
    z8 = jnp.zeros((t, half), F32)
    c = jnp.concatenate([cos, cos, one], axis=1)
    sa = jnp.concatenate([z8, sin, zero], axis=1)
    sb = jnp.concatenate([-sin, z8, zero], axis=1)
    return tuple(jnp.tile(a, (1, 2)) for a in (c, sa, sb))


def _rope(t, c, sa, sb):
    return t * c + pltpu.roll(t, 8, 1) * sa + pltpu.roll(t, 120, 1) * sb


def _rope_bwd(d, c, sa, sb):
    return d * c + pltpu.roll(d * sa, 120, 1) + pltpu.roll(d * sb, 8, 1)


def _ev_in(x, gpre, w_in, rc, rsa, rsb):
    t = x.shape[0]
    tm = min(TM_ROW, t)

    def body(x_ref, g_ref, w_ref, c_ref, sa_ref, sb_ref, h_ref, zag_ref, q_ref, k_ref, v_ref):
        xv = x_ref[...]
        h = (xv * _rsqrt_mean(xv) * g_ref[...]).astype(BF16)
        h_ref[...] = h
        z = jnp.dot(h, w_ref[...], preferred_element_type=F32)
        zag_ref[...] = z[:, :2 * A_CH].astype(BF16)
        c, sa, sb = c_ref[...], sa_ref[...], sb_ref[...]
        q0 = 2 * A_CH
        for j in range(Q_DIM // 128):
            q_ref[:, 128 * j:128 * (j + 1)] = _rope(z[:, q0 + 128 * j:q0 + 128 * (j + 1)], c, sa, sb).astype(BF16)
        k0 = q0 + Q_DIM
        k_ref[...] = _rope(z[:, k0:k0 + KV_DIM], c, sa, sb).astype(BF16)
        v_ref[...] = z[:, k0 + KV_DIM:k0 + 2 * KV_DIM].astype(BF16)

    return pl.pallas_call(
        body, name="ev_in", grid=(t // tm,),
        in_specs=[_rows(tm, D_MODEL), _full((1, D_MODEL)), _full((D_MODEL, EVEN_IN)),
                  _rows(tm, 128), _rows(tm, 128), _rows(tm, 128)],
        out_specs=[_rows(tm, D_MODEL), _rows(tm, 2 * A_CH), _rows(tm, Q_DIM), _rows(tm, KV_DIM), _rows(tm, KV_DIM)],
        out_shape=[jax.ShapeDtypeStruct((t, D_MODEL), BF16), jax.ShapeDtypeStruct((t, 2 * A_CH), BF16),
                   jax.ShapeDtypeStruct((t, Q_DIM), BF16), jax.ShapeDtypeStruct((t, KV_DIM), BF16),
                   jax.ShapeDtypeStruct((t, KV_DIM), BF16)],
        compiler_params=_cp(("arbitrary",)),
    )(x, gpre, w_in, rc, rsa, rsb)


def _glu(zag):
    z = zag.astype(F32)
    return z[:, :A_CH] * _sigmoid(z[:, A_CH:])


def _tap_copies(ext, cbuf, first_row, rows):
    for b in range(1, 8):
        s = first_row(b)
        cbuf[b - 1] = ext[s:s + rows, :]


def _conf_fwd(zag, conv_w, conv_b, ln_g, ln_b):
    t = zag.shape[0]
    tm = min(TM, t)
    rows = tm + HALO31 - 8

    def body(z_ref, w_ref, b_ref, g_ref, lb_ref, c_ref, a_ref, ext, cbuf):
        i = pl.program_id(0)

        @pl.when(i == 0)
        def _():
            ext[0:HALO31, :] = jnp.zeros((HALO31, A_CH), F32)

        ext[HALO31:HALO31 + tm, :] = _glu(z_ref[...])
        _tap_copies(ext, cbuf, lambda b: 8 - b, rows)
        for rs in range(0, tm, SUB):
            for cs in range(0, A_CH, 128):
                acc = jnp.zeros((SUB, 128), F32)
                for k in range(A_CONV):
                    lag_a, lag_b = divmod(k, 8)
                    r0 = HALO31 - 8 - 8 * lag_a + rs
                    src = (ext[r0 + 8:r0 + 8 + SUB, cs:cs + 128] if lag_b == 0
                           else cbuf[lag_b - 1, r0:r0 + SUB, cs:cs + 128])
                    acc = acc + w_ref[A_CONV - 1 - k:A_CONV - k, cs:cs + 128] * src
                c_ref[rs:rs + SUB, cs:cs + 128] = acc
        ext[0:HALO31, :] = ext[tm:tm + HALO31, :]
        cv = c_ref[...] + b_ref[...]
        c_ref[...] = cv
        mu = jnp.mean(cv, axis=-1, keepdims=True)
        xc = cv - mu
        ln = xc * lax.rsqrt(jnp.mean(xc * xc, axis=-1, keepdims=True) + LN_EPS) * g_ref[...] + lb_ref[...]
        a_ref[...] = (ln * _sigmoid(ln)).astype(BF16)

    return pl.pallas_call(
        body, name="conf_fwd", grid=(t // tm,),
        in_specs=[_rows(tm, 2 * A_CH), _full((32, A_CH)), _full((1, A_CH)), _full((1, A_CH)), _full((1, A_CH))],
        out_specs=[_rows(tm, A_CH), _rows(tm, A_CH)],
        out_shape=[jax.ShapeDtypeStruct((t, A_CH), F32), jax.ShapeDtypeStruct((t, A_CH), BF16)],
        scratch_shapes=[pltpu.VMEM((HALO31 + tm, A_CH), F32), pltpu.VMEM((7, rows, A_CH), F32)],
        compiler_params=_cp(("arbitrary",)),
    )(zag, conv_w, conv_b, ln_g, ln_b)


def _attn_mask(first_block):
    row = lax.broadcasted_iota(jnp.int32, (GROUP * BLOCK, 2 * BLOCK), 0) & (BLOCK - 1)
    col = lax.broadcasted_iota(jnp.int32, (GROUP * BLOCK, 2 * BLOCK), 1)
    diff = row + BLOCK - col
    return (diff >= 0) & (diff < BLOCK) & ((col >= BLOCK) | jnp.logical_not(first_block))


def _sink_rows(s_ref, h):
    grp = lax.broadcasted_iota(jnp.int32, (GROUP * BLOCK, 1), 0) >> 7
    out = jnp.full((GROUP * BLOCK, 1), s_ref[GROUP * h], F32)
    for g in range(1, GROUP):
        out = jnp.where(grp == g, s_ref[GROUP * h + g], out)
    return out


def _attn_probs(q4, k2, mask, sink):
    s = lax.dot_general(q4, k2, _CONTRACT_LAST, preferred_element_type=F32) * SCALE
    s = jnp.where(mask, s, NEG)
    m = jnp.maximum(jnp.max(s, axis=-1, keepdims=True), sink)
    p = jnp.exp(s - m)
    es = jnp.exp(sink - m)
    inv = 1.0 / (jnp.sum(p, axis=-1, keepdims=True) + es)
    return p * inv, es * inv


def _q_heads(q, h):
    return jnp.concatenate([q[:, HEAD_DIM * (GROUP * h + g):HEAD_DIM * (GROUP * h + g + 1)] for g in range(GROUP)],
                           axis=0)


def _kv_head(prev, cur, h):
    return jnp.concatenate([prev[:, HEAD_DIM * h:HEAD_DIM * (h + 1)], cur[:, HEAD_DIM * h:HEAD_DIM * (h + 1)]], axis=0)


def _attn_fwd(q, k, v, sinks):
    t = q.shape[0]
    nb = min(ATT_NB, t // BLOCK)
    rows = nb * BLOCK

    def body(s_ref, q_ref, kc_ref, kp_ref, vc_ref, vp_ref, o_ref):
        first = pl.program_id(0) == 0
        for b in range(nb):
            lo = BLOCK * b
            mask = _attn_mask(first) if b == 0 else _attn_mask(False)
            qv, kc, vc = q_ref[lo:lo + BLOCK, :], kc_ref[lo:lo + BLOCK, :], vc_ref[lo:lo + BLOCK, :]
            kp = kp_ref[...] if b == 0 else kc_ref[lo - BLOCK:lo, :]
            vp = vp_ref[...] if b == 0 else vc_ref[lo - BLOCK:lo, :]
            for h in range(N_KV_HEADS):
                pn, _ = _attn_probs(_q_heads(qv, h), _kv_head(kp, kc, h), mask, _sink_rows(s_ref, h))
                o4 = jnp.dot(pn.astype(BF16), _kv_head(vp, vc, h), preferred_element_type=F32).astype(BF16)
                for g in range(GROUP):
                    c0 = HEAD_DIM * (GROUP * h + g)
                    o_ref[lo:lo + BLOCK, c0:c0 + HEAD_DIM] = o4[BLOCK * g:BLOCK * (g + 1), :]

    cur = lambda n: pl.BlockSpec((rows, n), lambda i: (i, 0))
    prev = lambda n: pl.BlockSpec((BLOCK, n), lambda i: (jnp.maximum(i * nb - 1, 0), 0))
    return pl.pallas_call(
        body, name="attn_fwd", grid=(t // rows,),
        in_specs=[pl.BlockSpec(memory_space=pltpu.SMEM), cur(Q_DIM), cur(KV_DIM), prev(KV_DIM), cur(KV_DIM),
                  prev(KV_DIM)],
        out_specs=cur(Q_DIM),
        out_shape=jax.ShapeDtypeStruct((t, Q_DIM), BF16),
        compiler_params=_cp(("arbitrary",)),
    )(sinks, q, k, k, v, v)


def _out_post(lhs, ws, x_in, gpost):
    t = x_in.shape[0]
    tm = min(TM_ROW, t)
    n = len(lhs)

    def body(*refs):
        x_ref, g_ref, m_ref, xo_ref = refs[2 * n:]
        m = jnp.dot(refs[0][...], refs[n][...], preferred_element_type=F32)
        for j in range(1, n):
            m = m + jnp.dot(refs[j][...], refs[n + j][...], preferred_element_type=F32)
        m_ref[...] = m.astype(BF16)
        xo_ref[...] = x_ref[...] + m * _rsqrt_mean(m) * g_ref[...]

    return pl.pallas_call(
        body, name="out_post", grid=(t // tm,),
        in_specs=[_rows(tm, a.shape[1]) for a in lhs] + [_full(w.shape) for w in ws]
                 + [_rows(tm, D_MODEL), _full((1, D_MODEL))],
        out_specs=[_rows(tm, D_MODEL), _rows(tm, D_MODEL)],
        out_shape=[jax.ShapeDtypeStruct((t, D_MODEL), BF16), jax.ShapeDtypeStruct((t, D_MODEL), F32)],
        compiler_params=_cp(("arbitrary",)),
    )(*lhs, *ws, x_in, gpost)


def _conv3(w_ref, ext, tm):
    s = HALO3 - 2
    return (w_ref[0:1, :] * ext[s:s + tm, :] + w_ref[1:2, :] * ext[s + 1:s + 1 + tm, :]
            + w_ref[2:3, :] * ext[s + 2:s + 2 + tm, :])


def _ffn_fwd(x1, gpre, wup, layer, cw, wd, gpost):
    t = x1.shape[0]
    tm = min(TM, t)
    nc, n = wup.shape[1], wup.shape[4]

    def body(x_ref, gpre_ref, wup_ref, cw_ref, wd_ref, gpost_ref, h_ref, up_ref, u_ref, f_ref, xo_ref, h_s, acc, ext, hal):
        i, c = pl.program_id(0), pl.program_id(1)

        @pl.when(c == 0)
        def _():
            xv = x_ref[...]
            h = (xv * _rsqrt_mean(xv) * gpre_ref[...]).astype(BF16)
            h_s[...] = h
            h_ref[...] = h

        @pl.when(i == 0)
        def _():
            hal[c] = jnp.zeros((2, HALO3, n), F32)

        u = []
        for gv in range(2):
            up = jnp.dot(h_s[...], wup_ref[gv, 0, 0], preferred_element_type=F32)
            up_ref[gv, 0] = up.astype(BF16)
            ext[gv, 0:HALO3, :] = hal[c, gv]
            ext[gv, HALO3:HALO3 + tm, :] = up
            hal[c, gv] = ext[gv, tm:tm + HALO3, :]
            s = HALO3 - 2
            u.append(cw_ref[gv, 0, 0:1, :] * ext[gv, s:s + tm, :] + cw_ref[gv, 0, 1:2, :] * ext[gv, s + 1:s + 1 + tm, :]
                     + cw_ref[gv, 0, 2:3, :] * up)
            u_ref[gv, 0] = u[gv].astype(BF16)
        act = (u[0] * _sigmoid(u[0]) * u[1]).astype(BF16)
        part = jnp.dot(act, wd_ref[...], preferred_element_type=F32)

        @pl.when(c == 0)
        def _():
            acc[...] = part

        @pl.when(jnp.logical_and(c > 0, c < nc - 1))
        def _():
            acc[...] += part

        @pl.when(c == nc - 1)
        def _():
            f = acc[...] + part
            f_ref[...] = f
            xo_ref[...] = x_ref[...] + f * _rsqrt_mean(f) * gpost_ref[...]

    row = lambda w: pl.BlockSpec((tm, w), lambda i, c: (i, 0))
    one = _full((1, D_MODEL))
    return pl.pallas_call(
        body, name="ffn_fwd", grid=(t // tm, nc),
        in_specs=[row(D_MODEL), one, pl.BlockSpec((2, 1, 1, D_MODEL, n), lambda i, c: (0, c, layer, 0, 0)),
                  pl.BlockSpec((2, 1, 3, n), lambda i, c: (0, c, 0, 0)), pl.BlockSpec((n, D_MODEL), lambda i, c: (c, 0)),
                  one],
        out_specs=[row(D_MODEL), pl.BlockSpec((2, 1, tm, n), lambda i, c: (0, c, i, 0)),
                   pl.BlockSpec((2, 1, tm, n), lambda i, c: (0, c, i, 0)), row(D_MODEL), row(D_MODEL)],
        out_shape=[jax.ShapeDtypeStruct((t, D_MODEL), BF16), jax.ShapeDtypeStruct((2, nc, t, n), BF16),
                   jax.ShapeDtypeStruct((2, nc, t, n), BF16), jax.ShapeDtypeStruct((t, D_MODEL), F32),
                   jax.ShapeDtypeStruct((t, D_MODEL), F32)],
        scratch_shapes=[pltpu.VMEM((tm, D_MODEL), BF16), pltpu.VMEM((tm, D_MODEL), F32),
                        pltpu.VMEM((2, HALO3 + tm, n), F32), pltpu.VMEM((nc, 2, HALO3, n), F32)],
        compiler_params=_cp(("arbitrary", "arbitrary")),
    )(x1, gpre, wup, cw, wd, gpost)


def _od_fwd(x_in, gpre, w_in, cw, w_out, gpost):
    t = x_in.shape[0]
    tm = min(TM, t)
    ns, _, n = w_in.shape

    def body(x_ref, gpre_ref, w_ref, cw_ref, wo_ref, gpost_ref, h_ref, z_ref, cv_ref, y_ref, m_ref, xo_ref, z_s, ext):
        i = pl.program_id(0)
        xv = x_ref[...]
        h = (xv * _rsqrt_mean(xv) * gpre_ref[...]).astype(BF16)
        h_ref[...] = h
        for j in range(ns):
            z_s[:, n * j:n * (j + 1)] = jnp.dot(h, w_ref[j], preferred_element_type=F32)
        z_ref[...] = z_s[...].astype(BF16)

        @pl.when(i == 0)
        def _():
            ext[0:HALO3, :] = jnp.zeros((HALO3, SC_DIM), F32)

        ext[HALO3:HALO3 + tm, :] = z_s[:, SC_DIM:2 * SC_DIM] * z_s[:, 2 * SC_DIM:]
        cv = _conv3(cw_ref, ext, tm)
        cv_ref[...] = cv.astype(BF16)
        y = (z_s[:, :SC_DIM] * cv).astype(BF16)
        ext[0:HALO3, :] = ext[tm:tm + HALO3, :]
        y_ref[...] = y
        m = jnp.dot(y, wo_ref[...], preferred_element_type=F32)
        m_ref[...] = m
        xo_ref[...] = xv + m * _rsqrt_mean(m) * gpost_ref[...]

    return pl.pallas_call(
        body, name="od_fwd", grid=(t // tm,),
        in_specs=[_rows(tm, D_MODEL), _full((1, D_MODEL)), _full((ns, D_MODEL, n)), _full((3, SC_DIM)),
                  _full((SC_DIM, D_MODEL)), _full((1, D_MODEL))],
        out_specs=[_rows(tm, D_MODEL), _rows(tm, 3 * SC_DIM), _rows(tm, SC_DIM), _rows(tm, SC_DIM), _rows(tm, D_MODEL),
                   _rows(tm, D_MODEL)],
        out_shape=[jax.ShapeDtypeStruct((t, D_MODEL), BF16), jax.ShapeDtypeStruct((t, 3 * SC_DIM), BF16),
                   jax.ShapeDtypeStruct((t, SC_DIM), BF16), jax.ShapeDtypeStruct((t, SC_DIM), BF16),
                   jax.ShapeDtypeStruct((t, D_MODEL), F32), jax.ShapeDtypeStruct((t, D_MODEL), F32)],
        scratch_shapes=[pltpu.VMEM((tm, 3 * SC_DIM), F32), pltpu.VMEM((HALO3 + tm, SC_DIM), F32)],
        compiler_params=_cp(("arbitrary",)),
    )(x_in, gpre, w_in, cw, w_out, gpost)


def _dw2d(a, b, bm, bn):
    t, m = a.shape
    n = b.shape[1]
    tk = min(TK_DW, t)

    def body(a_ref, b_ref, o_ref):
        part = lax.dot_general(a_ref[...], b_ref[...], _CONTRACT_FIRST, preferred_element_type=F32)
        _acc_out(o_ref, pl.program_id(2) == 0, part)

    return pl.pallas_call(
        body, name="dw2d", grid=(m // bm, n // bn, t // tk),
        in_specs=[pl.BlockSpec((tk, bm), lambda i, j, k: (k, i)), pl.BlockSpec((tk, bn), lambda i, j, k: (k, j))],
        out_specs=pl.BlockSpec((bm, bn), lambda i, j, k: (i, j)),
        out_shape=jax.ShapeDtypeStruct((m, n), F32),
        compiler_params=_cp(("arbitrary", "arbitrary", "arbitrary")),
    )(a, b)


def _dw_cols(a, b, n_blk):
    t, m = a.shape
    s = b.shape[1] // n_blk
    tk = min(TK_DW, t)
    nk = t // tk

    def body(a_ref, b_ref, o_ref, ob_ref):
        part = lax.dot_general(a_ref[...], b_ref[...], _CONTRACT_FIRST, preferred_element_type=F32)
        for j in range(2):
            _acc_out(o_ref.at[j], pl.program_id(1) == 0, part[:, n_blk * j:n_blk * (j + 1)])

        @pl.when(pl.program_id(1) == nk - 1)
        def _():
            ob_ref[...] = o_ref[...].astype(BF16)

    spec = pl.BlockSpec((2, m, n_blk), lambda j, k: (j, 0, 0))
    return pl.pallas_call(
        body, name="dw_cols", grid=(s // 2, nk),
        in_specs=[pl.BlockSpec((tk, m), lambda j, k: (k, 0)), pl.BlockSpec((tk, 2 * n_blk), lambda j, k: (k, j))],
        out_specs=[spec, spec],
        out_shape=[jax.ShapeDtypeStruct((s, m, n_blk), F32), jax.ShapeDtypeStruct((s, m, n_blk), BF16)],
        compiler_params=_cp(("arbitrary", "arbitrary")),
    )(a, b)


def _dw_up(h, dup, layer, buf):
    t, m = h.shape
    s, _, n = dup.shape
    tk = min(TK_DW, t)
    nk = t // tk

    def body(*refs):
        a_ref, b_ref, o_ref, ob_ref = refs[0], refs[1], refs[-2], refs[-1]
        part = lax.dot_general(a_ref[...], b_ref[0], _CONTRACT_FIRST, preferred_element_type=F32)
        _acc_out(o_ref.at[0, 0], pl.program_id(1) == 0, part)

        @pl.when(pl.program_id(1) == nk - 1)
        def _():
            ob_ref[...] = o_ref[...].astype(BF16)

    spec = pl.BlockSpec((1, 1, m, n), lambda j, k: (j, layer, 0, 0))
    return pl.pallas_call(
        body, name="dw_up", grid=(s, nk),
        in_specs=[pl.BlockSpec((tk, m), lambda j, k: (k, 0)), pl.BlockSpec((1, tk, n), lambda j, k: (j, k, 0))]
                 + ([] if buf is None else [_ANY, _ANY]),
        out_specs=[spec, spec],
        out_shape=[jax.ShapeDtypeStruct((s, 2, m, n), F32), jax.ShapeDtypeStruct((s, 2, m, n), BF16)],
        input_output_aliases={} if buf is None else {2: 0, 3: 1},
        compiler_params=_cp(("arbitrary", "arbitrary")),
    )(h, dup, *([] if buf is None else buf))


def _dw_down(act, df, layer, buf):
    nc, t, n = act.shape
    d = df.shape[1]
    tk = min(TK_DW, t)
    nk = t // tk

    def body(*refs):
        a_ref, b_ref, o_ref, ob_ref = refs[0], refs[1], refs[-2], refs[-1]
        part = lax.dot_general(a_ref[0], b_ref[...], _CONTRACT_FIRST, preferred_element_type=F32)
        part = part.reshape(2, n // 2, d)
        first = pl.program_id(1) == 0

        @pl.when(first)
        def _():
            o_ref[:, 0] = part

        @pl.when(jnp.logical_not(first))
        def _():
            o_ref[:, 0] += part

        @pl.when(pl.program_id(1) == nk - 1)
        def _():
            ob_ref[...] = o_ref[...].astype(BF16)

    spec = pl.BlockSpec((2, 1, n // 2, d), lambda c, k: (c, layer, 0, 0))
    return pl.pallas_call(
        body, name="dw_down", grid=(nc, nk),
        in_specs=[pl.BlockSpec((1, tk, n), lambda c, k: (c, k, 0)), pl.BlockSpec((tk, d), lambda c, k: (k, 0))]
                 + ([] if buf is None else [_ANY, _ANY]),
        out_specs=[spec, spec],
        out_shape=[jax.ShapeDtypeStruct((2 * nc, 2, n // 2, d), F32), jax.ShapeDtypeStruct((2 * nc, 2, n // 2, d), BF16)],
        input_output_aliases={} if buf is None else {2: 0, 3: 1},
        compiler_params=_cp(("arbitrary", "arbitrary")),
    )(act, df, *([] if buf is None else buf))


def _dz_wt_rms_bwd(dz, w, x_in, gpre, dres):
    t, n = dz.shape
    tm = min(TM_ROW, t)

    def body(dz_ref, wt_ref, x_ref, g_ref, dres_ref, dx_ref, dg_ref):
        dh = lax.dot_general(dz_ref[...], wt_ref[...], _CONTRACT_LAST, preferred_element_type=F32)
        dx, dg = _rms_bwd(x_ref[...], g_ref[...], dh)
        dx_ref[...] = dres_ref[...] + dx
        _acc_out(dg_ref, pl.program_id(0) == 0, dg)

    return pl.pallas_call(
        body, name="dz_wt_rms_bwd", grid=(t // tm,),
        in_specs=[_rows(tm, n), _full((D_MODEL, n)), _rows(tm, D_MODEL), _full((1, D_MODEL)), _rows(tm, D_MODEL)],
        out_specs=[_rows(tm, D_MODEL), _full((1, D_MODEL))],
        out_shape=[jax.ShapeDtypeStruct((t, D_MODEL), F32), jax.ShapeDtypeStruct((1, D_MODEL), F32)],
        compiler_params=_cp(("arbitrary",)),
    )(dz, w, x_in, gpre, dres)


def _shift_matrices(shift, shift_h, tm, hb):
    row = lax.broadcasted_iota(jnp.int32, (2 * tm, tm), 0)
    col = lax.broadcasted_iota(jnp.int32, (2 * tm, tm), 1)
    hit = ((row < tm) & (col == row + 1)) | ((row >= tm) & (col == row - tm + 2))
    shift[...] = jnp.where(hit, 1.0, 0.0).astype(BF16)
    row = lax.broadcasted_iota(jnp.int32, (hb, hb), 0)
    col = lax.broadcasted_iota(jnp.int32, (hb, hb), 1)
    hit = ((row < HALO3) & (col == row - (HALO3 - 1))) | ((row >= HALO3) & (col == row - (2 * HALO3 - 2)))
    shift_h[...] = jnp.where(hit, 1.0, 0.0).astype(BF16)


def _next_rows(shift, shift_h, xb, nxt, d12_s, tm):
    d12_s[...] = jnp.dot(shift[...], xb, preferred_element_type=F32)
    edge = jnp.dot(shift_h[...], nxt, preferred_element_type=F32)
    d12_s[tm - HALO3:tm, :] += edge[0:HALO3, :]
    d12_s[2 * tm - HALO3:2 * tm, :] += edge[HALO3:2 * HALO3, :]


def _ffn_bwd(f, dxo, gpost, x_in, gpre, up, u, cw, wd, wup, layer, target=None):
    t = f.shape[0]
    tm = min(TM_BWD, t)
    nt = t // tm
    nc, n = up.shape[1], up.shape[3]
    hb = 2 * HALO3

    def body(*refs):
        f_ref, dxo_ref, gpost_ref, x_ref, gpre_ref, up_ref, u_ref, cw_ref, wd_ref, wup_ref = refs[:10]
        n_in = 10 if target is None else 11
        n_out = 7 if target is None else 8
        df_ref, act_ref, dup_ref, dx_ref, dgpost_ref, dgpre_ref, dcw_ref = refs[n_in:n_in + 7]
        df_s, acc, du_s, dub_s, d12_s, hal, shift, shift_h = refs[n_in + n_out:]
        i, c = pl.program_id(0), pl.program_id(1)

        def incoming():
            if target is None:
                return dxo_ref[...]
            return (dxo_ref[...] - refs[10][...]) * (1.0 / D_MODEL)

        @pl.when(c == 0)
        def _():
            dy = incoming()
            df, dg = _rms_bwd(f_ref[...], gpost_ref[...], dy)
            df_s[...] = df.astype(BF16)
            df_ref[...] = df.astype(BF16)
            _acc_out(dgpost_ref, i == 0, dg)
            if target is not None:
                part = jnp.zeros((1, 128), F32) + jnp.sum(dy * dy) * (0.5 * D_MODEL)
                _acc_out(refs[n_in + 7], i == 0, part)

        @pl.when(i == 0)
        def _():
            hal[c] = jnp.zeros((2, hb, n), BF16)
            dcw_ref[0, c] = jnp.zeros((8, n), F32)
            dcw_ref[1, c] = jnp.zeros((8, n), F32)

        @pl.when(jnp.logical_and(i == 0, c == 0))
        def _():
            _shift_matrices(shift, shift_h, tm, hb)

        dact = lax.dot_general(df_s[...], wd_ref[...], _CONTRACT_LAST, preferred_element_type=F32)
        g, v = u_ref[0, 0].astype(F32), u_ref[1, 0].astype(F32)
        sg = _sigmoid(g)
        sil = g * sg
        act_ref[0] = (sil * v).astype(BF16)
        dug = dact * v * (sg + sil * (1.0 - sg))
        duv = dact * sil
        du_s[0], du_s[1] = dug, duv
        dub_s[0], dub_s[1] = dug.astype(BF16), duv.astype(BF16)
        dh = None
        for gv in range(2):
            _next_rows(shift, shift_h, dub_s[gv], hal[c, gv], d12_s, tm)
            hal[c, gv] = dub_s[gv, 0:hb, :]
            du, d1, d2 = du_s[gv], d12_s[0:tm, :], d12_s[tm:2 * tm, :]
            dup = (cw_ref[gv, 0, 2:3, :] * du + cw_ref[gv, 0, 1:2, :] * d1 + cw_ref[gv, 0, 0:1, :] * d2).astype(BF16)
            dup_ref[gv, 0] = dup
            upc = up_ref[gv, 0].astype(F32)
            dcw_ref[gv, c, 2:3, :] += jnp.sum(upc * du, axis=0, keepdims=True)
            dcw_ref[gv, c, 1:2, :] += jnp.sum(upc * d1, axis=0, keepdims=True)
            dcw_ref[gv, c, 0:1, :] += jnp.sum(upc * d2, axis=0, keepdims=True)
            part = lax.dot_general(dup, wup_ref[gv, 0, 0], _CONTRACT_LAST, preferred_element_type=F32)
            dh = part if dh is None else dh + part
        _acc_out(acc, c == 0, dh)

        @pl.when(c == nc - 1)
        def _():
            dx, dg = _rms_bwd(x_ref[...], gpre_ref[...], acc[...])
            dx_ref[...] = incoming() + dx
            _acc_out(dgpre_ref, i == 0, dg)

    rrow = lambda w: pl.BlockSpec((tm, w), lambda i, c: (nt - 1 - i, 0))
    blk = pl.BlockSpec((2, 1, tm, n), lambda i, c: (0, c, nt - 1 - i, 0))
    one = _full((1, D_MODEL))
    return pl.pallas_call(
        body, name="ffn_bwd", grid=(nt, nc),
        in_specs=[rrow(D_MODEL), rrow(D_MODEL), one, rrow(D_MODEL), one, blk, blk,
                  pl.BlockSpec((2, 1, 3, n), lambda i, c: (0, c, 0, 0)),
                  pl.BlockSpec((n, D_MODEL), lambda i, c: (c, 0)),
                  pl.BlockSpec((2, 1, 1, D_MODEL, n), lambda i, c: (0, c, layer, 0, 0))]
                 + ([] if target is None else [rrow(D_MODEL)]),
        out_specs=[rrow(D_MODEL), pl.BlockSpec((1, tm, n), lambda i, c: (c, nt - 1 - i, 0)), blk, rrow(D_MODEL),
                   one, one, _full((2, nc, 8, n))] + ([] if target is None else [_full((1, 128))]),
        out_shape=[jax.ShapeDtypeStruct((t, D_MODEL), BF16), jax.ShapeDtypeStruct((nc, t, n), BF16),
                   jax.ShapeDtypeStruct((2, nc, t, n), BF16), jax.ShapeDtypeStruct((t, D_MODEL), F32),
                   jax.ShapeDtypeStruct((1, D_MODEL), F32), jax.ShapeDtypeStruct((1, D_MODEL), F32),
                   jax.ShapeDtypeStruct((2, nc, 8, n), F32)]
                  + ([] if target is None else [jax.ShapeDtypeStruct((1, 128), F32)]),
        scratch_shapes=[pltpu.VMEM((tm, D_MODEL), BF16), pltpu.VMEM((tm, D_MODEL), F32),
                        pltpu.VMEM((2, tm, n), F32), pltpu.VMEM((2, tm, n), BF16), pltpu.VMEM((2 * tm, n), F32),
                        pltpu.VMEM((nc, 2, hb, n), BF16), pltpu.VMEM((2 * tm, tm), BF16), pltpu.VMEM((hb, hb), BF16)],
        compiler_params=_cp(("arbitrary", "arbitrary")),
    )(f, dxo, gpost, x_in, gpre, up, u, cw, wd, wup, *([] if target is None else [target]))


def _od_bwd(m, dxo, gpost, x_in, gpre, z, cv, cw, w_out, wint):
    t = m.shape[0]
    tm = min(TM_BWD, t)
    nt = t // tm
    hb = 2 * HALO3

    def body(m_ref, dxo_ref, gpost_ref, x_ref, gpre_ref, z_ref, cv_ref, cw_ref, wo_ref, wi_ref,
             dm_ref, dz_ref, dx_ref, dgpost_ref, dgpre_ref, dcw_ref, dcvb_s, d12_s, dz_s, hal, shift, shift_h):
        i = pl.program_id(0)
        dxo = dxo_ref[...]
        dm, dg = _rms_bwd(m_ref[...], gpost_ref[...], dxo)
        dmb = dm.astype(BF16)
        dm_ref[...] = dmb
        _acc_out(dgpost_ref, i == 0, dg)

        @pl.when(i == 0)
        def _():
            hal[...] = jnp.zeros((hb, SC_DIM), BF16)
            dcw_ref[...] = jnp.zeros((8, SC_DIM), F32)
            _shift_matrices(shift, shift_h, tm, hb)

        dy = lax.dot_general(dmb, wo_ref[...], _CONTRACT_LAST, preferred_element_type=F32)
        z = z_ref[...].astype(F32)
        b, cg, u = z[:, :SC_DIM], z[:, SC_DIM:2 * SC_DIM], z[:, 2 * SC_DIM:]
        dz_s[:, 0:SC_DIM] = (dy * cv_ref[...].astype(F32)).astype(BF16)
        dcv = dy * b
        dcvb_s[...] = dcv.astype(BF16)
        _next_rows(shift, shift_h, dcvb_s[...], hal[...], d12_s, tm)
        hal[...] = dcvb_s[0:hb, :]
        d1, d2 = d12_s[0:tm, :], d12_s[tm:2 * tm, :]
        dcu = cw_ref[2:3, :] * dcv + cw_ref[1:2, :] * d1 + cw_ref[0:1, :] * d2
        cu = cg * u
        dcw_ref[2:3, :] += jnp.sum(cu * dcv, axis=0, keepdims=True)
        dcw_ref[1:2, :] += jnp.sum(cu * d1, axis=0, keepdims=True)
        dcw_ref[0:1, :] += jnp.sum(cu * d2, axis=0, keepdims=True)
        dz_s[:, SC_DIM:2 * SC_DIM] = (dcu * u).astype(BF16)
        dz_s[:, 2 * SC_DIM:3 * SC_DIM] = (dcu * cg).astype(BF16)
        dz_ref[...] = dz_s[...]
        dh = jnp.dot(dz_s[...], wi_ref[...], preferred_element_type=F32)
        dx, dg2 = _rms_bwd(x_ref[...], gpre_ref[...], dh)
        dx_ref[...] = dxo + dx
        _acc_out(dgpre_ref, i == 0, dg2)

    rrow = lambda w: pl.BlockSpec((tm, w), lambda i: (nt - 1 - i, 0))
    one = _full((1, D_MODEL))
    return pl.pallas_call(
        body, name="od_bwd", grid=(nt,),
        in_specs=[rrow(D_MODEL), rrow(D_MODEL), one, rrow(D_MODEL), one, rrow(3 * SC_DIM), rrow(SC_DIM),
                  _full((3, SC_DIM)), _full((SC_DIM, D_MODEL)), _full((3 * SC_DIM, D_MODEL))],
        out_specs=[rrow(D_MODEL), rrow(3 * SC_DIM), rrow(D_MODEL), one, one, _full((8, SC_DIM))],
        out_shape=[jax.ShapeDtypeStruct((t, D_MODEL), BF16), jax.ShapeDtypeStruct((t, 3 * SC_DIM), BF16),
                   jax.ShapeDtypeStruct((t, D_MODEL), F32), jax.ShapeDtypeStruct((1, D_MODEL), F32),
                   jax.ShapeDtypeStruct((1, D_MODEL), F32), jax.ShapeDtypeStruct((8, SC_DIM), F32)],
        scratch_shapes=[pltpu.VMEM((tm, SC_DIM), BF16), pltpu.VMEM((2 * tm, SC_DIM), F32),
                        pltpu.VMEM((tm, 3 * SC_DIM), BF16), pltpu.VMEM((hb, SC_DIM), BF16),
                        pltpu.VMEM((2 * tm, tm), BF16), pltpu.VMEM((hb, hb), BF16)],
        compiler_params=_cp(("arbitrary",)),
    )(m, dxo, gpost, x_in, gpre, z, cv, cw, w_out, wint)


def _ev_bwd1(m, dxo, gpost, w_out):
    t = m.shape[0]
    tm = min(TM_ROW, t)

    def body(m_ref, dxo_ref, g_ref, wot_ref, dm_ref, da_ref, do_ref, dg_ref):
        dm, dg = _rms_bwd(m_ref[...].astype(F32), g_ref[...], dxo_ref[...])
        dmb = dm.astype(BF16)
        dm_ref[...] = dmb
        _acc_out(dg_ref, pl.program_id(0) == 0, dg)
        dao = lax.dot_general(dmb, wot_ref[...], _CONTRACT_LAST, preferred_element_type=F32)
        da_ref[...] = dao[:, :A_CH]
        do_ref[...] = dao[:, A_CH:].astype(BF16)

    return pl.pallas_call(
        body, name="ev_bwd1", grid=(t // tm,),
        in_specs=[_rows(tm, D_MODEL), _rows(tm, D_MODEL), _full((1, D_MODEL)), _full((A_CH + Q_DIM, D_MODEL))],
        out_specs=[_rows(tm, D_MODEL), _rows(tm, A_CH), _rows(tm, Q_DIM), _full((1, D_MODEL))],
        out_shape=[jax.ShapeDtypeStruct((t, D_MODEL), BF16), jax.ShapeDtypeStruct((t, A_CH), F32),
                   jax.ShapeDtypeStruct((t, Q_DIM), BF16), jax.ShapeDtypeStruct((1, D_MODEL), F32)],
        compiler_params=_cp(("arbitrary",)),
    )(m, dxo, gpost, w_out)


def _conf_bwd(da, cv, zag, conv_w, ln_g, ln_b):
    t = da.shape[0]
    tm = min(TM, t)
    nt = t // tm
    rows = tm + HALO31 - 8

    def body(da_ref, c_ref, z_ref, w_ref, g_ref, lb_ref, dz_ref, dw_ref, dv_ref, ext_out, cbuf, glu_s, dglu_s):
        i = pl.program_id(0)

        @pl.when(i == 0)
        def _():
            ext_out[tm:tm + HALO31, :] = jnp.zeros((HALO31, A_CH), F32)
            dw_ref[...] = jnp.zeros((32, A_CH), F32)
            dv_ref[...] = jnp.zeros((8, A_CH), F32)

        x = c_ref[...]
        mu = jnp.mean(x, axis=-1, keepdims=True)
        xc = x - mu
        rstd = lax.rsqrt(jnp.mean(xc * xc, axis=-1, keepdims=True) + LN_EPS)
        xh = xc * rstd
        ln = xh * g_ref[...] + lb_ref[...]
        sl = _sigmoid(ln)
        dln = da_ref[...] * (sl * (1.0 + ln * (1.0 - sl)))
        dxh = dln * g_ref[...]
        dc = rstd * (dxh - jnp.mean(dxh, axis=-1, keepdims=True) - xh * jnp.mean(dxh * xh, axis=-1, keepdims=True))
        dv_ref[0:1, :] += jnp.sum(dc, axis=0, keepdims=True)
        dv_ref[1:2, :] += jnp.sum(dln * xh, axis=0, keepdims=True)
        dv_ref[2:3, :] += jnp.sum(dln, axis=0, keepdims=True)

        ext_out[0:tm, :] = dc
        _tap_copies(ext_out, cbuf, lambda b: b, rows)
        z = z_ref[...].astype(F32)
        al, sg = z[:, :A_CH], _sigmoid(z[:, A_CH:])
        glu_s[...] = al * sg
        for rs in range(0, tm, SUB):
            for cs in range(0, A_CH, 128):
                glu = glu_s[rs:rs + SUB, cs:cs + 128]
                acc = jnp.zeros((SUB, 128), F32)
                for k in range(A_CONV):
                    lag_a, lag_b = divmod(k, 8)
                    r0 = 8 * lag_a + rs
                    d = (ext_out[r0:r0 + SUB, cs:cs + 128] if lag_b == 0
                         else cbuf[lag_b - 1, r0:r0 + SUB, cs:cs + 128])
                    j = A_CONV - 1 - k
                    acc = acc + w_ref[j:j + 1, cs:cs + 128] * d
                    dw_ref[j:j + 1, cs:cs + 128] += jnp.sum(glu * d, axis=0, keepdims=True)
                dglu_s[rs:rs + SUB, cs:cs + 128] = acc
        dglu = dglu_s[...]
        ext_out[tm:tm + HALO31, :] = ext_out[0:HALO31, :]
        dz_ref[:, 0:A_CH] = (dglu * sg).astype(BF16)
        dz_ref[:, A_CH:2 * A_CH] = (dglu * al * sg * (1.0 - sg)).astype(BF16)

    rrow = lambda w: pl.BlockSpec((tm, w), lambda i: (nt - 1 - i, 0))
    return pl.pallas_call(
        body, name="conf_bwd", grid=(nt,),
        in_specs=[rrow(A_CH), rrow(A_CH), rrow(2 * A_CH), _full((32, A_CH)), _full((1, A_CH)), _full((1, A_CH))],
        out_specs=[rrow(2 * A_CH), _full((32, A_CH)), _full((8, A_CH))],
        out_shape=[jax.ShapeDtypeStruct((t, 2 * A_CH), BF16), jax.ShapeDtypeStruct((32, A_CH), F32),
                   jax.ShapeDtypeStruct((8, A_CH), F32)],
        scratch_shapes=[pltpu.VMEM((tm + HALO31, A_CH), F32), pltpu.VMEM((7, rows, A_CH), F32),
                        pltpu.VMEM((tm, A_CH), F32), pltpu.VMEM((tm, A_CH), F32)],
        compiler_params=_cp(("arbitrary",)),
    )(da, cv, zag, conv_w, ln_g, ln_b)


def _attn_bwd(q, k, v, do, sinks):
    t = q.shape[0]
    nb = min(ATT_NB, t // BLOCK)
    rows = nb * BLOCK
    ns = t // rows

    def body(s_ref, q_ref, kc_ref, kp_ref, vc_ref, vp_ref, do_ref, dq_ref, dk_ref, dv_ref, ds_ref, dkc, dvc):
        i = pl.program_id(0)
        r = ns - 1 - i

        @pl.when(i == 0)
        def _():
            dkc[...] = jnp.zeros_like(dkc)
            dvc[...] = jnp.zeros_like(dvc)
            ds_ref[...] = jnp.zeros_like(ds_ref)

        lane = lax.broadcasted_iota(jnp.int32, (1, N_Q_HEADS), 1)
        dsv = jnp.zeros((1, N_Q_HEADS), F32)
        for b in range(nb - 1, -1, -1):
            lo = BLOCK * b
            mask = _attn_mask(r == 0) if b == 0 else _attn_mask(False)
            qv, dov = q_ref[lo:lo + BLOCK, :], do_ref[lo:lo + BLOCK, :]
            kc, vc = kc_ref[lo:lo + BLOCK, :], vc_ref[lo:lo + BLOCK, :]
            kp = kp_ref[...] if b == 0 else kc_ref[lo - BLOCK:lo, :]
            vp = vp_ref[...] if b == 0 else vc_ref[lo - BLOCK:lo, :]
            for h in range(N_KV_HEADS):
                q4, do4 = _q_heads(qv, h), _q_heads(dov, h)
                k2, v2 = _kv_head(kp, kc, h), _kv_head(vp, vc, h)
                pn, ps = _attn_probs(q4, k2, mask, _sink_rows(s_ref, h))
                dp = lax.dot_general(do4, v2, _CONTRACT_LAST, preferred_element_type=F32)
                dl = jnp.sum(pn * dp, axis=-1, keepdims=True)
                dsb = (pn * (dp - dl)).astype(BF16)
                dq4 = (jnp.dot(dsb, k2, preferred_element_type=F32) * SCALE).astype(BF16)
                for g in range(GROUP):
                    c0 = HEAD_DIM * (GROUP * h + g)
                    dq_ref[lo:lo + BLOCK, c0:c0 + HEAD_DIM] = dq4[BLOCK * g:BLOCK * (g + 1), :]
                dk2 = lax.dot_general(dsb, q4, _CONTRACT_FIRST, preferred_element_type=F32) * SCALE
                dv2 = lax.dot_general(pn.astype(BF16), do4, _CONTRACT_FIRST, preferred_element_type=F32)
                dk_ref[lo:lo + BLOCK, HEAD_DIM * h:HEAD_DIM * (h + 1)] = dk2[BLOCK:, :] + dkc[h]
                dv_ref[lo:lo + BLOCK, HEAD_DIM * h:HEAD_DIM * (h + 1)] = dv2[BLOCK:, :] + dvc[h]
                dkc[h] = dk2[:BLOCK, :]
                dvc[h] = dv2[:BLOCK, :]
                srow = -ps * dl
                for g in range(GROUP):
                    dsv = dsv + jnp.where(lane == GROUP * h + g, jnp.sum(srow[BLOCK * g:BLOCK * (g + 1), :]), 0.0)
        ds_ref[...] += dsv

    cur = lambda n: pl.BlockSpec((rows, n), lambda i: (ns - 1 - i, 0))
    prev = lambda n: pl.BlockSpec((BLOCK, n), lambda i: (jnp.maximum((ns - 1 - i) * nb - 1, 0), 0))
    return pl.pallas_call(
        body, name="attn_bwd", grid=(ns,),
        in_specs=[pl.BlockSpec(memory_space=pltpu.SMEM), cur(Q_DIM), cur(KV_DIM), prev(KV_DIM), cur(KV_DIM),
                  prev(KV_DIM), cur(Q_DIM)],
        out_specs=[cur(Q_DIM), cur(KV_DIM), cur(KV_DIM), _full((1, N_Q_HEADS))],
        out_shape=[jax.ShapeDtypeStruct((t, Q_DIM), BF16), jax.ShapeDtypeStruct((t, KV_DIM), F32),
                   jax.ShapeDtypeStruct((t, KV_DIM), F32), jax.ShapeDtypeStruct((1, N_Q_HEADS), F32)],
        scratch_shapes=[pltpu.VMEM((N_KV_HEADS, BLOCK, HEAD_DIM), F32), pltpu.VMEM((N_KV_HEADS, BLOCK, HEAD_DIM), F32)],
        compiler_params=_cp(("arbitrary",)),
    )(sinks, q, k, k, v, v, do)


def _ev_dz(dzag, dq, dk, dv, rc, rsa, rsb):
    t = dzag.shape[0]
    tm = min(TM_ROW, t)

    def body(dzag_ref, dq_ref, dk_ref, dv_ref, c_ref, sa_ref, sb_ref, dz_ref):
        c, sa, sb = c_ref[...], sa_ref[...], sb_ref[...]
        dz_ref[:, 0:2 * A_CH] = dzag_ref[...]
        q0 = 2 * A_CH
        for j in range(Q_DIM // 128):
            d = dq_ref[:, 128 * j:128 * (j + 1)].astype(F32)
            dz_ref[:, q0 + 128 * j:q0 + 128 * (j + 1)] = _rope_bwd(d, c, sa, sb).astype(BF16)
        k0 = q0 + Q_DIM
        dz_ref[:, k0:k0 + KV_DIM] = _rope_bwd(dk_ref[...], c, sa, sb).astype(BF16)
        dz_ref[:, k0 + KV_DIM:k0 + 2 * KV_DIM] = dv_ref[...].astype(BF16)

    return pl.pallas_call(
        body, name="ev_dz", grid=(t // tm,),
        in_specs=[_rows(tm, 2 * A_CH), _rows(tm, Q_DIM), _rows(tm, KV_DIM), _rows(tm, KV_DIM),
                  _rows(tm, 128), _rows(tm, 128), _rows(tm, 128)],
        out_specs=_rows(tm, EVEN_IN),
        out_shape=jax.ShapeDtypeStruct((t, EVEN_IN), BF16),
        compiler_params=_cp(("arbitrary",)),
    )(dzag, dq, dk, dv, rc, rsa, rsb)


def _prep_ev(gat):
    p = {}
    p["ev_w_in"] = gat["ev_w_in"][:, 0].transpose(1, 0, 2).reshape(D_MODEL, EVEN_IN)
    p["ev_w_out"] = gat["ev_w_out"].reshape(A_CH + Q_DIM, D_MODEL)
    return p


def _prep_rest(gat):
    p = {}
    g = gat["od_w_in"][:, 0]
    p["od_w_in"] = g.transpose(1, 0, 2).reshape(1, D_MODEL, 3 * SC_DIM)
    p["od_w_in_t"] = g.transpose(0, 2, 1).reshape(3 * SC_DIM, D_MODEL)
    p["od_w_out"] = gat["od_w_out"].reshape(SC_DIM, D_MODEL)
    p["ffn_w_up"] = gat["ffn_w_up"].reshape(2, N_DEV // 2, 2, D_MODEL, FF_N)
    p["ffn_w_down"] = [gat["ffn_w_down"][:, i].reshape(D_FF, D_MODEL) for i in range(2)]
    return p


def _local_step(x, positions, target, p, rest_weights, s, token, grads_ready):
    row = lambda a, tok=None: a.reshape(1, -1) if tok is None else a.reshape(1, -1) + tok
    nc = N_DEV // 2
    rc, rsa, rsb = _rope_tables(positions)
    conv31 = jnp.pad(s["ev_a_conv_w"][0], ((0, 1), (0, 0)))
    cw_ffn = [s["ffn_conv_w"][i].reshape(3, 2, nc, FF_N).transpose(1, 2, 0, 3) for i in range(2)]
    sinks = s["ev_sinks"][0]
    big, g = {}, {}

    h0, zag, q, k, v = _ev_in(x, row(s["mix_norm_pre"][0], token), p["ev_w_in"], rc, rsa, rsb)
    cv, a = _conf_fwd(zag, conv31, s["ev_a_conv_b"], s["ev_a_ln_g"], s["ev_a_ln_b"])
    o = _attn_fwd(q, k, v, sinks)
    wo = p["ev_w_out"]
    m0, x1 = _out_post([a, o], [wo[:A_CH], wo[A_CH:]], x, row(s["mix_norm_post"][0]))
    p = {**p, **rest_weights(m0)}
    h1, up0, u0, f0, x2 = _ffn_fwd(x1, row(s["ffn_norm_pre"][0]), p["ffn_w_up"], 0, cw_ffn[0], p["ffn_w_down"][0],
                                   row(s["ffn_norm_post"][0]))
    h2, z, cv1, y, m1, x3 = _od_fwd(x2, row(s["mix_norm_pre"][1]), p["od_w_in"], s["od_conv_w"][0], p["od_w_out"],
                                    row(s["mix_norm_post"][1]))
    h3, up1, u1, f1, x4 = _ffn_fwd(x3, row(s["ffn_norm_pre"][1]), p["ffn_w_up"], 1, cw_ffn[1], p["ffn_w_down"][1],
                                   row(s["ffn_norm_post"][1]))

    def ffn_back(i, f, dxo, up, u, h, x_in, bufs, tok=None, tgt=None):
        df, act, dup, dx_in, dgpost, dgpre, dcw, *loss = _ffn_bwd(
            f, dxo, row(s["ffn_norm_post"][i], tok), x_in, row(s["ffn_norm_pre"][i]), up, u, cw_ffn[i],
            p["ffn_w_down"][i], p["ffn_w_up"], i, tgt)
        bufs = (_dw_up(h, dup.reshape(N_DEV, -1, FF_N), i, bufs[0]), _dw_down(act, df, i, bufs[1]))
        return dx_in, dgpost, dgpre, dcw[:, :, 0:3].transpose(2, 0, 1, 3).reshape(3, 2 * D_FF), bufs, loss

    dx, dgfpost1, dgfpre1, dcw1, bufs, (lpart,) = ffn_back(1, f1, x4, up1, u1, h3, x3, (None, None), None, target)

    dm1, dz, dx, dgpost1, dgpre1, dcw_od = _od_bwd(m1, dx, row(s["mix_norm_post"][1]), x2, row(s["mix_norm_pre"][1]), z,
                                                   cv1, s["od_conv_w"][0], p["od_w_out"], p["od_w_in_t"])
    big["od_w_out"] = _dw2d(y, dm1, SC_DIM, D_MODEL).reshape(N_DEV, -1, D_MODEL)
    big["od_w_in"], big["od_w_in:bf16"] = _dw_cols(h2, dz, 3 * SC_DIM // N_DEV)
    g["od_conv_w"] = dcw_od[None, 0:3]
    tok = grads_ready(["od_w_in", "od_w_out"], big)

    dx, dgfpost0, dgfpre0, dcw0, bufs, _ = ffn_back(0, f0, dx, up0, u0, h1, x1, bufs, tok)
    (big["ffn_w_up"], big["ffn_w_up:bf16"]), (big["ffn_w_down"], big["ffn_w_down:bf16"]) = bufs
    tok = grads_ready(["ffn_w_up", "ffn_w_down"], big)

    dm0, da, do, dgpost0 = _ev_bwd1(m0, dx, row(s["mix_norm_post"][0], tok), p["ev_w_out"])
    big["ev_w_out"] = jnp.concatenate([_dw2d(a, dm0, A_CH, D_MODEL), _dw2d(o, dm0, Q_DIM, D_MODEL)],
                                      axis=0).reshape(N_DEV, -1, D_MODEL)
    tok = grads_ready(["ev_w_out"], big)
    dzag, dcw31, dvec = _conf_bwd(da, cv, zag, conv31, s["ev_a_ln_g"] + tok, s["ev_a_ln_b"])
    dq, dk, dv, dsinks = _attn_bwd(q, k, v, do, sinks)
    dz0 = _ev_dz(dzag, dq, dk, dv, rc, rsa, rsb)
    dw_in = _dw2d(h0, dz0, D_MODEL, EVEN_IN // 2)
    big["ev_w_in"] = dw_in.reshape(D_MODEL, N_DEV, EVEN_IN // N_DEV).transpose(1, 0, 2)
    tok = grads_ready(["ev_w_in"], big)
    dx, dgpre0 = _dz_wt_rms_bwd(dz0, p["ev_w_in"], x, row(s["mix_norm_pre"][0], tok), dx)

    g["mix_norm_pre"] = jnp.concatenate([dgpre0, dgpre1], axis=0)
    g["mix_norm_post"] = jnp.concatenate([dgpost0, dgpost1], axis=0)
    g["ffn_norm_pre"] = jnp.concatenate([dgfpre0, dgfpre1], axis=0)
    g["ffn_norm_post"] = jnp.concatenate([dgfpost0, dgfpost1], axis=0)
    g["ev_a_conv_w"] = dcw31[None, 0:A_CONV]
    g["ev_a_conv_b"], g["ev_a_ln_g"], g["ev_a_ln_b"] = dvec[0:1], dvec[1:2], dvec[2:3]
    g["ev_sinks"] = dsinks
    g["ffn_conv_w"] = jnp.stack([dcw0, dcw1])
    return lpart[0, 0], dx, big, g


MESH = pl.DeviceIdType.MESH


def _all_gather(shards, name):
    nw = len(shards)

    def body(*refs):
        x_refs, out_refs = refs[:nw], refs[nw:2 * nw]
        send_sems, recv_sems, local_sems = refs[2 * nw:]
        x, y, c = lax.axis_index("x"), lax.axis_index("y"), lax.axis_index("c")
        me, sibling = (x, y, c), (x, y, 1 - c)
        chips = [(1 - x, y), (x, 1 - y), (1 - x, 1 - y)]

        def rows(w, px, py, pc):
            m_per = shards[w].shape[0]
            return out_refs[w].at[pl.ds((4 * px + 2 * py + pc) * m_per, m_per), :]

        def copy(w, k, block, to, src=None):
            return pltpu.make_async_remote_copy(
                src_ref=rows(w, *block) if src is None else src, dst_ref=rows(w, *block),
                send_sem=send_sems.at[w, k], recv_sem=recv_sems.at[w, k], device_id=to, device_id_type=MESH)

        mine, first, passed = [], [], []
        for w in range(nw):
            cp = pltpu.make_async_copy(x_refs[w], rows(w, *me), local_sems.at[w])
            cp.start()
            mine.append(cp)
            first.append([copy(w, 0, me, sibling, src=x_refs[w])]
                         + [copy(w, 1 + j, me, (*chip, c), src=x_refs[w]) for j, chip in enumerate(chips)])
            for cp in first[w]:
                cp.start()
        for w in range(nw):
            passed.append([copy(w, 4 + j, (*chip, c), sibling) for j, chip in enumerate(chips)])
            for j, chip in enumerate(chips):
                copy(w, 1 + j, (*chip, c), me).wait_recv()
                passed[w][j].start()
        for w in range(nw):
            copy(w, 0, sibling, me).wait_recv()
            for j, chip in enumerate(chips):
                copy(w, 4 + j, (*chip, 1 - c), me).wait_recv()
            for cp in first[w] + passed[w]:
                cp.wait_send()
            mine[w].wait()

    return pl.pallas_call(
        body, name=name,
        out_shape=[jax.ShapeDtypeStruct((N_DEV * a.shape[0], a.shape[1]), a.dtype) for a in shards],
        in_specs=[_ANY] * nw, out_specs=[_ANY] * nw,
        scratch_shapes=[pltpu.SemaphoreType.DMA((nw, 7)), pltpu.SemaphoreType.DMA((nw, 7)),
                        pltpu.SemaphoreType.DMA((nw,))],
    )(*shards)


_HBM = pl.BlockSpec(memory_space=pltpu.HBM)
_SEM = pl.BlockSpec(memory_space=pltpu.SEMAPHORE)
_EFFECT = pltpu.SideEffectType.DATAFLOW_SIDE_EFFECTING
_RELATIONS = [(dx, dy, dc) for dx in (0, 1) for dy in (0, 1) for dc in (0, 1)][1:]


def _peer(rel):
    x, y, c = lax.axis_index("x"), lax.axis_index("y"), lax.axis_index("c")
    px, py, pc = x ^ rel[0], y ^ rel[1], c ^ rel[2]
    return (px, py, pc), 4 * px + 2 * py + pc, 4 * x + 2 * y + c


def _exchange_copy(k, rel, src_ref, land_ref, send_sems, recv_sems, w, scatter):
    peer, peer_idx, my_idx = _peer(rel)
    src = src_ref.at[peer_idx] if scatter else src_ref
    return pltpu.make_async_remote_copy(
        src_ref=src, dst_ref=land_ref.at[my_idx], send_sem=send_sems.at[_sends(scatter) * w + k],
        recv_sem=recv_sems.at[7 * w + k], device_id=peer, device_id_type=MESH)


def _sends(scatter):
    return 7 if scatter else 8


def _own_copy(src_ref, land_ref, send_sems, w):
    my_idx = _peer(_RELATIONS[0])[2]
    return pltpu.make_async_copy(src_ref, land_ref.at[my_idx], send_sems.at[8 * w + 7])


def _exchange_start(srcs, scatter, name):
    nw = len(srcs)
    lands = [lax.empty((N_DEV,) + (a.shape[1:] if scatter else a.shape), a.dtype) for a in srcs]

    def body(*refs):
        src_refs, land_refs = refs[:nw], refs[nw:2 * nw]
        send_sems, recv_sems = refs[2 * nw], refs[2 * nw + 1]
        token = refs[-1]
        for w in range(nw):
            for k, rel in enumerate(_RELATIONS):
                _exchange_copy(k, rel, src_refs[w], land_refs[w], send_sems, recv_sems, w, scatter).start()
            if not scatter:
                _own_copy(src_refs[w], land_refs[w], send_sems, w).start()
        token[...] = jnp.zeros_like(token)

    hbm = lambda a: pltpu.HBM(a.shape, a.dtype)
    outs = pl.pallas_call(
        body, name=name,
        out_shape=(pltpu.SemaphoreType.DMA((_sends(scatter) * nw,)), pltpu.SemaphoreType.DMA((7 * nw,)),
                   *[hbm(a) for a in srcs],
                   *[hbm(a) for a in lands], jax.ShapeDtypeStruct((8, 128), F32)),
        in_specs=[_HBM] * (2 * nw),
        out_specs=(_SEM, _SEM, *[_HBM] * (2 * nw), pl.BlockSpec(memory_space=pltpu.VMEM)),
        input_output_aliases={i: 2 + i for i in range(2 * nw)},
        compiler_params=pltpu.CompilerParams(has_side_effects=_EFFECT),
    )(*[pltpu.with_memory_space_constraint(a, pltpu.HBM) for a in srcs],
      *[pltpu.with_memory_space_constraint(a, pltpu.HBM) for a in lands])
    return outs[0], outs[1], list(outs[2:2 + nw]), list(outs[2 + nw:2 + 2 * nw]), outs[-1]


def _exchange_wait(started, scatter, after, name):
    send_sems, recv_sems, srcs, lands, _ = started
    nw = len(srcs)

    def body(*refs):
        src_refs, land_refs = refs[:nw], refs[nw:2 * nw]
        send_s, recv_s = refs[2 * nw], refs[2 * nw + 1]
        for w in range(nw):
            for k, rel in enumerate(_RELATIONS):
                cp = _exchange_copy(k, rel, src_refs[w], land_refs[w], send_s, recv_s, w, scatter)
                cp.wait_send()
                _, peer_idx, _ = _peer(rel)
                pltpu.make_async_remote_copy(
                    src_ref=src_refs[w].at[peer_idx] if scatter else src_refs[w], dst_ref=land_refs[w].at[peer_idx],
                    send_sem=send_s.at[_sends(scatter) * w + k], recv_sem=recv_s.at[7 * w + k],
                    device_id=_peer(rel)[0], device_id_type=MESH).wait_recv()
            if not scatter:
                _own_copy(src_refs[w], land_refs[w], send_s, w).wait()

    hbm = lambda a: pltpu.HBM(a.shape, a.dtype)
    outs = pl.pallas_call(
        body, name=name, out_shape=tuple(hbm(a) for a in srcs + lands),
        in_specs=[_HBM] * (2 * nw) + [_SEM, _SEM, _ANY], out_specs=tuple([_HBM] * (2 * nw)),
        input_output_aliases={i: i for i in range(2 * nw)},
        compiler_params=pltpu.CompilerParams(has_side_effects=_EFFECT),
    )(*srcs, *lands, send_sems, recv_sems, after)
    return list(outs[nw:])


def _to_bf16(a):
    _, r, l = a.shape
    tr = _row_tile(r, 512)

    def body(a_ref, o_ref):
        o_ref[...] = a_ref[...].astype(BF16)

    spec = pl.BlockSpec((1, tr, l), lambda j, i: (j, i, 0))
    return pl.pallas_call(
        body, name="to_bf16", grid=(N_DEV, r // tr), in_specs=[spec], out_specs=spec,
        out_shape=jax.ShapeDtypeStruct(a.shape, BF16), compiler_params=_cp(("arbitrary", "arbitrary")),
    )(a)


def _row_tile(rows, cap):
    best = None
    for d in range(16, min(rows, cap) + 1, 16):
        if rows % d == 0:
            best = d
    return rows if best is None else best


def _adam_math(w, g, m, v):
    bc1 = 1.0 - ADAM_B1 ** ADAM_STEP
    bc2 = 1.0 - ADAM_B2 ** ADAM_STEP
    mn = ADAM_B1 * m + (1.0 - ADAM_B1) * g
    vn = ADAM_B2 * v + (1.0 - ADAM_B2) * (g * g)
    return -ADAM_LR * ((mn / bc1) / (jnp.sqrt(vn / bc2) + ADAM_EPS) + ADAM_WD * w), mn, vn


def _adamw_rs(gp, land, w, m, v, dev):
    _, r, l = gp.shape
    tr = _row_tile(r, 256)

    def body(i_ref, g_ref, b_ref, w_ref, m_ref, v_ref, go_ref, d_ref, mo_ref, vo_ref):
        g = g_ref[0]
        for j in range(N_DEV):
            g = g + jnp.where(i_ref[0] == j, 0.0, b_ref[j].astype(F32))
        go_ref[...] = g
        d_ref[...], mo_ref[...], vo_ref[...] = _adam_math(w_ref[...], g, m_ref[...], v_ref[...])

    spec = pl.BlockSpec((tr, l), lambda i, s: (i, 0))
    return pl.pallas_call(
        body, name="adamw_rs", out_shape=[jax.ShapeDtypeStruct((r, l), F32)] * 4,
        grid_spec=pltpu.PrefetchScalarGridSpec(
            num_scalar_prefetch=1, grid=(r // tr,),
            in_specs=[pl.BlockSpec((1, tr, l), lambda i, s: (s[0], i, 0)),
                      pl.BlockSpec((N_DEV, tr, l), lambda i, s: (0, i, 0)), spec, spec, spec],
            out_specs=[spec] * 4),
        compiler_params=_cp(("arbitrary",)),
    )(dev, gp, land, w, m, v)


def _sum_blocks(a, nblk):
    m = a.shape[0] // nblk
    n = a.shape[1]

    def body(a_ref, o_ref):
        acc = a_ref[0]
        for j in range(1, nblk):
            acc = acc + a_ref[j]
        o_ref[...] = acc

    return pl.pallas_call(
        body, name="sum_blocks", out_shape=jax.ShapeDtypeStruct((m, n), a.dtype),
        in_specs=[_full((nblk, m, n))], out_specs=_full((m, n)),
    )(a.reshape(nblk, m, n))


def _adamw(w, g, m, v):
    rows, c = w.shape

    def body(w_ref, g_ref, m_ref, v_ref, d_ref, mo_ref, vo_ref):
        d_ref[...], mo_ref[...], vo_ref[...] = _adam_math(w_ref[...], g_ref[...], m_ref[...], v_ref[...])

    return pl.pallas_call(
        body, name="adamw", in_specs=[_full((rows, c))] * 4, out_specs=[_full((rows, c))] * 3,
        out_shape=[jax.ShapeDtypeStruct((rows, c), F32)] * 3,
    )(w, g, m, v)


WEIGHTS = ["mix_norm_pre", "mix_norm_post", "ffn_norm_pre", "ffn_norm_post", "ev_w_in", "ev_a_conv_w", "ev_a_conv_b",
           "ev_a_ln_g", "ev_a_ln_b", "ev_sinks", "ev_w_out", "od_w_in", "od_conv_w", "od_w_out", "ffn_w_up",
           "ffn_conv_w", "ffn_w_down"]
BIG = ["ev_w_in", "ev_w_out", "od_w_in", "od_w_out", "ffn_w_up", "ffn_w_down"]
SMALL_REPL = ["mix_norm_pre", "mix_norm_post", "ffn_norm_pre", "ffn_norm_post", "ev_a_conv_b", "ev_a_ln_g",
              "ev_a_ln_b", "ev_sinks"]
SMALL_SHARDED = ["ev_a_conv_w", "od_conv_w", "ffn_conv_w"]


def _pack(arrs, rows):
    flat = jnp.concatenate([a.reshape(-1) for a in arrs])
    return jnp.pad(flat, (0, rows * LANES - flat.shape[0])).reshape(rows, LANES)


def _unpack(packed, shapes):
    flat, out, off = packed.reshape(-1), [], 0
    for s in shapes:
        n = 1
        for d in s:
            n *= d
        out.append(flat[off:off + n].reshape(s))
        off += n
    return out


def kernel(x, positions, mix_norm_pre, mix_norm_post, ffn_norm_pre, ffn_norm_post, ev_w_in, ev_a_conv_w, ev_a_conv_b, ev_a_ln_g, ev_a_ln_b, ev_sinks, ev_w_out, od_w_in, od_conv_w, od_w_out, ffn_w_up, ffn_conv_w, ffn_w_down, loss_target, m_mix_norm_pre, m_mix_norm_post, m_ffn_norm_pre, m_ffn_norm_post, m_ev_w_in, m_ev_a_conv_w, m_ev_a_conv_b, m_ev_a_ln_g, m_ev_a_ln_b, m_ev_sinks, m_ev_w_out, m_od_w_in, m_od_conv_w, m_od_w_out, m_ffn_w_up, m_ffn_conv_w, m_ffn_w_down, v_mix_norm_pre, v_mix_norm_post, v_ffn_norm_pre, v_ffn_norm_post, v_ev_w_in, v_ev_a_conv_w, v_ev_a_conv_b, v_ev_a_ln_g, v_ev_a_ln_b, v_ev_sinks, v_ev_w_out, v_od_w_in, v_od_conv_w, v_od_w_out, v_ffn_w_up, v_ffn_conv_w, v_ffn_w_down):
    w = dict(zip(WEIGHTS, (mix_norm_pre, mix_norm_post, ffn_norm_pre, ffn_norm_post, ev_w_in, ev_a_conv_w, ev_a_conv_b,
                           ev_a_ln_g, ev_a_ln_b, ev_sinks, ev_w_out, od_w_in, od_conv_w, od_w_out, ffn_w_up, ffn_conv_w,
                           ffn_w_down)))
    mom = dict(zip(WEIGHTS, (m_mix_norm_pre, m_mix_norm_post, m_ffn_norm_pre, m_ffn_norm_post, m_ev_w_in, m_ev_a_conv_w,
                             m_ev_a_conv_b, m_ev_a_ln_g, m_ev_a_ln_b, m_ev_sinks, m_ev_w_out, m_od_w_in, m_od_conv_w,
                             m_od_w_out, m_ffn_w_up, m_ffn_conv_w, m_ffn_w_down)))
    var = dict(zip(WEIGHTS, (v_mix_norm_pre, v_mix_norm_post, v_ffn_norm_pre, v_ffn_norm_post, v_ev_w_in, v_ev_a_conv_w,
                             v_ev_a_conv_b, v_ev_a_ln_g, v_ev_a_ln_b, v_ev_sinks, v_ev_w_out, v_od_w_in, v_od_conv_w,
                             v_od_w_out, v_ffn_w_up, v_ffn_conv_w, v_ffn_w_down)))
    ix, iy, ic = lax.axis_index("x"), lax.axis_index("y"), lax.axis_index("c")
    dev = 4 * ix + 2 * iy + ic
    two = lambda a: a.reshape(-1, a.shape[-1])

    dev1 = jnp.reshape(dev, (1,)).astype(jnp.int32)
    shard = {n: two(w[n].astype(BF16)) for n in BIG}
    gathered = lambda n, a: a.reshape((N_DEV,) + w[n].shape)
    ev_names = [n for n in BIG if n.startswith("ev_")]
    ev_gat = _all_gather([shard[n] for n in ev_names] + [_pack([w[n] for n in SMALL_SHARDED], 8)], "gather_ev")
    p = _prep_ev({n: gathered(n, a) for n, a in zip(ev_names, ev_gat)})
    rest_names = [n for n in BIG if not n.startswith("ev_")]
    first = shard[rest_names[0]] + (ev_gat[0][0:1, 0:1] * 0).astype(BF16)
    started = _exchange_start([first] + [shard[n] for n in rest_names[1:]], False, "gather_start")

    def rest_weights(after):
        lands = _exchange_wait(started, False, after, "gather_wait")
        return _prep_rest({n: gathered(n, a) for n, a in zip(rest_names, lands)})

    small = {n: w[n] for n in SMALL_REPL}
    small_shapes = [w[n].shape for n in SMALL_SHARDED]
    conv_gat = ev_gat[len(ev_names)].reshape(N_DEV, 8, LANES)
    per_dev = [_unpack(conv_gat[d], small_shapes) for d in range(N_DEV)]
    for k, n in enumerate(SMALL_SHARDED):
        small[n] = jnp.concatenate([per_dev[d][k] for d in range(N_DEV)], axis=-1)

    exchanges = []

    def grads_ready(names, big):
        blocks = lambda a, n: a.reshape(N_DEV, -1, w[n].shape[-1])
        bufs = [blocks(big[n], n) for n in names]
        payload = [blocks(big[n + ":bf16"], n) if n + ":bf16" in big else _to_bf16(b) for n, b in zip(names, bufs)]
        st = _exchange_start(payload, True, "grads_start_" + names[0])
        exchanges.append((names, bufs, st))
        return st[-1][0, 0]

    lpart, grad_x, big, g = _local_step(x[0], positions[0], loss_target[0], p, rest_weights, small, started[-1][0, 0],
                                        grads_ready)
    loss = lax.psum(lpart, ("x", "y", "c"))

    grads, delta, new_m, new_v = {}, {}, {}, {}
    for names, bufs, st in exchanges:
        lands = _exchange_wait(st, True, grad_x, "grads_wait_" + names[0])
        for n, b, land in zip(names, bufs, lands):
            outs = _adamw_rs(b, land, two(w[n]), two(mom[n]), two(var[n]), dev1)
            grads[n], delta[n], new_m[n], new_v[n] = (a.reshape(w[n].shape) for a in outs)

    small_names = SMALL_REPL + SMALL_SHARDED
    s_all = _sum_blocks(_all_gather([_pack([g[n] for n in small_names], 64)], "gather_small_grads")[0], N_DEV)
    for n, a in zip(small_names, _unpack(s_all, [small[n].shape for n in small_names])):
        if n in SMALL_SHARDED:
            width = w[n].shape[-1]
            a = lax.dynamic_slice_in_dim(a, dev * width, width, axis=a.ndim - 1)
        grads[n] = a
    pk = lambda dct: _pack([dct[n] for n in small_names], 16)
    outs = _adamw(pk(w), pk(grads), pk(mom), pk(var))
    for dst, packed in zip((delta, new_m, new_v), outs):
        for n, a in zip(small_names, _unpack(packed, [w[n].shape for n in small_names])):
            dst[n] = a

    return (loss, grad_x[None], *[grads[n] for n in WEIGHTS], *[delta[n] for n in WEIGHTS],
            *[new_m[n] for n in WEIGHTS], *[new_v[n] for n in WEIGHTS])
```

```python
import jax
import jax.numpy as jnp
from jax import lax
from jax.experimental import pallas as pl
from jax.experimental.pallas import tpu as pltpu

F32, BF16 = jnp.float32, jnp.bfloat16

D_MODEL = 1024
A_CH = 512
A_CONV = 31
Q_DIM = 512
KV_DIM = 128
HEAD_DIM = 64
N_Q_HEADS = 8
N_KV_HEADS = 2
GROUP = 4
BLOCK = 128
EVEN_IN = 1792
SC_DIM = 1024
D_FF = 2816
ROPE_THETA = 500000.0
ROPE_DIM = 16
RMS_EPS = 1e-6
LN_EPS = 1e-5
SCALE = HEAD_DIM ** -0.5
NEG = -1e30

ADAM_LR, ADAM_B1, ADAM_B2, ADAM_EPS, ADAM_WD, ADAM_STEP = 0.001, 0.9, 0.999, 1e-08, 0.01, 10

N_DEV = 8
FF_N = 2 * D_FF // N_DEV
LANES = 1024
HALO3 = 8
HALO31 = 32
VMEM_LIMIT = 56 * 1024 * 1024

TM = 512
TM_ROW = 1024
TM_BWD = 256
TK_DW = 4096
ATT_NB = 4
SUB = 128

_ANY = pl.BlockSpec(memory_space=pl.ANY)
_CONTRACT_LAST = (((1,), (1,)), ((), ()))
_CONTRACT_FIRST = (((0,), (0,)), ((), ()))


def _cp(sem, vmem=VMEM_LIMIT):
    return pltpu.CompilerParams(dimension_semantics=sem, vmem_limit_bytes=vmem)


def _full(shape):
    n = len(shape)
    return pl.BlockSpec(shape, lambda *_: (0,) * n)


def _rows(tm, n):
    return pl.BlockSpec((tm, n), lambda i, *_: (i, 0))


def _sigmoid(x):
    return 0.5 * jnp.tanh(0.5 * x) + 0.5


def _rsqrt_mean(x):
    return lax.rsqrt(jnp.mean(x * x, axis=-1, keepdims=True) + RMS_EPS)


def _rms_bwd(x, g, dy):
    r = _rsqrt_mean(x)
    xh = x * r
    dxh = dy * g
    dx = r * (dxh - xh * jnp.mean(dxh * xh, axis=-1, keepdims=True))
    return dx, jnp.sum(dy * xh, axis=0, keepdims=True)


def _acc_out(ref, first, val):
    @pl.when(first)
    def _():
        ref[...] = val

    @pl.when(jnp.logical_not(first))
    def _():
        ref[...] += val


def _rope_tables(positions):
    half = ROPE_DIM // 2
    inv_freq = ROPE_THETA ** (-(jnp.arange(half, dtype=F32) * 2.0 / ROPE_DIM))
    ang = positions.astype(F32)[:, None] * inv_freq
    cos, sin = jnp.cos(ang), jnp.sin(ang)
    t = positions.shape[0]
    one, zero = jnp.ones((t, HEAD_DIM - ROPE_DIM), F32), jnp.zeros((t, HEAD_DIM - ROPE_DIM), F32)
    z8 = jnp.zeros((t, half), F32)
    c = jnp.concatenate([cos, cos, one], axis=1)
    sa = jnp.concatenate([z8, sin, zero], axis=1)
    sb = jnp.concatenate([-sin, z8, zero], axis=1)
    return tuple(jnp.tile(a, (1, 2)) for a in (c, sa, sb))


def _rope(t, c, sa, sb):
    return t * c + pltpu.roll(t, 8, 1) * sa + pltpu.roll(t, 120, 1) * sb


def _rope_bwd(d, c, sa, sb):
    return d * c + pltpu.roll(d * sa, 120, 1) + pltpu.roll(d * sb, 8, 1)


def _ev_in(x, gpre, w_in, rc, rsa, rsb):
    t = x.shape[0]
    tm = min(TM_ROW, t)

    def body(x_ref, g_ref, w_ref, c_ref, sa_ref, sb_ref, h_ref, zag_ref, q_ref, k_ref, v_ref):
        xv = x_ref[...]
        h = (xv * _rsqrt_mean(xv) * g_ref[...]).astype(BF16)
        h_ref[...] = h
        z = jnp.dot(h, w_ref[...], preferred_element_type=F32)
        zag_ref[...] = z[:, :2 * A_CH].astype(BF16)
        c, sa, sb = c_ref[...], sa_ref[...], sb_ref[...]
        q0 = 2 * A_CH
        for j in range(Q_DIM // 128):
            q_ref[:, 128 * j:128 * (j + 1)] = _rope(z[:, q0 + 128 * j:q0 + 128 * (j + 1)], c, sa, sb).astype(BF16)
        k0 = q0 + Q_DIM
        k_ref[...] = _rope(z[:, k0:k0 + KV_DIM], c, sa, sb).astype(BF16)
        v_ref[...] = z[:, k0 + KV_DIM:k0 + 2 * KV_DIM].astype(BF16)

    return pl.pallas_call(
        body, name="ev_in", grid=(t // tm,),
        in_specs=[_rows(tm, D_MODEL), _full((1, D_MODEL)), _full((D_MODEL, EVEN_IN)),
                  _rows(tm, 128), _rows(tm, 128), _rows(tm, 128)],
        out_specs=[_rows(tm, D_MODEL), _rows(tm, 2 * A_CH), _rows(tm, Q_DIM), _rows(tm, KV_DIM), _rows(tm, KV_DIM)],
        out_shape=[jax.ShapeDtypeStruct((t, D_MODEL), BF16), jax.ShapeDtypeStruct((t, 2 * A_CH), BF16),
                   jax.ShapeDtypeStruct((t, Q_DIM), BF16), jax.ShapeDtypeStruct((t, KV_DIM), BF16),
                   jax.ShapeDtypeStruct((t, KV_DIM), BF16)],
        compiler_params=_cp(("arbitrary",)),
    )(x, gpre, w_in, rc, rsa, rsb)


def _glu(zag):
    z = zag.astype(F32)
    return z[:, :A_CH] * _sigmoid(z[:, A_CH:])


def _tap_copies(ext, cbuf, first_row, rows):
    for b in range(1, 8):
        s = first_row(b)
        cbuf[b - 1] = ext[s:s + rows, :]


def _conf_fwd(zag, conv_w, conv_b, ln_g, ln_b):
    t = zag.shape[0]
    tm = min(TM, t)
    rows = tm + HALO31 - 8

    def body(z_ref, w_ref, b_ref, g_ref, lb_ref, c_ref, a_ref, ext, cbuf):
        i = pl.program_id(0)

        @pl.when(i == 0)
        def _():
            ext[0:HALO31, :] = jnp.zeros((HALO31, A_CH), F32)

        ext[HALO31:HALO31 + tm, :] = _glu(z_ref[...])
        _tap_copies(ext, cbuf, lambda b: 8 - b, rows)
        for rs in range(0, tm, SUB):
            for cs in range(0, A_CH, 128):
                acc = jnp.zeros((SUB, 128), F32)
                for k in range(A_CONV):
                    lag_a, lag_b = divmod(k, 8)
                    r0 = HALO31 - 8 - 8 * lag_a + rs
                    src = (ext[r0 + 8:r0 + 8 + SUB, cs:cs + 128] if lag_b == 0
                           else cbuf[lag_b - 1, r0:r0 + SUB, cs:cs + 128])
                    acc = acc + w_ref[A_CONV - 1 - k:A_CONV - k, cs:cs + 128] * src
                c_ref[rs:rs + SUB, cs:cs + 128] = acc
        ext[0:HALO31, :] = ext[tm:tm + HALO31, :]
        cv = c_ref[...] + b_ref[...]
        c_ref[...] = cv
        mu = jnp.mean(cv, axis=-1, keepdims=True)
        xc = cv - mu
        ln = xc * lax.rsqrt(jnp.mean(xc * xc, axis=-1, keepdims=True) + LN_EPS) * g_ref[...] + lb_ref[...]
        a_ref[...] = (ln * _sigmoid(ln)).astype(BF16)

    return pl.pallas_call(
        body, name="conf_fwd", grid=(t // tm,),
        in_specs=[_rows(tm, 2 * A_CH), _full((32, A_CH)), _full((1, A_CH)), _full((1, A_CH)), _full((1, A_CH))],
        out_specs=[_rows(tm, A_CH), _rows(tm, A_CH)],
        out_shape=[jax.ShapeDtypeStruct((t, A_CH), F32), jax.ShapeDtypeStruct((t, A_CH), BF16)],
        scratch_shapes=[pltpu.VMEM((HALO31 + tm, A_CH), F32), pltpu.VMEM((7, rows, A_CH), F32)],
        compiler_params=_cp(("arbitrary",)),
    )(zag, conv_w, conv_b, ln_g, ln_b)


def _attn_mask(first_block):
    row = lax.broadcasted_iota(jnp.int32, (GROUP * BLOCK, 2 * BLOCK), 0) & (BLOCK - 1)
    col = lax.broadcasted_iota(jnp.int32, (GROUP * BLOCK, 2 * BLOCK), 1)
    diff = row + BLOCK - col
    return (diff >= 0) & (diff < BLOCK) & ((col >= BLOCK) | jnp.logical_not(first_block))


def _sink_rows(s_ref, h):
    grp = lax.broadcasted_iota(jnp.int32, (GROUP * BLOCK, 1), 0) >> 7
    out = jnp.full((GROUP * BLOCK, 1), s_ref[GROUP * h], F32)
    for g in range(1, GROUP):
        out = jnp.where(grp == g, s_ref[GROUP * h + g], out)
    return out


def _attn_probs(q4, k2, mask, sink):
    s = lax.dot_general(q4, k2, _CONTRACT_LAST, preferred_element_type=F32) * SCALE
    s = jnp.where(mask, s, NEG)
    m = jnp.maximum(jnp.max(s, axis=-1, keepdims=True), sink)
    p = jnp.exp(s - m)
    es = jnp.exp(sink - m)
    inv = 1.0 / (jnp.sum(p, axis=-1, keepdims=True) + es)
    return p * inv, es * inv


def _q_heads(q, h):
    return jnp.concatenate([q[:, HEAD_DIM * (GROUP * h + g):HEAD_DIM * (GROUP * h + g + 1)] for g in range(GROUP)],
                           axis=0)


def _kv_head(prev, cur, h):
    return jnp.concatenate([prev[:, HEAD_DIM * h:HEAD_DIM * (h + 1)], cur[:, HEAD_DIM * h:HEAD_DIM * (h + 1)]], axis=0)


def _attn_fwd(q, k, v, sinks):
    t = q.shape[0]
    nb = min(ATT_NB, t // BLOCK)
    rows = nb * BLOCK

    def body(s_ref, q_ref, kc_ref, kp_ref, vc_ref, vp_ref, o_ref):
        first = pl.program_id(0) == 0
        for b in range(nb):
            lo = BLOCK * b
            mask = _attn_mask(first) if b == 0 else _attn_mask(False)
            qv, kc, vc = q_ref[lo:lo + BLOCK, :], kc_ref[lo:lo + BLOCK, :], vc_ref[lo:lo + BLOCK, :]
            kp = kp_ref[...] if b == 0 else kc_ref[lo - BLOCK:lo, :]
            vp = vp_ref[...] if b == 0 else vc_ref[lo - BLOCK:lo, :]
            for h in range(N_KV_HEADS):
                pn, _ = _attn_probs(_q_heads(qv, h), _kv_head(kp, kc, h), mask, _sink_rows(s_ref, h))
                o4 = jnp.dot(pn.astype(BF16), _kv_head(vp, vc, h), preferred_element_type=F32).astype(BF16)
                for g in range(GROUP):
                    c0 = HEAD_DIM * (GROUP * h + g)
                    o_ref[lo:lo + BLOCK, c0:c0 + HEAD_DIM] = o4[BLOCK * g:BLOCK * (g + 1), :]

    cur = lambda n: pl.BlockSpec((rows, n), lambda i: (i, 0))
    prev = lambda n: pl.BlockSpec((BLOCK, n), lambda i: (jnp.maximum(i * nb - 1, 0), 0))
    return pl.pallas_call(
        body, name="attn_fwd", grid=(t // rows,),
        in_specs=[pl.BlockSpec(memory_space=pltpu.SMEM), cur(Q_DIM), cur(KV_DIM), prev(KV_DIM), cur(KV_DIM),
                  prev(KV_DIM)],
        out_specs=cur(Q_DIM),
        out_shape=jax.ShapeDtypeStruct((t, Q_DIM), BF16),
        compiler_params=_cp(("arbitrary",)),
    )(sinks, q, k, k, v, v)


def _out_post(lhs, ws, x_in, gpost):
    t = x_in.shape[0]
    tm = min(TM_ROW, t)
    n = len(lhs)

    def body(*refs):
        x_ref, g_ref, m_ref, xo_ref = refs[2 * n:]
        m = jnp.dot(refs[0][...], refs[n][...], preferred_element_type=F32)
        for j in range(1, n):
            m = m + jnp.dot(refs[j][...], refs[n + j][...], preferred_element_type=F32)
        m_ref[...] = m.astype(BF16)
        xo_ref[...] = x_ref[...] + m * _rsqrt_mean(m) * g_ref[...]

    return pl.pallas_call(
        body, name="out_post", grid=(t // tm,),
        in_specs=[_rows(tm, a.shape[1]) for a in lhs] + [_full(w.shape) for w in ws]
                 + [_rows(tm, D_MODEL), _full((1, D_MODEL))],
        out_specs=[_rows(tm, D_MODEL), _rows(tm, D_MODEL)],
        out_shape=[jax.ShapeDtypeStruct((t, D_MODEL), BF16), jax.ShapeDtypeStruct((t, D_MODEL), F32)],
        compiler_params=_cp(("arbitrary",)),
    )(*lhs, *ws, x_in, gpost)


def _conv3(w_ref, ext, tm):
    s = HALO3 - 2
    return (w_ref[0:1, :] * ext[s:s + tm, :] + w_ref[1:2, :] * ext[s + 1:s + 1 + tm, :]
            + w_ref[2:3, :] * ext[s + 2:s + 2 + tm, :])


def _ffn_fwd(x1, gpre, wup, layer, cw, wd, gpost):
    t = x1.shape[0]
    tm = min(TM, t)
    nc, n = wup.shape[1], wup.shape[4]

    def body(x_ref, gpre_ref, wup_ref, cw_ref, wd_ref, gpost_ref, h_ref, up_ref, u_ref, f_ref, xo_ref, h_s, acc, ext, hal):
        i, c = pl.program_id(0), pl.program_id(1)

        @pl.when(c == 0)
        def _():
            xv = x_ref[...]
            h = (xv * _rsqrt_mean(xv) * gpre_ref[...]).astype(BF16)
            h_s[...] = h
            h_ref[...] = h

        @pl.when(i == 0)
        def _():
            hal[c] = jnp.zeros((2, HALO3, n), F32)

        u = []
        for gv in range(2):
            up = jnp.dot(h_s[...], wup_ref[gv, 0, 0], preferred_element_type=F32)
            up_ref[gv, 0] = up.astype(BF16)
            ext[gv, 0:HALO3, :] = hal[c, gv]
            ext[gv, HALO3:HALO3 + tm, :] = up
            hal[c, gv] = ext[gv, tm:tm + HALO3, :]
            s = HALO3 - 2
            u.append(cw_ref[gv, 0, 0:1, :] * ext[gv, s:s + tm, :] + cw_ref[gv, 0, 1:2, :] * ext[gv, s + 1:s + 1 + tm, :]
                     + cw_ref[gv, 0, 2:3, :] * up)
            u_ref[gv, 0] = u[gv].astype(BF16)
        act = (u[0] * _sigmoid(u[0]) * u[1]).astype(BF16)
        part = jnp.dot(act, wd_ref[...], preferred_element_type=F32)

        @pl.when(c == 0)
        def _():
            acc[...] = part

        @pl.when(jnp.logical_and(c > 0, c < nc - 1))
        def _():
            acc[...] += part

        @pl.when(c == nc - 1)
        def _():
            f = acc[...] + part
            f_ref[...] = f
            xo_ref[...] = x_ref[...] + f * _rsqrt_mean(f) * gpost_ref[...]

    row = lambda w: pl.BlockSpec((tm, w), lambda i, c: (i, 0))
    one = _full((1, D_MODEL))
    return pl.pallas_call(
        body, name="ffn_fwd", grid=(t // tm, nc),
        in_specs=[row(D_MODEL), one, pl.BlockSpec((2, 1, 1, D_MODEL, n), lambda i, c: (0, c, layer, 0, 0)),
                  pl.BlockSpec((2, 1, 3, n), lambda i, c: (0, c, 0, 0)), pl.BlockSpec((n, D_MODEL), lambda i, c: (c, 0)),
                  one],
        out_specs=[row(D_MODEL), pl.BlockSpec((2, 1, tm, n), lambda i, c: (0, c, i, 0)),
                   pl.BlockSpec((2, 1, tm, n), lambda i, c: (0, c, i, 0)), row(D_MODEL), row(D_MODEL)],
        out_shape=[jax.ShapeDtypeStruct((t, D_MODEL), BF16), jax.ShapeDtypeStruct((2, nc, t, n), BF16),
                   jax.ShapeDtypeStruct((2, nc, t, n), BF16), jax.ShapeDtypeStruct((t, D_MODEL), F32),
                   jax.ShapeDtypeStruct((t, D_MODEL), F32)],
        scratch_shapes=[pltpu.VMEM((tm, D_MODEL), BF16), pltpu.VMEM((tm, D_MODEL), F32),
                        pltpu.VMEM((2, HALO3 + tm, n), F32), pltpu.VMEM((nc, 2, HALO3, n), F32)],
        compiler_params=_cp(("arbitrary", "arbitrary")),
    )(x1, gpre, wup, cw, wd, gpost)


def _od_fwd(x_in, gpre, w_in, cw, w_out, gpost):
    t = x_in.shape[0]
    tm = min(TM, t)
    ns, _, n = w_in.shape

    def body(x_ref, gpre_ref, w_ref, cw_ref, wo_ref, gpost_ref, h_ref, z_ref, cv_ref, y_ref, m_ref, xo_ref, z_s, ext):
        i = pl.program_id(0)
        xv = x_ref[...]
        h = (xv * _rsqrt_mean(xv) * gpre_ref[...]).astype(BF16)
        h_ref[...] = h
        for j in range(ns):
            z_s[:, n * j:n * (j + 1)] = jnp.dot(h, w_ref[j], preferred_element_type=F32)
        z_ref[...] = z_s[...].astype(BF16)

        @pl.when(i == 0)
        def _():
            ext[0:HALO3, :] = jnp.zeros((HALO3, SC_DIM), F32)

        ext[HALO3:HALO3 + tm, :] = z_s[:, SC_DIM:2 * SC_DIM] * z_s[:, 2 * SC_DIM:]
        cv = _conv3(cw_ref, ext, tm)
        cv_ref[...] = cv.astype(BF16)
        y = (z_s[:, :SC_DIM] * cv).astype(BF16)
        ext[0:HALO3, :] = ext[tm:tm + HALO3, :]
        y_ref[...] = y
        m = jnp.dot(y, wo_ref[...], preferred_element_type=F32)
        m_ref[...] = m
        xo_ref[...] = xv + m * _rsqrt_mean(m) * gpost_ref[...]

    return pl.pallas_call(
        body, name="od_fwd", grid=(t // tm,),
        in_specs=[_rows(tm, D_MODEL), _full((1, D_MODEL)), _full((ns, D_MODEL, n)), _full((3, SC_DIM)),
                  _full((SC_DIM, D_MODEL)), _full((1, D_MODEL))],
        out_specs=[_rows(tm, D_MODEL), _rows(tm, 3 * SC_DIM), _rows(tm, SC_DIM), _rows(tm, SC_DIM), _rows(tm, D_MODEL),
                   _rows(tm, D_MODEL)],
        out_shape=[jax.ShapeDtypeStruct((t, D_MODEL), BF16), jax.ShapeDtypeStruct((t, 3 * SC_DIM), BF16),
                   jax.ShapeDtypeStruct((t, SC_DIM), BF16), jax.ShapeDtypeStruct((t, SC_DIM), BF16),
                   jax.ShapeDtypeStruct((t, D_MODEL), F32), jax.ShapeDtypeStruct((t, D_MODEL), F32)],
        scratch_shapes=[pltpu.VMEM((tm, 3 * SC_DIM), F32), pltpu.VMEM((HALO3 + tm, SC_DIM), F32)],
        compiler_params=_cp(("arbitrary",)),
    )(x_in, gpre, w_in, cw, w_out, gpost)


def _dw2d(a, b, bm, bn):
    t, m = a.shape
    n = b.shape[1]
    tk = min(TK_DW, t)

    def body(a_ref, b_ref, o_ref):
        part = lax.dot_general(a_ref[...], b_ref[...], _CONTRACT_FIRST, preferred_element_type=F32)
        _acc_out(o_ref, pl.program_id(2) == 0, part)

    return pl.pallas_call(
        body, name="dw2d", grid=(m // bm, n // bn, t // tk),
        in_specs=[pl.BlockSpec((tk, bm), lambda i, j, k: (k, i)), pl.BlockSpec((tk, bn), lambda i, j, k: (k, j))],
        out_specs=pl.BlockSpec((bm, bn), lambda i, j, k: (i, j)),
        out_shape=jax.ShapeDtypeStruct((m, n), F32),
        compiler_params=_cp(("arbitrary", "arbitrary", "arbitrary")),
    )(a, b)


def _dw_cols(a, b, n_blk):
    t, m = a.shape
    s = b.shape[1] // n_blk
    tk = min(TK_DW, t)
    nk = t // tk

    def body(a_ref, b_ref, o_ref, ob_ref):
        part = lax.dot_general(a_ref[...], b_ref[...], _CONTRACT_FIRST, preferred_element_type=F32)
        for j in range(2):
            _acc_out(o_ref.at[j], pl.program_id(1) == 0, part[:, n_blk * j:n_blk * (j + 1)])

        @pl.when(pl.program_id(1) == nk - 1)
        def _():
            ob_ref[...] = o_ref[...].astype(BF16)

    spec = pl.BlockSpec((2, m, n_blk), lambda j, k: (j, 0, 0))
    return pl.pallas_call(
        body, name="dw_cols", grid=(s // 2, nk),
        in_specs=[pl.BlockSpec((tk, m), lambda j, k: (k, 0)), pl.BlockSpec((tk, 2 * n_blk), lambda j, k: (k, j))],
        out_specs=[spec, spec],
        out_shape=[jax.ShapeDtypeStruct((s, m, n_blk), F32), jax.ShapeDtypeStruct((s, m, n_blk), BF16)],
        compiler_params=_cp(("arbitrary", "arbitrary")),
    )(a, b)


def _dw_up(h, dup, layer, buf):
    t, m = h.shape
    s, _, n = dup.shape
    tk = min(TK_DW, t)
    nk = t // tk

    def body(*refs):
        a_ref, b_ref, o_ref, ob_ref = refs[0], refs[1], refs[-2], refs[-1]
        part = lax.dot_general(a_ref[...], b_ref[0], _CONTRACT_FIRST, preferred_element_type=F32)
        _acc_out(o_ref.at[0, 0], pl.program_id(1) == 0, part)

        @pl.when(pl.program_id(1) == nk - 1)
        def _():
            ob_ref[...] = o_ref[...].astype(BF16)

    spec = pl.BlockSpec((1, 1, m, n), lambda j, k: (j, layer, 0, 0))
    return pl.pallas_call(
        body, name="dw_up", grid=(s, nk),
        in_specs=[pl.BlockSpec((tk, m), lambda j, k: (k, 0)), pl.BlockSpec((1, tk, n), lambda j, k: (j, k, 0))]
                 + ([] if buf is None else [_ANY, _ANY]),
        out_specs=[spec, spec],
        out_shape=[jax.ShapeDtypeStruct((s, 2, m, n), F32), jax.ShapeDtypeStruct((s, 2, m, n), BF16)],
        input_output_aliases={} if buf is None else {2: 0, 3: 1},
        compiler_params=_cp(("arbitrary", "arbitrary")),
    )(h, dup, *([] if buf is None else buf))


def _dw_down(act, df, layer, buf):
    nc, t, n = act.shape
    d = df.shape[1]
    tk = min(TK_DW, t)
    nk = t // tk

    def body(*refs):
        a_ref, b_ref, o_ref, ob_ref = refs[0], refs[1], refs[-2], refs[-1]
        part = lax.dot_general(a_ref[0], b_ref[...], _CONTRACT_FIRST, preferred_element_type=F32)
        part = part.reshape(2, n // 2, d)
        first = pl.program_id(1) == 0

        @pl.when(first)
        def _():
            o_ref[:, 0] = part

        @pl.when(jnp.logical_not(first))
        def _():
            o_ref[:, 0] += part

        @pl.when(pl.program_id(1) == nk - 1)
        def _():
            ob_ref[...] = o_ref[...].astype(BF16)

    spec = pl.BlockSpec((2, 1, n // 2, d), lambda c, k: (c, layer, 0, 0))
    return pl.pallas_call(
        body, name="dw_down", grid=(nc, nk),
        in_specs=[pl.BlockSpec((1, tk, n), lambda c, k: (c, k, 0)), pl.BlockSpec((tk, d), lambda c, k: (k, 0))]
                 + ([] if buf is None else [_ANY, _ANY]),
        out_specs=[spec, spec],
        out_shape=[jax.ShapeDtypeStruct((2 * nc, 2, n // 2, d), F32), jax.ShapeDtypeStruct((2 * nc, 2, n // 2, d), BF16)],
        input_output_aliases={} if buf is None else {2: 0, 3: 1},
        compiler_params=_cp(("arbitrary", "arbitrary")),
    )(act, df, *([] if buf is None else buf))


def _dz_wt_rms_bwd(dz, w, x_in, gpre, dres):
    t, n = dz.shape
    tm = min(TM_ROW, t)

    def body(dz_ref, wt_ref, x_ref, g_ref, dres_ref, dx_ref, dg_ref):
        dh = lax.dot_general(dz_ref[...], wt_ref[...], _CONTRACT_LAST, preferred_element_type=F32)
        dx, dg = _rms_bwd(x_ref[...], g_ref[...], dh)
        dx_ref[...] = dres_ref[...] + dx
        _acc_out(dg_ref, pl.program_id(0) == 0, dg)

    return pl.pallas_call(
        body, name="dz_wt_rms_bwd", grid=(t // tm,),
        in_specs=[_rows(tm, n), _full((D_MODEL, n)), _rows(tm, D_MODEL), _full((1, D_MODEL)), _rows(tm, D_MODEL)],
        out_specs=[_rows(tm, D_MODEL), _full((1, D_MODEL))],
        out_shape=[jax.ShapeDtypeStruct((t, D_MODEL), F32), jax.ShapeDtypeStruct((1, D_MODEL), F32)],
        compiler_params=_cp(("arbitrary",)),
    )(dz, w, x_in, gpre, dres)


def _shift_matrices(shift, shift_h, tm, hb):
    row = lax.broadcasted_iota(jnp.int32, (2 * tm, tm), 0)
    col = lax.broadcasted_iota(jnp.int32, (2 * tm, tm), 1)
    hit = ((row < tm) & (col == row + 1)) | ((row >= tm) & (col == row - tm + 2))
    shift[...] = jnp.where(hit, 1.0, 0.0).astype(BF16)
    row = lax.broadcasted_iota(jnp.int32, (hb, hb), 0)
    col = lax.broadcasted_iota(jnp.int32, (hb, hb), 1)
    hit = ((row < HALO3) & (col == row - (HALO3 - 1))) | ((row >= HALO3) & (col == row - (2 * HALO3 - 2)))
    shift_h[...] = jnp.where(hit, 1.0, 0.0).astype(BF16)


def _next_rows(shift, shift_h, xb, nxt, d12_s, tm):
    d12_s[...] = jnp.dot(shift[...], xb, preferred_element_type=F32)
    edge = jnp.dot(shift_h[...], nxt, preferred_element_type=F32)
    d12_s[tm - HALO3:tm, :] += edge[0:HALO3, :]
    d12_s[2 * tm - HALO3:2 * tm, :] += edge[HALO3:2 * HALO3, :]


def _ffn_bwd(f, dxo, gpost, x_in, gpre, up, u, cw, wd, wup, layer, target=None):
    t = f.shape[0]
    tm = min(TM_BWD, t)
    nt = t // tm
    nc, n = up.shape[1], up.shape[3]
    hb = 2 * HALO3

    def body(*refs):
        f_ref, dxo_ref, gpost_ref, x_ref, gpre_ref, up_ref, u_ref, cw_ref, wd_ref, wup_ref = refs[:10]
        n_in = 10 if target is None else 11
        n_out = 7 if target is None else 8
        df_ref, act_ref, dup_ref, dx_ref, dgpost_ref, dgpre_ref, dcw_ref = refs[n_in:n_in + 7]
        df_s, acc, du_s, dub_s, d12_s, hal, shift, shift_h = refs[n_in + n_out:]
        i, c = pl.program_id(0), pl.program_id(1)

        def incoming():
            if target is None:
                return dxo_ref[...]
            return (dxo_ref[...] - refs[10][...]) * (1.0 / D_MODEL)

        @pl.when(c == 0)
        def _():
            dy = incoming()
            df, dg = _rms_bwd(f_ref[...], gpost_ref[...], dy)
            df_s[...] = df.astype(BF16)
            df_ref[...] = df.astype(BF16)
            _acc_out(dgpost_ref, i == 0, dg)
            if target is not None:
                part = jnp.zeros((1, 128), F32) + jnp.sum(dy * dy) * (0.5 * D_MODEL)
                _acc_out(refs[n_in + 7], i == 0, part)

        @pl.when(i == 0)
        def _():
            hal[c] = jnp.zeros((2, hb, n), BF16)
            dcw_ref[0, c] = jnp.zeros((8, n), F32)
            dcw_ref[1, c] = jnp.zeros((8, n), F32)

        @pl.when(jnp.logical_and(i == 0, c == 0))
        def _():
            _shift_matrices(shift, shift_h, tm, hb)

        dact = lax.dot_general(df_s[...], wd_ref[...], _CONTRACT_LAST, preferred_element_type=F32)
        g, v = u_ref[0, 0].astype(F32), u_ref[1, 0].astype(F32)
        sg = _sigmoid(g)
        sil = g * sg
        act_ref[0] = (sil * v).astype(BF16)
        dug = dact * v * (sg + sil * (1.0 - sg))
        duv = dact * sil
        du_s[0], du_s[1] = dug, duv
        dub_s[0], dub_s[1] = dug.astype(BF16), duv.astype(BF16)
        dh = None
        for gv in range(2):
            _next_rows(shift, shift_h, dub_s[gv], hal[c, gv], d12_s, tm)
            hal[c, gv] = dub_s[gv, 0:hb, :]
            du, d1, d2 = du_s[gv], d12_s[0:tm, :], d12_s[tm:2 * tm, :]
            dup = (cw_ref[gv, 0, 2:3, :] * du + cw_ref[gv, 0, 1:2, :] * d1 + cw_ref[gv, 0, 0:1, :] * d2).astype(BF16)
            dup_ref[gv, 0] = dup
            upc = up_ref[gv, 0].astype(F32)
            dcw_ref[gv, c, 2:3, :] += jnp.sum(upc * du, axis=0, keepdims=True)
            dcw_ref[gv, c, 1:2, :] += jnp.sum(upc * d1, axis=0, keepdims=True)
            dcw_ref[gv, c, 0:1, :] += jnp.sum(upc * d2, axis=0, keepdims=True)
            part = lax.dot_general(dup, wup_ref[gv, 0, 0], _CONTRACT_LAST, preferred_element_type=F32)
            dh = part if dh is None else dh + part
        _acc_out(acc, c == 0, dh)

        @pl.when(c == nc - 1)
        def _():
            dx, dg = _rms_bwd(x_ref[...], gpre_ref[...], acc[...])
            dx_ref[...] = incoming() + dx
            _acc_out(dgpre_ref, i == 0, dg)

    rrow = lambda w: pl.BlockSpec((tm, w), lambda i, c: (nt - 1 - i, 0))
    blk = pl.BlockSpec((2, 1, tm, n), lambda i, c: (0, c, nt - 1 - i, 0))
    one = _full((1, D_MODEL))
    return pl.pallas_call(
        body, name="ffn_bwd", grid=(nt, nc),
        in_specs=[rrow(D_MODEL), rrow(D_MODEL), one, rrow(D_MODEL), one, blk, blk,
                  pl.BlockSpec((2, 1, 3, n), lambda i, c: (0, c, 0, 0)),
                  pl.BlockSpec((n, D_MODEL), lambda i, c: (c, 0)),
                  pl.BlockSpec((2, 1, 1, D_MODEL, n), lambda i, c: (0, c, layer, 0, 0))]
                 + ([] if target is None else [rrow(D_MODEL)]),
        out_specs=[rrow(D_MODEL), pl.BlockSpec((1, tm, n), lambda i, c: (c, nt - 1 - i, 0)), blk, rrow(D_MODEL),
                   one, one, _full((2, nc, 8, n))] + ([] if target is None else [_full((1, 128))]),
        out_shape=[jax.ShapeDtypeStruct((t, D_MODEL), BF16), jax.ShapeDtypeStruct((nc, t, n), BF16),
                   jax.ShapeDtypeStruct((2, nc, t, n), BF16), jax.ShapeDtypeStruct((t, D_MODEL), F32),
                   jax.ShapeDtypeStruct((1, D_MODEL), F32), jax.ShapeDtypeStruct((1, D_MODEL), F32),
                   jax.ShapeDtypeStruct((2, nc, 8, n), F32)]
                  + ([] if target is None else [jax.ShapeDtypeStruct((1, 128), F32)]),
        scratch_shapes=[pltpu.VMEM((tm, D_MODEL), BF16), pltpu.VMEM((tm, D_MODEL), F32),
                        pltpu.VMEM((2, tm, n), F32), pltpu.VMEM((2, tm, n), BF16), pltpu.VMEM((2 * tm, n), F32),
                        pltpu.VMEM((nc, 2, hb, n), BF16), pltpu.VMEM((2 * tm, tm), BF16), pltpu.VMEM((hb, hb), BF16)],
        compiler_params=_cp(("arbitrary", "arbitrary")),
    )(f, dxo, gpost, x_in, gpre, up, u, cw, wd, wup, *([] if target is None else [target]))


def _od_bwd(m, dxo, gpost, x_in, gpre, z, cv, cw, w_out, wint):
    t = m.shape[0]
    tm = min(TM_BWD, t)
    nt = t // tm
    hb = 2 * HALO3

    def body(m_ref, dxo_ref, gpost_ref, x_ref, gpre_ref, z_ref, cv_ref, cw_ref, wo_ref, wi_ref,
             dm_ref, dz_ref, dx_ref, dgpost_ref, dgpre_ref, dcw_ref, dcvb_s, d12_s, dz_s, hal, shift, shift_h):
        i = pl.program_id(0)
        dxo = dxo_ref[...]
        dm, dg = _rms_bwd(m_ref[...], gpost_ref[...], dxo)
        dmb = dm.astype(BF16)
        dm_ref[...] = dmb
        _acc_out(dgpost_ref, i == 0, dg)

        @pl.when(i == 0)
        def _():
            hal[...] = jnp.zeros((hb, SC_DIM), BF16)
            dcw_ref[...] = jnp.zeros((8, SC_DIM), F32)
            _shift_matrices(shift, shift_h, tm, hb)

        dy = lax.dot_general(dmb, wo_ref[...], _CONTRACT_LAST, preferred_element_type=F32)
        z = z_ref[...].astype(F32)
        b, cg, u = z[:, :SC_DIM], z[:, SC_DIM:2 * SC_DIM], z[:, 2 * SC_DIM:]
        dz_s[:, 0:SC_DIM] = (dy * cv_ref[...].astype(F32)).astype(BF16)
        dcv = dy * b
        dcvb_s[...] = dcv.astype(BF16)
        _next_rows(shift, shift_h, dcvb_s[...], hal[...], d12_s, tm)
        hal[...] = dcvb_s[0:hb, :]
        d1, d2 = d12_s[0:tm, :], d12_s[tm:2 * tm, :]
        dcu = cw_ref[2:3, :] * dcv + cw_ref[1:2, :] * d1 + cw_ref[0:1, :] * d2
        cu = cg * u
        dcw_ref[2:3, :] += jnp.sum(cu * dcv, axis=0, keepdims=True)
        dcw_ref[1:2, :] += jnp.sum(cu * d1, axis=0, keepdims=True)
        dcw_ref[0:1, :] += jnp.sum(cu * d2, axis=0, keepdims=True)
        dz_s[:, SC_DIM:2 * SC_DIM] = (dcu * u).astype(BF16)
        dz_s[:, 2 * SC_DIM:3 * SC_DIM] = (dcu * cg).astype(BF16)
        dz_ref[...] = dz_s[...]
        dh = jnp.dot(dz_s[...], wi_ref[...], preferred_element_type=F32)
        dx, dg2 = _rms_bwd(x_ref[...], gpre_ref[...], dh)
        dx_ref[...] = dxo + dx
        _acc_out(dgpre_ref, i == 0, dg2)

    rrow = lambda w: pl.BlockSpec((tm, w), lambda i: (nt - 1 - i, 0))
    one = _full((1, D_MODEL))
    return pl.pallas_call(
        body, name="od_bwd", grid=(nt,),
        in_specs=[rrow(D_MODEL), rrow(D_MODEL), one, rrow(D_MODEL), one, rrow(3 * SC_DIM), rrow(SC_DIM),
                  _full((3, SC_DIM)), _full((SC_DIM, D_MODEL)), _full((3 * SC_DIM, D_MODEL))],
        out_specs=[rrow(D_MODEL), rrow(3 * SC_DIM), rrow(D_MODEL), one, one, _full((8, SC_DIM))],
        out_shape=[jax.ShapeDtypeStruct((t, D_MODEL), BF16), jax.ShapeDtypeStruct((t, 3 * SC_DIM), BF16),
                   jax.ShapeDtypeStruct((t, D_MODEL), F32), jax.ShapeDtypeStruct((1, D_MODEL), F32),
                   jax.ShapeDtypeStruct((1, D_MODEL), F32), jax.ShapeDtypeStruct((8, SC_DIM), F32)],
        scratch_shapes=[pltpu.VMEM((tm, SC_DIM), BF16), pltpu.VMEM((2 * tm, SC_DIM), F32),
                        pltpu.VMEM((tm, 3 * SC_DIM), BF16), pltpu.VMEM((hb, SC_DIM), BF16),
                        pltpu.VMEM((2 * tm, tm), BF16), pltpu.VMEM((hb, hb), BF16)],
        compiler_params=_cp(("arbitrary",)),
    )(m, dxo, gpost, x_in, gpre, z, cv, cw, w_out, wint)


def _ev_bwd1(m, dxo, gpost, w_out):
    t = m.shape[0]
    tm = min(TM_ROW, t)

    def body(m_ref, dxo_ref, g_ref, wot_ref, dm_ref, da_ref, do_ref, dg_ref):
        dm, dg = _rms_bwd(m_ref[...].astype(F32), g_ref[...], dxo_ref[...])
        dmb = dm.astype(BF16)
        dm_ref[...] = dmb
        _acc_out(dg_ref, pl.program_id(0) == 0, dg)
        dao = lax.dot_general(dmb, wot_ref[...], _CONTRACT_LAST, preferred_element_type=F32)
        da_ref[...] = dao[:, :A_CH]
        do_ref[...] = dao[:, A_CH:].astype(BF16)

    return pl.pallas_call(
        body, name="ev_bwd1", grid=(t // tm,),
        in_specs=[_rows(tm, D_MODEL), _rows(tm, D_MODEL), _full((1, D_MODEL)), _full((A_CH + Q_DIM, D_MODEL))],
        out_specs=[_rows(tm, D_MODEL), _rows(tm, A_CH), _rows(tm, Q_DIM), _full((1, D_MODEL))],
        out_shape=[jax.ShapeDtypeStruct((t, D_MODEL), BF16), jax.ShapeDtypeStruct((t, A_CH), F32),
                   jax.ShapeDtypeStruct((t, Q_DIM), BF16), jax.ShapeDtypeStruct((1, D_MODEL), F32)],
        compiler_params=_cp(("arbitrary",)),
    )(m, dxo, gpost, w_out)


def _conf_bwd(da, cv, zag, conv_w, ln_g, ln_b):
    t = da.shape[0]
    tm = min(TM, t)
    nt = t // tm
    rows = tm + HALO31 - 8

    def body(da_ref, c_ref, z_ref, w_ref, g_ref, lb_ref, dz_ref, dw_ref, dv_ref, ext_out, cbuf, glu_s, dglu_s):
        i = pl.program_id(0)

        @pl.when(i == 0)
        def _():
            ext_out[tm:tm + HALO31, :] = jnp.zeros((HALO31, A_CH), F32)
            dw_ref[...] = jnp.zeros((32, A_CH), F32)
            dv_ref[...] = jnp.zeros((8, A_CH), F32)

        x = c_ref[...]
        mu = jnp.mean(x, axis=-1, keepdims=True)
        xc = x - mu
        rstd = lax.rsqrt(jnp.mean(xc * xc, axis=-1, keepdims=True) + LN_EPS)
        xh = xc * rstd
        ln = xh * g_ref[...] + lb_ref[...]
        sl = _sigmoid(ln)
        dln = da_ref[...] * (sl * (1.0 + ln * (1.0 - sl)))
        dxh = dln * g_ref[...]
        dc = rstd * (dxh - jnp.mean(dxh, axis=-1, keepdims=True) - xh * jnp.mean(dxh * xh, axis=-1, keepdims=True))
        dv_ref[0:1, :] += jnp.sum(dc, axis=0, keepdims=True)
        dv_ref[1:2, :] += jnp.sum(dln * xh, axis=0, keepdims=True)
        dv_ref[2:3, :] += jnp.sum(dln, axis=0, keepdims=True)

        ext_out[0:tm, :] = dc
        _tap_copies(ext_out, cbuf, lambda b: b, rows)
        z = z_ref[...].astype(F32)
        al, sg = z[:, :A_CH], _sigmoid(z[:, A_CH:])
        glu_s[...] = al * sg
        for rs in range(0, tm, SUB):
            for cs in range(0, A_CH, 128):
                glu = glu_s[rs:rs + SUB, cs:cs + 128]
                acc = jnp.zeros((SUB, 128), F32)
                for k in range(A_CONV):
                    lag_a, lag_b = divmod(k, 8)
                    r0 = 8 * lag_a + rs
                    d = (ext_out[r0:r0 + SUB, cs:cs + 128] if lag_b == 0
                         else cbuf[lag_b - 1, r0:r0 + SUB, cs:cs + 128])
                    j = A_CONV - 1 - k
                    acc = acc + w_ref[j:j + 1, cs:cs + 128] * d
                    dw_ref[j:j + 1, cs:cs + 128] += jnp.sum(glu * d, axis=0, keepdims=True)
                dglu_s[rs:rs + SUB, cs:cs + 128] = acc
        dglu = dglu_s[...]
        ext_out[tm:tm + HALO31, :] = ext_out[0:HALO31, :]
        dz_ref[:, 0:A_CH] = (dglu * sg).astype(BF16)
        dz_ref[:, A_CH:2 * A_CH] = (dglu * al * sg * (1.0 - sg)).astype(BF16)

    rrow = lambda w: pl.BlockSpec((tm, w), lambda i: (nt - 1 - i, 0))
    return pl.pallas_call(
        body, name="conf_bwd", grid=(nt,),
        in_specs=[rrow(A_CH), rrow(A_CH), rrow(2 * A_CH), _full((32, A_CH)), _full((1, A_CH)), _full((1, A_CH))],
        out_specs=[rrow(2 * A_CH), _full((32, A_CH)), _full((8, A_CH))],
        out_shape=[jax.ShapeDtypeStruct((t, 2 * A_CH), BF16), jax.ShapeDtypeStruct((32, A_CH), F32),
                   jax.ShapeDtypeStruct((8, A_CH), F32)],
        scratch_shapes=[pltpu.VMEM((tm + HALO31, A_CH), F32), pltpu.VMEM((7, rows, A_CH), F32),
                        pltpu.VMEM((tm, A_CH), F32), pltpu.VMEM((tm, A_CH), F32)],
        compiler_params=_cp(("arbitrary",)),
    )(da, cv, zag, conv_w, ln_g, ln_b)


def _attn_bwd(q, k, v, do, sinks):
    t = q.shape[0]
    nb = min(ATT_NB, t // BLOCK)
    rows = nb * BLOCK
    ns = t // rows

    def body(s_ref, q_ref, kc_ref, kp_ref, vc_ref, vp_ref, do_ref, dq_ref, dk_ref, dv_ref, ds_ref, dkc, dvc):
        i = pl.program_id(0)
        r = ns - 1 - i

        @pl.when(i == 0)
        def _():
            dkc[...] = jnp.zeros_like(dkc)
            dvc[...] = jnp.zeros_like(dvc)
            ds_ref[...] = jnp.zeros_like(ds_ref)

        lane = lax.broadcasted_iota(jnp.int32, (1, N_Q_HEADS), 1)
        dsv = jnp.zeros((1, N_Q_HEADS), F32)
        for b in range(nb - 1, -1, -1):
            lo = BLOCK * b
            mask = _attn_mask(r == 0) if b == 0 else _attn_mask(False)
            qv, dov = q_ref[lo:lo + BLOCK, :], do_ref[lo:lo + BLOCK, :]
            kc, vc = kc_ref[lo:lo + BLOCK, :], vc_ref[lo:lo + BLOCK, :]
            kp = kp_ref[...] if b == 0 else kc_ref[lo - BLOCK:lo, :]
            vp = vp_ref[...] if b == 0 else vc_ref[lo - BLOCK:lo, :]
            for h in range(N_KV_HEADS):
                q4, do4 = _q_heads(qv, h), _q_heads(dov, h)
                k2, v2 = _kv_head(kp, kc, h), _kv_head(vp, vc, h)
                pn, ps = _attn_probs(q4, k2, mask, _sink_rows(s_ref, h))
                dp = lax.dot_general(do4, v2, _CONTRACT_LAST, preferred_element_type=F32)
                dl = jnp.sum(pn * dp, axis=-1, keepdims=True)
                dsb = (pn * (dp - dl)).astype(BF16)
                dq4 = (jnp.dot(dsb, k2, preferred_element_type=F32) * SCALE).astype(BF16)
                for g in range(GROUP):
                    c0 = HEAD_DIM * (GROUP * h + g)
                    dq_ref[lo:lo + BLOCK, c0:c0 + HEAD_DIM] = dq4[BLOCK * g:BLOCK * (g + 1), :]
                dk2 = lax.dot_general(dsb, q4, _CONTRACT_FIRST, preferred_element_type=F32) * SCALE
                dv2 = lax.dot_general(pn.astype(BF16), do4, _CONTRACT_FIRST, preferred_element_type=F32)
                dk_ref[lo:lo + BLOCK, HEAD_DIM * h:HEAD_DIM * (h + 1)] = dk2[BLOCK:, :] + dkc[h]
                dv_ref[lo:lo + BLOCK, HEAD_DIM * h:HEAD_DIM * (h + 1)] = dv2[BLOCK:, :] + dvc[h]
                dkc[h] = dk2[:BLOCK, :]
                dvc[h] = dv2[:BLOCK, :]
                srow = -ps * dl
                for g in range(GROUP):
                    dsv = dsv + jnp.where(lane == GROUP * h + g, jnp.sum(srow[BLOCK * g:BLOCK * (g + 1), :]), 0.0)
        ds_ref[...] += dsv

    cur = lambda n: pl.BlockSpec((rows, n), lambda i: (ns - 1 - i, 0))
    prev = lambda n: pl.BlockSpec((BLOCK, n), lambda i: (jnp.maximum((ns - 1 - i) * nb - 1, 0), 0))
    return pl.pallas_call(
        body, name="attn_bwd", grid=(ns,),
        in_specs=[pl.BlockSpec(memory_space=pltpu.SMEM), cur(Q_DIM), cur(KV_DIM), prev(KV_DIM), cur(KV_DIM),
                  prev(KV_DIM), cur(Q_DIM)],
        out_specs=[cur(Q_DIM), cur(KV_DIM), cur(KV_DIM), _full((1, N_Q_HEADS))],
        out_shape=[jax.ShapeDtypeStruct((t, Q_DIM), BF16), jax.ShapeDtypeStruct((t, KV_DIM), F32),
                   jax.ShapeDtypeStruct((t, KV_DIM), F32), jax.ShapeDtypeStruct((1, N_Q_HEADS), F32)],
        scratch_shapes=[pltpu.VMEM((N_KV_HEADS, BLOCK, HEAD_DIM), F32), pltpu.VMEM((N_KV_HEADS, BLOCK, HEAD_DIM), F32)],
        compiler_params=_cp(("arbitrary",)),
    )(sinks, q, k, k, v, v, do)


def _ev_dz(dzag, dq, dk, dv, rc, rsa, rsb):
    t = dzag.shape[0]
    tm = min(TM_ROW, t)

    def body(dzag_ref, dq_ref, dk_ref, dv_ref, c_ref, sa_ref, sb_ref, dz_ref):
        c, sa, sb = c_ref[...], sa_ref[...], sb_ref[...]
        dz_ref[:, 0:2 * A_CH] = dzag_ref[...]
        q0 = 2 * A_CH
        for j in range(Q_DIM // 128):
            d = dq_ref[:, 128 * j:128 * (j + 1)].astype(F32)
            dz_ref[:, q0 + 128 * j:q0 + 128 * (j + 1)] = _rope_bwd(d, c, sa, sb).astype(BF16)
        k0 = q0 + Q_DIM
        dz_ref[:, k0:k0 + KV_DIM] = _rope_bwd(dk_ref[...], c, sa, sb).astype(BF16)
        dz_ref[:, k0 + KV_DIM:k0 + 2 * KV_DIM] = dv_ref[...].astype(BF16)

    return pl.pallas_call(
        body, name="ev_dz", grid=(t // tm,),
        in_specs=[_rows(tm, 2 * A_CH), _rows(tm, Q_DIM), _rows(tm, KV_DIM), _rows(tm, KV_DIM),
                  _rows(tm, 128), _rows(tm, 128), _rows(tm, 128)],
        out_specs=_rows(tm, EVEN_IN),
        out_shape=jax.ShapeDtypeStruct((t, EVEN_IN), BF16),
        compiler_params=_cp(("arbitrary",)),
    )(dzag, dq, dk, dv, rc, rsa, rsb)


def _prep_ev(gat):
    p = {}
    p["ev_w_in"] = gat["ev_w_in"][:, 0].transpose(1, 0, 2).reshape(D_MODEL, EVEN_IN)
    p["ev_w_out"] = gat["ev_w_out"].reshape(A_CH + Q_DIM, D_MODEL)
    return p


def _prep_rest(gat):
    p = {}
    g = gat["od_w_in"][:, 0]
    p["od_w_in"] = g.transpose(1, 0, 2).reshape(1, D_MODEL, 3 * SC_DIM)
    p["od_w_in_t"] = g.transpose(0, 2, 1).reshape(3 * SC_DIM, D_MODEL)
    p["od_w_out"] = gat["od_w_out"].reshape(SC_DIM, D_MODEL)
    p["ffn_w_up"] = gat["ffn_w_up"].reshape(2, N_DEV // 2, 2, D_MODEL, FF_N)
    p["ffn_w_down"] = [gat["ffn_w_down"][:, i].reshape(D_FF, D_MODEL) for i in range(2)]
    return p


def _local_step(x, positions, target, p, rest_weights, s, token, grads_ready):
    row = lambda a, tok=None: a.reshape(1, -1) if tok is None else a.reshape(1, -1) + tok
    nc = N_DEV // 2
    rc, rsa, rsb = _rope_tables(positions)
    conv31 = jnp.pad(s["ev_a_conv_w"][0], ((0, 1), (0, 0)))
    cw_ffn = [s["ffn_conv_w"][i].reshape(3, 2, nc, FF_N).transpose(1, 2, 0, 3) for i in range(2)]
    sinks = s["ev_sinks"][0]
    big, g = {}, {}

    h0, zag, q, k, v = _ev_in(x, row(s["mix_norm_pre"][0], token), p["ev_w_in"], rc, rsa, rsb)
    cv, a = _conf_fwd(zag, conv31, s["ev_a_conv_b"], s["ev_a_ln_g"], s["ev_a_ln_b"])
    o = _attn_fwd(q, k, v, sinks)
    wo = p["ev_w_out"]
    m0, x1 = _out_post([a, o], [wo[:A_CH], wo[A_CH:]], x, row(s["mix_norm_post"][0]))
    p = {**p, **rest_weights(m0)}
    h1, up0, u0, f0, x2 = _ffn_fwd(x1, row(s["ffn_norm_pre"][0]), p["ffn_w_up"], 0, cw_ffn[0], p["ffn_w_down"][0],
                                   row(s["ffn_norm_post"][0]))
    h2, z, cv1, y, m1, x3 = _od_fwd(x2, row(s["mix_norm_pre"][1]), p["od_w_in"], s["od_conv_w"][0], p["od_w_out"],
                                    row(s["mix_norm_post"][1]))
    h3, up1, u1, f1, x4 = _ffn_fwd(x3, row(s["ffn_norm_pre"][1]), p["ffn_w_up"], 1, cw_ffn[1], p["ffn_w_down"][1],
                                   row(s["ffn_norm_post"][1]))

    def ffn_back(i, f, dxo, up, u, h, x_in, bufs, tok=None, tgt=None):
        df, act, dup, dx_in, dgpost, dgpre, dcw, *loss = _ffn_bwd(
            f, dxo, row(s["ffn_norm_post"][i], tok), x_in, row(s["ffn_norm_pre"][i]), up, u, cw_ffn[i],
            p["ffn_w_down"][i], p["ffn_w_up"], i, tgt)
        bufs = (_dw_up(h, dup.reshape(N_DEV, -1, FF_N), i, bufs[0]), _dw_down(act, df, i, bufs[1]))
        return dx_in, dgpost, dgpre, dcw[:, :, 0:3].transpose(2, 0, 1, 3).reshape(3, 2 * D_FF), bufs, loss

    dx, dgfpost1, dgfpre1, dcw1, bufs, (lpart,) = ffn_back(1, f1, x4, up1, u1, h3, x3, (None, None), None, target)

    dm1, dz, dx, dgpost1, dgpre1, dcw_od = _od_bwd(m1, dx, row(s["mix_norm_post"][1]), x2, row(s["mix_norm_pre"][1]), z,
                                                   cv1, s["od_conv_w"][0], p["od_w_out"], p["od_w_in_t"])
    big["od_w_out"] = _dw2d(y, dm1, SC_DIM, D_MODEL).reshape(N_DEV, -1, D_MODEL)
    big["od_w_in"], big["od_w_in:bf16"] = _dw_cols(h2, dz, 3 * SC_DIM // N_DEV)
    g["od_conv_w"] = dcw_od[None, 0:3]
    tok = grads_ready(["od_w_in", "od_w_out"], big)

    dx, dgfpost0, dgfpre0, dcw0, bufs, _ = ffn_back(0, f0, dx, up0, u0, h1, x1, bufs, tok)
    (big["ffn_w_up"], big["ffn_w_up:bf16"]), (big["ffn_w_down"], big["ffn_w_down:bf16"]) = bufs
    tok = grads_ready(["ffn_w_up", "ffn_w_down"], big)

    dm0, da, do, dgpost0 = _ev_bwd1(m0, dx, row(s["mix_norm_post"][0], tok), p["ev_w_out"])
    big["ev_w_out"] = jnp.concatenate([_dw2d(a, dm0, A_CH, D_MODEL), _dw2d(o, dm0, Q_DIM, D_MODEL)],
                                      axis=0).reshape(N_DEV, -1, D_MODEL)
    tok = grads_ready(["ev_w_out"], big)
    dzag, dcw31, dvec = _conf_bwd(da, cv, zag, conv31, s["ev_a_ln_g"] + tok, s["ev_a_ln_b"])
    dq, dk, dv, dsinks = _attn_bwd(q, k, v, do, sinks)
    dz0 = _ev_dz(dzag, dq, dk, dv, rc, rsa, rsb)
    dw_in = _dw2d(h0, dz0, D_MODEL // 2, EVEN_IN)
    big["ev_w_in"] = dw_in.reshape(D_MODEL, N_DEV, EVEN_IN // N_DEV).transpose(1, 0, 2)
    tok = grads_ready(["ev_w_in"], big)
    dx, dgpre0 = _dz_wt_rms_bwd(dz0, p["ev_w_in"], x, row(s["mix_norm_pre"][0], tok), dx)

    g["mix_norm_pre"] = jnp.concatenate([dgpre0, dgpre1], axis=0)
    g["mix_norm_post"] = jnp.concatenate([dgpost0, dgpost1], axis=0)
    g["ffn_norm_pre"] = jnp.concatenate([dgfpre0, dgfpre1], axis=0)
    g["ffn_norm_post"] = jnp.concatenate([dgfpost0, dgfpost1], axis=0)
    g["ev_a_conv_w"] = dcw31[None, 0:A_CONV]
    g["ev_a_conv_b"], g["ev_a_ln_g"], g["ev_a_ln_b"] = dvec[0:1], dvec[1:2], dvec[2:3]
    g["ev_sinks"] = dsinks
    g["ffn_conv_w"] = jnp.stack([dcw0, dcw1])
    return lpart[0, 0], dx, big, g


MESH = pl.DeviceIdType.MESH


def _all_gather(shards, name):
    nw = len(shards)

    def body(*refs):
        x_refs, out_refs = refs[:nw], refs[nw:2 * nw]
        send_sems, recv_sems, local_sems = refs[2 * nw:]
        x, y, c = lax.axis_index("x"), lax.axis_index("y"), lax.axis_index("c")
        me, sibling = (x, y, c), (x, y, 1 - c)
        chips = [(1 - x, y), (x, 1 - y), (1 - x, 1 - y)]

        def rows(w, px, py, pc):
            m_per = shards[w].shape[0]
            return out_refs[w].at[pl.ds((4 * px + 2 * py + pc) * m_per, m_per), :]

        def copy(w, k, block, to, src=None):
            return pltpu.make_async_remote_copy(
                src_ref=rows(w, *block) if src is None else src, dst_ref=rows(w, *block),
                send_sem=send_sems.at[w, k], recv_sem=recv_sems.at[w, k], device_id=to, device_id_type=MESH)

        mine, first, passed = [], [], []
        for w in range(nw):
            cp = pltpu.make_async_copy(x_refs[w], rows(w, *me), local_sems.at[w])
            cp.start()
            mine.append(cp)
            first.append([copy(w, 0, me, sibling, src=x_refs[w])]
                         + [copy(w, 1 + j, me, (*chip, c), src=x_refs[w]) for j, chip in enumerate(chips)])
            for cp in first[w]:
                cp.start()
        for w in range(nw):
            passed.append([copy(w, 4 + j, (*chip, c), sibling) for j, chip in enumerate(chips)])
            for j, chip in enumerate(chips):
                copy(w, 1 + j, (*chip, c), me).wait_recv()
                passed[w][j].start()
        for w in range(nw):
            copy(w, 0, sibling, me).wait_recv()
            for j, chip in enumerate(chips):
                copy(w, 4 + j, (*chip, 1 - c), me).wait_recv()
            for cp in first[w] + passed[w]:
                cp.wait_send()
            mine[w].wait()

    return pl.pallas_call(
        body, name=name,
        out_shape=[jax.ShapeDtypeStruct((N_DEV * a.shape[0], a.shape[1]), a.dtype) for a in shards],
        in_specs=[_ANY] * nw, out_specs=[_ANY] * nw,
        scratch_shapes=[pltpu.SemaphoreType.DMA((nw, 7)), pltpu.SemaphoreType.DMA((nw, 7)),
                        pltpu.SemaphoreType.DMA((nw,))],
    )(*shards)


_HBM = pl.BlockSpec(memory_space=pltpu.HBM)
_SEM = pl.BlockSpec(memory_space=pltpu.SEMAPHORE)
_EFFECT = pltpu.SideEffectType.DATAFLOW_SIDE_EFFECTING
_RELATIONS = [(dx, dy, dc) for dx in (0, 1) for dy in (0, 1) for dc in (0, 1)][1:]


def _peer(rel):
    x, y, c = lax.axis_index("x"), lax.axis_index("y"), lax.axis_index("c")
    px, py, pc = x ^ rel[0], y ^ rel[1], c ^ rel[2]
    return (px, py, pc), 4 * px + 2 * py + pc, 4 * x + 2 * y + c


def _exchange_copy(k, rel, src_ref, land_ref, send_sems, recv_sems, w, scatter):
    peer, peer_idx, my_idx = _peer(rel)
    src = src_ref.at[peer_idx] if scatter else src_ref
    return pltpu.make_async_remote_copy(
        src_ref=src, dst_ref=land_ref.at[my_idx], send_sem=send_sems.at[_sends(scatter) * w + k],
        recv_sem=recv_sems.at[7 * w + k], device_id=peer, device_id_type=MESH)


def _sends(scatter):
    return 7 if scatter else 8


def _own_copy(src_ref, land_ref, send_sems, w):
    my_idx = _peer(_RELATIONS[0])[2]
    return pltpu.make_async_copy(src_ref, land_ref.at[my_idx], send_sems.at[8 * w + 7])


def _exchange_start(srcs, scatter, name):
    nw = len(srcs)
    lands = [lax.empty((N_DEV,) + (a.shape[1:] if scatter else a.shape), a.dtype) for a in srcs]

    def body(*refs):
        src_refs, land_refs = refs[:nw], refs[nw:2 * nw]
        send_sems, recv_sems = refs[2 * nw], refs[2 * nw + 1]
        token = refs[-1]
        for w in range(nw):
            for k, rel in enumerate(_RELATIONS):
                _exchange_copy(k, rel, src_refs[w], land_refs[w], send_sems, recv_sems, w, scatter).start()
            if not scatter:
                _own_copy(src_refs[w], land_refs[w], send_sems, w).start()
        token[...] = jnp.zeros_like(token)

    hbm = lambda a: pltpu.HBM(a.shape, a.dtype)
    outs = pl.pallas_call(
        body, name=name,
        out_shape=(pltpu.SemaphoreType.DMA((_sends(scatter) * nw,)), pltpu.SemaphoreType.DMA((7 * nw,)),
                   *[hbm(a) for a in srcs],
                   *[hbm(a) for a in lands], jax.ShapeDtypeStruct((8, 128), F32)),
        in_specs=[_HBM] * (2 * nw),
        out_specs=(_SEM, _SEM, *[_HBM] * (2 * nw), pl.BlockSpec(memory_space=pltpu.VMEM)),
        input_output_aliases={i: 2 + i for i in range(2 * nw)},
        compiler_params=pltpu.CompilerParams(has_side_effects=_EFFECT),
    )(*[pltpu.with_memory_space_constraint(a, pltpu.HBM) for a in srcs],
      *[pltpu.with_memory_space_constraint(a, pltpu.HBM) for a in lands])
    return outs[0], outs[1], list(outs[2:2 + nw]), list(outs[2 + nw:2 + 2 * nw]), outs[-1]


def _exchange_wait(started, scatter, after, name):
    send_sems, recv_sems, srcs, lands, _ = started
    nw = len(srcs)

    def body(*refs):
        src_refs, land_refs = refs[:nw], refs[nw:2 * nw]
        send_s, recv_s = refs[2 * nw], refs[2 * nw + 1]
        for w in range(nw):
            for k, rel in enumerate(_RELATIONS):
                cp = _exchange_copy(k, rel, src_refs[w], land_refs[w], send_s, recv_s, w, scatter)
                cp.wait_send()
                _, peer_idx, _ = _peer(rel)
                pltpu.make_async_remote_copy(
                    src_ref=src_refs[w].at[peer_idx] if scatter else src_refs[w], dst_ref=land_refs[w].at[peer_idx],
                    send_sem=send_s.at[_sends(scatter) * w + k], recv_sem=recv_s.at[7 * w + k],
                    device_id=_peer(rel)[0], device_id_type=MESH).wait_recv()
            if not scatter:
                _own_copy(src_refs[w], land_refs[w], send_s, w).wait()

    hbm = lambda a: pltpu.HBM(a.shape, a.dtype)
    outs = pl.pallas_call(
        body, name=name, out_shape=tuple(hbm(a) for a in srcs + lands),
        in_specs=[_HBM] * (2 * nw) + [_SEM, _SEM, _ANY], out_specs=tuple([_HBM] * (2 * nw)),
        input_output_aliases={i: i for i in range(2 * nw)},
        compiler_params=pltpu.CompilerParams(has_side_effects=_EFFECT),
    )(*srcs, *lands, send_sems, recv_sems, after)
    return list(outs[nw:])


def _to_bf16(a):
    _, r, l = a.shape
    tr = _row_tile(r, 512)

    def body(a_ref, o_ref):
        o_ref[...] = a_ref[...].astype(BF16)

    spec = pl.BlockSpec((1, tr, l), lambda j, i: (j, i, 0))
    return pl.pallas_call(
        body, name="to_bf16", grid=(N_DEV, r // tr), in_specs=[spec], out_specs=spec,
        out_shape=jax.ShapeDtypeStruct(a.shape, BF16), compiler_params=_cp(("arbitrary", "arbitrary")),
    )(a)


def _row_tile(rows, cap):
    best = None
    for d in range(16, min(rows, cap) + 1, 16):
        if rows % d == 0:
            best = d
    return rows if best is None else best


def _adam_math(w, g, m, v):
    bc1 = 1.0 - ADAM_B1 ** ADAM_STEP
    bc2 = 1.0 - ADAM_B2 ** ADAM_STEP
    mn = ADAM_B1 * m + (1.0 - ADAM_B1) * g
    vn = ADAM_B2 * v + (1.0 - ADAM_B2) * (g * g)
    return -ADAM_LR * ((mn / bc1) / (jnp.sqrt(vn / bc2) + ADAM_EPS) + ADAM_WD * w), mn, vn


def _adamw_rs(gp, land, w, m, v, dev):
    _, r, l = gp.shape
    tr = _row_tile(r, 256)

    def body(i_ref, g_ref, b_ref, w_ref, m_ref, v_ref, go_ref, d_ref, mo_ref, vo_ref):
        g = g_ref[0]
        for j in range(N_DEV):
            g = g + jnp.where(i_ref[0] == j, 0.0, b_ref[j].astype(F32))
        go_ref[...] = g
        d_ref[...], mo_ref[...], vo_ref[...] = _adam_math(w_ref[...], g, m_ref[...], v_ref[...])

    spec = pl.BlockSpec((tr, l), lambda i, s: (i, 0))
    return pl.pallas_call(
        body, name="adamw_rs", out_shape=[jax.ShapeDtypeStruct((r, l), F32)] * 4,
        grid_spec=pltpu.PrefetchScalarGridSpec(
            num_scalar_prefetch=1, grid=(r // tr,),
            in_specs=[pl.BlockSpec((1, tr, l), lambda i, s: (s[0], i, 0)),
                      pl.BlockSpec((N_DEV, tr, l), lambda i, s: (0, i, 0)), spec, spec, spec],
            out_specs=[spec] * 4),
        compiler_params=_cp(("arbitrary",)),
    )(dev, gp, land, w, m, v)


def _sum_blocks(a, nblk):
    m = a.shape[0] // nblk
    n = a.shape[1]

    def body(a_ref, o_ref):
        acc = a_ref[0]
        for j in range(1, nblk):
            acc = acc + a_ref[j]
        o_ref[...] = acc

    return pl.pallas_call(
        body, name="sum_blocks", out_shape=jax.ShapeDtypeStruct((m, n), a.dtype),
        in_specs=[_full((nblk, m, n))], out_specs=_full((m, n)),
    )(a.reshape(nblk, m, n))


def _adamw(w, g, m, v):
    rows, c = w.shape

    def body(w_ref, g_ref, m_ref, v_ref, d_ref, mo_ref, vo_ref):
        d_ref[...], mo_ref[...], vo_ref[...] = _adam_math(w_ref[...], g_ref[...], m_ref[...], v_ref[...])

    return pl.pallas_call(
        body, name="adamw", in_specs=[_full((rows, c))] * 4, out_specs=[_full((rows, c))] * 3,
        out_shape=[jax.ShapeDtypeStruct((rows, c), F32)] * 3,
    )(w, g, m, v)


WEIGHTS = ["mix_norm_pre", "mix_norm_post", "ffn_norm_pre", "ffn_norm_post", "ev_w_in", "ev_a_conv_w", "ev_a_conv_b",
           "ev_a_ln_g", "ev_a_ln_b", "ev_sinks", "ev_w_out", "od_w_in", "od_conv_w", "od_w_out", "ffn_w_up",
           "ffn_conv_w", "ffn_w_down"]
BIG = ["ev_w_in", "ev_w_out", "od_w_in", "od_w_out", "ffn_w_up", "ffn_w_down"]
SMALL_REPL = ["mix_norm_pre", "mix_norm_post", "ffn_norm_pre", "ffn_norm_post", "ev_a_conv_b", "ev_a_ln_g",
              "ev_a_ln_b", "ev_sinks"]
SMALL_SHARDED = ["ev_a_conv_w", "od_conv_w", "ffn_conv_w"]


def _pack(arrs, rows):
    flat = jnp.concatenate([a.reshape(-1) for a in arrs])
    return jnp.pad(flat, (0, rows * LANES - flat.shape[0])).reshape(rows, LANES)


def _unpack(packed, shapes):
    flat, out, off = packed.reshape(-1), [], 0
    for s in shapes:
        n = 1
        for d in s:
            n *= d
        out.append(flat[off:off + n].reshape(s))
        off += n
    return out


def kernel(x, positions, mix_norm_pre, mix_norm_post, ffn_norm_pre, ffn_norm_post, ev_w_in, ev_a_conv_w, ev_a_conv_b, ev_a_ln_g, ev_a_ln_b, ev_sinks, ev_w_out, od_w_in, od_conv_w, od_w_out, ffn_w_up, ffn_conv_w, ffn_w_down, loss_target, m_mix_norm_pre, m_mix_norm_post, m_ffn_norm_pre, m_ffn_norm_post, m_ev_w_in, m_ev_a_conv_w, m_ev_a_conv_b, m_ev_a_ln_g, m_ev_a_ln_b, m_ev_sinks, m_ev_w_out, m_od_w_in, m_od_conv_w, m_od_w_out, m_ffn_w_up, m_ffn_conv_w, m_ffn_w_down, v_mix_norm_pre, v_mix_norm_post, v_ffn_norm_pre, v_ffn_norm_post, v_ev_w_in, v_ev_a_conv_w, v_ev_a_conv_b, v_ev_a_ln_g, v_ev_a_ln_b, v_ev_sinks, v_ev_w_out, v_od_w_in, v_od_conv_w, v_od_w_out, v_ffn_w_up, v_ffn_conv_w, v_ffn_w_down):
    w = dict(zip(WEIGHTS, (mix_norm_pre, mix_norm_post, ffn_norm_pre, ffn_norm_post, ev_w_in, ev_a_conv_w, ev_a_conv_b,
                           ev_a_ln_g, ev_a_ln_b, ev_sinks, ev_w_out, od_w_in, od_conv_w, od_w_out, ffn_w_up, ffn_conv_w,
                           ffn_w_down)))
    mom = dict(zip(WEIGHTS, (m_mix_norm_pre, m_mix_norm_post, m_ffn_norm_pre, m_ffn_norm_post, m_ev_w_in, m_ev_a_conv_w,
                             m_ev_a_conv_b, m_ev_a_ln_g, m_ev_a_ln_b, m_ev_sinks, m_ev_w_out, m_od_w_in, m_od_conv_w,
                             m_od_w_out, m_ffn_w_up, m_ffn_conv_w, m_ffn_w_down)))
    var = dict(zip(WEIGHTS, (v_mix_norm_pre, v_mix_norm_post, v_ffn_norm_pre, v_ffn_norm_post, v_ev_w_in, v_ev_a_conv_w,
                             v_ev_a_conv_b, v_ev_a_ln_g, v_ev_a_ln_b, v_ev_sinks, v_ev_w_out, v_od_w_in, v_od_conv_w,
                             v_od_w_out, v_ffn_w_up, v_ffn_conv_w, v_ffn_w_down)))
    ix, iy, ic = lax.axis_index("x"), lax.axis_index("y"), lax.axis_index("c")
    dev = 4 * ix + 2 * iy + ic
    two = lambda a: a.reshape(-1, a.shape[-1])

    dev1 = jnp.reshape(dev, (1,)).astype(jnp.int32)
    shard = {n: two(w[n].astype(BF16)) for n in BIG}
    gathered = lambda n, a: a.reshape((N_DEV,) + w[n].shape)
    ev_names = [n for n in BIG if n.startswith("ev_")]
    ev_gat = _all_gather([shard[n] for n in ev_names] + [_pack([w[n] for n in SMALL_SHARDED], 8)], "gather_ev")
    p = _prep_ev({n: gathered(n, a) for n, a in zip(ev_names, ev_gat)})
    rest_names = [n for n in BIG if not n.startswith("ev_")]
    first = shard[rest_names[0]] + (ev_gat[0][0:1, 0:1] * 0).astype(BF16)
    started = _exchange_start([first] + [shard[n] for n in rest_names[1:]], False, "gather_start")

    def rest_weights(after):
        lands = _exchange_wait(started, False, after, "gather_wait")
        return _prep_rest({n: gathered(n, a) for n, a in zip(rest_names, lands)})

    small = {n: w[n] for n in SMALL_REPL}
    small_shapes = [w[n].shape for n in SMALL_SHARDED]
    conv_gat = ev_gat[len(ev_names)].reshape(N_DEV, 8, LANES)
    per_dev = [_unpack(conv_gat[d], small_shapes) for d in range(N_DEV)]
    for k, n in enumerate(SMALL_SHARDED):
        small[n] = jnp.concatenate([per_dev[d][k] for d in range(N_DEV)], axis=-1)

    exchanges = []

    def grads_ready(names, big):
        blocks = lambda a, n: a.reshape(N_DEV, -1, w[n].shape[-1])
        bufs = [blocks(big[n], n) for n in names]
        payload = [blocks(big[n + ":bf16"], n) if n + ":bf16" in big else _to_bf16(b) for n, b in zip(names, bufs)]
        st = _exchange_start(payload, True, "grads_start_" + names[0])
        exchanges.append((names, bufs, st))
        return st[-1][0, 0]

    lpart, grad_x, big, g = _local_step(x[0], positions[0], loss_target[0], p, rest_weights, small, started[-1][0, 0],
                                        grads_ready)
    loss = lax.psum(lpart, ("x", "y", "c"))

    grads, delta, new_m, new_v = {}, {}, {}, {}
    for names, bufs, st in exchanges:
        lands = _exchange_wait(st, True, grad_x, "grads_wait_" + names[0])
        for n, b, land in zip(names, bufs, lands):
            outs = _adamw_rs(b, land, two(w[n]), two(mom[n]), two(var[n]), dev1)
            grads[n], delta[n], new_m[n], new_v[n] = (a.reshape(w[n].shape) for a in outs)

    small_names = SMALL_REPL + SMALL_SHARDED
    s_all = _sum_blocks(_all_gather([_pack([g[n] for n in small_names], 64)], "gather_small_grads")[0], N_DEV)
    for n, a in zip(small_names, _unpack(s_all, [small[n].shape for n in small_names])):
        if n in SMALL_SHARDED:
            width = w[n].shape[-1]
            a = lax.dynamic_slice_in_dim(a, dev * width, width, axis=a.ndim - 1)
        grads[n] = a
    pk = lambda dct: _pack([dct[n] for n in small_names], 16)
    outs = _adamw(pk(w), pk(grads), pk(mom), pk(var))
    for dst, packed in zip((delta, new_m, new_v), outs):
        for n, a in zip(small_names, _unpack(packed, [w[n].shape for n in small_names])):
            dst[n] = a

    return (loss, grad_x[None], *[grads[n] for n in WEIGHTS], *[delta[n] for n in WEIGHTS],
            *[new_m[n] for n in WEIGHTS], *[new_v[n] for n in WEIGHTS])
```
